```python
import jax, jax.numpy as jnp
from jax import lax
import numpy as np

D_MODEL = 1024
BATCH = 8
SEQ = 2048
DEPTH = 4

ATTN_HEADS = 8
ATTN_KV_HEADS = 2
HEAD_DIM = 64
WINDOW = 128
BLOCK = 128
DN_HEADS = 8
DN_DK = 64
DN_DV = 64
DN_CONV = 4
CHUNK = 64
CONV_DIM = D_MODEL
CONV_WIDTH = 31
D_FF = 2816
EPS = 1e-6

Q_A = ATTN_HEADS * HEAD_DIM
KV_A = ATTN_KV_HEADS * HEAD_DIM
QK_B = DN_HEADS * DN_DK
V_B = DN_HEADS * DN_DV
QKV_B = 2 * QK_B + V_B
IN_SPLIT_SIZES = (Q_A, KV_A, KV_A, QKV_B, V_B, DN_HEADS, DN_HEADS)
IN_COLS = sum(IN_SPLIT_SIZES)
MIX_WIDTH = Q_A + V_B
N_EVEN = (DEPTH + 1) // 2
N_ODD = DEPTH // 2

kernel_name = "hybrid_swa_deltanet_conformer_macaron"


def rmsnorm(x, w):
    xf = x.astype(jnp.float32)
    y = xf * lax.rsqrt(jnp.mean(xf * xf, axis=-1, keepdims=True) + EPS)
    return (y * w.astype(jnp.float32)).astype(x.dtype)


def layernorm(x, w, b):
    xf = x.astype(jnp.float32)
    mu = jnp.mean(xf, axis=-1, keepdims=True)
    xc = xf - mu
    y = xc * lax.rsqrt(jnp.mean(xc * xc, axis=-1, keepdims=True) + EPS)
    return (y * w.astype(jnp.float32) + b.astype(jnp.float32)).astype(x.dtype)


def l2norm(x):
    xf = x.astype(jnp.float32)
    return xf * lax.rsqrt(jnp.sum(xf * xf, axis=-1, keepdims=True) + EPS)


def causal_depthwise_conv(x, w):
    k_width = w.shape[0]
    return lax.conv_general_dilated(
        x, w[:, None, :].astype(x.dtype), window_strides=(1,), padding=[(k_width - 1, 0)],
        dimension_numbers=('NWC', 'WIO', 'NWC'), feature_group_count=x.shape[-1])


def swiglu(x, w_gate, w_up, w_down):
    return (jax.nn.silu(x @ w_gate) * (x @ w_up)) @ w_down


def alibi_slopes(n_heads):
    return jnp.asarray(2.0 ** (-8.0 * np.arange(1, n_heads + 1) / n_heads), dtype=jnp.float32)


def sliding_window_attention(q, k, v, sinks):
    B, T, Hq, d = q.shape
    Hkv = k.shape[2]
    G = Hq // Hkv
    N = T // BLOCK
    qb = q.reshape(B, N, BLOCK, Hkv, G, d)

    def with_prev(t):
        tb = t.reshape(B, N, BLOCK, Hkv, d)
        prev = jnp.pad(tb, ((0, 0), (1, 0), (0, 0), (0, 0), (0, 0)))[:, :-1]
        return jnp.concatenate([prev, tb], axis=2)

    kb, vb = with_prev(k), with_prev(v)
    s = jnp.einsum('bnikgd,bnjkd->bkgnij', qb, kb).astype(jnp.float32) * (d ** -0.5)
    i = jnp.arange(BLOCK)[:, None]
    j = jnp.arange(2 * BLOCK)[None, :]
    dist = i + BLOCK - j
    blk = jnp.arange(N)[:, None, None]
    valid = (dist >= 0) & (dist < WINDOW) & ((blk > 0) | (j >= BLOCK))
    slopes = alibi_slopes(Hq).reshape(Hkv, G)[:, :, None, None, None]
    s = s - slopes * dist.astype(jnp.float32)
    s = jnp.where(valid, s, -1e30)
    sink = sinks.astype(jnp.float32).reshape(Hkv, G)[:, :, None, None, None]
    m = jnp.maximum(jnp.max(s, axis=-1, keepdims=True), sink)
    e = jnp.exp(s - m)
    p = e / (jnp.sum(e, axis=-1, keepdims=True) + jnp.exp(sink - m))
    o = jnp.einsum('bkgnij,bnjkd->bnikgd', p.astype(v.dtype), vb)
    return o.reshape(B, T, Hq * d)


def gated_delta_rule_chunked(q, k, v, g, beta):
    B, T, H, dk = q.shape
    dv = v.shape[-1]
    N = T // CHUNK
    f32 = jnp.float32

    def chunks(t):
        return t.astype(f32).reshape(B, N, CHUNK, H, -1).transpose(0, 3, 1, 2, 4)

    def chunks_s(t):
        return t.astype(f32).reshape(B, N, CHUNK, H).transpose(0, 3, 1, 2)

    q = chunks(q) * (dk ** -0.5)
    k, v = chunks(k), chunks(v)
    g, beta = chunks_s(g), chunks_s(beta)
    gc = jnp.cumsum(g, axis=-1)
    causal = jnp.tril(jnp.ones((CHUNK, CHUNK), dtype=bool))
    strict = jnp.tril(jnp.ones((CHUNK, CHUNK), dtype=bool), -1)
    diff = gc[..., :, None] - gc[..., None, :]
    decay = jnp.where(causal, jnp.exp(jnp.where(causal, diff, 0.0)), 0.0)
    kb = k * beta[..., None]
    low = jnp.where(strict, jnp.einsum('bhnid,bhnjd->bhnij', kb, k) * decay, 0.0)
    rhs = jnp.concatenate([v * beta[..., None], kb * jnp.exp(gc)[..., None]], axis=-1)
    sol = lax.linalg.triangular_solve(low, rhs, left_side=True, lower=True, unit_diagonal=True)
    u, w = sol[..., :dv], sol[..., dv:]
    attn = jnp.einsum('bhnid,bhnjd->bhnij', q, k) * decay
    q_dec = q * jnp.exp(gc)[..., None]
    k_dec = k * jnp.exp(gc[..., -1:] - gc)[..., None]
    g_last = jnp.exp(gc[..., -1])

    def step(S, xs):
        u_n, w_n, attn_n, qd_n, kd_n, gl_n = xs
        v_new = u_n - jnp.einsum('bhcd,bhde->bhce', w_n, S)
        o_n = jnp.einsum('bhcd,bhde->bhce', qd_n, S) + jnp.einsum('bhij,bhje->bhie', attn_n, v_new)
        S = S * gl_n[..., None, None] + jnp.einsum('bhcd,bhce->bhde', kd_n, v_new)
        return S, o_n

    xs = tuple(jnp.moveaxis(t, 2, 0) for t in (u, w, attn, q_dec, k_dec, g_last))
    S0 = jnp.zeros((B, H, dk, dv), f32)
    _, o = lax.scan(step, S0, xs)
    return o.transpose(1, 0, 3, 2, 4).reshape(B, T, H, dv)


def attn_deltanet_mixer(h, w_in, dn_conv_w, attn_sinks, dn_a_log, dn_dt_bias, dn_norm_w, w_out):
    B, T, _ = h.shape
    proj = h @ w_in
    split_idx = list(np.cumsum(IN_SPLIT_SIZES)[:-1])
    qa, ka, va, qkv_b, z, b_raw, a_raw = jnp.split(proj, split_idx, axis=-1)
    att = sliding_window_attention(qa.reshape(B, T, ATTN_HEADS, HEAD_DIM),
                                   ka.reshape(B, T, ATTN_KV_HEADS, HEAD_DIM),
                                   va.reshape(B, T, ATTN_KV_HEADS, HEAD_DIM), attn_sinks)
    qkv_b = jax.nn.silu(causal_depthwise_conv(qkv_b, dn_conv_w))
    qb, kb, vb = jnp.split(qkv_b, [QK_B, 2 * QK_B], axis=-1)
    qb = l2norm(qb.reshape(B, T, DN_HEADS, DN_DK))
    kb = l2norm(kb.reshape(B, T, DN_HEADS, DN_DK))
    vb = vb.reshape(B, T, DN_HEADS, DN_DV)
    beta = jax.nn.sigmoid(b_raw.astype(jnp.float32))
    g = -jnp.exp(dn_a_log.astype(jnp.float32)) * jax.nn.softplus(
        a_raw.astype(jnp.float32) + dn_dt_bias.astype(jnp.float32))
    o = gated_delta_rule_chunked(qb, kb, vb, g, beta)
    o = rmsnorm(o, dn_norm_w) * jax.nn.silu(z.reshape(B, T, DN_HEADS, DN_DV).astype(jnp.float32))
    mix = jnp.concatenate([att, o.reshape(B, T, V_B).astype(h.dtype)], axis=-1)
    return mix @ w_out


def conformer_conv_module(h, w_pw1, b_pw1, w_dw, b_dw, ln_w, ln_b, w_pw2, b_pw2):
    u = h @ w_pw1 + b_pw1
    u = u[..., :CONV_DIM] * jax.nn.sigmoid(u[..., CONV_DIM:])
    u = causal_depthwise_conv(u, w_dw) + b_dw
    u = jax.nn.silu(layernorm(u, ln_w, ln_b))
    return u @ w_pw2 + b_pw2


def _fwd_setup_inputs(seed: int = 0) -> dict:
    key = jax.random.key(seed)
    ks = jax.random.split(key, 24)
    f32 = jnp.float32

    def nrm(k, shape, scale):
        return jax.random.normal(k, shape, f32) * scale

    dt = jnp.exp(jax.random.uniform(ks[10], (N_EVEN, DN_HEADS), f32,
                                    np.log(1e-3), np.log(1e-1)))
    return {
        "x": nrm(ks[0], (BATCH, SEQ, D_MODEL), 1.0),
        "norm_w": 1.0 + nrm(ks[1], (DEPTH, 3, D_MODEL), 0.02),
        "ffn_w_gate": nrm(ks[2], (DEPTH, 2, D_MODEL, D_FF), D_MODEL ** -0.5),
        "ffn_w_up": nrm(ks[3], (DEPTH, 2, D_MODEL, D_FF), D_MODEL ** -0.5),
        "ffn_w_down": nrm(ks[4], (DEPTH, 2, D_FF, D_MODEL), D_FF ** -0.5),
        "mix_w_in": nrm(ks[5], (N_EVEN, D_MODEL, IN_COLS), D_MODEL ** -0.5),
        "dn_conv_w": nrm(ks[6], (N_EVEN, DN_CONV, QKV_B), DN_CONV ** -0.5),
        "attn_sinks": nrm(ks[7], (N_EVEN, ATTN_HEADS), 0.5),
        "dn_a_log": jnp.log(jax.random.uniform(ks[8], (N_EVEN, DN_HEADS), f32, 1.0, 16.0)),
        "dn_dt_bias": dt + jnp.log(-jnp.expm1(-dt)),
        "dn_norm_w": 1.0 + nrm(ks[9], (N_EVEN, DN_DV), 0.02),
        "mix_w_out": nrm(ks[11], (N_EVEN, MIX_WIDTH, D_MODEL), MIX_WIDTH ** -0.5),
        "conv_w_pw1": nrm(ks[12], (N_ODD, D_MODEL, 2 * CONV_DIM), D_MODEL ** -0.5),
        "conv_b_pw1": nrm(ks[13], (N_ODD, 2 * CONV_DIM), 0.02),
        "conv_w_dw": nrm(ks[14], (N_ODD, CONV_WIDTH, CONV_DIM), CONV_WIDTH ** -0.5),
        "conv_b_dw": nrm(ks[15], (N_ODD, CONV_DIM), 0.02),
        "conv_ln_w": 1.0 + nrm(ks[16], (N_ODD, CONV_DIM), 0.02),
        "conv_ln_b": nrm(ks[17], (N_ODD, CONV_DIM), 0.02),
        "conv_w_pw2": nrm(ks[18], (N_ODD, CONV_DIM, D_MODEL), CONV_DIM ** -0.5),
        "conv_b_pw2": nrm(ks[19], (N_ODD, D_MODEL), 0.02),
        "final_norm_w": 1.0 + nrm(ks[20], (D_MODEL,), 0.02),
    }


def _fwd_reference(x, norm_w, ffn_w_gate, ffn_w_up, ffn_w_down, mix_w_in, dn_conv_w, attn_sinks,
              dn_a_log, dn_dt_bias, dn_norm_w, mix_w_out, conv_w_pw1, conv_b_pw1, conv_w_dw,
              conv_b_dw, conv_ln_w, conv_ln_b, conv_w_pw2, conv_b_pw2, final_norm_w):
    for layer in range(DEPTH):
        x = x + 0.5 * swiglu(rmsnorm(x, norm_w[layer, 0]),
                             ffn_w_gate[layer, 0], ffn_w_up[layer, 0], ffn_w_down[layer, 0])
        h = rmsnorm(x, norm_w[layer, 1])
        if layer % 2 == 0:
            e = layer // 2
            x = x + attn_deltanet_mixer(h, mix_w_in[e], dn_conv_w[e], attn_sinks[e], dn_a_log[e],
                                        dn_dt_bias[e], dn_norm_w[e], mix_w_out[e])
        else:
            c = layer // 2
            x = x + conformer_conv_module(h, conv_w_pw1[c], conv_b_pw1[c], conv_w_dw[c], conv_b_dw[c],
                                          conv_ln_w[c], conv_ln_b[c], conv_w_pw2[c], conv_b_pw2[c])
        x = x + 0.5 * swiglu(rmsnorm(x, norm_w[layer, 2]),
                             ffn_w_gate[layer, 1], ffn_w_up[layer, 1], ffn_w_down[layer, 1])
    return rmsnorm(x, final_norm_w)


import jax as _jax
import jax.numpy as _jnp

TWIN_FORMAT = 'train_step'
FWD_PARAMS = ['x', 'norm_w', 'ffn_w_gate', 'ffn_w_up', 'ffn_w_down', 'mix_w_in', 'dn_conv_w', 'attn_sinks', 'dn_a_log', 'dn_dt_bias', 'dn_norm_w', 'mix_w_out', 'conv_w_pw1', 'conv_b_pw1', 'conv_w_dw', 'conv_b_dw', 'conv_ln_w', 'conv_ln_b', 'conv_w_pw2', 'conv_b_pw2', 'final_norm_w']
TWIN_WEIGHTS = ['norm_w', 'ffn_w_gate', 'ffn_w_up', 'ffn_w_down', 'mix_w_in', 'dn_conv_w', 'attn_sinks', 'dn_a_log', 'dn_dt_bias', 'dn_norm_w', 'mix_w_out', 'conv_w_pw1', 'conv_b_pw1', 'conv_w_dw', 'conv_b_dw', 'conv_ln_w', 'conv_ln_b', 'conv_w_pw2', 'conv_b_pw2', 'final_norm_w']
TWIN_DIFF_INPUT = 'x'
TWIN_INPUTS = ['x', 'norm_w', 'ffn_w_gate', 'ffn_w_up', 'ffn_w_down', 'mix_w_in', 'dn_conv_w', 'attn_sinks', 'dn_a_log', 'dn_dt_bias', 'dn_norm_w', 'mix_w_out', 'conv_w_pw1', 'conv_b_pw1', 'conv_w_dw', 'conv_b_dw', 'conv_ln_w', 'conv_ln_b', 'conv_w_pw2', 'conv_b_pw2', 'final_norm_w', 'loss_target', 'm_norm_w', 'm_ffn_w_gate', 'm_ffn_w_up', 'm_ffn_w_down', 'm_mix_w_in', 'm_dn_conv_w', 'm_attn_sinks', 'm_dn_a_log', 'm_dn_dt_bias', 'm_dn_norm_w', 'm_mix_w_out', 'm_conv_w_pw1', 'm_conv_b_pw1', 'm_conv_w_dw', 'm_conv_b_dw', 'm_conv_ln_w', 'm_conv_ln_b', 'm_conv_w_pw2', 'm_conv_b_pw2', 'm_final_norm_w', 'v_norm_w', 'v_ffn_w_gate', 'v_ffn_w_up', 'v_ffn_w_down', 'v_mix_w_in', 'v_dn_conv_w', 'v_attn_sinks', 'v_dn_a_log', 'v_dn_dt_bias', 'v_dn_norm_w', 'v_mix_w_out', 'v_conv_w_pw1', 'v_conv_b_pw1', 'v_conv_w_dw', 'v_conv_b_dw', 'v_conv_ln_w', 'v_conv_ln_b', 'v_conv_w_pw2', 'v_conv_b_pw2', 'v_final_norm_w']
TWIN_OUTPUTS = ['loss', 'grad_x', 'grad_norm_w', 'grad_ffn_w_gate', 'grad_ffn_w_up', 'grad_ffn_w_down', 'grad_mix_w_in', 'grad_dn_conv_w', 'grad_attn_sinks', 'grad_dn_a_log', 'grad_dn_dt_bias', 'grad_dn_norm_w', 'grad_mix_w_out', 'grad_conv_w_pw1', 'grad_conv_b_pw1', 'grad_conv_w_dw', 'grad_conv_b_dw', 'grad_conv_ln_w', 'grad_conv_ln_b', 'grad_conv_w_pw2', 'grad_conv_b_pw2', 'grad_final_norm_w', 'delta_norm_w', 'delta_ffn_w_gate', 'delta_ffn_w_up', 'delta_ffn_w_down', 'delta_mix_w_in', 'delta_dn_conv_w', 'delta_attn_sinks', 'delta_dn_a_log', 'delta_dn_dt_bias', 'delta_dn_norm_w', 'delta_mix_w_out', 'delta_conv_w_pw1', 'delta_conv_b_pw1', 'delta_conv_w_dw', 'delta_conv_b_dw', 'delta_conv_ln_w', 'delta_conv_ln_b', 'delta_conv_w_pw2', 'delta_conv_b_pw2', 'delta_final_norm_w', 'new_m_norm_w', 'new_m_ffn_w_gate', 'new_m_ffn_w_up', 'new_m_ffn_w_down', 'new_m_mix_w_in', 'new_m_dn_conv_w', 'new_m_attn_sinks', 'new_m_dn_a_log', 'new_m_dn_dt_bias', 'new_m_dn_norm_w', 'new_m_mix_w_out', 'new_m_conv_w_pw1', 'new_m_conv_b_pw1', 'new_m_conv_w_dw', 'new_m_conv_b_dw', 'new_m_conv_ln_w', 'new_m_conv_ln_b', 'new_m_conv_w_pw2', 'new_m_conv_b_pw2', 'new_m_final_norm_w', 'new_v_norm_w', 'new_v_ffn_w_gate', 'new_v_ffn_w_up', 'new_v_ffn_w_down', 'new_v_mix_w_in', 'new_v_dn_conv_w', 'new_v_attn_sinks', 'new_v_dn_a_log', 'new_v_dn_dt_bias', 'new_v_dn_norm_w', 'new_v_mix_w_out', 'new_v_conv_w_pw1', 'new_v_conv_b_pw1', 'new_v_conv_w_dw', 'new_v_conv_b_dw', 'new_v_conv_ln_w', 'new_v_conv_ln_b', 'new_v_conv_w_pw2', 'new_v_conv_b_pw2', 'new_v_final_norm_w']
TWIN_LEAF_KINDS = {'loss': 'loss', 'grad_x': 'grad_x', 'grad_norm_w': 'grad_w', 'grad_ffn_w_gate': 'grad_w', 'grad_ffn_w_up': 'grad_w', 'grad_ffn_w_down': 'grad_w', 'grad_mix_w_in': 'grad_w', 'grad_dn_conv_w': 'grad_w', 'grad_attn_sinks': 'grad_w', 'grad_dn_a_log': 'grad_w', 'grad_dn_dt_bias': 'grad_w', 'grad_dn_norm_w': 'grad_w', 'grad_mix_w_out': 'grad_w', 'grad_conv_w_pw1': 'grad_w', 'grad_conv_b_pw1': 'grad_w', 'grad_conv_w_dw': 'grad_w', 'grad_conv_b_dw': 'grad_w', 'grad_conv_ln_w': 'grad_w', 'grad_conv_ln_b': 'grad_w', 'grad_conv_w_pw2': 'grad_w', 'grad_conv_b_pw2': 'grad_w', 'grad_final_norm_w': 'grad_w', 'delta_norm_w': 'delta_w', 'delta_ffn_w_gate': 'delta_w', 'delta_ffn_w_up': 'delta_w', 'delta_ffn_w_down': 'delta_w', 'delta_mix_w_in': 'delta_w', 'delta_dn_conv_w': 'delta_w', 'delta_attn_sinks': 'delta_w', 'delta_dn_a_log': 'delta_w', 'delta_dn_dt_bias': 'delta_w', 'delta_dn_norm_w': 'delta_w', 'delta_mix_w_out': 'delta_w', 'delta_conv_w_pw1': 'delta_w', 'delta_conv_b_pw1': 'delta_w', 'delta_conv_w_dw': 'delta_w', 'delta_conv_b_dw': 'delta_w', 'delta_conv_ln_w': 'delta_w', 'delta_conv_ln_b': 'delta_w', 'delta_conv_w_pw2': 'delta_w', 'delta_conv_b_pw2': 'delta_w', 'delta_final_norm_w': 'delta_w', 'new_m_norm_w': 'new_m', 'new_m_ffn_w_gate': 'new_m', 'new_m_ffn_w_up': 'new_m', 'new_m_ffn_w_down': 'new_m', 'new_m_mix_w_in': 'new_m', 'new_m_dn_conv_w': 'new_m', 'new_m_attn_sinks': 'new_m', 'new_m_dn_a_log': 'new_m', 'new_m_dn_dt_bias': 'new_m', 'new_m_dn_norm_w': 'new_m', 'new_m_mix_w_out': 'new_m', 'new_m_conv_w_pw1': 'new_m', 'new_m_conv_b_pw1': 'new_m', 'new_m_conv_w_dw': 'new_m', 'new_m_conv_b_dw': 'new_m', 'new_m_conv_ln_w': 'new_m', 'new_m_conv_ln_b': 'new_m', 'new_m_conv_w_pw2': 'new_m', 'new_m_conv_b_pw2': 'new_m', 'new_m_final_norm_w': 'new_m', 'new_v_norm_w': 'new_v', 'new_v_ffn_w_gate': 'new_v', 'new_v_ffn_w_up': 'new_v', 'new_v_ffn_w_down': 'new_v', 'new_v_mix_w_in': 'new_v', 'new_v_dn_conv_w': 'new_v', 'new_v_attn_sinks': 'new_v', 'new_v_dn_a_log': 'new_v', 'new_v_dn_dt_bias': 'new_v', 'new_v_dn_norm_w': 'new_v', 'new_v_mix_w_out': 'new_v', 'new_v_conv_w_pw1': 'new_v', 'new_v_conv_b_pw1': 'new_v', 'new_v_conv_w_dw': 'new_v', 'new_v_conv_b_dw': 'new_v', 'new_v_conv_ln_w': 'new_v', 'new_v_conv_ln_b': 'new_v', 'new_v_conv_w_pw2': 'new_v', 'new_v_conv_b_pw2': 'new_v', 'new_v_final_norm_w': 'new_v'}


def _forward(args):
    return _fwd_reference(*[args[k] for k in FWD_PARAMS])


def _output_shape():
    out = _jax.eval_shape(lambda: _forward(_fwd_setup_inputs(0)))
    return out.shape, out.dtype

N_MICROBATCH = 1
ADAM_LR = 0.001
ADAM_B1 = 0.9
ADAM_B2 = 0.999
ADAM_EPS = 1e-08
ADAM_WD = 0.01
ADAM_STEP = 10
PER_EXAMPLE_BATCH_AXIS = {'x': 0, 'loss_target': 0}
SHARED_INPUTS = []
_WEIGHT_DTYPES = {'norm_w': _jnp.float32, 'ffn_w_gate': _jnp.float32, 'ffn_w_up': _jnp.float32, 'ffn_w_down': _jnp.float32, 'mix_w_in': _jnp.float32, 'dn_conv_w': _jnp.float32, 'attn_sinks': _jnp.float32, 'dn_a_log': _jnp.float32, 'dn_dt_bias': _jnp.float32, 'dn_norm_w': _jnp.float32, 'mix_w_out': _jnp.float32, 'conv_w_pw1': _jnp.float32, 'conv_b_pw1': _jnp.float32, 'conv_w_dw': _jnp.float32, 'conv_b_dw': _jnp.float32, 'conv_ln_w': _jnp.float32, 'conv_ln_b': _jnp.float32, 'conv_w_pw2': _jnp.float32, 'conv_b_pw2': _jnp.float32, 'final_norm_w': _jnp.float32}
MOMENT_SCALE = {'norm_w': 6.730166e-02, 'ffn_w_gate': 2.252258e-02, 'ffn_w_up': 2.181803e-02, 'ffn_w_down': 3.618490e-02, 'mix_w_in': 6.415313e-02, 'dn_conv_w': 6.820658e-02, 'attn_sinks': 3.747776e-02, 'dn_a_log': 2.762196e-01, 'dn_dt_bias': 2.744661e-01, 'dn_norm_w': 2.362999e-01, 'mix_w_out': 5.777235e-02, 'conv_w_pw1': 4.620938e-02, 'conv_b_pw1': 5.880292e-02, 'conv_w_dw': 6.105286e-02, 'conv_b_dw': 1.443081e-01, 'conv_ln_w': 7.526853e-02, 'conv_ln_b': 7.010403e-02, 'conv_w_pw2': 6.009449e-02, 'conv_b_pw2': 1.134621e-01, 'final_norm_w': 1.603820e+01}


def _to_microbatches(a, axis):
    t = _jnp.moveaxis(a, axis, 0)
    t = t.reshape((N_MICROBATCH, t.shape[0] // N_MICROBATCH) + t.shape[1:])
    return _jnp.moveaxis(t, 1, axis + 1)


def setup_inputs(seed: int = 0) -> dict:
    inp = _fwd_setup_inputs(seed)
    key = _jax.random.fold_in(_jax.random.key(seed), 7919)
    shape, _ = _output_shape()
    out = dict(inp)
    out["loss_target"] = _jax.random.normal(_jax.random.fold_in(key, 0), shape, _jnp.float32)
    for i, name in enumerate(TWIN_WEIGHTS):
        w = inp[name].astype(_jnp.float32)
        if MOMENT_SCALE is None:
            s = _jnp.sqrt(_jnp.mean(_jnp.square(w)) + 1e-30)
        else:
            s = MOMENT_SCALE[name]
        km, kv = _jax.random.split(_jax.random.fold_in(key, i + 1))
        out[name] = w
        out["m_" + name] = s * _jax.random.normal(km, w.shape, _jnp.float32)
        out["v_" + name] = (s * s) * _jax.random.uniform(kv, w.shape, _jnp.float32, 0.5, 1.5)
    if N_MICROBATCH > 1:
        for name, axis in PER_EXAMPLE_BATCH_AXIS.items():
            out[name] = _to_microbatches(out[name], axis)
    return {'x': out['x'], 'norm_w': out['norm_w'], 'ffn_w_gate': out['ffn_w_gate'], 'ffn_w_up': out['ffn_w_up'], 'ffn_w_down': out['ffn_w_down'], 'mix_w_in': out['mix_w_in'], 'dn_conv_w': out['dn_conv_w'], 'attn_sinks': out['attn_sinks'], 'dn_a_log': out['dn_a_log'], 'dn_dt_bias': out['dn_dt_bias'], 'dn_norm_w': out['dn_norm_w'], 'mix_w_out': out['mix_w_out'], 'conv_w_pw1': out['conv_w_pw1'], 'conv_b_pw1': out['conv_b_pw1'], 'conv_w_dw': out['conv_w_dw'], 'conv_b_dw': out['conv_b_dw'], 'conv_ln_w': out['conv_ln_w'], 'conv_ln_b': out['conv_ln_b'], 'conv_w_pw2': out['conv_w_pw2'], 'conv_b_pw2': out['conv_b_pw2'], 'final_norm_w': out['final_norm_w'], 'loss_target': out['loss_target'], 'm_norm_w': out['m_norm_w'], 'm_ffn_w_gate': out['m_ffn_w_gate'], 'm_ffn_w_up': out['m_ffn_w_up'], 'm_ffn_w_down': out['m_ffn_w_down'], 'm_mix_w_in': out['m_mix_w_in'], 'm_dn_conv_w': out['m_dn_conv_w'], 'm_attn_sinks': out['m_attn_sinks'], 'm_dn_a_log': out['m_dn_a_log'], 'm_dn_dt_bias': out['m_dn_dt_bias'], 'm_dn_norm_w': out['m_dn_norm_w'], 'm_mix_w_out': out['m_mix_w_out'], 'm_conv_w_pw1': out['m_conv_w_pw1'], 'm_conv_b_pw1': out['m_conv_b_pw1'], 'm_conv_w_dw': out['m_conv_w_dw'], 'm_conv_b_dw': out['m_conv_b_dw'], 'm_conv_ln_w': out['m_conv_ln_w'], 'm_conv_ln_b': out['m_conv_ln_b'], 'm_conv_w_pw2': out['m_conv_w_pw2'], 'm_conv_b_pw2': out['m_conv_b_pw2'], 'm_final_norm_w': out['m_final_norm_w'], 'v_norm_w': out['v_norm_w'], 'v_ffn_w_gate': out['v_ffn_w_gate'], 'v_ffn_w_up': out['v_ffn_w_up'], 'v_ffn_w_down': out['v_ffn_w_down'], 'v_mix_w_in': out['v_mix_w_in'], 'v_dn_conv_w': out['v_dn_conv_w'], 'v_attn_sinks': out['v_attn_sinks'], 'v_dn_a_log': out['v_dn_a_log'], 'v_dn_dt_bias': out['v_dn_dt_bias'], 'v_dn_norm_w': out['v_dn_norm_w'], 'v_mix_w_out': out['v_mix_w_out'], 'v_conv_w_pw1': out['v_conv_w_pw1'], 'v_conv_b_pw1': out['v_conv_b_pw1'], 'v_conv_w_dw': out['v_conv_w_dw'], 'v_conv_b_dw': out['v_conv_b_dw'], 'v_conv_ln_w': out['v_conv_ln_w'], 'v_conv_ln_b': out['v_conv_ln_b'], 'v_conv_w_pw2': out['v_conv_w_pw2'], 'v_conv_b_pw2': out['v_conv_b_pw2'], 'v_final_norm_w': out['v_final_norm_w']}


def _loss(weights, diff, rest, loss_target):
    with _jax.named_scope("forward"):
        args = {**rest, TWIN_DIFF_INPUT: diff, **{k: w.astype(_WEIGHT_DTYPES[k]) for k, w in weights.items()}}
        y = _forward(args)
    with _jax.named_scope("loss_head"):
        err = _jnp.square(y.astype(_jnp.float32) - loss_target)
        return 0.5 * _jnp.sum(_jnp.mean(err, axis=-1)) if err.ndim else 0.5 * err


def _adamw(w, g, m, v):
    m = ADAM_B1 * m + (1.0 - ADAM_B1) * g
    v = ADAM_B2 * v + (1.0 - ADAM_B2) * _jnp.square(g)
    m_hat = m / (1.0 - ADAM_B1 ** ADAM_STEP)
    v_hat = v / (1.0 - ADAM_B2 ** ADAM_STEP)
    delta = -ADAM_LR * (m_hat / (_jnp.sqrt(v_hat) + ADAM_EPS) + ADAM_WD * w)
    return delta, m, v


def reference(x, norm_w, ffn_w_gate, ffn_w_up, ffn_w_down, mix_w_in, dn_conv_w, attn_sinks, dn_a_log, dn_dt_bias, dn_norm_w, mix_w_out, conv_w_pw1, conv_b_pw1, conv_w_dw, conv_b_dw, conv_ln_w, conv_ln_b, conv_w_pw2, conv_b_pw2, final_norm_w, loss_target, m_norm_w, m_ffn_w_gate, m_ffn_w_up, m_ffn_w_down, m_mix_w_in, m_dn_conv_w, m_attn_sinks, m_dn_a_log, m_dn_dt_bias, m_dn_norm_w, m_mix_w_out, m_conv_w_pw1, m_conv_b_pw1, m_conv_w_dw, m_conv_b_dw, m_conv_ln_w, m_conv_ln_b, m_conv_w_pw2, m_conv_b_pw2, m_final_norm_w, v_norm_w, v_ffn_w_gate, v_ffn_w_up, v_ffn_w_down, v_mix_w_in, v_dn_conv_w, v_attn_sinks, v_dn_a_log, v_dn_dt_bias, v_dn_norm_w, v_mix_w_out, v_conv_w_pw1, v_conv_b_pw1, v_conv_w_dw, v_conv_b_dw, v_conv_ln_w, v_conv_ln_b, v_conv_w_pw2, v_conv_b_pw2, v_final_norm_w):
    given = dict(x=x, norm_w=norm_w, ffn_w_gate=ffn_w_gate, ffn_w_up=ffn_w_up, ffn_w_down=ffn_w_down, mix_w_in=mix_w_in, dn_conv_w=dn_conv_w, attn_sinks=attn_sinks, dn_a_log=dn_a_log, dn_dt_bias=dn_dt_bias, dn_norm_w=dn_norm_w, mix_w_out=mix_w_out, conv_w_pw1=conv_w_pw1, conv_b_pw1=conv_b_pw1, conv_w_dw=conv_w_dw, conv_b_dw=conv_b_dw, conv_ln_w=conv_ln_w, conv_ln_b=conv_ln_b, conv_w_pw2=conv_w_pw2, conv_b_pw2=conv_b_pw2, final_norm_w=final_norm_w, loss_target=loss_target, m_norm_w=m_norm_w, m_ffn_w_gate=m_ffn_w_gate, m_ffn_w_up=m_ffn_w_up, m_ffn_w_down=m_ffn_w_down, m_mix_w_in=m_mix_w_in, m_dn_conv_w=m_dn_conv_w, m_attn_sinks=m_attn_sinks, m_dn_a_log=m_dn_a_log, m_dn_dt_bias=m_dn_dt_bias, m_dn_norm_w=m_dn_norm_w, m_mix_w_out=m_mix_w_out, m_conv_w_pw1=m_conv_w_pw1, m_conv_b_pw1=m_conv_b_pw1, m_conv_w_dw=m_conv_w_dw, m_conv_b_dw=m_conv_b_dw, m_conv_ln_w=m_conv_ln_w, m_conv_ln_b=m_conv_ln_b, m_conv_w_pw2=m_conv_w_pw2, m_conv_b_pw2=m_conv_b_pw2, m_final_norm_w=m_final_norm_w, v_norm_w=v_norm_w, v_ffn_w_gate=v_ffn_w_gate, v_ffn_w_up=v_ffn_w_up, v_ffn_w_down=v_ffn_w_down, v_mix_w_in=v_mix_w_in, v_dn_conv_w=v_dn_conv_w, v_attn_sinks=v_attn_sinks, v_dn_a_log=v_dn_a_log, v_dn_dt_bias=v_dn_dt_bias, v_dn_norm_w=v_dn_norm_w, v_mix_w_out=v_mix_w_out, v_conv_w_pw1=v_conv_w_pw1, v_conv_b_pw1=v_conv_b_pw1, v_conv_w_dw=v_conv_w_dw, v_conv_b_dw=v_conv_b_dw, v_conv_ln_w=v_conv_ln_w, v_conv_ln_b=v_conv_ln_b, v_conv_w_pw2=v_conv_w_pw2, v_conv_b_pw2=v_conv_b_pw2, v_final_norm_w=v_final_norm_w)
    weights = {n: given[n] for n in TWIN_WEIGHTS}
    shared = {n: given[n] for n in SHARED_INPUTS}
    per_example = {n: given[n] for n in ['x']}
    grad_fn = _jax.value_and_grad(_loss, argnums=(0, 1))

    def one_microbatch(ex, loss_target):
        ex = dict(ex)
        diff = ex.pop(TWIN_DIFF_INPUT)
        return grad_fn(weights, diff, {**shared, **ex}, loss_target)

    if N_MICROBATCH == 1:
        loss, (grad_w, grad_x) = one_microbatch(per_example, given["loss_target"])
    else:
        def body(carry, xs):
            loss_sum, grad_sum = carry
            l_k, (gw_k, gx_k) = one_microbatch(xs[0], xs[1])
            with _jax.named_scope("update"):
                return (loss_sum + l_k, _jax.tree.map(_jnp.add, grad_sum, gw_k)), gx_k

        init = (_jnp.zeros((), _jnp.float32), _jax.tree.map(_jnp.zeros_like, weights))
        (loss, grad_w), grad_x = _jax.lax.scan(body, init, (per_example, given["loss_target"]))
    with _jax.named_scope("update"):
        delta_w, new_m, new_v = {}, {}, {}
        for n in TWIN_WEIGHTS:
            delta_w[n], new_m[n], new_v[n] = _adamw(weights[n], grad_w[n], given["m_" + n], given["v_" + n])
    return (loss, grad_x, *[grad_w[n] for n in TWIN_WEIGHTS], *[delta_w[n] for n in TWIN_WEIGHTS],
            *[new_m[n] for n in TWIN_WEIGHTS], *[new_v[n] for n in TWIN_WEIGHTS])
```

```python
import functools

import numpy as np
import jax
import jax.numpy as jnp
from jax import lax
from jax.experimental import pallas as pl
from jax.experimental.pallas import tpu as pltpu

F32 = jnp.float32
BF16 = jnp.bfloat16
HI = lax.Precision.HIGHEST
EPS = 1e-6
N_DEV = 8
V7X_VMEM_LIMIT = 48 * 2**20
MESH = pl.DeviceIdType.MESH

DEPTH = 4
D_MODEL = 1024
ATTN_HEADS, ATTN_KV_HEADS, HEAD_DIM, ATTN_BLOCK = 8, 2, 64, 128
DN_HEADS, DN_D, DN_CHUNK, DN_CONV = 8, 64, 64, 4
CONV_WIDTH = 31
Q_A, KV_A, QKV_B, V_B = 512, 128, 1536, 512
IN_COLS = 2832
IN_COLS_PAD = 3072
OFF_QKVB = Q_A + 2 * KV_A
OFF_Z = OFF_QKVB + QKV_B
OFF_BETA = OFF_Z + V_B
OFF_A = OFF_BETA + DN_HEADS

ADAM_LR, ADAM_B1, ADAM_B2, ADAM_EPS, ADAM_WD, ADAM_STEP = 0.001, 0.9, 0.999, 1e-08, 0.01, 10


def _cparams(sem):
    return pltpu.CompilerParams(dimension_semantics=sem, vmem_limit_bytes=V7X_VMEM_LIMIT)


def _sigmoid(x):
    return 1.0 / (1.0 + jnp.exp(-x))


def _softplus(x):
    return jnp.maximum(x, 0.0) + jnp.log(1.0 + jnp.exp(-jnp.abs(x)))


def _dot(a, b):
    return jnp.dot(a, b, preferred_element_type=F32)


def _dot_nt(a, b):
    return lax.dot_general(a, b, (((1,), (1,)), ((), ())), preferred_element_type=F32)


def _dot_tn(a, b):
    return lax.dot_general(a, b, (((0,), (0,)), ((), ())), preferred_element_type=F32)


def _rms(x, w):
    return x * lax.rsqrt(jnp.mean(x * x, axis=-1, keepdims=True) + EPS) * w


def _rms_bwd(x, w, dxn):
    r = lax.rsqrt(jnp.mean(x * x, axis=-1, keepdims=True) + EPS)
    xh = x * r
    dxh = dxn * w
    dx = r * (dxh - xh * jnp.mean(dxh * xh, axis=-1, keepdims=True))
    return dx, jnp.sum(dxn * xh, axis=0, keepdims=True)


def ffn_fwd(x, nw, wg, wu, wd):
    T, D = x.shape
    J, _, F = wg.shape
    tm = min(T, 512)

    def body(x_ref, nw_ref, wg_ref, wu_ref, wd_ref, o_ref, xn_ref, acc_ref):
        j = pl.program_id(1)

        @pl.when(j == 0)
        def _():
            xn_ref[...] = _rms(x_ref[...], nw_ref[...]).astype(BF16)
            acc_ref[...] = jnp.zeros_like(acc_ref)

        xn = xn_ref[...]
        g = _dot(xn, wg_ref[0])
        u = _dot(xn, wu_ref[0])
        h = (g * _sigmoid(g) * u).astype(BF16)
        acc_ref[...] += _dot(h, wd_ref[0])

        @pl.when(j == J - 1)
        def _():
            o_ref[...] = x_ref[...] + 0.5 * acc_ref[...]

    return pl.pallas_call(
        body, name="ffn_fwd", grid=(T // tm, J),
        in_specs=[pl.BlockSpec((tm, D), lambda t, j: (t, 0)),
                  pl.BlockSpec((1, D), lambda t, j: (0, 0)),
                  pl.BlockSpec((1, D, F), lambda t, j: (j, 0, 0)),
                  pl.BlockSpec((1, D, F), lambda t, j: (j, 0, 0)),
                  pl.BlockSpec((1, F, D), lambda t, j: (j, 0, 0))],
        out_specs=pl.BlockSpec((tm, D), lambda t, j: (t, 0)),
        out_shape=jax.ShapeDtypeStruct((T, D), F32),
        scratch_shapes=[pltpu.VMEM((tm, D), BF16), pltpu.VMEM((tm, D), F32)],
        compiler_params=_cparams(("parallel", "arbitrary")),
    )(x, nw, wg, wu, wd)


def ffn_bwd(x, dy, nw, wg, wu, wd):
    T, D = x.shape
    J, _, F = wg.shape
    tm = min(T, 256)
    nt = T // tm

    def body(x_ref, dy_ref, nw_ref, wg_ref, wu_ref, wd_ref,
             dx_ref, dwg_ref, dwu_ref, dwd_ref, dnw_ref,
             xn_ref, dyh_ref, dxn_ref, awg_ref, awu_ref, awd_ref):
        j = pl.program_id(0)
        t = pl.program_id(1)
        rows = pl.ds(pl.multiple_of(t * tm, tm), tm)

        @pl.when(j == 0)
        def _():
            xn_ref[rows, :] = _rms(x_ref[...], nw_ref[...]).astype(BF16)
            dyh_ref[rows, :] = (0.5 * dy_ref[...]).astype(BF16)
            dxn_ref[rows, :] = jnp.zeros((tm, D), F32)

        @pl.when((j == 0) & (t == 0))
        def _():
            dnw_ref[...] = jnp.zeros_like(dnw_ref)

        @pl.when(t == 0)
        def _():
            awg_ref[...] = jnp.zeros_like(awg_ref)
            awu_ref[...] = jnp.zeros_like(awu_ref)
            awd_ref[...] = jnp.zeros_like(awd_ref)

        xn = xn_ref[rows, :]
        dyh = dyh_ref[rows, :]
        g = _dot(xn, wg_ref[0])
        u = _dot(xn, wu_ref[0])
        sg = _sigmoid(g)
        s = g * sg
        h = (s * u).astype(BF16)
        dh = _dot_nt(dyh, wd_ref[0])
        du = (dh * s).astype(BF16)
        dg = (dh * u * (sg * (1.0 + g * (1.0 - sg)))).astype(BF16)
        awd_ref[...] += _dot_tn(h, dyh)
        awg_ref[...] += _dot_tn(xn, dg)
        awu_ref[...] += _dot_tn(xn, du)
        dxn_ref[rows, :] += _dot_nt(dg, wg_ref[0]) + _dot_nt(du, wu_ref[0])

        @pl.when(t == nt - 1)
        def _():
            dwg_ref[0] = awg_ref[...].astype(BF16)
            dwu_ref[0] = awu_ref[...].astype(BF16)
            dwd_ref[0] = awd_ref[...].astype(BF16)

        @pl.when(j == J - 1)
        def _():
            dx, dnw = _rms_bwd(x_ref[...], nw_ref[...], dxn_ref[rows, :])
            dx_ref[...] = dy_ref[...] + dx
            dnw_ref[...] += dnw

    ends = lambda j, t: (jnp.where((j == 0) | (j == J - 1), t, 0), 0)
    last = lambda j, t: (jnp.where(j == J - 1, t, 0), 0)
    return pl.pallas_call(
        body, name="ffn_bwd", grid=(J, nt),
        in_specs=[pl.BlockSpec((tm, D), ends), pl.BlockSpec((tm, D), ends),
                  pl.BlockSpec((1, D), lambda j, t: (0, 0)),
                  pl.BlockSpec((1, D, F), lambda j, t: (j, 0, 0)),
                  pl.BlockSpec((1, D, F), lambda j, t: (j, 0, 0)),
                  pl.BlockSpec((1, F, D), lambda j, t: (j, 0, 0))],
        out_specs=[pl.BlockSpec((tm, D), last),
                   pl.BlockSpec((1, D, F), lambda j, t: (j, 0, 0)),
                   pl.BlockSpec((1, D, F), lambda j, t: (j, 0, 0)),
                   pl.BlockSpec((1, F, D), lambda j, t: (j, 0, 0)),
                   pl.BlockSpec((1, D), lambda j, t: (0, 0))],
        out_shape=[jax.ShapeDtypeStruct((T, D), F32),
                   jax.ShapeDtypeStruct((J, D, F), BF16), jax.ShapeDtypeStruct((J, D, F), BF16),
                   jax.ShapeDtypeStruct((J, F, D), BF16), jax.ShapeDtypeStruct((1, D), F32)],
        scratch_shapes=[pltpu.VMEM((T, D), BF16), pltpu.VMEM((T, D), BF16), pltpu.VMEM((T, D), F32),
                        pltpu.VMEM((D, F), F32), pltpu.VMEM((D, F), F32), pltpu.VMEM((F, D), F32)],
        compiler_params=_cparams(("arbitrary", "arbitrary")),
    )(x, dy, nw, wg, wu, wd)


def rmslin_fwd(x, nw, w, b):
    T, D = x.shape
    N = w.shape[1]
    tm = min(T, 256)

    def body(x_ref, nw_ref, w_ref, b_ref, o_ref):
        xn = _rms(x_ref[...], nw_ref[...]).astype(BF16)
        o_ref[...] = _dot(xn, w_ref[...]) + b_ref[...]

    return pl.pallas_call(
        body, name="rmslin_fwd", grid=(T // tm,),
        in_specs=[pl.BlockSpec((tm, D), lambda t: (t, 0)), pl.BlockSpec((1, D), lambda t: (0, 0)),
                  pl.BlockSpec((D, N), lambda t: (0, 0)), pl.BlockSpec((1, N), lambda t: (0, 0))],
        out_specs=pl.BlockSpec((tm, N), lambda t: (t, 0)),
        out_shape=jax.ShapeDtypeStruct((T, N), F32),
        compiler_params=_cparams(("parallel",)),
    )(x, nw, w, b)


def rmslin_bwd(x, dres, dproj, nw, w):
    T, D = x.shape
    N = w.shape[1]
    nb = 1024
    nc = N // nb
    tm = min(T, 256)
    nt = T // tm

    def body(x_ref, dres_ref, dp_ref, nw_ref, w_ref, dx_ref, dw_ref, db_ref, dnw_ref, xn_ref, dxn_ref):
        c = pl.program_id(0)
        t = pl.program_id(1)
        rows = pl.ds(pl.multiple_of(t * tm, tm), tm)

        @pl.when(c == 0)
        def _():
            xn_ref[rows, :] = _rms(x_ref[...], nw_ref[...]).astype(BF16)
            dxn_ref[rows, :] = jnp.zeros((tm, D), F32)

        @pl.when((c == 0) & (t == 0))
        def _():
            dnw_ref[...] = jnp.zeros_like(dnw_ref)

        @pl.when(t == 0)
        def _():
            dw_ref[...] = jnp.zeros_like(dw_ref)
            db_ref[...] = jnp.zeros_like(db_ref)

        dpf = dp_ref[...]
        dp = dpf.astype(BF16)
        dw_ref[...] += _dot_tn(xn_ref[rows, :], dp)
        db_ref[...] += jnp.sum(dpf, axis=0, keepdims=True)
        dxn_ref[rows, :] += _dot_nt(dp, w_ref[...])

        @pl.when(c == nc - 1)
        def _():
            dx, dnw = _rms_bwd(x_ref[...], nw_ref[...], dxn_ref[rows, :])
            dx_ref[...] = dres_ref[...] + dx
            dnw_ref[...] += dnw

    ends = lambda c, t: (jnp.where((c == 0) | (c == nc - 1), t, 0), 0)
    last = lambda c, t: (jnp.where(c == nc - 1, t, 0), 0)
    return pl.pallas_call(
        body, name="rmslin_bwd", grid=(nc, nt),
        in_specs=[pl.BlockSpec((tm, D), ends), pl.BlockSpec((tm, D), last),
                  pl.BlockSpec((tm, nb), lambda c, t: (t, c)),
                  pl.BlockSpec((1, D), lambda c, t: (0, 0)),
                  pl.BlockSpec((D, nb), lambda c, t: (0, c))],
        out_specs=[pl.BlockSpec((tm, D), last),
                   pl.BlockSpec((D, nb), lambda c, t: (0, c)),
                   pl.BlockSpec((1, nb), lambda c, t: (0, c)),
                   pl.BlockSpec((1, D), lambda c, t: (0, 0))],
        out_shape=[jax.ShapeDtypeStruct((T, D), F32), jax.ShapeDtypeStruct((D, N), F32),
                   jax.ShapeDtypeStruct((1, N), F32), jax.ShapeDtypeStruct((1, D), F32)],
        scratch_shapes=[pltpu.VMEM((T, D), BF16), pltpu.VMEM((T, D), F32)],
        compiler_params=_cparams(("arbitrary", "arbitrary")),
    )(x, dres, dproj, nw, w)


def lin_fwd(res, a, w, b):
    T, K = a.shape
    N = w.shape[1]
    tm = min(T, 512)

    def body(res_ref, a_ref, w_ref, b_ref, o_ref):
        o_ref[...] = res_ref[...] + _dot(a_ref[...].astype(BF16), w_ref[...]) + b_ref[...]

    return pl.pallas_call(
        body, name="lin_fwd", grid=(T // tm,),
        in_specs=[pl.BlockSpec((tm, N), lambda t: (t, 0)), pl.BlockSpec((tm, K), lambda t: (t, 0)),
                  pl.BlockSpec((K, N), lambda t: (0, 0)), pl.BlockSpec((1, N), lambda t: (0, 0))],
        out_specs=pl.BlockSpec((tm, N), lambda t: (t, 0)),
        out_shape=jax.ShapeDtypeStruct((T, N), F32),
        compiler_params=_cparams(("parallel",)),
    )(res, a, w, b)


def lin_bwd(a, dy, w):
    T, K = a.shape
    N = w.shape[1]
    tm = min(T, 256)

    def body(a_ref, dy_ref, w_ref, da_ref, dw_ref, db_ref):
        @pl.when(pl.program_id(0) == 0)
        def _():
            dw_ref[...] = jnp.zeros_like(dw_ref)
            db_ref[...] = jnp.zeros_like(db_ref)

        dyf = dy_ref[...]
        dyb = dyf.astype(BF16)
        da_ref[...] = _dot_nt(dyb, w_ref[...])
        dw_ref[...] += _dot_tn(a_ref[...].astype(BF16), dyb)
        db_ref[...] += jnp.sum(dyf, axis=0, keepdims=True)

    return pl.pallas_call(
        body, name="lin_bwd", grid=(T // tm,),
        in_specs=[pl.BlockSpec((tm, K), lambda t: (t, 0)), pl.BlockSpec((tm, N), lambda t: (t, 0)),
                  pl.BlockSpec((K, N), lambda t: (0, 0))],
        out_specs=[pl.BlockSpec((tm, K), lambda t: (t, 0)), pl.BlockSpec((K, N), lambda t: (0, 0)),
                   pl.BlockSpec((1, N), lambda t: (0, 0))],
        out_shape=[jax.ShapeDtypeStruct((T, K), F32), jax.ShapeDtypeStruct((K, N), F32),
                   jax.ShapeDtypeStruct((1, N), F32)],
        compiler_params=_cparams(("arbitrary",)),
    )(a, dy, w)


def loss_fwd_bwd(x, fw, target):
    T, D = x.shape
    tm = min(T, 256)

    def body(x_ref, fw_ref, tg_ref, loss_ref, dx_ref, dfw_ref):
        @pl.when(pl.program_id(0) == 0)
        def _():
            loss_ref[...] = jnp.zeros_like(loss_ref)
            dfw_ref[...] = jnp.zeros_like(dfw_ref)

        xv = x_ref[...]
        w = fw_ref[...]
        err = _rms(xv, w) - tg_ref[...]
        row = jnp.sum(err * err, axis=-1, keepdims=True)
        loss_ref[...] += (0.5 / D) * jnp.sum(row, axis=0, keepdims=True)
        dx, dfw = _rms_bwd(xv, w, err * (1.0 / D))
        dx_ref[...] = dx
        dfw_ref[...] += dfw

    return pl.pallas_call(
        body, name="loss_fwd_bwd", grid=(T // tm,),
        in_specs=[pl.BlockSpec((tm, D), lambda t: (t, 0)), pl.BlockSpec((1, D), lambda t: (0, 0)),
                  pl.BlockSpec((tm, D), lambda t: (t, 0))],
        out_specs=[pl.BlockSpec((1, 1), lambda t: (0, 0)), pl.BlockSpec((tm, D), lambda t: (t, 0)),
                   pl.BlockSpec((1, D), lambda t: (0, 0))],
        out_shape=[jax.ShapeDtypeStruct((1, 1), F32), jax.ShapeDtypeStruct((T, D), F32),
                   jax.ShapeDtypeStruct((1, D), F32)],
        compiler_params=_cparams(("arbitrary",)),
    )(x, fw, target)


def _attn_masks(n, rows, blk):
    r = lax.broadcasted_iota(jnp.int32, (rows, 2 * blk), 0)
    jj = lax.broadcasted_iota(jnp.int32, (rows, 2 * blk), 1)
    dist = (r % blk) + blk - jj
    valid = (dist >= 0) & (dist < blk) & ((n > 0) | (jj >= blk))
    return dist.astype(F32), valid


def _attn_block(q, kcat, vcat, sink, slope, dist, valid):
    d = q.shape[-1]
    s = _dot_nt(q.astype(BF16), kcat.astype(BF16)) * (d ** -0.5)
    s = jnp.where(valid, s - slope * dist, -1e30)
    m = lax.stop_gradient(jnp.maximum(jnp.max(s, axis=-1, keepdims=True), sink))
    e = jnp.exp(s - m)
    p = e / (jnp.sum(e, axis=-1, keepdims=True) + jnp.exp(sink - m))
    return _dot(p.astype(BF16), vcat.astype(BF16))


def _attn_specs(G, blk, d):
    qs = pl.BlockSpec((G, blk, d), lambda h, n: (h, n, 0))
    kprev = pl.BlockSpec((1, blk, d), lambda h, n: (h, jnp.maximum(n - 1, 0), 0))
    kcur = pl.BlockSpec((1, blk, d), lambda h, n: (h, n, 0))
    rowp = pl.BlockSpec((G * blk, 1), lambda h, n: (h, 0))
    return qs, kprev, kcur, rowp


def attn_fwd(q, k, v, sink_rows, slope_rows):
    Hq, T, d = q.shape
    Hkv = k.shape[0]
    G = Hq // Hkv
    blk = ATTN_BLOCK
    nblk = T // blk

    def body(q_ref, kp_ref, kc_ref, vp_ref, vc_ref, sink_ref, slope_ref, o_ref):
        n = pl.program_id(1)
        dist, valid = _attn_masks(n, G * blk, blk)
        kcat = jnp.concatenate([kp_ref[0], kc_ref[0]], axis=0)
        vcat = jnp.concatenate([vp_ref[0], vc_ref[0]], axis=0)
        o = _attn_block(q_ref[...].reshape(G * blk, d), kcat, vcat, sink_ref[...], slope_ref[...], dist, valid)
        o_ref[...] = o.reshape(G, blk, d)

    qs, kprev, kcur, rowp = _attn_specs(G, blk, d)
    return pl.pallas_call(
        body, name="attn_fwd", grid=(Hkv, nblk),
        in_specs=[qs, kprev, kcur, kprev, kcur, rowp, rowp],
        out_specs=qs,
        out_shape=jax.ShapeDtypeStruct((Hq, T, d), F32),
        compiler_params=_cparams(("parallel", "parallel")),
    )(q, k, k, v, v, sink_rows, slope_rows)


def attn_bwd(q, k, v, sink_rows, slope_rows, do):
    Hq, T, d = q.shape
    Hkv = k.shape[0]
    G = Hq // Hkv
    blk = ATTN_BLOCK
    nblk = T // blk

    def body(q_ref, kp_ref, kc_ref, vp_ref, vc_ref, sink_ref, slope_ref, do_ref,
             dq_ref, dk_ref, dv_ref, dsink_ref):
        n = pl.program_id(1)

        @pl.when(n == 0)
        def _():
            dk_ref[...] = jnp.zeros_like(dk_ref)
            dv_ref[...] = jnp.zeros_like(dv_ref)
            dsink_ref[...] = jnp.zeros_like(dsink_ref)

        dist, valid = _attn_masks(n, G * blk, blk)
        kcat = jnp.concatenate([kp_ref[0], kc_ref[0]], axis=0)
        vcat = jnp.concatenate([vp_ref[0], vc_ref[0]], axis=0)
        fn = functools.partial(_attn_block, slope=slope_ref[...], dist=dist, valid=valid)
        _, vjp = jax.vjp(fn, q_ref[...].reshape(G * blk, d), kcat, vcat, sink_ref[...])
        dq, dkcat, dvcat, dsink = vjp(do_ref[...].reshape(G * blk, d))
        dq_ref[...] = dq.reshape(G, blk, d)
        dsink_ref[...] += dsink

        @pl.when(n == 0)
        def _():
            dk_ref[0, 0:blk, :] += dkcat[blk:]
            dv_ref[0, 0:blk, :] += dvcat[blk:]

        @pl.when(n > 0)
        def _():
            rows = pl.ds(pl.multiple_of((n - 1) * blk, blk), 2 * blk)
            dk_ref[0, rows, :] += dkcat
            dv_ref[0, rows, :] += dvcat

    qs, kprev, kcur, rowp = _attn_specs(G, blk, d)
    kvfull = pl.BlockSpec((1, T, d), lambda h, n: (h, 0, 0))
    return pl.pallas_call(
        body, name="attn_bwd", grid=(Hkv, nblk),
        in_specs=[qs, kprev, kcur, kprev, kcur, rowp, rowp, qs],
        out_specs=[qs, kvfull, kvfull, rowp],
        out_shape=[jax.ShapeDtypeStruct((Hq, T, d), F32), jax.ShapeDtypeStruct((Hkv, T, d), F32),
                   jax.ShapeDtypeStruct((Hkv, T, d), F32), jax.ShapeDtypeStruct((Hq * blk, 1), F32)],
        compiler_params=_cparams(("parallel", "arbitrary")),
    )(q, k, k, v, v, sink_rows, slope_rows, do)


def _bmm_nn(a, b):
    return lax.dot_general(a, b, (((2,), (1,)), ((0,), (0,))), precision=HI, preferred_element_type=F32)


def _bmm_nt(a, b):
    return lax.dot_general(a, b, (((2,), (2,)), ((0,), (0,))), precision=HI, preferred_element_type=F32)


def _bmm_tn(a, b):
    return lax.dot_general(a, b, (((1,), (1,)), ((0,), (0,))), precision=HI, preferred_element_type=F32)


def _dn_chunk(qc, kc, vc, zc, braw, araw, alog, dtb, nw, S):
    H, C, D = qc.shape
    row = lax.broadcasted_iota(jnp.int32, (H, C, C), 1)
    col = lax.broadcasted_iota(jnp.int32, (H, C, C), 2)
    causal = row >= col
    strict = row > col
    eye = (row == col).astype(F32)
    ltri = causal.astype(F32)
    ones = jnp.ones((H, C, C), F32)

    q = qc * lax.rsqrt(jnp.sum(qc * qc, axis=-1, keepdims=True) + EPS) * (D ** -0.5)
    k = kc * lax.rsqrt(jnp.sum(kc * kc, axis=-1, keepdims=True) + EPS)
    beta = _sigmoid(braw)
    g = -jnp.exp(alog) * _softplus(araw + dtb)
    a_col = _bmm_nn(ltri, jnp.broadcast_to(g, (H, C, C)))
    a_row = _bmm_nn(ones, eye * a_col)
    decay = jnp.where(causal, jnp.exp(jnp.where(causal, a_col - a_row, 0.0)), 0.0)
    kb = k * beta
    low = jnp.where(strict, _bmm_nt(kb, k) * decay, 0.0)
    e_col = jnp.exp(a_col)
    tinv = eye - low
    p = low
    for _ in range(5):
        p = _bmm_nn(p, p)
        tinv = tinv + _bmm_nn(tinv, p)
    u = _bmm_nn(tinv, vc * beta)
    w = _bmm_nn(tinv, kb * e_col)
    attn = _bmm_nt(q, k) * decay
    gl = a_col[:, C - 1:C, :]
    k_dec = k * jnp.exp(gl - a_col)
    v_new = u - _bmm_nn(w, S)
    o = _bmm_nn(q * e_col, S) + _bmm_nn(attn, v_new)
    s_new = S * jnp.exp(jnp.broadcast_to(gl, (H, D, D))) + _bmm_tn(k_dec, v_new)
    on = o * lax.rsqrt(jnp.mean(o * o, axis=-1, keepdims=True) + EPS) * nw
    return on * (zc * _sigmoid(zc)), s_new


def dn_fwd(q, k, v, z, braw, araw, alog, dtb, nw):
    H, T, D = q.shape
    C = DN_CHUNK
    N = T // C

    def body(q_ref, k_ref, v_ref, z_ref, b_ref, a_ref, alog_ref, dtb_ref, nw_ref, o_ref, sall_ref, s_ref):
        @pl.when(pl.program_id(0) == 0)
        def _():
            s_ref[...] = jnp.zeros_like(s_ref)

        s_in = s_ref[...]
        sall_ref[0] = s_in
        on, s_new = _dn_chunk(q_ref[...], k_ref[...], v_ref[...], z_ref[...], b_ref[...], a_ref[...],
                              alog_ref[...], dtb_ref[...], nw_ref[...], s_in)
        o_ref[...] = on
        s_ref[...] = s_new

    tok = pl.BlockSpec((H, C, D), lambda n: (0, n, 0))
    tok1 = pl.BlockSpec((H, C, 1), lambda n: (0, n, 0))
    par = pl.BlockSpec((H, 1, 1), lambda n: (0, 0, 0))
    return pl.pallas_call(
        body, name="dn_fwd", grid=(N,),
        in_specs=[tok, tok, tok, tok, tok1, tok1, par, par, pl.BlockSpec((1, 1, D), lambda n: (0, 0, 0))],
        out_specs=[tok, pl.BlockSpec((1, H, D, D), lambda n: (n, 0, 0, 0))],
        out_shape=[jax.ShapeDtypeStruct((H, T, D), F32), jax.ShapeDtypeStruct((N, H, D, D), F32)],
        scratch_shapes=[pltpu.VMEM((H, D, D), F32)],
        compiler_params=_cparams(("arbitrary",)),
    )(q, k, v, z, braw, araw, alog, dtb, nw)


def dn_bwd(q, k, v, z, braw, araw, alog, dtb, nw, sall, do):
    H, T, D = q.shape
    C = DN_CHUNK
    N = T // C

    def body(q_ref, k_ref, v_ref, z_ref, b_ref, a_ref, alog_ref, dtb_ref, nw_ref, sall_ref, do_ref,
             dq_ref, dk_ref, dv_ref, dz_ref, db_ref, da_ref, dalog_ref, ddtb_ref, dnw_ref, ds_ref):
        @pl.when(pl.program_id(0) == 0)
        def _():
            ds_ref[...] = jnp.zeros_like(ds_ref)
            dalog_ref[...] = jnp.zeros_like(dalog_ref)
            ddtb_ref[...] = jnp.zeros_like(ddtb_ref)
            dnw_ref[...] = jnp.zeros_like(dnw_ref)

        args = (q_ref[...], k_ref[...], v_ref[...], z_ref[...], b_ref[...], a_ref[...],
                alog_ref[...], dtb_ref[...], nw_ref[...], sall_ref[0])
        _, vjp = jax.vjp(_dn_chunk, *args)
        dq, dk, dv, dz, db, da, dalog, ddtb, dnw, ds = vjp((do_ref[...], ds_ref[...]))
        dq_ref[...] = dq
        dk_ref[...] = dk
        dv_ref[...] = dv
        dz_ref[...] = dz
        db_ref[...] = db
        da_ref[...] = da
        dalog_ref[...] += dalog
        ddtb_ref[...] += ddtb
        dnw_ref[...] += dnw
        ds_ref[...] = ds

    rev = lambda i: (0, N - 1 - i, 0)
    tok = pl.BlockSpec((H, C, D), rev)
    tok1 = pl.BlockSpec((H, C, 1), rev)
    par = pl.BlockSpec((H, 1, 1), lambda i: (0, 0, 0))
    nws = pl.BlockSpec((1, 1, D), lambda i: (0, 0, 0))
    return pl.pallas_call(
        body, name="dn_bwd", grid=(N,),
        in_specs=[tok, tok, tok, tok, tok1, tok1, par, par, nws,
                  pl.BlockSpec((1, H, D, D), lambda i: (N - 1 - i, 0, 0, 0)), tok],
        out_specs=[tok, tok, tok, tok, tok1, tok1, par, par, nws],
        out_shape=[jax.ShapeDtypeStruct((H, T, D), F32)] * 4 + [jax.ShapeDtypeStruct((H, T, 1), F32)] * 2
        + [jax.ShapeDtypeStruct((H, 1, 1), F32)] * 2 + [jax.ShapeDtypeStruct((1, 1, D), F32)],
        scratch_shapes=[pltpu.VMEM((H, D, D), F32)],
        compiler_params=_cparams(("arbitrary",)),
    )(q, k, v, z, braw, araw, alog, dtb, nw, sall, do)


def _conv_taps(buf_ref, w, width, halo, tm):
    acc = w[0:1, :] * buf_ref[pl.ds(halo - (width - 1), tm), :]
    for kk in range(1, width):
        acc = acc + w[kk:kk + 1, :] * buf_ref[pl.ds(halo - (width - 1) + kk, tm), :]
    return acc


def _conv_taps_bwd(dbuf_ref, w, width, tm):
    acc = w[0:1, :] * dbuf_ref[pl.ds(width - 1, tm), :]
    for kk in range(1, width):
        acc = acc + w[kk:kk + 1, :] * dbuf_ref[pl.ds(width - 1 - kk, tm), :]
    return acc


def _conv_dw_acc(dw_ref, dout, buf_ref, width, halo, tm):
    for kk in range(width):
        dw_ref[pl.ds(kk, 1), :] += jnp.sum(dout * buf_ref[pl.ds(halo - (width - 1) + kk, tm), :],
                                            axis=0, keepdims=True)


DNC_HALO = 8
DNC_COLS = 768


def dnconv_fwd(proj, w):
    T = proj.shape[0]
    tm = min(T, 256)
    hb = tm // DNC_HALO

    def body(x_ref, h_ref, w_ref, o_ref, buf_ref):
        i = pl.program_id(0)
        buf_ref[0:DNC_HALO, :] = jnp.where(i > 0, h_ref[...], 0.0)
        buf_ref[DNC_HALO:, :] = x_ref[...]
        acc = _conv_taps(buf_ref, w_ref[...], DN_CONV, DNC_HALO, tm)
        o_ref[...] = acc * _sigmoid(acc)

    return pl.pallas_call(
        body, name="dnconv_fwd", grid=(T // tm, 2),
        in_specs=[pl.BlockSpec((tm, DNC_COLS), lambda i, c: (i, 1 + c)),
                  pl.BlockSpec((DNC_HALO, DNC_COLS), lambda i, c: (jnp.maximum(i * hb - 1, 0), 1 + c)),
                  pl.BlockSpec((DN_CONV, DNC_COLS), lambda i, c: (0, c))],
        out_specs=pl.BlockSpec((tm, DNC_COLS), lambda i, c: (i, c)),
        out_shape=jax.ShapeDtypeStruct((T, QKV_B), F32),
        scratch_shapes=[pltpu.VMEM((DNC_HALO + tm, DNC_COLS), F32)],
        compiler_params=_cparams(("parallel", "parallel")),
    )(proj, proj, w)


def dnconv_bwd(proj, w, dout):
    T = proj.shape[0]
    tm = min(T, 256)
    nt = T // tm
    hb = tm // DNC_HALO

    def body(x_ref, h_ref, w_ref, do_ref, dx_ref, dw_ref, buf_ref, dbuf_ref):
        r = pl.program_id(1)
        i = nt - 1 - r

        @pl.when(r == 0)
        def _():
            dw_ref[...] = jnp.zeros_like(dw_ref)
            dbuf_ref[tm:, :] = jnp.zeros((DNC_HALO, DNC_COLS), F32)

        buf_ref[0:DNC_HALO, :] = jnp.where(i > 0, h_ref[...], 0.0)
        buf_ref[DNC_HALO:, :] = x_ref[...]
        wv = w_ref[...]
        acc = _conv_taps(buf_ref, wv, DN_CONV, DNC_HALO, tm)
        sg = _sigmoid(acc)
        dacc = do_ref[...] * (sg * (1.0 + acc * (1.0 - sg)))
        dbuf_ref[0:tm, :] = dacc
        dx_ref[...] = _conv_taps_bwd(dbuf_ref, wv, DN_CONV, tm)
        _conv_dw_acc(dw_ref, dacc, buf_ref, DN_CONV, DNC_HALO, tm)
        dbuf_ref[tm:, :] = dacc[0:DNC_HALO, :]

    return pl.pallas_call(
        body, name="dnconv_bwd", grid=(2, nt),
        in_specs=[pl.BlockSpec((tm, DNC_COLS), lambda c, r: (nt - 1 - r, 1 + c)),
                  pl.BlockSpec((DNC_HALO, DNC_COLS), lambda c, r: (jnp.maximum((nt - 1 - r) * hb - 1, 0), 1 + c)),
                  pl.BlockSpec((DN_CONV, DNC_COLS), lambda c, r: (0, c)),
                  pl.BlockSpec((tm, DNC_COLS), lambda c, r: (nt - 1 - r, c))],
        out_specs=[pl.BlockSpec((tm, DNC_COLS), lambda c, r: (nt - 1 - r, c)),
                   pl.BlockSpec((DN_CONV, DNC_COLS), lambda c, r: (0, c))],
        out_shape=[jax.ShapeDtypeStruct((T, QKV_B), F32), jax.ShapeDtypeStruct((DN_CONV, QKV_B), F32)],
        scratch_shapes=[pltpu.VMEM((DNC_HALO + tm, DNC_COLS), F32), pltpu.VMEM((tm + DNC_HALO, DNC_COLS), F32)],
        compiler_params=_cparams(("parallel", "arbitrary")),
    )(proj, proj, w, dout)


CV_HALO = 32


def _cv_post(cv, lnw, lnb):
    mu = jnp.mean(cv, axis=-1, keepdims=True)
    xc = cv - mu
    y = xc * lax.rsqrt(jnp.mean(xc * xc, axis=-1, keepdims=True) + EPS) * lnw + lnb
    return y * _sigmoid(y)


def cv_fwd(ab, w, bdw, lnw, lnb):
    T = ab.shape[0]
    D = ab.shape[1] // 2
    tm = min(T, 256)
    hb = tm // CV_HALO

    def body(a_ref, b_ref, ah_ref, bh_ref, w_ref, bdw_ref, lnw_ref, lnb_ref, o_ref, buf_ref):
        i = pl.program_id(0)
        buf_ref[0:CV_HALO, :] = jnp.where(i > 0, ah_ref[...] * _sigmoid(bh_ref[...]), 0.0)
        buf_ref[CV_HALO:, :] = a_ref[...] * _sigmoid(b_ref[...])
        cv = _conv_taps(buf_ref, w_ref[...], CONV_WIDTH, CV_HALO, tm) + bdw_ref[...]
        o_ref[...] = _cv_post(cv, lnw_ref[...], lnb_ref[...])

    halo = lambda c: pl.BlockSpec((CV_HALO, D), lambda i: (jnp.maximum(i * hb - 1, 0), c))
    vec = pl.BlockSpec((1, D), lambda i: (0, 0))
    return pl.pallas_call(
        body, name="cv_fwd", grid=(T // tm,),
        in_specs=[pl.BlockSpec((tm, D), lambda i: (i, 0)), pl.BlockSpec((tm, D), lambda i: (i, 1)),
                  halo(0), halo(1), pl.BlockSpec((CONV_WIDTH, D), lambda i: (0, 0)), vec, vec, vec],
        out_specs=pl.BlockSpec((tm, D), lambda i: (i, 0)),
        out_shape=jax.ShapeDtypeStruct((T, D), F32),
        scratch_shapes=[pltpu.VMEM((CV_HALO + tm, D), F32)],
        compiler_params=_cparams(("parallel",)),
    )(ab, ab, ab, ab, w, bdw, lnw, lnb)


def cv_bwd(ab, w, bdw, lnw, lnb, dout):
    T = ab.shape[0]
    D = ab.shape[1] // 2
    tm = min(T, 256)
    nt = T // tm
    hb = tm // CV_HALO

    def body(a_ref, b_ref, ah_ref, bh_ref, w_ref, bdw_ref, lnw_ref, lnb_ref, do_ref,
             da_ref, db_ref, dw_ref, dbdw_ref, dlnw_ref, dlnb_ref, buf_ref, dbuf_ref):
        r = pl.program_id(0)
        i = nt - 1 - r

        @pl.when(r == 0)
        def _():
            dw_ref[...] = jnp.zeros_like(dw_ref)
            dbdw_ref[...] = jnp.zeros_like(dbdw_ref)
            dlnw_ref[...] = jnp.zeros_like(dlnw_ref)
            dlnb_ref[...] = jnp.zeros_like(dlnb_ref)
            dbuf_ref[tm:, :] = jnp.zeros((CV_HALO, D), F32)

        a = a_ref[...]
        sb = _sigmoid(b_ref[...])
        buf_ref[0:CV_HALO, :] = jnp.where(i > 0, ah_ref[...] * _sigmoid(bh_ref[...]), 0.0)
        buf_ref[CV_HALO:, :] = a * sb
        wv = w_ref[...]
        cv = _conv_taps(buf_ref, wv, CONV_WIDTH, CV_HALO, tm) + bdw_ref[...]
        _, vjp = jax.vjp(_cv_post, cv, lnw_ref[...], lnb_ref[...])
        dcv, dlnw, dlnb = vjp(do_ref[...])
        dlnw_ref[...] += dlnw
        dlnb_ref[...] += dlnb
        dbdw_ref[...] += jnp.sum(dcv, axis=0, keepdims=True)
        dbuf_ref[0:tm, :] = dcv
        du = _conv_taps_bwd(dbuf_ref, wv, CONV_WIDTH, tm)
        _conv_dw_acc(dw_ref, dcv, buf_ref, CONV_WIDTH, CV_HALO, tm)
        dbuf_ref[tm:, :] = dcv[0:CV_HALO, :]
        da_ref[...] = du * sb
        db_ref[...] = du * a * sb * (1.0 - sb)

    tile = lambda c: pl.BlockSpec((tm, D), lambda r: (nt - 1 - r, c))
    halo = lambda c: pl.BlockSpec((CV_HALO, D), lambda r: (jnp.maximum((nt - 1 - r) * hb - 1, 0), c))
    vec = pl.BlockSpec((1, D), lambda r: (0, 0))
    wsp = pl.BlockSpec((CONV_WIDTH, D), lambda r: (0, 0))
    da, db, dw, dbdw, dlnw, dlnb = pl.pallas_call(
        body, name="cv_bwd", grid=(nt,),
        in_specs=[tile(0), tile(1), halo(0), halo(1), wsp, vec, vec, vec, tile(0)],
        out_specs=[tile(0), tile(0), wsp, vec, vec, vec],
        out_shape=[jax.ShapeDtypeStruct((T, D), F32), jax.ShapeDtypeStruct((T, D), F32),
                   jax.ShapeDtypeStruct((CONV_WIDTH, D), F32)] + [jax.ShapeDtypeStruct((1, D), F32)] * 3,
        scratch_shapes=[pltpu.VMEM((CV_HALO + tm, D), F32), pltpu.VMEM((tm + CV_HALO, D), F32)],
        compiler_params=_cparams(("arbitrary",)),
    )(ab, ab, ab, ab, w, bdw, lnw, lnb, dout)
    return jnp.concatenate([da, db], axis=1), dw, dbdw, dlnw, dlnb


def adamw(w, m, v, slots):
    L, R, C = w.shape
    tr = max([d for d in range(16, 257, 16) if R % d == 0], default=R)
    c1 =1.0 / (1.0 - ADAM_B1 ** ADAM_STEP)
    c2 = 1.0 / (1.0 - ADAM_B2 ** ADAM_STEP)

    def body(w_ref, m_ref, v_ref, s_ref, g_ref, d_ref, nm_ref, nv_ref):
        g = s_ref[0].astype(F32)
        for j in range(1, N_DEV):
            g = g + s_ref[j].astype(F32)
        nm = ADAM_B1 * m_ref[...] + (1.0 - ADAM_B1) * g
        nv = ADAM_B2 * v_ref[...] + (1.0 - ADAM_B2) * (g * g)
        g_ref[...] = g
        nm_ref[...] = nm
        nv_ref[...] = nv
        d_ref[...] = -ADAM_LR * ((nm * c1) / (jnp.sqrt(nv * c2) + ADAM_EPS) + ADAM_WD * w_ref[...])

    blk = pl.BlockSpec((1, tr, C), lambda l, r: (l, r, 0))
    return pl.pallas_call(
        body, name="adamw", grid=(L, R // tr),
        in_specs=[blk, blk, blk, pl.BlockSpec((N_DEV, 1, tr, C), lambda l, r: (0, l, r, 0))],
        out_specs=[blk, blk, blk, blk],
        out_shape=[jax.ShapeDtypeStruct((L, R, C), F32)] * 4,
        compiler_params=_cparams(("parallel", "parallel")),
    )(w, m, v, slots)


def _position():
    return lax.axis_index("x"), lax.axis_index("y"), lax.axis_index("c")


def all_gather(shards):
    n = len(shards)

    def body(*refs):
        ins, outs = refs[:n], refs[n:2 * n]
        send_sems, recv_sems, local_sems = refs[2 * n:]
        x, y, c = _position()
        sibling = (x, y, 1 - c)
        chips = [(1 - x, y), (x, 1 - y), (1 - x, 1 - y)]
        idx = lambda px, py, pc: 4 * px + 2 * py + pc
        me = idx(x, y, c)

        def copy(a, k, slot, to, src=None):
            dst = outs[a].at[slot]
            return pltpu.make_async_remote_copy(
                src_ref=dst if src is None else src, dst_ref=dst,
                send_sem=send_sems.at[a, k], recv_sem=recv_sems.at[a, k], device_id=to, device_id_type=MESH)

        mine = [pltpu.make_async_copy(ins[a], outs[a].at[me], local_sems.at[a]) for a in range(n)]
        for cp in mine:
            cp.start()
        first = []
        for a in range(n):
            first.append(copy(a, 0, me, sibling, src=ins[a]))
            first += [copy(a, 1 + j, me, (*chip, c), src=ins[a]) for j, chip in enumerate(chips)]
        for cp in first:
            cp.start()
        passed = []
        for j, chip in enumerate(chips):
            for a in range(n):
                copy(a, 1 + j, idx(*chip, c), (x, y, c)).wait_recv()
                cp = copy(a, 4 + j, idx(*chip, c), sibling)
                cp.start()
                passed.append(cp)
        for a in range(n):
            copy(a, 0, idx(x, y, 1 - c), (x, y, c)).wait_recv()
            for j, chip in enumerate(chips):
                copy(a, 4 + j, idx(*chip, 1 - c), (x, y, c)).wait_recv()
        for cp in first + passed:
            cp.wait_send()
        for cp in mine:
            cp.wait()

    hbm = pl.BlockSpec(memory_space=pl.ANY)
    return pl.pallas_call(
        body, name="all_gather",
        in_specs=[hbm] * n, out_specs=[hbm] * n,
        out_shape=[jax.ShapeDtypeStruct((N_DEV,) + s.shape, s.dtype) for s in shards],
        scratch_shapes=[pltpu.SemaphoreType.DMA((n, 7)), pltpu.SemaphoreType.DMA((n, 7)),
                        pltpu.SemaphoreType.DMA((n,))],
    )(*shards)


def reduce_scatter_slots(blocks):
    n = len(blocks)

    def body(*refs):
        ins, outs = refs[:n], refs[n:2 * n]
        send_sems, recv_sems, local_sems = refs[2 * n:]
        x, y, c = _position()
        me = 4 * x + 2 * y + c

        def peer(r):
            return (x ^ ((r >> 2) & 1), y ^ ((r >> 1) & 1), c ^ (r & 1))

        def copy(a, r):
            px, py, pc = peer(r)
            return pltpu.make_async_remote_copy(
                src_ref=ins[a].at[4 * px + 2 * py + pc], dst_ref=outs[a].at[me],
                send_sem=send_sems.at[a, r - 1], recv_sem=recv_sems.at[a, r - 1],
                device_id=(px, py, pc), device_id_type=MESH)

        def landed(a, r):
            px, py, pc = peer(r)
            return pltpu.make_async_remote_copy(
                src_ref=ins[a].at[me], dst_ref=outs[a].at[4 * px + 2 * py + pc],
                send_sem=send_sems.at[a, r - 1], recv_sem=recv_sems.at[a, r - 1],
                device_id=(x, y, c), device_id_type=MESH)

        mine = [pltpu.make_async_copy(ins[a].at[me], outs[a].at[me], local_sems.at[a]) for a in range(n)]
        for cp in mine:
            cp.start()
        sent = [copy(a, r) for a in range(n) for r in range(1, N_DEV)]
        for cp in sent:
            cp.start()
        for a in range(n):
            for r in range(1, N_DEV):
                landed(a, r).wait_recv()
        for cp in sent:
            cp.wait_send()
        for cp in mine:
            cp.wait()

    hbm = pl.BlockSpec(memory_space=pl.ANY)
    return pl.pallas_call(
        body, name="reduce_scatter_slots",
        in_specs=[hbm] * n, out_specs=[hbm] * n,
        out_shape=[jax.ShapeDtypeStruct(b.shape, b.dtype) for b in blocks],
        scratch_shapes=[pltpu.SemaphoreType.DMA((n, 7)), pltpu.SemaphoreType.DMA((n, 7)),
                        pltpu.SemaphoreType.DMA((n,))],
    )(*blocks)


def _unshard(g, axis):
    g = jnp.moveaxis(g, 0, axis)
    s = g.shape
    return g.reshape(s[:axis] + (s[axis] * s[axis + 1],) + s[axis + 2:])


def _to_blocks(full, axis):
    s = full.shape
    g = full.reshape(s[:axis] + (N_DEV, s[axis] // N_DEV) + s[axis + 1:])
    return jnp.moveaxis(g, axis, 0)


def _heads(a, h):
    T = a.shape[0]
    return a.reshape(T, h, a.shape[1] // h).transpose(1, 0, 2)


def _unheads(a):
    h, T, d = a.shape
    return a.transpose(1, 0, 2).reshape(T, h * d)


SMALL = (("norm_w", 2), ("dn_conv_w", 2), ("conv_b_pw1", 1), ("conv_w_dw", 2), ("conv_b_dw", 1),
         ("conv_ln_w", 1), ("conv_ln_b", 1), ("conv_b_pw2", 1),
         ("attn_sinks", None), ("dn_a_log", None), ("dn_dt_bias", None), ("dn_norm_w", None), ("final_norm_w", None))
LANES = 128


def _pack(parts):
    flat = jnp.concatenate([p.reshape(-1) for p in parts])
    pad = (-flat.shape[0]) % LANES
    return jnp.pad(flat, (0, pad))


def _unpack(flat, shapes):
    out, off = [], 0
    for s in shapes:
        n = int(np.prod(s))
        out.append(flat[off:off + n].reshape(s))
        off += n
    return out


def kernel(x, norm_w, ffn_w_gate, ffn_w_up, ffn_w_down, mix_w_in, dn_conv_w, attn_sinks, dn_a_log, dn_dt_bias, dn_norm_w, mix_w_out, conv_w_pw1, conv_b_pw1, conv_w_dw, conv_b_dw, conv_ln_w, conv_ln_b, conv_w_pw2, conv_b_pw2, final_norm_w, loss_target, m_norm_w, m_ffn_w_gate, m_ffn_w_up, m_ffn_w_down, m_mix_w_in, m_dn_conv_w, m_attn_sinks, m_dn_a_log, m_dn_dt_bias, m_dn_norm_w, m_mix_w_out, m_conv_w_pw1, m_conv_b_pw1, m_conv_w_dw, m_conv_b_dw, m_conv_ln_w, m_conv_ln_b, m_conv_w_pw2, m_conv_b_pw2, m_final_norm_w, v_norm_w, v_ffn_w_gate, v_ffn_w_up, v_ffn_w_down, v_mix_w_in, v_dn_conv_w, v_attn_sinks, v_dn_a_log, v_dn_dt_bias, v_dn_norm_w, v_mix_w_out, v_conv_w_pw1, v_conv_b_pw1, v_conv_w_dw, v_conv_b_dw, v_conv_ln_w, v_conv_ln_b, v_conv_w_pw2, v_conv_b_pw2, v_final_norm_w):
    W = dict(norm_w=norm_w, ffn_w_gate=ffn_w_gate, ffn_w_up=ffn_w_up, ffn_w_down=ffn_w_down, mix_w_in=mix_w_in,
             dn_conv_w=dn_conv_w, attn_sinks=attn_sinks, dn_a_log=dn_a_log, dn_dt_bias=dn_dt_bias,
             dn_norm_w=dn_norm_w, mix_w_out=mix_w_out, conv_w_pw1=conv_w_pw1, conv_b_pw1=conv_b_pw1,
             conv_w_dw=conv_w_dw, conv_b_dw=conv_b_dw, conv_ln_w=conv_ln_w, conv_ln_b=conv_ln_b,
             conv_w_pw2=conv_w_pw2, conv_b_pw2=conv_b_pw2, final_norm_w=final_norm_w)
    M = dict(norm_w=m_norm_w, ffn_w_gate=m_ffn_w_gate, ffn_w_up=m_ffn_w_up, ffn_w_down=m_ffn_w_down,
             mix_w_in=m_mix_w_in, dn_conv_w=m_dn_conv_w, attn_sinks=m_attn_sinks, dn_a_log=m_dn_a_log,
             dn_dt_bias=m_dn_dt_bias, dn_norm_w=m_dn_norm_w, mix_w_out=m_mix_w_out, conv_w_pw1=m_conv_w_pw1,
             conv_b_pw1=m_conv_b_pw1, conv_w_dw=m_conv_w_dw, conv_b_dw=m_conv_b_dw, conv_ln_w=m_conv_ln_w,
             conv_ln_b=m_conv_ln_b, conv_w_pw2=m_conv_w_pw2, conv_b_pw2=m_conv_b_pw2, final_norm_w=m_final_norm_w)
    V = dict(norm_w=v_norm_w, ffn_w_gate=v_ffn_w_gate, ffn_w_up=v_ffn_w_up, ffn_w_down=v_ffn_w_down,
             mix_w_in=v_mix_w_in, dn_conv_w=v_dn_conv_w, attn_sinks=v_attn_sinks, dn_a_log=v_dn_a_log,
             dn_dt_bias=v_dn_dt_bias, dn_norm_w=v_dn_norm_w, mix_w_out=v_mix_w_out, conv_w_pw1=v_conv_w_pw1,
             conv_b_pw1=v_conv_b_pw1, conv_w_dw=v_conv_w_dw, conv_b_dw=v_conv_b_dw, conv_ln_w=v_conv_ln_w,
             conv_ln_b=v_conv_ln_b, conv_w_pw2=v_conv_w_pw2, conv_b_pw2=v_conv_b_pw2, final_norm_w=v_final_norm_w)

    T, D = x.shape[1], x.shape[2]
    xs = x[0]
    F8 = ffn_w_gate.shape[-1]
    n_ffn = DEPTH * 2

    big = ("ffn_w_gate", "ffn_w_up", "ffn_w_down", "mix_w_in", "mix_w_out", "conv_w_pw1", "conv_w_pw2")
    shard3 = {k: W[k].reshape((-1,) + W[k].shape[-2:]) for k in big}
    small_sharded = [(k, ax) for k, ax in SMALL if ax is not None]
    small_pack = _pack([W[k] for k, _ in small_sharded])[None, :]
    gathered = all_gather([shard3[k].astype(BF16) for k in big] + [small_pack])
    G = dict(zip(big, gathered[:-1]))
    small_full = {}
    for (k, ax), parts in zip(small_sharded,
                              zip(*[_unpack(gathered[-1][s, 0], [W[k].shape for k, _ in small_sharded])
                                    for s in range(N_DEV)])):
        small_full[k] = _unshard(jnp.stack(parts), ax)
    nw_full = small_full["norm_w"]

    wg = [G["ffn_w_gate"][:, i] for i in range(n_ffn)]
    wu = [G["ffn_w_up"][:, i] for i in range(n_ffn)]
    wd = [G["ffn_w_down"][:, i] for i in range(n_ffn)]
    w_in = [jnp.pad(_unshard(G["mix_w_in"][:, e], 1), ((0, 0), (0, IN_COLS_PAD - IN_COLS))) for e in range(2)]
    w_out = [G["mix_w_out"][:, e].reshape(D, D) for e in range(2)]
    w_pw1 = [_unshard(G["conv_w_pw1"][:, e], 1) for e in range(2)]
    w_pw2 = [G["conv_w_pw2"][:, e].reshape(D, D) for e in range(2)]
    zero_in = jnp.zeros((1, IN_COLS_PAD), F32)
    zero_d = jnp.zeros((1, D), F32)
    slope_rows = jnp.asarray(np.repeat(2.0 ** (-8.0 * np.arange(1, ATTN_HEADS + 1) / ATTN_HEADS), ATTN_BLOCK)
                             .astype(np.float32)[:, None])

    saved = []
    h = xs
    for l in range(DEPTH):
        e = l // 2
        st = {"x0": h}
        h = ffn_fwd(h, nw_full[l, 0][None], wg[2 * l], wu[2 * l], wd[2 * l])
        st["x1"] = h
        if l % 2 == 0:
            proj = rmslin_fwd(h, nw_full[l, 1][None], w_in[e], zero_in)
            qkvc = dnconv_fwd(proj, small_full["dn_conv_w"][e])
            st["qa"] = _heads(proj[:, :Q_A], ATTN_HEADS)
            st["ka"] = _heads(proj[:, Q_A:Q_A + KV_A], ATTN_KV_HEADS)
            st["va"] = _heads(proj[:, Q_A + KV_A:OFF_QKVB], ATTN_KV_HEADS)
            st["qb"] = _heads(qkvc[:, :512], DN_HEADS)
            st["kb"] = _heads(qkvc[:, 512:1024], DN_HEADS)
            st["vb"] = _heads(qkvc[:, 1024:], DN_HEADS)
            st["zb"] = _heads(proj[:, OFF_Z:OFF_BETA], DN_HEADS)
            st["braw"] = proj[:, OFF_BETA:OFF_A].T[:, :, None]
            st["araw"] = proj[:, OFF_A:IN_COLS].T[:, :, None]
            st["sink_rows"] = jnp.repeat(attn_sinks[e], ATTN_BLOCK)[:, None]
            st["alog"] = dn_a_log[e].reshape(DN_HEADS, 1, 1)
            st["dtb"] = dn_dt_bias[e].reshape(DN_HEADS, 1, 1)
            st["dnw"] = dn_norm_w[e].reshape(1, 1, DN_D)
            att = attn_fwd(st["qa"], st["ka"], st["va"], st["sink_rows"], slope_rows)
            og, st["sall"] = dn_fwd(st["qb"], st["kb"], st["vb"], st["zb"], st["braw"], st["araw"],
                                    st["alog"], st["dtb"], st["dnw"])
            st["proj"] = proj
            st["mix"] = jnp.concatenate([_unheads(att), _unheads(og)], axis=1)
            h = lin_fwd(h, st["mix"], w_out[e], zero_d)
        else:
            st["ab"] = rmslin_fwd(h, nw_full[l, 1][None], w_pw1[e], small_full["conv_b_pw1"][e][None])
            st["act"] = cv_fwd(st["ab"], small_full["conv_w_dw"][e], small_full["conv_b_dw"][e][None],
                               small_full["conv_ln_w"][e][None], small_full["conv_ln_b"][e][None])
            h = lin_fwd(h, st["act"], w_pw2[e], small_full["conv_b_pw2"][e][None])
        st["x2"] = h
        h = ffn_fwd(h, nw_full[l, 2][None], wg[2 * l + 1], wu[2 * l + 1], wd[2 * l + 1])
        saved.append(st)

    loss_part, dh, dfinal = loss_fwd_bwd(h, final_norm_w[None], loss_target[0])
    loss = lax.psum(loss_part[0, 0], ("x", "y", "c"))

    dwg, dwu, dwd = [None] * n_ffn, [None] * n_ffn, [None] * n_ffn
    d_norm = [[None] * 3 for _ in range(DEPTH)]
    d_in, d_out, d_pw1, d_pw2 = [None] * 2, [None] * 2, [None] * 2, [None] * 2
    d_small = {k: [None, None] for k in ("dn_conv_w", "conv_b_pw1", "conv_w_dw", "conv_b_dw", "conv_ln_w",
                                         "conv_ln_b", "conv_b_pw2", "attn_sinks", "dn_a_log", "dn_dt_bias",
                                         "dn_norm_w")}
    for l in reversed(range(DEPTH)):
        e = l // 2
        st = saved[l]
        dh, dwg[2 * l + 1], dwu[2 * l + 1], dwd[2 * l + 1], d_norm[l][2] = ffn_bwd(
            st["x2"], dh, nw_full[l, 2][None], wg[2 * l + 1], wu[2 * l + 1], wd[2 * l + 1])
        if l % 2 == 0:
            dmix, d_out[e], _ = lin_bwd(st["mix"], dh, w_out[e])
            dqb, dkb, dvb, dzb, dbraw, daraw, dalog, ddtb, ddnw = dn_bwd(
                st["qb"], st["kb"], st["vb"], st["zb"], st["braw"], st["araw"], st["alog"], st["dtb"], st["dnw"],
                st["sall"], _heads(dmix[:, Q_A:], DN_HEADS))
            dqa, dka, dva, dsink = attn_bwd(st["qa"], st["ka"], st["va"], st["sink_rows"], slope_rows,
                                            _heads(dmix[:, :Q_A], ATTN_HEADS))
            dqkv, d_small["dn_conv_w"][e] = dnconv_bwd(
                st["proj"], small_full["dn_conv_w"][e],
                jnp.concatenate([_unheads(dqb), _unheads(dkb), _unheads(dvb)], axis=1))
            dproj = jnp.concatenate(
                [_unheads(dqa), _unheads(dka), _unheads(dva), dqkv, _unheads(dzb), dbraw[:, :, 0].T,
                 daraw[:, :, 0].T, jnp.zeros((T, IN_COLS_PAD - IN_COLS), F32)], axis=1)
            dh, d_in[e], _, d_norm[l][1] = rmslin_bwd(st["x1"], dh, dproj, nw_full[l, 1][None], w_in[e])
            d_small["attn_sinks"][e] = jnp.sum(dsink.reshape(ATTN_HEADS, ATTN_BLOCK), axis=1)
            d_small["dn_a_log"][e] = dalog.reshape(DN_HEADS)
            d_small["dn_dt_bias"][e] = ddtb.reshape(DN_HEADS)
            d_small["dn_norm_w"][e] = ddnw.reshape(DN_D)
        else:
            dact, d_pw2[e], d_small["conv_b_pw2"][e] = lin_bwd(st["act"], dh, w_pw2[e])
            dab, d_small["conv_w_dw"][e], d_small["conv_b_dw"][e], d_small["conv_ln_w"][e], d_small["conv_ln_b"][e] = cv_bwd(
                st["ab"], small_full["conv_w_dw"][e], small_full["conv_b_dw"][e][None],
                small_full["conv_ln_w"][e][None], small_full["conv_ln_b"][e][None], dact)
            dh, d_pw1[e], d_small["conv_b_pw1"][e], d_norm[l][1] = rmslin_bwd(
                st["x1"], dh, dab, nw_full[l, 1][None], w_pw1[e])
        dh, dwg[2 * l], dwu[2 * l], dwd[2 * l], d_norm[l][0] = ffn_bwd(
            st["x0"], dh, nw_full[l, 0][None], wg[2 * l], wu[2 * l], wd[2 * l])
    grad_x = dh[None]

    send = {
        "ffn_w_gate": jnp.stack(dwg, axis=1), "ffn_w_up": jnp.stack(dwu, axis=1), "ffn_w_down": jnp.stack(dwd, axis=1),
        "mix_w_in": jnp.stack([_to_blocks(d_in[e][:, :IN_COLS], 1) for e in range(2)], axis=1).astype(BF16),
        "mix_w_out": jnp.stack([d_out[e].reshape(N_DEV, D // N_DEV, D) for e in range(2)], axis=1).astype(BF16),
        "conv_w_pw1": jnp.stack([_to_blocks(d_pw1[e], 1) for e in range(2)], axis=1).astype(BF16),
        "conv_w_pw2": jnp.stack([d_pw2[e].reshape(N_DEV, D // N_DEV, D) for e in range(2)], axis=1).astype(BF16),
    }
    full_small = {"norm_w": jnp.stack([jnp.concatenate(r, axis=0) for r in d_norm]),
                  "final_norm_w": dfinal[0]}
    for k, pair in d_small.items():
        full_small[k] = jnp.stack([p.reshape(W[k].shape[1:-1] + (-1,)) if SMALL_AXIS[k] is not None
                                   else p for p in pair])
    rows = []
    for s in range(N_DEV):
        parts = [_to_blocks(full_small[k], ax)[s] if ax is not None else full_small[k] for k, ax in SMALL]
        rows.append(_pack(parts))
    send_small = jnp.stack(rows)[:, None, None, :]
    slots = reduce_scatter_slots([send[k] for k in big] + [send_small])
    S = dict(zip(big, slots[:-1]))

    res = {}
    for k in big:
        outs = adamw(shard3[k], M[k].reshape(shard3[k].shape), V[k].reshape(shard3[k].shape), S[k])
        res[k] = [o.reshape(W[k].shape) for o in outs]
    pk = lambda d: _pack([d[k] for k, _ in SMALL])[None, None, :]
    outs = adamw(pk(W), pk(M), pk(V), slots[-1])
    shapes = [W[k].shape for k, _ in SMALL]
    unp = [_unpack(o[0, 0], shapes) for o in outs]
    for i, (k, _) in enumerate(SMALL):
        res[k] = [u[i] for u in unp]

    order = ("norm_w", "ffn_w_gate", "ffn_w_up", "ffn_w_down", "mix_w_in", "dn_conv_w", "attn_sinks", "dn_a_log",
             "dn_dt_bias", "dn_norm_w", "mix_w_out", "conv_w_pw1", "conv_b_pw1", "conv_w_dw", "conv_b_dw",
             "conv_ln_w", "conv_ln_b", "conv_w_pw2", "conv_b_pw2", "final_norm_w")
    return (loss, grad_x, *[res[k][0] for k in order], *[res[k][1] for k in order],
            *[res[k][2] for k in order], *[res[k][3] for k in order])


SMALL_AXIS = dict(SMALL)
```

```python
import functools

import numpy as np
import jax
import jax.numpy as jnp
from jax import lax
from jax.experimental import pallas as pl
from jax.experimental.pallas import tpu as pltpu

F32 = jnp.float32
BF16 = jnp.bfloat16
HI = lax.Precision.HIGHEST
EPS = 1e-6
N_DEV = 8
V7X_VMEM_LIMIT = 48 * 2**20
MESH = pl.DeviceIdType.MESH
LANES = 128

DEPTH = 4
D_MODEL = 1024
ATTN_HEADS, ATTN_KV_HEADS, HEAD_DIM, ATTN_BLOCK = 8, 2, 64, 128
DN_HEADS, DN_D, DN_CHUNK, DN_CONV = 8, 64, 64, 4
CONV_WIDTH = 31
Q_A, KV_A, QKV_B, V_B = 512, 128, 1536, 512
IN_COLS = 2832
IN_COLS_PAD = 3072
OFF_QKVB = Q_A + 2 * KV_A
OFF_Z = OFF_QKVB + QKV_B
OFF_BETA = OFF_Z + V_B
OFF_A = OFF_BETA + DN_HEADS

ADAM_LR, ADAM_B1, ADAM_B2, ADAM_EPS, ADAM_WD, ADAM_STEP = 0.001, 0.9, 0.999, 1e-08, 0.01, 10


def _cparams(sem):
    return pltpu.CompilerParams(dimension_semantics=sem, vmem_limit_bytes=V7X_VMEM_LIMIT)


def _sigmoid(x):
    return 1.0 / (1.0 + jnp.exp(-x))


def _softplus(x):
    return jnp.maximum(x, 0.0) + jnp.log(1.0 + jnp.exp(-jnp.abs(x)))


def _dot(a, b):
    return jnp.dot(a, b, preferred_element_type=F32)


def _dot_nt(a, b):
    return lax.dot_general(a, b, (((1,), (1,)), ((), ())), preferred_element_type=F32)


def _dot_tn(a, b):
    return lax.dot_general(a, b, (((0,), (0,)), ((), ())), preferred_element_type=F32)


def _rms(x, w):
    return x * lax.rsqrt(jnp.mean(x * x, axis=-1, keepdims=True) + EPS) * w


def _rms_bwd(x, w, dxn):
    r = lax.rsqrt(jnp.mean(x * x, axis=-1, keepdims=True) + EPS)
    xh = x * r
    dxh = dxn * w
    dx = r * (dxh - xh * jnp.mean(dxh * xh, axis=-1, keepdims=True))
    return dx, jnp.sum(dxn * xh, axis=0, keepdims=True)


def _position():
    return lax.axis_index("x"), lax.axis_index("y"), lax.axis_index("c")


def _dev_index(px, py, pc):
    return 4 * px + 2 * py + pc


def _rcopy(src, dst, send_sem, recv_sem, to):
    return pltpu.make_async_remote_copy(src_ref=src, dst_ref=dst, send_sem=send_sem, recv_sem=recv_sem,
                                        device_id=to, device_id_type=MESH)


def _ag_start(srcs, outs, send, recv, local):
    x, y, c = _position()
    me = _dev_index(x, y, c)
    chips = [(1 - x, y), (x, 1 - y), (1 - x, 1 - y)]
    for a, (src, out) in enumerate(zip(srcs, outs)):
        pltpu.make_async_copy(src, out.at[me], local.at[a]).start()
        _rcopy(src, out.at[me], send.at[a, 0], recv.at[a, 0], (x, y, 1 - c)).start()
        for j, chip in enumerate(chips):
            _rcopy(src, out.at[me], send.at[a, 1 + j], recv.at[a, 1 + j], (*chip, c)).start()


def _ag_finish(srcs, outs, send, recv, local):
    x, y, c = _position()
    me = _dev_index(x, y, c)
    sibling = (x, y, 1 - c)
    chips = [(1 - x, y), (x, 1 - y), (1 - x, 1 - y)]
    for j, chip in enumerate(chips):
        for a, out in enumerate(outs):
            blk = out.at[_dev_index(*chip, c)]
            _rcopy(blk, blk, send.at[a, 1 + j], recv.at[a, 1 + j], (x, y, c)).wait_recv()
            _rcopy(blk, blk, send.at[a, 4 + j], recv.at[a, 4 + j], sibling).start()
    for a, (src, out) in enumerate(zip(srcs, outs)):
        blk = out.at[_dev_index(x, y, 1 - c)]
        _rcopy(blk, blk, send.at[a, 0], recv.at[a, 0], (x, y, c)).wait_recv()
        for j, chip in enumerate(chips):
            blk = out.at[_dev_index(*chip, 1 - c)]
            _rcopy(blk, blk, send.at[a, 4 + j], recv.at[a, 4 + j], (x, y, c)).wait_recv()
        for k in range(N_DEV - 1):
            _rcopy(out.at[me], out.at[me], send.at[a, k], recv.at[a, k], (x, y, c)).wait_send()
        pltpu.make_async_copy(src, out.at[me], local.at[a]).wait()


def _rs_peer(r):
    x, y, c = _position()
    return x ^ ((r >> 2) & 1), y ^ ((r >> 1) & 1), c ^ (r & 1)


def _rs_start(ins, outs, send, recv, local):
    me = _dev_index(*_position())
    for a, (src, out) in enumerate(zip(ins, outs)):
        pltpu.make_async_copy(src.at[me], out.at[me], local.at[a]).start()
        for r in range(1, N_DEV):
            p = _rs_peer(r)
            _rcopy(src.at[_dev_index(*p)], out.at[me], send.at[a, r - 1], recv.at[a, r - 1], p).start()


def _rs_finish(ins, outs, send, recv, local):
    pos = _position()
    me = _dev_index(*pos)
    for a, (src, out) in enumerate(zip(ins, outs)):
        for r in range(1, N_DEV):
            blk = out.at[_dev_index(*_rs_peer(r))]
            _rcopy(blk, blk, send.at[a, r - 1], recv.at[a, r - 1], pos).wait_recv()
        for r in range(1, N_DEV):
            _rcopy(src.at[me], out.at[me], send.at[a, r - 1], recv.at[a, r - 1], pos).wait_send()
        pltpu.make_async_copy(src.at[me], out.at[me], local.at[a]).wait()


def _pcall(body, args, *, name, grid, in_specs, out_specs, out_shape, sem, scratch_shapes=(), ag=(), rs=()):
    na, nr = len(ag), len(rs)
    if na + nr == 0:
        outs = pl.pallas_call(body, name=name, grid=grid, in_specs=in_specs, out_specs=out_specs,
                              out_shape=out_shape, scratch_shapes=list(scratch_shapes),
                              compiler_params=_cparams(sem))(*args)
        return list(outs), [], []
    n_in, n_out, n_scr = len(in_specs), len(out_specs), len(scratch_shapes)
    ag_idx = [i for _, i in ag]

    def wrapped(*refs):
        cin, refs = refs[:n_in], refs[n_in:]
        ag_in, refs = refs[:na], refs[na:]
        rs_in, refs = refs[:nr], refs[nr:]
        cout, refs = refs[:n_out], refs[n_out:]
        ag_out, refs = refs[:na], refs[na:]
        rs_out, refs = refs[:nr], refs[nr:]
        cscr, sems = refs[:n_scr], refs[n_scr:]
        ag_src = [r if i is None else r.at[i] for r, i in zip(ag_in, ag_idx)]
        ids = [pl.program_id(d) for d in range(len(grid))]
        first = functools.reduce(jnp.logical_and, [i == 0 for i in ids])
        last = functools.reduce(jnp.logical_and, [i == g - 1 for i, g in zip(ids, grid)])

        @pl.when(first)
        def _():
            if na:
                _ag_start(ag_src, ag_out, *sems[:3])
            if nr:
                _rs_start(rs_in, rs_out, *sems[-3:])

        body(*cin, *cout, *cscr)

        @pl.when(last)
        def _():
            if na:
                _ag_finish(ag_src, ag_out, *sems[:3])
            if nr:
                _rs_finish(rs_in, rs_out, *sems[-3:])

    hbm = pl.BlockSpec(memory_space=pl.ANY)
    sem_shapes = []
    for n in (na, nr):
        if n:
            sem_shapes += [pltpu.SemaphoreType.DMA((n, N_DEV - 1)), pltpu.SemaphoreType.DMA((n, N_DEV - 1)),
                           pltpu.SemaphoreType.DMA((n,))]
    outs = pl.pallas_call(
        wrapped, name=name, grid=grid,
        in_specs=list(in_specs) + [hbm] * (na + nr),
        out_specs=list(out_specs) + [hbm] * (na + nr),
        out_shape=list(out_shape)
        + [jax.ShapeDtypeStruct((N_DEV,) + a.shape[-2:], a.dtype) for a, _ in ag]
        + [jax.ShapeDtypeStruct(b.shape, b.dtype) for b in rs],
        scratch_shapes=list(scratch_shapes) + sem_shapes,
        compiler_params=_cparams(sem),
    )(*args, *[a for a, _ in ag], *rs)
    return list(outs[:n_out]), list(outs[n_out:n_out + na]), list(outs[n_out + na:])


def exchange(ag=(), rs=()):
    def body(o_ref):
        o_ref[...] = jnp.zeros_like(o_ref)

    _, gathered, slots = _pcall(body, (), name="exchange", grid=(1,), in_specs=[],
                                out_specs=[pl.BlockSpec((8, LANES), lambda i: (0, 0))],
                                out_shape=[jax.ShapeDtypeStruct((8, LANES), F32)], sem=("arbitrary",), ag=ag, rs=rs)
    return gathered, slots


def ffn_fwd(x, nw, wg, wu, wd, ag=()):
    T, D = x.shape
    J, _, F = wg.shape
    tm = min(T, 512)

    def body(x_ref, nw_ref, wg_ref, wu_ref, wd_ref, o_ref, xn_ref, acc_ref):
        j = pl.program_id(1)

        @pl.when(j == 0)
        def _():
            xn_ref[...] = _rms(x_ref[...], nw_ref[...]).astype(BF16)
            acc_ref[...] = jnp.zeros_like(acc_ref)

        xn = xn_ref[...]
        g = _dot(xn, wg_ref[0])
        u = _dot(xn, wu_ref[0])
        h = (g * _sigmoid(g) * u).astype(BF16)
        acc_ref[...] += _dot(h, wd_ref[0])

        @pl.when(j == J - 1)
        def _():
            o_ref[...] = x_ref[...] + 0.5 * acc_ref[...]

    (out,), gathered, _ = _pcall(
        body, (x, nw, wg, wu, wd), name="ffn_fwd", grid=(T // tm, J),
        in_specs=[pl.BlockSpec((tm, D), lambda t, j: (t, 0)),
                  pl.BlockSpec((1, D), lambda t, j: (0, 0)),
                  pl.BlockSpec((1, D, F), lambda t, j: (j, 0, 0)),
                  pl.BlockSpec((1, D, F), lambda t, j: (j, 0, 0)),
                  pl.BlockSpec((1, F, D), lambda t, j: (j, 0, 0))],
        out_specs=[pl.BlockSpec((tm, D), lambda t, j: (t, 0))],
        out_shape=[jax.ShapeDtypeStruct((T, D), F32)],
        scratch_shapes=[pltpu.VMEM((tm, D), BF16), pltpu.VMEM((tm, D), F32)],
        sem=("arbitrary", "arbitrary"), ag=ag)
    return out, gathered


def ffn_bwd(x, dy, nw, wg, wu, wd, rs=()):
    T, D = x.shape
    J, _, F = wg.shape
    tm = min(T, 256)
    nt = T // tm

    def body(x_ref, dy_ref, nw_ref, wg_ref, wu_ref, wd_ref,
             dx_ref, dwg_ref, dwu_ref, dwd_ref, dnw_ref,
             xn_ref, dyh_ref, dxn_ref, awg_ref, awu_ref, awd_ref):
        j = pl.program_id(0)
        t = pl.program_id(1)
        rows = pl.ds(pl.multiple_of(t * tm, tm), tm)

        @pl.when(j == 0)
        def _():
            xn_ref[rows, :] = _rms(x_ref[...], nw_ref[...]).astype(BF16)
            dyh_ref[rows, :] = (0.5 * dy_ref[...]).astype(BF16)
            dxn_ref[rows, :] = jnp.zeros((tm, D), F32)

        @pl.when((j == 0) & (t == 0))
        def _():
            dnw_ref[...] = jnp.zeros_like(dnw_ref)

        @pl.when(t == 0)
        def _():
            awg_ref[...] = jnp.zeros_like(awg_ref)
            awu_ref[...] = jnp.zeros_like(awu_ref)
            awd_ref[...] = jnp.zeros_like(awd_ref)

        xn = xn_ref[rows, :]
        dyh = dyh_ref[rows, :]
        g = _dot(xn, wg_ref[0])
        u = _dot(xn, wu_ref[0])
        sg = _sigmoid(g)
        s = g * sg
        h = (s * u).astype(BF16)
        dh = _dot_nt(dyh, wd_ref[0])
        du = (dh * s).astype(BF16)
        dg = (dh * u * (sg * (1.0 + g * (1.0 - sg)))).astype(BF16)
        awd_ref[...] += _dot_tn(h, dyh)
        awg_ref[...] += _dot_tn(xn, dg)
        awu_ref[...] += _dot_tn(xn, du)
        dxn_ref[rows, :] += _dot_nt(dg, wg_ref[0]) + _dot_nt(du, wu_ref[0])

        @pl.when(t == nt - 1)
        def _():
            dwg_ref[0] = awg_ref[...].astype(BF16)
            dwu_ref[0] = awu_ref[...].astype(BF16)
            dwd_ref[0] = awd_ref[...].astype(BF16)

        @pl.when(j == J - 1)
        def _():
            dx, dnw = _rms_bwd(x_ref[...], nw_ref[...], dxn_ref[rows, :])
            dx_ref[...] = dy_ref[...] + dx
            dnw_ref[...] += dnw

    ends = lambda j, t: (jnp.where((j == 0) | (j == J - 1), t, 0), 0)
    last = lambda j, t: (jnp.where(j == J - 1, t, 0), 0)
    outs, _, slots = _pcall(
        body, (x, dy, nw, wg, wu, wd), name="ffn_bwd", grid=(J, nt),
        in_specs=[pl.BlockSpec((tm, D), ends), pl.BlockSpec((tm, D), ends),
                  pl.BlockSpec((1, D), lambda j, t: (0, 0)),
                  pl.BlockSpec((1, D, F), lambda j, t: (j, 0, 0)),
                  pl.BlockSpec((1, D, F), lambda j, t: (j, 0, 0)),
                  pl.BlockSpec((1, F, D), lambda j, t: (j, 0, 0))],
        out_specs=[pl.BlockSpec((tm, D), last),
                   pl.BlockSpec((1, D, F), lambda j, t: (j, 0, 0)),
                   pl.BlockSpec((1, D, F), lambda j, t: (j, 0, 0)),
                   pl.BlockSpec((1, F, D), lambda j, t: (j, 0, 0)),
                   pl.BlockSpec((1, D), lambda j, t: (0, 0))],
        out_shape=[jax.ShapeDtypeStruct((T, D), F32),
                   jax.ShapeDtypeStruct((J, D, F), BF16), jax.ShapeDtypeStruct((J, D, F), BF16),
                   jax.ShapeDtypeStruct((J, F, D), BF16), jax.ShapeDtypeStruct((1, D), F32)],
        scratch_shapes=[pltpu.VMEM((T, D), BF16), pltpu.VMEM((T, D), BF16), pltpu.VMEM((T, D), F32),
                        pltpu.VMEM((D, F), F32), pltpu.VMEM((D, F), F32), pltpu.VMEM((F, D), F32)],
        sem=("arbitrary", "arbitrary"), rs=rs)
    return outs, slots


def rmslin_fwd(x, nw, w, b):
    T, D = x.shape
    N = w.shape[1]
    tm = min(T, 256)

    def body(x_ref, nw_ref, w_ref, b_ref, o_ref):
        xn = _rms(x_ref[...], nw_ref[...]).astype(BF16)
        o_ref[...] = _dot(xn, w_ref[...]) + b_ref[...]

    return pl.pallas_call(
        body, name="rmslin_fwd", grid=(T // tm,),
        in_specs=[pl.BlockSpec((tm, D), lambda t: (t, 0)), pl.BlockSpec((1, D), lambda t: (0, 0)),
                  pl.BlockSpec((D, N), lambda t: (0, 0)), pl.BlockSpec((1, N), lambda t: (0, 0))],
        out_specs=pl.BlockSpec((tm, N), lambda t: (t, 0)),
        out_shape=jax.ShapeDtypeStruct((T, N), F32),
        compiler_params=_cparams(("parallel",)),
    )(x, nw, w, b)


def rmslin_bwd(x, dres, dproj, nw, w):
    T, D = x.shape
    N = w.shape[1]
    nb = 1024
    nc = N // nb
    tm = min(T, 256)
    nt = T // tm

    def body(x_ref, dres_ref, dp_ref, nw_ref, w_ref, dx_ref, dw_ref, db_ref, dnw_ref, xn_ref, dxn_ref):
        c = pl.program_id(0)
        t = pl.program_id(1)
        rows = pl.ds(pl.multiple_of(t * tm, tm), tm)

        @pl.when(c == 0)
        def _():
            xn_ref[rows, :] = _rms(x_ref[...], nw_ref[...]).astype(BF16)
            dxn_ref[rows, :] = jnp.zeros((tm, D), F32)

        @pl.when((c == 0) & (t == 0))
        def _():
            dnw_ref[...] = jnp.zeros_like(dnw_ref)

        @pl.when(t == 0)
        def _():
            dw_ref[...] = jnp.zeros_like(dw_ref)
            db_ref[...] = jnp.zeros_like(db_ref)

        dpf = dp_ref[...]
        dp = dpf.astype(BF16)
        dw_ref[...] += _dot_tn(xn_ref[rows, :], dp)
        db_ref[...] += jnp.sum(dpf, axis=0, keepdims=True)
        dxn_ref[rows, :] += _dot_nt(dp, w_ref[...])

        @pl.when(c == nc - 1)
        def _():
            dx, dnw = _rms_bwd(x_ref[...], nw_ref[...], dxn_ref[rows, :])
            dx_ref[...] = dres_ref[...] + dx
            dnw_ref[...] += dnw

    ends = lambda c, t: (jnp.where((c == 0) | (c == nc - 1), t, 0), 0)
    last = lambda c, t: (jnp.where(c == nc - 1, t, 0), 0)
    return pl.pallas_call(
        body, name="rmslin_bwd", grid=(nc, nt),
        in_specs=[pl.BlockSpec((tm, D), ends), pl.BlockSpec((tm, D), last),
                  pl.BlockSpec((tm, nb), lambda c, t: (t, c)),
                  pl.BlockSpec((1, D), lambda c, t: (0, 0)),
                  pl.BlockSpec((D, nb), lambda c, t: (0, c))],
        out_specs=[pl.BlockSpec((tm, D), last),
                   pl.BlockSpec((D, nb), lambda c, t: (0, c)),
                   pl.BlockSpec((1, nb), lambda c, t: (0, c)),
                   pl.BlockSpec((1, D), lambda c, t: (0, 0))],
        out_shape=[jax.ShapeDtypeStruct((T, D), F32), jax.ShapeDtypeStruct((D, N), F32),
                   jax.ShapeDtypeStruct((1, N), F32), jax.ShapeDtypeStruct((1, D), F32)],
        scratch_shapes=[pltpu.VMEM((T, D), BF16), pltpu.VMEM((T, D), F32)],
        compiler_params=_cparams(("arbitrary", "arbitrary")),
    )(x, dres, dproj, nw, w)


def lin_fwd(res, a, w, b):
    T, K = a.shape
    N = w.shape[1]
    tm = min(T, 512)

    def body(res_ref, a_ref, w_ref, b_ref, o_ref):
        o_ref[...] = res_ref[...] + _dot(a_ref[...].astype(BF16), w_ref[...]) + b_ref[...]

    return pl.pallas_call(
        body, name="lin_fwd", grid=(T // tm,),
        in_specs=[pl.BlockSpec((tm, N), lambda t: (t, 0)), pl.BlockSpec((tm, K), lambda t: (t, 0)),
                  pl.BlockSpec((K, N), lambda t: (0, 0)), pl.BlockSpec((1, N), lambda t: (0, 0))],
        out_specs=pl.BlockSpec((tm, N), lambda t: (t, 0)),
        out_shape=jax.ShapeDtypeStruct((T, N), F32),
        compiler_params=_cparams(("parallel",)),
    )(res, a, w, b)


def lin_bwd(a, dy, w):
    T, K = a.shape
    N = w.shape[1]
    tm = min(T, 256)

    def body(a_ref, dy_ref, w_ref, da_ref, dw_ref, db_ref):
        @pl.when(pl.program_id(0) == 0)
        def _():
            dw_ref[...] = jnp.zeros_like(dw_ref)
            db_ref[...] = jnp.zeros_like(db_ref)

        dyf = dy_ref[...]
        dyb = dyf.astype(BF16)
        da_ref[...] = _dot_nt(dyb, w_ref[...])
        dw_ref[...] += _dot_tn(a_ref[...].astype(BF16), dyb)
        db_ref[...] += jnp.sum(dyf, axis=0, keepdims=True)

    return pl.pallas_call(
        body, name="lin_bwd", grid=(T // tm,),
        in_specs=[pl.BlockSpec((tm, K), lambda t: (t, 0)), pl.BlockSpec((tm, N), lambda t: (t, 0)),
                  pl.BlockSpec((K, N), lambda t: (0, 0))],
        out_specs=[pl.BlockSpec((tm, K), lambda t: (t, 0)), pl.BlockSpec((K, N), lambda t: (0, 0)),
                   pl.BlockSpec((1, N), lambda t: (0, 0))],
        out_shape=[jax.ShapeDtypeStruct((T, K), F32), jax.ShapeDtypeStruct((K, N), F32),
                   jax.ShapeDtypeStruct((1, N), F32)],
        compiler_params=_cparams(("arbitrary",)),
    )(a, dy, w)


def loss_fwd_bwd(x, fw, target):
    T, D = x.shape
    tm = min(T, 256)

    def body(x_ref, fw_ref, tg_ref, loss_ref, dx_ref, dfw_ref):
        @pl.when(pl.program_id(0) == 0)
        def _():
            loss_ref[...] = jnp.zeros_like(loss_ref)
            dfw_ref[...] = jnp.zeros_like(dfw_ref)

        xv = x_ref[...]
        w = fw_ref[...]
        err = _rms(xv, w) - tg_ref[...]
        row = jnp.sum(err * err, axis=-1, keepdims=True)
        loss_ref[...] += (0.5 / D) * jnp.sum(row, axis=0, keepdims=True)
        dx, dfw = _rms_bwd(xv, w, err * (1.0 / D))
        dx_ref[...] = dx
        dfw_ref[...] += dfw

    return pl.pallas_call(
        body, name="loss_fwd_bwd", grid=(T // tm,),
        in_specs=[pl.BlockSpec((tm, D), lambda t: (t, 0)), pl.BlockSpec((1, D), lambda t: (0, 0)),
                  pl.BlockSpec((tm, D), lambda t: (t, 0))],
        out_specs=[pl.BlockSpec((1, 1), lambda t: (0, 0)), pl.BlockSpec((tm, D), lambda t: (t, 0)),
                   pl.BlockSpec((1, D), lambda t: (0, 0))],
        out_shape=[jax.ShapeDtypeStruct((1, 1), F32), jax.ShapeDtypeStruct((T, D), F32),
                   jax.ShapeDtypeStruct((1, D), F32)],
        compiler_params=_cparams(("arbitrary",)),
    )(x, fw, target)


def _attn_masks(n, rows, blk):
    r = lax.broadcasted_iota(jnp.int32, (rows, 2 * blk), 0)
    jj = lax.broadcasted_iota(jnp.int32, (rows, 2 * blk), 1)
    dist = (r % blk) + blk - jj
    valid = (dist >= 0) & (dist < blk) & ((n > 0) | (jj >= blk))
    return dist.astype(F32), valid


def _attn_block(q, kcat, vcat, sink, slope, dist, valid):
    d = q.shape[-1]
    s = _dot_nt(q.astype(BF16), kcat.astype(BF16)) * (d ** -0.5)
    s = jnp.where(valid, s - slope * dist, -1e30)
    m = lax.stop_gradient(jnp.maximum(jnp.max(s, axis=-1, keepdims=True), sink))
    e = jnp.exp(s - m)
    p = e / (jnp.sum(e, axis=-1, keepdims=True) + jnp.exp(sink - m))
    return _dot(p.astype(BF16), vcat.astype(BF16))


def _attn_specs(G, blk, d):
    qs = pl.BlockSpec((G, blk, d), lambda h, n: (h, n, 0))
    kprev = pl.BlockSpec((1, blk, d), lambda h, n: (h, jnp.maximum(n - 1, 0), 0))
    kcur = pl.BlockSpec((1, blk, d), lambda h, n: (h, n, 0))
    rowp = pl.BlockSpec((G * blk, 1), lambda h, n: (h, 0))
    return qs, kprev, kcur, rowp


def attn_fwd(q, k, v, sink_rows, slope_rows):
    Hq, T, d = q.shape
    Hkv = k.shape[0]
    G = Hq // Hkv
    blk = ATTN_BLOCK
    nblk = T // blk

    def body(q_ref, kp_ref, kc_ref, vp_ref, vc_ref, sink_ref, slope_ref, o_ref):
        n = pl.program_id(1)
        dist, valid = _attn_masks(n, G * blk, blk)
        kcat = jnp.concatenate([kp_ref[0], kc_ref[0]], axis=0)
        vcat = jnp.concatenate([vp_ref[0], vc_ref[0]], axis=0)
        o = _attn_block(q_ref[...].reshape(G * blk, d), kcat, vcat, sink_ref[...], slope_ref[...], dist, valid)
        o_ref[...] = o.reshape(G, blk, d)

    qs, kprev, kcur, rowp = _attn_specs(G, blk, d)
    return pl.pallas_call(
        body, name="attn_fwd", grid=(Hkv, nblk),
        in_specs=[qs, kprev, kcur, kprev, kcur, rowp, rowp],
        out_specs=qs,
        out_shape=jax.ShapeDtypeStruct((Hq, T, d), F32),
        compiler_params=_cparams(("parallel", "parallel")),
    )(q, k, k, v, v, sink_rows, slope_rows)


def attn_bwd(q, k, v, sink_rows, slope_rows, do):
    Hq, T, d = q.shape
    Hkv = k.shape[0]
    G = Hq // Hkv
    blk = ATTN_BLOCK
    nblk = T // blk

    def body(q_ref, kp_ref, kc_ref, vp_ref, vc_ref, sink_ref, slope_ref, do_ref,
             dq_ref, dk_ref, dv_ref, dsink_ref):
        n = pl.program_id(1)

        @pl.when(n == 0)
        def _():
            dk_ref[...] = jnp.zeros_like(dk_ref)
            dv_ref[...] = jnp.zeros_like(dv_ref)
            dsink_ref[...] = jnp.zeros_like(dsink_ref)

        dist, valid = _attn_masks(n, G * blk, blk)
        kcat = jnp.concatenate([kp_ref[0], kc_ref[0]], axis=0)
        vcat = jnp.concatenate([vp_ref[0], vc_ref[0]], axis=0)
        fn = functools.partial(_attn_block, slope=slope_ref[...], dist=dist, valid=valid)
        _, vjp = jax.vjp(fn, q_ref[...].reshape(G * blk, d), kcat, vcat, sink_ref[...])
        dq, dkcat, dvcat, dsink = vjp(do_ref[...].reshape(G * blk, d))
        dq_ref[...] = dq.reshape(G, blk, d)
        dsink_ref[...] += dsink

        @pl.when(n == 0)
        def _():
            dk_ref[0, 0:blk, :] += dkcat[blk:]
            dv_ref[0, 0:blk, :] += dvcat[blk:]

        @pl.when(n > 0)
        def _():
            rows = pl.ds(pl.multiple_of((n - 1) * blk, blk), 2 * blk)
            dk_ref[0, rows, :] += dkcat
            dv_ref[0, rows, :] += dvcat

    qs, kprev, kcur, rowp = _attn_specs(G, blk, d)
    kvfull = pl.BlockSpec((1, T, d), lambda h, n: (h, 0, 0))
    return pl.pallas_call(
        body, name="attn_bwd", grid=(Hkv, nblk),
        in_specs=[qs, kprev, kcur, kprev, kcur, rowp, rowp, qs],
        out_specs=[qs, kvfull, kvfull, rowp],
        out_shape=[jax.ShapeDtypeStruct((Hq, T, d), F32), jax.ShapeDtypeStruct((Hkv, T, d), F32),
                   jax.ShapeDtypeStruct((Hkv, T, d), F32), jax.ShapeDtypeStruct((Hq * blk, 1), F32)],
        compiler_params=_cparams(("parallel", "arbitrary")),
    )(q, k, k, v, v, sink_rows, slope_rows, do)


def _bmm_nn(a, b):
    return lax.dot_general(a, b, (((2,), (1,)), ((0,), (0,))), precision=HI, preferred_element_type=F32)


def _bmm_nt(a, b):
    return lax.dot_general(a, b, (((2,), (2,)), ((0,), (0,))), precision=HI, preferred_element_type=F32)


def _bmm_tn(a, b):
    return lax.dot_general(a, b, (((1,), (1,)), ((0,), (0,))), precision=HI, preferred_element_type=F32)


def _dn_chunk(qc, kc, vc, zc, braw, araw, alog, dtb, nw, S):
    H, C, D = qc.shape
    row = lax.broadcasted_iota(jnp.int32, (H, C, C), 1)
    col = lax.broadcasted_iota(jnp.int32, (H, C, C), 2)
    causal = row >= col
    strict = row > col
    eye = (row == col).astype(F32)
    ltri = causal.astype(F32)
    ones = jnp.ones((H, C, C), F32)

    q = qc * lax.rsqrt(jnp.sum(qc * qc, axis=-1, keepdims=True) + EPS) * (D ** -0.5)
    k = kc * lax.rsqrt(jnp.sum(kc * kc, axis=-1, keepdims=True) + EPS)
    beta = _sigmoid(braw)
    g = -jnp.exp(alog) * _softplus(araw + dtb)
    a_col = _bmm_nn(ltri, jnp.broadcast_to(g, (H, C, C)))
    a_row = _bmm_nn(ones, eye * a_col)
    decay = jnp.where(causal, jnp.exp(jnp.where(causal, a_col - a_row, 0.0)), 0.0)
    kb = k * beta
    low = jnp.where(strict, _bmm_nt(kb, k) * decay, 0.0)
    e_col = jnp.exp(a_col)
    tinv = eye - low
    p = low
    for _ in range(5):
        p = _bmm_nn(p, p)
        tinv = tinv + _bmm_nn(tinv, p)
    u = _bmm_nn(tinv, vc * beta)
    w = _bmm_nn(tinv, kb * e_col)
    attn = _bmm_nt(q, k) * decay
    gl = a_col[:, C - 1:C, :]
    k_dec = k * jnp.exp(gl - a_col)
    v_new = u - _bmm_nn(w, S)
    o = _bmm_nn(q * e_col, S) + _bmm_nn(attn, v_new)
    s_new = S * jnp.exp(jnp.broadcast_to(gl, (H, D, D))) + _bmm_tn(k_dec, v_new)
    on = o * lax.rsqrt(jnp.mean(o * o, axis=-1, keepdims=True) + EPS) * nw
    return on * (zc * _sigmoid(zc)), s_new


def dn_fwd(q, k, v, z, braw, araw, alog, dtb, nw, ag=()):
    H, T, D = q.shape
    C = DN_CHUNK
    N = T // C

    def body(q_ref, k_ref, v_ref, z_ref, b_ref, a_ref, alog_ref, dtb_ref, nw_ref, o_ref, sall_ref, s_ref):
        @pl.when(pl.program_id(0) == 0)
        def _():
            s_ref[...] = jnp.zeros_like(s_ref)

        s_in = s_ref[...]
        sall_ref[0] = s_in
        on, s_new = _dn_chunk(q_ref[...], k_ref[...], v_ref[...], z_ref[...], b_ref[...], a_ref[...],
                              alog_ref[...], dtb_ref[...], nw_ref[...], s_in)
        o_ref[...] = on
        s_ref[...] = s_new

    tok = pl.BlockSpec((H, C, D), lambda n: (0, n, 0))
    tok1 = pl.BlockSpec((H, C, 1), lambda n: (0, n, 0))
    par = pl.BlockSpec((H, 1, 1), lambda n: (0, 0, 0))
    outs, gathered, _ = _pcall(
        body, (q, k, v, z, braw, araw, alog, dtb, nw), name="dn_fwd", grid=(N,),
        in_specs=[tok, tok, tok, tok, tok1, tok1, par, par, pl.BlockSpec((1, 1, D), lambda n: (0, 0, 0))],
        out_specs=[tok, pl.BlockSpec((1, H, D, D), lambda n: (n, 0, 0, 0))],
        out_shape=[jax.ShapeDtypeStruct((H, T, D), F32), jax.ShapeDtypeStruct((N, H, D, D), F32)],
        scratch_shapes=[pltpu.VMEM((H, D, D), F32)], sem=("arbitrary",), ag=ag)
    return outs, gathered


def dn_bwd(q, k, v, z, braw, araw, alog, dtb, nw, sall, do, rs=()):
    H, T, D = q.shape
    C = DN_CHUNK
    N = T // C

    def body(q_ref, k_ref, v_ref, z_ref, b_ref, a_ref, alog_ref, dtb_ref, nw_ref, sall_ref, do_ref,
             dq_ref, dk_ref, dv_ref, dz_ref, db_ref, da_ref, dalog_ref, ddtb_ref, dnw_ref, ds_ref):
        @pl.when(pl.program_id(0) == 0)
        def _():
            ds_ref[...] = jnp.zeros_like(ds_ref)
            dalog_ref[...] = jnp.zeros_like(dalog_ref)
            ddtb_ref[...] = jnp.zeros_like(ddtb_ref)
            dnw_ref[...] = jnp.zeros_like(dnw_ref)

        args = (q_ref[...], k_ref[...], v_ref[...], z_ref[...], b_ref[...], a_ref[...],
                alog_ref[...], dtb_ref[...], nw_ref[...], sall_ref[0])
        _, vjp = jax.vjp(_dn_chunk, *args)
        dq, dk, dv, dz, db, da, dalog, ddtb, dnw, ds = vjp((do_ref[...], ds_ref[...]))
        dq_ref[...] = dq
        dk_ref[...] = dk
        dv_ref[...] = dv
        dz_ref[...] = dz
        db_ref[...] = db
        da_ref[...] = da
        dalog_ref[...] += dalog
        ddtb_ref[...] += ddtb
        dnw_ref[...] += dnw
        ds_ref[...] = ds

    rev = lambda i: (0, N - 1 - i, 0)
    tok = pl.BlockSpec((H, C, D), rev)
    tok1 = pl.BlockSpec((H, C, 1), rev)
    par = pl.BlockSpec((H, 1, 1), lambda i: (0, 0, 0))
    nws = pl.BlockSpec((1, 1, D), lambda i: (0, 0, 0))
    outs, _, slots = _pcall(
        body, (q, k, v, z, braw, araw, alog, dtb, nw, sall, do), name="dn_bwd", grid=(N,),
        in_specs=[tok, tok, tok, tok, tok1, tok1, par, par, nws,
                  pl.BlockSpec((1, H, D, D), lambda i: (N - 1 - i, 0, 0, 0)), tok],
        out_specs=[tok, tok, tok, tok, tok1, tok1, par, par, nws],
        out_shape=[jax.ShapeDtypeStruct((H, T, D), F32)] * 4 + [jax.ShapeDtypeStruct((H, T, 1), F32)] * 2
        + [jax.ShapeDtypeStruct((H, 1, 1), F32)] * 2 + [jax.ShapeDtypeStruct((1, 1, D), F32)],
        scratch_shapes=[pltpu.VMEM((H, D, D), F32)], sem=("arbitrary",), rs=rs)
    return outs, slots


def _conv_taps(buf_ref, w, width, halo, tm):
    acc = w[0:1, :] * buf_ref[pl.ds(halo - (width - 1), tm), :]
    for kk in range(1, width):
        acc = acc + w[kk:kk + 1, :] * buf_ref[pl.ds(halo - (width - 1) + kk, tm), :]
    return acc


def _conv_taps_bwd(dbuf_ref, w, width, tm):
    acc = w[0:1, :] * dbuf_ref[pl.ds(width - 1, tm), :]
    for kk in range(1, width):
        acc = acc + w[kk:kk + 1, :] * dbuf_ref[pl.ds(width - 1 - kk, tm), :]
    return acc


def _conv_dw_acc(dw_ref, dout, buf_ref, width, halo, tm):
    for kk in range(width):
        dw_ref[pl.ds(kk, 1), :] += jnp.sum(dout * buf_ref[pl.ds(halo - (width - 1) + kk, tm), :],
                                            axis=0, keepdims=True)


DNC_HALO = 8
DNC_COLS = 768


def dnconv_fwd(proj, w):
    T = proj.shape[0]
    tm = min(T, 256)
    hb = tm // DNC_HALO

    def body(x_ref, h_ref, w_ref, o_ref, buf_ref):
        i = pl.program_id(0)
        buf_ref[0:DNC_HALO, :] = jnp.where(i > 0, h_ref[...], 0.0)
        buf_ref[DNC_HALO:, :] = x_ref[...]
        acc = _conv_taps(buf_ref, w_ref[...], DN_CONV, DNC_HALO, tm)
        o_ref[...] = acc * _sigmoid(acc)

    return pl.pallas_call(
        body, name="dnconv_fwd", grid=(T // tm, 2),
        in_specs=[pl.BlockSpec((tm, DNC_COLS), lambda i, c: (i, 1 + c)),
                  pl.BlockSpec((DNC_HALO, DNC_COLS), lambda i, c: (jnp.maximum(i * hb - 1, 0), 1 + c)),
                  pl.BlockSpec((DN_CONV, DNC_COLS), lambda i, c: (0, c))],
        out_specs=pl.BlockSpec((tm, DNC_COLS), lambda i, c: (i, c)),
        out_shape=jax.ShapeDtypeStruct((T, QKV_B), F32),
        scratch_shapes=[pltpu.VMEM((DNC_HALO + tm, DNC_COLS), F32)],
        compiler_params=_cparams(("parallel", "parallel")),
    )(proj, proj, w)


def dnconv_bwd(proj, w, dout):
    T = proj.shape[0]
    tm = min(T, 256)
    nt = T // tm
    hb = tm // DNC_HALO

    def body(x_ref, h_ref, w_ref, do_ref, dx_ref, dw_ref, buf_ref, dbuf_ref):
        r = pl.program_id(1)
        i = nt - 1 - r

        @pl.when(r == 0)
        def _():
            dw_ref[...] = jnp.zeros_like(dw_ref)
            dbuf_ref[tm:, :] = jnp.zeros((DNC_HALO, DNC_COLS), F32)

        buf_ref[0:DNC_HALO, :] = jnp.where(i > 0, h_ref[...], 0.0)
        buf_ref[DNC_HALO:, :] = x_ref[...]
        wv = w_ref[...]
        acc = _conv_taps(buf_ref, wv, DN_CONV, DNC_HALO, tm)
        sg = _sigmoid(acc)
        dacc = do_ref[...] * (sg * (1.0 + acc * (1.0 - sg)))
        dbuf_ref[0:tm, :] = dacc
        dx_ref[...] = _conv_taps_bwd(dbuf_ref, wv, DN_CONV, tm)
        _conv_dw_acc(dw_ref, dacc, buf_ref, DN_CONV, DNC_HALO, tm)
        dbuf_ref[tm:, :] = dacc[0:DNC_HALO, :]

    return pl.pallas_call(
        body, name="dnconv_bwd", grid=(2, nt),
        in_specs=[pl.BlockSpec((tm, DNC_COLS), lambda c, r: (nt - 1 - r, 1 + c)),
                  pl.BlockSpec((DNC_HALO, DNC_COLS), lambda c, r: (jnp.maximum((nt - 1 - r) * hb - 1, 0), 1 + c)),
                  pl.BlockSpec((DN_CONV, DNC_COLS), lambda c, r: (0, c)),
                  pl.BlockSpec((tm, DNC_COLS), lambda c, r: (nt - 1 - r, c))],
        out_specs=[pl.BlockSpec((tm, DNC_COLS), lambda c, r: (nt - 1 - r, c)),
                   pl.BlockSpec((DN_CONV, DNC_COLS), lambda c, r: (0, c))],
        out_shape=[jax.ShapeDtypeStruct((T, QKV_B), F32), jax.ShapeDtypeStruct((DN_CONV, QKV_B), F32)],
        scratch_shapes=[pltpu.VMEM((DNC_HALO + tm, DNC_COLS), F32), pltpu.VMEM((tm + DNC_HALO, DNC_COLS), F32)],
        compiler_params=_cparams(("parallel", "arbitrary")),
    )(proj, proj, w, dout)


CV_HALO = 32


def _cv_post(cv, lnw, lnb):
    mu = jnp.mean(cv, axis=-1, keepdims=True)
    xc = cv - mu
    y = xc * lax.rsqrt(jnp.mean(xc * xc, axis=-1, keepdims=True) + EPS) * lnw + lnb
    return y * _sigmoid(y)


def cv_fwd(ab, w, bdw, lnw, lnb, ag=()):
    T = ab.shape[0]
    D = ab.shape[1] // 2
    tm = min(T, 256)
    hb = tm // CV_HALO

    def body(a_ref, b_ref, ah_ref, bh_ref, w_ref, bdw_ref, lnw_ref, lnb_ref, o_ref, buf_ref):
        i = pl.program_id(0)
        buf_ref[0:CV_HALO, :] = jnp.where(i > 0, ah_ref[...] * _sigmoid(bh_ref[...]), 0.0)
        buf_ref[CV_HALO:, :] = a_ref[...] * _sigmoid(b_ref[...])
        cv = _conv_taps(buf_ref, w_ref[...], CONV_WIDTH, CV_HALO, tm) + bdw_ref[...]
        o_ref[...] = _cv_post(cv, lnw_ref[...], lnb_ref[...])

    halo = lambda c: pl.BlockSpec((CV_HALO, D), lambda i: (jnp.maximum(i * hb - 1, 0), c))
    vec = pl.BlockSpec((1, D), lambda i: (0, 0))
    (out,), gathered, _ = _pcall(
        body, (ab, ab, ab, ab, w, bdw, lnw, lnb), name="cv_fwd", grid=(T // tm,),
        in_specs=[pl.BlockSpec((tm, D), lambda i: (i, 0)), pl.BlockSpec((tm, D), lambda i: (i, 1)),
                  halo(0), halo(1), pl.BlockSpec((CONV_WIDTH, D), lambda i: (0, 0)), vec, vec, vec],
        out_specs=[pl.BlockSpec((tm, D), lambda i: (i, 0))],
        out_shape=[jax.ShapeDtypeStruct((T, D), F32)],
        scratch_shapes=[pltpu.VMEM((CV_HALO + tm, D), F32)], sem=("arbitrary",), ag=ag)
    return out, gathered


def cv_bwd(ab, w, bdw, lnw, lnb, dout, rs=()):
    T = ab.shape[0]
    D = ab.shape[1] // 2
    tm = min(T, 256)
    nt = T // tm
    hb = tm // CV_HALO

    def body(a_ref, b_ref, ah_ref, bh_ref, w_ref, bdw_ref, lnw_ref, lnb_ref, do_ref,
             da_ref, db_ref, dw_ref, dbdw_ref, dlnw_ref, dlnb_ref, buf_ref, dbuf_ref):
        r = pl.program_id(0)
        i = nt - 1 - r

        @pl.when(r == 0)
        def _():
            dw_ref[...] = jnp.zeros_like(dw_ref)
            dbdw_ref[...] = jnp.zeros_like(dbdw_ref)
            dlnw_ref[...] = jnp.zeros_like(dlnw_ref)
            dlnb_ref[...] = jnp.zeros_like(dlnb_ref)
            dbuf_ref[tm:, :] = jnp.zeros((CV_HALO, D), F32)

        a = a_ref[...]
        sb = _sigmoid(b_ref[...])
        buf_ref[0:CV_HALO, :] = jnp.where(i > 0, ah_ref[...] * _sigmoid(bh_ref[...]), 0.0)
        buf_ref[CV_HALO:, :] = a * sb
        wv = w_ref[...]
        cv = _conv_taps(buf_ref, wv, CONV_WIDTH, CV_HALO, tm) + bdw_ref[...]
        _, vjp = jax.vjp(_cv_post, cv, lnw_ref[...], lnb_ref[...])
        dcv, dlnw, dlnb = vjp(do_ref[...])
        dlnw_ref[...] += dlnw
        dlnb_ref[...] += dlnb
        dbdw_ref[...] += jnp.sum(dcv, axis=0, keepdims=True)
        dbuf_ref[0:tm, :] = dcv
        du = _conv_taps_bwd(dbuf_ref, wv, CONV_WIDTH, tm)
        _conv_dw_acc(dw_ref, dcv, buf_ref, CONV_WIDTH, CV_HALO, tm)
        dbuf_ref[tm:, :] = dcv[0:CV_HALO, :]
        da_ref[...] = du * sb
        db_ref[...] = du * a * sb * (1.0 - sb)

    tile = lambda c: pl.BlockSpec((tm, D), lambda r: (nt - 1 - r, c))
    halo = lambda c: pl.BlockSpec((CV_HALO, D), lambda r: (jnp.maximum((nt - 1 - r) * hb - 1, 0), c))
    vec = pl.BlockSpec((1, D), lambda r: (0, 0))
    wsp = pl.BlockSpec((CONV_WIDTH, D), lambda r: (0, 0))
    (da, db, dw, dbdw, dlnw, dlnb), _, slots = _pcall(
        body, (ab, ab, ab, ab, w, bdw, lnw, lnb, dout), name="cv_bwd", grid=(nt,),
        in_specs=[tile(0), tile(1), halo(0), halo(1), wsp, vec, vec, vec, tile(0)],
        out_specs=[tile(0), tile(0), wsp, vec, vec, vec],
        out_shape=[jax.ShapeDtypeStruct((T, D), F32), jax.ShapeDtypeStruct((T, D), F32),
                   jax.ShapeDtypeStruct((CONV_WIDTH, D), F32)] + [jax.ShapeDtypeStruct((1, D), F32)] * 3,
        scratch_shapes=[pltpu.VMEM((CV_HALO + tm, D), F32), pltpu.VMEM((tm + CV_HALO, D), F32)],
        sem=("arbitrary",), rs=rs)
    return (jnp.concatenate([da, db], axis=1), dw, dbdw, dlnw, dlnb), slots


def adamw(w, m, v, slots):
    L, R, C = w.shape
    tr = max([d for d in range(16, 257, 16) if R % d == 0 and N_DEV * d * C * 2 <= 2**19], default=R)
    c1 = 1.0 / (1.0 - ADAM_B1 ** ADAM_STEP)
    c2 = 1.0 / (1.0 - ADAM_B2 ** ADAM_STEP)

    def body(w_ref, m_ref, v_ref, *rest):
        s_refs = rest[:L]
        g_ref, d_ref, nm_ref, nv_ref = rest[L:]
        l = pl.program_id(0)
        for k in range(L):
            @pl.when(l == k)
            def _(s_ref=s_refs[k]):
                g = s_ref[0].astype(F32)
                for j in range(1, N_DEV):
                    g = g + s_ref[j].astype(F32)
                nm = ADAM_B1 * m_ref[0] + (1.0 - ADAM_B1) * g
                nv = ADAM_B2 * v_ref[0] + (1.0 - ADAM_B2) * (g * g)
                g_ref[0] = g
                nm_ref[0] = nm
                nv_ref[0] = nv
                d_ref[0] = -ADAM_LR * ((nm * c1) / (jnp.sqrt(nv * c2) + ADAM_EPS) + ADAM_WD * w_ref[0])

    blk = pl.BlockSpec((1, tr, C), lambda l, r: (l, r, 0))
    slot = lambda k: pl.BlockSpec((N_DEV, tr, C), lambda l, r: (0, jnp.where(l == k, r, 0), 0))
    return pl.pallas_call(
        body, name="adamw", grid=(L, R // tr),
        in_specs=[blk, blk, blk] + [slot(k) for k in range(L)],
        out_specs=[blk, blk, blk, blk],
        out_shape=[jax.ShapeDtypeStruct((L, R, C), F32)] * 4,
        compiler_params=_cparams(("arbitrary", "arbitrary")),
    )(w, m, v, *slots)


def _unshard(g, axis):
    g = jnp.moveaxis(g, 0, axis)
    s = g.shape
    return g.reshape(s[:axis] + (s[axis] * s[axis + 1],) + s[axis + 2:])


def _to_blocks(full, axis):
    s = full.shape
    g = full.reshape(s[:axis] + (N_DEV, s[axis] // N_DEV) + s[axis + 1:])
    return jnp.moveaxis(g, axis, 0)


def _heads(a, h):
    T = a.shape[0]
    return a.reshape(T, h, a.shape[1] // h).transpose(1, 0, 2)


def _unheads(a):
    h, T, d = a.shape
    return a.transpose(1, 0, 2).reshape(T, h * d)


SMALL = (("norm_w", 2), ("dn_conv_w", 2), ("conv_b_pw1", 1), ("conv_w_dw", 2), ("conv_b_dw", 1),
         ("conv_ln_w", 1), ("conv_ln_b", 1), ("conv_b_pw2", 1),
         ("attn_sinks", None), ("dn_a_log", None), ("dn_dt_bias", None), ("dn_norm_w", None), ("final_norm_w", None))

def _pack(parts):
    flat = jnp.concatenate([p.reshape(-1) for p in parts])
    pad = (-flat.shape[0]) % LANES
    return jnp.pad(flat, (0, pad))


def _unpack(flat, shapes):
    out, off = [], 0
    for s in shapes:
        n = int(np.prod(s))
        out.append(flat[off:off + n].reshape(s))
        off += n
    return out


def kernel(x, norm_w, ffn_w_gate, ffn_w_up, ffn_w_down, mix_w_in, dn_conv_w, attn_sinks, dn_a_log, dn_dt_bias, dn_norm_w, mix_w_out, conv_w_pw1, conv_b_pw1, conv_w_dw, conv_b_dw, conv_ln_w, conv_ln_b, conv_w_pw2, conv_b_pw2, final_norm_w, loss_target, m_norm_w, m_ffn_w_gate, m_ffn_w_up, m_ffn_w_down, m_mix_w_in, m_dn_conv_w, m_attn_sinks, m_dn_a_log, m_dn_dt_bias, m_dn_norm_w, m_mix_w_out, m_conv_w_pw1, m_conv_b_pw1, m_conv_w_dw, m_conv_b_dw, m_conv_ln_w, m_conv_ln_b, m_conv_w_pw2, m_conv_b_pw2, m_final_norm_w, v_norm_w, v_ffn_w_gate, v_ffn_w_up, v_ffn_w_down, v_mix_w_in, v_dn_conv_w, v_attn_sinks, v_dn_a_log, v_dn_dt_bias, v_dn_norm_w, v_mix_w_out, v_conv_w_pw1, v_conv_b_pw1, v_conv_w_dw, v_conv_b_dw, v_conv_ln_w, v_conv_ln_b, v_conv_w_pw2, v_conv_b_pw2, v_final_norm_w):
    W = dict(norm_w=norm_w, ffn_w_gate=ffn_w_gate, ffn_w_up=ffn_w_up, ffn_w_down=ffn_w_down, mix_w_in=mix_w_in,
             dn_conv_w=dn_conv_w, attn_sinks=attn_sinks, dn_a_log=dn_a_log, dn_dt_bias=dn_dt_bias,
             dn_norm_w=dn_norm_w, mix_w_out=mix_w_out, conv_w_pw1=conv_w_pw1, conv_b_pw1=conv_b_pw1,
             conv_w_dw=conv_w_dw, conv_b_dw=conv_b_dw, conv_ln_w=conv_ln_w, conv_ln_b=conv_ln_b,
             conv_w_pw2=conv_w_pw2, conv_b_pw2=conv_b_pw2, final_norm_w=final_norm_w)
    M = dict(norm_w=m_norm_w, ffn_w_gate=m_ffn_w_gate, ffn_w_up=m_ffn_w_up, ffn_w_down=m_ffn_w_down,
             mix_w_in=m_mix_w_in, dn_conv_w=m_dn_conv_w, attn_sinks=m_attn_sinks, dn_a_log=m_dn_a_log,
             dn_dt_bias=m_dn_dt_bias, dn_norm_w=m_dn_norm_w, mix_w_out=m_mix_w_out, conv_w_pw1=m_conv_w_pw1,
             conv_b_pw1=m_conv_b_pw1, conv_w_dw=m_conv_w_dw, conv_b_dw=m_conv_b_dw, conv_ln_w=m_conv_ln_w,
             conv_ln_b=m_conv_ln_b, conv_w_pw2=m_conv_w_pw2, conv_b_pw2=m_conv_b_pw2, final_norm_w=m_final_norm_w)
    V = dict(norm_w=v_norm_w, ffn_w_gate=v_ffn_w_gate, ffn_w_up=v_ffn_w_up, ffn_w_down=v_ffn_w_down,
             mix_w_in=v_mix_w_in, dn_conv_w=v_dn_conv_w, attn_sinks=v_attn_sinks, dn_a_log=v_dn_a_log,
             dn_dt_bias=v_dn_dt_bias, dn_norm_w=v_dn_norm_w, mix_w_out=v_mix_w_out, conv_w_pw1=v_conv_w_pw1,
             conv_b_pw1=v_conv_b_pw1, conv_w_dw=v_conv_w_dw, conv_b_dw=v_conv_b_dw, conv_ln_w=v_conv_ln_w,
             conv_ln_b=v_conv_ln_b, conv_w_pw2=v_conv_w_pw2, conv_b_pw2=v_conv_b_pw2, final_norm_w=v_final_norm_w)

    T, D = x.shape[1], x.shape[2]
    xs = x[0]
    F8 = ffn_w_gate.shape[-1]
    n_ffn = DEPTH * 2

    big = ("ffn_w_gate", "ffn_w_up", "ffn_w_down", "mix_w_in", "mix_w_out", "conv_w_pw1", "conv_w_pw2")
    shard3 = {k: W[k].reshape((-1,) + W[k].shape[-2:]) for k in big}
    shard_bf = {k: shard3[k].astype(BF16) for k in big}
    ffn_unit = lambda i: [("ffn_w_gate", i), ("ffn_w_up", i), ("ffn_w_down", i)]
    even_unit = lambda e: [("mix_w_in", e), ("mix_w_out", e)]
    odd_unit = lambda e: [("conv_w_pw1", e), ("conv_w_pw2", e)]
    have = {}

    def ag_jobs(units):
        return [(shard_bf[k], i) for k, i in units]

    def ag_done(units, gathered):
        have.update(zip(units, gathered))

    small_sharded = [(k, ax) for k, ax in SMALL if ax is not None]
    small_pack = _pack([W[k] for k, _ in small_sharded])[None, :]
    first_units = ffn_unit(0) + even_unit(0)
    gathered, _ = exchange(ag=ag_jobs(first_units) + [(small_pack, None)])
    ag_done(first_units, gathered[:-1])
    small_full = {}
    for (k, ax), parts in zip(small_sharded,
                              zip(*[_unpack(gathered[-1][s, 0], [W[k].shape for k, _ in small_sharded])
                                    for s in range(N_DEV)])):
        small_full[k] = _unshard(jnp.stack(parts), ax)
    nw_full = small_full["norm_w"]

    ffn_w = lambda i: [have[u] for u in ffn_unit(i)]
    w_in_of = lambda e: jnp.pad(_unshard(have[("mix_w_in", e)], 1), ((0, 0), (0, IN_COLS_PAD - IN_COLS)))
    w_out_of = lambda e: have[("mix_w_out", e)].reshape(D, D)
    w_pw1_of = lambda e: _unshard(have[("conv_w_pw1", e)], 1)
    w_pw2_of = lambda e: have[("conv_w_pw2", e)].reshape(D, D)
    fwd_carry = {("F", 0): ffn_unit(1), ("M", 0): ffn_unit(2) + odd_unit(0), ("F", 1): ffn_unit(3),
                 ("F", 2): ffn_unit(4), ("M", 1): even_unit(1), ("F", 3): ffn_unit(5), ("F", 4): ffn_unit(6),
                 ("M", 2): ffn_unit(7) + odd_unit(1)}
    zero_in = jnp.zeros((1, IN_COLS_PAD), F32)
    zero_d = jnp.zeros((1, D), F32)
    slope_rows = jnp.asarray(np.repeat(2.0 ** (-8.0 * np.arange(1, ATTN_HEADS + 1) / ATTN_HEADS), ATTN_BLOCK)
                             .astype(np.float32)[:, None])

    saved = []
    h = xs
    w_in, w_out, w_pw1, w_pw2 = {}, {}, {}, {}

    def ffn_forward(h, l, half):
        i = 2 * l + half
        units = fwd_carry.get(("F", i), [])
        h, gathered = ffn_fwd(h, nw_full[l, 2 * half][None], *ffn_w(i), ag=ag_jobs(units))
        ag_done(units, gathered)
        return h

    for l in range(DEPTH):
        e = l // 2
        st = {"x0": h}
        h = ffn_forward(h, l, 0)
        st["x1"] = h
        units = fwd_carry.get(("M", l), [])
        if l % 2 == 0:
            w_in[e], w_out[e] = w_in_of(e), w_out_of(e)
            proj = rmslin_fwd(h, nw_full[l, 1][None], w_in[e], zero_in)
            qkvc = dnconv_fwd(proj, small_full["dn_conv_w"][e])
            st["qa"] = _heads(proj[:, :Q_A], ATTN_HEADS)
            st["ka"] = _heads(proj[:, Q_A:Q_A + KV_A], ATTN_KV_HEADS)
            st["va"] = _heads(proj[:, Q_A + KV_A:OFF_QKVB], ATTN_KV_HEADS)
            st["qb"] = _heads(qkvc[:, :512], DN_HEADS)
            st["kb"] = _heads(qkvc[:, 512:1024], DN_HEADS)
            st["vb"] = _heads(qkvc[:, 1024:], DN_HEADS)
            st["zb"] = _heads(proj[:, OFF_Z:OFF_BETA], DN_HEADS)
            st["braw"] = proj[:, OFF_BETA:OFF_A].T[:, :, None]
            st["araw"] = proj[:, OFF_A:IN_COLS].T[:, :, None]
            st["sink_rows"] = jnp.repeat(attn_sinks[e], ATTN_BLOCK)[:, None]
            st["alog"] = dn_a_log[e].reshape(DN_HEADS, 1, 1)
            st["dtb"] = dn_dt_bias[e].reshape(DN_HEADS, 1, 1)
            st["dnw"] = dn_norm_w[e].reshape(1, 1, DN_D)
            att = attn_fwd(st["qa"], st["ka"], st["va"], st["sink_rows"], slope_rows)
            (og, st["sall"]), gathered = dn_fwd(st["qb"], st["kb"], st["vb"], st["zb"], st["braw"], st["araw"],
                                                st["alog"], st["dtb"], st["dnw"], ag=ag_jobs(units))
            ag_done(units, gathered)
            st["proj"] = proj
            st["mix"] = jnp.concatenate([_unheads(att), _unheads(og)], axis=1)
            h = lin_fwd(h, st["mix"], w_out[e], zero_d)
        else:
            w_pw1[e], w_pw2[e] = w_pw1_of(e), w_pw2_of(e)
            st["ab"] = rmslin_fwd(h, nw_full[l, 1][None], w_pw1[e], small_full["conv_b_pw1"][e][None])
            st["act"], gathered = cv_fwd(st["ab"], small_full["conv_w_dw"][e], small_full["conv_b_dw"][e][None],
                                         small_full["conv_ln_w"][e][None], small_full["conv_ln_b"][e][None],
                                         ag=ag_jobs(units))
            ag_done(units, gathered)
            h = lin_fwd(h, st["act"], w_pw2[e], small_full["conv_b_pw2"][e][None])
        st["x2"] = h
        h = ffn_forward(h, l, 1)
        saved.append(st)

    loss_part, dh, dfinal = loss_fwd_bwd(h, final_norm_w[None], loss_target[0])
    loss = lax.psum(loss_part[0, 0], ("x", "y", "c"))

    d_norm = [[None] * 3 for _ in range(DEPTH)]
    d_small = {k: [None, None] for k in ("dn_conv_w", "conv_b_pw1", "conv_w_dw", "conv_b_dw", "conv_ln_w",
                                         "conv_ln_b", "conv_b_pw2", "attn_sinks", "dn_a_log", "dn_dt_bias",
                                         "dn_norm_w")}
    pending, slot = [], {}

    def take_pending():
        units = list(pending)
        pending.clear()
        return [u for u, _ in units], [b for _, b in units]

    def ffn_backward(dh, l, half):
        i = 2 * l + half
        units, blocks = take_pending()
        (dh, dg, du, dd, d_norm[l][2 * half]), slots = ffn_bwd(
            st["x2" if half else "x0"], dh, nw_full[l, 2 * half][None], *ffn_w(i), rs=blocks)
        slot.update(zip(units, slots))
        pending.extend(zip(ffn_unit(i), (dg, du, dd)))
        return dh

    for l in reversed(range(DEPTH)):
        e = l // 2
        st = saved[l]
        dh = ffn_backward(dh, l, 1)
        if l % 2 == 0:
            dmix, d_out, _ = lin_bwd(st["mix"], dh, w_out[e])
            pending.append((("mix_w_out", e), d_out.reshape(N_DEV, D // N_DEV, D).astype(BF16)))
            units, blocks = take_pending()
            (dqb, dkb, dvb, dzb, dbraw, daraw, dalog, ddtb, ddnw), slots = dn_bwd(
                st["qb"], st["kb"], st["vb"], st["zb"], st["braw"], st["araw"], st["alog"], st["dtb"], st["dnw"],
                st["sall"], _heads(dmix[:, Q_A:], DN_HEADS), rs=blocks)
            slot.update(zip(units, slots))
            dqa, dka, dva, dsink = attn_bwd(st["qa"], st["ka"], st["va"], st["sink_rows"], slope_rows,
                                            _heads(dmix[:, :Q_A], ATTN_HEADS))
            dqkv, d_small["dn_conv_w"][e] = dnconv_bwd(
                st["proj"], small_full["dn_conv_w"][e],
                jnp.concatenate([_unheads(dqb), _unheads(dkb), _unheads(dvb)], axis=1))
            dproj = jnp.concatenate(
                [_unheads(dqa), _unheads(dka), _unheads(dva), dqkv, _unheads(dzb), dbraw[:, :, 0].T,
                 daraw[:, :, 0].T, jnp.zeros((T, IN_COLS_PAD - IN_COLS), F32)], axis=1)
            dh, d_in, _, d_norm[l][1] = rmslin_bwd(st["x1"], dh, dproj, nw_full[l, 1][None], w_in[e])
            pending.append((("mix_w_in", e), _to_blocks(d_in[:, :IN_COLS], 1).astype(BF16)))
            d_small["attn_sinks"][e] = jnp.sum(dsink.reshape(ATTN_HEADS, ATTN_BLOCK), axis=1)
            d_small["dn_a_log"][e] = dalog.reshape(DN_HEADS)
            d_small["dn_dt_bias"][e] = ddtb.reshape(DN_HEADS)
            d_small["dn_norm_w"][e] = ddnw.reshape(DN_D)
        else:
            dact, d_pw2, d_small["conv_b_pw2"][e] = lin_bwd(st["act"], dh, w_pw2[e])
            pending.append((("conv_w_pw2", e), d_pw2.reshape(N_DEV, D // N_DEV, D).astype(BF16)))
            units, blocks = take_pending()
            (dab, d_small["conv_w_dw"][e], d_small["conv_b_dw"][e], d_small["conv_ln_w"][e],
             d_small["conv_ln_b"][e]), slots = cv_bwd(
                st["ab"], small_full["conv_w_dw"][e], small_full["conv_b_dw"][e][None],
                small_full["conv_ln_w"][e][None], small_full["conv_ln_b"][e][None], dact, rs=blocks)
            slot.update(zip(units, slots))
            dh, d_pw1, d_small["conv_b_pw1"][e], d_norm[l][1] = rmslin_bwd(
                st["x1"], dh, dab, nw_full[l, 1][None], w_pw1[e])
            pending.append((("conv_w_pw1", e), _to_blocks(d_pw1, 1).astype(BF16)))
        dh = ffn_backward(dh, l, 0)
    grad_x = dh[None]

    full_small = {"norm_w": jnp.stack([jnp.concatenate(r, axis=0) for r in d_norm]),
                  "final_norm_w": dfinal[0]}
    for k, pair in d_small.items():
        full_small[k] = jnp.stack([p.reshape(W[k].shape[1:-1] + (-1,)) if SMALL_AXIS[k] is not None
                                   else p for p in pair])
    rows = []
    for s in range(N_DEV):
        parts = [_to_blocks(full_small[k], ax)[s] if ax is not None else full_small[k] for k, ax in SMALL]
        rows.append(_pack(parts))
    send_small = jnp.stack(rows)[:, None, :]
    units, blocks = take_pending()
    _, slots = exchange(rs=blocks + [send_small])
    slot.update(zip(units, slots[:-1]))

    res = {}
    for k in big:
        outs = adamw(shard3[k], M[k].reshape(shard3[k].shape), V[k].reshape(shard3[k].shape),
                     [slot[(k, i)] for i in range(shard3[k].shape[0])])
        res[k] = [o.reshape(W[k].shape) for o in outs]
    pk = lambda d: _pack([d[k] for k, _ in SMALL])[None, None, :]
    outs = adamw(pk(W), pk(M), pk(V), [slots[-1]])
    shapes = [W[k].shape for k, _ in SMALL]
    unp = [_unpack(o[0, 0], shapes) for o in outs]
    for i, (k, _) in enumerate(SMALL):
        res[k] = [u[i] for u in unp]

    order = ("norm_w", "ffn_w_gate", "ffn_w_up", "ffn_w_down", "mix_w_in", "dn_conv_w", "attn_sinks", "dn_a_log",
             "dn_dt_bias", "dn_norm_w", "mix_w_out", "conv_w_pw1", "conv_b_pw1", "conv_w_dw", "conv_b_dw",
             "conv_ln_w", "conv_ln_b", "conv_w_pw2", "conv_b_pw2", "final_norm_w")
    return (loss, grad_x, *[res[k][0] for k in order], *[res[k][1] for k in order],
            *[res[k][2] for k in order], *[res[k][3] for k in order])


SMALL_AXIS = dict(SMALL)
```

```python
import functools

import numpy as np
import jax
import jax.numpy as jnp
from jax import lax
from jax.experimental import pallas as pl
from jax.experimental.pallas import tpu as pltpu

F32 = jnp.float32
BF16 = jnp.bfloat16
HI = lax.Precision.HIGHEST
EPS = 1e-6
N_DEV = 8
V7X_VMEM_LIMIT = 56 * 2**20
MESH = pl.DeviceIdType.MESH
LANES = 128

DEPTH = 4
D_MODEL = 1024
ATTN_HEADS, ATTN_KV_HEADS, HEAD_DIM, ATTN_BLOCK = 8, 2, 64, 128
DN_HEADS, DN_D, DN_CHUNK, DN_CONV = 8, 64, 64, 4
CONV_WIDTH = 31
Q_A, KV_A, QKV_B, V_B = 512, 128, 1536, 512
IN_COLS = 2832
IN_COLS_PAD = 3072
OFF_QKVB = Q_A + 2 * KV_A
OFF_Z = OFF_QKVB + QKV_B
OFF_BETA = OFF_Z + V_B
OFF_A = OFF_BETA + DN_HEADS

ADAM_LR, ADAM_B1, ADAM_B2, ADAM_EPS, ADAM_WD, ADAM_STEP = 0.001, 0.9, 0.999, 1e-08, 0.01, 10


def _cparams(sem):
    return pltpu.CompilerParams(dimension_semantics=sem, vmem_limit_bytes=V7X_VMEM_LIMIT)


def _sigmoid(x):
    return 1.0 / (1.0 + jnp.exp(-x))


def _softplus(x):
    return jnp.maximum(x, 0.0) + jnp.log(1.0 + jnp.exp(-jnp.abs(x)))


def _dot(a, b):
    return jnp.dot(a, b, preferred_element_type=F32)


def _dot_nt(a, b):
    return lax.dot_general(a, b, (((1,), (1,)), ((), ())), preferred_element_type=F32)


def _dot_tn(a, b):
    return lax.dot_general(a, b, (((0,), (0,)), ((), ())), preferred_element_type=F32)


def _rms(x, w):
    return x * lax.rsqrt(jnp.mean(x * x, axis=-1, keepdims=True) + EPS) * w


def _rms_bwd(x, w, dxn):
    r = lax.rsqrt(jnp.mean(x * x, axis=-1, keepdims=True) + EPS)
    xh = x * r
    dxh = dxn * w
    dx = r * (dxh - xh * jnp.mean(dxh * xh, axis=-1, keepdims=True))
    return dx, jnp.sum(dxn * xh, axis=0, keepdims=True)


def _position():
    return lax.axis_index("x"), lax.axis_index("y"), lax.axis_index("c")


def _dev_index(px, py, pc):
    return 4 * px + 2 * py + pc


def _rcopy(src, dst, send_sem, recv_sem, to):
    return pltpu.make_async_remote_copy(src_ref=src, dst_ref=dst, send_sem=send_sem, recv_sem=recv_sem,
                                        device_id=to, device_id_type=MESH)


def _ag_start(srcs, outs, send, recv, local):
    x, y, c = _position()
    me = _dev_index(x, y, c)
    chips = [(1 - x, y), (x, 1 - y), (1 - x, 1 - y)]
    for a, (src, out) in enumerate(zip(srcs, outs)):
        pltpu.make_async_copy(src, out.at[me], local.at[a]).start()
        _rcopy(src, out.at[me], send.at[a, 0], recv.at[a, 0], (x, y, 1 - c)).start()
        for j, chip in enumerate(chips):
            _rcopy(src, out.at[me], send.at[a, 1 + j], recv.at[a, 1 + j], (*chip, c)).start()


def _ag_finish(srcs, outs, send, recv, local):
    x, y, c = _position()
    me = _dev_index(x, y, c)
    sibling = (x, y, 1 - c)
    chips = [(1 - x, y), (x, 1 - y), (1 - x, 1 - y)]
    for j, chip in enumerate(chips):
        for a, out in enumerate(outs):
            blk = out.at[_dev_index(*chip, c)]
            _rcopy(blk, blk, send.at[a, 1 + j], recv.at[a, 1 + j], (x, y, c)).wait_recv()
            _rcopy(blk, blk, send.at[a, 4 + j], recv.at[a, 4 + j], sibling).start()
    for a, (src, out) in enumerate(zip(srcs, outs)):
        blk = out.at[_dev_index(x, y, 1 - c)]
        _rcopy(blk, blk, send.at[a, 0], recv.at[a, 0], (x, y, c)).wait_recv()
        for j, chip in enumerate(chips):
            blk = out.at[_dev_index(*chip, 1 - c)]
            _rcopy(blk, blk, send.at[a, 4 + j], recv.at[a, 4 + j], (x, y, c)).wait_recv()
        for k in range(N_DEV - 1):
            _rcopy(out.at[me], out.at[me], send.at[a, k], recv.at[a, k], (x, y, c)).wait_send()
        pltpu.make_async_copy(src, out.at[me], local.at[a]).wait()


def _rs_peer(r):
    x, y, c = _position()
    return x ^ ((r >> 2) & 1), y ^ ((r >> 1) & 1), c ^ (r & 1)


def _rs_start(ins, outs, send, recv, local):
    me = _dev_index(*_position())
    for a, (src, out) in enumerate(zip(ins, outs)):
        pltpu.make_async_copy(src.at[me], out.at[me], local.at[a]).start()
        for r in range(1, N_DEV):
            p = _rs_peer(r)
            _rcopy(src.at[_dev_index(*p)], out.at[me], send.at[a, r - 1], recv.at[a, r - 1], p).start()


def _rs_finish(ins, outs, send, recv, local):
    pos = _position()
    me = _dev_index(*pos)
    for a, (src, out) in enumerate(zip(ins, outs)):
        for r in range(1, N_DEV):
            blk = out.at[_dev_index(*_rs_peer(r))]
            _rcopy(blk, blk, send.at[a, r - 1], recv.at[a, r - 1], pos).wait_recv()
        for r in range(1, N_DEV):
            _rcopy(src.at[me], out.at[me], send.at[a, r - 1], recv.at[a, r - 1], pos).wait_send()
        pltpu.make_async_copy(src.at[me], out.at[me], local.at[a]).wait()


def _pcall(body, args, *, name, grid, in_specs, out_specs, out_shape, sem, scratch_shapes=(), ag=(), rs=()):
    na, nr = len(ag), len(rs)
    if na + nr == 0:
        outs = pl.pallas_call(body, name=name, grid=grid, in_specs=in_specs, out_specs=out_specs,
                              out_shape=out_shape, scratch_shapes=list(scratch_shapes),
                              compiler_params=_cparams(sem))(*args)
        return list(outs), [], []
    n_in, n_out, n_scr = len(in_specs), len(out_specs), len(scratch_shapes)
    ag_idx = [i for _, i in ag]

    def wrapped(*refs):
        cin, refs = refs[:n_in], refs[n_in:]
        ag_in, refs = refs[:na], refs[na:]
        rs_in, refs = refs[:nr], refs[nr:]
        cout, refs = refs[:n_out], refs[n_out:]
        ag_out, refs = refs[:na], refs[na:]
        rs_out, refs = refs[:nr], refs[nr:]
        cscr, sems = refs[:n_scr], refs[n_scr:]
        ag_src = [r if i is None else r.at[i] for r, i in zip(ag_in, ag_idx)]
        ids = [pl.program_id(d) for d in range(len(grid))]
        first = functools.reduce(jnp.logical_and, [i == 0 for i in ids])
        last = functools.reduce(jnp.logical_and, [i == g - 1 for i, g in zip(ids, grid)])

        @pl.when(first)
        def _():
            if na:
                _ag_start(ag_src, ag_out, *sems[:3])
            if nr:
                _rs_start(rs_in, rs_out, *sems[-3:])

        body(*cin, *cout, *cscr)

        @pl.when(last)
        def _():
            if na:
                _ag_finish(ag_src, ag_out, *sems[:3])
            if nr:
                _rs_finish(rs_in, rs_out, *sems[-3:])

    hbm = pl.BlockSpec(memory_space=pl.ANY)
    sem_shapes = []
    for n in (na, nr):
        if n:
            sem_shapes += [pltpu.SemaphoreType.DMA((n, N_DEV - 1)), pltpu.SemaphoreType.DMA((n, N_DEV - 1)),
                           pltpu.SemaphoreType.DMA((n,))]
    outs = pl.pallas_call(
        wrapped, name=name, grid=grid,
        in_specs=list(in_specs) + [hbm] * (na + nr),
        out_specs=list(out_specs) + [hbm] * (na + nr),
        out_shape=list(out_shape)
        + [jax.ShapeDtypeStruct((N_DEV,) + a.shape[-2:], a.dtype) for a, _ in ag]
        + [jax.ShapeDtypeStruct(b.shape, b.dtype) for b in rs],
        scratch_shapes=list(scratch_shapes) + sem_shapes,
        compiler_params=_cparams(sem),
    )(*args, *[a for a, _ in ag], *rs)
    return list(outs[:n_out]), list(outs[n_out:n_out + na]), list(outs[n_out + na:])


def exchange(ag=(), rs=()):
    def body(o_ref):
        o_ref[...] = jnp.zeros_like(o_ref)

    _, gathered, slots = _pcall(body, (), name="exchange", grid=(1,), in_specs=[],
                                out_specs=[pl.BlockSpec((8, LANES), lambda i: (0, 0))],
                                out_shape=[jax.ShapeDtypeStruct((8, LANES), F32)], sem=("arbitrary",), ag=ag, rs=rs)
    return gathered, slots


def ffn_fwd(x, nw, wg, wu, wd, ag=()):
    T, D = x.shape
    J, _, F = wg.shape
    tm = min(T, 1024)

    def body(x_ref, nw_ref, wg_ref, wu_ref, wd_ref, o_ref, xn_ref, acc_ref):
        j = pl.program_id(1)

        @pl.when(j == 0)
        def _():
            xn_ref[...] = _rms(x_ref[...], nw_ref[...]).astype(BF16)
            acc_ref[...] = jnp.zeros_like(acc_ref)

        xn = xn_ref[...]
        g = _dot(xn, wg_ref[0])
        u = _dot(xn, wu_ref[0])
        h = (g * _sigmoid(g) * u).astype(BF16)
        acc_ref[...] += _dot(h, wd_ref[0])

        @pl.when(j == J - 1)
        def _():
            o_ref[...] = x_ref[...] + 0.5 * acc_ref[...]

    (out,), gathered, _ = _pcall(
        body, (x, nw, wg, wu, wd), name="ffn_fwd", grid=(T // tm, J),
        in_specs=[pl.BlockSpec((tm, D), lambda t, j: (t, 0)),
                  pl.BlockSpec((1, D), lambda t, j: (0, 0)),
                  pl.BlockSpec((1, D, F), lambda t, j: (j, 0, 0)),
                  pl.BlockSpec((1, D, F), lambda t, j: (j, 0, 0)),
                  pl.BlockSpec((1, F, D), lambda t, j: (j, 0, 0))],
        out_specs=[pl.BlockSpec((tm, D), lambda t, j: (t, 0))],
        out_shape=[jax.ShapeDtypeStruct((T, D), F32)],
        scratch_shapes=[pltpu.VMEM((tm, D), BF16), pltpu.VMEM((tm, D), F32)],
        sem=("arbitrary", "arbitrary"), ag=ag)
    return out, gathered


def ffn_bwd(x, dy, nw, wg, wu, wd, rs=()):
    T, D = x.shape
    J, _, F = wg.shape
    tm = min(T, 512)
    nt = T // tm

    def body(x_ref, dy_ref, nw_ref, wg_ref, wu_ref, wd_ref,
             dx_ref, dwg_ref, dwu_ref, dwd_ref, dnw_ref,
             xn_ref, dyh_ref, dxn_ref, awg_ref, awu_ref, awd_ref):
        j = pl.program_id(0)
        t = pl.program_id(1)
        rows = pl.ds(pl.multiple_of(t * tm, tm), tm)

        @pl.when(j == 0)
        def _():
            xn_ref[rows, :] = _rms(x_ref[...], nw_ref[...]).astype(BF16)
            dyh_ref[rows, :] = (0.5 * dy_ref[...]).astype(BF16)
            dxn_ref[rows, :] = jnp.zeros((tm, D), F32)

        @pl.when((j == 0) & (t == 0))
        def _():
            dnw_ref[...] = jnp.zeros_like(dnw_ref)

        @pl.when(t == 0)
        def _():
            awg_ref[...] = jnp.zeros_like(awg_ref)
            awu_ref[...] = jnp.zeros_like(awu_ref)
            awd_ref[...] = jnp.zeros_like(awd_ref)

        xn = xn_ref[rows, :]
        dyh = dyh_ref[rows, :]
        g = _dot(xn, wg_ref[0])
        u = _dot(xn, wu_ref[0])
        sg = _sigmoid(g)
        s = g * sg
        h = (s * u).astype(BF16)
        dh = _dot_nt(dyh, wd_ref[0])
        du = (dh * s).astype(BF16)
        dg = (dh * u * (sg * (1.0 + g * (1.0 - sg)))).astype(BF16)
        awd_ref[...] += _dot_tn(h, dyh)
        awg_ref[...] += _dot_tn(xn, dg)
        awu_ref[...] += _dot_tn(xn, du)
        dxn_ref[rows, :] += _dot_nt(dg, wg_ref[0]) + _dot_nt(du, wu_ref[0])

        @pl.when(t == nt - 1)
        def _():
            dwg_ref[0] = awg_ref[...].astype(BF16)
            dwu_ref[0] = awu_ref[...].astype(BF16)
            dwd_ref[0] = awd_ref[...].astype(BF16)

        @pl.when(j == J - 1)
        def _():
            dx, dnw = _rms_bwd(x_ref[...], nw_ref[...], dxn_ref[rows, :])
            dx_ref[...] = dy_ref[...] + dx
            dnw_ref[...] += dnw

    ends = lambda j, t: (jnp.where((j == 0) | (j == J - 1), t, 0), 0)
    last = lambda j, t: (jnp.where(j == J - 1, t, 0), 0)
    outs, _, slots = _pcall(
        body, (x, dy, nw, wg, wu, wd), name="ffn_bwd", grid=(J, nt),
        in_specs=[pl.BlockSpec((tm, D), ends), pl.BlockSpec((tm, D), ends),
                  pl.BlockSpec((1, D), lambda j, t: (0, 0)),
                  pl.BlockSpec((1, D, F), lambda j, t: (j, 0, 0)),
                  pl.BlockSpec((1, D, F), lambda j, t: (j, 0, 0)),
                  pl.BlockSpec((1, F, D), lambda j, t: (j, 0, 0))],
        out_specs=[pl.BlockSpec((tm, D), last),
                   pl.BlockSpec((1, D, F), lambda j, t: (j, 0, 0)),
                   pl.BlockSpec((1, D, F), lambda j, t: (j, 0, 0)),
                   pl.BlockSpec((1, F, D), lambda j, t: (j, 0, 0)),
                   pl.BlockSpec((1, D), lambda j, t: (0, 0))],
        out_shape=[jax.ShapeDtypeStruct((T, D), F32),
                   jax.ShapeDtypeStruct((J, D, F), BF16), jax.ShapeDtypeStruct((J, D, F), BF16),
                   jax.ShapeDtypeStruct((J, F, D), BF16), jax.ShapeDtypeStruct((1, D), F32)],
        scratch_shapes=[pltpu.VMEM((T, D), BF16), pltpu.VMEM((T, D), BF16), pltpu.VMEM((T, D), F32),
                        pltpu.VMEM((D, F), F32), pltpu.VMEM((D, F), F32), pltpu.VMEM((F, D), F32)],
        sem=("arbitrary", "arbitrary"), rs=rs)
    return outs, slots


def rmslin_fwd(x, nw, w, b):
    T, D = x.shape
    N = w.shape[1]
    tm = min(T, 256)

    def body(x_ref, nw_ref, w_ref, b_ref, o_ref):
        xn = _rms(x_ref[...], nw_ref[...]).astype(BF16)
        o_ref[...] = _dot(xn, w_ref[...]) + b_ref[...]

    return pl.pallas_call(
        body, name="rmslin_fwd", grid=(T // tm,),
        in_specs=[pl.BlockSpec((tm, D), lambda t: (t, 0)), pl.BlockSpec((1, D), lambda t: (0, 0)),
                  pl.BlockSpec((D, N), lambda t: (0, 0)), pl.BlockSpec((1, N), lambda t: (0, 0))],
        out_specs=pl.BlockSpec((tm, N), lambda t: (t, 0)),
        out_shape=jax.ShapeDtypeStruct((T, N), F32),
        compiler_params=_cparams(("parallel",)),
    )(x, nw, w, b)


def rmslin_bwd(x, dres, dproj, nw, w):
    T, D = x.shape
    N = w.shape[1]
    nb = 1024
    nc = N // nb
    tm = min(T, 256)
    nt = T // tm

    def body(x_ref, dres_ref, dp_ref, nw_ref, w_ref, dx_ref, dw_ref, db_ref, dnw_ref, xn_ref, dxn_ref):
        c = pl.program_id(0)
        t = pl.program_id(1)
        rows = pl.ds(pl.multiple_of(t * tm, tm), tm)

        @pl.when(c == 0)
        def _():
            xn_ref[rows, :] = _rms(x_ref[...], nw_ref[...]).astype(BF16)
            dxn_ref[rows, :] = jnp.zeros((tm, D), F32)

        @pl.when((c == 0) & (t == 0))
        def _():
            dnw_ref[...] = jnp.zeros_like(dnw_ref)

        @pl.when(t == 0)
        def _():
            dw_ref[...] = jnp.zeros_like(dw_ref)
            db_ref[...] = jnp.zeros_like(db_ref)

        dpf = dp_ref[...]
        dp = dpf.astype(BF16)
        dw_ref[...] += _dot_tn(xn_ref[rows, :], dp)
        db_ref[...] += jnp.sum(dpf, axis=0, keepdims=True)
        dxn_ref[rows, :] += _dot_nt(dp, w_ref[...])

        @pl.when(c == nc - 1)
        def _():
            dx, dnw = _rms_bwd(x_ref[...], nw_ref[...], dxn_ref[rows, :])
            dx_ref[...] = dres_ref[...] + dx
            dnw_ref[...] += dnw

    ends = lambda c, t: (jnp.where((c == 0) | (c == nc - 1), t, 0), 0)
    last = lambda c, t: (jnp.where(c == nc - 1, t, 0), 0)
    return pl.pallas_call(
        body, name="rmslin_bwd", grid=(nc, nt),
        in_specs=[pl.BlockSpec((tm, D), ends), pl.BlockSpec((tm, D), last),
                  pl.BlockSpec((tm, nb), lambda c, t: (t, c)),
                  pl.BlockSpec((1, D), lambda c, t: (0, 0)),
                  pl.BlockSpec((D, nb), lambda c, t: (0, c))],
        out_specs=[pl.BlockSpec((tm, D), last),
                   pl.BlockSpec((D, nb), lambda c, t: (0, c)),
                   pl.BlockSpec((1, nb), lambda c, t: (0, c)),
                   pl.BlockSpec((1, D), lambda c, t: (0, 0))],
        out_shape=[jax.ShapeDtypeStruct((T, D), F32), jax.ShapeDtypeStruct((D, N), F32),
                   jax.ShapeDtypeStruct((1, N), F32), jax.ShapeDtypeStruct((1, D), F32)],
        scratch_shapes=[pltpu.VMEM((T, D), BF16), pltpu.VMEM((T, D), F32)],
        compiler_params=_cparams(("arbitrary", "arbitrary")),
    )(x, dres, dproj, nw, w)


def lin_fwd(res, a, w, b):
    T, K = a.shape
    N = w.shape[1]
    tm = min(T, 512)

    def body(res_ref, a_ref, w_ref, b_ref, o_ref):
        o_ref[...] = res_ref[...] + _dot(a_ref[...].astype(BF16), w_ref[...]) + b_ref[...]

    return pl.pallas_call(
        body, name="lin_fwd", grid=(T // tm,),
        in_specs=[pl.BlockSpec((tm, N), lambda t: (t, 0)), pl.BlockSpec((tm, K), lambda t: (t, 0)),
                  pl.BlockSpec((K, N), lambda t: (0, 0)), pl.BlockSpec((1, N), lambda t: (0, 0))],
        out_specs=pl.BlockSpec((tm, N), lambda t: (t, 0)),
        out_shape=jax.ShapeDtypeStruct((T, N), F32),
        compiler_params=_cparams(("parallel",)),
    )(res, a, w, b)


def lin_bwd(a, dy, w):
    T, K = a.shape
    N = w.shape[1]
    tm = min(T, 256)

    def body(a_ref, dy_ref, w_ref, da_ref, dw_ref, db_ref):
        @pl.when(pl.program_id(0) == 0)
        def _():
            dw_ref[...] = jnp.zeros_like(dw_ref)
            db_ref[...] = jnp.zeros_like(db_ref)

        dyf = dy_ref[...]
        dyb = dyf.astype(BF16)
        da_ref[...] = _dot_nt(dyb, w_ref[...])
        dw_ref[...] += _dot_tn(a_ref[...].astype(BF16), dyb)
        db_ref[...] += jnp.sum(dyf, axis=0, keepdims=True)

    return pl.pallas_call(
        body, name="lin_bwd", grid=(T // tm,),
        in_specs=[pl.BlockSpec((tm, K), lambda t: (t, 0)), pl.BlockSpec((tm, N), lambda t: (t, 0)),
                  pl.BlockSpec((K, N), lambda t: (0, 0))],
        out_specs=[pl.BlockSpec((tm, K), lambda t: (t, 0)), pl.BlockSpec((K, N), lambda t: (0, 0)),
                   pl.BlockSpec((1, N), lambda t: (0, 0))],
        out_shape=[jax.ShapeDtypeStruct((T, K), F32), jax.ShapeDtypeStruct((K, N), F32),
                   jax.ShapeDtypeStruct((1, N), F32)],
        compiler_params=_cparams(("arbitrary",)),
    )(a, dy, w)


def loss_fwd_bwd(x, fw, target):
    T, D = x.shape
    tm = min(T, 256)

    def body(x_ref, fw_ref, tg_ref, loss_ref, dx_ref, dfw_ref):
        @pl.when(pl.program_id(0) == 0)
        def _():
            loss_ref[...] = jnp.zeros_like(loss_ref)
            dfw_ref[...] = jnp.zeros_like(dfw_ref)

        xv = x_ref[...]
        w = fw_ref[...]
        err = _rms(xv, w) - tg_ref[...]
        row = jnp.sum(err * err, axis=-1, keepdims=True)
        loss_ref[...] += (0.5 / D) * jnp.sum(row, axis=0, keepdims=True)
        dx, dfw = _rms_bwd(xv, w, err * (1.0 / D))
        dx_ref[...] = dx
        dfw_ref[...] += dfw

    return pl.pallas_call(
        body, name="loss_fwd_bwd", grid=(T // tm,),
        in_specs=[pl.BlockSpec((tm, D), lambda t: (t, 0)), pl.BlockSpec((1, D), lambda t: (0, 0)),
                  pl.BlockSpec((tm, D), lambda t: (t, 0))],
        out_specs=[pl.BlockSpec((1, 1), lambda t: (0, 0)), pl.BlockSpec((tm, D), lambda t: (t, 0)),
                   pl.BlockSpec((1, D), lambda t: (0, 0))],
        out_shape=[jax.ShapeDtypeStruct((1, 1), F32), jax.ShapeDtypeStruct((T, D), F32),
                   jax.ShapeDtypeStruct((1, D), F32)],
        compiler_params=_cparams(("arbitrary",)),
    )(x, fw, target)


def _attn_masks(n, rows, blk):
    r = lax.broadcasted_iota(jnp.int32, (rows, 2 * blk), 0)
    jj = lax.broadcasted_iota(jnp.int32, (rows, 2 * blk), 1)
    dist = (r % blk) + blk - jj
    valid = (dist >= 0) & (dist < blk) & ((n > 0) | (jj >= blk))
    return dist.astype(F32), valid


def _attn_block(q, kcat, vcat, sink, slope, dist, valid):
    d = q.shape[-1]
    s = _dot_nt(q.astype(BF16), kcat.astype(BF16)) * (d ** -0.5)
    s = jnp.where(valid, s - slope * dist, -1e30)
    m = lax.stop_gradient(jnp.maximum(jnp.max(s, axis=-1, keepdims=True), sink))
    e = jnp.exp(s - m)
    p = e / (jnp.sum(e, axis=-1, keepdims=True) + jnp.exp(sink - m))
    return _dot(p.astype(BF16), vcat.astype(BF16))


def _attn_specs(G, blk, d):
    qs = pl.BlockSpec((G, blk, d), lambda h, n: (h, n, 0))
    kprev = pl.BlockSpec((1, blk, d), lambda h, n: (h, jnp.maximum(n - 1, 0), 0))
    kcur = pl.BlockSpec((1, blk, d), lambda h, n: (h, n, 0))
    rowp = pl.BlockSpec((G * blk, 1), lambda h, n: (h, 0))
    return qs, kprev, kcur, rowp


def attn_fwd(q, k, v, sink_rows, slope_rows):
    Hq, T, d = q.shape
    Hkv = k.shape[0]
    G = Hq // Hkv
    blk = ATTN_BLOCK
    nblk = T // blk

    def body(q_ref, kp_ref, kc_ref, vp_ref, vc_ref, sink_ref, slope_ref, o_ref):
        n = pl.program_id(1)
        dist, valid = _attn_masks(n, G * blk, blk)
        kcat = jnp.concatenate([kp_ref[0], kc_ref[0]], axis=0)
        vcat = jnp.concatenate([vp_ref[0], vc_ref[0]], axis=0)
        o = _attn_block(q_ref[...].reshape(G * blk, d), kcat, vcat, sink_ref[...], slope_ref[...], dist, valid)
        o_ref[...] = o.reshape(G, blk, d)

    qs, kprev, kcur, rowp = _attn_specs(G, blk, d)
    return pl.pallas_call(
        body, name="attn_fwd", grid=(Hkv, nblk),
        in_specs=[qs, kprev, kcur, kprev, kcur, rowp, rowp],
        out_specs=qs,
        out_shape=jax.ShapeDtypeStruct((Hq, T, d), F32),
        compiler_params=_cparams(("parallel", "parallel")),
    )(q, k, k, v, v, sink_rows, slope_rows)


def attn_bwd(q, k, v, sink_rows, slope_rows, do):
    Hq, T, d = q.shape
    Hkv = k.shape[0]
    G = Hq // Hkv
    blk = ATTN_BLOCK
    nblk = T // blk

    def body(q_ref, kp_ref, kc_ref, vp_ref, vc_ref, sink_ref, slope_ref, do_ref,
             dq_ref, dk_ref, dv_ref, dsink_ref):
        n = pl.program_id(1)

        @pl.when(n == 0)
        def _():
            dk_ref[...] = jnp.zeros_like(dk_ref)
            dv_ref[...] = jnp.zeros_like(dv_ref)
            dsink_ref[...] = jnp.zeros_like(dsink_ref)

        dist, valid = _attn_masks(n, G * blk, blk)
        kcat = jnp.concatenate([kp_ref[0], kc_ref[0]], axis=0)
        vcat = jnp.concatenate([vp_ref[0], vc_ref[0]], axis=0)
        fn = functools.partial(_attn_block, slope=slope_ref[...], dist=dist, valid=valid)
        _, vjp = jax.vjp(fn, q_ref[...].reshape(G * blk, d), kcat, vcat, sink_ref[...])
        dq, dkcat, dvcat, dsink = vjp(do_ref[...].reshape(G * blk, d))
        dq_ref[...] = dq.reshape(G, blk, d)
        dsink_ref[...] += dsink

        @pl.when(n == 0)
        def _():
            dk_ref[0, 0:blk, :] += dkcat[blk:]
            dv_ref[0, 0:blk, :] += dvcat[blk:]

        @pl.when(n > 0)
        def _():
            rows = pl.ds(pl.multiple_of((n - 1) * blk, blk), 2 * blk)
            dk_ref[0, rows, :] += dkcat
            dv_ref[0, rows, :] += dvcat

    qs, kprev, kcur, rowp = _attn_specs(G, blk, d)
    kvfull = pl.BlockSpec((1, T, d), lambda h, n: (h, 0, 0))
    return pl.pallas_call(
        body, name="attn_bwd", grid=(Hkv, nblk),
        in_specs=[qs, kprev, kcur, kprev, kcur, rowp, rowp, qs],
        out_specs=[qs, kvfull, kvfull, rowp],
        out_shape=[jax.ShapeDtypeStruct((Hq, T, d), F32), jax.ShapeDtypeStruct((Hkv, T, d), F32),
                   jax.ShapeDtypeStruct((Hkv, T, d), F32), jax.ShapeDtypeStruct((Hq * blk, 1), F32)],
        compiler_params=_cparams(("parallel", "arbitrary")),
    )(q, k, k, v, v, sink_rows, slope_rows, do)


def _bmm(a, b, dims, exact):
    if exact:
        return lax.dot_general(a, b, dims, precision=HI, preferred_element_type=F32)
    return lax.dot_general(a.astype(BF16), b.astype(BF16), dims, preferred_element_type=F32)


def _bmm_nn(a, b, exact=False):
    return _bmm(a, b, (((2,), (1,)), ((0,), (0,))), exact)


def _bmm_nt(a, b, exact=False):
    return _bmm(a, b, (((2,), (2,)), ((0,), (0,))), exact)


def _bmm_tn(a, b, exact=False):
    return _bmm(a, b, (((1,), (1,)), ((0,), (0,))), exact)


def _dn_chunk(qc, kc, vc, zc, braw, araw, alog, dtb, nw, S):
    H, C, D = qc.shape
    row = lax.broadcasted_iota(jnp.int32, (H, C, C), 1)
    col = lax.broadcasted_iota(jnp.int32, (H, C, C), 2)
    causal = row >= col
    strict = row > col
    eye = (row == col).astype(F32)
    ltri = causal.astype(F32)
    ones = jnp.ones((H, C, C), F32)

    q = qc * lax.rsqrt(jnp.sum(qc * qc, axis=-1, keepdims=True) + EPS) * (D ** -0.5)
    k = kc * lax.rsqrt(jnp.sum(kc * kc, axis=-1, keepdims=True) + EPS)
    beta = _sigmoid(braw)
    g = -jnp.exp(alog) * _softplus(araw + dtb)
    a_col = _bmm_nn(ltri, jnp.broadcast_to(g, (H, C, C)), True)
    a_row = _bmm_nn(ones, eye * a_col, True)
    decay = jnp.where(causal, jnp.exp(jnp.where(causal, a_col - a_row, 0.0)), 0.0)
    kb = k * beta
    low = jnp.where(strict, _bmm_nt(kb, k, True) * decay, 0.0)
    e_col = jnp.exp(a_col)
    tinv = eye - low
    p = low
    for _ in range(5):
        p = _bmm_nn(p, p)
        tinv = tinv + _bmm_nn(tinv, p)
    u = _bmm_nn(tinv, vc * beta)
    w = _bmm_nn(tinv, kb * e_col)
    attn = _bmm_nt(q, k, True) * decay
    gl = a_col[:, C - 1:C, :]
    k_dec = k * jnp.exp(gl - a_col)
    v_new = u - _bmm_nn(w, S)
    o = _bmm_nn(q * e_col, S) + _bmm_nn(attn, v_new)
    s_new = S * jnp.exp(jnp.broadcast_to(gl, (H, D, D))) + _bmm_tn(k_dec, v_new)
    on = o * lax.rsqrt(jnp.mean(o * o, axis=-1, keepdims=True) + EPS) * nw
    return on * (zc * _sigmoid(zc)), s_new


def dn_fwd(q, k, v, z, braw, araw, alog, dtb, nw, ag=()):
    H, T, D = q.shape
    C = DN_CHUNK
    N = T // C

    def body(q_ref, k_ref, v_ref, z_ref, b_ref, a_ref, alog_ref, dtb_ref, nw_ref, o_ref, sall_ref, s_ref):
        @pl.when(pl.program_id(0) == 0)
        def _():
            s_ref[...] = jnp.zeros_like(s_ref)

        s_in = s_ref[...]
        sall_ref[0] = s_in
        on, s_new = _dn_chunk(q_ref[...], k_ref[...], v_ref[...], z_ref[...], b_ref[...], a_ref[...],
                              alog_ref[...], dtb_ref[...], nw_ref[...], s_in)
        o_ref[...] = on
        s_ref[...] = s_new

    tok = pl.BlockSpec((H, C, D), lambda n: (0, n, 0))
    tok1 = pl.BlockSpec((H, C, 1), lambda n: (0, n, 0))
    par = pl.BlockSpec((H, 1, 1), lambda n: (0, 0, 0))
    outs, gathered, _ = _pcall(
        body, (q, k, v, z, braw, araw, alog, dtb, nw), name="dn_fwd", grid=(N,),
        in_specs=[tok, tok, tok, tok, tok1, tok1, par, par, pl.BlockSpec((1, 1, D), lambda n: (0, 0, 0))],
        out_specs=[tok, pl.BlockSpec((1, H, D, D), lambda n: (n, 0, 0, 0))],
        out_shape=[jax.ShapeDtypeStruct((H, T, D), F32), jax.ShapeDtypeStruct((N, H, D, D), F32)],
        scratch_shapes=[pltpu.VMEM((H, D, D), F32)], sem=("arbitrary",), ag=ag)
    return outs, gathered


def dn_bwd(q, k, v, z, braw, araw, alog, dtb, nw, sall, do, rs=()):
    H, T, D = q.shape
    C = DN_CHUNK
    N = T // C

    def body(q_ref, k_ref, v_ref, z_ref, b_ref, a_ref, alog_ref, dtb_ref, nw_ref, sall_ref, do_ref,
             dq_ref, dk_ref, dv_ref, dz_ref, db_ref, da_ref, dalog_ref, ddtb_ref, dnw_ref, ds_ref):
        @pl.when(pl.program_id(0) == 0)
        def _():
            ds_ref[...] = jnp.zeros_like(ds_ref)
            dalog_ref[...] = jnp.zeros_like(dalog_ref)
            ddtb_ref[...] = jnp.zeros_like(ddtb_ref)
            dnw_ref[...] = jnp.zeros_like(dnw_ref)

        args = (q_ref[...], k_ref[...], v_ref[...], z_ref[...], b_ref[...], a_ref[...],
                alog_ref[...], dtb_ref[...], nw_ref[...], sall_ref[0])
        _, vjp = jax.vjp(_dn_chunk, *args)
        dq, dk, dv, dz, db, da, dalog, ddtb, dnw, ds = vjp((do_ref[...], ds_ref[...]))
        dq_ref[...] = dq
        dk_ref[...] = dk
        dv_ref[...] = dv
        dz_ref[...] = dz
        db_ref[...] = db
        da_ref[...] = da
        dalog_ref[...] += dalog
        ddtb_ref[...] += ddtb
        dnw_ref[...] += dnw
        ds_ref[...] = ds

    rev = lambda i: (0, N - 1 - i, 0)
    tok = pl.BlockSpec((H, C, D), rev)
    tok1 = pl.BlockSpec((H, C, 1), rev)
    par = pl.BlockSpec((H, 1, 1), lambda i: (0, 0, 0))
    nws = pl.BlockSpec((1, 1, D), lambda i: (0, 0, 0))
    outs, _, slots = _pcall(
        body, (q, k, v, z, braw, araw, alog, dtb, nw, sall, do), name="dn_bwd", grid=(N,),
        in_specs=[tok, tok, tok, tok, tok1, tok1, par, par, nws,
                  pl.BlockSpec((1, H, D, D), lambda i: (N - 1 - i, 0, 0, 0)), tok],
        out_specs=[tok, tok, tok, tok, tok1, tok1, par, par, nws],
        out_shape=[jax.ShapeDtypeStruct((H, T, D), F32)] * 4 + [jax.ShapeDtypeStruct((H, T, 1), F32)] * 2
        + [jax.ShapeDtypeStruct((H, 1, 1), F32)] * 2 + [jax.ShapeDtypeStruct((1, 1, D), F32)],
        scratch_shapes=[pltpu.VMEM((H, D, D), F32)], sem=("arbitrary",), rs=rs)
    return outs, slots


def _conv_taps(buf_ref, w, width, halo, tm):
    acc = w[0:1, :] * buf_ref[pl.ds(halo - (width - 1), tm), :]
    for kk in range(1, width):
        acc = acc + w[kk:kk + 1, :] * buf_ref[pl.ds(halo - (width - 1) + kk, tm), :]
    return acc


def _conv_taps_bwd(dbuf_ref, w, width, tm):
    acc = w[0:1, :] * dbuf_ref[pl.ds(width - 1, tm), :]
    for kk in range(1, width):
        acc = acc + w[kk:kk + 1, :] * dbuf_ref[pl.ds(width - 1 - kk, tm), :]
    return acc


def _conv_dw_acc(dw_ref, dout, buf_ref, width, halo, tm):
    for kk in range(width):
        dw_ref[pl.ds(kk, 1), :] += jnp.sum(dout * buf_ref[pl.ds(halo - (width - 1) + kk, tm), :],
                                            axis=0, keepdims=True)


DNC_HALO = 8
DNC_COLS = 768


def dnconv_fwd(proj, w):
    T = proj.shape[0]
    tm = min(T, 256)
    hb = tm // DNC_HALO

    def body(x_ref, h_ref, w_ref, o_ref, buf_ref):
        i = pl.program_id(0)
        buf_ref[0:DNC_HALO, :] = jnp.where(i > 0, h_ref[...], 0.0)
        buf_ref[DNC_HALO:, :] = x_ref[...]
        acc = _conv_taps(buf_ref, w_ref[...], DN_CONV, DNC_HALO, tm)
        o_ref[...] = acc * _sigmoid(acc)

    return pl.pallas_call(
        body, name="dnconv_fwd", grid=(T // tm, 2),
        in_specs=[pl.BlockSpec((tm, DNC_COLS), lambda i, c: (i, 1 + c)),
                  pl.BlockSpec((DNC_HALO, DNC_COLS), lambda i, c: (jnp.maximum(i * hb - 1, 0), 1 + c)),
                  pl.BlockSpec((DN_CONV, DNC_COLS), lambda i, c: (0, c))],
        out_specs=pl.BlockSpec((tm, DNC_COLS), lambda i, c: (i, c)),
        out_shape=jax.ShapeDtypeStruct((T, QKV_B), F32),
        scratch_shapes=[pltpu.VMEM((DNC_HALO + tm, DNC_COLS), F32)],
        compiler_params=_cparams(("parallel", "parallel")),
    )(proj, proj, w)


def dnconv_bwd(proj, w, dout):
    T = proj.shape[0]
    tm = min(T, 256)
    nt = T // tm
    hb = tm // DNC_HALO

    def body(x_ref, h_ref, w_ref, do_ref, dx_ref, dw_ref, buf_ref, dbuf_ref):
        r = pl.program_id(1)
        i = nt - 1 - r

        @pl.when(r == 0)
        def _():
            dw_ref[...] = jnp.zeros_like(dw_ref)
            dbuf_ref[tm:, :] = jnp.zeros((DNC_HALO, DNC_COLS), F32)

        buf_ref[0:DNC_HALO, :] = jnp.where(i > 0, h_ref[...], 0.0)
        buf_ref[DNC_HALO:, :] = x_ref[...]
        wv = w_ref[...]
        acc = _conv_taps(buf_ref, wv, DN_CONV, DNC_HALO, tm)
        sg = _sigmoid(acc)
        dacc = do_ref[...] * (sg * (1.0 + acc * (1.0 - sg)))
        dbuf_ref[0:tm, :] = dacc
        dx_ref[...] = _conv_taps_bwd(dbuf_ref, wv, DN_CONV, tm)
        _conv_dw_acc(dw_ref, dacc, buf_ref, DN_CONV, DNC_HALO, tm)
        dbuf_ref[tm:, :] = dacc[0:DNC_HALO, :]

    return pl.pallas_call(
        body, name="dnconv_bwd", grid=(2, nt),
        in_specs=[pl.BlockSpec((tm, DNC_COLS), lambda c, r: (nt - 1 - r, 1 + c)),
                  pl.BlockSpec((DNC_HALO, DNC_COLS), lambda c, r: (jnp.maximum((nt - 1 - r) * hb - 1, 0), 1 + c)),
                  pl.BlockSpec((DN_CONV, DNC_COLS), lambda c, r: (0, c)),
                  pl.BlockSpec((tm, DNC_COLS), lambda c, r: (nt - 1 - r, c))],
        out_specs=[pl.BlockSpec((tm, DNC_COLS), lambda c, r: (nt - 1 - r, c)),
                   pl.BlockSpec((DN_CONV, DNC_COLS), lambda c, r: (0, c))],
        out_shape=[jax.ShapeDtypeStruct((T, QKV_B), F32), jax.ShapeDtypeStruct((DN_CONV, QKV_B), F32)],
        scratch_shapes=[pltpu.VMEM((DNC_HALO + tm, DNC_COLS), F32), pltpu.VMEM((tm + DNC_HALO, DNC_COLS), F32)],
        compiler_params=_cparams(("parallel", "arbitrary")),
    )(proj, proj, w, dout)


CV_HALO = 32


def _cv_post(cv, lnw, lnb):
    mu = jnp.mean(cv, axis=-1, keepdims=True)
    xc = cv - mu
    y = xc * lax.rsqrt(jnp.mean(xc * xc, axis=-1, keepdims=True) + EPS) * lnw + lnb
    return y * _sigmoid(y)


def cv_fwd(ab, w, bdw, lnw, lnb, ag=()):
    T = ab.shape[0]
    D = ab.shape[1] // 2
    tm = min(T, 256)
    hb = tm // CV_HALO

    def body(a_ref, b_ref, ah_ref, bh_ref, w_ref, bdw_ref, lnw_ref, lnb_ref, o_ref, buf_ref):
        i = pl.program_id(0)
        buf_ref[0:CV_HALO, :] = jnp.where(i > 0, ah_ref[...] * _sigmoid(bh_ref[...]), 0.0)
        buf_ref[CV_HALO:, :] = a_ref[...] * _sigmoid(b_ref[...])
        cv = _conv_taps(buf_ref, w_ref[...], CONV_WIDTH, CV_HALO, tm) + bdw_ref[...]
        o_ref[...] = _cv_post(cv, lnw_ref[...], lnb_ref[...])

    halo = lambda c: pl.BlockSpec((CV_HALO, D), lambda i: (jnp.maximum(i * hb - 1, 0), c))
    vec = pl.BlockSpec((1, D), lambda i: (0, 0))
    (out,), gathered, _ = _pcall(
        body, (ab, ab, ab, ab, w, bdw, lnw, lnb), name="cv_fwd", grid=(T // tm,),
        in_specs=[pl.BlockSpec((tm, D), lambda i: (i, 0)), pl.BlockSpec((tm, D), lambda i: (i, 1)),
                  halo(0), halo(1), pl.BlockSpec((CONV_WIDTH, D), lambda i: (0, 0)), vec, vec, vec],
        out_specs=[pl.BlockSpec((tm, D), lambda i: (i, 0))],
        out_shape=[jax.ShapeDtypeStruct((T, D), F32)],
        scratch_shapes=[pltpu.VMEM((CV_HALO + tm, D), F32)], sem=("arbitrary",), ag=ag)
    return out, gathered


def cv_bwd(ab, w, bdw, lnw, lnb, dout, rs=()):
    T = ab.shape[0]
    D = ab.shape[1] // 2
    tm = min(T, 256)
    nt = T // tm
    hb = tm // CV_HALO

    def body(a_ref, b_ref, ah_ref, bh_ref, w_ref, bdw_ref, lnw_ref, lnb_ref, do_ref,
             da_ref, db_ref, dw_ref, dbdw_ref, dlnw_ref, dlnb_ref, buf_ref, dbuf_ref):
        r = pl.program_id(0)
        i = nt - 1 - r

        @pl.when(r == 0)
        def _():
            dw_ref[...] = jnp.zeros_like(dw_ref)
            dbdw_ref[...] = jnp.zeros_like(dbdw_ref)
            dlnw_ref[...] = jnp.zeros_like(dlnw_ref)
            dlnb_ref[...] = jnp.zeros_like(dlnb_ref)
            dbuf_ref[tm:, :] = jnp.zeros((CV_HALO, D), F32)

        a = a_ref[...]
        sb = _sigmoid(b_ref[...])
        buf_ref[0:CV_HALO, :] = jnp.where(i > 0, ah_ref[...] * _sigmoid(bh_ref[...]), 0.0)
        buf_ref[CV_HALO:, :] = a * sb
        wv = w_ref[...]
        cv = _conv_taps(buf_ref, wv, CONV_WIDTH, CV_HALO, tm) + bdw_ref[...]
        _, vjp = jax.vjp(_cv_post, cv, lnw_ref[...], lnb_ref[...])
        dcv, dlnw, dlnb = vjp(do_ref[...])
        dlnw_ref[...] += dlnw
        dlnb_ref[...] += dlnb
        dbdw_ref[...] += jnp.sum(dcv, axis=0, keepdims=True)
        dbuf_ref[0:tm, :] = dcv
        du = _conv_taps_bwd(dbuf_ref, wv, CONV_WIDTH, tm)
        _conv_dw_acc(dw_ref, dcv, buf_ref, CONV_WIDTH, CV_HALO, tm)
        dbuf_ref[tm:, :] = dcv[0:CV_HALO, :]
        da_ref[...] = du * sb
        db_ref[...] = du * a * sb * (1.0 - sb)

    tile = lambda c: pl.BlockSpec((tm, D), lambda r: (nt - 1 - r, c))
    halo = lambda c: pl.BlockSpec((CV_HALO, D), lambda r: (jnp.maximum((nt - 1 - r) * hb - 1, 0), c))
    vec = pl.BlockSpec((1, D), lambda r: (0, 0))
    wsp = pl.BlockSpec((CONV_WIDTH, D), lambda r: (0, 0))
    (da, db, dw, dbdw, dlnw, dlnb), _, slots = _pcall(
        body, (ab, ab, ab, ab, w, bdw, lnw, lnb, dout), name="cv_bwd", grid=(nt,),
        in_specs=[tile(0), tile(1), halo(0), halo(1), wsp, vec, vec, vec, tile(0)],
        out_specs=[tile(0), tile(0), wsp, vec, vec, vec],
        out_shape=[jax.ShapeDtypeStruct((T, D), F32), jax.ShapeDtypeStruct((T, D), F32),
                   jax.ShapeDtypeStruct((CONV_WIDTH, D), F32)] + [jax.ShapeDtypeStruct((1, D), F32)] * 3,
        scratch_shapes=[pltpu.VMEM((CV_HALO + tm, D), F32), pltpu.VMEM((tm + CV_HALO, D), F32)],
        sem=("arbitrary",), rs=rs)
    return (jnp.concatenate([da, db], axis=1), dw, dbdw, dlnw, dlnb), slots


def adamw(w, m, v, slots):
    L, R, C = w.shape
    tr = max([d for d in range(16, 257, 16) if R % d == 0 and N_DEV * d * C * 2 <= 2**19], default=R)
    c1 = 1.0 / (1.0 - ADAM_B1 ** ADAM_STEP)
    c2 = 1.0 / (1.0 - ADAM_B2 ** ADAM_STEP)

    def body(w_ref, m_ref, v_ref, *rest):
        s_refs = rest[:L]
        g_ref, d_ref, nm_ref, nv_ref = rest[L:]
        l = pl.program_id(0)
        for k in range(L):
            @pl.when(l == k)
            def _(s_ref=s_refs[k]):
                g = s_ref[0].astype(F32)
                for j in range(1, N_DEV):
                    g = g + s_ref[j].astype(F32)
                nm = ADAM_B1 * m_ref[0] + (1.0 - ADAM_B1) * g
                nv = ADAM_B2 * v_ref[0] + (1.0 - ADAM_B2) * (g * g)
                g_ref[0] = g
                nm_ref[0] = nm
                nv_ref[0] = nv
                d_ref[0] = -ADAM_LR * ((nm * c1) / (jnp.sqrt(nv * c2) + ADAM_EPS) + ADAM_WD * w_ref[0])

    blk = pl.BlockSpec((1, tr, C), lambda l, r: (l, r, 0))
    slot = lambda k: pl.BlockSpec((N_DEV, tr, C), lambda l, r: (0, jnp.where(l == k, r, 0), 0))
    return pl.pallas_call(
        body, name="adamw", grid=(L, R // tr),
        in_specs=[blk, blk, blk] + [slot(k) for k in range(L)],
        out_specs=[blk, blk, blk, blk],
        out_shape=[jax.ShapeDtypeStruct((L, R, C), F32)] * 4,
        compiler_params=_cparams(("arbitrary", "arbitrary")),
    )(w, m, v, *slots)


def _unshard(g, axis):
    g = jnp.moveaxis(g, 0, axis)
    s = g.shape
    return g.reshape(s[:axis] + (s[axis] * s[axis + 1],) + s[axis + 2:])


def _to_blocks(full, axis):
    s = full.shape
    g = full.reshape(s[:axis] + (N_DEV, s[axis] // N_DEV) + s[axis + 1:])
    return jnp.moveaxis(g, axis, 0)


def _heads(a, h):
    T = a.shape[0]
    return a.reshape(T, h, a.shape[1] // h).transpose(1, 0, 2)


def _unheads(a):
    h, T, d = a.shape
    return a.transpose(1, 0, 2).reshape(T, h * d)


SMALL = (("norm_w", 2), ("dn_conv_w", 2), ("conv_b_pw1", 1), ("conv_w_dw", 2), ("conv_b_dw", 1),
         ("conv_ln_w", 1), ("conv_ln_b", 1), ("conv_b_pw2", 1),
         ("attn_sinks", None), ("dn_a_log", None), ("dn_dt_bias", None), ("dn_norm_w", None), ("final_norm_w", None))

def _pack(parts):
    flat = jnp.concatenate([p.reshape(-1) for p in parts])
    pad = (-flat.shape[0]) % LANES
    return jnp.pad(flat, (0, pad))


def _unpack(flat, shapes):
    out, off = [], 0
    for s in shapes:
        n = int(np.prod(s))
        out.append(flat[off:off + n].reshape(s))
        off += n
    return out


def kernel(x, norm_w, ffn_w_gate, ffn_w_up, ffn_w_down, mix_w_in, dn_conv_w, attn_sinks, dn_a_log, dn_dt_bias, dn_norm_w, mix_w_out, conv_w_pw1, conv_b_pw1, conv_w_dw, conv_b_dw, conv_ln_w, conv_ln_b, conv_w_pw2, conv_b_pw2, final_norm_w, loss_target, m_norm_w, m_ffn_w_gate, m_ffn_w_up, m_ffn_w_down, m_mix_w_in, m_dn_conv_w, m_attn_sinks, m_dn_a_log, m_dn_dt_bias, m_dn_norm_w, m_mix_w_out, m_conv_w_pw1, m_conv_b_pw1, m_conv_w_dw, m_conv_b_dw, m_conv_ln_w, m_conv_ln_b, m_conv_w_pw2, m_conv_b_pw2, m_final_norm_w, v_norm_w, v_ffn_w_gate, v_ffn_w_up, v_ffn_w_down, v_mix_w_in, v_dn_conv_w, v_attn_sinks, v_dn_a_log, v_dn_dt_bias, v_dn_norm_w, v_mix_w_out, v_conv_w_pw1, v_conv_b_pw1, v_conv_w_dw, v_conv_b_dw, v_conv_ln_w, v_conv_ln_b, v_conv_w_pw2, v_conv_b_pw2, v_final_norm_w):
    W = dict(norm_w=norm_w, ffn_w_gate=ffn_w_gate, ffn_w_up=ffn_w_up, ffn_w_down=ffn_w_down, mix_w_in=mix_w_in,
             dn_conv_w=dn_conv_w, attn_sinks=attn_sinks, dn_a_log=dn_a_log, dn_dt_bias=dn_dt_bias,
             dn_norm_w=dn_norm_w, mix_w_out=mix_w_out, conv_w_pw1=conv_w_pw1, conv_b_pw1=conv_b_pw1,
             conv_w_dw=conv_w_dw, conv_b_dw=conv_b_dw, conv_ln_w=conv_ln_w, conv_ln_b=conv_ln_b,
             conv_w_pw2=conv_w_pw2, conv_b_pw2=conv_b_pw2, final_norm_w=final_norm_w)
    M = dict(norm_w=m_norm_w, ffn_w_gate=m_ffn_w_gate, ffn_w_up=m_ffn_w_up, ffn_w_down=m_ffn_w_down,
             mix_w_in=m_mix_w_in, dn_conv_w=m_dn_conv_w, attn_sinks=m_attn_sinks, dn_a_log=m_dn_a_log,
             dn_dt_bias=m_dn_dt_bias, dn_norm_w=m_dn_norm_w, mix_w_out=m_mix_w_out, conv_w_pw1=m_conv_w_pw1,
             conv_b_pw1=m_conv_b_pw1, conv_w_dw=m_conv_w_dw, conv_b_dw=m_conv_b_dw, conv_ln_w=m_conv_ln_w,
             conv_ln_b=m_conv_ln_b, conv_w_pw2=m_conv_w_pw2, conv_b_pw2=m_conv_b_pw2, final_norm_w=m_final_norm_w)
    V = dict(norm_w=v_norm_w, ffn_w_gate=v_ffn_w_gate, ffn_w_up=v_ffn_w_up, ffn_w_down=v_ffn_w_down,
             mix_w_in=v_mix_w_in, dn_conv_w=v_dn_conv_w, attn_sinks=v_attn_sinks, dn_a_log=v_dn_a_log,
             dn_dt_bias=v_dn_dt_bias, dn_norm_w=v_dn_norm_w, mix_w_out=v_mix_w_out, conv_w_pw1=v_conv_w_pw1,
             conv_b_pw1=v_conv_b_pw1, conv_w_dw=v_conv_w_dw, conv_b_dw=v_conv_b_dw, conv_ln_w=v_conv_ln_w,
             conv_ln_b=v_conv_ln_b, conv_w_pw2=v_conv_w_pw2, conv_b_pw2=v_conv_b_pw2, final_norm_w=v_final_norm_w)

    T, D = x.shape[1], x.shape[2]
    xs = x[0]
    F8 = ffn_w_gate.shape[-1]
    n_ffn = DEPTH * 2

    big = ("ffn_w_gate", "ffn_w_up", "ffn_w_down", "mix_w_in", "mix_w_out", "conv_w_pw1", "conv_w_pw2")
    shard3 = {k: W[k].reshape((-1,) + W[k].shape[-2:]) for k in big}
    shard_bf = {k: shard3[k].astype(BF16) for k in big}
    ffn_unit = lambda i: [("ffn_w_gate", i), ("ffn_w_up", i), ("ffn_w_down", i)]
    even_unit = lambda e: [("mix_w_in", e), ("mix_w_out", e)]
    odd_unit = lambda e: [("conv_w_pw1", e), ("conv_w_pw2", e)]
    have = {}

    def ag_jobs(units):
        return [(shard_bf[k], i) for k, i in units]

    def ag_done(units, gathered):
        have.update(zip(units, gathered))

    small_sharded = [(k, ax) for k, ax in SMALL if ax is not None]
    small_pack = _pack([W[k] for k, _ in small_sharded])[None, :]
    first_units = ffn_unit(0) + even_unit(0)
    gathered, _ = exchange(ag=ag_jobs(first_units) + [(small_pack, None)])
    ag_done(first_units, gathered[:-1])
    small_full = {}
    for (k, ax), parts in zip(small_sharded,
                              zip(*[_unpack(gathered[-1][s, 0], [W[k].shape for k, _ in small_sharded])
                                    for s in range(N_DEV)])):
        small_full[k] = _unshard(jnp.stack(parts), ax)
    nw_full = small_full["norm_w"]

    ffn_w = lambda i: [have[u] for u in ffn_unit(i)]
    w_in_of = lambda e: jnp.pad(_unshard(have[("mix_w_in", e)], 1), ((0, 0), (0, IN_COLS_PAD - IN_COLS)))
    w_out_of = lambda e: have[("mix_w_out", e)].reshape(D, D)
    w_pw1_of = lambda e: _unshard(have[("conv_w_pw1", e)], 1)
    w_pw2_of = lambda e: have[("conv_w_pw2", e)].reshape(D, D)
    fwd_carry = {("F", 0): ffn_unit(1), ("M", 0): ffn_unit(2) + odd_unit(0), ("F", 1): ffn_unit(3),
                 ("F", 2): ffn_unit(4), ("M", 1): even_unit(1), ("F", 3): ffn_unit(5), ("F", 4): ffn_unit(6),
                 ("M", 2): ffn_unit(7) + odd_unit(1)}
    zero_in = jnp.zeros((1, IN_COLS_PAD), F32)
    zero_d = jnp.zeros((1, D), F32)
    slope_rows = jnp.asarray(np.repeat(2.0 ** (-8.0 * np.arange(1, ATTN_HEADS + 1) / ATTN_HEADS), ATTN_BLOCK)
                             .astype(np.float32)[:, None])

    saved = []
    h = xs
    w_in, w_out, w_pw1, w_pw2 = {}, {}, {}, {}

    def ffn_forward(h, l, half):
        i = 2 * l + half
        units = fwd_carry.get(("F", i), [])
        h, gathered = ffn_fwd(h, nw_full[l, 2 * half][None], *ffn_w(i), ag=ag_jobs(units))
        ag_done(units, gathered)
        return h

    for l in range(DEPTH):
        e = l // 2
        st = {"x0": h}
        h = ffn_forward(h, l, 0)
        st["x1"] = h
        units = fwd_carry.get(("M", l), [])
        if l % 2 == 0:
            w_in[e], w_out[e] = w_in_of(e), w_out_of(e)
            proj = rmslin_fwd(h, nw_full[l, 1][None], w_in[e], zero_in)
            qkvc = dnconv_fwd(proj, small_full["dn_conv_w"][e])
            st["qa"] = _heads(proj[:, :Q_A], ATTN_HEADS)
            st["ka"] = _heads(proj[:, Q_A:Q_A + KV_A], ATTN_KV_HEADS)
            st["va"] = _heads(proj[:, Q_A + KV_A:OFF_QKVB], ATTN_KV_HEADS)
            st["qb"] = _heads(qkvc[:, :512], DN_HEADS)
            st["kb"] = _heads(qkvc[:, 512:1024], DN_HEADS)
            st["vb"] = _heads(qkvc[:, 1024:], DN_HEADS)
            st["zb"] = _heads(proj[:, OFF_Z:OFF_BETA], DN_HEADS)
            st["braw"] = proj[:, OFF_BETA:OFF_A].T[:, :, None]
            st["araw"] = proj[:, OFF_A:IN_COLS].T[:, :, None]
            st["sink_rows"] = jnp.repeat(attn_sinks[e], ATTN_BLOCK)[:, None]
            st["alog"] = dn_a_log[e].reshape(DN_HEADS, 1, 1)
            st["dtb"] = dn_dt_bias[e].reshape(DN_HEADS, 1, 1)
            st["dnw"] = dn_norm_w[e].reshape(1, 1, DN_D)
            att = attn_fwd(st["qa"], st["ka"], st["va"], st["sink_rows"], slope_rows)
            (og, st["sall"]), gathered = dn_fwd(st["qb"], st["kb"], st["vb"], st["zb"], st["braw"], st["araw"],
                                                st["alog"], st["dtb"], st["dnw"], ag=ag_jobs(units))
            ag_done(units, gathered)
            st["proj"] = proj
            st["mix"] = jnp.concatenate([_unheads(att), _unheads(og)], axis=1)
            h = lin_fwd(h, st["mix"], w_out[e], zero_d)
        else:
            w_pw1[e], w_pw2[e] = w_pw1_of(e), w_pw2_of(e)
            st["ab"] = rmslin_fwd(h, nw_full[l, 1][None], w_pw1[e], small_full["conv_b_pw1"][e][None])
            st["act"], gathered = cv_fwd(st["ab"], small_full["conv_w_dw"][e], small_full["conv_b_dw"][e][None],
                                         small_full["conv_ln_w"][e][None], small_full["conv_ln_b"][e][None],
                                         ag=ag_jobs(units))
            ag_done(units, gathered)
            h = lin_fwd(h, st["act"], w_pw2[e], small_full["conv_b_pw2"][e][None])
        st["x2"] = h
        h = ffn_forward(h, l, 1)
        saved.append(st)

    loss_part, dh, dfinal = loss_fwd_bwd(h, final_norm_w[None], loss_target[0])
    loss = lax.psum(loss_part[0, 0], ("x", "y", "c"))

    d_norm = [[None] * 3 for _ in range(DEPTH)]
    d_small = {k: [None, None] for k in ("dn_conv_w", "conv_b_pw1", "conv_w_dw", "conv_b_dw", "conv_ln_w",
                                         "conv_ln_b", "conv_b_pw2", "attn_sinks", "dn_a_log", "dn_dt_bias",
                                         "dn_norm_w")}
    pending, slot = [], {}

    def take_pending():
        units = list(pending)
        pending.clear()
        return [u for u, _ in units], [b for _, b in units]

    def ffn_backward(dh, l, half):
        i = 2 * l + half
        units, blocks = take_pending()
        (dh, dg, du, dd, d_norm[l][2 * half]), slots = ffn_bwd(
            st["x2" if half else "x0"], dh, nw_full[l, 2 * half][None], *ffn_w(i), rs=blocks)
        slot.update(zip(units, slots))
        pending.extend(zip(ffn_unit(i), (dg, du, dd)))
        return dh

    for l in reversed(range(DEPTH)):
        e = l // 2
        st = saved[l]
        dh = ffn_backward(dh, l, 1)
        if l % 2 == 0:
            dmix, d_out, _ = lin_bwd(st["mix"], dh, w_out[e])
            pending.append((("mix_w_out", e), d_out.reshape(N_DEV, D // N_DEV, D).astype(BF16)))
            units, blocks = take_pending()
            (dqb, dkb, dvb, dzb, dbraw, daraw, dalog, ddtb, ddnw), slots = dn_bwd(
                st["qb"], st["kb"], st["vb"], st["zb"], st["braw"], st["araw"], st["alog"], st["dtb"], st["dnw"],
                st["sall"], _heads(dmix[:, Q_A:], DN_HEADS), rs=blocks)
            slot.update(zip(units, slots))
            dqa, dka, dva, dsink = attn_bwd(st["qa"], st["ka"], st["va"], st["sink_rows"], slope_rows,
                                            _heads(dmix[:, :Q_A], ATTN_HEADS))
            dqkv, d_small["dn_conv_w"][e] = dnconv_bwd(
                st["proj"], small_full["dn_conv_w"][e],
                jnp.concatenate([_unheads(dqb), _unheads(dkb), _unheads(dvb)], axis=1))
            dproj = jnp.concatenate(
                [_unheads(dqa), _unheads(dka), _unheads(dva), dqkv, _unheads(dzb), dbraw[:, :, 0].T,
                 daraw[:, :, 0].T, jnp.zeros((T, IN_COLS_PAD - IN_COLS), F32)], axis=1)
            dh, d_in, _, d_norm[l][1] = rmslin_bwd(st["x1"], dh, dproj, nw_full[l, 1][None], w_in[e])
            pending.append((("mix_w_in", e), _to_blocks(d_in[:, :IN_COLS], 1).astype(BF16)))
            d_small["attn_sinks"][e] = jnp.sum(dsink.reshape(ATTN_HEADS, ATTN_BLOCK), axis=1)
            d_small["dn_a_log"][e] = dalog.reshape(DN_HEADS)
            d_small["dn_dt_bias"][e] = ddtb.reshape(DN_HEADS)
            d_small["dn_norm_w"][e] = ddnw.reshape(DN_D)
        else:
            dact, d_pw2, d_small["conv_b_pw2"][e] = lin_bwd(st["act"], dh, w_pw2[e])
            pending.append((("conv_w_pw2", e), d_pw2.reshape(N_DEV, D // N_DEV, D).astype(BF16)))
            units, blocks = take_pending()
            (dab, d_small["conv_w_dw"][e], d_small["conv_b_dw"][e], d_small["conv_ln_w"][e],
             d_small["conv_ln_b"][e]), slots = cv_bwd(
                st["ab"], small_full["conv_w_dw"][e], small_full["conv_b_dw"][e][None],
                small_full["conv_ln_w"][e][None], small_full["conv_ln_b"][e][None], dact, rs=blocks)
            slot.update(zip(units, slots))
            dh, d_pw1, d_small["conv_b_pw1"][e], d_norm[l][1] = rmslin_bwd(
                st["x1"], dh, dab, nw_full[l, 1][None], w_pw1[e])
            pending.append((("conv_w_pw1", e), _to_blocks(d_pw1, 1).astype(BF16)))
        dh = ffn_backward(dh, l, 0)
    grad_x = dh[None]

    full_small = {"norm_w": jnp.stack([jnp.concatenate(r, axis=0) for r in d_norm]),
                  "final_norm_w": dfinal[0]}
    for k, pair in d_small.items():
        full_small[k] = jnp.stack([p.reshape(W[k].shape[1:-1] + (-1,)) if SMALL_AXIS[k] is not None
                                   else p for p in pair])
    rows = []
    for s in range(N_DEV):
        parts = [_to_blocks(full_small[k], ax)[s] if ax is not None else full_small[k] for k, ax in SMALL]
        rows.append(_pack(parts))
    send_small = jnp.stack(rows)[:, None, :]
    units, blocks = take_pending()
    _, slots = exchange(rs=blocks + [send_small])
    slot.update(zip(units, slots[:-1]))

    res = {}
    for k in big:
        outs = adamw(shard3[k], M[k].reshape(shard3[k].shape), V[k].reshape(shard3[k].shape),
                     [slot[(k, i)] for i in range(shard3[k].shape[0])])
        res[k] = [o.reshape(W[k].shape) for o in outs]
    pk = lambda d: _pack([d[k] for k, _ in SMALL])[None, None, :]
    outs = adamw(pk(W), pk(M), pk(V), [slots[-1]])
    shapes = [W[k].shape for k, _ in SMALL]
    unp = [_unpack(o[0, 0], shapes) for o in outs]
    for i, (k, _) in enumerate(SMALL):
        res[k] = [u[i] for u in unp]

    order = ("norm_w", "ffn_w_gate", "ffn_w_up", "ffn_w_down", "mix_w_in", "dn_conv_w", "attn_sinks", "dn_a_log",
             "dn_dt_bias", "dn_norm_w", "mix_w_out", "conv_w_pw1", "conv_b_pw1", "conv_w_dw", "conv_b_dw",
             "conv_ln_w", "conv_ln_b", "conv_w_pw2", "conv_b_pw2", "final_norm_w")
    return (loss, grad_x, *[res[k][0] for k in order], *[res[k][1] for k in order],
            *[res[k][2] for k in order], *[res[k][3] for k in order])


SMALL_AXIS = dict(SMALL)
```

```python
import functools

import numpy as np
import jax
import jax.numpy as jnp
from jax import lax
from jax.experimental import pallas as pl
from jax.experimental.pallas import tpu as pltpu

F32 = jnp.float32
BF16 = jnp.bfloat16
HI = lax.Precision.HIGHEST
EPS = 1e-6
N_DEV = 8
V7X_VMEM_LIMIT = 56 * 2**20
MESH = pl.DeviceIdType.MESH
LANES = 128
SUBLANES = 8

DEPTH = 4
D_MODEL = 1024
ATTN_HEADS, ATTN_KV_HEADS, HEAD_DIM, ATTN_BLOCK = 8, 2, 64, 128
DN_HEADS, DN_D, DN_CHUNK, DN_CONV = 8, 64, 64, 4
CONV_WIDTH = 31
Q_A, KV_A, QKV_B, V_B = 512, 128, 1536, 512
IN_COLS = 2832
IN_COLS_PAD = 3072
OFF_QKVB = Q_A + 2 * KV_A
OFF_Z = OFF_QKVB + QKV_B
OFF_BETA = OFF_Z + V_B
OFF_A = OFF_BETA + DN_HEADS

FWD_CARRY_BYTES = {"F": 12 * 2**20, "E": 20 * 2**20, "O": 9 * 2**20}
BWD_CARRY_BYTES = {"F": 14 * 2**20, "E": 20 * 2**20, "O": 19 * 2**20}

ADAM_LR, ADAM_B1, ADAM_B2, ADAM_EPS, ADAM_WD, ADAM_STEP = 0.001, 0.9, 0.999, 1e-08, 0.01, 10


def _cparams(sem):
    return pltpu.CompilerParams(dimension_semantics=sem, vmem_limit_bytes=V7X_VMEM_LIMIT)


def _sigmoid(x):
    return 1.0 / (1.0 + jnp.exp(-x))


def _softplus(x):
    return jnp.maximum(x, 0.0) + jnp.log(1.0 + jnp.exp(-jnp.abs(x)))


def _dot(a, b):
    return jnp.dot(a, b, preferred_element_type=F32)


def _dot_nt(a, b):
    return lax.dot_general(a, b, (((1,), (1,)), ((), ())), preferred_element_type=F32)


def _dot_tn(a, b):
    return lax.dot_general(a, b, (((0,), (0,)), ((), ())), preferred_element_type=F32)


def _rms(x, w):
    return x * lax.rsqrt(jnp.mean(x * x, axis=-1, keepdims=True) + EPS) * w


def _rms_bwd(x, w, dxn):
    r = lax.rsqrt(jnp.mean(x * x, axis=-1, keepdims=True) + EPS)
    xh = x * r
    dxh = dxn * w
    dx = r * (dxh - xh * jnp.mean(dxh * xh, axis=-1, keepdims=True))
    return dx, jnp.sum(dxn * xh, axis=0, keepdims=True)


def _position():
    return lax.axis_index("x"), lax.axis_index("y"), lax.axis_index("c")


def _dev_index(px, py, pc):
    return 4 * px + 2 * py + pc


def _rcopy(src, dst, send_sem, recv_sem, to):
    return pltpu.make_async_remote_copy(src_ref=src, dst_ref=dst, send_sem=send_sem, recv_sem=recv_sem,
                                        device_id=to, device_id_type=MESH)


def _ag_start(srcs, outs, send, recv, local):
    x, y, c = _position()
    me = _dev_index(x, y, c)
    chips = [(1 - x, y), (x, 1 - y), (1 - x, 1 - y)]
    for a, (src, out) in enumerate(zip(srcs, outs)):
        pltpu.make_async_copy(src, out.at[me], local.at[a]).start()
        _rcopy(src, out.at[me], send.at[a, 0], recv.at[a, 0], (x, y, 1 - c)).start()
        for j, chip in enumerate(chips):
            _rcopy(src, out.at[me], send.at[a, 1 + j], recv.at[a, 1 + j], (*chip, c)).start()


def _ag_finish(srcs, outs, send, recv, local):
    x, y, c = _position()
    me = _dev_index(x, y, c)
    sibling = (x, y, 1 - c)
    chips = [(1 - x, y), (x, 1 - y), (1 - x, 1 - y)]
    for j, chip in enumerate(chips):
        for a, out in enumerate(outs):
            blk = out.at[_dev_index(*chip, c)]
            _rcopy(blk, blk, send.at[a, 1 + j], recv.at[a, 1 + j], (x, y, c)).wait_recv()
            _rcopy(blk, blk, send.at[a, 4 + j], recv.at[a, 4 + j], sibling).start()
    for a, (src, out) in enumerate(zip(srcs, outs)):
        blk = out.at[_dev_index(x, y, 1 - c)]
        _rcopy(blk, blk, send.at[a, 0], recv.at[a, 0], (x, y, c)).wait_recv()
        for j, chip in enumerate(chips):
            blk = out.at[_dev_index(*chip, 1 - c)]
            _rcopy(blk, blk, send.at[a, 4 + j], recv.at[a, 4 + j], (x, y, c)).wait_recv()
        for k in range(N_DEV - 1):
            _rcopy(out.at[me], out.at[me], send.at[a, k], recv.at[a, k], (x, y, c)).wait_send()
        pltpu.make_async_copy(src, out.at[me], local.at[a]).wait()


def _rs_peer(r):
    x, y, c = _position()
    return x ^ ((r >> 2) & 1), y ^ ((r >> 1) & 1), c ^ (r & 1)


def _rs_start(ins, outs, send, recv, local):
    me = _dev_index(*_position())
    for a, (src, out) in enumerate(zip(ins, outs)):
        pltpu.make_async_copy(src.at[me], out.at[me], local.at[a]).start()
        for r in range(1, N_DEV):
            p = _rs_peer(r)
            _rcopy(src.at[_dev_index(*p)], out.at[me], send.at[a, r - 1], recv.at[a, r - 1], p).start()


def _rs_finish(ins, outs, send, recv, local):
    pos = _position()
    me = _dev_index(*pos)
    for a, (src, out) in enumerate(zip(ins, outs)):
        for r in range(1, N_DEV):
            blk = out.at[_dev_index(*_rs_peer(r))]
            _rcopy(blk, blk, send.at[a, r - 1], recv.at[a, r - 1], pos).wait_recv()
        for r in range(1, N_DEV):
            _rcopy(src.at[me], out.at[me], send.at[a, r - 1], recv.at[a, r - 1], pos).wait_send()
        pltpu.make_async_copy(src.at[me], out.at[me], local.at[a]).wait()


def _pcall(body, args, *, name, grid, in_specs, out_specs, out_shape, sem, scratch_shapes=(), ag=(), rs=()):
    na, nr = len(ag), len(rs)
    if na + nr == 0:
        outs = pl.pallas_call(body, name=name, grid=grid, in_specs=in_specs, out_specs=out_specs,
                              out_shape=out_shape, scratch_shapes=list(scratch_shapes),
                              compiler_params=_cparams(sem))(*args)
        return list(outs), [], []
    n_in, n_out, n_scr = len(in_specs), len(out_specs), len(scratch_shapes)
    ag_idx = [i for _, i in ag]

    def wrapped(*refs):
        cin, refs = refs[:n_in], refs[n_in:]
        ag_in, refs = refs[:na], refs[na:]
        rs_in, refs = refs[:nr], refs[nr:]
        cout, refs = refs[:n_out], refs[n_out:]
        ag_out, refs = refs[:na], refs[na:]
        rs_out, refs = refs[:nr], refs[nr:]
        cscr, sems = refs[:n_scr], refs[n_scr:]
        ag_src = [r if i is None else r.at[i] for r, i in zip(ag_in, ag_idx)]
        ids = [pl.program_id(d) for d in range(len(grid))]
        first = functools.reduce(jnp.logical_and, [i == 0 for i in ids])
        last = functools.reduce(jnp.logical_and, [i == g - 1 for i, g in zip(ids, grid)])

        @pl.when(first)
        def _():
            if na:
                _ag_start(ag_src, ag_out, *sems[:3])
            if nr:
                _rs_start(rs_in, rs_out, *sems[-3:])

        body(*cin, *cout, *cscr)

        @pl.when(last)
        def _():
            if na:
                _ag_finish(ag_src, ag_out, *sems[:3])
            if nr:
                _rs_finish(rs_in, rs_out, *sems[-3:])

    hbm = pl.BlockSpec(memory_space=pl.ANY)
    sem_shapes = []
    for n in (na, nr):
        if n:
            sem_shapes += [pltpu.SemaphoreType.DMA((n, N_DEV - 1)), pltpu.SemaphoreType.DMA((n, N_DEV - 1)),
                           pltpu.SemaphoreType.DMA((n,))]
    outs = pl.pallas_call(
        wrapped, name=name, grid=grid,
        in_specs=list(in_specs) + [hbm] * (na + nr),
        out_specs=list(out_specs) + [hbm] * (na + nr),
        out_shape=list(out_shape)
        + [jax.ShapeDtypeStruct((N_DEV,) + a.shape[-2:], a.dtype) for a, _ in ag]
        + [jax.ShapeDtypeStruct(b.shape, b.dtype) for b in rs],
        scratch_shapes=list(scratch_shapes) + sem_shapes,
        compiler_params=_cparams(sem),
    )(*args, *[a for a, _ in ag], *rs)
    return list(outs[:n_out]), list(outs[n_out:n_out + na]), list(outs[n_out + na:])


def exchange(ag=(), rs=()):
    def body(o_ref):
        o_ref[...] = jnp.zeros_like(o_ref)

    _, gathered, slots = _pcall(body, (), name="exchange", grid=(1,), in_specs=[],
                                out_specs=[pl.BlockSpec((8, LANES), lambda i: (0, 0))],
                                out_shape=[jax.ShapeDtypeStruct((8, LANES), F32)], sem=("arbitrary",), ag=ag, rs=rs)
    return gathered, slots


def ffn_fwd(x, nw, wg, wu, wd, ag=()):
    T, D = x.shape
    J, _, F = wg.shape
    tm = min(T, 1024)

    def body(x_ref, nw_ref, wg_ref, wu_ref, wd_ref, o_ref, xn_ref, acc_ref):
        j = pl.program_id(1)

        @pl.when(j == 0)
        def _():
            xn_ref[...] = _rms(x_ref[...], nw_ref[...]).astype(BF16)
            acc_ref[...] = jnp.zeros_like(acc_ref)

        xn = xn_ref[...]
        g = _dot(xn, wg_ref[0])
        u = _dot(xn, wu_ref[0])
        h = (g * _sigmoid(g) * u).astype(BF16)
        acc_ref[...] += _dot(h, wd_ref[0])

        @pl.when(j == J - 1)
        def _():
            o_ref[...] = x_ref[...] + 0.5 * acc_ref[...]

    (out,), gathered, _ = _pcall(
        body, (x, nw, wg, wu, wd), name="ffn_fwd", grid=(T // tm, J),
        in_specs=[pl.BlockSpec((tm, D), lambda t, j: (t, 0)),
                  pl.BlockSpec((1, D), lambda t, j: (0, 0)),
                  pl.BlockSpec((1, D, F), lambda t, j: (j, 0, 0)),
                  pl.BlockSpec((1, D, F), lambda t, j: (j, 0, 0)),
                  pl.BlockSpec((1, F, D), lambda t, j: (j, 0, 0))],
        out_specs=[pl.BlockSpec((tm, D), lambda t, j: (t, 0))],
        out_shape=[jax.ShapeDtypeStruct((T, D), F32)],
        scratch_shapes=[pltpu.VMEM((tm, D), BF16), pltpu.VMEM((tm, D), F32)],
        sem=("arbitrary", "arbitrary"), ag=ag)
    return out, gathered


def ffn_bwd(x, dy, nw, wg, wu, wd, rs=()):
    T, D = x.shape
    J, _, F = wg.shape
    tm = min(T, 512)
    nt = T // tm

    def body(x_ref, dy_ref, nw_ref, wg_ref, wu_ref, wd_ref,
             dx_ref, dwg_ref, dwu_ref, dwd_ref, dnw_ref,
             xn_ref, dyh_ref, dxn_ref, awg_ref, awu_ref, awd_ref):
        j = pl.program_id(0)
        t = pl.program_id(1)
        rows = pl.ds(pl.multiple_of(t * tm, tm), tm)

        @pl.when(j == 0)
        def _():
            xn_ref[rows, :] = _rms(x_ref[...], nw_ref[...]).astype(BF16)
            dyh_ref[rows, :] = (0.5 * dy_ref[...]).astype(BF16)
            dxn_ref[rows, :] = jnp.zeros((tm, D), F32)

        @pl.when((j == 0) & (t == 0))
        def _():
            dnw_ref[...] = jnp.zeros_like(dnw_ref)

        @pl.when(t == 0)
        def _():
            awg_ref[...] = jnp.zeros_like(awg_ref)
            awu_ref[...] = jnp.zeros_like(awu_ref)
            awd_ref[...] = jnp.zeros_like(awd_ref)

        xn = xn_ref[rows, :]
        dyh = dyh_ref[rows, :]
        g = _dot(xn, wg_ref[0])
        u = _dot(xn, wu_ref[0])
        sg = _sigmoid(g)
        s = g * sg
        h = (s * u).astype(BF16)
        dh = _dot_nt(dyh, wd_ref[0])
        du = (dh * s).astype(BF16)
        dg = (dh * u * (sg * (1.0 + g * (1.0 - sg)))).astype(BF16)
        awd_ref[...] += _dot_tn(h, dyh)
        awg_ref[...] += _dot_tn(xn, dg)
        awu_ref[...] += _dot_tn(xn, du)
        dxn_ref[rows, :] += _dot_nt(dg, wg_ref[0]) + _dot_nt(du, wu_ref[0])

        @pl.when(t == nt - 1)
        def _():
            dwg_ref[0] = awg_ref[...].astype(BF16)
            dwu_ref[0] = awu_ref[...].astype(BF16)
            dwd_ref[0] = awd_ref[...].astype(BF16)

        @pl.when(j == J - 1)
        def _():
            dx, dnw = _rms_bwd(x_ref[...], nw_ref[...], dxn_ref[rows, :])
            dx_ref[...] = dy_ref[...] + dx
            dnw_ref[...] += dnw

    ends = lambda j, t: (jnp.where((j == 0) | (j == J - 1), t, 0), 0)
    last = lambda j, t: (jnp.where(j == J - 1, t, 0), 0)
    outs, _, slots = _pcall(
        body, (x, dy, nw, wg, wu, wd), name="ffn_bwd", grid=(J, nt),
        in_specs=[pl.BlockSpec((tm, D), ends), pl.BlockSpec((tm, D), ends),
                  pl.BlockSpec((1, D), lambda j, t: (0, 0)),
                  pl.BlockSpec((1, D, F), lambda j, t: (j, 0, 0)),
                  pl.BlockSpec((1, D, F), lambda j, t: (j, 0, 0)),
                  pl.BlockSpec((1, F, D), lambda j, t: (j, 0, 0))],
        out_specs=[pl.BlockSpec((tm, D), last),
                   pl.BlockSpec((1, D, F), lambda j, t: (j, 0, 0)),
                   pl.BlockSpec((1, D, F), lambda j, t: (j, 0, 0)),
                   pl.BlockSpec((1, F, D), lambda j, t: (j, 0, 0)),
                   pl.BlockSpec((1, D), lambda j, t: (0, 0))],
        out_shape=[jax.ShapeDtypeStruct((T, D), F32),
                   jax.ShapeDtypeStruct((J, D, F), BF16), jax.ShapeDtypeStruct((J, D, F), BF16),
                   jax.ShapeDtypeStruct((J, F, D), BF16), jax.ShapeDtypeStruct((1, D), F32)],
        scratch_shapes=[pltpu.VMEM((T, D), BF16), pltpu.VMEM((T, D), BF16), pltpu.VMEM((T, D), F32),
                        pltpu.VMEM((D, F), F32), pltpu.VMEM((D, F), F32), pltpu.VMEM((F, D), F32)],
        sem=("arbitrary", "arbitrary"), rs=rs)
    return outs, slots


def rmslin_fwd(x, nw, w, b):
    T, D = x.shape
    N = w.shape[1]
    tm = min(T, 256)

    def body(x_ref, nw_ref, w_ref, b_ref, o_ref):
        xn = _rms(x_ref[...], nw_ref[...]).astype(BF16)
        o_ref[...] = _dot(xn, w_ref[...]) + b_ref[...]

    return pl.pallas_call(
        body, name="rmslin_fwd", grid=(T // tm,),
        in_specs=[pl.BlockSpec((tm, D), lambda t: (t, 0)), pl.BlockSpec((1, D), lambda t: (0, 0)),
                  pl.BlockSpec((D, N), lambda t: (0, 0)), pl.BlockSpec((1, N), lambda t: (0, 0))],
        out_specs=pl.BlockSpec((tm, N), lambda t: (t, 0)),
        out_shape=jax.ShapeDtypeStruct((T, N), F32),
        compiler_params=_cparams(("parallel",)),
    )(x, nw, w, b)


def rmslin_bwd(x, dres, dproj, nw, w):
    T, D = x.shape
    N = w.shape[1]
    nb = 1024
    nc = N // nb
    tm = min(T, 256)
    nt = T // tm

    def body(x_ref, dres_ref, dp_ref, nw_ref, w_ref, dx_ref, dw_ref, db_ref, dnw_ref, xn_ref, dxn_ref):
        c = pl.program_id(0)
        t = pl.program_id(1)
        rows = pl.ds(pl.multiple_of(t * tm, tm), tm)

        @pl.when(c == 0)
        def _():
            xn_ref[rows, :] = _rms(x_ref[...], nw_ref[...]).astype(BF16)
            dxn_ref[rows, :] = jnp.zeros((tm, D), F32)

        @pl.when((c == 0) & (t == 0))
        def _():
            dnw_ref[...] = jnp.zeros_like(dnw_ref)

        @pl.when(t == 0)
        def _():
            dw_ref[...] = jnp.zeros_like(dw_ref)
            db_ref[...] = jnp.zeros_like(db_ref)

        dpf = dp_ref[...]
        dp = dpf.astype(BF16)
        dw_ref[...] += _dot_tn(xn_ref[rows, :], dp)
        db_ref[...] += jnp.sum(dpf, axis=0, keepdims=True)
        dxn_ref[rows, :] += _dot_nt(dp, w_ref[...])

        @pl.when(c == nc - 1)
        def _():
            dx, dnw = _rms_bwd(x_ref[...], nw_ref[...], dxn_ref[rows, :])
            dx_ref[...] = dres_ref[...] + dx
            dnw_ref[...] += dnw

    ends = lambda c, t: (jnp.where((c == 0) | (c == nc - 1), t, 0), 0)
    last = lambda c, t: (jnp.where(c == nc - 1, t, 0), 0)
    return pl.pallas_call(
        body, name="rmslin_bwd", grid=(nc, nt),
        in_specs=[pl.BlockSpec((tm, D), ends), pl.BlockSpec((tm, D), last),
                  pl.BlockSpec((tm, nb), lambda c, t: (t, c)),
                  pl.BlockSpec((1, D), lambda c, t: (0, 0)),
                  pl.BlockSpec((D, nb), lambda c, t: (0, c))],
        out_specs=[pl.BlockSpec((tm, D), last),
                   pl.BlockSpec((D, nb), lambda c, t: (0, c)),
                   pl.BlockSpec((1, nb), lambda c, t: (0, c)),
                   pl.BlockSpec((1, D), lambda c, t: (0, 0))],
        out_shape=[jax.ShapeDtypeStruct((T, D), F32), jax.ShapeDtypeStruct((D, N), F32),
                   jax.ShapeDtypeStruct((1, N), F32), jax.ShapeDtypeStruct((1, D), F32)],
        scratch_shapes=[pltpu.VMEM((T, D), BF16), pltpu.VMEM((T, D), F32)],
        compiler_params=_cparams(("arbitrary", "arbitrary")),
    )(x, dres, dproj, nw, w)


def lin_fwd(res, parts, w, b):
    T = res.shape[0]
    K, N = w.shape
    tm = min(T, 512)
    n = len(parts)
    offs = [sum(p.shape[1] for p in parts[:i]) for i in range(n + 1)]

    def body(res_ref, *refs):
        a_refs, (w_ref, b_ref, o_ref) = refs[:n], refs[n:]
        acc = res_ref[...] + b_ref[...]
        for i, a_ref in enumerate(a_refs):
            acc = acc + _dot(a_ref[...].astype(BF16), w_ref[offs[i]:offs[i + 1], :])
        o_ref[...] = acc

    return pl.pallas_call(
        body, name="lin_fwd", grid=(T // tm,),
        in_specs=[pl.BlockSpec((tm, N), lambda t: (t, 0))]
        + [pl.BlockSpec((tm, p.shape[1]), lambda t: (t, 0)) for p in parts]
        + [pl.BlockSpec((K, N), lambda t: (0, 0)), pl.BlockSpec((1, N), lambda t: (0, 0))],
        out_specs=pl.BlockSpec((tm, N), lambda t: (t, 0)),
        out_shape=jax.ShapeDtypeStruct((T, N), F32),
        compiler_params=_cparams(("parallel",)),
    )(res, *parts, w, b)


def lin_bwd(parts, dy, w):
    T = dy.shape[0]
    K, N = w.shape
    tm = min(T, 256)
    n = len(parts)
    offs = [sum(p.shape[1] for p in parts[:i]) for i in range(n + 1)]

    def body(*refs):
        a_refs, (dy_ref, w_ref, da_ref, dw_ref, db_ref) = refs[:n], refs[n:]

        @pl.when(pl.program_id(0) == 0)
        def _():
            dw_ref[...] = jnp.zeros_like(dw_ref)
            db_ref[...] = jnp.zeros_like(db_ref)

        dyf = dy_ref[...]
        dyb = dyf.astype(BF16)
        da_ref[...] = _dot_nt(dyb, w_ref[...])
        for i, a_ref in enumerate(a_refs):
            dw_ref[offs[i]:offs[i + 1], :] += _dot_tn(a_ref[...].astype(BF16), dyb)
        db_ref[...] += jnp.sum(dyf, axis=0, keepdims=True)

    return pl.pallas_call(
        body, name="lin_bwd", grid=(T // tm,),
        in_specs=[pl.BlockSpec((tm, p.shape[1]), lambda t: (t, 0)) for p in parts]
        + [pl.BlockSpec((tm, N), lambda t: (t, 0)), pl.BlockSpec((K, N), lambda t: (0, 0))],
        out_specs=[pl.BlockSpec((tm, K), lambda t: (t, 0)), pl.BlockSpec((K, N), lambda t: (0, 0)),
                   pl.BlockSpec((1, N), lambda t: (0, 0))],
        out_shape=[jax.ShapeDtypeStruct((T, K), F32), jax.ShapeDtypeStruct((K, N), F32),
                   jax.ShapeDtypeStruct((1, N), F32)],
        compiler_params=_cparams(("arbitrary",)),
    )(*parts, dy, w)


def loss_fwd_bwd(x, fw, target):
    T, D = x.shape
    tm = min(T, 256)

    def body(x_ref, fw_ref, tg_ref, loss_ref, dx_ref, dfw_ref):
        @pl.when(pl.program_id(0) == 0)
        def _():
            loss_ref[...] = jnp.zeros_like(loss_ref)
            dfw_ref[...] = jnp.zeros_like(dfw_ref)

        xv = x_ref[...]
        w = fw_ref[...]
        err = _rms(xv, w) - tg_ref[...]
        row = jnp.sum(err * err, axis=-1, keepdims=True)
        loss_ref[...] += (0.5 / D) * jnp.sum(row, axis=0, keepdims=True)
        dx, dfw = _rms_bwd(xv, w, err * (1.0 / D))
        dx_ref[...] = dx
        dfw_ref[...] += dfw

    return pl.pallas_call(
        body, name="loss_fwd_bwd", grid=(T // tm,),
        in_specs=[pl.BlockSpec((tm, D), lambda t: (t, 0)), pl.BlockSpec((1, D), lambda t: (0, 0)),
                  pl.BlockSpec((tm, D), lambda t: (t, 0))],
        out_specs=[pl.BlockSpec((1, 1), lambda t: (0, 0)), pl.BlockSpec((tm, D), lambda t: (t, 0)),
                   pl.BlockSpec((1, D), lambda t: (0, 0))],
        out_shape=[jax.ShapeDtypeStruct((1, 1), F32), jax.ShapeDtypeStruct((T, D), F32),
                   jax.ShapeDtypeStruct((1, D), F32)],
        compiler_params=_cparams(("arbitrary",)),
    )(x, fw, target)


def _attn_masks(n, rows, blk):
    r = lax.broadcasted_iota(jnp.int32, (rows, 2 * blk), 0)
    jj = lax.broadcasted_iota(jnp.int32, (rows, 2 * blk), 1)
    dist = (r % blk) + blk - jj
    valid = (dist >= 0) & (dist < blk) & ((n > 0) | (jj >= blk))
    return dist.astype(F32), valid


def _attn_block(q, kcat, vcat, sink, slope, dist, valid):
    d = q.shape[-1]
    s = _dot_nt(q.astype(BF16), kcat.astype(BF16)) * (d ** -0.5)
    s = jnp.where(valid, s - slope * dist, -1e30)
    m = lax.stop_gradient(jnp.maximum(jnp.max(s, axis=-1, keepdims=True), sink))
    e = jnp.exp(s - m)
    p = e / (jnp.sum(e, axis=-1, keepdims=True) + jnp.exp(sink - m))
    return _dot(p.astype(BF16), vcat.astype(BF16))


ATTN_G = ATTN_HEADS // ATTN_KV_HEADS
ATTN_QW = ATTN_G * HEAD_DIM
ATTN_KCOL = Q_A // KV_A


def _attn_specs():
    blk = ATTN_BLOCK
    qs = pl.BlockSpec((blk, ATTN_QW), lambda h, n: (n, h))
    prev = lambda c: pl.BlockSpec((blk, KV_A), lambda h, n: (jnp.maximum(n - 1, 0), c))
    cur = lambda c: pl.BlockSpec((blk, KV_A), lambda h, n: (n, c))
    rowp = pl.BlockSpec((ATTN_G * blk, 1), lambda h, n: (h, 0))
    return qs, [prev(ATTN_KCOL), cur(ATTN_KCOL), prev(ATTN_KCOL + 1), cur(ATTN_KCOL + 1)], rowp


def _attn_operands(h, q_ref, kp_ref, kc_ref, vp_ref, vc_ref):
    d = HEAD_DIM
    q = jnp.concatenate([q_ref[:, g * d:(g + 1) * d] for g in range(ATTN_G)], axis=0)
    pick = lambda r: jnp.where(h == 0, r[:, :d], r[:, d:])
    kcat = jnp.concatenate([pick(kp_ref[...]), pick(kc_ref[...])], axis=0)
    vcat = jnp.concatenate([pick(vp_ref[...]), pick(vc_ref[...])], axis=0)
    return q, kcat, vcat


def attn_fwd(proj, sink_rows, slope_rows):
    T = proj.shape[0]
    blk, d = ATTN_BLOCK, HEAD_DIM

    def body(q_ref, kp_ref, kc_ref, vp_ref, vc_ref, sink_ref, slope_ref, o_ref):
        h, n = pl.program_id(0), pl.program_id(1)
        dist, valid = _attn_masks(n, ATTN_G * blk, blk)
        q, kcat, vcat = _attn_operands(h, q_ref, kp_ref, kc_ref, vp_ref, vc_ref)
        o = _attn_block(q, kcat, vcat, sink_ref[...], slope_ref[...], dist, valid)
        for g in range(ATTN_G):
            o_ref[:, g * d:(g + 1) * d] = o[g * blk:(g + 1) * blk]

    qs, kv, rowp = _attn_specs()
    return pl.pallas_call(
        body, name="attn_fwd", grid=(ATTN_KV_HEADS, T // blk),
        in_specs=[qs] + kv + [rowp, rowp],
        out_specs=qs,
        out_shape=jax.ShapeDtypeStruct((T, Q_A), F32),
        compiler_params=_cparams(("parallel", "parallel")),
    )(proj, proj, proj, proj, proj, sink_rows, slope_rows)


def attn_bwd(proj, sink_rows, slope_rows, dmix):
    T = proj.shape[0]
    blk, d = ATTN_BLOCK, HEAD_DIM

    def body(q_ref, kp_ref, kc_ref, vp_ref, vc_ref, sink_ref, slope_ref, do_ref, dq_ref, dkv_ref, dsink_ref):
        h, n = pl.program_id(0), pl.program_id(1)

        @pl.when((h == 0) & (n == 0))
        def _():
            dkv_ref[...] = jnp.zeros_like(dkv_ref)

        @pl.when(n == 0)
        def _():
            dsink_ref[...] = jnp.zeros_like(dsink_ref)

        dist, valid = _attn_masks(n, ATTN_G * blk, blk)
        q, kcat, vcat = _attn_operands(h, q_ref, kp_ref, kc_ref, vp_ref, vc_ref)
        do = jnp.concatenate([do_ref[:, g * d:(g + 1) * d] for g in range(ATTN_G)], axis=0)
        fn = functools.partial(_attn_block, slope=slope_ref[...], dist=dist, valid=valid)
        _, vjp = jax.vjp(fn, q, kcat, vcat, sink_ref[...])
        dq, dkcat, dvcat, dsink = vjp(do)
        for g in range(ATTN_G):
            dq_ref[:, g * d:(g + 1) * d] = dq[g * blk:(g + 1) * blk]
        dsink_ref[...] += dsink
        lane = lax.broadcasted_iota(jnp.int32, (2 * blk, 2 * KV_A), 1)
        mine = (lane % KV_A) // d == h
        both = jnp.where(mine, jnp.concatenate([dkcat, dkcat, dvcat, dvcat], axis=1), 0.0)

        @pl.when(n == 0)
        def _():
            dkv_ref[0:blk, :] += both[blk:]

        @pl.when(n > 0)
        def _():
            rows = pl.ds(pl.multiple_of((n - 1) * blk, blk), 2 * blk)
            dkv_ref[rows, :] += both

    qs, kv, rowp = _attn_specs()
    return pl.pallas_call(
        body, name="attn_bwd", grid=(ATTN_KV_HEADS, T // blk),
        in_specs=[qs] + kv + [rowp, rowp, qs],
        out_specs=[qs, pl.BlockSpec((T, 2 * KV_A), lambda h, n: (0, 0)), rowp],
        out_shape=[jax.ShapeDtypeStruct((T, Q_A), F32), jax.ShapeDtypeStruct((T, 2 * KV_A), F32),
                   jax.ShapeDtypeStruct((ATTN_HEADS * blk, 1), F32)],
        compiler_params=_cparams(("arbitrary", "arbitrary")),
    )(proj, proj, proj, proj, proj, sink_rows, slope_rows, dmix)


def _bmm(a, b, dims, exact):
    if exact:
        return lax.dot_general(a, b, dims, precision=HI, preferred_element_type=F32)
    return lax.dot_general(a.astype(BF16), b.astype(BF16), dims, preferred_element_type=F32)


def _bmm_nn(a, b, exact=False):
    return _bmm(a, b, (((2,), (1,)), ((0,), (0,))), exact)


def _bmm_nt(a, b, exact=False):
    return _bmm(a, b, (((2,), (2,)), ((0,), (0,))), exact)


def _bmm_tn(a, b, exact=False):
    return _bmm(a, b, (((1,), (1,)), ((0,), (0,))), exact)


def _dn_chunk(qc, kc, vc, zc, braw, araw, alog, dtb, nw, S):
    H, C, D = qc.shape
    row = lax.broadcasted_iota(jnp.int32, (H, C, C), 1)
    col = lax.broadcasted_iota(jnp.int32, (H, C, C), 2)
    causal = row >= col
    strict = row > col
    eye = (row == col).astype(F32)
    ltri = causal.astype(F32)
    ones = jnp.ones((H, C, C), F32)

    q = qc * lax.rsqrt(jnp.sum(qc * qc, axis=-1, keepdims=True) + EPS) * (D ** -0.5)
    k = kc * lax.rsqrt(jnp.sum(kc * kc, axis=-1, keepdims=True) + EPS)
    beta = _sigmoid(braw)
    g = -jnp.exp(alog) * _softplus(araw + dtb)
    a_col = _bmm_nn(ltri, jnp.broadcast_to(g, (H, C, C)), True)
    a_row = _bmm_nn(ones, eye * a_col, True)
    decay = jnp.where(causal, jnp.exp(jnp.where(causal, a_col - a_row, 0.0)), 0.0)
    kb = k * beta
    low = jnp.where(strict, _bmm_nt(kb, k, True) * decay, 0.0)
    e_col = jnp.exp(a_col)
    tinv = eye - low
    p = low
    for _ in range(5):
        p = _bmm_nn(p, p)
        tinv = tinv + _bmm_nn(tinv, p)
    u = _bmm_nn(tinv, vc * beta)
    w = _bmm_nn(tinv, kb * e_col)
    attn = _bmm_nt(q, k, True) * decay
    gl = a_col[:, C - 1:C, :]
    k_dec = k * jnp.exp(gl - a_col)
    v_new = u - _bmm_nn(w, S)
    o = _bmm_nn(q * e_col, S) + _bmm_nn(attn, v_new)
    s_new = S * jnp.exp(jnp.broadcast_to(gl, (H, D, D))) + _bmm_tn(k_dec, v_new)
    on = o * lax.rsqrt(jnp.mean(o * o, axis=-1, keepdims=True) + EPS) * nw
    return on * (zc * _sigmoid(zc)), s_new


DN_ZCOLS = IN_COLS_PAD - OFF_Z
DN_ZBLK = OFF_Z // DN_ZCOLS


def _dn_heads(a, off):
    return jnp.stack([a[:, off + h * DN_D:off + (h + 1) * DN_D] for h in range(DN_HEADS)])


def _dn_gate_cols(zb, off):
    return jnp.stack([zb[:, off + h:off + h + 1] for h in range(DN_HEADS)])


def _dn_operands(x_ref, zb_ref):
    x, zb = x_ref[...], zb_ref[...]
    return (_dn_heads(x, 0), _dn_heads(x, V_B), _dn_heads(x, 2 * V_B), _dn_heads(zb, 0),
            _dn_gate_cols(zb, V_B), _dn_gate_cols(zb, V_B + DN_HEADS))


def dn_fwd(qkvc, proj, alog, dtb, nw, ag=()):
    T = qkvc.shape[0]
    H, C, D = DN_HEADS, DN_CHUNK, DN_D
    N = T // C

    def body(x_ref, zb_ref, alog_ref, dtb_ref, nw_ref, o_ref, sall_ref, s_ref):
        @pl.when(pl.program_id(0) == 0)
        def _():
            s_ref[...] = jnp.zeros_like(s_ref)

        s_in = s_ref[...]
        sall_ref[0] = s_in
        on, s_new = _dn_chunk(*_dn_operands(x_ref, zb_ref), alog_ref[...], dtb_ref[...], nw_ref[...], s_in)
        for h in range(H):
            o_ref[:, h * D:(h + 1) * D] = on[h]
        s_ref[...] = s_new

    par = pl.BlockSpec((H, 1, 1), lambda n: (0, 0, 0))
    outs, gathered, _ = _pcall(
        body, (qkvc, proj, alog, dtb, nw), name="dn_fwd", grid=(N,),
        in_specs=[pl.BlockSpec((C, QKV_B), lambda n: (n, 0)), pl.BlockSpec((C, DN_ZCOLS), lambda n: (n, DN_ZBLK)),
                  par, par, pl.BlockSpec((1, 1, D), lambda n: (0, 0, 0))],
        out_specs=[pl.BlockSpec((C, V_B), lambda n: (n, 0)), pl.BlockSpec((1, H, D, D), lambda n: (n, 0, 0, 0))],
        out_shape=[jax.ShapeDtypeStruct((T, V_B), F32), jax.ShapeDtypeStruct((N, H, D, D), F32)],
        scratch_shapes=[pltpu.VMEM((H, D, D), F32)], sem=("arbitrary",), ag=ag)
    return outs, gathered


def dn_bwd(qkvc, proj, alog, dtb, nw, sall, dmix, rs=()):
    T = qkvc.shape[0]
    H, C, D = DN_HEADS, DN_CHUNK, DN_D
    N = T // C

    def body(x_ref, zb_ref, alog_ref, dtb_ref, nw_ref, sall_ref, do_ref,
             dx_ref, dzb_ref, dalog_ref, ddtb_ref, dnw_ref, ds_ref):
        @pl.when(pl.program_id(0) == 0)
        def _():
            ds_ref[...] = jnp.zeros_like(ds_ref)
            dalog_ref[...] = jnp.zeros_like(dalog_ref)
            ddtb_ref[...] = jnp.zeros_like(ddtb_ref)
            dnw_ref[...] = jnp.zeros_like(dnw_ref)

        args = (*_dn_operands(x_ref, zb_ref), alog_ref[...], dtb_ref[...], nw_ref[...], sall_ref[0])
        _, vjp = jax.vjp(_dn_chunk, *args)
        dq, dk, dv, dz, db, da, dalog, ddtb, dnw, ds = vjp((_dn_heads(do_ref[...], 0), ds_ref[...]))
        for h in range(H):
            cols = slice(h * D, (h + 1) * D)
            dx_ref[:, cols] = dq[h]
            dx_ref[:, V_B + h * D:V_B + (h + 1) * D] = dk[h]
            dx_ref[:, 2 * V_B + h * D:2 * V_B + (h + 1) * D] = dv[h]
            dzb_ref[:, cols] = dz[h]
        lane = lax.broadcasted_iota(jnp.int32, (C, LANES), 1)
        tail = jnp.zeros((C, LANES), F32)
        for h in range(H):
            tail = tail + jnp.where(lane == h, jnp.broadcast_to(db[h], (C, LANES)), 0.0)
            tail = tail + jnp.where(lane == H + h, jnp.broadcast_to(da[h], (C, LANES)), 0.0)
        dzb_ref[:, V_B:V_B + LANES] = tail
        dzb_ref[:, V_B + LANES:] = jnp.zeros((C, DN_ZCOLS - V_B - LANES), F32)
        dalog_ref[...] += dalog
        ddtb_ref[...] += ddtb
        dnw_ref[...] += dnw
        ds_ref[...] = ds

    par = pl.BlockSpec((H, 1, 1), lambda i: (0, 0, 0))
    nws = pl.BlockSpec((1, 1, D), lambda i: (0, 0, 0))
    outs, _, slots = _pcall(
        body, (qkvc, proj, alog, dtb, nw, sall, dmix), name="dn_bwd", grid=(N,),
        in_specs=[pl.BlockSpec((C, QKV_B), lambda i: (N - 1 - i, 0)),
                  pl.BlockSpec((C, DN_ZCOLS), lambda i: (N - 1 - i, DN_ZBLK)), par, par, nws,
                  pl.BlockSpec((1, H, D, D), lambda i: (N - 1 - i, 0, 0, 0)),
                  pl.BlockSpec((C, V_B), lambda i: (N - 1 - i, 1))],
        out_specs=[pl.BlockSpec((C, QKV_B), lambda i: (N - 1 - i, 0)),
                   pl.BlockSpec((C, DN_ZCOLS), lambda i: (N - 1 - i, 0)), par, par, nws],
        out_shape=[jax.ShapeDtypeStruct((T, QKV_B), F32), jax.ShapeDtypeStruct((T, DN_ZCOLS), F32)]
        + [jax.ShapeDtypeStruct((H, 1, 1), F32)] * 2 + [jax.ShapeDtypeStruct((1, 1, D), F32)],
        scratch_shapes=[pltpu.VMEM((H, D, D), F32)], sem=("arbitrary",), rs=rs)
    return outs, slots


def _conv_taps(buf_ref, w, width, halo, tm):
    acc = None
    for kk, win in _windows(buf_ref, [halo - (width - 1) + kk for kk in range(width)], tm):
        term = w[kk:kk + 1, :] * win
        acc = term if acc is None else acc + term
    return acc


def _windows(ref, offsets, tm):
    for res in range(SUBLANES):
        ks = [k for k, o in enumerate(offsets) if o % SUBLANES == res]
        if not ks:
            continue
        lo = min(offsets[k] for k in ks)
        hi = max(offsets[k] for k in ks)
        shifted = ref[pl.ds(lo, tm + hi - lo), :]
        for k in ks:
            yield k, shifted[offsets[k] - lo:offsets[k] - lo + tm]


def _conv_taps_bwd(dbuf_ref, w, width, tm):
    acc = None
    for kk, win in _windows(dbuf_ref, [width - 1 - kk for kk in range(width)], tm):
        term = w[kk:kk + 1, :] * win
        acc = term if acc is None else acc + term
    return acc


def _conv_dw_acc(dw_ref, dout, buf_ref, width, halo, tm):
    for kk, win in _windows(buf_ref, [halo - (width - 1) + kk for kk in range(width)], tm):
        dw_ref[pl.ds(kk, 1), :] += jnp.sum(dout * win, axis=0, keepdims=True)


DNC_HALO = 8
DNC_COLS = 768


def dnconv_fwd(proj, w):
    T = proj.shape[0]
    tm = min(T, 256)
    hb = tm // DNC_HALO

    def body(x_ref, h_ref, w_ref, o_ref, buf_ref):
        i = pl.program_id(0)
        buf_ref[0:DNC_HALO, :] = jnp.where(i > 0, h_ref[...], 0.0)
        buf_ref[DNC_HALO:, :] = x_ref[...]
        acc = _conv_taps(buf_ref, w_ref[...], DN_CONV, DNC_HALO, tm)
        o_ref[...] = acc * _sigmoid(acc)

    return pl.pallas_call(
        body, name="dnconv_fwd", grid=(T // tm, 2),
        in_specs=[pl.BlockSpec((tm, DNC_COLS), lambda i, c: (i, 1 + c)),
                  pl.BlockSpec((DNC_HALO, DNC_COLS), lambda i, c: (jnp.maximum(i * hb - 1, 0), 1 + c)),
                  pl.BlockSpec((DN_CONV, DNC_COLS), lambda i, c: (0, c))],
        out_specs=pl.BlockSpec((tm, DNC_COLS), lambda i, c: (i, c)),
        out_shape=jax.ShapeDtypeStruct((T, QKV_B), F32),
        scratch_shapes=[pltpu.VMEM((DNC_HALO + tm, DNC_COLS), F32)],
        compiler_params=_cparams(("parallel", "parallel")),
    )(proj, proj, w)


def dnconv_bwd(proj, w, dout):
    T = proj.shape[0]
    tm = min(T, 256)
    nt = T // tm
    hb = tm // DNC_HALO

    def body(x_ref, h_ref, w_ref, do_ref, dx_ref, dw_ref, buf_ref, dbuf_ref):
        r = pl.program_id(1)
        i = nt - 1 - r

        @pl.when(r == 0)
        def _():
            dw_ref[...] = jnp.zeros_like(dw_ref)
            dbuf_ref[tm:, :] = jnp.zeros((DNC_HALO, DNC_COLS), F32)

        buf_ref[0:DNC_HALO, :] = jnp.where(i > 0, h_ref[...], 0.0)
        buf_ref[DNC_HALO:, :] = x_ref[...]
        wv = w_ref[...]
        acc = _conv_taps(buf_ref, wv, DN_CONV, DNC_HALO, tm)
        sg = _sigmoid(acc)
        dacc = do_ref[...] * (sg * (1.0 + acc * (1.0 - sg)))
        dbuf_ref[0:tm, :] = dacc
        dx_ref[...] = _conv_taps_bwd(dbuf_ref, wv, DN_CONV, tm)
        _conv_dw_acc(dw_ref, dacc, buf_ref, DN_CONV, DNC_HALO, tm)
        dbuf_ref[tm:, :] = dacc[0:DNC_HALO, :]

    return pl.pallas_call(
        body, name="dnconv_bwd", grid=(2, nt),
        in_specs=[pl.BlockSpec((tm, DNC_COLS), lambda c, r: (nt - 1 - r, 1 + c)),
                  pl.BlockSpec((DNC_HALO, DNC_COLS), lambda c, r: (jnp.maximum((nt - 1 - r) * hb - 1, 0), 1 + c)),
                  pl.BlockSpec((DN_CONV, DNC_COLS), lambda c, r: (0, c)),
                  pl.BlockSpec((tm, DNC_COLS), lambda c, r: (nt - 1 - r, c))],
        out_specs=[pl.BlockSpec((tm, DNC_COLS), lambda c, r: (nt - 1 - r, c)),
                   pl.BlockSpec((DN_CONV, DNC_COLS), lambda c, r: (0, c))],
        out_shape=[jax.ShapeDtypeStruct((T, QKV_B), F32), jax.ShapeDtypeStruct((DN_CONV, QKV_B), F32)],
        scratch_shapes=[pltpu.VMEM((DNC_HALO + tm, DNC_COLS), F32), pltpu.VMEM((tm + DNC_HALO, DNC_COLS), F32)],
        compiler_params=_cparams(("parallel", "arbitrary")),
    )(proj, proj, w, dout)


CV_HALO = 32


def _cv_post(cv, lnw, lnb):
    mu = jnp.mean(cv, axis=-1, keepdims=True)
    xc = cv - mu
    y = xc * lax.rsqrt(jnp.mean(xc * xc, axis=-1, keepdims=True) + EPS) * lnw + lnb
    return y * _sigmoid(y)


def cv_fwd(ab, w, bdw, lnw, lnb, ag=()):
    T = ab.shape[0]
    D = ab.shape[1] // 2
    tm = min(T, 256)
    hb = tm // CV_HALO

    def body(a_ref, b_ref, ah_ref, bh_ref, w_ref, bdw_ref, lnw_ref, lnb_ref, o_ref, buf_ref):
        i = pl.program_id(0)
        buf_ref[0:CV_HALO, :] = jnp.where(i > 0, ah_ref[...] * _sigmoid(bh_ref[...]), 0.0)
        buf_ref[CV_HALO:, :] = a_ref[...] * _sigmoid(b_ref[...])
        cv = _conv_taps(buf_ref, w_ref[...], CONV_WIDTH, CV_HALO, tm) + bdw_ref[...]
        o_ref[...] = _cv_post(cv, lnw_ref[...], lnb_ref[...])

    halo = lambda c: pl.BlockSpec((CV_HALO, D), lambda i: (jnp.maximum(i * hb - 1, 0), c))
    vec = pl.BlockSpec((1, D), lambda i: (0, 0))
    (out,), gathered, _ = _pcall(
        body, (ab, ab, ab, ab, w, bdw, lnw, lnb), name="cv_fwd", grid=(T // tm,),
        in_specs=[pl.BlockSpec((tm, D), lambda i: (i, 0)), pl.BlockSpec((tm, D), lambda i: (i, 1)),
                  halo(0), halo(1), pl.BlockSpec((CONV_WIDTH, D), lambda i: (0, 0)), vec, vec, vec],
        out_specs=[pl.BlockSpec((tm, D), lambda i: (i, 0))],
        out_shape=[jax.ShapeDtypeStruct((T, D), F32)],
        scratch_shapes=[pltpu.VMEM((CV_HALO + tm, D), F32)], sem=("arbitrary",), ag=ag)
    return out, gathered


def cv_bwd(ab, w, bdw, lnw, lnb, dout, rs=()):
    T = ab.shape[0]
    D = ab.shape[1] // 2
    tm = min(T, 256)
    nt = T // tm
    hb = tm // CV_HALO

    def body(a_ref, b_ref, ah_ref, bh_ref, w_ref, bdw_ref, lnw_ref, lnb_ref, do_ref,
             da_ref, db_ref, dw_ref, dbdw_ref, dlnw_ref, dlnb_ref, buf_ref, dbuf_ref):
        r = pl.program_id(0)
        i = nt - 1 - r

        @pl.when(r == 0)
        def _():
            dw_ref[...] = jnp.zeros_like(dw_ref)
            dbdw_ref[...] = jnp.zeros_like(dbdw_ref)
            dlnw_ref[...] = jnp.zeros_like(dlnw_ref)
            dlnb_ref[...] = jnp.zeros_like(dlnb_ref)
            dbuf_ref[tm:, :] = jnp.zeros((CV_HALO, D), F32)

        a = a_ref[...]
        sb = _sigmoid(b_ref[...])
        buf_ref[0:CV_HALO, :] = jnp.where(i > 0, ah_ref[...] * _sigmoid(bh_ref[...]), 0.0)
        buf_ref[CV_HALO:, :] = a * sb
        wv = w_ref[...]
        cv = _conv_taps(buf_ref, wv, CONV_WIDTH, CV_HALO, tm) + bdw_ref[...]
        _, vjp = jax.vjp(_cv_post, cv, lnw_ref[...], lnb_ref[...])
        dcv, dlnw, dlnb = vjp(do_ref[...])
        dlnw_ref[...] += dlnw
        dlnb_ref[...] += dlnb
        dbdw_ref[...] += jnp.sum(dcv, axis=0, keepdims=True)
        dbuf_ref[0:tm, :] = dcv
        du = _conv_taps_bwd(dbuf_ref, wv, CONV_WIDTH, tm)
        _conv_dw_acc(dw_ref, dcv, buf_ref, CONV_WIDTH, CV_HALO, tm)
        dbuf_ref[tm:, :] = dcv[0:CV_HALO, :]
        da_ref[...] = du * sb
        db_ref[...] = du * a * sb * (1.0 - sb)

    tile = lambda c: pl.BlockSpec((tm, D), lambda r: (nt - 1 - r, c))
    halo = lambda c: pl.BlockSpec((CV_HALO, D), lambda r: (jnp.maximum((nt - 1 - r) * hb - 1, 0), c))
    vec = pl.BlockSpec((1, D), lambda r: (0, 0))
    wsp = pl.BlockSpec((CONV_WIDTH, D), lambda r: (0, 0))
    (da, db, dw, dbdw, dlnw, dlnb), _, slots = _pcall(
        body, (ab, ab, ab, ab, w, bdw, lnw, lnb, dout), name="cv_bwd", grid=(nt,),
        in_specs=[tile(0), tile(1), halo(0), halo(1), wsp, vec, vec, vec, tile(0)],
        out_specs=[tile(0), tile(0), wsp, vec, vec, vec],
        out_shape=[jax.ShapeDtypeStruct((T, D), F32), jax.ShapeDtypeStruct((T, D), F32),
                   jax.ShapeDtypeStruct((CONV_WIDTH, D), F32)] + [jax.ShapeDtypeStruct((1, D), F32)] * 3,
        scratch_shapes=[pltpu.VMEM((CV_HALO + tm, D), F32), pltpu.VMEM((tm + CV_HALO, D), F32)],
        sem=("arbitrary",), rs=rs)
    return (jnp.concatenate([da, db], axis=1), dw, dbdw, dlnw, dlnb), slots


def adamw(w, m, v, slots):
    L, R, C = w.shape
    tr = max([d for d in range(16, 257, 16) if R % d == 0 and N_DEV * d * C * 2 <= 2**19], default=R)
    c1 = 1.0 / (1.0 - ADAM_B1 ** ADAM_STEP)
    c2 = 1.0 / (1.0 - ADAM_B2 ** ADAM_STEP)

    def body(w_ref, m_ref, v_ref, *rest):
        s_refs = rest[:L]
        g_ref, d_ref, nm_ref, nv_ref = rest[L:]
        l = pl.program_id(0)
        for k in range(L):
            @pl.when(l == k)
            def _(s_ref=s_refs[k]):
                g = s_ref[0].astype(F32)
                for j in range(1, N_DEV):
                    g = g + s_ref[j].astype(F32)
                nm = ADAM_B1 * m_ref[0] + (1.0 - ADAM_B1) * g
                nv = ADAM_B2 * v_ref[0] + (1.0 - ADAM_B2) * (g * g)
                g_ref[0] = g
                nm_ref[0] = nm
                nv_ref[0] = nv
                d_ref[0] = -ADAM_LR * ((nm * c1) / (jnp.sqrt(nv * c2) + ADAM_EPS) + ADAM_WD * w_ref[0])

    blk = pl.BlockSpec((1, tr, C), lambda l, r: (l, r, 0))
    slot = lambda k: pl.BlockSpec((N_DEV, tr, C), lambda l, r: (0, jnp.where(l == k, r, 0), 0))
    return pl.pallas_call(
        body, name="adamw", grid=(L, R // tr),
        in_specs=[blk, blk, blk] + [slot(k) for k in range(L)],
        out_specs=[blk, blk, blk, blk],
        out_shape=[jax.ShapeDtypeStruct((L, R, C), F32)] * 4,
        compiler_params=_cparams(("arbitrary", "arbitrary")),
    )(w, m, v, *slots)


def _unshard(g, axis):
    g = jnp.moveaxis(g, 0, axis)
    s = g.shape
    return g.reshape(s[:axis] + (s[axis] * s[axis + 1],) + s[axis + 2:])


def _to_blocks(full, axis):
    s = full.shape
    g = full.reshape(s[:axis] + (N_DEV, s[axis] // N_DEV) + s[axis + 1:])
    return jnp.moveaxis(g, axis, 0)


def _heads(a, h):
    T = a.shape[0]
    return a.reshape(T, h, a.shape[1] // h).transpose(1, 0, 2)


def _unheads(a):
    h, T, d = a.shape
    return a.transpose(1, 0, 2).reshape(T, h * d)


SMALL = (("norm_w", 2), ("dn_conv_w", 2), ("conv_b_pw1", 1), ("conv_w_dw", 2), ("conv_b_dw", 1),
         ("conv_ln_w", 1), ("conv_ln_b", 1), ("conv_b_pw2", 1),
         ("attn_sinks", None), ("dn_a_log", None), ("dn_dt_bias", None), ("dn_norm_w", None), ("final_norm_w", None))

def _pack(parts):
    flat = jnp.concatenate([p.reshape(-1) for p in parts])
    pad = (-flat.shape[0]) % LANES
    return jnp.pad(flat, (0, pad))


def _unpack(flat, shapes):
    out, off = [], 0
    for s in shapes:
        n = int(np.prod(s))
        out.append(flat[off:off + n].reshape(s))
        off += n
    return out


def kernel(x, norm_w, ffn_w_gate, ffn_w_up, ffn_w_down, mix_w_in, dn_conv_w, attn_sinks, dn_a_log, dn_dt_bias, dn_norm_w, mix_w_out, conv_w_pw1, conv_b_pw1, conv_w_dw, conv_b_dw, conv_ln_w, conv_ln_b, conv_w_pw2, conv_b_pw2, final_norm_w, loss_target, m_norm_w, m_ffn_w_gate, m_ffn_w_up, m_ffn_w_down, m_mix_w_in, m_dn_conv_w, m_attn_sinks, m_dn_a_log, m_dn_dt_bias, m_dn_norm_w, m_mix_w_out, m_conv_w_pw1, m_conv_b_pw1, m_conv_w_dw, m_conv_b_dw, m_conv_ln_w, m_conv_ln_b, m_conv_w_pw2, m_conv_b_pw2, m_final_norm_w, v_norm_w, v_ffn_w_gate, v_ffn_w_up, v_ffn_w_down, v_mix_w_in, v_dn_conv_w, v_attn_sinks, v_dn_a_log, v_dn_dt_bias, v_dn_norm_w, v_mix_w_out, v_conv_w_pw1, v_conv_b_pw1, v_conv_w_dw, v_conv_b_dw, v_conv_ln_w, v_conv_ln_b, v_conv_w_pw2, v_conv_b_pw2, v_final_norm_w):
    W = dict(norm_w=norm_w, ffn_w_gate=ffn_w_gate, ffn_w_up=ffn_w_up, ffn_w_down=ffn_w_down, mix_w_in=mix_w_in,
             dn_conv_w=dn_conv_w, attn_sinks=attn_sinks, dn_a_log=dn_a_log, dn_dt_bias=dn_dt_bias,
             dn_norm_w=dn_norm_w, mix_w_out=mix_w_out, conv_w_pw1=conv_w_pw1, conv_b_pw1=conv_b_pw1,
             conv_w_dw=conv_w_dw, conv_b_dw=conv_b_dw, conv_ln_w=conv_ln_w, conv_ln_b=conv_ln_b,
             conv_w_pw2=conv_w_pw2, conv_b_pw2=conv_b_pw2, final_norm_w=final_norm_w)
    M = dict(norm_w=m_norm_w, ffn_w_gate=m_ffn_w_gate, ffn_w_up=m_ffn_w_up, ffn_w_down=m_ffn_w_down,
             mix_w_in=m_mix_w_in, dn_conv_w=m_dn_conv_w, attn_sinks=m_attn_sinks, dn_a_log=m_dn_a_log,
             dn_dt_bias=m_dn_dt_bias, dn_norm_w=m_dn_norm_w, mix_w_out=m_mix_w_out, conv_w_pw1=m_conv_w_pw1,
             conv_b_pw1=m_conv_b_pw1, conv_w_dw=m_conv_w_dw, conv_b_dw=m_conv_b_dw, conv_ln_w=m_conv_ln_w,
             conv_ln_b=m_conv_ln_b, conv_w_pw2=m_conv_w_pw2, conv_b_pw2=m_conv_b_pw2, final_norm_w=m_final_norm_w)
    V = dict(norm_w=v_norm_w, ffn_w_gate=v_ffn_w_gate, ffn_w_up=v_ffn_w_up, ffn_w_down=v_ffn_w_down,
             mix_w_in=v_mix_w_in, dn_conv_w=v_dn_conv_w, attn_sinks=v_attn_sinks, dn_a_log=v_dn_a_log,
             dn_dt_bias=v_dn_dt_bias, dn_norm_w=v_dn_norm_w, mix_w_out=v_mix_w_out, conv_w_pw1=v_conv_w_pw1,
             conv_b_pw1=v_conv_b_pw1, conv_w_dw=v_conv_w_dw, conv_b_dw=v_conv_b_dw, conv_ln_w=v_conv_ln_w,
             conv_ln_b=v_conv_ln_b, conv_w_pw2=v_conv_w_pw2, conv_b_pw2=v_conv_b_pw2, final_norm_w=v_final_norm_w)

    T, D = x.shape[1], x.shape[2]
    xs = x[0]
    F8 = ffn_w_gate.shape[-1]
    n_ffn = DEPTH * 2

    big = ("ffn_w_gate", "ffn_w_up", "ffn_w_down", "mix_w_in", "mix_w_out", "conv_w_pw1", "conv_w_pw2")
    shard3 = {k: W[k].reshape((-1,) + W[k].shape[-2:]) for k in big}
    shard_bf = {k: shard3[k].astype(BF16) for k in big}
    ffn_unit = lambda i: [("ffn_w_gate", i), ("ffn_w_up", i), ("ffn_w_down", i)]
    even_unit = lambda e: [("mix_w_in", e), ("mix_w_out", e)]
    odd_unit = lambda e: [("conv_w_pw1", e), ("conv_w_pw2", e)]
    have = {}

    def ag_jobs(units):
        return [(shard_bf[k], i) for k, i in units]

    def ag_done(units, gathered):
        have.update(zip(units, gathered))

    small_sharded = [(k, ax) for k, ax in SMALL if ax is not None]
    small_pack = _pack([W[k] for k, _ in small_sharded])[None, :]
    first_units = ffn_unit(0) + even_unit(0)
    gathered, _ = exchange(ag=ag_jobs(first_units) + [(small_pack, None)])
    ag_done(first_units, gathered[:-1])
    small_full = {}
    for (k, ax), parts in zip(small_sharded,
                              zip(*[_unpack(gathered[-1][s, 0], [W[k].shape for k, _ in small_sharded])
                                    for s in range(N_DEV)])):
        small_full[k] = _unshard(jnp.stack(parts), ax)
    nw_full = small_full["norm_w"]

    ffn_w = lambda i: [have[u] for u in ffn_unit(i)]
    w_in_of = lambda e: jnp.pad(_unshard(have[("mix_w_in", e)], 1), ((0, 0), (0, IN_COLS_PAD - IN_COLS)))
    w_out_of = lambda e: have[("mix_w_out", e)].reshape(D, D)
    w_pw1_of = lambda e: _unshard(have[("conv_w_pw1", e)], 1)
    w_pw2_of = lambda e: have[("conv_w_pw2", e)].reshape(D, D)
    fwd_order = [key for l in range(DEPTH) for key in (("F", 2 * l), ("M", l), ("F", 2 * l + 1))]
    needed = {}
    for l in range(DEPTH):
        needed[("F", 2 * l)], needed[("F", 2 * l + 1)] = ffn_unit(2 * l), ffn_unit(2 * l + 1)
        needed[("M", l)] = even_unit(l // 2) if l % 2 == 0 else odd_unit(l // 2)
    queue = [(u, pos) for pos, key in enumerate(fwd_order) for u in needed[key] if u not in first_units]
    unit_bytes = lambda u: N_DEV * shard_bf[u[0]][u[1]].size * 2
    fwd_carry, at = {}, 0
    for pos, key in enumerate(fwd_order):
        cap = FWD_CARRY_BYTES[key[0] if key[0] == "F" else ("E" if key[1] % 2 == 0 else "O")]
        taken, used = [], 0
        while at < len(queue) and (queue[at][1] <= pos + 1 or used + unit_bytes(queue[at][0]) <= cap):
            taken.append(queue[at][0])
            used += unit_bytes(queue[at][0])
            at += 1
        fwd_carry[key] = taken
    zero_in = jnp.zeros((1, IN_COLS_PAD), F32)
    zero_d = jnp.zeros((1, D), F32)
    slope_rows = jnp.asarray(np.repeat(2.0 ** (-8.0 * np.arange(1, ATTN_HEADS + 1) / ATTN_HEADS), ATTN_BLOCK)
                             .astype(np.float32)[:, None])

    saved = []
    h = xs
    w_in, w_out, w_pw1, w_pw2 = {}, {}, {}, {}

    def ffn_forward(h, l, half):
        i = 2 * l + half
        units = fwd_carry.get(("F", i), [])
        h, gathered = ffn_fwd(h, nw_full[l, 2 * half][None], *ffn_w(i), ag=ag_jobs(units))
        ag_done(units, gathered)
        return h

    for l in range(DEPTH):
        e = l // 2
        st = {"x0": h}
        h = ffn_forward(h, l, 0)
        st["x1"] = h
        units = fwd_carry.get(("M", l), [])
        if l % 2 == 0:
            w_in[e], w_out[e] = w_in_of(e), w_out_of(e)
            proj = rmslin_fwd(h, nw_full[l, 1][None], w_in[e], zero_in)
            st["proj"] = proj
            st["qkvc"] = dnconv_fwd(proj, small_full["dn_conv_w"][e])
            st["sink_rows"] = jnp.repeat(attn_sinks[e], ATTN_BLOCK)[:, None]
            st["alog"] = dn_a_log[e].reshape(DN_HEADS, 1, 1)
            st["dtb"] = dn_dt_bias[e].reshape(DN_HEADS, 1, 1)
            st["dnw"] = dn_norm_w[e].reshape(1, 1, DN_D)
            st["att"] = attn_fwd(proj, st["sink_rows"], slope_rows)
            (st["og"], st["sall"]), gathered = dn_fwd(st["qkvc"], proj, st["alog"], st["dtb"], st["dnw"],
                                                      ag=ag_jobs(units))
            ag_done(units, gathered)
            h = lin_fwd(h, [st["att"], st["og"]], w_out[e], zero_d)
        else:
            w_pw1[e], w_pw2[e] = w_pw1_of(e), w_pw2_of(e)
            st["ab"] = rmslin_fwd(h, nw_full[l, 1][None], w_pw1[e], small_full["conv_b_pw1"][e][None])
            st["act"], gathered = cv_fwd(st["ab"], small_full["conv_w_dw"][e], small_full["conv_b_dw"][e][None],
                                         small_full["conv_ln_w"][e][None], small_full["conv_ln_b"][e][None],
                                         ag=ag_jobs(units))
            ag_done(units, gathered)
            h = lin_fwd(h, [st["act"]], w_pw2[e], small_full["conv_b_pw2"][e][None])
        st["x2"] = h
        h = ffn_forward(h, l, 1)
        saved.append(st)

    loss_part, dh, dfinal = loss_fwd_bwd(h, final_norm_w[None], loss_target[0])
    loss = lax.psum(loss_part[0, 0], ("x", "y", "c"))

    d_norm = [[None] * 3 for _ in range(DEPTH)]
    d_small = {k: [None, None] for k in ("dn_conv_w", "conv_b_pw1", "conv_w_dw", "conv_b_dw", "conv_ln_w",
                                         "conv_ln_b", "conv_b_pw2", "attn_sinks", "dn_a_log", "dn_dt_bias",
                                         "dn_norm_w")}
    pending, slot = [], {}

    def take_pending(cap=None):
        n, used = 0, 0
        while n < len(pending) and (cap is None or used + pending[n][1].size * 2 <= cap):
            used += pending[n][1].size * 2
            n += 1
        units = pending[:n]
        del pending[:n]
        return [u for u, _ in units], [b for _, b in units]

    def ffn_backward(dh, l, half):
        i = 2 * l + half
        units, blocks = take_pending(BWD_CARRY_BYTES["F"])
        (dh, dg, du, dd, d_norm[l][2 * half]), slots = ffn_bwd(
            st["x2" if half else "x0"], dh, nw_full[l, 2 * half][None], *ffn_w(i), rs=blocks)
        slot.update(zip(units, slots))
        pending.extend(zip(ffn_unit(i), (dg, du, dd)))
        return dh

    for l in reversed(range(DEPTH)):
        e = l // 2
        st = saved[l]
        dh = ffn_backward(dh, l, 1)
        if l % 2 == 0:
            dmix, d_out, _ = lin_bwd([st["att"], st["og"]], dh, w_out[e])
            pending.append((("mix_w_out", e), d_out.reshape(N_DEV, D // N_DEV, D).astype(BF16)))
            units, blocks = take_pending(BWD_CARRY_BYTES["E"])
            (dqkvc, dzba, dalog, ddtb, ddnw), slots = dn_bwd(
                st["qkvc"], st["proj"], st["alog"], st["dtb"], st["dnw"], st["sall"], dmix, rs=blocks)
            slot.update(zip(units, slots))
            dqa, dkva, dsink = attn_bwd(st["proj"], st["sink_rows"], slope_rows, dmix)
            dqkv, d_small["dn_conv_w"][e] = dnconv_bwd(st["proj"], small_full["dn_conv_w"][e], dqkvc)
            dproj = jnp.concatenate([dqa, dkva, dqkv, dzba], axis=1)
            dh, d_in, _, d_norm[l][1] = rmslin_bwd(st["x1"], dh, dproj, nw_full[l, 1][None], w_in[e])
            pending.append((("mix_w_in", e), _to_blocks(d_in[:, :IN_COLS], 1).astype(BF16)))
            d_small["attn_sinks"][e] = jnp.sum(dsink.reshape(ATTN_HEADS, ATTN_BLOCK), axis=1)
            d_small["dn_a_log"][e] = dalog.reshape(DN_HEADS)
            d_small["dn_dt_bias"][e] = ddtb.reshape(DN_HEADS)
            d_small["dn_norm_w"][e] = ddnw.reshape(DN_D)
        else:
            dact, d_pw2, d_small["conv_b_pw2"][e] = lin_bwd([st["act"]], dh, w_pw2[e])
            pending.append((("conv_w_pw2", e), d_pw2.reshape(N_DEV, D // N_DEV, D).astype(BF16)))
            units, blocks = take_pending(BWD_CARRY_BYTES["O"])
            (dab, d_small["conv_w_dw"][e], d_small["conv_b_dw"][e], d_small["conv_ln_w"][e],
             d_small["conv_ln_b"][e]), slots = cv_bwd(
                st["ab"], small_full["conv_w_dw"][e], small_full["conv_b_dw"][e][None],
                small_full["conv_ln_w"][e][None], small_full["conv_ln_b"][e][None], dact, rs=blocks)
            slot.update(zip(units, slots))
            dh, d_pw1, d_small["conv_b_pw1"][e], d_norm[l][1] = rmslin_bwd(
                st["x1"], dh, dab, nw_full[l, 1][None], w_pw1[e])
            pending.append((("conv_w_pw1", e), _to_blocks(d_pw1, 1).astype(BF16)))
        dh = ffn_backward(dh, l, 0)
    grad_x = dh[None]

    full_small = {"norm_w": jnp.stack([jnp.concatenate(r, axis=0) for r in d_norm]),
                  "final_norm_w": dfinal[0]}
    for k, pair in d_small.items():
        full_small[k] = jnp.stack([p.reshape(W[k].shape[1:-1] + (-1,)) if SMALL_AXIS[k] is not None
                                   else p for p in pair])
    rows = []
    for s in range(N_DEV):
        parts = [_to_blocks(full_small[k], ax)[s] if ax is not None else full_small[k] for k, ax in SMALL]
        rows.append(_pack(parts))
    send_small = jnp.stack(rows)[:, None, :]
    units, blocks = take_pending()
    _, slots = exchange(rs=blocks + [send_small])
    slot.update(zip(units, slots[:-1]))

    res = {}
    for k in big:
        outs = adamw(shard3[k], M[k].reshape(shard3[k].shape), V[k].reshape(shard3[k].shape),
                     [slot[(k, i)] for i in range(shard3[k].shape[0])])
        res[k] = [o.reshape(W[k].shape) for o in outs]
    pk = lambda d: _pack([d[k] for k, _ in SMALL])[None, None, :]
    outs = adamw(pk(W), pk(M), pk(V), [slots[-1]])
    shapes = [W[k].shape for k, _ in SMALL]
    unp = [_unpack(o[0, 0], shapes) for o in outs]
    for i, (k, _) in enumerate(SMALL):
        res[k] = [u[i] for u in unp]

    order = ("norm_w", "ffn_w_gate", "ffn_w_up", "ffn_w_down", "mix_w_in", "dn_conv_w", "attn_sinks", "dn_a_log",
             "dn_dt_bias", "dn_norm_w", "mix_w_out", "conv_w_pw1", "conv_b_pw1", "conv_w_dw", "conv_b_dw",
             "conv_ln_w", "conv_ln_b", "conv_w_pw2", "conv_b_pw2", "final_norm_w")
    return (loss, grad_x, *[res[k][0] for k in order], *[res[k][1] for k in order],
            *[res[k][2] for k in order], *[res[k][3] for k in order])


SMALL_AXIS = dict(SMALL)
```

```python
import functools

import numpy as np
import jax
import jax.numpy as jnp
from jax import lax
from jax.experimental import pallas as pl
from jax.experimental.pallas import tpu as pltpu

F32 = jnp.float32
BF16 = jnp.bfloat16
HI = lax.Precision.HIGHEST
EPS = 1e-6
N_DEV = 8
V7X_VMEM_LIMIT = 60 * 2**20
MESH = pl.DeviceIdType.MESH
LANES = 128
SUBLANES = 8

DEPTH = 4
D_MODEL = 1024
ATTN_HEADS, ATTN_KV_HEADS, HEAD_DIM, ATTN_BLOCK = 8, 2, 64, 128
DN_HEADS, DN_D, DN_CHUNK, DN_CONV = 8, 64, 64, 4
CONV_WIDTH = 31
Q_A, KV_A, QKV_B, V_B = 512, 128, 1536, 512
IN_COLS = 2832
IN_COLS_PAD = 3072
OFF_QKVB = Q_A + 2 * KV_A
OFF_Z = OFF_QKVB + QKV_B
OFF_BETA = OFF_Z + V_B
OFF_A = OFF_BETA + DN_HEADS

FWD_CARRY_BYTES = {"F": 12 * 2**20, "E": 20 * 2**20, "O": 9 * 2**20}
BWD_CARRY_BYTES = {"F": 14 * 2**20, "E": 20 * 2**20, "O": 19 * 2**20}

ADAM_SLOT_BLOCK = 3 * 2**19

ADAM_LR, ADAM_B1, ADAM_B2, ADAM_EPS, ADAM_WD, ADAM_STEP = 0.001, 0.9, 0.999, 1e-08, 0.01, 10


def _cparams(sem):
    return pltpu.CompilerParams(dimension_semantics=sem, vmem_limit_bytes=V7X_VMEM_LIMIT)


def _sigmoid(x):
    return 1.0 / (1.0 + jnp.exp(-x))


def _softplus(x):
    return jnp.maximum(x, 0.0) + jnp.log(1.0 + jnp.exp(-jnp.abs(x)))


def _dot(a, b):
    return jnp.dot(a, b, preferred_element_type=F32)


def _dot_nt(a, b):
    return lax.dot_general(a, b, (((1,), (1,)), ((), ())), preferred_element_type=F32)


def _dot_tn(a, b):
    return lax.dot_general(a, b, (((0,), (0,)), ((), ())), preferred_element_type=F32)


def _rms(x, w):
    return x * lax.rsqrt(jnp.mean(x * x, axis=-1, keepdims=True) + EPS) * w


def _rms_bwd(x, w, dxn):
    r = lax.rsqrt(jnp.mean(x * x, axis=-1, keepdims=True) + EPS)
    xh = x * r
    dxh = dxn * w
    dx = r * (dxh - xh * jnp.mean(dxh * xh, axis=-1, keepdims=True))
    return dx, jnp.sum(dxn * xh, axis=0, keepdims=True)


def _position():
    return lax.axis_index("x"), lax.axis_index("y"), lax.axis_index("c")


def _dev_index(px, py, pc):
    return 4 * px + 2 * py + pc


def _rcopy(src, dst, send_sem, recv_sem, to):
    return pltpu.make_async_remote_copy(src_ref=src, dst_ref=dst, send_sem=send_sem, recv_sem=recv_sem,
                                        device_id=to, device_id_type=MESH)


def _ag_start(srcs, outs, send, recv, local):
    x, y, c = _position()
    me = _dev_index(x, y, c)
    chips = [(1 - x, y), (x, 1 - y), (1 - x, 1 - y)]
    for a, (src, out) in enumerate(zip(srcs, outs)):
        pltpu.make_async_copy(src, out.at[me], local.at[a]).start()
        _rcopy(src, out.at[me], send.at[a, 0], recv.at[a, 0], (x, y, 1 - c)).start()
        for j, chip in enumerate(chips):
            _rcopy(src, out.at[me], send.at[a, 1 + j], recv.at[a, 1 + j], (*chip, c)).start()


def _ag_finish(srcs, outs, send, recv, local):
    x, y, c = _position()
    me = _dev_index(x, y, c)
    sibling = (x, y, 1 - c)
    chips = [(1 - x, y), (x, 1 - y), (1 - x, 1 - y)]
    for j, chip in enumerate(chips):
        for a, out in enumerate(outs):
            blk = out.at[_dev_index(*chip, c)]
            _rcopy(blk, blk, send.at[a, 1 + j], recv.at[a, 1 + j], (x, y, c)).wait_recv()
            _rcopy(blk, blk, send.at[a, 4 + j], recv.at[a, 4 + j], sibling).start()
    for a, (src, out) in enumerate(zip(srcs, outs)):
        blk = out.at[_dev_index(x, y, 1 - c)]
        _rcopy(blk, blk, send.at[a, 0], recv.at[a, 0], (x, y, c)).wait_recv()
        for j, chip in enumerate(chips):
            blk = out.at[_dev_index(*chip, 1 - c)]
            _rcopy(blk, blk, send.at[a, 4 + j], recv.at[a, 4 + j], (x, y, c)).wait_recv()
        for k in range(N_DEV - 1):
            _rcopy(out.at[me], out.at[me], send.at[a, k], recv.at[a, k], (x, y, c)).wait_send()
        pltpu.make_async_copy(src, out.at[me], local.at[a]).wait()


def _rs_peer(r):
    x, y, c = _position()
    return x ^ ((r >> 2) & 1), y ^ ((r >> 1) & 1), c ^ (r & 1)


def _rs_start(ins, outs, send, recv, local):
    me = _dev_index(*_position())
    for a, (src, out) in enumerate(zip(ins, outs)):
        pltpu.make_async_copy(src.at[me], out.at[me], local.at[a]).start()
        for r in range(1, N_DEV):
            p = _rs_peer(r)
            _rcopy(src.at[_dev_index(*p)], out.at[me], send.at[a, r - 1], recv.at[a, r - 1], p).start()


def _rs_finish(ins, outs, send, recv, local):
    pos = _position()
    me = _dev_index(*pos)
    for a, (src, out) in enumerate(zip(ins, outs)):
        for r in range(1, N_DEV):
            blk = out.at[_dev_index(*_rs_peer(r))]
            _rcopy(blk, blk, send.at[a, r - 1], recv.at[a, r - 1], pos).wait_recv()
        for r in range(1, N_DEV):
            _rcopy(src.at[me], out.at[me], send.at[a, r - 1], recv.at[a, r - 1], pos).wait_send()
        pltpu.make_async_copy(src.at[me], out.at[me], local.at[a]).wait()


def _pcall(body, args, *, name, grid, in_specs, out_specs, out_shape, sem, scratch_shapes=(), ag=(), rs=()):
    na, nr = len(ag), len(rs)
    if na + nr == 0:
        outs = pl.pallas_call(body, name=name, grid=grid, in_specs=in_specs, out_specs=out_specs,
                              out_shape=out_shape, scratch_shapes=list(scratch_shapes),
                              compiler_params=_cparams(sem))(*args)
        return list(outs), [], []
    n_in, n_out, n_scr = len(in_specs), len(out_specs), len(scratch_shapes)
    ag_idx = [i for _, i in ag]

    def wrapped(*refs):
        cin, refs = refs[:n_in], refs[n_in:]
        ag_in, refs = refs[:na], refs[na:]
        rs_in, refs = refs[:nr], refs[nr:]
        cout, refs = refs[:n_out], refs[n_out:]
        ag_out, refs = refs[:na], refs[na:]
        rs_out, refs = refs[:nr], refs[nr:]
        cscr, sems = refs[:n_scr], refs[n_scr:]
        ag_src = [r if i is None else r.at[i] for r, i in zip(ag_in, ag_idx)]
        ids = [pl.program_id(d) for d in range(len(grid))]
        first = functools.reduce(jnp.logical_and, [i == 0 for i in ids])
        last = functools.reduce(jnp.logical_and, [i == g - 1 for i, g in zip(ids, grid)])

        @pl.when(first)
        def _():
            if na:
                _ag_start(ag_src, ag_out, *sems[:3])
            if nr:
                _rs_start(rs_in, rs_out, *sems[-3:])

        body(*cin, *cout, *cscr)

        @pl.when(last)
        def _():
            if na:
                _ag_finish(ag_src, ag_out, *sems[:3])
            if nr:
                _rs_finish(rs_in, rs_out, *sems[-3:])

    hbm = pl.BlockSpec(memory_space=pl.ANY)
    sem_shapes = []
    for n in (na, nr):
        if n:
            sem_shapes += [pltpu.SemaphoreType.DMA((n, N_DEV - 1)), pltpu.SemaphoreType.DMA((n, N_DEV - 1)),
                           pltpu.SemaphoreType.DMA((n,))]
    outs = pl.pallas_call(
        wrapped, name=name, grid=grid,
        in_specs=list(in_specs) + [hbm] * (na + nr),
        out_specs=list(out_specs) + [hbm] * (na + nr),
        out_shape=list(out_shape)
        + [jax.ShapeDtypeStruct((N_DEV,) + a.shape[-2:], a.dtype) for a, _ in ag]
        + [jax.ShapeDtypeStruct(b.shape, b.dtype) for b in rs],
        scratch_shapes=list(scratch_shapes) + sem_shapes,
        compiler_params=_cparams(sem),
    )(*args, *[a for a, _ in ag], *rs)
    return list(outs[:n_out]), list(outs[n_out:n_out + na]), list(outs[n_out + na:])


def exchange(ag=(), rs=()):
    def body(o_ref):
        o_ref[...] = jnp.zeros_like(o_ref)

    _, gathered, slots = _pcall(body, (), name="exchange", grid=(1,), in_specs=[],
                                out_specs=[pl.BlockSpec((8, LANES), lambda i: (0, 0))],
                                out_shape=[jax.ShapeDtypeStruct((8, LANES), F32)], sem=("arbitrary",), ag=ag, rs=rs)
    return gathered, slots


FFN_PAIR = 2


def _pair_cols(w_ref):
    return jnp.concatenate([w_ref[p] for p in range(FFN_PAIR)], axis=1)


def ffn_fwd(x, nw, wg, wu, wd, ag=()):
    T, D = x.shape
    F = wg.shape[2]
    P = FFN_PAIR
    J = wg.shape[0] // P
    tm = min(T, 1024)

    def body(x_ref, nw_ref, wg_ref, wu_ref, wd_ref, o_ref, xn_ref, acc_ref):
        j = pl.program_id(1)

        @pl.when(j == 0)
        def _():
            xn_ref[...] = _rms(x_ref[...], nw_ref[...]).astype(BF16)
            acc_ref[...] = jnp.zeros_like(acc_ref)

        xn = xn_ref[...]
        g = _dot(xn, _pair_cols(wg_ref))
        u = _dot(xn, _pair_cols(wu_ref))
        h = (g * _sigmoid(g) * u).astype(BF16)
        acc_ref[...] += _dot(h, wd_ref[...].reshape(P * F, D))

        @pl.when(j == J - 1)
        def _():
            o_ref[...] = x_ref[...] + 0.5 * acc_ref[...]

    (out,), gathered, _ = _pcall(
        body, (x, nw, wg, wu, wd), name="ffn_fwd", grid=(T // tm, J),
        in_specs=[pl.BlockSpec((tm, D), lambda t, j: (t, 0)),
                  pl.BlockSpec((1, D), lambda t, j: (0, 0)),
                  pl.BlockSpec((P, D, F), lambda t, j: (j, 0, 0)),
                  pl.BlockSpec((P, D, F), lambda t, j: (j, 0, 0)),
                  pl.BlockSpec((P, F, D), lambda t, j: (j, 0, 0))],
        out_specs=[pl.BlockSpec((tm, D), lambda t, j: (t, 0))],
        out_shape=[jax.ShapeDtypeStruct((T, D), F32)],
        scratch_shapes=[pltpu.VMEM((tm, D), BF16), pltpu.VMEM((tm, D), F32)],
        sem=("arbitrary", "arbitrary"), ag=ag)
    return out, gathered


def ffn_bwd(x, dy, nw, wg, wu, wd, rs=()):
    T, D = x.shape
    F = wg.shape[2]
    P = FFN_PAIR
    J = wg.shape[0] // P
    tm = min(T, 256)
    nt = T // tm

    def body(x_ref, dy_ref, nw_ref, wg_ref, wu_ref, wd_ref,
             dx_ref, dwg_ref, dwu_ref, dwd_ref, dnw_ref,
             xn_ref, dyh_ref, dxn_ref, awg_ref, awu_ref, awd_ref):
        j = pl.program_id(0)
        t = pl.program_id(1)
        rows = pl.ds(pl.multiple_of(t * tm, tm), tm)

        @pl.when(j == 0)
        def _():
            xn_ref[rows, :] = _rms(x_ref[...], nw_ref[...]).astype(BF16)
            dyh_ref[rows, :] = (0.5 * dy_ref[...]).astype(BF16)
            dxn_ref[rows, :] = jnp.zeros((tm, D), F32)

        @pl.when((j == 0) & (t == 0))
        def _():
            dnw_ref[...] = jnp.zeros_like(dnw_ref)

        @pl.when(t == 0)
        def _():
            awg_ref[...] = jnp.zeros_like(awg_ref)
            awu_ref[...] = jnp.zeros_like(awu_ref)
            awd_ref[...] = jnp.zeros_like(awd_ref)

        xn = xn_ref[rows, :]
        dyh = dyh_ref[rows, :]
        wg2, wu2 = _pair_cols(wg_ref), _pair_cols(wu_ref)
        g = _dot(xn, wg2)
        u = _dot(xn, wu2)
        sg = _sigmoid(g)
        s = g * sg
        h = (s * u).astype(BF16)
        dh = _dot_nt(dyh, wd_ref[...].reshape(P * F, D))
        du = (dh * s).astype(BF16)
        dg = (dh * u * (sg * (1.0 + g * (1.0 - sg)))).astype(BF16)
        awd_ref[...] += _dot_tn(h, dyh)
        awg_ref[...] += _dot_tn(xn, dg)
        awu_ref[...] += _dot_tn(xn, du)
        dxn_ref[rows, :] += _dot_nt(dg, wg2) + _dot_nt(du, wu2)

        @pl.when(t == nt - 1)
        def _():
            for p in range(P):
                dwg_ref[p] = awg_ref[:, p * F:(p + 1) * F].astype(BF16)
                dwu_ref[p] = awu_ref[:, p * F:(p + 1) * F].astype(BF16)
            dwd_ref[...] = awd_ref[...].astype(BF16).reshape(P, F, D)

        @pl.when(j == J - 1)
        def _():
            dx, dnw = _rms_bwd(x_ref[...], nw_ref[...], dxn_ref[rows, :])
            dx_ref[...] = dy_ref[...] + dx
            dnw_ref[...] += dnw

    ends = lambda j, t: (jnp.where((j == 0) | (j == J - 1), t, 0), 0)
    last = lambda j, t: (jnp.where(j == J - 1, t, 0), 0)
    outs, _, slots = _pcall(
        body, (x, dy, nw, wg, wu, wd), name="ffn_bwd", grid=(J, nt),
        in_specs=[pl.BlockSpec((tm, D), ends), pl.BlockSpec((tm, D), ends),
                  pl.BlockSpec((1, D), lambda j, t: (0, 0)),
                  pl.BlockSpec((P, D, F), lambda j, t: (j, 0, 0)),
                  pl.BlockSpec((P, D, F), lambda j, t: (j, 0, 0)),
                  pl.BlockSpec((P, F, D), lambda j, t: (j, 0, 0))],
        out_specs=[pl.BlockSpec((tm, D), last),
                   pl.BlockSpec((P, D, F), lambda j, t: (j, 0, 0)),
                   pl.BlockSpec((P, D, F), lambda j, t: (j, 0, 0)),
                   pl.BlockSpec((P, F, D), lambda j, t: (j, 0, 0)),
                   pl.BlockSpec((1, D), lambda j, t: (0, 0))],
        out_shape=[jax.ShapeDtypeStruct((T, D), F32),
                   jax.ShapeDtypeStruct((P * J, D, F), BF16), jax.ShapeDtypeStruct((P * J, D, F), BF16),
                   jax.ShapeDtypeStruct((P * J, F, D), BF16), jax.ShapeDtypeStruct((1, D), F32)],
        scratch_shapes=[pltpu.VMEM((T, D), BF16), pltpu.VMEM((T, D), BF16), pltpu.VMEM((T, D), F32),
                        pltpu.VMEM((D, P * F), F32), pltpu.VMEM((D, P * F), F32), pltpu.VMEM((P * F, D), F32)],
        sem=("arbitrary", "arbitrary"), rs=rs)
    return outs, slots


def rmslin_fwd(x, nw, w, b):
    T, D = x.shape
    N = w.shape[1]
    tm = min(T, 256)

    def body(x_ref, nw_ref, w_ref, b_ref, o_ref):
        xn = _rms(x_ref[...], nw_ref[...]).astype(BF16)
        o_ref[...] = _dot(xn, w_ref[...]) + b_ref[...]

    return pl.pallas_call(
        body, name="rmslin_fwd", grid=(T // tm,),
        in_specs=[pl.BlockSpec((tm, D), lambda t: (t, 0)), pl.BlockSpec((1, D), lambda t: (0, 0)),
                  pl.BlockSpec((D, N), lambda t: (0, 0)), pl.BlockSpec((1, N), lambda t: (0, 0))],
        out_specs=pl.BlockSpec((tm, N), lambda t: (t, 0)),
        out_shape=jax.ShapeDtypeStruct((T, N), F32),
        compiler_params=_cparams(("parallel",)),
    )(x, nw, w, b)


def rmslin_bwd(x, dres, dproj, nw, w):
    T, D = x.shape
    N = w.shape[1]
    nb = 1024
    nc = N // nb
    tm = min(T, 256)
    nt = T // tm

    def body(x_ref, dres_ref, dp_ref, nw_ref, w_ref, dx_ref, dw_ref, db_ref, dnw_ref, xn_ref, dxn_ref):
        c = pl.program_id(0)
        t = pl.program_id(1)
        rows = pl.ds(pl.multiple_of(t * tm, tm), tm)

        @pl.when(c == 0)
        def _():
            xn_ref[rows, :] = _rms(x_ref[...], nw_ref[...]).astype(BF16)
            dxn_ref[rows, :] = jnp.zeros((tm, D), F32)

        @pl.when((c == 0) & (t == 0))
        def _():
            dnw_ref[...] = jnp.zeros_like(dnw_ref)

        @pl.when(t == 0)
        def _():
            dw_ref[...] = jnp.zeros_like(dw_ref)
            db_ref[...] = jnp.zeros_like(db_ref)

        dpf = dp_ref[...]
        dp = dpf.astype(BF16)
        dw_ref[...] += _dot_tn(xn_ref[rows, :], dp)
        db_ref[...] += jnp.sum(dpf, axis=0, keepdims=True)
        dxn_ref[rows, :] += _dot_nt(dp, w_ref[...])

        @pl.when(c == nc - 1)
        def _():
            dx, dnw = _rms_bwd(x_ref[...], nw_ref[...], dxn_ref[rows, :])
            dx_ref[...] = dres_ref[...] + dx
            dnw_ref[...] += dnw

    ends = lambda c, t: (jnp.where((c == 0) | (c == nc - 1), t, 0), 0)
    last = lambda c, t: (jnp.where(c == nc - 1, t, 0), 0)
    return pl.pallas_call(
        body, name="rmslin_bwd", grid=(nc, nt),
        in_specs=[pl.BlockSpec((tm, D), ends), pl.BlockSpec((tm, D), last),
                  pl.BlockSpec((tm, nb), lambda c, t: (t, c)),
                  pl.BlockSpec((1, D), lambda c, t: (0, 0)),
                  pl.BlockSpec((D, nb), lambda c, t: (0, c))],
        out_specs=[pl.BlockSpec((tm, D), last),
                   pl.BlockSpec((D, nb), lambda c, t: (0, c)),
                   pl.BlockSpec((1, nb), lambda c, t: (0, c)),
                   pl.BlockSpec((1, D), lambda c, t: (0, 0))],
        out_shape=[jax.ShapeDtypeStruct((T, D), F32), jax.ShapeDtypeStruct((D, N), F32),
                   jax.ShapeDtypeStruct((1, N), F32), jax.ShapeDtypeStruct((1, D), F32)],
        scratch_shapes=[pltpu.VMEM((T, D), BF16), pltpu.VMEM((T, D), F32)],
        compiler_params=_cparams(("arbitrary", "arbitrary")),
    )(x, dres, dproj, nw, w)


def lin_fwd(res, parts, w, b):
    T = res.shape[0]
    K, N = w.shape
    tm = min(T, 512)
    n = len(parts)
    offs = [sum(p.shape[1] for p in parts[:i]) for i in range(n + 1)]

    def body(res_ref, *refs):
        a_refs, (w_ref, b_ref, o_ref) = refs[:n], refs[n:]
        acc = res_ref[...] + b_ref[...]
        for i, a_ref in enumerate(a_refs):
            acc = acc + _dot(a_ref[...].astype(BF16), w_ref[offs[i]:offs[i + 1], :])
        o_ref[...] = acc

    return pl.pallas_call(
        body, name="lin_fwd", grid=(T // tm,),
        in_specs=[pl.BlockSpec((tm, N), lambda t: (t, 0))]
        + [pl.BlockSpec((tm, p.shape[1]), lambda t: (t, 0)) for p in parts]
        + [pl.BlockSpec((K, N), lambda t: (0, 0)), pl.BlockSpec((1, N), lambda t: (0, 0))],
        out_specs=pl.BlockSpec((tm, N), lambda t: (t, 0)),
        out_shape=jax.ShapeDtypeStruct((T, N), F32),
        compiler_params=_cparams(("parallel",)),
    )(res, *parts, w, b)


def lin_bwd(parts, dy, w):
    T = dy.shape[0]
    K, N = w.shape
    tm = min(T, 256)
    n = len(parts)
    offs = [sum(p.shape[1] for p in parts[:i]) for i in range(n + 1)]

    def body(*refs):
        a_refs, (dy_ref, w_ref, da_ref, dw_ref, db_ref) = refs[:n], refs[n:]

        @pl.when(pl.program_id(0) == 0)
        def _():
            dw_ref[...] = jnp.zeros_like(dw_ref)
            db_ref[...] = jnp.zeros_like(db_ref)

        dyf = dy_ref[...]
        dyb = dyf.astype(BF16)
        da_ref[...] = _dot_nt(dyb, w_ref[...])
        for i, a_ref in enumerate(a_refs):
            dw_ref[offs[i]:offs[i + 1], :] += _dot_tn(a_ref[...].astype(BF16), dyb)
        db_ref[...] += jnp.sum(dyf, axis=0, keepdims=True)

    return pl.pallas_call(
        body, name="lin_bwd", grid=(T // tm,),
        in_specs=[pl.BlockSpec((tm, p.shape[1]), lambda t: (t, 0)) for p in parts]
        + [pl.BlockSpec((tm, N), lambda t: (t, 0)), pl.BlockSpec((K, N), lambda t: (0, 0))],
        out_specs=[pl.BlockSpec((tm, K), lambda t: (t, 0)), pl.BlockSpec((K, N), lambda t: (0, 0)),
                   pl.BlockSpec((1, N), lambda t: (0, 0))],
        out_shape=[jax.ShapeDtypeStruct((T, K), F32), jax.ShapeDtypeStruct((K, N), F32),
                   jax.ShapeDtypeStruct((1, N), F32)],
        compiler_params=_cparams(("arbitrary",)),
    )(*parts, dy, w)


def loss_fwd_bwd(x, fw, target):
    T, D = x.shape
    tm = min(T, 256)

    def body(x_ref, fw_ref, tg_ref, loss_ref, dx_ref, dfw_ref):
        @pl.when(pl.program_id(0) == 0)
        def _():
            loss_ref[...] = jnp.zeros_like(loss_ref)
            dfw_ref[...] = jnp.zeros_like(dfw_ref)

        xv = x_ref[...]
        w = fw_ref[...]
        err = _rms(xv, w) - tg_ref[...]
        row = jnp.sum(err * err, axis=-1, keepdims=True)
        loss_ref[...] += (0.5 / D) * jnp.sum(row, axis=0, keepdims=True)
        dx, dfw = _rms_bwd(xv, w, err * (1.0 / D))
        dx_ref[...] = dx
        dfw_ref[...] += dfw

    return pl.pallas_call(
        body, name="loss_fwd_bwd", grid=(T // tm,),
        in_specs=[pl.BlockSpec((tm, D), lambda t: (t, 0)), pl.BlockSpec((1, D), lambda t: (0, 0)),
                  pl.BlockSpec((tm, D), lambda t: (t, 0))],
        out_specs=[pl.BlockSpec((1, 1), lambda t: (0, 0)), pl.BlockSpec((tm, D), lambda t: (t, 0)),
                   pl.BlockSpec((1, D), lambda t: (0, 0))],
        out_shape=[jax.ShapeDtypeStruct((1, 1), F32), jax.ShapeDtypeStruct((T, D), F32),
                   jax.ShapeDtypeStruct((1, D), F32)],
        compiler_params=_cparams(("arbitrary",)),
    )(x, fw, target)


def _attn_masks(n, rows, blk):
    r = lax.broadcasted_iota(jnp.int32, (rows, 2 * blk), 0)
    jj = lax.broadcasted_iota(jnp.int32, (rows, 2 * blk), 1)
    dist = (r % blk) + blk - jj
    valid = (dist >= 0) & (dist < blk) & ((n > 0) | (jj >= blk))
    return dist.astype(F32), valid


def _attn_block(q, kcat, vcat, sink, slope, dist, valid):
    d = q.shape[-1]
    s = _dot_nt(q.astype(BF16), kcat.astype(BF16)) * (d ** -0.5)
    s = jnp.where(valid, s - slope * dist, -1e30)
    m = lax.stop_gradient(jnp.maximum(jnp.max(s, axis=-1, keepdims=True), sink))
    e = jnp.exp(s - m)
    p = e / (jnp.sum(e, axis=-1, keepdims=True) + jnp.exp(sink - m))
    return _dot(p.astype(BF16), vcat.astype(BF16))


ATTN_G = ATTN_HEADS // ATTN_KV_HEADS
ATTN_QW = ATTN_G * HEAD_DIM
ATTN_KCOL = Q_A // KV_A


def _attn_specs():
    blk = ATTN_BLOCK
    qs = pl.BlockSpec((blk, ATTN_QW), lambda h, n: (n, h))
    prev = lambda c: pl.BlockSpec((blk, KV_A), lambda h, n: (jnp.maximum(n - 1, 0), c))
    cur = lambda c: pl.BlockSpec((blk, KV_A), lambda h, n: (n, c))
    rowp = pl.BlockSpec((ATTN_G * blk, 1), lambda h, n: (h, 0))
    return qs, [prev(ATTN_KCOL), cur(ATTN_KCOL), prev(ATTN_KCOL + 1), cur(ATTN_KCOL + 1)], rowp


def _attn_operands(h, q_ref, kp_ref, kc_ref, vp_ref, vc_ref):
    d = HEAD_DIM
    q = jnp.concatenate([q_ref[:, g * d:(g + 1) * d] for g in range(ATTN_G)], axis=0)
    pick = lambda r: jnp.where(h == 0, r[:, :d], r[:, d:])
    kcat = jnp.concatenate([pick(kp_ref[...]), pick(kc_ref[...])], axis=0)
    vcat = jnp.concatenate([pick(vp_ref[...]), pick(vc_ref[...])], axis=0)
    return q, kcat, vcat


def attn_fwd(proj, sink_rows, slope_rows):
    T = proj.shape[0]
    blk, d = ATTN_BLOCK, HEAD_DIM

    def body(q_ref, kp_ref, kc_ref, vp_ref, vc_ref, sink_ref, slope_ref, o_ref):
        h, n = pl.program_id(0), pl.program_id(1)
        dist, valid = _attn_masks(n, ATTN_G * blk, blk)
        q, kcat, vcat = _attn_operands(h, q_ref, kp_ref, kc_ref, vp_ref, vc_ref)
        o = _attn_block(q, kcat, vcat, sink_ref[...], slope_ref[...], dist, valid)
        for g in range(ATTN_G):
            o_ref[:, g * d:(g + 1) * d] = o[g * blk:(g + 1) * blk]

    qs, kv, rowp = _attn_specs()
    return pl.pallas_call(
        body, name="attn_fwd", grid=(ATTN_KV_HEADS, T // blk),
        in_specs=[qs] + kv + [rowp, rowp],
        out_specs=qs,
        out_shape=jax.ShapeDtypeStruct((T, Q_A), F32),
        compiler_params=_cparams(("parallel", "parallel")),
    )(proj, proj, proj, proj, proj, sink_rows, slope_rows)


def attn_bwd(proj, sink_rows, slope_rows, dmix):
    T = proj.shape[0]
    blk, d = ATTN_BLOCK, HEAD_DIM

    def body(q_ref, kp_ref, kc_ref, vp_ref, vc_ref, sink_ref, slope_ref, do_ref, dq_ref, dkv_ref, dsink_ref):
        h, n = pl.program_id(0), pl.program_id(1)

        @pl.when((h == 0) & (n == 0))
        def _():
            dkv_ref[...] = jnp.zeros_like(dkv_ref)

        @pl.when(n == 0)
        def _():
            dsink_ref[...] = jnp.zeros_like(dsink_ref)

        dist, valid = _attn_masks(n, ATTN_G * blk, blk)
        q, kcat, vcat = _attn_operands(h, q_ref, kp_ref, kc_ref, vp_ref, vc_ref)
        do = jnp.concatenate([do_ref[:, g * d:(g + 1) * d] for g in range(ATTN_G)], axis=0)
        fn = functools.partial(_attn_block, slope=slope_ref[...], dist=dist, valid=valid)
        _, vjp = jax.vjp(fn, q, kcat, vcat, sink_ref[...])
        dq, dkcat, dvcat, dsink = vjp(do)
        for g in range(ATTN_G):
            dq_ref[:, g * d:(g + 1) * d] = dq[g * blk:(g + 1) * blk]
        dsink_ref[...] += dsink
        lane = lax.broadcasted_iota(jnp.int32, (2 * blk, 2 * KV_A), 1)
        mine = (lane % KV_A) // d == h
        both = jnp.where(mine, jnp.concatenate([dkcat, dkcat, dvcat, dvcat], axis=1), 0.0)

        @pl.when(n == 0)
        def _():
            dkv_ref[0:blk, :] += both[blk:]

        @pl.when(n > 0)
        def _():
            rows = pl.ds(pl.multiple_of((n - 1) * blk, blk), 2 * blk)
            dkv_ref[rows, :] += both

    qs, kv, rowp = _attn_specs()
    return pl.pallas_call(
        body, name="attn_bwd", grid=(ATTN_KV_HEADS, T // blk),
        in_specs=[qs] + kv + [rowp, rowp, qs],
        out_specs=[qs, pl.BlockSpec((T, 2 * KV_A), lambda h, n: (0, 0)), rowp],
        out_shape=[jax.ShapeDtypeStruct((T, Q_A), F32), jax.ShapeDtypeStruct((T, 2 * KV_A), F32),
                   jax.ShapeDtypeStruct((ATTN_HEADS * blk, 1), F32)],
        compiler_params=_cparams(("arbitrary", "arbitrary")),
    )(proj, proj, proj, proj, proj, sink_rows, slope_rows, dmix)


def _bmm(a, b, dims, exact):
    if exact:
        return lax.dot_general(a, b, dims, precision=HI, preferred_element_type=F32)
    return lax.dot_general(a.astype(BF16), b.astype(BF16), dims, preferred_element_type=F32)


def _bmm_nn(a, b, exact=False):
    return _bmm(a, b, (((2,), (1,)), ((0,), (0,))), exact)


def _bmm_nt(a, b, exact=False):
    return _bmm(a, b, (((2,), (2,)), ((0,), (0,))), exact)


def _bmm_tn(a, b, exact=False):
    return _bmm(a, b, (((1,), (1,)), ((0,), (0,))), exact)


def _dn_chunk(qc, kc, vc, zc, braw, araw, alog, dtb, nw, S):
    H, C, D = qc.shape
    row = lax.broadcasted_iota(jnp.int32, (H, C, C), 1)
    col = lax.broadcasted_iota(jnp.int32, (H, C, C), 2)
    causal = row >= col
    strict = row > col
    eye = (row == col).astype(F32)
    ltri = causal.astype(F32)
    ones = jnp.ones((H, C, C), F32)

    q = qc * lax.rsqrt(jnp.sum(qc * qc, axis=-1, keepdims=True) + EPS) * (D ** -0.5)
    k = kc * lax.rsqrt(jnp.sum(kc * kc, axis=-1, keepdims=True) + EPS)
    beta = _sigmoid(braw)
    g = -jnp.exp(alog) * _softplus(araw + dtb)
    a_col = _bmm_nn(ltri, jnp.broadcast_to(g, (H, C, C)), True)
    a_row = _bmm_nn(ones, eye * a_col, True)
    decay = jnp.where(causal, jnp.exp(jnp.where(causal, a_col - a_row, 0.0)), 0.0)
    kb = k * beta
    low = jnp.where(strict, _bmm_nt(kb, k, True) * decay, 0.0)
    e_col = jnp.exp(a_col)
    tinv = eye - low
    p = low
    for _ in range(5):
        p = _bmm_nn(p, p)
        tinv = tinv + _bmm_nn(tinv, p)
    u = _bmm_nn(tinv, vc * beta)
    w = _bmm_nn(tinv, kb * e_col)
    attn = _bmm_nt(q, k, True) * decay
    gl = a_col[:, C - 1:C, :]
    k_dec = k * jnp.exp(gl - a_col)
    v_new = u - _bmm_nn(w, S)
    o = _bmm_nn(q * e_col, S) + _bmm_nn(attn, v_new)
    s_new = S * jnp.exp(jnp.broadcast_to(gl, (H, D, D))) + _bmm_tn(k_dec, v_new)
    on = o * lax.rsqrt(jnp.mean(o * o, axis=-1, keepdims=True) + EPS) * nw
    return on * (zc * _sigmoid(zc)), s_new


DN_ZCOLS = IN_COLS_PAD - OFF_Z
DN_ZBLK = OFF_Z // DN_ZCOLS


def _dn_heads(a, off):
    return jnp.stack([a[:, off + h * DN_D:off + (h + 1) * DN_D] for h in range(DN_HEADS)])


def _dn_gate_cols(zb, off):
    return jnp.stack([zb[:, off + h:off + h + 1] for h in range(DN_HEADS)])


def _dn_operands(x_ref, zb_ref):
    x, zb = x_ref[...], zb_ref[...]
    return (_dn_heads(x, 0), _dn_heads(x, V_B), _dn_heads(x, 2 * V_B), _dn_heads(zb, 0),
            _dn_gate_cols(zb, V_B), _dn_gate_cols(zb, V_B + DN_HEADS))


def dn_fwd(qkvc, proj, alog, dtb, nw, ag=()):
    T = qkvc.shape[0]
    H, C, D = DN_HEADS, DN_CHUNK, DN_D
    N = T // C

    def body(x_ref, zb_ref, alog_ref, dtb_ref, nw_ref, o_ref, sall_ref, s_ref):
        @pl.when(pl.program_id(0) == 0)
        def _():
            s_ref[...] = jnp.zeros_like(s_ref)

        s_in = s_ref[...]
        sall_ref[0] = s_in
        on, s_new = _dn_chunk(*_dn_operands(x_ref, zb_ref), alog_ref[...], dtb_ref[...], nw_ref[...], s_in)
        for h in range(H):
            o_ref[:, h * D:(h + 1) * D] = on[h]
        s_ref[...] = s_new

    par = pl.BlockSpec((H, 1, 1), lambda n: (0, 0, 0))
    outs, gathered, _ = _pcall(
        body, (qkvc, proj, alog, dtb, nw), name="dn_fwd", grid=(N,),
        in_specs=[pl.BlockSpec((C, QKV_B), lambda n: (n, 0)), pl.BlockSpec((C, DN_ZCOLS), lambda n: (n, DN_ZBLK)),
                  par, par, pl.BlockSpec((1, 1, D), lambda n: (0, 0, 0))],
        out_specs=[pl.BlockSpec((C, V_B), lambda n: (n, 0)), pl.BlockSpec((1, H, D, D), lambda n: (n, 0, 0, 0))],
        out_shape=[jax.ShapeDtypeStruct((T, V_B), F32), jax.ShapeDtypeStruct((N, H, D, D), F32)],
        scratch_shapes=[pltpu.VMEM((H, D, D), F32)], sem=("arbitrary",), ag=ag)
    return outs, gathered


def dn_bwd(qkvc, proj, alog, dtb, nw, sall, dmix, rs=()):
    T = qkvc.shape[0]
    H, C, D = DN_HEADS, DN_CHUNK, DN_D
    N = T // C

    def body(x_ref, zb_ref, alog_ref, dtb_ref, nw_ref, sall_ref, do_ref,
             dx_ref, dzb_ref, dalog_ref, ddtb_ref, dnw_ref, ds_ref):
        @pl.when(pl.program_id(0) == 0)
        def _():
            ds_ref[...] = jnp.zeros_like(ds_ref)
            dalog_ref[...] = jnp.zeros_like(dalog_ref)
            ddtb_ref[...] = jnp.zeros_like(ddtb_ref)
            dnw_ref[...] = jnp.zeros_like(dnw_ref)

        args = (*_dn_operands(x_ref, zb_ref), alog_ref[...], dtb_ref[...], nw_ref[...], sall_ref[0])
        _, vjp = jax.vjp(_dn_chunk, *args)
        dq, dk, dv, dz, db, da, dalog, ddtb, dnw, ds = vjp((_dn_heads(do_ref[...], 0), ds_ref[...]))
        for h in range(H):
            cols = slice(h * D, (h + 1) * D)
            dx_ref[:, cols] = dq[h]
            dx_ref[:, V_B + h * D:V_B + (h + 1) * D] = dk[h]
            dx_ref[:, 2 * V_B + h * D:2 * V_B + (h + 1) * D] = dv[h]
            dzb_ref[:, cols] = dz[h]
        lane = lax.broadcasted_iota(jnp.int32, (C, LANES), 1)
        tail = jnp.zeros((C, LANES), F32)
        for h in range(H):
            tail = tail + jnp.where(lane == h, jnp.broadcast_to(db[h], (C, LANES)), 0.0)
            tail = tail + jnp.where(lane == H + h, jnp.broadcast_to(da[h], (C, LANES)), 0.0)
        dzb_ref[:, V_B:V_B + LANES] = tail
        dzb_ref[:, V_B + LANES:] = jnp.zeros((C, DN_ZCOLS - V_B - LANES), F32)
        dalog_ref[...] += dalog
        ddtb_ref[...] += ddtb
        dnw_ref[...] += dnw
        ds_ref[...] = ds

    par = pl.BlockSpec((H, 1, 1), lambda i: (0, 0, 0))
    nws = pl.BlockSpec((1, 1, D), lambda i: (0, 0, 0))
    outs, _, slots = _pcall(
        body, (qkvc, proj, alog, dtb, nw, sall, dmix), name="dn_bwd", grid=(N,),
        in_specs=[pl.BlockSpec((C, QKV_B), lambda i: (N - 1 - i, 0)),
                  pl.BlockSpec((C, DN_ZCOLS), lambda i: (N - 1 - i, DN_ZBLK)), par, par, nws,
                  pl.BlockSpec((1, H, D, D), lambda i: (N - 1 - i, 0, 0, 0)),
                  pl.BlockSpec((C, V_B), lambda i: (N - 1 - i, 1))],
        out_specs=[pl.BlockSpec((C, QKV_B), lambda i: (N - 1 - i, 0)),
                   pl.BlockSpec((C, DN_ZCOLS), lambda i: (N - 1 - i, 0)), par, par, nws],
        out_shape=[jax.ShapeDtypeStruct((T, QKV_B), F32), jax.ShapeDtypeStruct((T, DN_ZCOLS), F32)]
        + [jax.ShapeDtypeStruct((H, 1, 1), F32)] * 2 + [jax.ShapeDtypeStruct((1, 1, D), F32)],
        scratch_shapes=[pltpu.VMEM((H, D, D), F32)], sem=("arbitrary",), rs=rs)
    return outs, slots


def _conv_taps(buf_ref, w, width, halo, tm):
    acc = None
    for kk, win in _windows(buf_ref, [halo - (width - 1) + kk for kk in range(width)], tm):
        term = w[kk:kk + 1, :] * win
        acc = term if acc is None else acc + term
    return acc


def _windows(ref, offsets, tm):
    for res in range(SUBLANES):
        ks = [k for k, o in enumerate(offsets) if o % SUBLANES == res]
        if not ks:
            continue
        lo = min(offsets[k] for k in ks)
        hi = max(offsets[k] for k in ks)
        shifted = ref[pl.ds(lo, tm + hi - lo), :]
        for k in ks:
            yield k, shifted[offsets[k] - lo:offsets[k] - lo + tm]


def _conv_taps_bwd(dbuf_ref, w, width, tm):
    acc = None
    for kk, win in _windows(dbuf_ref, [width - 1 - kk for kk in range(width)], tm):
        term = w[kk:kk + 1, :] * win
        acc = term if acc is None else acc + term
    return acc


def _conv_dw_acc(dw_ref, dout, buf_ref, width, halo, tm):
    for kk, win in _windows(buf_ref, [halo - (width - 1) + kk for kk in range(width)], tm):
        dw_ref[pl.ds(kk, 1), :] += jnp.sum(dout * win, axis=0, keepdims=True)


DNC_HALO = 8
DNC_COLS = 768


def dnconv_fwd(proj, w):
    T = proj.shape[0]
    tm = min(T, 256)
    hb = tm // DNC_HALO

    def body(x_ref, h_ref, w_ref, o_ref, buf_ref):
        i = pl.program_id(0)
        buf_ref[0:DNC_HALO, :] = jnp.where(i > 0, h_ref[...], 0.0)
        buf_ref[DNC_HALO:, :] = x_ref[...]
        acc = _conv_taps(buf_ref, w_ref[...], DN_CONV, DNC_HALO, tm)
        o_ref[...] = acc * _sigmoid(acc)

    return pl.pallas_call(
        body, name="dnconv_fwd", grid=(T // tm, 2),
        in_specs=[pl.BlockSpec((tm, DNC_COLS), lambda i, c: (i, 1 + c)),
                  pl.BlockSpec((DNC_HALO, DNC_COLS), lambda i, c: (jnp.maximum(i * hb - 1, 0), 1 + c)),
                  pl.BlockSpec((DN_CONV, DNC_COLS), lambda i, c: (0, c))],
        out_specs=pl.BlockSpec((tm, DNC_COLS), lambda i, c: (i, c)),
        out_shape=jax.ShapeDtypeStruct((T, QKV_B), F32),
        scratch_shapes=[pltpu.VMEM((DNC_HALO + tm, DNC_COLS), F32)],
        compiler_params=_cparams(("parallel", "parallel")),
    )(proj, proj, w)


def dnconv_bwd(proj, w, dout):
    T = proj.shape[0]
    tm = min(T, 256)
    nt = T // tm
    hb = tm // DNC_HALO

    def body(x_ref, h_ref, w_ref, do_ref, dx_ref, dw_ref, buf_ref, dbuf_ref):
        r = pl.program_id(1)
        i = nt - 1 - r

        @pl.when(r == 0)
        def _():
            dw_ref[...] = jnp.zeros_like(dw_ref)
            dbuf_ref[tm:, :] = jnp.zeros((DNC_HALO, DNC_COLS), F32)

        buf_ref[0:DNC_HALO, :] = jnp.where(i > 0, h_ref[...], 0.0)
        buf_ref[DNC_HALO:, :] = x_ref[...]
        wv = w_ref[...]
        acc = _conv_taps(buf_ref, wv, DN_CONV, DNC_HALO, tm)
        sg = _sigmoid(acc)
        dacc = do_ref[...] * (sg * (1.0 + acc * (1.0 - sg)))
        dbuf_ref[0:tm, :] = dacc
        dx_ref[...] = _conv_taps_bwd(dbuf_ref, wv, DN_CONV, tm)
        _conv_dw_acc(dw_ref, dacc, buf_ref, DN_CONV, DNC_HALO, tm)
        dbuf_ref[tm:, :] = dacc[0:DNC_HALO, :]

    return pl.pallas_call(
        body, name="dnconv_bwd", grid=(2, nt),
        in_specs=[pl.BlockSpec((tm, DNC_COLS), lambda c, r: (nt - 1 - r, 1 + c)),
                  pl.BlockSpec((DNC_HALO, DNC_COLS), lambda c, r: (jnp.maximum((nt - 1 - r) * hb - 1, 0), 1 + c)),
                  pl.BlockSpec((DN_CONV, DNC_COLS), lambda c, r: (0, c)),
                  pl.BlockSpec((tm, DNC_COLS), lambda c, r: (nt - 1 - r, c))],
        out_specs=[pl.BlockSpec((tm, DNC_COLS), lambda c, r: (nt - 1 - r, c)),
                   pl.BlockSpec((DN_CONV, DNC_COLS), lambda c, r: (0, c))],
        out_shape=[jax.ShapeDtypeStruct((T, QKV_B), F32), jax.ShapeDtypeStruct((DN_CONV, QKV_B), F32)],
        scratch_shapes=[pltpu.VMEM((DNC_HALO + tm, DNC_COLS), F32), pltpu.VMEM((tm + DNC_HALO, DNC_COLS), F32)],
        compiler_params=_cparams(("parallel", "arbitrary")),
    )(proj, proj, w, dout)


CV_HALO = 32


def _cv_post(cv, lnw, lnb):
    mu = jnp.mean(cv, axis=-1, keepdims=True)
    xc = cv - mu
    y = xc * lax.rsqrt(jnp.mean(xc * xc, axis=-1, keepdims=True) + EPS) * lnw + lnb
    return y * _sigmoid(y)


def cv_fwd(ab, w, bdw, lnw, lnb, ag=()):
    T = ab.shape[0]
    D = ab.shape[1] // 2
    tm = min(T, 256)
    hb = tm // CV_HALO

    def body(a_ref, b_ref, ah_ref, bh_ref, w_ref, bdw_ref, lnw_ref, lnb_ref, o_ref, buf_ref):
        i = pl.program_id(0)
        buf_ref[0:CV_HALO, :] = jnp.where(i > 0, ah_ref[...] * _sigmoid(bh_ref[...]), 0.0)
        buf_ref[CV_HALO:, :] = a_ref[...] * _sigmoid(b_ref[...])
        cv = _conv_taps(buf_ref, w_ref[...], CONV_WIDTH, CV_HALO, tm) + bdw_ref[...]
        o_ref[...] = _cv_post(cv, lnw_ref[...], lnb_ref[...])

    halo = lambda c: pl.BlockSpec((CV_HALO, D), lambda i: (jnp.maximum(i * hb - 1, 0), c))
    vec = pl.BlockSpec((1, D), lambda i: (0, 0))
    (out,), gathered, _ = _pcall(
        body, (ab, ab, ab, ab, w, bdw, lnw, lnb), name="cv_fwd", grid=(T // tm,),
        in_specs=[pl.BlockSpec((tm, D), lambda i: (i, 0)), pl.BlockSpec((tm, D), lambda i: (i, 1)),
                  halo(0), halo(1), pl.BlockSpec((CONV_WIDTH, D), lambda i: (0, 0)), vec, vec, vec],
        out_specs=[pl.BlockSpec((tm, D), lambda i: (i, 0))],
        out_shape=[jax.ShapeDtypeStruct((T, D), F32)],
        scratch_shapes=[pltpu.VMEM((CV_HALO + tm, D), F32)], sem=("arbitrary",), ag=ag)
    return out, gathered


def cv_bwd(ab, w, bdw, lnw, lnb, dout, rs=()):
    T = ab.shape[0]
    D = ab.shape[1] // 2
    tm = min(T, 256)
    nt = T // tm
    hb = tm // CV_HALO

    def body(a_ref, b_ref, ah_ref, bh_ref, w_ref, bdw_ref, lnw_ref, lnb_ref, do_ref,
             da_ref, db_ref, dw_ref, dbdw_ref, dlnw_ref, dlnb_ref, buf_ref, dbuf_ref):
        r = pl.program_id(0)
        i = nt - 1 - r

        @pl.when(r == 0)
        def _():
            dw_ref[...] = jnp.zeros_like(dw_ref)
            dbdw_ref[...] = jnp.zeros_like(dbdw_ref)
            dlnw_ref[...] = jnp.zeros_like(dlnw_ref)
            dlnb_ref[...] = jnp.zeros_like(dlnb_ref)
            dbuf_ref[tm:, :] = jnp.zeros((CV_HALO, D), F32)

        a = a_ref[...]
        sb = _sigmoid(b_ref[...])
        buf_ref[0:CV_HALO, :] = jnp.where(i > 0, ah_ref[...] * _sigmoid(bh_ref[...]), 0.0)
        buf_ref[CV_HALO:, :] = a * sb
        wv = w_ref[...]
        cv = _conv_taps(buf_ref, wv, CONV_WIDTH, CV_HALO, tm) + bdw_ref[...]
        _, vjp = jax.vjp(_cv_post, cv, lnw_ref[...], lnb_ref[...])
        dcv, dlnw, dlnb = vjp(do_ref[...])
        dlnw_ref[...] += dlnw
        dlnb_ref[...] += dlnb
        dbdw_ref[...] += jnp.sum(dcv, axis=0, keepdims=True)
        dbuf_ref[0:tm, :] = dcv
        du = _conv_taps_bwd(dbuf_ref, wv, CONV_WIDTH, tm)
        _conv_dw_acc(dw_ref, dcv, buf_ref, CONV_WIDTH, CV_HALO, tm)
        dbuf_ref[tm:, :] = dcv[0:CV_HALO, :]
        da_ref[...] = du * sb
        db_ref[...] = du * a * sb * (1.0 - sb)

    tile = lambda c: pl.BlockSpec((tm, D), lambda r: (nt - 1 - r, c))
    halo = lambda c: pl.BlockSpec((CV_HALO, D), lambda r: (jnp.maximum((nt - 1 - r) * hb - 1, 0), c))
    vec = pl.BlockSpec((1, D), lambda r: (0, 0))
    wsp = pl.BlockSpec((CONV_WIDTH, D), lambda r: (0, 0))
    (da, db, dw, dbdw, dlnw, dlnb), _, slots = _pcall(
        body, (ab, ab, ab, ab, w, bdw, lnw, lnb, dout), name="cv_bwd", grid=(nt,),
        in_specs=[tile(0), tile(1), halo(0), halo(1), wsp, vec, vec, vec, tile(0)],
        out_specs=[tile(0), tile(0), wsp, vec, vec, vec],
        out_shape=[jax.ShapeDtypeStruct((T, D), F32), jax.ShapeDtypeStruct((T, D), F32),
                   jax.ShapeDtypeStruct((CONV_WIDTH, D), F32)] + [jax.ShapeDtypeStruct((1, D), F32)] * 3,
        scratch_shapes=[pltpu.VMEM((CV_HALO + tm, D), F32), pltpu.VMEM((tm + CV_HALO, D), F32)],
        sem=("arbitrary",), rs=rs)
    return (jnp.concatenate([da, db], axis=1), dw, dbdw, dlnw, dlnb), slots


def adamw(w, m, v, slots):
    L, R, C = w.shape
    fits = lambda r, c: N_DEV * r * c * 2 <= ADAM_SLOT_BLOCK
    tiles = [(R, C)] if fits(R, C) else []
    tiles += [(d, C) for d in range(16, R, 16) if R % d == 0 and fits(d, C)]
    tiles += [(R, d) for d in range(LANES, C, LANES) if C % d == 0 and fits(R, d)]
    tr, tc = max(tiles, key=lambda t: t[0] * t[1])
    c1 =1.0 / (1.0 - ADAM_B1 ** ADAM_STEP)
    c2 = 1.0 / (1.0 - ADAM_B2 ** ADAM_STEP)

    def body(w_ref, m_ref, v_ref, *rest):
        s_refs = rest[:L]
        g_ref, d_ref, nm_ref, nv_ref = rest[L:]
        l = pl.program_id(0)
        for k in range(L):
            @pl.when(l == k)
            def _(s_ref=s_refs[k]):
                g = s_ref[0].astype(F32)
                for j in range(1, N_DEV):
                    g = g + s_ref[j].astype(F32)
                nm = ADAM_B1 * m_ref[0] + (1.0 - ADAM_B1) * g
                nv = ADAM_B2 * v_ref[0] + (1.0 - ADAM_B2) * (g * g)
                g_ref[0] = g
                nm_ref[0] = nm
                nv_ref[0] = nv
                d_ref[0] = -ADAM_LR * ((nm * c1) / (jnp.sqrt(nv * c2) + ADAM_EPS) + ADAM_WD * w_ref[0])

    nc = C // tc
    blk = pl.BlockSpec((1, tr, tc), lambda l, i: (l, i // nc, i % nc))
    slot = lambda k: pl.BlockSpec((N_DEV, tr, tc), lambda l, i: (0, jnp.where(l == k, i // nc, 0),
                                                                 jnp.where(l == k, i % nc, 0)))
    return pl.pallas_call(
        body, name="adamw", grid=(L, (R // tr) * nc),
        in_specs=[blk, blk, blk] + [slot(k) for k in range(L)],
        out_specs=[blk, blk, blk, blk],
        out_shape=[jax.ShapeDtypeStruct((L, R, C), F32)] * 4,
        compiler_params=_cparams(("arbitrary", "arbitrary")),
    )(w, m, v, *slots)


def _unshard(g, axis):
    g = jnp.moveaxis(g, 0, axis)
    s = g.shape
    return g.reshape(s[:axis] + (s[axis] * s[axis + 1],) + s[axis + 2:])


def _to_blocks(full, axis):
    s = full.shape
    g = full.reshape(s[:axis] + (N_DEV, s[axis] // N_DEV) + s[axis + 1:])
    return jnp.moveaxis(g, axis, 0)


def _heads(a, h):
    T = a.shape[0]
    return a.reshape(T, h, a.shape[1] // h).transpose(1, 0, 2)


def _unheads(a):
    h, T, d = a.shape
    return a.transpose(1, 0, 2).reshape(T, h * d)


SMALL = (("norm_w", 2), ("dn_conv_w", 2), ("conv_b_pw1", 1), ("conv_w_dw", 2), ("conv_b_dw", 1),
         ("conv_ln_w", 1), ("conv_ln_b", 1), ("conv_b_pw2", 1),
         ("attn_sinks", None), ("dn_a_log", None), ("dn_dt_bias", None), ("dn_norm_w", None), ("final_norm_w", None))

def _pack(parts):
    flat = jnp.concatenate([p.reshape(-1) for p in parts])
    pad = (-flat.shape[0]) % LANES
    return jnp.pad(flat, (0, pad))


def _unpack(flat, shapes):
    out, off = [], 0
    for s in shapes:
        n = int(np.prod(s))
        out.append(flat[off:off + n].reshape(s))
        off += n
    return out


def kernel(x, norm_w, ffn_w_gate, ffn_w_up, ffn_w_down, mix_w_in, dn_conv_w, attn_sinks, dn_a_log, dn_dt_bias, dn_norm_w, mix_w_out, conv_w_pw1, conv_b_pw1, conv_w_dw, conv_b_dw, conv_ln_w, conv_ln_b, conv_w_pw2, conv_b_pw2, final_norm_w, loss_target, m_norm_w, m_ffn_w_gate, m_ffn_w_up, m_ffn_w_down, m_mix_w_in, m_dn_conv_w, m_attn_sinks, m_dn_a_log, m_dn_dt_bias, m_dn_norm_w, m_mix_w_out, m_conv_w_pw1, m_conv_b_pw1, m_conv_w_dw, m_conv_b_dw, m_conv_ln_w, m_conv_ln_b, m_conv_w_pw2, m_conv_b_pw2, m_final_norm_w, v_norm_w, v_ffn_w_gate, v_ffn_w_up, v_ffn_w_down, v_mix_w_in, v_dn_conv_w, v_attn_sinks, v_dn_a_log, v_dn_dt_bias, v_dn_norm_w, v_mix_w_out, v_conv_w_pw1, v_conv_b_pw1, v_conv_w_dw, v_conv_b_dw, v_conv_ln_w, v_conv_ln_b, v_conv_w_pw2, v_conv_b_pw2, v_final_norm_w):
    W = dict(norm_w=norm_w, ffn_w_gate=ffn_w_gate, ffn_w_up=ffn_w_up, ffn_w_down=ffn_w_down, mix_w_in=mix_w_in,
             dn_conv_w=dn_conv_w, attn_sinks=attn_sinks, dn_a_log=dn_a_log, dn_dt_bias=dn_dt_bias,
             dn_norm_w=dn_norm_w, mix_w_out=mix_w_out, conv_w_pw1=conv_w_pw1, conv_b_pw1=conv_b_pw1,
             conv_w_dw=conv_w_dw, conv_b_dw=conv_b_dw, conv_ln_w=conv_ln_w, conv_ln_b=conv_ln_b,
             conv_w_pw2=conv_w_pw2, conv_b_pw2=conv_b_pw2, final_norm_w=final_norm_w)
    M = dict(norm_w=m_norm_w, ffn_w_gate=m_ffn_w_gate, ffn_w_up=m_ffn_w_up, ffn_w_down=m_ffn_w_down,
             mix_w_in=m_mix_w_in, dn_conv_w=m_dn_conv_w, attn_sinks=m_attn_sinks, dn_a_log=m_dn_a_log,
             dn_dt_bias=m_dn_dt_bias, dn_norm_w=m_dn_norm_w, mix_w_out=m_mix_w_out, conv_w_pw1=m_conv_w_pw1,
             conv_b_pw1=m_conv_b_pw1, conv_w_dw=m_conv_w_dw, conv_b_dw=m_conv_b_dw, conv_ln_w=m_conv_ln_w,
             conv_ln_b=m_conv_ln_b, conv_w_pw2=m_conv_w_pw2, conv_b_pw2=m_conv_b_pw2, final_norm_w=m_final_norm_w)
    V = dict(norm_w=v_norm_w, ffn_w_gate=v_ffn_w_gate, ffn_w_up=v_ffn_w_up, ffn_w_down=v_ffn_w_down,
             mix_w_in=v_mix_w_in, dn_conv_w=v_dn_conv_w, attn_sinks=v_attn_sinks, dn_a_log=v_dn_a_log,
             dn_dt_bias=v_dn_dt_bias, dn_norm_w=v_dn_norm_w, mix_w_out=v_mix_w_out, conv_w_pw1=v_conv_w_pw1,
             conv_b_pw1=v_conv_b_pw1, conv_w_dw=v_conv_w_dw, conv_b_dw=v_conv_b_dw, conv_ln_w=v_conv_ln_w,
             conv_ln_b=v_conv_ln_b, conv_w_pw2=v_conv_w_pw2, conv_b_pw2=v_conv_b_pw2, final_norm_w=v_final_norm_w)

    T, D = x.shape[1], x.shape[2]
    xs = x[0]
    F8 = ffn_w_gate.shape[-1]
    n_ffn = DEPTH * 2

    big = ("ffn_w_gate", "ffn_w_up", "ffn_w_down", "mix_w_in", "mix_w_out", "conv_w_pw1", "conv_w_pw2")
    shard3 = {k: W[k].reshape((-1,) + W[k].shape[-2:]) for k in big}
    shard_bf = {k: shard3[k].astype(BF16) for k in big}
    ffn_unit = lambda i: [("ffn_w_gate", i), ("ffn_w_up", i), ("ffn_w_down", i)]
    even_unit = lambda e: [("mix_w_in", e), ("mix_w_out", e)]
    odd_unit = lambda e: [("conv_w_pw1", e), ("conv_w_pw2", e)]
    have = {}

    def ag_jobs(units):
        return [(shard_bf[k], i) for k, i in units]

    def ag_done(units, gathered):
        have.update(zip(units, gathered))

    small_sharded = [(k, ax) for k, ax in SMALL if ax is not None]
    small_pack = _pack([W[k] for k, _ in small_sharded])[None, :]
    first_units = ffn_unit(0)
    gathered, _ = exchange(ag=ag_jobs(first_units) + [(small_pack, None)])
    ag_done(first_units, gathered[:-1])
    small_full = {}
    for (k, ax), parts in zip(small_sharded,
                              zip(*[_unpack(gathered[-1][s, 0], [W[k].shape for k, _ in small_sharded])
                                    for s in range(N_DEV)])):
        small_full[k] = _unshard(jnp.stack(parts), ax)
    nw_full = small_full["norm_w"]

    ffn_w = lambda i: [have[u] for u in ffn_unit(i)]
    w_in_of = lambda e: jnp.pad(_unshard(have[("mix_w_in", e)], 1), ((0, 0), (0, IN_COLS_PAD - IN_COLS)))
    w_out_of = lambda e: have[("mix_w_out", e)].reshape(D, D)
    w_pw1_of = lambda e: _unshard(have[("conv_w_pw1", e)], 1)
    w_pw2_of = lambda e: have[("conv_w_pw2", e)].reshape(D, D)
    fwd_order = [key for l in range(DEPTH) for key in (("F", 2 * l), ("M", l), ("F", 2 * l + 1))]
    needed = {}
    for l in range(DEPTH):
        needed[("F", 2 * l)], needed[("F", 2 * l + 1)] = ffn_unit(2 * l), ffn_unit(2 * l + 1)
        needed[("M", l)] = even_unit(l // 2) if l % 2 == 0 else odd_unit(l // 2)
    queue = [(u, pos) for pos, key in enumerate(fwd_order) for u in needed[key] if u not in first_units]
    unit_bytes = lambda u: N_DEV * shard_bf[u[0]][u[1]].size * 2
    fwd_carry, at = {}, 0
    for pos, key in enumerate(fwd_order):
        cap = FWD_CARRY_BYTES[key[0] if key[0] == "F" else ("E" if key[1] % 2 == 0 else "O")]
        taken, used = [], 0
        while at < len(queue) and (queue[at][1] <= pos + 1 or used + unit_bytes(queue[at][0]) <= cap):
            taken.append(queue[at][0])
            used += unit_bytes(queue[at][0])
            at += 1
        fwd_carry[key] = taken
    zero_in = jnp.zeros((1, IN_COLS_PAD), F32)
    zero_d = jnp.zeros((1, D), F32)
    slope_rows = jnp.asarray(np.repeat(2.0 ** (-8.0 * np.arange(1, ATTN_HEADS + 1) / ATTN_HEADS), ATTN_BLOCK)
                             .astype(np.float32)[:, None])

    saved = []
    h = xs
    w_in, w_out, w_pw1, w_pw2 = {}, {}, {}, {}

    def ffn_forward(h, l, half):
        i = 2 * l + half
        units = fwd_carry.get(("F", i), [])
        h, gathered = ffn_fwd(h, nw_full[l, 2 * half][None], *ffn_w(i), ag=ag_jobs(units))
        ag_done(units, gathered)
        return h

    for l in range(DEPTH):
        e = l // 2
        st = {"x0": h}
        h = ffn_forward(h, l, 0)
        st["x1"] = h
        units = fwd_carry.get(("M", l), [])
        if l % 2 == 0:
            w_in[e], w_out[e] = w_in_of(e), w_out_of(e)
            proj = rmslin_fwd(h, nw_full[l, 1][None], w_in[e], zero_in)
            st["proj"] = proj
            st["qkvc"] = dnconv_fwd(proj, small_full["dn_conv_w"][e])
            st["sink_rows"] = jnp.repeat(attn_sinks[e], ATTN_BLOCK)[:, None]
            st["alog"] = dn_a_log[e].reshape(DN_HEADS, 1, 1)
            st["dtb"] = dn_dt_bias[e].reshape(DN_HEADS, 1, 1)
            st["dnw"] = dn_norm_w[e].reshape(1, 1, DN_D)
            st["att"] = attn_fwd(proj, st["sink_rows"], slope_rows)
            (st["og"], st["sall"]), gathered = dn_fwd(st["qkvc"], proj, st["alog"], st["dtb"], st["dnw"],
                                                      ag=ag_jobs(units))
            ag_done(units, gathered)
            h = lin_fwd(h, [st["att"], st["og"]], w_out[e], zero_d)
        else:
            w_pw1[e], w_pw2[e] = w_pw1_of(e), w_pw2_of(e)
            st["ab"] = rmslin_fwd(h, nw_full[l, 1][None], w_pw1[e], small_full["conv_b_pw1"][e][None])
            st["act"], gathered = cv_fwd(st["ab"], small_full["conv_w_dw"][e], small_full["conv_b_dw"][e][None],
                                         small_full["conv_ln_w"][e][None], small_full["conv_ln_b"][e][None],
                                         ag=ag_jobs(units))
            ag_done(units, gathered)
            h = lin_fwd(h, [st["act"]], w_pw2[e], small_full["conv_b_pw2"][e][None])
        st["x2"] = h
        h = ffn_forward(h, l, 1)
        saved.append(st)

    loss_part, dh, dfinal = loss_fwd_bwd(h, final_norm_w[None], loss_target[0])
    loss = lax.psum(loss_part[0, 0], ("x", "y", "c"))

    d_norm = [[None] * 3 for _ in range(DEPTH)]
    d_small = {k: [None, None] for k in ("dn_conv_w", "conv_b_pw1", "conv_w_dw", "conv_b_dw", "conv_ln_w",
                                         "conv_ln_b", "conv_b_pw2", "attn_sinks", "dn_a_log", "dn_dt_bias",
                                         "dn_norm_w")}
    pending, slot = [], {}

    def take_pending(cap=None):
        n, used = 0, 0
        while n < len(pending) and (cap is None or used + pending[n][1].size * 2 <= cap):
            used += pending[n][1].size * 2
            n += 1
        units = pending[:n]
        del pending[:n]
        return [u for u, _ in units], [b for _, b in units]

    def ffn_backward(dh, l, half):
        i = 2 * l + half
        units, blocks = take_pending(BWD_CARRY_BYTES["F"])
        (dh, dg, du, dd, d_norm[l][2 * half]), slots = ffn_bwd(
            st["x2" if half else "x0"], dh, nw_full[l, 2 * half][None], *ffn_w(i), rs=blocks)
        slot.update(zip(units, slots))
        pending.extend(zip(ffn_unit(i), (dg, du, dd)))
        return dh

    for l in reversed(range(DEPTH)):
        e = l // 2
        st = saved[l]
        dh = ffn_backward(dh, l, 1)
        if l % 2 == 0:
            dmix, d_out, _ = lin_bwd([st["att"], st["og"]], dh, w_out[e])
            pending.append((("mix_w_out", e), d_out.reshape(N_DEV, D // N_DEV, D).astype(BF16)))
            units, blocks = take_pending(BWD_CARRY_BYTES["E"])
            (dqkvc, dzba, dalog, ddtb, ddnw), slots = dn_bwd(
                st["qkvc"], st["proj"], st["alog"], st["dtb"], st["dnw"], st["sall"], dmix, rs=blocks)
            slot.update(zip(units, slots))
            dqa, dkva, dsink = attn_bwd(st["proj"], st["sink_rows"], slope_rows, dmix)
            dqkv, d_small["dn_conv_w"][e] = dnconv_bwd(st["proj"], small_full["dn_conv_w"][e], dqkvc)
            dproj = jnp.concatenate([dqa, dkva, dqkv, dzba], axis=1)
            dh, d_in, _, d_norm[l][1] = rmslin_bwd(st["x1"], dh, dproj, nw_full[l, 1][None], w_in[e])
            pending.append((("mix_w_in", e), _to_blocks(d_in[:, :IN_COLS], 1).astype(BF16)))
            d_small["attn_sinks"][e] = jnp.sum(dsink.reshape(ATTN_HEADS, ATTN_BLOCK), axis=1)
            d_small["dn_a_log"][e] = dalog.reshape(DN_HEADS)
            d_small["dn_dt_bias"][e] = ddtb.reshape(DN_HEADS)
            d_small["dn_norm_w"][e] = ddnw.reshape(DN_D)
        else:
            dact, d_pw2, d_small["conv_b_pw2"][e] = lin_bwd([st["act"]], dh, w_pw2[e])
            pending.append((("conv_w_pw2", e), d_pw2.reshape(N_DEV, D // N_DEV, D).astype(BF16)))
            units, blocks = take_pending(BWD_CARRY_BYTES["O"])
            (dab, d_small["conv_w_dw"][e], d_small["conv_b_dw"][e], d_small["conv_ln_w"][e],
             d_small["conv_ln_b"][e]), slots = cv_bwd(
                st["ab"], small_full["conv_w_dw"][e], small_full["conv_b_dw"][e][None],
                small_full["conv_ln_w"][e][None], small_full["conv_ln_b"][e][None], dact, rs=blocks)
            slot.update(zip(units, slots))
            dh, d_pw1, d_small["conv_b_pw1"][e], d_norm[l][1] = rmslin_bwd(
                st["x1"], dh, dab, nw_full[l, 1][None], w_pw1[e])
            pending.append((("conv_w_pw1", e), _to_blocks(d_pw1, 1).astype(BF16)))
        dh = ffn_backward(dh, l, 0)
    grad_x = dh[None]

    full_small = {"norm_w": jnp.stack([jnp.concatenate(r, axis=0) for r in d_norm]),
                  "final_norm_w": dfinal[0]}
    for k, pair in d_small.items():
        full_small[k] = jnp.stack([p.reshape(W[k].shape[1:-1] + (-1,)) if SMALL_AXIS[k] is not None
                                   else p for p in pair])
    rows = []
    for s in range(N_DEV):
        parts = [_to_blocks(full_small[k], ax)[s] if ax is not None else full_small[k] for k, ax in SMALL]
        rows.append(_pack(parts))
    send_small = jnp.stack(rows)[:, None, :]
    units, blocks = take_pending()
    _, slots = exchange(rs=blocks + [send_small])
    slot.update(zip(units, slots[:-1]))

    res = {}
    for k in big:
        outs = adamw(shard3[k], M[k].reshape(shard3[k].shape), V[k].reshape(shard3[k].shape),
                     [slot[(k, i)] for i in range(shard3[k].shape[0])])
        res[k] = [o.reshape(W[k].shape) for o in outs]
    pk = lambda d: _pack([d[k] for k, _ in SMALL])[None, None, :]
    outs = adamw(pk(W), pk(M), pk(V), [slots[-1]])
    shapes = [W[k].shape for k, _ in SMALL]
    unp = [_unpack(o[0, 0], shapes) for o in outs]
    for i, (k, _) in enumerate(SMALL):
        res[k] = [u[i] for u in unp]

    order = ("norm_w", "ffn_w_gate", "ffn_w_up", "ffn_w_down", "mix_w_in", "dn_conv_w", "attn_sinks", "dn_a_log",
             "dn_dt_bias", "dn_norm_w", "mix_w_out", "conv_w_pw1", "conv_b_pw1", "conv_w_dw", "conv_b_dw",
             "conv_ln_w", "conv_ln_b", "conv_w_pw2", "conv_b_pw2", "final_norm_w")
    return (loss, grad_x, *[res[k][0] for k in order], *[res[k][1] for k in order],
            *[res[k][2] for k in order], *[res[k][3] for k in order])


SMALL_AXIS = dict(SMALL)
```

```python
import functools

import numpy as np
import jax
import jax.numpy as jnp
from jax import lax
from jax.experimental import pallas as pl
from jax.experimental.pallas import tpu as pltpu

F32 = jnp.float32
BF16 = jnp.bfloat16
HI = lax.Precision.HIGHEST
EPS = 1e-6
N_DEV = 8
V7X_VMEM_LIMIT = 60 * 2**20
MESH = pl.DeviceIdType.MESH
LANES = 128
SUBLANES = 8

DEPTH = 4
D_MODEL = 1024
ATTN_HEADS, ATTN_KV_HEADS, HEAD_DIM, ATTN_BLOCK = 8, 2, 64, 128
DN_HEADS, DN_D, DN_CHUNK, DN_CONV = 8, 64, 64, 4
CONV_WIDTH = 31
Q_A, KV_A, QKV_B, V_B = 512, 128, 1536, 512
IN_COLS = 2832
IN_COLS_PAD = 3072
OFF_QKVB = Q_A + 2 * KV_A
OFF_Z = OFF_QKVB + QKV_B
OFF_BETA = OFF_Z + V_B
OFF_A = OFF_BETA + DN_HEADS

FWD_CARRY_BYTES = {"F": 12 * 2**20, "A": 6 * 2**20, "E": 18 * 2**20, "O": 12 * 2**20}
BWD_CARRY_BYTES = {"F": 12 * 2**20, "A": 6 * 2**20, "E": 24 * 2**20, "O": 12 * 2**20}

ADAM_SLOT_BLOCK = 3 * 2**19

ADAM_LR, ADAM_B1, ADAM_B2, ADAM_EPS, ADAM_WD, ADAM_STEP = 0.001, 0.9, 0.999, 1e-08, 0.01, 10


def _cparams(sem):
    return pltpu.CompilerParams(dimension_semantics=sem, vmem_limit_bytes=V7X_VMEM_LIMIT)


def _sigmoid(x):
    return 1.0 / (1.0 + jnp.exp(-x))


def _softplus(x):
    return jnp.maximum(x, 0.0) + jnp.log(1.0 + jnp.exp(-jnp.abs(x)))


def _dot(a, b):
    return jnp.dot(a, b, preferred_element_type=F32)


def _dot_nt(a, b):
    return lax.dot_general(a, b, (((1,), (1,)), ((), ())), preferred_element_type=F32)


def _dot_tn(a, b):
    return lax.dot_general(a, b, (((0,), (0,)), ((), ())), preferred_element_type=F32)


def _rms(x, w):
    return x * lax.rsqrt(jnp.mean(x * x, axis=-1, keepdims=True) + EPS) * w


def _rms_bwd(x, w, dxn):
    r = lax.rsqrt(jnp.mean(x * x, axis=-1, keepdims=True) + EPS)
    xh = x * r
    dxh = dxn * w
    dx = r * (dxh - xh * jnp.mean(dxh * xh, axis=-1, keepdims=True))
    return dx, jnp.sum(dxn * xh, axis=0, keepdims=True)


def _position():
    return lax.axis_index("x"), lax.axis_index("y"), lax.axis_index("c")


def _dev_index(px, py, pc):
    return 4 * px + 2 * py + pc


def _rcopy(src, dst, send_sem, recv_sem, to):
    return pltpu.make_async_remote_copy(src_ref=src, dst_ref=dst, send_sem=send_sem, recv_sem=recv_sem,
                                        device_id=to, device_id_type=MESH)


def _ag_start(srcs, outs, send, recv, local):
    x, y, c = _position()
    me = _dev_index(x, y, c)
    chips = [(1 - x, y), (x, 1 - y), (1 - x, 1 - y)]
    for a, (src, out) in enumerate(zip(srcs, outs)):
        pltpu.make_async_copy(src, out.at[me], local.at[a]).start()
        _rcopy(src, out.at[me], send.at[a, 0], recv.at[a, 0], (x, y, 1 - c)).start()
        for j, chip in enumerate(chips):
            _rcopy(src, out.at[me], send.at[a, 1 + j], recv.at[a, 1 + j], (*chip, c)).start()


def _ag_finish(srcs, outs, send, recv, local):
    x, y, c = _position()
    me = _dev_index(x, y, c)
    sibling = (x, y, 1 - c)
    chips = [(1 - x, y), (x, 1 - y), (1 - x, 1 - y)]
    for j, chip in enumerate(chips):
        for a, out in enumerate(outs):
            blk = out.at[_dev_index(*chip, c)]
            _rcopy(blk, blk, send.at[a, 1 + j], recv.at[a, 1 + j], (x, y, c)).wait_recv()
            _rcopy(blk, blk, send.at[a, 4 + j], recv.at[a, 4 + j], sibling).start()
    for a, (src, out) in enumerate(zip(srcs, outs)):
        blk = out.at[_dev_index(x, y, 1 - c)]
        _rcopy(blk, blk, send.at[a, 0], recv.at[a, 0], (x, y, c)).wait_recv()
        for j, chip in enumerate(chips):
            blk = out.at[_dev_index(*chip, 1 - c)]
            _rcopy(blk, blk, send.at[a, 4 + j], recv.at[a, 4 + j], (x, y, c)).wait_recv()
        for k in range(N_DEV - 1):
            _rcopy(out.at[me], out.at[me], send.at[a, k], recv.at[a, k], (x, y, c)).wait_send()
        pltpu.make_async_copy(src, out.at[me], local.at[a]).wait()


def _rs_peer(r):
    x, y, c = _position()
    return x ^ ((r >> 2) & 1), y ^ ((r >> 1) & 1), c ^ (r & 1)


def _rs_start(ins, outs, send, recv, local):
    me = _dev_index(*_position())
    for a, (src, out) in enumerate(zip(ins, outs)):
        pltpu.make_async_copy(src.at[me], out.at[me], local.at[a]).start()
        for r in range(1, N_DEV):
            p = _rs_peer(r)
            _rcopy(src.at[_dev_index(*p)], out.at[me], send.at[a, r - 1], recv.at[a, r - 1], p).start()


def _rs_finish(ins, outs, send, recv, local):
    pos = _position()
    me = _dev_index(*pos)
    for a, (src, out) in enumerate(zip(ins, outs)):
        for r in range(1, N_DEV):
            blk = out.at[_dev_index(*_rs_peer(r))]
            _rcopy(blk, blk, send.at[a, r - 1], recv.at[a, r - 1], pos).wait_recv()
        for r in range(1, N_DEV):
            _rcopy(src.at[me], out.at[me], send.at[a, r - 1], recv.at[a, r - 1], pos).wait_send()
        pltpu.make_async_copy(src.at[me], out.at[me], local.at[a]).wait()


def _pcall(body, args, *, name, grid, in_specs, out_specs, out_shape, sem, scratch_shapes=(), ag=(), rs=()):
    na, nr = len(ag), len(rs)
    if na + nr == 0:
        outs = pl.pallas_call(body, name=name, grid=grid, in_specs=in_specs, out_specs=out_specs,
                              out_shape=out_shape, scratch_shapes=list(scratch_shapes),
                              compiler_params=_cparams(sem))(*args)
        return list(outs), [], []
    n_in, n_out, n_scr = len(in_specs), len(out_specs), len(scratch_shapes)
    ag_idx = [i for _, i in ag]

    def wrapped(*refs):
        cin, refs = refs[:n_in], refs[n_in:]
        ag_in, refs = refs[:na], refs[na:]
        rs_in, refs = refs[:nr], refs[nr:]
        cout, refs = refs[:n_out], refs[n_out:]
        ag_out, refs = refs[:na], refs[na:]
        rs_out, refs = refs[:nr], refs[nr:]
        cscr, sems = refs[:n_scr], refs[n_scr:]
        ag_src = [r if i is None else r.at[i] for r, i in zip(ag_in, ag_idx)]
        ids = [pl.program_id(d) for d in range(len(grid))]
        first = functools.reduce(jnp.logical_and, [i == 0 for i in ids])
        last = functools.reduce(jnp.logical_and, [i == g - 1 for i, g in zip(ids, grid)])

        @pl.when(first)
        def _():
            if na:
                _ag_start(ag_src, ag_out, *sems[:3])
            if nr:
                _rs_start(rs_in, rs_out, *sems[-3:])

        body(*cin, *cout, *cscr)

        @pl.when(last)
        def _():
            if na:
                _ag_finish(ag_src, ag_out, *sems[:3])
            if nr:
                _rs_finish(rs_in, rs_out, *sems[-3:])

    hbm = pl.BlockSpec(memory_space=pl.ANY)
    sem_shapes = []
    for n in (na, nr):
        if n:
            sem_shapes += [pltpu.SemaphoreType.DMA((n, N_DEV - 1)), pltpu.SemaphoreType.DMA((n, N_DEV - 1)),
                           pltpu.SemaphoreType.DMA((n,))]
    outs = pl.pallas_call(
        wrapped, name=name, grid=grid,
        in_specs=list(in_specs) + [hbm] * (na + nr),
        out_specs=list(out_specs) + [hbm] * (na + nr),
        out_shape=list(out_shape)
        + [jax.ShapeDtypeStruct((N_DEV,) + a.shape[-2:], a.dtype) for a, _ in ag]
        + [jax.ShapeDtypeStruct(b.shape, b.dtype) for b in rs],
        scratch_shapes=list(scratch_shapes) + sem_shapes,
        compiler_params=_cparams(sem),
    )(*args, *[a for a, _ in ag], *rs)
    return list(outs[:n_out]), list(outs[n_out:n_out + na]), list(outs[n_out + na:])


def exchange(ag=(), rs=()):
    def body(o_ref):
        o_ref[...] = jnp.zeros_like(o_ref)

    _, gathered, slots = _pcall(body, (), name="exchange", grid=(1,), in_specs=[],
                                out_specs=[pl.BlockSpec((8, LANES), lambda i: (0, 0))],
                                out_shape=[jax.ShapeDtypeStruct((8, LANES), F32)], sem=("arbitrary",), ag=ag, rs=rs)
    return gathered, slots


FFN_PAIR = 2


def _pair_cols(w_ref):
    return jnp.concatenate([w_ref[p] for p in range(FFN_PAIR)], axis=1)


def ffn_fwd(x, nw, wg, wu, wd, ag=()):
    T, D = x.shape
    F = wg.shape[2]
    P = FFN_PAIR
    J = wg.shape[0] // P
    tm = min(T, 1024)

    def body(x_ref, nw_ref, wg_ref, wu_ref, wd_ref, o_ref, xn_ref, acc_ref):
        j = pl.program_id(1)

        @pl.when(j == 0)
        def _():
            xn_ref[...] = _rms(x_ref[...], nw_ref[...]).astype(BF16)
            acc_ref[...] = jnp.zeros_like(acc_ref)

        xn = xn_ref[...]
        g = _dot(xn, _pair_cols(wg_ref))
        u = _dot(xn, _pair_cols(wu_ref))
        h = (g * _sigmoid(g) * u).astype(BF16)
        acc_ref[...] += _dot(h, wd_ref[...].reshape(P * F, D))

        @pl.when(j == J - 1)
        def _():
            o_ref[...] = x_ref[...] + 0.5 * acc_ref[...]

    (out,), gathered, _ = _pcall(
        body, (x, nw, wg, wu, wd), name="ffn_fwd", grid=(T // tm, J),
        in_specs=[pl.BlockSpec((tm, D), lambda t, j: (t, 0)),
                  pl.BlockSpec((1, D), lambda t, j: (0, 0)),
                  pl.BlockSpec((P, D, F), lambda t, j: (j, 0, 0)),
                  pl.BlockSpec((P, D, F), lambda t, j: (j, 0, 0)),
                  pl.BlockSpec((P, F, D), lambda t, j: (j, 0, 0))],
        out_specs=[pl.BlockSpec((tm, D), lambda t, j: (t, 0))],
        out_shape=[jax.ShapeDtypeStruct((T, D), F32)],
        scratch_shapes=[pltpu.VMEM((tm, D), BF16), pltpu.VMEM((tm, D), F32)],
        sem=("arbitrary", "arbitrary"), ag=ag)
    return out, gathered


def ffn_bwd(x, dy, nw, wg, wu, wd, rs=()):
    T, D = x.shape
    F = wg.shape[2]
    P = FFN_PAIR
    J = wg.shape[0] // P
    tm = min(T, 256)
    nt = T // tm

    def body(x_ref, dy_ref, nw_ref, wg_ref, wu_ref, wd_ref,
             dx_ref, dwg_ref, dwu_ref, dwd_ref, dnw_ref,
             xn_ref, dyh_ref, dxn_ref, awg_ref, awu_ref, awd_ref):
        j = pl.program_id(0)
        t = pl.program_id(1)
        rows = pl.ds(pl.multiple_of(t * tm, tm), tm)

        @pl.when(j == 0)
        def _():
            xn_ref[rows, :] = _rms(x_ref[...], nw_ref[...]).astype(BF16)
            dyh_ref[rows, :] = (0.5 * dy_ref[...]).astype(BF16)
            dxn_ref[rows, :] = jnp.zeros((tm, D), F32)

        @pl.when((j == 0) & (t == 0))
        def _():
            dnw_ref[...] = jnp.zeros_like(dnw_ref)

        @pl.when(t == 0)
        def _():
            awg_ref[...] = jnp.zeros_like(awg_ref)
            awu_ref[...] = jnp.zeros_like(awu_ref)
            awd_ref[...] = jnp.zeros_like(awd_ref)

        xn = xn_ref[rows, :]
        dyh = dyh_ref[rows, :]
        wg2, wu2 = _pair_cols(wg_ref), _pair_cols(wu_ref)
        g = _dot(xn, wg2)
        u = _dot(xn, wu2)
        sg = _sigmoid(g)
        s = g * sg
        h = (s * u).astype(BF16)
        dh = _dot_nt(dyh, wd_ref[...].reshape(P * F, D))
        du = (dh * s).astype(BF16)
        dg = (dh * u * (sg * (1.0 + g * (1.0 - sg)))).astype(BF16)
        awd_ref[...] += _dot_tn(h, dyh)
        awg_ref[...] += _dot_tn(xn, dg)
        awu_ref[...] += _dot_tn(xn, du)
        dxn_ref[rows, :] += _dot_nt(dg, wg2) + _dot_nt(du, wu2)

        @pl.when(t == nt - 1)
        def _():
            for p in range(P):
                dwg_ref[p] = awg_ref[:, p * F:(p + 1) * F].astype(BF16)
                dwu_ref[p] = awu_ref[:, p * F:(p + 1) * F].astype(BF16)
            dwd_ref[...] = awd_ref[...].astype(BF16).reshape(P, F, D)

        @pl.when(j == J - 1)
        def _():
            dx, dnw = _rms_bwd(x_ref[...], nw_ref[...], dxn_ref[rows, :])
            dx_ref[...] = dy_ref[...] + dx
            dnw_ref[...] += dnw

    ends = lambda j, t: (jnp.where((j == 0) | (j == J - 1), t, 0), 0)
    last = lambda j, t: (jnp.where(j == J - 1, t, 0), 0)
    outs, _, slots = _pcall(
        body, (x, dy, nw, wg, wu, wd), name="ffn_bwd", grid=(J, nt),
        in_specs=[pl.BlockSpec((tm, D), ends), pl.BlockSpec((tm, D), ends),
                  pl.BlockSpec((1, D), lambda j, t: (0, 0)),
                  pl.BlockSpec((P, D, F), lambda j, t: (j, 0, 0)),
                  pl.BlockSpec((P, D, F), lambda j, t: (j, 0, 0)),
                  pl.BlockSpec((P, F, D), lambda j, t: (j, 0, 0))],
        out_specs=[pl.BlockSpec((tm, D), last),
                   pl.BlockSpec((P, D, F), lambda j, t: (j, 0, 0)),
                   pl.BlockSpec((P, D, F), lambda j, t: (j, 0, 0)),
                   pl.BlockSpec((P, F, D), lambda j, t: (j, 0, 0)),
                   pl.BlockSpec((1, D), lambda j, t: (0, 0))],
        out_shape=[jax.ShapeDtypeStruct((T, D), F32),
                   jax.ShapeDtypeStruct((P * J, D, F), BF16), jax.ShapeDtypeStruct((P * J, D, F), BF16),
                   jax.ShapeDtypeStruct((P * J, F, D), BF16), jax.ShapeDtypeStruct((1, D), F32)],
        scratch_shapes=[pltpu.VMEM((T, D), BF16), pltpu.VMEM((T, D), BF16), pltpu.VMEM((T, D), F32),
                        pltpu.VMEM((D, P * F), F32), pltpu.VMEM((D, P * F), F32), pltpu.VMEM((P * F, D), F32)],
        sem=("arbitrary", "arbitrary"), rs=rs)
    return outs, slots


def rmslin_fwd(x, nw, w, b):
    T, D = x.shape
    N = w.shape[1]
    tm = min(T, 256)

    def body(x_ref, nw_ref, w_ref, b_ref, o_ref):
        xn = _rms(x_ref[...], nw_ref[...]).astype(BF16)
        o_ref[...] = _dot(xn, w_ref[...]) + b_ref[...]

    return pl.pallas_call(
        body, name="rmslin_fwd", grid=(T // tm,),
        in_specs=[pl.BlockSpec((tm, D), lambda t: (t, 0)), pl.BlockSpec((1, D), lambda t: (0, 0)),
                  pl.BlockSpec((D, N), lambda t: (0, 0)), pl.BlockSpec((1, N), lambda t: (0, 0))],
        out_specs=pl.BlockSpec((tm, N), lambda t: (t, 0)),
        out_shape=jax.ShapeDtypeStruct((T, N), F32),
        compiler_params=_cparams(("parallel",)),
    )(x, nw, w, b)


def rmslin_bwd(x, dres, dproj, nw, w):
    T, D = x.shape
    N = w.shape[1]
    nb = 1024
    nc = N // nb
    tm = min(T, 256)
    nt = T // tm

    def body(x_ref, dres_ref, dp_ref, nw_ref, w_ref, dx_ref, dw_ref, db_ref, dnw_ref, xn_ref, dxn_ref):
        c = pl.program_id(0)
        t = pl.program_id(1)
        rows = pl.ds(pl.multiple_of(t * tm, tm), tm)

        @pl.when(c == 0)
        def _():
            xn_ref[rows, :] = _rms(x_ref[...], nw_ref[...]).astype(BF16)
            dxn_ref[rows, :] = jnp.zeros((tm, D), F32)

        @pl.when((c == 0) & (t == 0))
        def _():
            dnw_ref[...] = jnp.zeros_like(dnw_ref)

        @pl.when(t == 0)
        def _():
            dw_ref[...] = jnp.zeros_like(dw_ref)
            db_ref[...] = jnp.zeros_like(db_ref)

        dpf = dp_ref[...]
        dp = dpf.astype(BF16)
        dw_ref[...] += _dot_tn(xn_ref[rows, :], dp)
        db_ref[...] += jnp.sum(dpf, axis=0, keepdims=True)
        dxn_ref[rows, :] += _dot_nt(dp, w_ref[...])

        @pl.when(c == nc - 1)
        def _():
            dx, dnw = _rms_bwd(x_ref[...], nw_ref[...], dxn_ref[rows, :])
            dx_ref[...] = dres_ref[...] + dx
            dnw_ref[...] += dnw

    ends = lambda c, t: (jnp.where((c == 0) | (c == nc - 1), t, 0), 0)
    last = lambda c, t: (jnp.where(c == nc - 1, t, 0), 0)
    return pl.pallas_call(
        body, name="rmslin_bwd", grid=(nc, nt),
        in_specs=[pl.BlockSpec((tm, D), ends), pl.BlockSpec((tm, D), last),
                  pl.BlockSpec((tm, nb), lambda c, t: (t, c)),
                  pl.BlockSpec((1, D), lambda c, t: (0, 0)),
                  pl.BlockSpec((D, nb), lambda c, t: (0, c))],
        out_specs=[pl.BlockSpec((tm, D), last),
                   pl.BlockSpec((D, nb), lambda c, t: (0, c)),
                   pl.BlockSpec((1, nb), lambda c, t: (0, c)),
                   pl.BlockSpec((1, D), lambda c, t: (0, 0))],
        out_shape=[jax.ShapeDtypeStruct((T, D), F32), jax.ShapeDtypeStruct((D, N), F32),
                   jax.ShapeDtypeStruct((1, N), F32), jax.ShapeDtypeStruct((1, D), F32)],
        scratch_shapes=[pltpu.VMEM((T, D), BF16), pltpu.VMEM((T, D), F32)],
        compiler_params=_cparams(("arbitrary", "arbitrary")),
    )(x, dres, dproj, nw, w)


def lin_fwd(res, parts, w, b):
    T = res.shape[0]
    K, N = w.shape
    tm = min(T, 512)
    n = len(parts)
    offs = [sum(p.shape[1] for p in parts[:i]) for i in range(n + 1)]

    def body(res_ref, *refs):
        a_refs, (w_ref, b_ref, o_ref) = refs[:n], refs[n:]
        acc = res_ref[...] + b_ref[...]
        for i, a_ref in enumerate(a_refs):
            acc = acc + _dot(a_ref[...].astype(BF16), w_ref[offs[i]:offs[i + 1], :])
        o_ref[...] = acc

    return pl.pallas_call(
        body, name="lin_fwd", grid=(T // tm,),
        in_specs=[pl.BlockSpec((tm, N), lambda t: (t, 0))]
        + [pl.BlockSpec((tm, p.shape[1]), lambda t: (t, 0)) for p in parts]
        + [pl.BlockSpec((K, N), lambda t: (0, 0)), pl.BlockSpec((1, N), lambda t: (0, 0))],
        out_specs=pl.BlockSpec((tm, N), lambda t: (t, 0)),
        out_shape=jax.ShapeDtypeStruct((T, N), F32),
        compiler_params=_cparams(("parallel",)),
    )(res, *parts, w, b)


def lin_bwd(parts, dy, w):
    T = dy.shape[0]
    K, N = w.shape
    tm = min(T, 256)
    n = len(parts)
    offs = [sum(p.shape[1] for p in parts[:i]) for i in range(n + 1)]

    def body(*refs):
        a_refs, (dy_ref, w_ref, da_ref, dw_ref, db_ref) = refs[:n], refs[n:]

        @pl.when(pl.program_id(0) == 0)
        def _():
            dw_ref[...] = jnp.zeros_like(dw_ref)
            db_ref[...] = jnp.zeros_like(db_ref)

        dyf = dy_ref[...]
        dyb = dyf.astype(BF16)
        da_ref[...] = _dot_nt(dyb, w_ref[...])
        for i, a_ref in enumerate(a_refs):
            dw_ref[offs[i]:offs[i + 1], :] += _dot_tn(a_ref[...].astype(BF16), dyb)
        db_ref[...] += jnp.sum(dyf, axis=0, keepdims=True)

    return pl.pallas_call(
        body, name="lin_bwd", grid=(T // tm,),
        in_specs=[pl.BlockSpec((tm, p.shape[1]), lambda t: (t, 0)) for p in parts]
        + [pl.BlockSpec((tm, N), lambda t: (t, 0)), pl.BlockSpec((K, N), lambda t: (0, 0))],
        out_specs=[pl.BlockSpec((tm, K), lambda t: (t, 0)), pl.BlockSpec((K, N), lambda t: (0, 0)),
                   pl.BlockSpec((1, N), lambda t: (0, 0))],
        out_shape=[jax.ShapeDtypeStruct((T, K), F32), jax.ShapeDtypeStruct((K, N), F32),
                   jax.ShapeDtypeStruct((1, N), F32)],
        compiler_params=_cparams(("arbitrary",)),
    )(*parts, dy, w)


def loss_fwd_bwd(x, fw, target):
    T, D = x.shape
    tm = min(T, 256)

    def body(x_ref, fw_ref, tg_ref, loss_ref, dx_ref, dfw_ref):
        @pl.when(pl.program_id(0) == 0)
        def _():
            loss_ref[...] = jnp.zeros_like(loss_ref)
            dfw_ref[...] = jnp.zeros_like(dfw_ref)

        xv = x_ref[...]
        w = fw_ref[...]
        err = _rms(xv, w) - tg_ref[...]
        row = jnp.sum(err * err, axis=-1, keepdims=True)
        loss_ref[...] += (0.5 / D) * jnp.sum(row, axis=0, keepdims=True)
        dx, dfw = _rms_bwd(xv, w, err * (1.0 / D))
        dx_ref[...] = dx
        dfw_ref[...] += dfw

    return pl.pallas_call(
        body, name="loss_fwd_bwd", grid=(T // tm,),
        in_specs=[pl.BlockSpec((tm, D), lambda t: (t, 0)), pl.BlockSpec((1, D), lambda t: (0, 0)),
                  pl.BlockSpec((tm, D), lambda t: (t, 0))],
        out_specs=[pl.BlockSpec((1, 1), lambda t: (0, 0)), pl.BlockSpec((tm, D), lambda t: (t, 0)),
                   pl.BlockSpec((1, D), lambda t: (0, 0))],
        out_shape=[jax.ShapeDtypeStruct((1, 1), F32), jax.ShapeDtypeStruct((T, D), F32),
                   jax.ShapeDtypeStruct((1, D), F32)],
        compiler_params=_cparams(("arbitrary",)),
    )(x, fw, target)


def _attn_masks(n, rows, blk):
    r = lax.broadcasted_iota(jnp.int32, (rows, 2 * blk), 0)
    jj = lax.broadcasted_iota(jnp.int32, (rows, 2 * blk), 1)
    dist = (r % blk) + blk - jj
    valid = (dist >= 0) & (dist < blk) & ((n > 0) | (jj >= blk))
    return dist.astype(F32), valid


def _attn_block(q, kcat, vcat, sink, slope, dist, valid):
    d = q.shape[-1]
    s = _dot_nt(q.astype(BF16), kcat.astype(BF16)) * (d ** -0.5)
    s = jnp.where(valid, s - slope * dist, -1e30)
    m = lax.stop_gradient(jnp.maximum(jnp.max(s, axis=-1, keepdims=True), sink))
    e = jnp.exp(s - m)
    p = e / (jnp.sum(e, axis=-1, keepdims=True) + jnp.exp(sink - m))
    return _dot(p.astype(BF16), vcat.astype(BF16))


ATTN_G = ATTN_HEADS // ATTN_KV_HEADS
ATTN_QW = ATTN_G * HEAD_DIM
ATTN_KCOL = Q_A // KV_A


def _attn_specs():
    blk = ATTN_BLOCK
    qs = pl.BlockSpec((blk, ATTN_QW), lambda h, n: (n, h))
    prev = lambda c: pl.BlockSpec((blk, KV_A), lambda h, n: (jnp.maximum(n - 1, 0), c))
    cur = lambda c: pl.BlockSpec((blk, KV_A), lambda h, n: (n, c))
    rowp = pl.BlockSpec((ATTN_G * blk, 1), lambda h, n: (h, 0))
    return qs, [prev(ATTN_KCOL), cur(ATTN_KCOL), prev(ATTN_KCOL + 1), cur(ATTN_KCOL + 1)], rowp


def _attn_operands(h, q_ref, kp_ref, kc_ref, vp_ref, vc_ref):
    d = HEAD_DIM
    q = jnp.concatenate([q_ref[:, g * d:(g + 1) * d] for g in range(ATTN_G)], axis=0)
    pick = lambda r: jnp.where(h == 0, r[:, :d], r[:, d:])
    kcat = jnp.concatenate([pick(kp_ref[...]), pick(kc_ref[...])], axis=0)
    vcat = jnp.concatenate([pick(vp_ref[...]), pick(vc_ref[...])], axis=0)
    return q, kcat, vcat


def attn_fwd(proj, sink_rows, slope_rows, ag=()):
    T = proj.shape[0]
    blk, d = ATTN_BLOCK, HEAD_DIM

    def body(q_ref, kp_ref, kc_ref, vp_ref, vc_ref, sink_ref, slope_ref, o_ref):
        h, n = pl.program_id(0), pl.program_id(1)
        dist, valid = _attn_masks(n, ATTN_G * blk, blk)
        q, kcat, vcat = _attn_operands(h, q_ref, kp_ref, kc_ref, vp_ref, vc_ref)
        o = _attn_block(q, kcat, vcat, sink_ref[...], slope_ref[...], dist, valid)
        for g in range(ATTN_G):
            o_ref[:, g * d:(g + 1) * d] = o[g * blk:(g + 1) * blk]

    qs, kv, rowp = _attn_specs()
    (out,), gathered, _ = _pcall(
        body, (proj, proj, proj, proj, proj, sink_rows, slope_rows), name="attn_fwd",
        grid=(ATTN_KV_HEADS, T // blk), in_specs=[qs] + kv + [rowp, rowp], out_specs=[qs],
        out_shape=[jax.ShapeDtypeStruct((T, Q_A), F32)], sem=("arbitrary", "arbitrary"), ag=ag)
    return out, gathered


def attn_bwd(proj, sink_rows, slope_rows, dmix, rs=()):
    T = proj.shape[0]
    blk, d = ATTN_BLOCK, HEAD_DIM

    def body(q_ref, kp_ref, kc_ref, vp_ref, vc_ref, sink_ref, slope_ref, do_ref, dq_ref, dkv_ref, dsink_ref):
        h, n = pl.program_id(0), pl.program_id(1)

        @pl.when((h == 0) & (n == 0))
        def _():
            dkv_ref[...] = jnp.zeros_like(dkv_ref)

        @pl.when(n == 0)
        def _():
            dsink_ref[...] = jnp.zeros_like(dsink_ref)

        dist, valid = _attn_masks(n, ATTN_G * blk, blk)
        q, kcat, vcat = _attn_operands(h, q_ref, kp_ref, kc_ref, vp_ref, vc_ref)
        do = jnp.concatenate([do_ref[:, g * d:(g + 1) * d] for g in range(ATTN_G)], axis=0)
        fn = functools.partial(_attn_block, slope=slope_ref[...], dist=dist, valid=valid)
        _, vjp = jax.vjp(fn, q, kcat, vcat, sink_ref[...])
        dq, dkcat, dvcat, dsink = vjp(do)
        for g in range(ATTN_G):
            dq_ref[:, g * d:(g + 1) * d] = dq[g * blk:(g + 1) * blk]
        dsink_ref[...] += dsink
        lane = lax.broadcasted_iota(jnp.int32, (2 * blk, 2 * KV_A), 1)
        mine = (lane % KV_A) // d == h
        both = jnp.where(mine, jnp.concatenate([dkcat, dkcat, dvcat, dvcat], axis=1), 0.0)

        @pl.when(n == 0)
        def _():
            dkv_ref[0:blk, :] += both[blk:]

        @pl.when(n > 0)
        def _():
            rows = pl.ds(pl.multiple_of((n - 1) * blk, blk), 2 * blk)
            dkv_ref[rows, :] += both

    qs, kv, rowp = _attn_specs()
    outs, _, slots = _pcall(
        body, (proj, proj, proj, proj, proj, sink_rows, slope_rows, dmix), name="attn_bwd",
        grid=(ATTN_KV_HEADS, T // blk), in_specs=[qs] + kv + [rowp, rowp, qs],
        out_specs=[qs, pl.BlockSpec((T, 2 * KV_A), lambda h, n: (0, 0)), rowp],
        out_shape=[jax.ShapeDtypeStruct((T, Q_A), F32), jax.ShapeDtypeStruct((T, 2 * KV_A), F32),
                   jax.ShapeDtypeStruct((ATTN_HEADS * blk, 1), F32)],
        sem=("arbitrary", "arbitrary"), rs=rs)
    return outs, slots


def _bmm(a, b, dims, exact):
    if exact:
        return lax.dot_general(a, b, dims, precision=HI, preferred_element_type=F32)
    return lax.dot_general(a.astype(BF16), b.astype(BF16), dims, preferred_element_type=F32)


def _bmm_nn(a, b, exact=False):
    return _bmm(a, b, (((2,), (1,)), ((0,), (0,))), exact)


def _bmm_nt(a, b, exact=False):
    return _bmm(a, b, (((2,), (2,)), ((0,), (0,))), exact)


def _bmm_tn(a, b, exact=False):
    return _bmm(a, b, (((1,), (1,)), ((0,), (0,))), exact)


def _dn_chunk(qc, kc, vc, zc, braw, araw, alog, dtb, nw, S):
    H, C, D = qc.shape
    row = lax.broadcasted_iota(jnp.int32, (H, C, C), 1)
    col = lax.broadcasted_iota(jnp.int32, (H, C, C), 2)
    causal = row >= col
    strict = row > col
    eye = (row == col).astype(F32)
    ltri = causal.astype(F32)
    ones = jnp.ones((H, C, C), F32)

    q = qc * lax.rsqrt(jnp.sum(qc * qc, axis=-1, keepdims=True) + EPS) * (D ** -0.5)
    k = kc * lax.rsqrt(jnp.sum(kc * kc, axis=-1, keepdims=True) + EPS)
    beta = _sigmoid(braw)
    g = -jnp.exp(alog) * _softplus(araw + dtb)
    a_col = _bmm_nn(ltri, jnp.broadcast_to(g, (H, C, C)), True)
    a_row = _bmm_nn(ones, eye * a_col, True)
    decay = jnp.where(causal, jnp.exp(jnp.where(causal, a_col - a_row, 0.0)), 0.0)
    kb = k * beta
    low = jnp.where(strict, _bmm_nt(kb, k, True) * decay, 0.0)
    e_col = jnp.exp(a_col)
    tinv = eye - low
    p = low
    for _ in range(5):
        p = _bmm_nn(p, p)
        tinv = tinv + _bmm_nn(tinv, p)
    u = _bmm_nn(tinv, vc * beta)
    w = _bmm_nn(tinv, kb * e_col)
    attn = _bmm_nt(q, k, True) * decay
    gl = a_col[:, C - 1:C, :]
    k_dec = k * jnp.exp(gl - a_col)
    v_new = u - _bmm_nn(w, S)
    o = _bmm_nn(q * e_col, S) + _bmm_nn(attn, v_new)
    s_new = S * jnp.exp(jnp.broadcast_to(gl, (H, D, D))) + _bmm_tn(k_dec, v_new)
    on = o * lax.rsqrt(jnp.mean(o * o, axis=-1, keepdims=True) + EPS) * nw
    return on * (zc * _sigmoid(zc)), s_new


DN_ZCOLS = IN_COLS_PAD - OFF_Z
DN_ZBLK = OFF_Z // DN_ZCOLS


def _dn_heads(a, off):
    return jnp.stack([a[:, off + h * DN_D:off + (h + 1) * DN_D] for h in range(DN_HEADS)])


def _dn_gate_cols(zb, off):
    return jnp.stack([zb[:, off + h:off + h + 1] for h in range(DN_HEADS)])


def _dn_operands(x_ref, zb_ref):
    x, zb = x_ref[...], zb_ref[...]
    return (_dn_heads(x, 0), _dn_heads(x, V_B), _dn_heads(x, 2 * V_B), _dn_heads(zb, 0),
            _dn_gate_cols(zb, V_B), _dn_gate_cols(zb, V_B + DN_HEADS))


def dn_fwd(qkvc, proj, alog, dtb, nw, ag=()):
    T = qkvc.shape[0]
    H, C, D = DN_HEADS, DN_CHUNK, DN_D
    N = T // C

    def body(x_ref, zb_ref, alog_ref, dtb_ref, nw_ref, o_ref, sall_ref, s_ref):
        @pl.when(pl.program_id(0) == 0)
        def _():
            s_ref[...] = jnp.zeros_like(s_ref)

        s_in = s_ref[...]
        sall_ref[0] = s_in
        on, s_new = _dn_chunk(*_dn_operands(x_ref, zb_ref), alog_ref[...], dtb_ref[...], nw_ref[...], s_in)
        for h in range(H):
            o_ref[:, h * D:(h + 1) * D] = on[h]
        s_ref[...] = s_new

    par = pl.BlockSpec((H, 1, 1), lambda n: (0, 0, 0))
    outs, gathered, _ = _pcall(
        body, (qkvc, proj, alog, dtb, nw), name="dn_fwd", grid=(N,),
        in_specs=[pl.BlockSpec((C, QKV_B), lambda n: (n, 0)), pl.BlockSpec((C, DN_ZCOLS), lambda n: (n, DN_ZBLK)),
                  par, par, pl.BlockSpec((1, 1, D), lambda n: (0, 0, 0))],
        out_specs=[pl.BlockSpec((C, V_B), lambda n: (n, 0)), pl.BlockSpec((1, H, D, D), lambda n: (n, 0, 0, 0))],
        out_shape=[jax.ShapeDtypeStruct((T, V_B), F32), jax.ShapeDtypeStruct((N, H, D, D), F32)],
        scratch_shapes=[pltpu.VMEM((H, D, D), F32)], sem=("arbitrary",), ag=ag)
    return outs, gathered


def dn_bwd(qkvc, proj, alog, dtb, nw, sall, dmix, rs=()):
    T = qkvc.shape[0]
    H, C, D = DN_HEADS, DN_CHUNK, DN_D
    N = T // C

    def body(x_ref, zb_ref, alog_ref, dtb_ref, nw_ref, sall_ref, do_ref,
             dx_ref, dzb_ref, dalog_ref, ddtb_ref, dnw_ref, ds_ref):
        @pl.when(pl.program_id(0) == 0)
        def _():
            ds_ref[...] = jnp.zeros_like(ds_ref)
            dalog_ref[...] = jnp.zeros_like(dalog_ref)
            ddtb_ref[...] = jnp.zeros_like(ddtb_ref)
            dnw_ref[...] = jnp.zeros_like(dnw_ref)

        args = (*_dn_operands(x_ref, zb_ref), alog_ref[...], dtb_ref[...], nw_ref[...], sall_ref[0])
        _, vjp = jax.vjp(_dn_chunk, *args)
        dq, dk, dv, dz, db, da, dalog, ddtb, dnw, ds = vjp((_dn_heads(do_ref[...], 0), ds_ref[...]))
        for h in range(H):
            cols = slice(h * D, (h + 1) * D)
            dx_ref[:, cols] = dq[h]
            dx_ref[:, V_B + h * D:V_B + (h + 1) * D] = dk[h]
            dx_ref[:, 2 * V_B + h * D:2 * V_B + (h + 1) * D] = dv[h]
            dzb_ref[:, cols] = dz[h]
        lane = lax.broadcasted_iota(jnp.int32, (C, LANES), 1)
        tail = jnp.zeros((C, LANES), F32)
        for h in range(H):
            tail = tail + jnp.where(lane == h, jnp.broadcast_to(db[h], (C, LANES)), 0.0)
            tail = tail + jnp.where(lane == H + h, jnp.broadcast_to(da[h], (C, LANES)), 0.0)
        dzb_ref[:, V_B:V_B + LANES] = tail
        dzb_ref[:, V_B + LANES:] = jnp.zeros((C, DN_ZCOLS - V_B - LANES), F32)
        dalog_ref[...] += dalog
        ddtb_ref[...] += ddtb
        dnw_ref[...] += dnw
        ds_ref[...] = ds

    par = pl.BlockSpec((H, 1, 1), lambda i: (0, 0, 0))
    nws = pl.BlockSpec((1, 1, D), lambda i: (0, 0, 0))
    outs, _, slots = _pcall(
        body, (qkvc, proj, alog, dtb, nw, sall, dmix), name="dn_bwd", grid=(N,),
        in_specs=[pl.BlockSpec((C, QKV_B), lambda i: (N - 1 - i, 0)),
                  pl.BlockSpec((C, DN_ZCOLS), lambda i: (N - 1 - i, DN_ZBLK)), par, par, nws,
                  pl.BlockSpec((1, H, D, D), lambda i: (N - 1 - i, 0, 0, 0)),
                  pl.BlockSpec((C, V_B), lambda i: (N - 1 - i, 1))],
        out_specs=[pl.BlockSpec((C, QKV_B), lambda i: (N - 1 - i, 0)),
                   pl.BlockSpec((C, DN_ZCOLS), lambda i: (N - 1 - i, 0)), par, par, nws],
        out_shape=[jax.ShapeDtypeStruct((T, QKV_B), F32), jax.ShapeDtypeStruct((T, DN_ZCOLS), F32)]
        + [jax.ShapeDtypeStruct((H, 1, 1), F32)] * 2 + [jax.ShapeDtypeStruct((1, 1, D), F32)],
        scratch_shapes=[pltpu.VMEM((H, D, D), F32)], sem=("arbitrary",), rs=rs)
    return outs, slots


def _conv_taps(buf_ref, w, width, halo, tm):
    acc = None
    for kk, win in _windows(buf_ref, [halo - (width - 1) + kk for kk in range(width)], tm):
        term = w[kk:kk + 1, :] * win
        acc = term if acc is None else acc + term
    return acc


def _windows(ref, offsets, tm):
    for res in range(SUBLANES):
        ks = [k for k, o in enumerate(offsets) if o % SUBLANES == res]
        if not ks:
            continue
        lo = min(offsets[k] for k in ks)
        hi = max(offsets[k] for k in ks)
        shifted = ref[pl.ds(lo, tm + hi - lo), :]
        for k in ks:
            yield k, shifted[offsets[k] - lo:offsets[k] - lo + tm]


def _conv_taps_bwd(dbuf_ref, w, width, tm):
    acc = None
    for kk, win in _windows(dbuf_ref, [width - 1 - kk for kk in range(width)], tm):
        term = w[kk:kk + 1, :] * win
        acc = term if acc is None else acc + term
    return acc


def _conv_dw_acc(dw_ref, dout, buf_ref, width, halo, tm):
    for kk, win in _windows(buf_ref, [halo - (width - 1) + kk for kk in range(width)], tm):
        dw_ref[pl.ds(kk, 1), :] += jnp.sum(dout * win, axis=0, keepdims=True)


DNC_HALO = 8
DNC_COLS = 768


def dnconv_fwd(proj, w):
    T = proj.shape[0]
    tm = min(T, 256)
    hb = tm // DNC_HALO

    def body(x_ref, h_ref, w_ref, o_ref, buf_ref):
        i = pl.program_id(0)
        buf_ref[0:DNC_HALO, :] = jnp.where(i > 0, h_ref[...], 0.0)
        buf_ref[DNC_HALO:, :] = x_ref[...]
        acc = _conv_taps(buf_ref, w_ref[...], DN_CONV, DNC_HALO, tm)
        o_ref[...] = acc * _sigmoid(acc)

    return pl.pallas_call(
        body, name="dnconv_fwd", grid=(T // tm, 2),
        in_specs=[pl.BlockSpec((tm, DNC_COLS), lambda i, c: (i, 1 + c)),
                  pl.BlockSpec((DNC_HALO, DNC_COLS), lambda i, c: (jnp.maximum(i * hb - 1, 0), 1 + c)),
                  pl.BlockSpec((DN_CONV, DNC_COLS), lambda i, c: (0, c))],
        out_specs=pl.BlockSpec((tm, DNC_COLS), lambda i, c: (i, c)),
        out_shape=jax.ShapeDtypeStruct((T, QKV_B), F32),
        scratch_shapes=[pltpu.VMEM((DNC_HALO + tm, DNC_COLS), F32)],
        compiler_params=_cparams(("parallel", "parallel")),
    )(proj, proj, w)


def dnconv_bwd(proj, w, dout):
    T = proj.shape[0]
    tm = min(T, 256)
    nt = T // tm
    hb = tm // DNC_HALO

    def body(x_ref, h_ref, w_ref, do_ref, dx_ref, dw_ref, buf_ref, dbuf_ref):
        r = pl.program_id(1)
        i = nt - 1 - r

        @pl.when(r == 0)
        def _():
            dw_ref[...] = jnp.zeros_like(dw_ref)
            dbuf_ref[tm:, :] = jnp.zeros((DNC_HALO, DNC_COLS), F32)

        buf_ref[0:DNC_HALO, :] = jnp.where(i > 0, h_ref[...], 0.0)
        buf_ref[DNC_HALO:, :] = x_ref[...]
        wv = w_ref[...]
        acc = _conv_taps(buf_ref, wv, DN_CONV, DNC_HALO, tm)
        sg = _sigmoid(acc)
        dacc = do_ref[...] * (sg * (1.0 + acc * (1.0 - sg)))
        dbuf_ref[0:tm, :] = dacc
        dx_ref[...] = _conv_taps_bwd(dbuf_ref, wv, DN_CONV, tm)
        _conv_dw_acc(dw_ref, dacc, buf_ref, DN_CONV, DNC_HALO, tm)
        dbuf_ref[tm:, :] = dacc[0:DNC_HALO, :]

    return pl.pallas_call(
        body, name="dnconv_bwd", grid=(2, nt),
        in_specs=[pl.BlockSpec((tm, DNC_COLS), lambda c, r: (nt - 1 - r, 1 + c)),
                  pl.BlockSpec((DNC_HALO, DNC_COLS), lambda c, r: (jnp.maximum((nt - 1 - r) * hb - 1, 0), 1 + c)),
                  pl.BlockSpec((DN_CONV, DNC_COLS), lambda c, r: (0, c)),
                  pl.BlockSpec((tm, DNC_COLS), lambda c, r: (nt - 1 - r, c))],
        out_specs=[pl.BlockSpec((tm, DNC_COLS), lambda c, r: (nt - 1 - r, c)),
                   pl.BlockSpec((DN_CONV, DNC_COLS), lambda c, r: (0, c))],
        out_shape=[jax.ShapeDtypeStruct((T, QKV_B), F32), jax.ShapeDtypeStruct((DN_CONV, QKV_B), F32)],
        scratch_shapes=[pltpu.VMEM((DNC_HALO + tm, DNC_COLS), F32), pltpu.VMEM((tm + DNC_HALO, DNC_COLS), F32)],
        compiler_params=_cparams(("parallel", "arbitrary")),
    )(proj, proj, w, dout)


CV_HALO = 32


def _cv_post(cv, lnw, lnb):
    mu = jnp.mean(cv, axis=-1, keepdims=True)
    xc = cv - mu
    y = xc * lax.rsqrt(jnp.mean(xc * xc, axis=-1, keepdims=True) + EPS) * lnw + lnb
    return y * _sigmoid(y)


def cv_fwd(ab, w, bdw, lnw, lnb, ag=()):
    T = ab.shape[0]
    D = ab.shape[1] // 2
    tm = min(T, 256)
    hb = tm // CV_HALO

    def body(a_ref, b_ref, ah_ref, bh_ref, w_ref, bdw_ref, lnw_ref, lnb_ref, o_ref, buf_ref):
        i = pl.program_id(0)
        buf_ref[0:CV_HALO, :] = jnp.where(i > 0, ah_ref[...] * _sigmoid(bh_ref[...]), 0.0)
        buf_ref[CV_HALO:, :] = a_ref[...] * _sigmoid(b_ref[...])
        cv = _conv_taps(buf_ref, w_ref[...], CONV_WIDTH, CV_HALO, tm) + bdw_ref[...]
        o_ref[...] = _cv_post(cv, lnw_ref[...], lnb_ref[...])

    halo = lambda c: pl.BlockSpec((CV_HALO, D), lambda i: (jnp.maximum(i * hb - 1, 0), c))
    vec = pl.BlockSpec((1, D), lambda i: (0, 0))
    (out,), gathered, _ = _pcall(
        body, (ab, ab, ab, ab, w, bdw, lnw, lnb), name="cv_fwd", grid=(T // tm,),
        in_specs=[pl.BlockSpec((tm, D), lambda i: (i, 0)), pl.BlockSpec((tm, D), lambda i: (i, 1)),
                  halo(0), halo(1), pl.BlockSpec((CONV_WIDTH, D), lambda i: (0, 0)), vec, vec, vec],
        out_specs=[pl.BlockSpec((tm, D), lambda i: (i, 0))],
        out_shape=[jax.ShapeDtypeStruct((T, D), F32)],
        scratch_shapes=[pltpu.VMEM((CV_HALO + tm, D), F32)], sem=("arbitrary",), ag=ag)
    return out, gathered


def cv_bwd(ab, w, bdw, lnw, lnb, dout, rs=()):
    T = ab.shape[0]
    D = ab.shape[1] // 2
    tm = min(T, 256)
    nt = T // tm
    hb = tm // CV_HALO

    def body(a_ref, b_ref, ah_ref, bh_ref, w_ref, bdw_ref, lnw_ref, lnb_ref, do_ref,
             da_ref, db_ref, dw_ref, dbdw_ref, dlnw_ref, dlnb_ref, buf_ref, dbuf_ref):
        r = pl.program_id(0)
        i = nt - 1 - r

        @pl.when(r == 0)
        def _():
            dw_ref[...] = jnp.zeros_like(dw_ref)
            dbdw_ref[...] = jnp.zeros_like(dbdw_ref)
            dlnw_ref[...] = jnp.zeros_like(dlnw_ref)
            dlnb_ref[...] = jnp.zeros_like(dlnb_ref)
            dbuf_ref[tm:, :] = jnp.zeros((CV_HALO, D), F32)

        a = a_ref[...]
        sb = _sigmoid(b_ref[...])
        buf_ref[0:CV_HALO, :] = jnp.where(i > 0, ah_ref[...] * _sigmoid(bh_ref[...]), 0.0)
        buf_ref[CV_HALO:, :] = a * sb
        wv = w_ref[...]
        cv = _conv_taps(buf_ref, wv, CONV_WIDTH, CV_HALO, tm) + bdw_ref[...]
        _, vjp = jax.vjp(_cv_post, cv, lnw_ref[...], lnb_ref[...])
        dcv, dlnw, dlnb = vjp(do_ref[...])
        dlnw_ref[...] += dlnw
        dlnb_ref[...] += dlnb
        dbdw_ref[...] += jnp.sum(dcv, axis=0, keepdims=True)
        dbuf_ref[0:tm, :] = dcv
        du = _conv_taps_bwd(dbuf_ref, wv, CONV_WIDTH, tm)
        _conv_dw_acc(dw_ref, dcv, buf_ref, CONV_WIDTH, CV_HALO, tm)
        dbuf_ref[tm:, :] = dcv[0:CV_HALO, :]
        da_ref[...] = du * sb
        db_ref[...] = du * a * sb * (1.0 - sb)

    tile = lambda c: pl.BlockSpec((tm, D), lambda r: (nt - 1 - r, c))
    halo = lambda c: pl.BlockSpec((CV_HALO, D), lambda r: (jnp.maximum((nt - 1 - r) * hb - 1, 0), c))
    vec = pl.BlockSpec((1, D), lambda r: (0, 0))
    wsp = pl.BlockSpec((CONV_WIDTH, D), lambda r: (0, 0))
    (da, db, dw, dbdw, dlnw, dlnb), _, slots = _pcall(
        body, (ab, ab, ab, ab, w, bdw, lnw, lnb, dout), name="cv_bwd", grid=(nt,),
        in_specs=[tile(0), tile(1), halo(0), halo(1), wsp, vec, vec, vec, tile(0)],
        out_specs=[tile(0), tile(0), wsp, vec, vec, vec],
        out_shape=[jax.ShapeDtypeStruct((T, D), F32), jax.ShapeDtypeStruct((T, D), F32),
                   jax.ShapeDtypeStruct((CONV_WIDTH, D), F32)] + [jax.ShapeDtypeStruct((1, D), F32)] * 3,
        scratch_shapes=[pltpu.VMEM((CV_HALO + tm, D), F32), pltpu.VMEM((tm + CV_HALO, D), F32)],
        sem=("arbitrary",), rs=rs)
    return (jnp.concatenate([da, db], axis=1), dw, dbdw, dlnw, dlnb), slots


def adamw(w, m, v, slots):
    L, R, C = w.shape
    fits = lambda r, c: N_DEV * r * c * 2 <= ADAM_SLOT_BLOCK
    tiles = [(R, C)] if fits(R, C) else []
    tiles += [(d, C) for d in range(16, R, 16) if R % d == 0 and fits(d, C)]
    tiles += [(R, d) for d in range(LANES, C, LANES) if C % d == 0 and fits(R, d)]
    tr, tc = max(tiles, key=lambda t: t[0] * t[1])
    c1 =1.0 / (1.0 - ADAM_B1 ** ADAM_STEP)
    c2 = 1.0 / (1.0 - ADAM_B2 ** ADAM_STEP)

    def body(w_ref, m_ref, v_ref, *rest):
        s_refs = rest[:L]
        g_ref, d_ref, nm_ref, nv_ref = rest[L:]
        l = pl.program_id(0)
        for k in range(L):
            @pl.when(l == k)
            def _(s_ref=s_refs[k]):
                g = s_ref[0].astype(F32)
                for j in range(1, N_DEV):
                    g = g + s_ref[j].astype(F32)
                nm = ADAM_B1 * m_ref[0] + (1.0 - ADAM_B1) * g
                nv = ADAM_B2 * v_ref[0] + (1.0 - ADAM_B2) * (g * g)
                g_ref[0] = g
                nm_ref[0] = nm
                nv_ref[0] = nv
                d_ref[0] = -ADAM_LR * ((nm * c1) / (jnp.sqrt(nv * c2) + ADAM_EPS) + ADAM_WD * w_ref[0])

    nc = C // tc
    blk = pl.BlockSpec((1, tr, tc), lambda l, i: (l, i // nc, i % nc))
    slot = lambda k: pl.BlockSpec((N_DEV, tr, tc), lambda l, i: (0, jnp.where(l == k, i // nc, 0),
                                                                 jnp.where(l == k, i % nc, 0)))
    return pl.pallas_call(
        body, name="adamw", grid=(L, (R // tr) * nc),
        in_specs=[blk, blk, blk] + [slot(k) for k in range(L)],
        out_specs=[blk, blk, blk, blk],
        out_shape=[jax.ShapeDtypeStruct((L, R, C), F32)] * 4,
        compiler_params=_cparams(("arbitrary", "arbitrary")),
    )(w, m, v, *slots)


def _unshard(g, axis):
    g = jnp.moveaxis(g, 0, axis)
    s = g.shape
    return g.reshape(s[:axis] + (s[axis] * s[axis + 1],) + s[axis + 2:])


def _to_blocks(full, axis):
    s = full.shape
    g = full.reshape(s[:axis] + (N_DEV, s[axis] // N_DEV) + s[axis + 1:])
    return jnp.moveaxis(g, axis, 0)


def _heads(a, h):
    T = a.shape[0]
    return a.reshape(T, h, a.shape[1] // h).transpose(1, 0, 2)


def _unheads(a):
    h, T, d = a.shape
    return a.transpose(1, 0, 2).reshape(T, h * d)


SMALL = (("norm_w", 2), ("dn_conv_w", 2), ("conv_b_pw1", 1), ("conv_w_dw", 2), ("conv_b_dw", 1),
         ("conv_ln_w", 1), ("conv_ln_b", 1), ("conv_b_pw2", 1),
         ("attn_sinks", None), ("dn_a_log", None), ("dn_dt_bias", None), ("dn_norm_w", None), ("final_norm_w", None))

def _pack(parts):
    flat = jnp.concatenate([p.reshape(-1) for p in parts])
    pad = (-flat.shape[0]) % LANES
    return jnp.pad(flat, (0, pad))


def _unpack(flat, shapes):
    out, off = [], 0
    for s in shapes:
        n = int(np.prod(s))
        out.append(flat[off:off + n].reshape(s))
        off += n
    return out


def kernel(x, norm_w, ffn_w_gate, ffn_w_up, ffn_w_down, mix_w_in, dn_conv_w, attn_sinks, dn_a_log, dn_dt_bias, dn_norm_w, mix_w_out, conv_w_pw1, conv_b_pw1, conv_w_dw, conv_b_dw, conv_ln_w, conv_ln_b, conv_w_pw2, conv_b_pw2, final_norm_w, loss_target, m_norm_w, m_ffn_w_gate, m_ffn_w_up, m_ffn_w_down, m_mix_w_in, m_dn_conv_w, m_attn_sinks, m_dn_a_log, m_dn_dt_bias, m_dn_norm_w, m_mix_w_out, m_conv_w_pw1, m_conv_b_pw1, m_conv_w_dw, m_conv_b_dw, m_conv_ln_w, m_conv_ln_b, m_conv_w_pw2, m_conv_b_pw2, m_final_norm_w, v_norm_w, v_ffn_w_gate, v_ffn_w_up, v_ffn_w_down, v_mix_w_in, v_dn_conv_w, v_attn_sinks, v_dn_a_log, v_dn_dt_bias, v_dn_norm_w, v_mix_w_out, v_conv_w_pw1, v_conv_b_pw1, v_conv_w_dw, v_conv_b_dw, v_conv_ln_w, v_conv_ln_b, v_conv_w_pw2, v_conv_b_pw2, v_final_norm_w):
    W = dict(norm_w=norm_w, ffn_w_gate=ffn_w_gate, ffn_w_up=ffn_w_up, ffn_w_down=ffn_w_down, mix_w_in=mix_w_in,
             dn_conv_w=dn_conv_w, attn_sinks=attn_sinks, dn_a_log=dn_a_log, dn_dt_bias=dn_dt_bias,
             dn_norm_w=dn_norm_w, mix_w_out=mix_w_out, conv_w_pw1=conv_w_pw1, conv_b_pw1=conv_b_pw1,
             conv_w_dw=conv_w_dw, conv_b_dw=conv_b_dw, conv_ln_w=conv_ln_w, conv_ln_b=conv_ln_b,
             conv_w_pw2=conv_w_pw2, conv_b_pw2=conv_b_pw2, final_norm_w=final_norm_w)
    M = dict(norm_w=m_norm_w, ffn_w_gate=m_ffn_w_gate, ffn_w_up=m_ffn_w_up, ffn_w_down=m_ffn_w_down,
             mix_w_in=m_mix_w_in, dn_conv_w=m_dn_conv_w, attn_sinks=m_attn_sinks, dn_a_log=m_dn_a_log,
             dn_dt_bias=m_dn_dt_bias, dn_norm_w=m_dn_norm_w, mix_w_out=m_mix_w_out, conv_w_pw1=m_conv_w_pw1,
             conv_b_pw1=m_conv_b_pw1, conv_w_dw=m_conv_w_dw, conv_b_dw=m_conv_b_dw, conv_ln_w=m_conv_ln_w,
             conv_ln_b=m_conv_ln_b, conv_w_pw2=m_conv_w_pw2, conv_b_pw2=m_conv_b_pw2, final_norm_w=m_final_norm_w)
    V = dict(norm_w=v_norm_w, ffn_w_gate=v_ffn_w_gate, ffn_w_up=v_ffn_w_up, ffn_w_down=v_ffn_w_down,
             mix_w_in=v_mix_w_in, dn_conv_w=v_dn_conv_w, attn_sinks=v_attn_sinks, dn_a_log=v_dn_a_log,
             dn_dt_bias=v_dn_dt_bias, dn_norm_w=v_dn_norm_w, mix_w_out=v_mix_w_out, conv_w_pw1=v_conv_w_pw1,
             conv_b_pw1=v_conv_b_pw1, conv_w_dw=v_conv_w_dw, conv_b_dw=v_conv_b_dw, conv_ln_w=v_conv_ln_w,
             conv_ln_b=v_conv_ln_b, conv_w_pw2=v_conv_w_pw2, conv_b_pw2=v_conv_b_pw2, final_norm_w=v_final_norm_w)

    T, D = x.shape[1], x.shape[2]
    xs = x[0]
    F8 = ffn_w_gate.shape[-1]
    n_ffn = DEPTH * 2

    big = ("ffn_w_gate", "ffn_w_up", "ffn_w_down", "mix_w_in", "mix_w_out", "conv_w_pw1", "conv_w_pw2")
    shard3 = {k: W[k].reshape((-1,) + W[k].shape[-2:]) for k in big}
    shard_bf = {k: shard3[k].astype(BF16) for k in big}
    ffn_unit = lambda i: [("ffn_w_gate", i), ("ffn_w_up", i), ("ffn_w_down", i)]
    even_unit = lambda e: [("mix_w_in", e), ("mix_w_out", e)]
    odd_unit = lambda e: [("conv_w_pw1", e), ("conv_w_pw2", e)]
    have = {}

    def ag_jobs(units):
        return [(shard_bf[k], i) for k, i in units]

    def ag_done(units, gathered):
        have.update(zip(units, gathered))

    small_sharded = [(k, ax) for k, ax in SMALL if ax is not None]
    small_pack = _pack([W[k] for k, _ in small_sharded])[None, :]
    first_units = ffn_unit(0)
    gathered, _ = exchange(ag=ag_jobs(first_units) + [(small_pack, None)])
    ag_done(first_units, gathered[:-1])
    small_full = {}
    for (k, ax), parts in zip(small_sharded,
                              zip(*[_unpack(gathered[-1][s, 0], [W[k].shape for k, _ in small_sharded])
                                    for s in range(N_DEV)])):
        small_full[k] = _unshard(jnp.stack(parts), ax)
    nw_full = small_full["norm_w"]

    ffn_w = lambda i: [have[u] for u in ffn_unit(i)]
    w_in_of = lambda e: jnp.pad(_unshard(have[("mix_w_in", e)], 1), ((0, 0), (0, IN_COLS_PAD - IN_COLS)))
    w_out_of = lambda e: have[("mix_w_out", e)].reshape(D, D)
    w_pw1_of = lambda e: _unshard(have[("conv_w_pw1", e)], 1)
    w_pw2_of = lambda e: have[("conv_w_pw2", e)].reshape(D, D)
    fwd_order, needed = [], {}
    for l in range(DEPTH):
        mixer = [("A", l), ("E", l)] if l % 2 == 0 else [("O", l)]
        fwd_order += [("F", 2 * l)] + mixer + [("F", 2 * l + 1)]
        needed[("F", 2 * l)], needed[("F", 2 * l + 1)] = ffn_unit(2 * l), ffn_unit(2 * l + 1)
        needed[mixer[0]] = even_unit(l // 2) if l % 2 == 0 else odd_unit(l // 2)
    queue = [(u, pos) for pos, key in enumerate(fwd_order) for u in needed.get(key, []) if u not in first_units]
    unit_bytes = lambda u: N_DEV * shard_bf[u[0]][u[1]].size * 2
    fwd_carry, at = {}, 0
    for pos, key in enumerate(fwd_order):
        cap = FWD_CARRY_BYTES[key[0]]
        taken, used = [], 0
        while at < len(queue) and (queue[at][1] <= pos + 1 or used + unit_bytes(queue[at][0]) <= cap):
            taken.append(queue[at][0])
            used += unit_bytes(queue[at][0])
            at += 1
        fwd_carry[key] = taken
    zero_in = jnp.zeros((1, IN_COLS_PAD), F32)
    zero_d = jnp.zeros((1, D), F32)
    slope_rows = jnp.asarray(np.repeat(2.0 ** (-8.0 * np.arange(1, ATTN_HEADS + 1) / ATTN_HEADS), ATTN_BLOCK)
                             .astype(np.float32)[:, None])

    saved = []
    h = xs
    w_in, w_out, w_pw1, w_pw2 = {}, {}, {}, {}

    def ffn_forward(h, l, half):
        i = 2 * l + half
        units = fwd_carry.get(("F", i), [])
        h, gathered = ffn_fwd(h, nw_full[l, 2 * half][None], *ffn_w(i), ag=ag_jobs(units))
        ag_done(units, gathered)
        return h

    for l in range(DEPTH):
        e = l // 2
        st = {"x0": h}
        h = ffn_forward(h, l, 0)
        st["x1"] = h
        if l % 2 == 0:
            w_in[e], w_out[e] = w_in_of(e), w_out_of(e)
            proj = rmslin_fwd(h, nw_full[l, 1][None], w_in[e], zero_in)
            st["proj"] = proj
            st["qkvc"] = dnconv_fwd(proj, small_full["dn_conv_w"][e])
            st["sink_rows"] = jnp.repeat(attn_sinks[e], ATTN_BLOCK)[:, None]
            st["alog"] = dn_a_log[e].reshape(DN_HEADS, 1, 1)
            st["dtb"] = dn_dt_bias[e].reshape(DN_HEADS, 1, 1)
            st["dnw"] = dn_norm_w[e].reshape(1, 1, DN_D)
            units = fwd_carry[("A", l)]
            st["att"], gathered = attn_fwd(proj, st["sink_rows"], slope_rows, ag=ag_jobs(units))
            ag_done(units, gathered)
            units = fwd_carry[("E", l)]
            (st["og"], st["sall"]), gathered = dn_fwd(st["qkvc"], proj, st["alog"], st["dtb"], st["dnw"],
                                                      ag=ag_jobs(units))
            ag_done(units, gathered)
            h = lin_fwd(h, [st["att"], st["og"]], w_out[e], zero_d)
        else:
            units = fwd_carry[("O", l)]
            w_pw1[e], w_pw2[e] = w_pw1_of(e), w_pw2_of(e)
            st["ab"] = rmslin_fwd(h, nw_full[l, 1][None], w_pw1[e], small_full["conv_b_pw1"][e][None])
            st["act"], gathered = cv_fwd(st["ab"], small_full["conv_w_dw"][e], small_full["conv_b_dw"][e][None],
                                         small_full["conv_ln_w"][e][None], small_full["conv_ln_b"][e][None],
                                         ag=ag_jobs(units))
            ag_done(units, gathered)
            h = lin_fwd(h, [st["act"]], w_pw2[e], small_full["conv_b_pw2"][e][None])
        st["x2"] = h
        h = ffn_forward(h, l, 1)
        saved.append(st)

    loss_part, dh, dfinal = loss_fwd_bwd(h, final_norm_w[None], loss_target[0])
    loss = lax.psum(loss_part[0, 0], ("x", "y", "c"))

    d_norm = [[None] * 3 for _ in range(DEPTH)]
    d_small = {k: [None, None] for k in ("dn_conv_w", "conv_b_pw1", "conv_w_dw", "conv_b_dw", "conv_ln_w",
                                         "conv_ln_b", "conv_b_pw2", "attn_sinks", "dn_a_log", "dn_dt_bias",
                                         "dn_norm_w")}
    pending, slot = [], {}

    def take_pending(cap=None):
        n, used = 0, 0
        while n < len(pending) and (cap is None or used + pending[n][1].size * 2 <= cap):
            used += pending[n][1].size * 2
            n += 1
        units = pending[:n]
        del pending[:n]
        return [u for u, _ in units], [b for _, b in units]

    def ffn_backward(dh, l, half):
        i = 2 * l + half
        units, blocks = take_pending(BWD_CARRY_BYTES["F"])
        (dh, dg, du, dd, d_norm[l][2 * half]), slots = ffn_bwd(
            st["x2" if half else "x0"], dh, nw_full[l, 2 * half][None], *ffn_w(i), rs=blocks)
        slot.update(zip(units, slots))
        pending.extend(zip(ffn_unit(i), (dg, du, dd)))
        return dh

    for l in reversed(range(DEPTH)):
        e = l // 2
        st = saved[l]
        dh = ffn_backward(dh, l, 1)
        if l % 2 == 0:
            dmix, d_out, _ = lin_bwd([st["att"], st["og"]], dh, w_out[e])
            pending.append((("mix_w_out", e), d_out.reshape(N_DEV, D // N_DEV, D).astype(BF16)))
            units, blocks = take_pending(BWD_CARRY_BYTES["E"])
            (dqkvc, dzba, dalog, ddtb, ddnw), slots = dn_bwd(
                st["qkvc"], st["proj"], st["alog"], st["dtb"], st["dnw"], st["sall"], dmix, rs=blocks)
            slot.update(zip(units, slots))
            units, blocks = take_pending(BWD_CARRY_BYTES["A"])
            (dqa, dkva, dsink), slots = attn_bwd(st["proj"], st["sink_rows"], slope_rows, dmix, rs=blocks)
            slot.update(zip(units, slots))
            dqkv, d_small["dn_conv_w"][e] = dnconv_bwd(st["proj"], small_full["dn_conv_w"][e], dqkvc)
            dproj = jnp.concatenate([dqa, dkva, dqkv, dzba], axis=1)
            dh, d_in, _, d_norm[l][1] = rmslin_bwd(st["x1"], dh, dproj, nw_full[l, 1][None], w_in[e])
            pending.append((("mix_w_in", e), _to_blocks(d_in[:, :IN_COLS], 1).astype(BF16)))
            d_small["attn_sinks"][e] = jnp.sum(dsink.reshape(ATTN_HEADS, ATTN_BLOCK), axis=1)
            d_small["dn_a_log"][e] = dalog.reshape(DN_HEADS)
            d_small["dn_dt_bias"][e] = ddtb.reshape(DN_HEADS)
            d_small["dn_norm_w"][e] = ddnw.reshape(DN_D)
        else:
            dact, d_pw2, d_small["conv_b_pw2"][e] = lin_bwd([st["act"]], dh, w_pw2[e])
            pending.append((("conv_w_pw2", e), d_pw2.reshape(N_DEV, D // N_DEV, D).astype(BF16)))
            units, blocks = take_pending(BWD_CARRY_BYTES["O"])
            (dab, d_small["conv_w_dw"][e], d_small["conv_b_dw"][e], d_small["conv_ln_w"][e],
             d_small["conv_ln_b"][e]), slots = cv_bwd(
                st["ab"], small_full["conv_w_dw"][e], small_full["conv_b_dw"][e][None],
                small_full["conv_ln_w"][e][None], small_full["conv_ln_b"][e][None], dact, rs=blocks)
            slot.update(zip(units, slots))
            dh, d_pw1, d_small["conv_b_pw1"][e], d_norm[l][1] = rmslin_bwd(
                st["x1"], dh, dab, nw_full[l, 1][None], w_pw1[e])
            pending.append((("conv_w_pw1", e), _to_blocks(d_pw1, 1).astype(BF16)))
        dh = ffn_backward(dh, l, 0)
    grad_x = dh[None]

    full_small = {"norm_w": jnp.stack([jnp.concatenate(r, axis=0) for r in d_norm]),
                  "final_norm_w": dfinal[0]}
    for k, pair in d_small.items():
        full_small[k] = jnp.stack([p.reshape(W[k].shape[1:-1] + (-1,)) if SMALL_AXIS[k] is not None
                                   else p for p in pair])
    rows = []
    for s in range(N_DEV):
        parts = [_to_blocks(full_small[k], ax)[s] if ax is not None else full_small[k] for k, ax in SMALL]
        rows.append(_pack(parts))
    send_small = jnp.stack(rows)[:, None, :]
    units, blocks = take_pending()
    _, slots = exchange(rs=blocks + [send_small])
    slot.update(zip(units, slots[:-1]))

    res = {}
    for k in big:
        outs = adamw(shard3[k], M[k].reshape(shard3[k].shape), V[k].reshape(shard3[k].shape),
                     [slot[(k, i)] for i in range(shard3[k].shape[0])])
        res[k] = [o.reshape(W[k].shape) for o in outs]
    pk = lambda d: _pack([d[k] for k, _ in SMALL])[None, None, :]
    outs = adamw(pk(W), pk(M), pk(V), [slots[-1]])
    shapes = [W[k].shape for k, _ in SMALL]
    unp = [_unpack(o[0, 0], shapes) for o in outs]
    for i, (k, _) in enumerate(SMALL):
        res[k] = [u[i] for u in unp]

    order = ("norm_w", "ffn_w_gate", "ffn_w_up", "ffn_w_down", "mix_w_in", "dn_conv_w", "attn_sinks", "dn_a_log",
             "dn_dt_bias", "dn_norm_w", "mix_w_out", "conv_w_pw1", "conv_b_pw1", "conv_w_dw", "conv_b_dw",
             "conv_ln_w", "conv_ln_b", "conv_w_pw2", "conv_b_pw2", "final_norm_w")
    return (loss, grad_x, *[res[k][0] for k in order], *[res[k][1] for k in order],
            *[res[k][2] for k in order], *[res[k][3] for k in order])


SMALL_AXIS = dict(SMALL)
```

```python
import functools

import numpy as np
import jax
import jax.numpy as jnp
from jax import lax
from jax.experimental import pallas as pl
from jax.experimental.pallas import tpu as pltpu

F32 = jnp.float32
BF16 = jnp.bfloat16
EPS = 1e-6
N_DEV = 8
V7X_VMEM_LIMIT = 60 * 2**20
MESH = pl.DeviceIdType.MESH
LANES = 128
SUBLANES = 8

DEPTH = 4
D_MODEL = 1024
ATTN_HEADS, ATTN_KV_HEADS, HEAD_DIM, ATTN_BLOCK = 8, 2, 64, 128
DN_HEADS, DN_D, DN_CHUNK, DN_CONV = 8, 64, 64, 4
CONV_WIDTH = 31
Q_A, KV_A, QKV_B, V_B = 512, 128, 1536, 512
IN_COLS = 2832
IN_COLS_PAD = 3072
OFF_QKVB = Q_A + 2 * KV_A
OFF_Z = OFF_QKVB + QKV_B
OFF_BETA = OFF_Z + V_B
OFF_A = OFF_BETA + DN_HEADS

FWD_CARRY_BYTES = {"F": 12 * 2**20, "A": 6 * 2**20, "E": 18 * 2**20, "O": 12 * 2**20}
BWD_CARRY_BYTES = {"F": 12 * 2**20, "A": 6 * 2**20, "E": 24 * 2**20, "O": 12 * 2**20}

ADAM_SLOT_BLOCK = 3 * 2**19

ADAM_LR, ADAM_B1, ADAM_B2, ADAM_EPS, ADAM_WD, ADAM_STEP = 0.001, 0.9, 0.999, 1e-08, 0.01, 10


def _cparams(sem):
    return pltpu.CompilerParams(dimension_semantics=sem, vmem_limit_bytes=V7X_VMEM_LIMIT)


def _sigmoid(x):
    return 1.0 / (1.0 + jnp.exp(-x))


def _softplus(x):
    return jnp.maximum(x, 0.0) + jnp.log(1.0 + jnp.exp(-jnp.abs(x)))


def _dot(a, b):
    return jnp.dot(a, b, preferred_element_type=F32)


def _dot_nt(a, b):
    return lax.dot_general(a, b, (((1,), (1,)), ((), ())), preferred_element_type=F32)


def _dot_tn(a, b):
    return lax.dot_general(a, b, (((0,), (0,)), ((), ())), preferred_element_type=F32)


def _rms(x, w):
    return x * lax.rsqrt(jnp.mean(x * x, axis=-1, keepdims=True) + EPS) * w


def _rms_bwd(x, w, dxn):
    r = lax.rsqrt(jnp.mean(x * x, axis=-1, keepdims=True) + EPS)
    xh = x * r
    dxh = dxn * w
    dx = r * (dxh - xh * jnp.mean(dxh * xh, axis=-1, keepdims=True))
    return dx, jnp.sum(dxn * xh, axis=0, keepdims=True)


def _position():
    return lax.axis_index("x"), lax.axis_index("y"), lax.axis_index("c")


def _dev_index(px, py, pc):
    return 4 * px + 2 * py + pc


def _rcopy(src, dst, send_sem, recv_sem, to):
    return pltpu.make_async_remote_copy(src_ref=src, dst_ref=dst, send_sem=send_sem, recv_sem=recv_sem,
                                        device_id=to, device_id_type=MESH)


def _ag_start(srcs, outs, send, recv, local):
    x, y, c = _position()
    me = _dev_index(x, y, c)
    chips = [(1 - x, y), (x, 1 - y), (1 - x, 1 - y)]
    for a, (src, out) in enumerate(zip(srcs, outs)):
        pltpu.make_async_copy(src, out.at[me], local.at[a]).start()
        _rcopy(src, out.at[me], send.at[a, 0], recv.at[a, 0], (x, y, 1 - c)).start()
        for j, chip in enumerate(chips):
            _rcopy(src, out.at[me], send.at[a, 1 + j], recv.at[a, 1 + j], (*chip, c)).start()


def _ag_finish(srcs, outs, send, recv, local):
    x, y, c = _position()
    me = _dev_index(x, y, c)
    sibling = (x, y, 1 - c)
    chips = [(1 - x, y), (x, 1 - y), (1 - x, 1 - y)]
    for j, chip in enumerate(chips):
        for a, out in enumerate(outs):
            blk = out.at[_dev_index(*chip, c)]
            _rcopy(blk, blk, send.at[a, 1 + j], recv.at[a, 1 + j], (x, y, c)).wait_recv()
            _rcopy(blk, blk, send.at[a, 4 + j], recv.at[a, 4 + j], sibling).start()
    for a, (src, out) in enumerate(zip(srcs, outs)):
        blk = out.at[_dev_index(x, y, 1 - c)]
        _rcopy(blk, blk, send.at[a, 0], recv.at[a, 0], (x, y, c)).wait_recv()
        for j, chip in enumerate(chips):
            blk = out.at[_dev_index(*chip, 1 - c)]
            _rcopy(blk, blk, send.at[a, 4 + j], recv.at[a, 4 + j], (x, y, c)).wait_recv()
        for k in range(N_DEV - 1):
            _rcopy(out.at[me], out.at[me], send.at[a, k], recv.at[a, k], (x, y, c)).wait_send()
        pltpu.make_async_copy(src, out.at[me], local.at[a]).wait()


def _rs_peer(r):
    x, y, c = _position()
    return x ^ ((r >> 2) & 1), y ^ ((r >> 1) & 1), c ^ (r & 1)


def _rs_start(ins, outs, send, recv, local):
    me = _dev_index(*_position())
    for a, (src, out) in enumerate(zip(ins, outs)):
        pltpu.make_async_copy(src.at[me], out.at[me], local.at[a]).start()
        for r in range(1, N_DEV):
            p = _rs_peer(r)
            _rcopy(src.at[_dev_index(*p)], out.at[me], send.at[a, r - 1], recv.at[a, r - 1], p).start()


def _rs_finish(ins, outs, send, recv, local):
    pos = _position()
    me = _dev_index(*pos)
    for a, (src, out) in enumerate(zip(ins, outs)):
        for r in range(1, N_DEV):
            blk = out.at[_dev_index(*_rs_peer(r))]
            _rcopy(blk, blk, send.at[a, r - 1], recv.at[a, r - 1], pos).wait_recv()
        for r in range(1, N_DEV):
            _rcopy(src.at[me], out.at[me], send.at[a, r - 1], recv.at[a, r - 1], pos).wait_send()
        pltpu.make_async_copy(src.at[me], out.at[me], local.at[a]).wait()


def _pcall(body, args, *, name, grid, in_specs, out_specs, out_shape, sem, scratch_shapes=(), ag=(), rs=()):
    na, nr = len(ag), len(rs)
    if na + nr == 0:
        outs = pl.pallas_call(body, name=name, grid=grid, in_specs=in_specs, out_specs=out_specs,
                              out_shape=out_shape, scratch_shapes=list(scratch_shapes),
                              compiler_params=_cparams(sem))(*args)
        return list(outs), [], []
    n_in, n_out, n_scr = len(in_specs), len(out_specs), len(scratch_shapes)
    ag_idx = [i for _, i in ag]

    def wrapped(*refs):
        cin, refs = refs[:n_in], refs[n_in:]
        ag_in, refs = refs[:na], refs[na:]
        rs_in, refs = refs[:nr], refs[nr:]
        cout, refs = refs[:n_out], refs[n_out:]
        ag_out, refs = refs[:na], refs[na:]
        rs_out, refs = refs[:nr], refs[nr:]
        cscr, sems = refs[:n_scr], refs[n_scr:]
        ag_src = [r if i is None else r.at[i] for r, i in zip(ag_in, ag_idx)]
        ids = [pl.program_id(d) for d in range(len(grid))]
        first = functools.reduce(jnp.logical_and, [i == 0 for i in ids])
        last = functools.reduce(jnp.logical_and, [i == g - 1 for i, g in zip(ids, grid)])

        @pl.when(first)
        def _():
            if na:
                _ag_start(ag_src, ag_out, *sems[:3])
            if nr:
                _rs_start(rs_in, rs_out, *sems[-3:])

        body(*cin, *cout, *cscr)

        @pl.when(last)
        def _():
            if na:
                _ag_finish(ag_src, ag_out, *sems[:3])
            if nr:
                _rs_finish(rs_in, rs_out, *sems[-3:])

    hbm = pl.BlockSpec(memory_space=pl.ANY)
    sem_shapes = []
    for n in (na, nr):
        if n:
            sem_shapes += [pltpu.SemaphoreType.DMA((n, N_DEV - 1)), pltpu.SemaphoreType.DMA((n, N_DEV - 1)),
                           pltpu.SemaphoreType.DMA((n,))]
    outs = pl.pallas_call(
        wrapped, name=name, grid=grid,
        in_specs=list(in_specs) + [hbm] * (na + nr),
        out_specs=list(out_specs) + [hbm] * (na + nr),
        out_shape=list(out_shape)
        + [jax.ShapeDtypeStruct((N_DEV,) + a.shape[-2:], a.dtype) for a, _ in ag]
        + [jax.ShapeDtypeStruct(b.shape, b.dtype) for b in rs],
        scratch_shapes=list(scratch_shapes) + sem_shapes,
        compiler_params=_cparams(sem),
    )(*args, *[a for a, _ in ag], *rs)
    return list(outs[:n_out]), list(outs[n_out:n_out + na]), list(outs[n_out + na:])


def exchange(ag=(), rs=()):
    def body(o_ref):
        o_ref[...] = jnp.zeros_like(o_ref)

    _, gathered, slots = _pcall(body, (), name="exchange", grid=(1,), in_specs=[],
                                out_specs=[pl.BlockSpec((8, LANES), lambda i: (0, 0))],
                                out_shape=[jax.ShapeDtypeStruct((8, LANES), F32)], sem=("arbitrary",), ag=ag, rs=rs)
    return gathered, slots


FFN_PAIR = 2


def _pair_cols(w_ref):
    return jnp.concatenate([w_ref[p] for p in range(FFN_PAIR)], axis=1)


def ffn_fwd(x, nw, wg, wu, wd, ag=()):
    T, D = x.shape
    F = wg.shape[2]
    P = FFN_PAIR
    J = wg.shape[0] // P
    tm = min(T, 1024)

    def body(x_ref, nw_ref, wg_ref, wu_ref, wd_ref, o_ref, xn_ref, acc_ref):
        j = pl.program_id(1)

        @pl.when(j == 0)
        def _():
            xn_ref[...] = _rms(x_ref[...], nw_ref[...]).astype(BF16)
            acc_ref[...] = jnp.zeros_like(acc_ref)

        xn = xn_ref[...]
        g = _dot(xn, _pair_cols(wg_ref))
        u = _dot(xn, _pair_cols(wu_ref))
        h = (g * _sigmoid(g) * u).astype(BF16)
        acc_ref[...] += _dot(h, wd_ref[...].reshape(P * F, D))

        @pl.when(j == J - 1)
        def _():
            o_ref[...] = x_ref[...] + 0.5 * acc_ref[...]

    (out,), gathered, _ = _pcall(
        body, (x, nw, wg, wu, wd), name="ffn_fwd", grid=(T // tm, J),
        in_specs=[pl.BlockSpec((tm, D), lambda t, j: (t, 0)),
                  pl.BlockSpec((1, D), lambda t, j: (0, 0)),
                  pl.BlockSpec((P, D, F), lambda t, j: (j, 0, 0)),
                  pl.BlockSpec((P, D, F), lambda t, j: (j, 0, 0)),
                  pl.BlockSpec((P, F, D), lambda t, j: (j, 0, 0))],
        out_specs=[pl.BlockSpec((tm, D), lambda t, j: (t, 0))],
        out_shape=[jax.ShapeDtypeStruct((T, D), F32)],
        scratch_shapes=[pltpu.VMEM((tm, D), BF16), pltpu.VMEM((tm, D), F32)],
        sem=("arbitrary", "arbitrary"), ag=ag)
    return out, gathered


def ffn_bwd(x, dy, nw, wg, wu, wd, rs=()):
    T, D = x.shape
    F = wg.shape[2]
    P = FFN_PAIR
    J = wg.shape[0] // P
    tm = min(T, 256)
    nt = T // tm

    def body(x_ref, dy_ref, nw_ref, wg_ref, wu_ref, wd_ref,
             dx_ref, dwg_ref, dwu_ref, dwd_ref, dnw_ref,
             xn_ref, dyh_ref, dxn_ref, awg_ref, awu_ref, awd_ref):
        j = pl.program_id(0)
        t = pl.program_id(1)
        rows = pl.ds(pl.multiple_of(t * tm, tm), tm)

        @pl.when(j == 0)
        def _():
            xn_ref[rows, :] = _rms(x_ref[...], nw_ref[...]).astype(BF16)
            dyh_ref[rows, :] = (0.5 * dy_ref[...]).astype(BF16)
            dxn_ref[rows, :] = jnp.zeros((tm, D), F32)

        @pl.when((j == 0) & (t == 0))
        def _():
            dnw_ref[...] = jnp.zeros_like(dnw_ref)

        @pl.when(t == 0)
        def _():
            awg_ref[...] = jnp.zeros_like(awg_ref)
            awu_ref[...] = jnp.zeros_like(awu_ref)
            awd_ref[...] = jnp.zeros_like(awd_ref)

        xn = xn_ref[rows, :]
        dyh = dyh_ref[rows, :]
        wg2, wu2 = _pair_cols(wg_ref), _pair_cols(wu_ref)
        g = _dot(xn, wg2)
        u = _dot(xn, wu2)
        sg = _sigmoid(g)
        s = g * sg
        h = (s * u).astype(BF16)
        dh = _dot_nt(dyh, wd_ref[...].reshape(P * F, D))
        du = (dh * s).astype(BF16)
        dg = (dh * u * (sg * (1.0 + g * (1.0 - sg)))).astype(BF16)
        awd_ref[...] += _dot_tn(h, dyh)
        awg_ref[...] += _dot_tn(xn, dg)
        awu_ref[...] += _dot_tn(xn, du)
        dxn_ref[rows, :] += _dot_nt(dg, wg2) + _dot_nt(du, wu2)

        @pl.when(t == nt - 1)
        def _():
            for p in range(P):
                dwg_ref[p] = awg_ref[:, p * F:(p + 1) * F].astype(BF16)
                dwu_ref[p] = awu_ref[:, p * F:(p + 1) * F].astype(BF16)
            dwd_ref[...] = awd_ref[...].astype(BF16).reshape(P, F, D)

        @pl.when(j == J - 1)
        def _():
            dx, dnw = _rms_bwd(x_ref[...], nw_ref[...], dxn_ref[rows, :])
            dx_ref[...] = dy_ref[...] + dx
            dnw_ref[...] += dnw

    ends = lambda j, t: (jnp.where((j == 0) | (j == J - 1), t, 0), 0)
    last = lambda j, t: (jnp.where(j == J - 1, t, 0), 0)
    outs, _, slots = _pcall(
        body, (x, dy, nw, wg, wu, wd), name="ffn_bwd", grid=(J, nt),
        in_specs=[pl.BlockSpec((tm, D), ends), pl.BlockSpec((tm, D), ends),
                  pl.BlockSpec((1, D), lambda j, t: (0, 0)),
                  pl.BlockSpec((P, D, F), lambda j, t: (j, 0, 0)),
                  pl.BlockSpec((P, D, F), lambda j, t: (j, 0, 0)),
                  pl.BlockSpec((P, F, D), lambda j, t: (j, 0, 0))],
        out_specs=[pl.BlockSpec((tm, D), last),
                   pl.BlockSpec((P, D, F), lambda j, t: (j, 0, 0)),
                   pl.BlockSpec((P, D, F), lambda j, t: (j, 0, 0)),
                   pl.BlockSpec((P, F, D), lambda j, t: (j, 0, 0)),
                   pl.BlockSpec((1, D), lambda j, t: (0, 0))],
        out_shape=[jax.ShapeDtypeStruct((T, D), F32),
                   jax.ShapeDtypeStruct((P * J, D, F), BF16), jax.ShapeDtypeStruct((P * J, D, F), BF16),
                   jax.ShapeDtypeStruct((P * J, F, D), BF16), jax.ShapeDtypeStruct((1, D), F32)],
        scratch_shapes=[pltpu.VMEM((T, D), BF16), pltpu.VMEM((T, D), BF16), pltpu.VMEM((T, D), F32),
                        pltpu.VMEM((D, P * F), F32), pltpu.VMEM((D, P * F), F32), pltpu.VMEM((P * F, D), F32)],
        sem=("arbitrary", "arbitrary"), rs=rs)
    return outs, slots


def rmslin_fwd(x, nw, w, b):
    T, D = x.shape
    N = w.shape[1]
    tm = min(T, 256)

    def body(x_ref, nw_ref, w_ref, b_ref, o_ref):
        xn = _rms(x_ref[...], nw_ref[...]).astype(BF16)
        o_ref[...] = _dot(xn, w_ref[...]) + b_ref[...]

    return pl.pallas_call(
        body, name="rmslin_fwd", grid=(T // tm,),
        in_specs=[pl.BlockSpec((tm, D), lambda t: (t, 0)), pl.BlockSpec((1, D), lambda t: (0, 0)),
                  pl.BlockSpec((D, N), lambda t: (0, 0)), pl.BlockSpec((1, N), lambda t: (0, 0))],
        out_specs=pl.BlockSpec((tm, N), lambda t: (t, 0)),
        out_shape=jax.ShapeDtypeStruct((T, N), F32),
        compiler_params=_cparams(("parallel",)),
    )(x, nw, w, b)


def rmslin_bwd(x, dres, dproj, nw, w):
    T, D = x.shape
    N = w.shape[1]
    nb = 1024
    nc = N // nb
    tm = min(T, 256)
    nt = T // tm

    def body(x_ref, dres_ref, dp_ref, nw_ref, w_ref, dx_ref, dw_ref, db_ref, dnw_ref, xn_ref, dxn_ref):
        c = pl.program_id(0)
        t = pl.program_id(1)
        rows = pl.ds(pl.multiple_of(t * tm, tm), tm)

        @pl.when(c == 0)
        def _():
            xn_ref[rows, :] = _rms(x_ref[...], nw_ref[...]).astype(BF16)
            dxn_ref[rows, :] = jnp.zeros((tm, D), F32)

        @pl.when((c == 0) & (t == 0))
        def _():
            dnw_ref[...] = jnp.zeros_like(dnw_ref)

        @pl.when(t == 0)
        def _():
            dw_ref[...] = jnp.zeros_like(dw_ref)
            db_ref[...] = jnp.zeros_like(db_ref)

        dpf = dp_ref[...]
        dp = dpf.astype(BF16)
        dw_ref[...] += _dot_tn(xn_ref[rows, :], dp)
        db_ref[...] += jnp.sum(dpf, axis=0, keepdims=True)
        dxn_ref[rows, :] += _dot_nt(dp, w_ref[...])

        @pl.when(c == nc - 1)
        def _():
            dx, dnw = _rms_bwd(x_ref[...], nw_ref[...], dxn_ref[rows, :])
            dx_ref[...] = dres_ref[...] + dx
            dnw_ref[...] += dnw

    ends = lambda c, t: (jnp.where((c == 0) | (c == nc - 1), t, 0), 0)
    last = lambda c, t: (jnp.where(c == nc - 1, t, 0), 0)
    return pl.pallas_call(
        body, name="rmslin_bwd", grid=(nc, nt),
        in_specs=[pl.BlockSpec((tm, D), ends), pl.BlockSpec((tm, D), last),
                  pl.BlockSpec((tm, nb), lambda c, t: (t, c)),
                  pl.BlockSpec((1, D), lambda c, t: (0, 0)),
                  pl.BlockSpec((D, nb), lambda c, t: (0, c))],
        out_specs=[pl.BlockSpec((tm, D), last),
                   pl.BlockSpec((D, nb), lambda c, t: (0, c)),
                   pl.BlockSpec((1, nb), lambda c, t: (0, c)),
                   pl.BlockSpec((1, D), lambda c, t: (0, 0))],
        out_shape=[jax.ShapeDtypeStruct((T, D), F32), jax.ShapeDtypeStruct((D, N), F32),
                   jax.ShapeDtypeStruct((1, N), F32), jax.ShapeDtypeStruct((1, D), F32)],
        scratch_shapes=[pltpu.VMEM((T, D), BF16), pltpu.VMEM((T, D), F32)],
        compiler_params=_cparams(("arbitrary", "arbitrary")),
    )(x, dres, dproj, nw, w)


def lin_fwd(res, parts, w, b):
    T = res.shape[0]
    K, N = w.shape
    tm = min(T, 512)
    n = len(parts)
    offs = [sum(p.shape[1] for p in parts[:i]) for i in range(n + 1)]

    def body(res_ref, *refs):
        a_refs, (w_ref, b_ref, o_ref) = refs[:n], refs[n:]
        acc = res_ref[...] + b_ref[...]
        for i, a_ref in enumerate(a_refs):
            acc = acc + _dot(a_ref[...].astype(BF16), w_ref[offs[i]:offs[i + 1], :])
        o_ref[...] = acc

    return pl.pallas_call(
        body, name="lin_fwd", grid=(T // tm,),
        in_specs=[pl.BlockSpec((tm, N), lambda t: (t, 0))]
        + [pl.BlockSpec((tm, p.shape[1]), lambda t: (t, 0)) for p in parts]
        + [pl.BlockSpec((K, N), lambda t: (0, 0)), pl.BlockSpec((1, N), lambda t: (0, 0))],
        out_specs=pl.BlockSpec((tm, N), lambda t: (t, 0)),
        out_shape=jax.ShapeDtypeStruct((T, N), F32),
        compiler_params=_cparams(("parallel",)),
    )(res, *parts, w, b)


def lin_bwd(parts, dy, w):
    T = dy.shape[0]
    K, N = w.shape
    tm = min(T, 256)
    n = len(parts)
    offs = [sum(p.shape[1] for p in parts[:i]) for i in range(n + 1)]

    def body(*refs):
        a_refs, (dy_ref, w_ref, da_ref, dw_ref, db_ref) = refs[:n], refs[n:]

        @pl.when(pl.program_id(0) == 0)
        def _():
            dw_ref[...] = jnp.zeros_like(dw_ref)
            db_ref[...] = jnp.zeros_like(db_ref)

        dyf = dy_ref[...]
        dyb = dyf.astype(BF16)
        da_ref[...] = _dot_nt(dyb, w_ref[...])
        for i, a_ref in enumerate(a_refs):
            dw_ref[offs[i]:offs[i + 1], :] += _dot_tn(a_ref[...].astype(BF16), dyb)
        db_ref[...] += jnp.sum(dyf, axis=0, keepdims=True)

    return pl.pallas_call(
        body, name="lin_bwd", grid=(T // tm,),
        in_specs=[pl.BlockSpec((tm, p.shape[1]), lambda t: (t, 0)) for p in parts]
        + [pl.BlockSpec((tm, N), lambda t: (t, 0)), pl.BlockSpec((K, N), lambda t: (0, 0))],
        out_specs=[pl.BlockSpec((tm, K), lambda t: (t, 0)), pl.BlockSpec((K, N), lambda t: (0, 0)),
                   pl.BlockSpec((1, N), lambda t: (0, 0))],
        out_shape=[jax.ShapeDtypeStruct((T, K), F32), jax.ShapeDtypeStruct((K, N), F32),
                   jax.ShapeDtypeStruct((1, N), F32)],
        compiler_params=_cparams(("arbitrary",)),
    )(*parts, dy, w)


def loss_fwd_bwd(x, fw, target):
    T, D = x.shape
    tm = min(T, 256)

    def body(x_ref, fw_ref, tg_ref, loss_ref, dx_ref, dfw_ref):
        @pl.when(pl.program_id(0) == 0)
        def _():
            loss_ref[...] = jnp.zeros_like(loss_ref)
            dfw_ref[...] = jnp.zeros_like(dfw_ref)

        xv = x_ref[...]
        w = fw_ref[...]
        err = _rms(xv, w) - tg_ref[...]
        row = jnp.sum(err * err, axis=-1, keepdims=True)
        loss_ref[...] += (0.5 / D) * jnp.sum(row, axis=0, keepdims=True)
        dx, dfw = _rms_bwd(xv, w, err * (1.0 / D))
        dx_ref[...] = dx
        dfw_ref[...] += dfw

    return pl.pallas_call(
        body, name="loss_fwd_bwd", grid=(T // tm,),
        in_specs=[pl.BlockSpec((tm, D), lambda t: (t, 0)), pl.BlockSpec((1, D), lambda t: (0, 0)),
                  pl.BlockSpec((tm, D), lambda t: (t, 0))],
        out_specs=[pl.BlockSpec((1, 1), lambda t: (0, 0)), pl.BlockSpec((tm, D), lambda t: (t, 0)),
                   pl.BlockSpec((1, D), lambda t: (0, 0))],
        out_shape=[jax.ShapeDtypeStruct((1, 1), F32), jax.ShapeDtypeStruct((T, D), F32),
                   jax.ShapeDtypeStruct((1, D), F32)],
        compiler_params=_cparams(("arbitrary",)),
    )(x, fw, target)


def _attn_masks(n, rows, blk):
    r = lax.broadcasted_iota(jnp.int32, (rows, 2 * blk), 0)
    jj = lax.broadcasted_iota(jnp.int32, (rows, 2 * blk), 1)
    dist = (r % blk) + blk - jj
    valid = (dist >= 0) & (dist < blk) & ((n > 0) | (jj >= blk))
    return dist.astype(F32), valid


def _attn_block(q, kcat, vcat, sink, slope, dist, valid):
    d = q.shape[-1]
    s = _dot_nt(q.astype(BF16), kcat.astype(BF16)) * (d ** -0.5)
    s = jnp.where(valid, s - slope * dist, -1e30)
    m = lax.stop_gradient(jnp.maximum(jnp.max(s, axis=-1, keepdims=True), sink))
    e = jnp.exp(s - m)
    p = e / (jnp.sum(e, axis=-1, keepdims=True) + jnp.exp(sink - m))
    return _dot(p.astype(BF16), vcat.astype(BF16))


ATTN_G = ATTN_HEADS // ATTN_KV_HEADS
ATTN_QW = ATTN_G * HEAD_DIM
ATTN_KCOL = Q_A // KV_A


def _attn_specs():
    blk = ATTN_BLOCK
    qs = pl.BlockSpec((blk, ATTN_QW), lambda h, n: (n, h))
    prev = lambda c: pl.BlockSpec((blk, KV_A), lambda h, n: (jnp.maximum(n - 1, 0), c))
    cur = lambda c: pl.BlockSpec((blk, KV_A), lambda h, n: (n, c))
    rowp = pl.BlockSpec((ATTN_G * blk, 1), lambda h, n: (h, 0))
    return qs, [prev(ATTN_KCOL), cur(ATTN_KCOL), prev(ATTN_KCOL + 1), cur(ATTN_KCOL + 1)], rowp


def _attn_operands(h, q_ref, kp_ref, kc_ref, vp_ref, vc_ref):
    d = HEAD_DIM
    q = jnp.concatenate([q_ref[:, g * d:(g + 1) * d] for g in range(ATTN_G)], axis=0)
    pick = lambda r: jnp.where(h == 0, r[:, :d], r[:, d:])
    kcat = jnp.concatenate([pick(kp_ref[...]), pick(kc_ref[...])], axis=0)
    vcat = jnp.concatenate([pick(vp_ref[...]), pick(vc_ref[...])], axis=0)
    return q, kcat, vcat


def attn_fwd(proj, sink_rows, slope_rows, ag=()):
    T = proj.shape[0]
    blk, d = ATTN_BLOCK, HEAD_DIM

    def body(q_ref, kp_ref, kc_ref, vp_ref, vc_ref, sink_ref, slope_ref, o_ref):
        h, n = pl.program_id(0), pl.program_id(1)
        dist, valid = _attn_masks(n, ATTN_G * blk, blk)
        q, kcat, vcat = _attn_operands(h, q_ref, kp_ref, kc_ref, vp_ref, vc_ref)
        o = _attn_block(q, kcat, vcat, sink_ref[...], slope_ref[...], dist, valid)
        for g in range(ATTN_G):
            o_ref[:, g * d:(g + 1) * d] = o[g * blk:(g + 1) * blk]

    qs, kv, rowp = _attn_specs()
    (out,), gathered, _ = _pcall(
        body, (proj, proj, proj, proj, proj, sink_rows, slope_rows), name="attn_fwd",
        grid=(ATTN_KV_HEADS, T // blk), in_specs=[qs] + kv + [rowp, rowp], out_specs=[qs],
        out_shape=[jax.ShapeDtypeStruct((T, Q_A), F32)], sem=("arbitrary", "arbitrary"), ag=ag)
    return out, gathered


def attn_bwd(proj, sink_rows, slope_rows, dmix, rs=()):
    T = proj.shape[0]
    blk, d = ATTN_BLOCK, HEAD_DIM

    def body(q_ref, kp_ref, kc_ref, vp_ref, vc_ref, sink_ref, slope_ref, do_ref, dq_ref, dkv_ref, dsink_ref):
        h, n = pl.program_id(0), pl.program_id(1)

        @pl.when((h == 0) & (n == 0))
        def _():
            dkv_ref[...] = jnp.zeros_like(dkv_ref)

        @pl.when(n == 0)
        def _():
            dsink_ref[...] = jnp.zeros_like(dsink_ref)

        dist, valid = _attn_masks(n, ATTN_G * blk, blk)
        q, kcat, vcat = _attn_operands(h, q_ref, kp_ref, kc_ref, vp_ref, vc_ref)
        do = jnp.concatenate([do_ref[:, g * d:(g + 1) * d] for g in range(ATTN_G)], axis=0)
        fn = functools.partial(_attn_block, slope=slope_ref[...], dist=dist, valid=valid)
        _, vjp = jax.vjp(fn, q, kcat, vcat, sink_ref[...])
        dq, dkcat, dvcat, dsink = vjp(do)
        for g in range(ATTN_G):
            dq_ref[:, g * d:(g + 1) * d] = dq[g * blk:(g + 1) * blk]
        dsink_ref[...] += dsink
        lane = lax.broadcasted_iota(jnp.int32, (2 * blk, 2 * KV_A), 1)
        mine = (lane % KV_A) // d == h
        both = jnp.where(mine, jnp.concatenate([dkcat, dkcat, dvcat, dvcat], axis=1), 0.0)

        @pl.when(n == 0)
        def _():
            dkv_ref[0:blk, :] += both[blk:]

        @pl.when(n > 0)
        def _():
            rows = pl.ds(pl.multiple_of((n - 1) * blk, blk), 2 * blk)
            dkv_ref[rows, :] += both

    qs, kv, rowp = _attn_specs()
    outs, _, slots = _pcall(
        body, (proj, proj, proj, proj, proj, sink_rows, slope_rows, dmix), name="attn_bwd",
        grid=(ATTN_KV_HEADS, T // blk), in_specs=[qs] + kv + [rowp, rowp, qs],
        out_specs=[qs, pl.BlockSpec((T, 2 * KV_A), lambda h, n: (0, 0)), rowp],
        out_shape=[jax.ShapeDtypeStruct((T, Q_A), F32), jax.ShapeDtypeStruct((T, 2 * KV_A), F32),
                   jax.ShapeDtypeStruct((ATTN_HEADS * blk, 1), F32)],
        sem=("arbitrary", "arbitrary"), rs=rs)
    return outs, slots


_NN = (((2,), (1,)), ((0,), (0,)))
_NT = (((2,), (2,)), ((0,), (0,)))
_TN = (((1,), (1,)), ((0,), (0,)))


def _bmm(a, b, dims):
    return lax.dot_general(a.astype(BF16), b.astype(BF16), dims, preferred_element_type=F32)


def _split(x, terms):
    out = []
    for _ in range(terms):
        t = x.astype(BF16)
        out.append(t)
        x = x - t.astype(F32)
    return out


def _fine_product(a, b, dims):
    (ah, al), (bh, bl) = _split(a, 2), _split(b, 2)
    dot = lambda x, y: lax.dot_general(x, y, dims, preferred_element_type=F32)
    return dot(ah, bh) + (dot(ah, bl) + dot(al, bh))


def _mask_product(mask, x, dims):
    mb = mask.astype(BF16)
    parts = [lax.dot_general(mb, t, dims, preferred_element_type=F32) for t in _split(x, 3)]
    return parts[0] + (parts[1] + parts[2])


@jax.custom_vjp
def _fine_nt(a, b):
    return _fine_product(a, b, _NT)


_fine_nt.defvjp(lambda a, b: (_fine_product(a, b, _NT), (a, b)),
                lambda res, ct: (_fine_product(ct, res[1], _NN), _fine_product(ct, res[0], _TN)))


@jax.custom_vjp
def _mask_nn(mask, x):
    return _mask_product(mask, x, _NN)


_mask_nn.defvjp(lambda mask, x: (_mask_product(mask, x, _NN), mask),
                lambda mask, ct: (jnp.zeros_like(mask), _mask_product(mask, ct, _TN)))


@jax.custom_vjp
def _unit_lower_inverse(low):
    n = low.shape[-1]
    eye = (lax.broadcasted_iota(jnp.int32, low.shape, 1) == lax.broadcasted_iota(jnp.int32, low.shape, 2)).astype(F32)
    tinv = eye - low
    p = low
    for _ in range(n.bit_length() - 2):
        p = _bmm(p, p, _NN)
        tinv = tinv + _bmm(tinv, p, _NN)
    return tinv


def _unit_lower_inverse_fwd(low):
    tinv = _unit_lower_inverse(low)
    return tinv, tinv


_unit_lower_inverse.defvjp(_unit_lower_inverse_fwd, lambda tinv, ct: (-_bmm(_bmm(tinv, ct, _TN), tinv, _NT),))


def _dn_chunk(qc, kc, vc, zc, braw, araw, alog, dtb, nw, S):
    H, C, D = qc.shape
    row = lax.broadcasted_iota(jnp.int32, (H, C, C), 1)
    col = lax.broadcasted_iota(jnp.int32, (H, C, C), 2)
    causal = row >= col
    strict = row > col
    eye = (row == col).astype(F32)

    q = qc * lax.rsqrt(jnp.sum(qc * qc, axis=-1, keepdims=True) + EPS) * (D ** -0.5)
    k = kc * lax.rsqrt(jnp.sum(kc * kc, axis=-1, keepdims=True) + EPS)
    beta = _sigmoid(braw)
    g = -jnp.exp(alog) * _softplus(araw + dtb)
    a_col = _mask_nn(causal.astype(F32), jnp.broadcast_to(g, (H, C, C)))
    a_row = _mask_nn(jnp.ones((H, C, C), F32), eye * a_col)
    decay = jnp.where(causal, jnp.exp(jnp.where(causal, a_col - a_row, 0.0)), 0.0)
    kb = k * beta
    tinv = _unit_lower_inverse(jnp.where(strict, _fine_nt(kb, k) * decay, 0.0))
    e_col = jnp.exp(a_col)
    u = _bmm(tinv, vc * beta, _NN)
    w = _bmm(tinv, kb * e_col, _NN)
    attn = _fine_nt(q, k) * decay
    gl = a_col[:, C - 1:C, :]
    k_dec = k * jnp.exp(gl - a_col)
    v_new = u - _bmm(w, S, _NN)
    o = _bmm(q * e_col, S, _NN) + _bmm(attn, v_new, _NN)
    s_new = S * jnp.exp(jnp.broadcast_to(gl, (H, D, D))) + _bmm(k_dec, v_new, _TN)
    on = o * lax.rsqrt(jnp.mean(o * o, axis=-1, keepdims=True) + EPS) * nw
    return on * (zc * _sigmoid(zc)), s_new


DN_ZCOLS = IN_COLS_PAD - OFF_Z
DN_ZBLK = OFF_Z // DN_ZCOLS


def _dn_heads(a, off):
    return jnp.stack([a[:, off + h * DN_D:off + (h + 1) * DN_D] for h in range(DN_HEADS)])


def _dn_gate_cols(zb, off):
    return jnp.stack([zb[:, off + h:off + h + 1] for h in range(DN_HEADS)])


def _dn_operands(x_ref, zb_ref):
    x, zb = x_ref[...], zb_ref[...]
    return (_dn_heads(x, 0), _dn_heads(x, V_B), _dn_heads(x, 2 * V_B), _dn_heads(zb, 0),
            _dn_gate_cols(zb, V_B), _dn_gate_cols(zb, V_B + DN_HEADS))


def dn_fwd(qkvc, proj, alog, dtb, nw, ag=()):
    T = qkvc.shape[0]
    H, C, D = DN_HEADS, DN_CHUNK, DN_D
    N = T // C

    def body(x_ref, zb_ref, alog_ref, dtb_ref, nw_ref, o_ref, sall_ref, s_ref):
        @pl.when(pl.program_id(0) == 0)
        def _():
            s_ref[...] = jnp.zeros_like(s_ref)

        s_in = s_ref[...]
        sall_ref[0] = s_in
        on, s_new = _dn_chunk(*_dn_operands(x_ref, zb_ref), alog_ref[...], dtb_ref[...], nw_ref[...], s_in)
        for h in range(H):
            o_ref[:, h * D:(h + 1) * D] = on[h]
        s_ref[...] = s_new

    par = pl.BlockSpec((H, 1, 1), lambda n: (0, 0, 0))
    outs, gathered, _ = _pcall(
        body, (qkvc, proj, alog, dtb, nw), name="dn_fwd", grid=(N,),
        in_specs=[pl.BlockSpec((C, QKV_B), lambda n: (n, 0)), pl.BlockSpec((C, DN_ZCOLS), lambda n: (n, DN_ZBLK)),
                  par, par, pl.BlockSpec((1, 1, D), lambda n: (0, 0, 0))],
        out_specs=[pl.BlockSpec((C, V_B), lambda n: (n, 0)), pl.BlockSpec((1, H, D, D), lambda n: (n, 0, 0, 0))],
        out_shape=[jax.ShapeDtypeStruct((T, V_B), F32), jax.ShapeDtypeStruct((N, H, D, D), F32)],
        scratch_shapes=[pltpu.VMEM((H, D, D), F32)], sem=("arbitrary",), ag=ag)
    return outs, gathered


def dn_bwd(qkvc, proj, alog, dtb, nw, sall, dmix, rs=()):
    T = qkvc.shape[0]
    H, C, D = DN_HEADS, DN_CHUNK, DN_D
    N = T // C

    def body(x_ref, zb_ref, alog_ref, dtb_ref, nw_ref, sall_ref, do_ref,
             dx_ref, dzb_ref, dalog_ref, ddtb_ref, dnw_ref, ds_ref):
        @pl.when(pl.program_id(0) == 0)
        def _():
            ds_ref[...] = jnp.zeros_like(ds_ref)
            dalog_ref[...] = jnp.zeros_like(dalog_ref)
            ddtb_ref[...] = jnp.zeros_like(ddtb_ref)
            dnw_ref[...] = jnp.zeros_like(dnw_ref)

        args = (*_dn_operands(x_ref, zb_ref), alog_ref[...], dtb_ref[...], nw_ref[...], sall_ref[0])
        _, vjp = jax.vjp(_dn_chunk, *args)
        dq, dk, dv, dz, db, da, dalog, ddtb, dnw, ds = vjp((_dn_heads(do_ref[...], 0), ds_ref[...]))
        for h in range(H):
            cols = slice(h * D, (h + 1) * D)
            dx_ref[:, cols] = dq[h]
            dx_ref[:, V_B + h * D:V_B + (h + 1) * D] = dk[h]
            dx_ref[:, 2 * V_B + h * D:2 * V_B + (h + 1) * D] = dv[h]
            dzb_ref[:, cols] = dz[h]
        lane = lax.broadcasted_iota(jnp.int32, (C, LANES), 1)
        tail = jnp.zeros((C, LANES), F32)
        for h in range(H):
            tail = tail + jnp.where(lane == h, jnp.broadcast_to(db[h], (C, LANES)), 0.0)
            tail = tail + jnp.where(lane == H + h, jnp.broadcast_to(da[h], (C, LANES)), 0.0)
        dzb_ref[:, V_B:V_B + LANES] = tail
        dzb_ref[:, V_B + LANES:] = jnp.zeros((C, DN_ZCOLS - V_B - LANES), F32)
        dalog_ref[...] += dalog
        ddtb_ref[...] += ddtb
        dnw_ref[...] += dnw
        ds_ref[...] = ds

    par = pl.BlockSpec((H, 1, 1), lambda i: (0, 0, 0))
    nws = pl.BlockSpec((1, 1, D), lambda i: (0, 0, 0))
    outs, _, slots = _pcall(
        body, (qkvc, proj, alog, dtb, nw, sall, dmix), name="dn_bwd", grid=(N,),
        in_specs=[pl.BlockSpec((C, QKV_B), lambda i: (N - 1 - i, 0)),
                  pl.BlockSpec((C, DN_ZCOLS), lambda i: (N - 1 - i, DN_ZBLK)), par, par, nws,
                  pl.BlockSpec((1, H, D, D), lambda i: (N - 1 - i, 0, 0, 0)),
                  pl.BlockSpec((C, V_B), lambda i: (N - 1 - i, 1))],
        out_specs=[pl.BlockSpec((C, QKV_B), lambda i: (N - 1 - i, 0)),
                   pl.BlockSpec((C, DN_ZCOLS), lambda i: (N - 1 - i, 0)), par, par, nws],
        out_shape=[jax.ShapeDtypeStruct((T, QKV_B), F32), jax.ShapeDtypeStruct((T, DN_ZCOLS), F32)]
        + [jax.ShapeDtypeStruct((H, 1, 1), F32)] * 2 + [jax.ShapeDtypeStruct((1, 1, D), F32)],
        scratch_shapes=[pltpu.VMEM((H, D, D), F32)], sem=("arbitrary",), rs=rs)
    return outs, slots


def _conv_taps(buf_ref, w, width, halo, tm):
    acc = None
    for kk, win in _windows(buf_ref, [halo - (width - 1) + kk for kk in range(width)], tm):
        term = w[kk:kk + 1, :] * win
        acc = term if acc is None else acc + term
    return acc


def _windows(ref, offsets, tm):
    for res in range(SUBLANES):
        ks = [k for k, o in enumerate(offsets) if o % SUBLANES == res]
        if not ks:
            continue
        lo = min(offsets[k] for k in ks)
        hi = max(offsets[k] for k in ks)
        shifted = ref[pl.ds(lo, tm + hi - lo), :]
        for k in ks:
            yield k, shifted[offsets[k] - lo:offsets[k] - lo + tm]


def _conv_taps_bwd(dbuf_ref, w, width, tm):
    acc = None
    for kk, win in _windows(dbuf_ref, [width - 1 - kk for kk in range(width)], tm):
        term = w[kk:kk + 1, :] * win
        acc = term if acc is None else acc + term
    return acc


def _conv_dw_acc(dw_ref, dout, buf_ref, width, halo, tm):
    for kk, win in _windows(buf_ref, [halo - (width - 1) + kk for kk in range(width)], tm):
        dw_ref[pl.ds(kk, 1), :] += jnp.sum(dout * win, axis=0, keepdims=True)


DNC_HALO = 8
DNC_COLS = 768


def dnconv_fwd(proj, w):
    T = proj.shape[0]
    tm = min(T, 256)
    hb = tm // DNC_HALO

    def body(x_ref, h_ref, w_ref, o_ref, buf_ref):
        i = pl.program_id(0)
        buf_ref[0:DNC_HALO, :] = jnp.where(i > 0, h_ref[...], 0.0)
        buf_ref[DNC_HALO:, :] = x_ref[...]
        acc = _conv_taps(buf_ref, w_ref[...], DN_CONV, DNC_HALO, tm)
        o_ref[...] = acc * _sigmoid(acc)

    return pl.pallas_call(
        body, name="dnconv_fwd", grid=(T // tm, 2),
        in_specs=[pl.BlockSpec((tm, DNC_COLS), lambda i, c: (i, 1 + c)),
                  pl.BlockSpec((DNC_HALO, DNC_COLS), lambda i, c: (jnp.maximum(i * hb - 1, 0), 1 + c)),
                  pl.BlockSpec((DN_CONV, DNC_COLS), lambda i, c: (0, c))],
        out_specs=pl.BlockSpec((tm, DNC_COLS), lambda i, c: (i, c)),
        out_shape=jax.ShapeDtypeStruct((T, QKV_B), F32),
        scratch_shapes=[pltpu.VMEM((DNC_HALO + tm, DNC_COLS), F32)],
        compiler_params=_cparams(("parallel", "parallel")),
    )(proj, proj, w)


def dnconv_bwd(proj, w, dout):
    T = proj.shape[0]
    tm = min(T, 256)
    nt = T // tm
    hb = tm // DNC_HALO

    def body(x_ref, h_ref, w_ref, do_ref, dx_ref, dw_ref, buf_ref, dbuf_ref):
        r = pl.program_id(1)
        i = nt - 1 - r

        @pl.when(r == 0)
        def _():
            dw_ref[...] = jnp.zeros_like(dw_ref)
            dbuf_ref[tm:, :] = jnp.zeros((DNC_HALO, DNC_COLS), F32)

        buf_ref[0:DNC_HALO, :] = jnp.where(i > 0, h_ref[...], 0.0)
        buf_ref[DNC_HALO:, :] = x_ref[...]
        wv = w_ref[...]
        acc = _conv_taps(buf_ref, wv, DN_CONV, DNC_HALO, tm)
        sg = _sigmoid(acc)
        dacc = do_ref[...] * (sg * (1.0 + acc * (1.0 - sg)))
        dbuf_ref[0:tm, :] = dacc
        dx_ref[...] = _conv_taps_bwd(dbuf_ref, wv, DN_CONV, tm)
        _conv_dw_acc(dw_ref, dacc, buf_ref, DN_CONV, DNC_HALO, tm)
        dbuf_ref[tm:, :] = dacc[0:DNC_HALO, :]

    return pl.pallas_call(
        body, name="dnconv_bwd", grid=(2, nt),
        in_specs=[pl.BlockSpec((tm, DNC_COLS), lambda c, r: (nt - 1 - r, 1 + c)),
                  pl.BlockSpec((DNC_HALO, DNC_COLS), lambda c, r: (jnp.maximum((nt - 1 - r) * hb - 1, 0), 1 + c)),
                  pl.BlockSpec((DN_CONV, DNC_COLS), lambda c, r: (0, c)),
                  pl.BlockSpec((tm, DNC_COLS), lambda c, r: (nt - 1 - r, c))],
        out_specs=[pl.BlockSpec((tm, DNC_COLS), lambda c, r: (nt - 1 - r, c)),
                   pl.BlockSpec((DN_CONV, DNC_COLS), lambda c, r: (0, c))],
        out_shape=[jax.ShapeDtypeStruct((T, QKV_B), F32), jax.ShapeDtypeStruct((DN_CONV, QKV_B), F32)],
        scratch_shapes=[pltpu.VMEM((DNC_HALO + tm, DNC_COLS), F32), pltpu.VMEM((tm + DNC_HALO, DNC_COLS), F32)],
        compiler_params=_cparams(("parallel", "arbitrary")),
    )(proj, proj, w, dout)


CV_HALO = 32


def _cv_post(cv, lnw, lnb):
    mu = jnp.mean(cv, axis=-1, keepdims=True)
    xc = cv - mu
    y = xc * lax.rsqrt(jnp.mean(xc * xc, axis=-1, keepdims=True) + EPS) * lnw + lnb
    return y * _sigmoid(y)


def cv_fwd(ab, w, bdw, lnw, lnb, ag=()):
    T = ab.shape[0]
    D = ab.shape[1] // 2
    tm = min(T, 256)
    hb = tm // CV_HALO

    def body(a_ref, b_ref, ah_ref, bh_ref, w_ref, bdw_ref, lnw_ref, lnb_ref, o_ref, buf_ref):
        i = pl.program_id(0)
        buf_ref[0:CV_HALO, :] = jnp.where(i > 0, ah_ref[...] * _sigmoid(bh_ref[...]), 0.0)
        buf_ref[CV_HALO:, :] = a_ref[...] * _sigmoid(b_ref[...])
        cv = _conv_taps(buf_ref, w_ref[...], CONV_WIDTH, CV_HALO, tm) + bdw_ref[...]
        o_ref[...] = _cv_post(cv, lnw_ref[...], lnb_ref[...])

    halo = lambda c: pl.BlockSpec((CV_HALO, D), lambda i: (jnp.maximum(i * hb - 1, 0), c))
    vec = pl.BlockSpec((1, D), lambda i: (0, 0))
    (out,), gathered, _ = _pcall(
        body, (ab, ab, ab, ab, w, bdw, lnw, lnb), name="cv_fwd", grid=(T // tm,),
        in_specs=[pl.BlockSpec((tm, D), lambda i: (i, 0)), pl.BlockSpec((tm, D), lambda i: (i, 1)),
                  halo(0), halo(1), pl.BlockSpec((CONV_WIDTH, D), lambda i: (0, 0)), vec, vec, vec],
        out_specs=[pl.BlockSpec((tm, D), lambda i: (i, 0))],
        out_shape=[jax.ShapeDtypeStruct((T, D), F32)],
        scratch_shapes=[pltpu.VMEM((CV_HALO + tm, D), F32)], sem=("arbitrary",), ag=ag)
    return out, gathered


def cv_bwd(ab, w, bdw, lnw, lnb, dout, rs=()):
    T = ab.shape[0]
    D = ab.shape[1] // 2
    tm = min(T, 256)
    nt = T // tm
    hb = tm // CV_HALO

    def body(a_ref, b_ref, ah_ref, bh_ref, w_ref, bdw_ref, lnw_ref, lnb_ref, do_ref,
             da_ref, db_ref, dw_ref, dbdw_ref, dlnw_ref, dlnb_ref, buf_ref, dbuf_ref):
        r = pl.program_id(0)
        i = nt - 1 - r

        @pl.when(r == 0)
        def _():
            dw_ref[...] = jnp.zeros_like(dw_ref)
            dbdw_ref[...] = jnp.zeros_like(dbdw_ref)
            dlnw_ref[...] = jnp.zeros_like(dlnw_ref)
            dlnb_ref[...] = jnp.zeros_like(dlnb_ref)
            dbuf_ref[tm:, :] = jnp.zeros((CV_HALO, D), F32)

        a = a_ref[...]
        sb = _sigmoid(b_ref[...])
        buf_ref[0:CV_HALO, :] = jnp.where(i > 0, ah_ref[...] * _sigmoid(bh_ref[...]), 0.0)
        buf_ref[CV_HALO:, :] = a * sb
        wv = w_ref[...]
        cv = _conv_taps(buf_ref, wv, CONV_WIDTH, CV_HALO, tm) + bdw_ref[...]
        _, vjp = jax.vjp(_cv_post, cv, lnw_ref[...], lnb_ref[...])
        dcv, dlnw, dlnb = vjp(do_ref[...])
        dlnw_ref[...] += dlnw
        dlnb_ref[...] += dlnb
        dbdw_ref[...] += jnp.sum(dcv, axis=0, keepdims=True)
        dbuf_ref[0:tm, :] = dcv
        du = _conv_taps_bwd(dbuf_ref, wv, CONV_WIDTH, tm)
        _conv_dw_acc(dw_ref, dcv, buf_ref, CONV_WIDTH, CV_HALO, tm)
        dbuf_ref[tm:, :] = dcv[0:CV_HALO, :]
        da_ref[...] = du * sb
        db_ref[...] = du * a * sb * (1.0 - sb)

    tile = lambda c: pl.BlockSpec((tm, D), lambda r: (nt - 1 - r, c))
    halo = lambda c: pl.BlockSpec((CV_HALO, D), lambda r: (jnp.maximum((nt - 1 - r) * hb - 1, 0), c))
    vec = pl.BlockSpec((1, D), lambda r: (0, 0))
    wsp = pl.BlockSpec((CONV_WIDTH, D), lambda r: (0, 0))
    (da, db, dw, dbdw, dlnw, dlnb), _, slots = _pcall(
        body, (ab, ab, ab, ab, w, bdw, lnw, lnb, dout), name="cv_bwd", grid=(nt,),
        in_specs=[tile(0), tile(1), halo(0), halo(1), wsp, vec, vec, vec, tile(0)],
        out_specs=[tile(0), tile(0), wsp, vec, vec, vec],
        out_shape=[jax.ShapeDtypeStruct((T, D), F32), jax.ShapeDtypeStruct((T, D), F32),
                   jax.ShapeDtypeStruct((CONV_WIDTH, D), F32)] + [jax.ShapeDtypeStruct((1, D), F32)] * 3,
        scratch_shapes=[pltpu.VMEM((CV_HALO + tm, D), F32), pltpu.VMEM((tm + CV_HALO, D), F32)],
        sem=("arbitrary",), rs=rs)
    return (jnp.concatenate([da, db], axis=1), dw, dbdw, dlnw, dlnb), slots


def adamw(w, m, v, slots, rs=()):
    L, R, C = w.shape
    fits = lambda r, c: N_DEV * r * c * 2 <= ADAM_SLOT_BLOCK
    tiles = [(R, C)] if fits(R, C) else []
    tiles += [(d, C) for d in range(16, R, 16) if R % d == 0 and fits(d, C)]
    tiles += [(R, d) for d in range(LANES, C, LANES) if C % d == 0 and fits(R, d)]
    tr, tc = max(tiles, key=lambda t: t[0] * t[1])
    c1 =1.0 / (1.0 - ADAM_B1 ** ADAM_STEP)
    c2 = 1.0 / (1.0 - ADAM_B2 ** ADAM_STEP)

    def body(w_ref, m_ref, v_ref, *rest):
        s_refs = rest[:L]
        g_ref, d_ref, nm_ref, nv_ref = rest[L:]
        l = pl.program_id(0)
        for k in range(L):
            @pl.when(l == k)
            def _(s_ref=s_refs[k]):
                g = s_ref[0].astype(F32)
                for j in range(1, N_DEV):
                    g = g + s_ref[j].astype(F32)
                nm = ADAM_B1 * m_ref[0] + (1.0 - ADAM_B1) * g
                nv = ADAM_B2 * v_ref[0] + (1.0 - ADAM_B2) * (g * g)
                g_ref[0] = g
                nm_ref[0] = nm
                nv_ref[0] = nv
                d_ref[0] = -ADAM_LR * ((nm * c1) / (jnp.sqrt(nv * c2) + ADAM_EPS) + ADAM_WD * w_ref[0])

    nc = C // tc
    blk = pl.BlockSpec((1, tr, tc), lambda l, i: (l, i // nc, i % nc))
    slot = lambda k: pl.BlockSpec((N_DEV, tr, tc), lambda l, i: (0, jnp.where(l == k, i // nc, 0),
                                                                 jnp.where(l == k, i % nc, 0)))
    outs, _, landed = _pcall(
        body, (w, m, v, *slots), name="adamw", grid=(L, (R // tr) * nc),
        in_specs=[blk, blk, blk] + [slot(k) for k in range(L)],
        out_specs=[blk, blk, blk, blk],
        out_shape=[jax.ShapeDtypeStruct((L, R, C), F32)] * 4,
        sem=("arbitrary", "arbitrary"), rs=rs)
    return outs, landed


def _unshard(g, axis):
    g = jnp.moveaxis(g, 0, axis)
    s = g.shape
    return g.reshape(s[:axis] + (s[axis] * s[axis + 1],) + s[axis + 2:])


def _to_blocks(full, axis):
    s = full.shape
    g = full.reshape(s[:axis] + (N_DEV, s[axis] // N_DEV) + s[axis + 1:])
    return jnp.moveaxis(g, axis, 0)


def _heads(a, h):
    T = a.shape[0]
    return a.reshape(T, h, a.shape[1] // h).transpose(1, 0, 2)


def _unheads(a):
    h, T, d = a.shape
    return a.transpose(1, 0, 2).reshape(T, h * d)


SMALL = (("norm_w", 2), ("dn_conv_w", 2), ("conv_b_pw1", 1), ("conv_w_dw", 2), ("conv_b_dw", 1),
         ("conv_ln_w", 1), ("conv_ln_b", 1), ("conv_b_pw2", 1),
         ("attn_sinks", None), ("dn_a_log", None), ("dn_dt_bias", None), ("dn_norm_w", None), ("final_norm_w", None))

def _pack(parts):
    flat = jnp.concatenate([p.reshape(-1) for p in parts])
    pad = (-flat.shape[0]) % LANES
    return jnp.pad(flat, (0, pad))


def _unpack(flat, shapes):
    out, off = [], 0
    for s in shapes:
        n = int(np.prod(s))
        out.append(flat[off:off + n].reshape(s))
        off += n
    return out


def kernel(x, norm_w, ffn_w_gate, ffn_w_up, ffn_w_down, mix_w_in, dn_conv_w, attn_sinks, dn_a_log, dn_dt_bias, dn_norm_w, mix_w_out, conv_w_pw1, conv_b_pw1, conv_w_dw, conv_b_dw, conv_ln_w, conv_ln_b, conv_w_pw2, conv_b_pw2, final_norm_w, loss_target, m_norm_w, m_ffn_w_gate, m_ffn_w_up, m_ffn_w_down, m_mix_w_in, m_dn_conv_w, m_attn_sinks, m_dn_a_log, m_dn_dt_bias, m_dn_norm_w, m_mix_w_out, m_conv_w_pw1, m_conv_b_pw1, m_conv_w_dw, m_conv_b_dw, m_conv_ln_w, m_conv_ln_b, m_conv_w_pw2, m_conv_b_pw2, m_final_norm_w, v_norm_w, v_ffn_w_gate, v_ffn_w_up, v_ffn_w_down, v_mix_w_in, v_dn_conv_w, v_attn_sinks, v_dn_a_log, v_dn_dt_bias, v_dn_norm_w, v_mix_w_out, v_conv_w_pw1, v_conv_b_pw1, v_conv_w_dw, v_conv_b_dw, v_conv_ln_w, v_conv_ln_b, v_conv_w_pw2, v_conv_b_pw2, v_final_norm_w):
    W = dict(norm_w=norm_w, ffn_w_gate=ffn_w_gate, ffn_w_up=ffn_w_up, ffn_w_down=ffn_w_down, mix_w_in=mix_w_in,
             dn_conv_w=dn_conv_w, attn_sinks=attn_sinks, dn_a_log=dn_a_log, dn_dt_bias=dn_dt_bias,
             dn_norm_w=dn_norm_w, mix_w_out=mix_w_out, conv_w_pw1=conv_w_pw1, conv_b_pw1=conv_b_pw1,
             conv_w_dw=conv_w_dw, conv_b_dw=conv_b_dw, conv_ln_w=conv_ln_w, conv_ln_b=conv_ln_b,
             conv_w_pw2=conv_w_pw2, conv_b_pw2=conv_b_pw2, final_norm_w=final_norm_w)
    M = dict(norm_w=m_norm_w, ffn_w_gate=m_ffn_w_gate, ffn_w_up=m_ffn_w_up, ffn_w_down=m_ffn_w_down,
             mix_w_in=m_mix_w_in, dn_conv_w=m_dn_conv_w, attn_sinks=m_attn_sinks, dn_a_log=m_dn_a_log,
             dn_dt_bias=m_dn_dt_bias, dn_norm_w=m_dn_norm_w, mix_w_out=m_mix_w_out, conv_w_pw1=m_conv_w_pw1,
             conv_b_pw1=m_conv_b_pw1, conv_w_dw=m_conv_w_dw, conv_b_dw=m_conv_b_dw, conv_ln_w=m_conv_ln_w,
             conv_ln_b=m_conv_ln_b, conv_w_pw2=m_conv_w_pw2, conv_b_pw2=m_conv_b_pw2, final_norm_w=m_final_norm_w)
    V = dict(norm_w=v_norm_w, ffn_w_gate=v_ffn_w_gate, ffn_w_up=v_ffn_w_up, ffn_w_down=v_ffn_w_down,
             mix_w_in=v_mix_w_in, dn_conv_w=v_dn_conv_w, attn_sinks=v_attn_sinks, dn_a_log=v_dn_a_log,
             dn_dt_bias=v_dn_dt_bias, dn_norm_w=v_dn_norm_w, mix_w_out=v_mix_w_out, conv_w_pw1=v_conv_w_pw1,
             conv_b_pw1=v_conv_b_pw1, conv_w_dw=v_conv_w_dw, conv_b_dw=v_conv_b_dw, conv_ln_w=v_conv_ln_w,
             conv_ln_b=v_conv_ln_b, conv_w_pw2=v_conv_w_pw2, conv_b_pw2=v_conv_b_pw2, final_norm_w=v_final_norm_w)

    T, D = x.shape[1], x.shape[2]
    xs = x[0]
    F8 = ffn_w_gate.shape[-1]
    n_ffn = DEPTH * 2

    big = ("ffn_w_gate", "ffn_w_up", "ffn_w_down", "mix_w_in", "mix_w_out", "conv_w_pw1", "conv_w_pw2")
    shard3 = {k: W[k].reshape((-1,) + W[k].shape[-2:]) for k in big}
    shard_bf = {k: shard3[k].astype(BF16) for k in big}
    ffn_unit = lambda i: [("ffn_w_gate", i), ("ffn_w_up", i), ("ffn_w_down", i)]
    even_unit = lambda e: [("mix_w_in", e), ("mix_w_out", e)]
    odd_unit = lambda e: [("conv_w_pw1", e), ("conv_w_pw2", e)]
    have = {}

    def ag_jobs(units):
        return [(shard_bf[k], i) for k, i in units]

    def ag_done(units, gathered):
        have.update(zip(units, gathered))

    small_sharded = [(k, ax) for k, ax in SMALL if ax is not None]
    small_pack = _pack([W[k] for k, _ in small_sharded])[None, :]
    first_units = ffn_unit(0)
    gathered, _ = exchange(ag=ag_jobs(first_units) + [(small_pack, None)])
    ag_done(first_units, gathered[:-1])
    small_full = {}
    for (k, ax), parts in zip(small_sharded,
                              zip(*[_unpack(gathered[-1][s, 0], [W[k].shape for k, _ in small_sharded])
                                    for s in range(N_DEV)])):
        small_full[k] = _unshard(jnp.stack(parts), ax)
    nw_full = small_full["norm_w"]

    ffn_w = lambda i: [have[u] for u in ffn_unit(i)]
    w_in_of = lambda e: jnp.pad(_unshard(have[("mix_w_in", e)], 1), ((0, 0), (0, IN_COLS_PAD - IN_COLS)))
    w_out_of = lambda e: have[("mix_w_out", e)].reshape(D, D)
    w_pw1_of = lambda e: _unshard(have[("conv_w_pw1", e)], 1)
    w_pw2_of = lambda e: have[("conv_w_pw2", e)].reshape(D, D)
    fwd_order, needed = [], {}
    for l in range(DEPTH):
        mixer = [("A", l), ("E", l)] if l % 2 == 0 else [("O", l)]
        fwd_order += [("F", 2 * l)] + mixer + [("F", 2 * l + 1)]
        needed[("F", 2 * l)], needed[("F", 2 * l + 1)] = ffn_unit(2 * l), ffn_unit(2 * l + 1)
        needed[mixer[0]] = even_unit(l // 2) if l % 2 == 0 else odd_unit(l // 2)
    queue = [(u, pos) for pos, key in enumerate(fwd_order) for u in needed.get(key, []) if u not in first_units]
    unit_bytes = lambda u: N_DEV * shard_bf[u[0]][u[1]].size * 2
    fwd_carry, at = {}, 0
    for pos, key in enumerate(fwd_order):
        cap = FWD_CARRY_BYTES[key[0]]
        taken, used = [], 0
        while at < len(queue) and (queue[at][1] <= pos + 1 or used + unit_bytes(queue[at][0]) <= cap):
            taken.append(queue[at][0])
            used += unit_bytes(queue[at][0])
            at += 1
        fwd_carry[key] = taken
    zero_in = jnp.zeros((1, IN_COLS_PAD), F32)
    zero_d = jnp.zeros((1, D), F32)
    slope_rows = jnp.asarray(np.repeat(2.0 ** (-8.0 * np.arange(1, ATTN_HEADS + 1) / ATTN_HEADS), ATTN_BLOCK)
                             .astype(np.float32)[:, None])

    saved = []
    h = xs
    w_in, w_out, w_pw1, w_pw2 = {}, {}, {}, {}

    def ffn_forward(h, l, half):
        i = 2 * l + half
        units = fwd_carry.get(("F", i), [])
        h, gathered = ffn_fwd(h, nw_full[l, 2 * half][None], *ffn_w(i), ag=ag_jobs(units))
        ag_done(units, gathered)
        return h

    for l in range(DEPTH):
        e = l // 2
        st = {"x0": h}
        h = ffn_forward(h, l, 0)
        st["x1"] = h
        if l % 2 == 0:
            w_in[e], w_out[e] = w_in_of(e), w_out_of(e)
            proj = rmslin_fwd(h, nw_full[l, 1][None], w_in[e], zero_in)
            st["proj"] = proj
            st["qkvc"] = dnconv_fwd(proj, small_full["dn_conv_w"][e])
            st["sink_rows"] = jnp.repeat(attn_sinks[e], ATTN_BLOCK)[:, None]
            st["alog"] = dn_a_log[e].reshape(DN_HEADS, 1, 1)
            st["dtb"] = dn_dt_bias[e].reshape(DN_HEADS, 1, 1)
            st["dnw"] = dn_norm_w[e].reshape(1, 1, DN_D)
            units = fwd_carry[("A", l)]
            st["att"], gathered = attn_fwd(proj, st["sink_rows"], slope_rows, ag=ag_jobs(units))
            ag_done(units, gathered)
            units = fwd_carry[("E", l)]
            (st["og"], st["sall"]), gathered = dn_fwd(st["qkvc"], proj, st["alog"], st["dtb"], st["dnw"],
                                                      ag=ag_jobs(units))
            ag_done(units, gathered)
            h = lin_fwd(h, [st["att"], st["og"]], w_out[e], zero_d)
        else:
            units = fwd_carry[("O", l)]
            w_pw1[e], w_pw2[e] = w_pw1_of(e), w_pw2_of(e)
            st["ab"] = rmslin_fwd(h, nw_full[l, 1][None], w_pw1[e], small_full["conv_b_pw1"][e][None])
            st["act"], gathered = cv_fwd(st["ab"], small_full["conv_w_dw"][e], small_full["conv_b_dw"][e][None],
                                         small_full["conv_ln_w"][e][None], small_full["conv_ln_b"][e][None],
                                         ag=ag_jobs(units))
            ag_done(units, gathered)
            h = lin_fwd(h, [st["act"]], w_pw2[e], small_full["conv_b_pw2"][e][None])
        st["x2"] = h
        h = ffn_forward(h, l, 1)
        saved.append(st)

    loss_part, dh, dfinal = loss_fwd_bwd(h, final_norm_w[None], loss_target[0])
    loss = lax.psum(loss_part[0, 0], ("x", "y", "c"))

    d_norm = [[None] * 3 for _ in range(DEPTH)]
    d_small = {k: [None, None] for k in ("dn_conv_w", "conv_b_pw1", "conv_w_dw", "conv_b_dw", "conv_ln_w",
                                         "conv_ln_b", "conv_b_pw2", "attn_sinks", "dn_a_log", "dn_dt_bias",
                                         "dn_norm_w")}
    pending, slot = [], {}

    def take_pending(cap=None):
        n, used = 0, 0
        while n < len(pending) and (cap is None or used + pending[n][1].size * 2 <= cap):
            used += pending[n][1].size * 2
            n += 1
        units = pending[:n]
        del pending[:n]
        return [u for u, _ in units], [b for _, b in units]

    def ffn_backward(dh, l, half):
        i = 2 * l + half
        units, blocks = take_pending(BWD_CARRY_BYTES["F"])
        (dh, dg, du, dd, d_norm[l][2 * half]), slots = ffn_bwd(
            st["x2" if half else "x0"], dh, nw_full[l, 2 * half][None], *ffn_w(i), rs=blocks)
        slot.update(zip(units, slots))
        pending.extend(zip(ffn_unit(i), (dg, du, dd)))
        return dh

    for l in reversed(range(DEPTH)):
        e = l // 2
        st = saved[l]
        dh = ffn_backward(dh, l, 1)
        if l % 2 == 0:
            dmix, d_out, _ = lin_bwd([st["att"], st["og"]], dh, w_out[e])
            pending.append((("mix_w_out", e), d_out.reshape(N_DEV, D // N_DEV, D).astype(BF16)))
            units, blocks = take_pending(BWD_CARRY_BYTES["E"])
            (dqkvc, dzba, dalog, ddtb, ddnw), slots = dn_bwd(
                st["qkvc"], st["proj"], st["alog"], st["dtb"], st["dnw"], st["sall"], dmix, rs=blocks)
            slot.update(zip(units, slots))
            units, blocks = take_pending(BWD_CARRY_BYTES["A"])
            (dqa, dkva, dsink), slots = attn_bwd(st["proj"], st["sink_rows"], slope_rows, dmix, rs=blocks)
            slot.update(zip(units, slots))
            dqkv, d_small["dn_conv_w"][e] = dnconv_bwd(st["proj"], small_full["dn_conv_w"][e], dqkvc)
            dproj = jnp.concatenate([dqa, dkva, dqkv, dzba], axis=1)
            dh, d_in, _, d_norm[l][1] = rmslin_bwd(st["x1"], dh, dproj, nw_full[l, 1][None], w_in[e])
            pending.append((("mix_w_in", e), _to_blocks(d_in[:, :IN_COLS], 1).astype(BF16)))
            d_small["attn_sinks"][e] = jnp.sum(dsink.reshape(ATTN_HEADS, ATTN_BLOCK), axis=1)
            d_small["dn_a_log"][e] = dalog.reshape(DN_HEADS)
            d_small["dn_dt_bias"][e] = ddtb.reshape(DN_HEADS)
            d_small["dn_norm_w"][e] = ddnw.reshape(DN_D)
        else:
            dact, d_pw2, d_small["conv_b_pw2"][e] = lin_bwd([st["act"]], dh, w_pw2[e])
            pending.append((("conv_w_pw2", e), d_pw2.reshape(N_DEV, D // N_DEV, D).astype(BF16)))
            units, blocks = take_pending(BWD_CARRY_BYTES["O"])
            (dab, d_small["conv_w_dw"][e], d_small["conv_b_dw"][e], d_small["conv_ln_w"][e],
             d_small["conv_ln_b"][e]), slots = cv_bwd(
                st["ab"], small_full["conv_w_dw"][e], small_full["conv_b_dw"][e][None],
                small_full["conv_ln_w"][e][None], small_full["conv_ln_b"][e][None], dact, rs=blocks)
            slot.update(zip(units, slots))
            dh, d_pw1, d_small["conv_b_pw1"][e], d_norm[l][1] = rmslin_bwd(
                st["x1"], dh, dab, nw_full[l, 1][None], w_pw1[e])
            pending.append((("conv_w_pw1", e), _to_blocks(d_pw1, 1).astype(BF16)))
        dh = ffn_backward(dh, l, 0)
    grad_x = dh[None]

    full_small = {"norm_w": jnp.stack([jnp.concatenate(r, axis=0) for r in d_norm]),
                  "final_norm_w": dfinal[0]}
    for k, pair in d_small.items():
        full_small[k] = jnp.stack([p.reshape(W[k].shape[1:-1] + (-1,)) if SMALL_AXIS[k] is not None
                                   else p for p in pair])
    rows = []
    for s in range(N_DEV):
        parts = [_to_blocks(full_small[k], ax)[s] if ax is not None else full_small[k] for k, ax in SMALL]
        rows.append(_pack(parts))
    send_small = jnp.stack(rows)[:, None, :]
    pending.append((("small", 0), send_small))

    res = {}
    waiting = lambda k: [(u, b) for u, b in pending if u[0] == k]
    adam_order = sorted(big, key=lambda k: len(waiting(k))) + ["small"]
    for n, k in enumerate(adam_order[:-1]):
        riders = next((waiting(kk) for kk in adam_order[n + 1:] if waiting(kk)), [])
        pending[:] = [p for p in pending if p[0] not in [u for u, _ in riders]]
        outs, slots = adamw(shard3[k], M[k].reshape(shard3[k].shape), V[k].reshape(shard3[k].shape),
                            [slot[(k, i)] for i in range(shard3[k].shape[0])], rs=[b for _, b in riders])
        slot.update(zip([u for u, _ in riders], slots))
        res[k] = [o.reshape(W[k].shape) for o in outs]
    pk = lambda d: _pack([d[k] for k, _ in SMALL])[None, None, :]
    outs, _ = adamw(pk(W), pk(M), pk(V), [slot[("small", 0)]])
    shapes = [W[k].shape for k, _ in SMALL]
    unp = [_unpack(o[0, 0], shapes) for o in outs]
    for i, (k, _) in enumerate(SMALL):
        res[k] = [u[i] for u in unp]

    order = ("norm_w", "ffn_w_gate", "ffn_w_up", "ffn_w_down", "mix_w_in", "dn_conv_w", "attn_sinks", "dn_a_log",
             "dn_dt_bias", "dn_norm_w", "mix_w_out", "conv_w_pw1", "conv_b_pw1", "conv_w_dw", "conv_b_dw",
             "conv_ln_w", "conv_ln_b", "conv_w_pw2", "conv_b_pw2", "final_norm_w")
    return (loss, grad_x, *[res[k][0] for k in order], *[res[k][1] for k in order],
            *[res[k][2] for k in order], *[res[k][3] for k in order])


SMALL_AXIS = dict(SMALL)
```

```python
import functools

import numpy as np
import jax
import jax.numpy as jnp
from jax import lax
from jax.experimental import pallas as pl
from jax.experimental.pallas import tpu as pltpu

F32 = jnp.float32
BF16 = jnp.bfloat16
EPS = 1e-6
N_DEV = 8
V7X_VMEM_LIMIT = 60 * 2**20
MESH = pl.DeviceIdType.MESH
LANES = 128
SUBLANES = 8

DEPTH = 4
D_MODEL = 1024
ATTN_HEADS, ATTN_KV_HEADS, HEAD_DIM, ATTN_BLOCK = 8, 2, 64, 128
DN_HEADS, DN_D, DN_CHUNK, DN_CONV = 8, 64, 64, 4
CONV_WIDTH = 31
Q_A, KV_A, QKV_B, V_B = 512, 128, 1536, 512
IN_COLS = 2832
IN_COLS_PAD = 3072
OFF_QKVB = Q_A + 2 * KV_A
OFF_Z = OFF_QKVB + QKV_B
OFF_BETA = OFF_Z + V_B
OFF_A = OFF_BETA + DN_HEADS

FWD_CARRY_BYTES = {"F": 12 * 2**20, "A": 6 * 2**20, "E": 18 * 2**20, "O": 12 * 2**20}
BWD_CARRY_BYTES = {"F": 14 * 2**20, "A": 6 * 2**20, "E": 18 * 2**20, "O": 12 * 2**20}

ADAM_SLOT_BLOCK = 3 * 2**19

ADAM_LR, ADAM_B1, ADAM_B2, ADAM_EPS, ADAM_WD, ADAM_STEP = 0.001, 0.9, 0.999, 1e-08, 0.01, 10


def _cparams(sem):
    return pltpu.CompilerParams(dimension_semantics=sem, vmem_limit_bytes=V7X_VMEM_LIMIT)


def _sigmoid(x):
    return 1.0 / (1.0 + jnp.exp(-x))


def _softplus(x):
    return jnp.maximum(x, 0.0) + jnp.log(1.0 + jnp.exp(-jnp.abs(x)))


def _dot(a, b):
    return jnp.dot(a, b, preferred_element_type=F32)


def _dot_nt(a, b):
    return lax.dot_general(a, b, (((1,), (1,)), ((), ())), preferred_element_type=F32)


def _dot_tn(a, b):
    return lax.dot_general(a, b, (((0,), (0,)), ((), ())), preferred_element_type=F32)


def _rms(x, w):
    return x * lax.rsqrt(jnp.mean(x * x, axis=-1, keepdims=True) + EPS) * w


def _rms_bwd(x, w, dxn):
    r = lax.rsqrt(jnp.mean(x * x, axis=-1, keepdims=True) + EPS)
    xh = x * r
    dxh = dxn * w
    dx = r * (dxh - xh * jnp.mean(dxh * xh, axis=-1, keepdims=True))
    return dx, jnp.sum(dxn * xh, axis=0, keepdims=True)


def _position():
    return lax.axis_index("x"), lax.axis_index("y"), lax.axis_index("c")


def _dev_index(px, py, pc):
    return 4 * px + 2 * py + pc


def _rcopy(src, dst, send_sem, recv_sem, to):
    return pltpu.make_async_remote_copy(src_ref=src, dst_ref=dst, send_sem=send_sem, recv_sem=recv_sem,
                                        device_id=to, device_id_type=MESH)


def _ag_start(srcs, outs, send, recv, local):
    x, y, c = _position()
    me = _dev_index(x, y, c)
    chips = [(1 - x, y), (x, 1 - y), (1 - x, 1 - y)]
    for a, (src, out) in enumerate(zip(srcs, outs)):
        pltpu.make_async_copy(src, out.at[me], local.at[a]).start()
        _rcopy(src, out.at[me], send.at[a, 0], recv.at[a, 0], (x, y, 1 - c)).start()
        for j, chip in enumerate(chips):
            _rcopy(src, out.at[me], send.at[a, 1 + j], recv.at[a, 1 + j], (*chip, c)).start()


def _ag_finish(srcs, outs, send, recv, local):
    x, y, c = _position()
    me = _dev_index(x, y, c)
    sibling = (x, y, 1 - c)
    chips = [(1 - x, y), (x, 1 - y), (1 - x, 1 - y)]
    for j, chip in enumerate(chips):
        for a, out in enumerate(outs):
            blk = out.at[_dev_index(*chip, c)]
            _rcopy(blk, blk, send.at[a, 1 + j], recv.at[a, 1 + j], (x, y, c)).wait_recv()
            _rcopy(blk, blk, send.at[a, 4 + j], recv.at[a, 4 + j], sibling).start()
    for a, (src, out) in enumerate(zip(srcs, outs)):
        blk = out.at[_dev_index(x, y, 1 - c)]
        _rcopy(blk, blk, send.at[a, 0], recv.at[a, 0], (x, y, c)).wait_recv()
        for j, chip in enumerate(chips):
            blk = out.at[_dev_index(*chip, 1 - c)]
            _rcopy(blk, blk, send.at[a, 4 + j], recv.at[a, 4 + j], (x, y, c)).wait_recv()
        for k in range(N_DEV - 1):
            _rcopy(out.at[me], out.at[me], send.at[a, k], recv.at[a, k], (x, y, c)).wait_send()
        pltpu.make_async_copy(src, out.at[me], local.at[a]).wait()


def _rs_peer(r):
    x, y, c = _position()
    return x ^ ((r >> 2) & 1), y ^ ((r >> 1) & 1), c ^ (r & 1)


def _rs_start(ins, outs, send, recv, local):
    me = _dev_index(*_position())
    for a, (src, out) in enumerate(zip(ins, outs)):
        pltpu.make_async_copy(src.at[me], out.at[me], local.at[a]).start()
        for r in range(1, N_DEV):
            p = _rs_peer(r)
            _rcopy(src.at[_dev_index(*p)], out.at[me], send.at[a, r - 1], recv.at[a, r - 1], p).start()


def _rs_finish(ins, outs, send, recv, local):
    pos = _position()
    me = _dev_index(*pos)
    for a, (src, out) in enumerate(zip(ins, outs)):
        for r in range(1, N_DEV):
            blk = out.at[_dev_index(*_rs_peer(r))]
            _rcopy(blk, blk, send.at[a, r - 1], recv.at[a, r - 1], pos).wait_recv()
        for r in range(1, N_DEV):
            _rcopy(src.at[me], out.at[me], send.at[a, r - 1], recv.at[a, r - 1], pos).wait_send()
        pltpu.make_async_copy(src.at[me], out.at[me], local.at[a]).wait()


def _pcall(body, args, *, name, grid, in_specs, out_specs, out_shape, sem, scratch_shapes=(), ag=(), rs=()):
    na, nr = len(ag), len(rs)
    if na + nr == 0:
        outs = pl.pallas_call(body, name=name, grid=grid, in_specs=in_specs, out_specs=out_specs,
                              out_shape=out_shape, scratch_shapes=list(scratch_shapes),
                              compiler_params=_cparams(sem))(*args)
        return list(outs), [], []
    n_in, n_out, n_scr = len(in_specs), len(out_specs), len(scratch_shapes)
    ag_idx = [i for _, i in ag]

    def wrapped(*refs):
        cin, refs = refs[:n_in], refs[n_in:]
        ag_in, refs = refs[:na], refs[na:]
        rs_in, refs = refs[:nr], refs[nr:]
        cout, refs = refs[:n_out], refs[n_out:]
        ag_out, refs = refs[:na], refs[na:]
        rs_out, refs = refs[:nr], refs[nr:]
        cscr, sems = refs[:n_scr], refs[n_scr:]
        ag_src = [r if i is None else r.at[i] for r, i in zip(ag_in, ag_idx)]
        ids = [pl.program_id(d) for d in range(len(grid))]
        first = functools.reduce(jnp.logical_and, [i == 0 for i in ids])
        last = functools.reduce(jnp.logical_and, [i == g - 1 for i, g in zip(ids, grid)])

        @pl.when(first)
        def _():
            if na:
                _ag_start(ag_src, ag_out, *sems[:3])
            if nr:
                _rs_start(rs_in, rs_out, *sems[-3:])

        body(*cin, *cout, *cscr)

        @pl.when(last)
        def _():
            if na:
                _ag_finish(ag_src, ag_out, *sems[:3])
            if nr:
                _rs_finish(rs_in, rs_out, *sems[-3:])

    hbm = pl.BlockSpec(memory_space=pl.ANY)
    sem_shapes = []
    for n in (na, nr):
        if n:
            sem_shapes += [pltpu.SemaphoreType.DMA((n, N_DEV - 1)), pltpu.SemaphoreType.DMA((n, N_DEV - 1)),
                           pltpu.SemaphoreType.DMA((n,))]
    outs = pl.pallas_call(
        wrapped, name=name, grid=grid,
        in_specs=list(in_specs) + [hbm] * (na + nr),
        out_specs=list(out_specs) + [hbm] * (na + nr),
        out_shape=list(out_shape)
        + [jax.ShapeDtypeStruct((N_DEV,) + a.shape[-2:], a.dtype) for a, _ in ag]
        + [jax.ShapeDtypeStruct(b.shape, b.dtype) for b in rs],
        scratch_shapes=list(scratch_shapes) + sem_shapes,
        compiler_params=_cparams(sem),
    )(*args, *[a for a, _ in ag], *rs)
    return list(outs[:n_out]), list(outs[n_out:n_out + na]), list(outs[n_out + na:])


def exchange(ag=(), rs=()):
    def body(o_ref):
        o_ref[...] = jnp.zeros_like(o_ref)

    _, gathered, slots = _pcall(body, (), name="exchange", grid=(1,), in_specs=[],
                                out_specs=[pl.BlockSpec((8, LANES), lambda i: (0, 0))],
                                out_shape=[jax.ShapeDtypeStruct((8, LANES), F32)], sem=("arbitrary",), ag=ag, rs=rs)
    return gathered, slots


FFN_PAIR = 2


def _pair_cols(w_ref):
    return jnp.concatenate([w_ref[p] for p in range(FFN_PAIR)], axis=1)


def ffn_fwd(x, nw, wg, wu, wd, ag=()):
    T, D = x.shape
    F = wg.shape[2]
    P = FFN_PAIR
    J = wg.shape[0] // P
    tm = min(T, 1024)

    def body(x_ref, nw_ref, wg_ref, wu_ref, wd_ref, o_ref, xn_ref, acc_ref):
        j = pl.program_id(1)

        @pl.when(j == 0)
        def _():
            xn_ref[...] = _rms(x_ref[...], nw_ref[...]).astype(BF16)
            acc_ref[...] = jnp.zeros_like(acc_ref)

        xn = xn_ref[...]
        g = _dot(xn, _pair_cols(wg_ref))
        u = _dot(xn, _pair_cols(wu_ref))
        h = (g * _sigmoid(g) * u).astype(BF16)
        acc_ref[...] += _dot(h, wd_ref[...].reshape(P * F, D))

        @pl.when(j == J - 1)
        def _():
            o_ref[...] = x_ref[...] + 0.5 * acc_ref[...]

    (out,), gathered, _ = _pcall(
        body, (x, nw, wg, wu, wd), name="ffn_fwd", grid=(T // tm, J),
        in_specs=[pl.BlockSpec((tm, D), lambda t, j: (t, 0)),
                  pl.BlockSpec((1, D), lambda t, j: (0, 0)),
                  pl.BlockSpec((P, D, F), lambda t, j: (j, 0, 0)),
                  pl.BlockSpec((P, D, F), lambda t, j: (j, 0, 0)),
                  pl.BlockSpec((P, F, D), lambda t, j: (j, 0, 0))],
        out_specs=[pl.BlockSpec((tm, D), lambda t, j: (t, 0))],
        out_shape=[jax.ShapeDtypeStruct((T, D), F32)],
        scratch_shapes=[pltpu.VMEM((tm, D), BF16), pltpu.VMEM((tm, D), F32)],
        sem=("arbitrary", "arbitrary"), ag=ag)
    return out, gathered


def ffn_bwd(x, dy, nw, wg, wu, wd, rs=()):
    T, D = x.shape
    F = wg.shape[2]
    P = FFN_PAIR
    J = wg.shape[0] // P
    tm = min(T, 256)
    nt = T // tm

    def body(x_ref, dy_ref, nw_ref, wg_ref, wu_ref, wd_ref,
             dx_ref, dwg_ref, dwu_ref, dwd_ref, dnw_ref,
             xn_ref, dyh_ref, dxn_ref, awg_ref, awu_ref, awd_ref):
        j = pl.program_id(0)
        t = pl.program_id(1)
        rows = pl.ds(pl.multiple_of(t * tm, tm), tm)

        @pl.when(j == 0)
        def _():
            xn_ref[rows, :] = _rms(x_ref[...], nw_ref[...]).astype(BF16)
            dyh_ref[rows, :] = (0.5 * dy_ref[...]).astype(BF16)
            dxn_ref[rows, :] = jnp.zeros((tm, D), F32)

        @pl.when((j == 0) & (t == 0))
        def _():
            dnw_ref[...] = jnp.zeros_like(dnw_ref)

        @pl.when(t == 0)
        def _():
            awg_ref[...] = jnp.zeros_like(awg_ref)
            awu_ref[...] = jnp.zeros_like(awu_ref)
            awd_ref[...] = jnp.zeros_like(awd_ref)

        xn = xn_ref[rows, :]
        dyh = dyh_ref[rows, :]
        wg2, wu2 = _pair_cols(wg_ref), _pair_cols(wu_ref)
        g = _dot(xn, wg2)
        u = _dot(xn, wu2)
        sg = _sigmoid(g)
        s = g * sg
        h = (s * u).astype(BF16)
        dh = _dot_nt(dyh, wd_ref[...].reshape(P * F, D))
        du = (dh * s).astype(BF16)
        dg = (dh * u * (sg * (1.0 + g * (1.0 - sg)))).astype(BF16)
        awd_ref[...] += _dot_tn(h, dyh)
        awg_ref[...] += _dot_tn(xn, dg)
        awu_ref[...] += _dot_tn(xn, du)
        dxn_ref[rows, :] += _dot_nt(dg, wg2) + _dot_nt(du, wu2)

        @pl.when(t == nt - 1)
        def _():
            for p in range(P):
                dwg_ref[p] = awg_ref[:, p * F:(p + 1) * F].astype(BF16)
                dwu_ref[p] = awu_ref[:, p * F:(p + 1) * F].astype(BF16)
            dwd_ref[...] = awd_ref[...].astype(BF16).reshape(P, F, D)

        @pl.when(j == J - 1)
        def _():
            dx, dnw = _rms_bwd(x_ref[...], nw_ref[...], dxn_ref[rows, :])
            dx_ref[...] = dy_ref[...] + dx
            dnw_ref[...] += dnw

    ends = lambda j, t: (jnp.where((j == 0) | (j == J - 1), t, 0), 0)
    last = lambda j, t: (jnp.where(j == J - 1, t, 0), 0)
    outs, _, slots = _pcall(
        body, (x, dy, nw, wg, wu, wd), name="ffn_bwd", grid=(J, nt),
        in_specs=[pl.BlockSpec((tm, D), ends), pl.BlockSpec((tm, D), ends),
                  pl.BlockSpec((1, D), lambda j, t: (0, 0)),
                  pl.BlockSpec((P, D, F), lambda j, t: (j, 0, 0)),
                  pl.BlockSpec((P, D, F), lambda j, t: (j, 0, 0)),
                  pl.BlockSpec((P, F, D), lambda j, t: (j, 0, 0))],
        out_specs=[pl.BlockSpec((tm, D), last),
                   pl.BlockSpec((P, D, F), lambda j, t: (j, 0, 0)),
                   pl.BlockSpec((P, D, F), lambda j, t: (j, 0, 0)),
                   pl.BlockSpec((P, F, D), lambda j, t: (j, 0, 0)),
                   pl.BlockSpec((1, D), lambda j, t: (0, 0))],
        out_shape=[jax.ShapeDtypeStruct((T, D), F32),
                   jax.ShapeDtypeStruct((P * J, D, F), BF16), jax.ShapeDtypeStruct((P * J, D, F), BF16),
                   jax.ShapeDtypeStruct((P * J, F, D), BF16), jax.ShapeDtypeStruct((1, D), F32)],
        scratch_shapes=[pltpu.VMEM((T, D), BF16), pltpu.VMEM((T, D), BF16), pltpu.VMEM((T, D), F32),
                        pltpu.VMEM((D, P * F), F32), pltpu.VMEM((D, P * F), F32), pltpu.VMEM((P * F, D), F32)],
        sem=("arbitrary", "arbitrary"), rs=rs)
    return outs, slots


def rmslin_fwd(x, nw, w, b):
    T, D = x.shape
    N = w.shape[1]
    tm = min(T, 256)

    def body(x_ref, nw_ref, w_ref, b_ref, o_ref):
        xn = _rms(x_ref[...], nw_ref[...]).astype(BF16)
        o_ref[...] = _dot(xn, w_ref[...]) + b_ref[...]

    return pl.pallas_call(
        body, name="rmslin_fwd", grid=(T // tm,),
        in_specs=[pl.BlockSpec((tm, D), lambda t: (t, 0)), pl.BlockSpec((1, D), lambda t: (0, 0)),
                  pl.BlockSpec((D, N), lambda t: (0, 0)), pl.BlockSpec((1, N), lambda t: (0, 0))],
        out_specs=pl.BlockSpec((tm, N), lambda t: (t, 0)),
        out_shape=jax.ShapeDtypeStruct((T, N), F32),
        compiler_params=_cparams(("parallel",)),
    )(x, nw, w, b)


def rmslin_bwd(x, dres, dproj, nw, w):
    T, D = x.shape
    N = w.shape[1]
    nb = 1024
    nc = N // nb
    tm = min(T, 256)
    nt = T // tm

    def body(x_ref, dres_ref, dp_ref, nw_ref, w_ref, dx_ref, dw_ref, db_ref, dnw_ref, xn_ref, dxn_ref):
        c = pl.program_id(0)
        t = pl.program_id(1)
        rows = pl.ds(pl.multiple_of(t * tm, tm), tm)

        @pl.when(c == 0)
        def _():
            xn_ref[rows, :] = _rms(x_ref[...], nw_ref[...]).astype(BF16)
            dxn_ref[rows, :] = jnp.zeros((tm, D), F32)

        @pl.when((c == 0) & (t == 0))
        def _():
            dnw_ref[...] = jnp.zeros_like(dnw_ref)

        @pl.when(t == 0)
        def _():
            dw_ref[...] = jnp.zeros_like(dw_ref)
            db_ref[...] = jnp.zeros_like(db_ref)

        dpf = dp_ref[...]
        dp = dpf.astype(BF16)
        dw_ref[...] += _dot_tn(xn_ref[rows, :], dp)
        db_ref[...] += jnp.sum(dpf, axis=0, keepdims=True)
        dxn_ref[rows, :] += _dot_nt(dp, w_ref[...])

        @pl.when(c == nc - 1)
        def _():
            dx, dnw = _rms_bwd(x_ref[...], nw_ref[...], dxn_ref[rows, :])
            dx_ref[...] = dres_ref[...] + dx
            dnw_ref[...] += dnw

    ends = lambda c, t: (jnp.where((c == 0) | (c == nc - 1), t, 0), 0)
    last = lambda c, t: (jnp.where(c == nc - 1, t, 0), 0)
    return pl.pallas_call(
        body, name="rmslin_bwd", grid=(nc, nt),
        in_specs=[pl.BlockSpec((tm, D), ends), pl.BlockSpec((tm, D), last),
                  pl.BlockSpec((tm, nb), lambda c, t: (t, c)),
                  pl.BlockSpec((1, D), lambda c, t: (0, 0)),
                  pl.BlockSpec((D, nb), lambda c, t: (0, c))],
        out_specs=[pl.BlockSpec((tm, D), last),
                   pl.BlockSpec((D, nb), lambda c, t: (0, c)),
                   pl.BlockSpec((1, nb), lambda c, t: (0, c)),
                   pl.BlockSpec((1, D), lambda c, t: (0, 0))],
        out_shape=[jax.ShapeDtypeStruct((T, D), F32), jax.ShapeDtypeStruct((D, N), F32),
                   jax.ShapeDtypeStruct((1, N), F32), jax.ShapeDtypeStruct((1, D), F32)],
        scratch_shapes=[pltpu.VMEM((T, D), BF16), pltpu.VMEM((T, D), F32)],
        compiler_params=_cparams(("arbitrary", "arbitrary")),
    )(x, dres, dproj, nw, w)


def lin_fwd(res, parts, w, b):
    T = res.shape[0]
    K, N = w.shape
    tm = min(T, 512)
    n = len(parts)
    offs = [sum(p.shape[1] for p in parts[:i]) for i in range(n + 1)]

    def body(res_ref, *refs):
        a_refs, (w_ref, b_ref, o_ref) = refs[:n], refs[n:]
        acc = res_ref[...] + b_ref[...]
        for i, a_ref in enumerate(a_refs):
            acc = acc + _dot(a_ref[...].astype(BF16), w_ref[offs[i]:offs[i + 1], :])
        o_ref[...] = acc

    return pl.pallas_call(
        body, name="lin_fwd", grid=(T // tm,),
        in_specs=[pl.BlockSpec((tm, N), lambda t: (t, 0))]
        + [pl.BlockSpec((tm, p.shape[1]), lambda t: (t, 0)) for p in parts]
        + [pl.BlockSpec((K, N), lambda t: (0, 0)), pl.BlockSpec((1, N), lambda t: (0, 0))],
        out_specs=pl.BlockSpec((tm, N), lambda t: (t, 0)),
        out_shape=jax.ShapeDtypeStruct((T, N), F32),
        compiler_params=_cparams(("parallel",)),
    )(res, *parts, w, b)


def lin_bwd(parts, dy, w):
    T = dy.shape[0]
    K, N = w.shape
    tm = min(T, 256)
    n = len(parts)
    offs = [sum(p.shape[1] for p in parts[:i]) for i in range(n + 1)]

    def body(*refs):
        a_refs, (dy_ref, w_ref, da_ref, dw_ref, db_ref) = refs[:n], refs[n:]

        @pl.when(pl.program_id(0) == 0)
        def _():
            dw_ref[...] = jnp.zeros_like(dw_ref)
            db_ref[...] = jnp.zeros_like(db_ref)

        dyf = dy_ref[...]
        dyb = dyf.astype(BF16)
        da_ref[...] = _dot_nt(dyb, w_ref[...])
        for i, a_ref in enumerate(a_refs):
            dw_ref[offs[i]:offs[i + 1], :] += _dot_tn(a_ref[...].astype(BF16), dyb)
        db_ref[...] += jnp.sum(dyf, axis=0, keepdims=True)

    return pl.pallas_call(
        body, name="lin_bwd", grid=(T // tm,),
        in_specs=[pl.BlockSpec((tm, p.shape[1]), lambda t: (t, 0)) for p in parts]
        + [pl.BlockSpec((tm, N), lambda t: (t, 0)), pl.BlockSpec((K, N), lambda t: (0, 0))],
        out_specs=[pl.BlockSpec((tm, K), lambda t: (t, 0)), pl.BlockSpec((K, N), lambda t: (0, 0)),
                   pl.BlockSpec((1, N), lambda t: (0, 0))],
        out_shape=[jax.ShapeDtypeStruct((T, K), F32), jax.ShapeDtypeStruct((K, N), F32),
                   jax.ShapeDtypeStruct((1, N), F32)],
        compiler_params=_cparams(("arbitrary",)),
    )(*parts, dy, w)


def loss_fwd_bwd(x, fw, target):
    T, D = x.shape
    tm = min(T, 256)

    def body(x_ref, fw_ref, tg_ref, loss_ref, dx_ref, dfw_ref):
        @pl.when(pl.program_id(0) == 0)
        def _():
            loss_ref[...] = jnp.zeros_like(loss_ref)
            dfw_ref[...] = jnp.zeros_like(dfw_ref)

        xv = x_ref[...]
        w = fw_ref[...]
        err = _rms(xv, w) - tg_ref[...]
        row = jnp.sum(err * err, axis=-1, keepdims=True)
        loss_ref[...] += (0.5 / D) * jnp.sum(row, axis=0, keepdims=True)
        dx, dfw = _rms_bwd(xv, w, err * (1.0 / D))
        dx_ref[...] = dx
        dfw_ref[...] += dfw

    return pl.pallas_call(
        body, name="loss_fwd_bwd", grid=(T // tm,),
        in_specs=[pl.BlockSpec((tm, D), lambda t: (t, 0)), pl.BlockSpec((1, D), lambda t: (0, 0)),
                  pl.BlockSpec((tm, D), lambda t: (t, 0))],
        out_specs=[pl.BlockSpec((1, 1), lambda t: (0, 0)), pl.BlockSpec((tm, D), lambda t: (t, 0)),
                   pl.BlockSpec((1, D), lambda t: (0, 0))],
        out_shape=[jax.ShapeDtypeStruct((1, 1), F32), jax.ShapeDtypeStruct((T, D), F32),
                   jax.ShapeDtypeStruct((1, D), F32)],
        compiler_params=_cparams(("arbitrary",)),
    )(x, fw, target)


def _attn_masks(n, rows, blk):
    r = lax.broadcasted_iota(jnp.int32, (rows, 2 * blk), 0)
    jj = lax.broadcasted_iota(jnp.int32, (rows, 2 * blk), 1)
    dist = (r % blk) + blk - jj
    valid = (dist >= 0) & (dist < blk) & ((n > 0) | (jj >= blk))
    return dist.astype(F32), valid


def _attn_block(q, kcat, vcat, sink, slope, dist, valid):
    d = q.shape[-1]
    s = _dot_nt(q.astype(BF16), kcat.astype(BF16)) * (d ** -0.5)
    s = jnp.where(valid, s - slope * dist, -1e30)
    m = lax.stop_gradient(jnp.maximum(jnp.max(s, axis=-1, keepdims=True), sink))
    e = jnp.exp(s - m)
    p = e / (jnp.sum(e, axis=-1, keepdims=True) + jnp.exp(sink - m))
    return _dot(p.astype(BF16), vcat.astype(BF16))


ATTN_G = ATTN_HEADS // ATTN_KV_HEADS
ATTN_QW = ATTN_G * HEAD_DIM
ATTN_KCOL = Q_A // KV_A


def _attn_specs():
    blk = ATTN_BLOCK
    qs = pl.BlockSpec((blk, ATTN_QW), lambda h, n: (n, h))
    prev = lambda c: pl.BlockSpec((blk, KV_A), lambda h, n: (jnp.maximum(n - 1, 0), c))
    cur = lambda c: pl.BlockSpec((blk, KV_A), lambda h, n: (n, c))
    rowp = pl.BlockSpec((ATTN_G * blk, 1), lambda h, n: (h, 0))
    return qs, [prev(ATTN_KCOL), cur(ATTN_KCOL), prev(ATTN_KCOL + 1), cur(ATTN_KCOL + 1)], rowp


def _attn_operands(h, q_ref, kp_ref, kc_ref, vp_ref, vc_ref):
    d = HEAD_DIM
    q = jnp.concatenate([q_ref[:, g * d:(g + 1) * d] for g in range(ATTN_G)], axis=0)
    pick = lambda r: jnp.where(h == 0, r[:, :d], r[:, d:])
    kcat = jnp.concatenate([pick(kp_ref[...]), pick(kc_ref[...])], axis=0)
    vcat = jnp.concatenate([pick(vp_ref[...]), pick(vc_ref[...])], axis=0)
    return q, kcat, vcat


def attn_fwd(proj, sink_rows, slope_rows, ag=()):
    T = proj.shape[0]
    blk, d = ATTN_BLOCK, HEAD_DIM

    def body(q_ref, kp_ref, kc_ref, vp_ref, vc_ref, sink_ref, slope_ref, o_ref):
        h, n = pl.program_id(0), pl.program_id(1)
        dist, valid = _attn_masks(n, ATTN_G * blk, blk)
        q, kcat, vcat = _attn_operands(h, q_ref, kp_ref, kc_ref, vp_ref, vc_ref)
        o = _attn_block(q, kcat, vcat, sink_ref[...], slope_ref[...], dist, valid)
        for g in range(ATTN_G):
            o_ref[:, g * d:(g + 1) * d] = o[g * blk:(g + 1) * blk]

    qs, kv, rowp = _attn_specs()
    (out,), gathered, _ = _pcall(
        body, (proj, proj, proj, proj, proj, sink_rows, slope_rows), name="attn_fwd",
        grid=(ATTN_KV_HEADS, T // blk), in_specs=[qs] + kv + [rowp, rowp], out_specs=[qs],
        out_shape=[jax.ShapeDtypeStruct((T, Q_A), F32)], sem=("arbitrary", "arbitrary"), ag=ag)
    return out, gathered


def attn_bwd(proj, sink_rows, slope_rows, dmix, rs=()):
    T = proj.shape[0]
    blk, d = ATTN_BLOCK, HEAD_DIM

    def body(q_ref, kp_ref, kc_ref, vp_ref, vc_ref, sink_ref, slope_ref, do_ref, dq_ref, dkv_ref, dsink_ref):
        h, n = pl.program_id(0), pl.program_id(1)

        @pl.when((h == 0) & (n == 0))
        def _():
            dkv_ref[...] = jnp.zeros_like(dkv_ref)

        @pl.when(n == 0)
        def _():
            dsink_ref[...] = jnp.zeros_like(dsink_ref)

        dist, valid = _attn_masks(n, ATTN_G * blk, blk)
        q, kcat, vcat = _attn_operands(h, q_ref, kp_ref, kc_ref, vp_ref, vc_ref)
        do = jnp.concatenate([do_ref[:, g * d:(g + 1) * d] for g in range(ATTN_G)], axis=0)
        fn = functools.partial(_attn_block, slope=slope_ref[...], dist=dist, valid=valid)
        _, vjp = jax.vjp(fn, q, kcat, vcat, sink_ref[...])
        dq, dkcat, dvcat, dsink = vjp(do)
        for g in range(ATTN_G):
            dq_ref[:, g * d:(g + 1) * d] = dq[g * blk:(g + 1) * blk]
        dsink_ref[...] += dsink
        lane = lax.broadcasted_iota(jnp.int32, (2 * blk, 2 * KV_A), 1)
        mine = (lane % KV_A) // d == h
        both = jnp.where(mine, jnp.concatenate([dkcat, dkcat, dvcat, dvcat], axis=1), 0.0)

        @pl.when(n == 0)
        def _():
            dkv_ref[0:blk, :] += both[blk:]

        @pl.when(n > 0)
        def _():
            rows = pl.ds(pl.multiple_of((n - 1) * blk, blk), 2 * blk)
            dkv_ref[rows, :] += both

    qs, kv, rowp = _attn_specs()
    outs, _, slots = _pcall(
        body, (proj, proj, proj, proj, proj, sink_rows, slope_rows, dmix), name="attn_bwd",
        grid=(ATTN_KV_HEADS, T // blk), in_specs=[qs] + kv + [rowp, rowp, qs],
        out_specs=[qs, pl.BlockSpec((T, 2 * KV_A), lambda h, n: (0, 0)), rowp],
        out_shape=[jax.ShapeDtypeStruct((T, Q_A), F32), jax.ShapeDtypeStruct((T, 2 * KV_A), F32),
                   jax.ShapeDtypeStruct((ATTN_HEADS * blk, 1), F32)],
        sem=("arbitrary", "arbitrary"), rs=rs)
    return outs, slots


_NN = (((2,), (1,)), ((0,), (0,)))
_NT = (((2,), (2,)), ((0,), (0,)))
_TN = (((1,), (1,)), ((0,), (0,)))


def _bmm(a, b, dims):
    return lax.dot_general(a.astype(BF16), b.astype(BF16), dims, preferred_element_type=F32)


def _split(x, terms):
    out = []
    for _ in range(terms):
        t = x.astype(BF16)
        out.append(t)
        x = x - t.astype(F32)
    return out


def _fine_product(a, b, dims):
    (ah, al), (bh, bl) = _split(a, 2), _split(b, 2)
    dot = lambda x, y: lax.dot_general(x, y, dims, preferred_element_type=F32)
    return dot(ah, bh) + (dot(ah, bl) + dot(al, bh))


def _mask_product(mask, x, dims):
    mb = mask.astype(BF16)
    parts = [lax.dot_general(mb, t, dims, preferred_element_type=F32) for t in _split(x, 3)]
    return parts[0] + (parts[1] + parts[2])


@jax.custom_vjp
def _fine_nt(a, b):
    return _fine_product(a, b, _NT)


_fine_nt.defvjp(lambda a, b: (_fine_product(a, b, _NT), (a, b)),
                lambda res, ct: (_fine_product(ct, res[1], _NN), _fine_product(ct, res[0], _TN)))


@jax.custom_vjp
def _mask_nn(mask, x):
    return _mask_product(mask, x, _NN)


_mask_nn.defvjp(lambda mask, x: (_mask_product(mask, x, _NN), mask),
                lambda mask, ct: (jnp.zeros_like(mask), _mask_product(mask, ct, _TN)))


@jax.custom_vjp
def _unit_lower_inverse(low):
    n = low.shape[-1]
    eye = (lax.broadcasted_iota(jnp.int32, low.shape, 1) == lax.broadcasted_iota(jnp.int32, low.shape, 2)).astype(F32)
    tinv = eye - low
    p = low
    for _ in range(n.bit_length() - 2):
        p = _bmm(p, p, _NN)
        tinv = tinv + _bmm(tinv, p, _NN)
    return tinv


def _unit_lower_inverse_fwd(low):
    tinv = _unit_lower_inverse(low)
    return tinv, tinv


_unit_lower_inverse.defvjp(_unit_lower_inverse_fwd, lambda tinv, ct: (-_bmm(_bmm(tinv, ct, _TN), tinv, _NT),))


def _dn_chunk(qc, kc, vc, zc, braw, araw, alog, dtb, nw, S):
    H, C, D = qc.shape
    row = lax.broadcasted_iota(jnp.int32, (H, C, C), 1)
    col = lax.broadcasted_iota(jnp.int32, (H, C, C), 2)
    causal = row >= col
    strict = row > col
    eye = (row == col).astype(F32)

    q = qc * lax.rsqrt(jnp.sum(qc * qc, axis=-1, keepdims=True) + EPS) * (D ** -0.5)
    k = kc * lax.rsqrt(jnp.sum(kc * kc, axis=-1, keepdims=True) + EPS)
    beta = _sigmoid(braw)
    g = -jnp.exp(alog) * _softplus(araw + dtb)
    a_col = _mask_nn(causal.astype(F32), jnp.broadcast_to(g, (H, C, C)))
    a_row = _mask_nn(jnp.ones((H, C, C), F32), eye * a_col)
    decay = jnp.where(causal, jnp.exp(jnp.where(causal, a_col - a_row, 0.0)), 0.0)
    kb = k * beta
    tinv = _unit_lower_inverse(jnp.where(strict, _fine_nt(kb, k) * decay, 0.0))
    e_col = jnp.exp(a_col)
    u = _bmm(tinv, vc * beta, _NN)
    w = _bmm(tinv, kb * e_col, _NN)
    attn = _fine_nt(q, k) * decay
    gl = a_col[:, C - 1:C, :]
    k_dec = k * jnp.exp(gl - a_col)
    v_new = u - _bmm(w, S, _NN)
    o = _bmm(q * e_col, S, _NN) + _bmm(attn, v_new, _NN)
    s_new = S * jnp.exp(jnp.broadcast_to(gl, (H, D, D))) + _bmm(k_dec, v_new, _TN)
    on = o * lax.rsqrt(jnp.mean(o * o, axis=-1, keepdims=True) + EPS) * nw
    return on * (zc * _sigmoid(zc)), s_new


DN_ZCOLS = IN_COLS_PAD - OFF_Z
DN_ZBLK = OFF_Z // DN_ZCOLS


def _dn_heads(a, off):
    return jnp.stack([a[:, off + h * DN_D:off + (h + 1) * DN_D] for h in range(DN_HEADS)])


def _dn_gate_cols(zb, off):
    return jnp.stack([zb[:, off + h:off + h + 1] for h in range(DN_HEADS)])


def _dn_operands(x_ref, zb_ref):
    x, zb = x_ref[...], zb_ref[...]
    return (_dn_heads(x, 0), _dn_heads(x, V_B), _dn_heads(x, 2 * V_B), _dn_heads(zb, 0),
            _dn_gate_cols(zb, V_B), _dn_gate_cols(zb, V_B + DN_HEADS))


def dn_fwd(qkvc, proj, alog, dtb, nw, ag=()):
    T = qkvc.shape[0]
    H, C, D = DN_HEADS, DN_CHUNK, DN_D
    N = T // C

    def body(x_ref, zb_ref, alog_ref, dtb_ref, nw_ref, o_ref, sall_ref, s_ref):
        @pl.when(pl.program_id(0) == 0)
        def _():
            s_ref[...] = jnp.zeros_like(s_ref)

        s_in = s_ref[...]
        sall_ref[0] = s_in
        on, s_new = _dn_chunk(*_dn_operands(x_ref, zb_ref), alog_ref[...], dtb_ref[...], nw_ref[...], s_in)
        for h in range(H):
            o_ref[:, h * D:(h + 1) * D] = on[h]
        s_ref[...] = s_new

    par = pl.BlockSpec((H, 1, 1), lambda n: (0, 0, 0))
    outs, gathered, _ = _pcall(
        body, (qkvc, proj, alog, dtb, nw), name="dn_fwd", grid=(N,),
        in_specs=[pl.BlockSpec((C, QKV_B), lambda n: (n, 0)), pl.BlockSpec((C, DN_ZCOLS), lambda n: (n, DN_ZBLK)),
                  par, par, pl.BlockSpec((1, 1, D), lambda n: (0, 0, 0))],
        out_specs=[pl.BlockSpec((C, V_B), lambda n: (n, 0)), pl.BlockSpec((1, H, D, D), lambda n: (n, 0, 0, 0))],
        out_shape=[jax.ShapeDtypeStruct((T, V_B), F32), jax.ShapeDtypeStruct((N, H, D, D), F32)],
        scratch_shapes=[pltpu.VMEM((H, D, D), F32)], sem=("arbitrary",), ag=ag)
    return outs, gathered


def dn_bwd(qkvc, proj, alog, dtb, nw, sall, dmix, rs=()):
    T = qkvc.shape[0]
    H, C, D = DN_HEADS, DN_CHUNK, DN_D
    N = T // C

    def body(x_ref, zb_ref, alog_ref, dtb_ref, nw_ref, sall_ref, do_ref,
             dx_ref, dzb_ref, dalog_ref, ddtb_ref, dnw_ref, ds_ref):
        @pl.when(pl.program_id(0) == 0)
        def _():
            ds_ref[...] = jnp.zeros_like(ds_ref)
            dalog_ref[...] = jnp.zeros_like(dalog_ref)
            ddtb_ref[...] = jnp.zeros_like(ddtb_ref)
            dnw_ref[...] = jnp.zeros_like(dnw_ref)

        args = (*_dn_operands(x_ref, zb_ref), alog_ref[...], dtb_ref[...], nw_ref[...], sall_ref[0])
        _, vjp = jax.vjp(_dn_chunk, *args)
        dq, dk, dv, dz, db, da, dalog, ddtb, dnw, ds = vjp((_dn_heads(do_ref[...], 0), ds_ref[...]))
        for h in range(H):
            cols = slice(h * D, (h + 1) * D)
            dx_ref[:, cols] = dq[h]
            dx_ref[:, V_B + h * D:V_B + (h + 1) * D] = dk[h]
            dx_ref[:, 2 * V_B + h * D:2 * V_B + (h + 1) * D] = dv[h]
            dzb_ref[:, cols] = dz[h]
        lane = lax.broadcasted_iota(jnp.int32, (C, LANES), 1)
        tail = jnp.zeros((C, LANES), F32)
        for h in range(H):
            tail = tail + jnp.where(lane == h, jnp.broadcast_to(db[h], (C, LANES)), 0.0)
            tail = tail + jnp.where(lane == H + h, jnp.broadcast_to(da[h], (C, LANES)), 0.0)
        dzb_ref[:, V_B:V_B + LANES] = tail
        dzb_ref[:, V_B + LANES:] = jnp.zeros((C, DN_ZCOLS - V_B - LANES), F32)
        dalog_ref[...] += dalog
        ddtb_ref[...] += ddtb
        dnw_ref[...] += dnw
        ds_ref[...] = ds

    par = pl.BlockSpec((H, 1, 1), lambda i: (0, 0, 0))
    nws = pl.BlockSpec((1, 1, D), lambda i: (0, 0, 0))
    outs, _, slots = _pcall(
        body, (qkvc, proj, alog, dtb, nw, sall, dmix), name="dn_bwd", grid=(N,),
        in_specs=[pl.BlockSpec((C, QKV_B), lambda i: (N - 1 - i, 0)),
                  pl.BlockSpec((C, DN_ZCOLS), lambda i: (N - 1 - i, DN_ZBLK)), par, par, nws,
                  pl.BlockSpec((1, H, D, D), lambda i: (N - 1 - i, 0, 0, 0)),
                  pl.BlockSpec((C, V_B), lambda i: (N - 1 - i, 1))],
        out_specs=[pl.BlockSpec((C, QKV_B), lambda i: (N - 1 - i, 0)),
                   pl.BlockSpec((C, DN_ZCOLS), lambda i: (N - 1 - i, 0)), par, par, nws],
        out_shape=[jax.ShapeDtypeStruct((T, QKV_B), F32), jax.ShapeDtypeStruct((T, DN_ZCOLS), F32)]
        + [jax.ShapeDtypeStruct((H, 1, 1), F32)] * 2 + [jax.ShapeDtypeStruct((1, 1, D), F32)],
        scratch_shapes=[pltpu.VMEM((H, D, D), F32)], sem=("arbitrary",), rs=rs)
    return outs, slots


def _conv_taps(buf_ref, w, width, halo, tm):
    acc = None
    for kk, win in _windows(buf_ref, [halo - (width - 1) + kk for kk in range(width)], tm):
        term = w[kk:kk + 1, :] * win
        acc = term if acc is None else acc + term
    return acc


def _windows(ref, offsets, tm):
    for res in range(SUBLANES):
        ks = [k for k, o in enumerate(offsets) if o % SUBLANES == res]
        if not ks:
            continue
        lo = min(offsets[k] for k in ks)
        hi = max(offsets[k] for k in ks)
        shifted = ref[pl.ds(lo, tm + hi - lo), :]
        for k in ks:
            yield k, shifted[offsets[k] - lo:offsets[k] - lo + tm]


def _conv_taps_bwd(dbuf_ref, w, width, tm):
    acc = None
    for kk, win in _windows(dbuf_ref, [width - 1 - kk for kk in range(width)], tm):
        term = w[kk:kk + 1, :] * win
        acc = term if acc is None else acc + term
    return acc


def _conv_dw_acc(dw_ref, dout, buf_ref, width, halo, tm):
    for kk, win in _windows(buf_ref, [halo - (width - 1) + kk for kk in range(width)], tm):
        dw_ref[pl.ds(kk, 1), :] += jnp.sum(dout * win, axis=0, keepdims=True)


DNC_HALO = 8
DNC_COLS = 768


def dnconv_fwd(proj, w):
    T = proj.shape[0]
    tm = min(T, 256)
    hb = tm // DNC_HALO

    def body(x_ref, h_ref, w_ref, o_ref, buf_ref):
        i = pl.program_id(0)
        buf_ref[0:DNC_HALO, :] = jnp.where(i > 0, h_ref[...], 0.0)
        buf_ref[DNC_HALO:, :] = x_ref[...]
        acc = _conv_taps(buf_ref, w_ref[...], DN_CONV, DNC_HALO, tm)
        o_ref[...] = acc * _sigmoid(acc)

    return pl.pallas_call(
        body, name="dnconv_fwd", grid=(T // tm, 2),
        in_specs=[pl.BlockSpec((tm, DNC_COLS), lambda i, c: (i, 1 + c)),
                  pl.BlockSpec((DNC_HALO, DNC_COLS), lambda i, c: (jnp.maximum(i * hb - 1, 0), 1 + c)),
                  pl.BlockSpec((DN_CONV, DNC_COLS), lambda i, c: (0, c))],
        out_specs=pl.BlockSpec((tm, DNC_COLS), lambda i, c: (i, c)),
        out_shape=jax.ShapeDtypeStruct((T, QKV_B), F32),
        scratch_shapes=[pltpu.VMEM((DNC_HALO + tm, DNC_COLS), F32)],
        compiler_params=_cparams(("parallel", "parallel")),
    )(proj, proj, w)


def dnconv_bwd(proj, w, dout):
    T = proj.shape[0]
    tm = min(T, 256)
    nt = T // tm
    hb = tm // DNC_HALO

    def body(x_ref, h_ref, w_ref, do_ref, dx_ref, dw_ref, buf_ref, dbuf_ref):
        r = pl.program_id(1)
        i = nt - 1 - r

        @pl.when(r == 0)
        def _():
            dw_ref[...] = jnp.zeros_like(dw_ref)
            dbuf_ref[tm:, :] = jnp.zeros((DNC_HALO, DNC_COLS), F32)

        buf_ref[0:DNC_HALO, :] = jnp.where(i > 0, h_ref[...], 0.0)
        buf_ref[DNC_HALO:, :] = x_ref[...]
        wv = w_ref[...]
        acc = _conv_taps(buf_ref, wv, DN_CONV, DNC_HALO, tm)
        sg = _sigmoid(acc)
        dacc = do_ref[...] * (sg * (1.0 + acc * (1.0 - sg)))
        dbuf_ref[0:tm, :] = dacc
        dx_ref[...] = _conv_taps_bwd(dbuf_ref, wv, DN_CONV, tm)
        _conv_dw_acc(dw_ref, dacc, buf_ref, DN_CONV, DNC_HALO, tm)
        dbuf_ref[tm:, :] = dacc[0:DNC_HALO, :]

    return pl.pallas_call(
        body, name="dnconv_bwd", grid=(2, nt),
        in_specs=[pl.BlockSpec((tm, DNC_COLS), lambda c, r: (nt - 1 - r, 1 + c)),
                  pl.BlockSpec((DNC_HALO, DNC_COLS), lambda c, r: (jnp.maximum((nt - 1 - r) * hb - 1, 0), 1 + c)),
                  pl.BlockSpec((DN_CONV, DNC_COLS), lambda c, r: (0, c)),
                  pl.BlockSpec((tm, DNC_COLS), lambda c, r: (nt - 1 - r, c))],
        out_specs=[pl.BlockSpec((tm, DNC_COLS), lambda c, r: (nt - 1 - r, c)),
                   pl.BlockSpec((DN_CONV, DNC_COLS), lambda c, r: (0, c))],
        out_shape=[jax.ShapeDtypeStruct((T, QKV_B), F32), jax.ShapeDtypeStruct((DN_CONV, QKV_B), F32)],
        scratch_shapes=[pltpu.VMEM((DNC_HALO + tm, DNC_COLS), F32), pltpu.VMEM((tm + DNC_HALO, DNC_COLS), F32)],
        compiler_params=_cparams(("parallel", "arbitrary")),
    )(proj, proj, w, dout)


CV_HALO = 32


def _cv_post(cv, lnw, lnb):
    mu = jnp.mean(cv, axis=-1, keepdims=True)
    xc = cv - mu
    y = xc * lax.rsqrt(jnp.mean(xc * xc, axis=-1, keepdims=True) + EPS) * lnw + lnb
    return y * _sigmoid(y)


def cv_fwd(ab, w, bdw, lnw, lnb, ag=()):
    T = ab.shape[0]
    D = ab.shape[1] // 2
    tm = min(T, 256)
    hb = tm // CV_HALO

    def body(a_ref, b_ref, ah_ref, bh_ref, w_ref, bdw_ref, lnw_ref, lnb_ref, o_ref, cv_ref, buf_ref):
        i = pl.program_id(0)
        buf_ref[0:CV_HALO, :] = jnp.where(i > 0, ah_ref[...] * _sigmoid(bh_ref[...]), 0.0)
        buf_ref[CV_HALO:, :] = a_ref[...] * _sigmoid(b_ref[...])
        cv = _conv_taps(buf_ref, w_ref[...], CONV_WIDTH, CV_HALO, tm) + bdw_ref[...]
        cv_ref[...] = cv
        o_ref[...] = _cv_post(cv, lnw_ref[...], lnb_ref[...])

    halo = lambda c: pl.BlockSpec((CV_HALO, D), lambda i: (jnp.maximum(i * hb - 1, 0), c))
    vec = pl.BlockSpec((1, D), lambda i: (0, 0))
    tile = pl.BlockSpec((tm, D), lambda i: (i, 0))
    outs, gathered, _ = _pcall(
        body, (ab, ab, ab, ab, w, bdw, lnw, lnb), name="cv_fwd", grid=(T // tm,),
        in_specs=[tile, pl.BlockSpec((tm, D), lambda i: (i, 1)),
                  halo(0), halo(1), pl.BlockSpec((CONV_WIDTH, D), lambda i: (0, 0)), vec, vec, vec],
        out_specs=[tile, tile],
        out_shape=[jax.ShapeDtypeStruct((T, D), F32), jax.ShapeDtypeStruct((T, D), F32)],
        scratch_shapes=[pltpu.VMEM((CV_HALO + tm, D), F32)], sem=("arbitrary",), ag=ag)
    return outs, gathered


def cv_bwd(ab, cv, w, lnw, lnb, dout, rs=()):
    T = ab.shape[0]
    D = ab.shape[1] // 2
    tm = min(T, 256)
    nt = T // tm
    hb = tm // CV_HALO

    def body(a_ref, b_ref, ah_ref, bh_ref, cv_ref, w_ref, lnw_ref, lnb_ref, do_ref,
             da_ref, db_ref, dw_ref, dbdw_ref, dlnw_ref, dlnb_ref, buf_ref, dbuf_ref):
        r = pl.program_id(0)
        i = nt - 1 - r

        @pl.when(r == 0)
        def _():
            dw_ref[...] = jnp.zeros_like(dw_ref)
            dbdw_ref[...] = jnp.zeros_like(dbdw_ref)
            dlnw_ref[...] = jnp.zeros_like(dlnw_ref)
            dlnb_ref[...] = jnp.zeros_like(dlnb_ref)
            dbuf_ref[tm:, :] = jnp.zeros((CV_HALO, D), F32)

        a = a_ref[...]
        sb = _sigmoid(b_ref[...])
        buf_ref[0:CV_HALO, :] = jnp.where(i > 0, ah_ref[...] * _sigmoid(bh_ref[...]), 0.0)
        buf_ref[CV_HALO:, :] = a * sb
        wv = w_ref[...]
        _, vjp = jax.vjp(_cv_post, cv_ref[...], lnw_ref[...], lnb_ref[...])
        dcv, dlnw, dlnb = vjp(do_ref[...])
        dlnw_ref[...] += dlnw
        dlnb_ref[...] += dlnb
        dbdw_ref[...] += jnp.sum(dcv, axis=0, keepdims=True)
        dbuf_ref[0:tm, :] = dcv
        du = _conv_taps_bwd(dbuf_ref, wv, CONV_WIDTH, tm)
        _conv_dw_acc(dw_ref, dcv, buf_ref, CONV_WIDTH, CV_HALO, tm)
        dbuf_ref[tm:, :] = dcv[0:CV_HALO, :]
        da_ref[...] = du * sb
        db_ref[...] = du * a * sb * (1.0 - sb)

    tile = lambda c: pl.BlockSpec((tm, D), lambda r: (nt - 1 - r, c))
    halo = lambda c: pl.BlockSpec((CV_HALO, D), lambda r: (jnp.maximum((nt - 1 - r) * hb - 1, 0), c))
    vec = pl.BlockSpec((1, D), lambda r: (0, 0))
    wsp = pl.BlockSpec((CONV_WIDTH, D), lambda r: (0, 0))
    (da, db, dw, dbdw, dlnw, dlnb), _, slots = _pcall(
        body, (ab, ab, ab, ab, cv, w, lnw, lnb, dout), name="cv_bwd", grid=(nt,),
        in_specs=[tile(0), tile(1), halo(0), halo(1), tile(0), wsp, vec, vec, tile(0)],
        out_specs=[tile(0), tile(0), wsp, vec, vec, vec],
        out_shape=[jax.ShapeDtypeStruct((T, D), F32), jax.ShapeDtypeStruct((T, D), F32),
                   jax.ShapeDtypeStruct((CONV_WIDTH, D), F32)] + [jax.ShapeDtypeStruct((1, D), F32)] * 3,
        scratch_shapes=[pltpu.VMEM((CV_HALO + tm, D), F32), pltpu.VMEM((tm + CV_HALO, D), F32)],
        sem=("arbitrary",), rs=rs)
    return (jnp.concatenate([da, db], axis=1), dw, dbdw, dlnw, dlnb), slots


def adamw(w, m, v, slots, rs=()):
    L, R, C = w.shape
    fits = lambda r, c: N_DEV * r * c * 2 <= ADAM_SLOT_BLOCK
    tiles = [(R, C)] if fits(R, C) else []
    tiles += [(d, C) for d in range(16, R, 16) if R % d == 0 and fits(d, C)]
    tiles += [(R, d) for d in range(LANES, C, LANES) if C % d == 0 and fits(R, d)]
    tr, tc = max(tiles, key=lambda t: t[0] * t[1])
    c1 =1.0 / (1.0 - ADAM_B1 ** ADAM_STEP)
    c2 = 1.0 / (1.0 - ADAM_B2 ** ADAM_STEP)

    def body(w_ref, m_ref, v_ref, *rest):
        s_refs = rest[:L]
        g_ref, d_ref, nm_ref, nv_ref = rest[L:]
        l = pl.program_id(0)
        for k in range(L):
            @pl.when(l == k)
            def _(s_ref=s_refs[k]):
                g = s_ref[0].astype(F32)
                for j in range(1, N_DEV):
                    g = g + s_ref[j].astype(F32)
                nm = ADAM_B1 * m_ref[0] + (1.0 - ADAM_B1) * g
                nv = ADAM_B2 * v_ref[0] + (1.0 - ADAM_B2) * (g * g)
                g_ref[0] = g
                nm_ref[0] = nm
                nv_ref[0] = nv
                d_ref[0] = -ADAM_LR * ((nm * c1) / (jnp.sqrt(nv * c2) + ADAM_EPS) + ADAM_WD * w_ref[0])

    nc = C // tc
    blk = pl.BlockSpec((1, tr, tc), lambda l, i: (l, i // nc, i % nc))
    slot = lambda k: pl.BlockSpec((N_DEV, tr, tc), lambda l, i: (0, jnp.where(l == k, i // nc, 0),
                                                                 jnp.where(l == k, i % nc, 0)))
    outs, _, landed = _pcall(
        body, (w, m, v, *slots), name="adamw", grid=(L, (R // tr) * nc),
        in_specs=[blk, blk, blk] + [slot(k) for k in range(L)],
        out_specs=[blk, blk, blk, blk],
        out_shape=[jax.ShapeDtypeStruct((L, R, C), F32)] * 4,
        sem=("arbitrary", "arbitrary"), rs=rs)
    return outs, landed


def _unshard(g, axis):
    g = jnp.moveaxis(g, 0, axis)
    s = g.shape
    return g.reshape(s[:axis] + (s[axis] * s[axis + 1],) + s[axis + 2:])


def _to_blocks(full, axis):
    s = full.shape
    g = full.reshape(s[:axis] + (N_DEV, s[axis] // N_DEV) + s[axis + 1:])
    return jnp.moveaxis(g, axis, 0)


def _heads(a, h):
    T = a.shape[0]
    return a.reshape(T, h, a.shape[1] // h).transpose(1, 0, 2)


def _unheads(a):
    h, T, d = a.shape
    return a.transpose(1, 0, 2).reshape(T, h * d)


SMALL = (("norm_w", 2), ("dn_conv_w", 2), ("conv_b_pw1", 1), ("conv_w_dw", 2), ("conv_b_dw", 1),
         ("conv_ln_w", 1), ("conv_ln_b", 1), ("conv_b_pw2", 1),
         ("attn_sinks", None), ("dn_a_log", None), ("dn_dt_bias", None), ("dn_norm_w", None), ("final_norm_w", None))

def _pack(parts):
    flat = jnp.concatenate([p.reshape(-1) for p in parts])
    pad = (-flat.shape[0]) % LANES
    return jnp.pad(flat, (0, pad))


def _unpack(flat, shapes):
    out, off = [], 0
    for s in shapes:
        n = int(np.prod(s))
        out.append(flat[off:off + n].reshape(s))
        off += n
    return out


def kernel(x, norm_w, ffn_w_gate, ffn_w_up, ffn_w_down, mix_w_in, dn_conv_w, attn_sinks, dn_a_log, dn_dt_bias, dn_norm_w, mix_w_out, conv_w_pw1, conv_b_pw1, conv_w_dw, conv_b_dw, conv_ln_w, conv_ln_b, conv_w_pw2, conv_b_pw2, final_norm_w, loss_target, m_norm_w, m_ffn_w_gate, m_ffn_w_up, m_ffn_w_down, m_mix_w_in, m_dn_conv_w, m_attn_sinks, m_dn_a_log, m_dn_dt_bias, m_dn_norm_w, m_mix_w_out, m_conv_w_pw1, m_conv_b_pw1, m_conv_w_dw, m_conv_b_dw, m_conv_ln_w, m_conv_ln_b, m_conv_w_pw2, m_conv_b_pw2, m_final_norm_w, v_norm_w, v_ffn_w_gate, v_ffn_w_up, v_ffn_w_down, v_mix_w_in, v_dn_conv_w, v_attn_sinks, v_dn_a_log, v_dn_dt_bias, v_dn_norm_w, v_mix_w_out, v_conv_w_pw1, v_conv_b_pw1, v_conv_w_dw, v_conv_b_dw, v_conv_ln_w, v_conv_ln_b, v_conv_w_pw2, v_conv_b_pw2, v_final_norm_w):
    W = dict(norm_w=norm_w, ffn_w_gate=ffn_w_gate, ffn_w_up=ffn_w_up, ffn_w_down=ffn_w_down, mix_w_in=mix_w_in,
             dn_conv_w=dn_conv_w, attn_sinks=attn_sinks, dn_a_log=dn_a_log, dn_dt_bias=dn_dt_bias,
             dn_norm_w=dn_norm_w, mix_w_out=mix_w_out, conv_w_pw1=conv_w_pw1, conv_b_pw1=conv_b_pw1,
             conv_w_dw=conv_w_dw, conv_b_dw=conv_b_dw, conv_ln_w=conv_ln_w, conv_ln_b=conv_ln_b,
             conv_w_pw2=conv_w_pw2, conv_b_pw2=conv_b_pw2, final_norm_w=final_norm_w)
    M = dict(norm_w=m_norm_w, ffn_w_gate=m_ffn_w_gate, ffn_w_up=m_ffn_w_up, ffn_w_down=m_ffn_w_down,
             mix_w_in=m_mix_w_in, dn_conv_w=m_dn_conv_w, attn_sinks=m_attn_sinks, dn_a_log=m_dn_a_log,
             dn_dt_bias=m_dn_dt_bias, dn_norm_w=m_dn_norm_w, mix_w_out=m_mix_w_out, conv_w_pw1=m_conv_w_pw1,
             conv_b_pw1=m_conv_b_pw1, conv_w_dw=m_conv_w_dw, conv_b_dw=m_conv_b_dw, conv_ln_w=m_conv_ln_w,
             conv_ln_b=m_conv_ln_b, conv_w_pw2=m_conv_w_pw2, conv_b_pw2=m_conv_b_pw2, final_norm_w=m_final_norm_w)
    V = dict(norm_w=v_norm_w, ffn_w_gate=v_ffn_w_gate, ffn_w_up=v_ffn_w_up, ffn_w_down=v_ffn_w_down,
             mix_w_in=v_mix_w_in, dn_conv_w=v_dn_conv_w, attn_sinks=v_attn_sinks, dn_a_log=v_dn_a_log,
             dn_dt_bias=v_dn_dt_bias, dn_norm_w=v_dn_norm_w, mix_w_out=v_mix_w_out, conv_w_pw1=v_conv_w_pw1,
             conv_b_pw1=v_conv_b_pw1, conv_w_dw=v_conv_w_dw, conv_b_dw=v_conv_b_dw, conv_ln_w=v_conv_ln_w,
             conv_ln_b=v_conv_ln_b, conv_w_pw2=v_conv_w_pw2, conv_b_pw2=v_conv_b_pw2, final_norm_w=v_final_norm_w)

    T, D = x.shape[1], x.shape[2]
    xs = x[0]
    F8 = ffn_w_gate.shape[-1]
    n_ffn = DEPTH * 2

    big = ("ffn_w_gate", "ffn_w_up", "ffn_w_down", "mix_w_in", "mix_w_out", "conv_w_pw1", "conv_w_pw2")
    shard3 = {k: W[k].reshape((-1,) + W[k].shape[-2:]) for k in big}
    shard_bf = {k: shard3[k].astype(BF16) for k in big}
    ffn_unit = lambda i: [("ffn_w_gate", i), ("ffn_w_up", i), ("ffn_w_down", i)]
    even_unit = lambda e: [("mix_w_in", e), ("mix_w_out", e)]
    odd_unit = lambda e: [("conv_w_pw1", e), ("conv_w_pw2", e)]
    have = {}

    def ag_jobs(units):
        return [(shard_bf[k], i) for k, i in units]

    def ag_done(units, gathered):
        have.update(zip(units, gathered))

    small_sharded = [(k, ax) for k, ax in SMALL if ax is not None]
    small_pack = _pack([W[k] for k, _ in small_sharded])[None, :]
    first_units = ffn_unit(0)
    gathered, _ = exchange(ag=ag_jobs(first_units) + [(small_pack, None)])
    ag_done(first_units, gathered[:-1])
    small_full = {}
    for (k, ax), parts in zip(small_sharded,
                              zip(*[_unpack(gathered[-1][s, 0], [W[k].shape for k, _ in small_sharded])
                                    for s in range(N_DEV)])):
        small_full[k] = _unshard(jnp.stack(parts), ax)
    nw_full = small_full["norm_w"]

    ffn_w = lambda i: [have[u] for u in ffn_unit(i)]
    w_in_of = lambda e: jnp.pad(_unshard(have[("mix_w_in", e)], 1), ((0, 0), (0, IN_COLS_PAD - IN_COLS)))
    w_out_of = lambda e: have[("mix_w_out", e)].reshape(D, D)
    w_pw1_of = lambda e: _unshard(have[("conv_w_pw1", e)], 1)
    w_pw2_of = lambda e: have[("conv_w_pw2", e)].reshape(D, D)
    fwd_order, needed = [], {}
    for l in range(DEPTH):
        mixer = [("A", l), ("E", l)] if l % 2 == 0 else [("O", l)]
        fwd_order += [("F", 2 * l)] + mixer + [("F", 2 * l + 1)]
        needed[("F", 2 * l)], needed[("F", 2 * l + 1)] = ffn_unit(2 * l), ffn_unit(2 * l + 1)
        needed[mixer[0]] = even_unit(l // 2) if l % 2 == 0 else odd_unit(l // 2)
    queue = [(u, pos) for pos, key in enumerate(fwd_order) for u in needed.get(key, []) if u not in first_units]
    unit_bytes = lambda u: N_DEV * shard_bf[u[0]][u[1]].size * 2
    fwd_carry, at = {}, 0
    for pos, key in enumerate(fwd_order):
        cap = FWD_CARRY_BYTES[key[0]]
        taken, used = [], 0
        while at < len(queue) and (queue[at][1] <= pos + 1 or used + unit_bytes(queue[at][0]) <= cap):
            taken.append(queue[at][0])
            used += unit_bytes(queue[at][0])
            at += 1
        fwd_carry[key] = taken
    zero_in = jnp.zeros((1, IN_COLS_PAD), F32)
    zero_d = jnp.zeros((1, D), F32)
    slope_rows = jnp.asarray(np.repeat(2.0 ** (-8.0 * np.arange(1, ATTN_HEADS + 1) / ATTN_HEADS), ATTN_BLOCK)
                             .astype(np.float32)[:, None])

    saved = []
    h = xs
    w_in, w_out, w_pw1, w_pw2 = {}, {}, {}, {}

    def ffn_forward(h, l, half):
        i = 2 * l + half
        units = fwd_carry.get(("F", i), [])
        h, gathered = ffn_fwd(h, nw_full[l, 2 * half][None], *ffn_w(i), ag=ag_jobs(units))
        ag_done(units, gathered)
        return h

    for l in range(DEPTH):
        e = l // 2
        st = {"x0": h}
        h = ffn_forward(h, l, 0)
        st["x1"] = h
        if l % 2 == 0:
            w_in[e], w_out[e] = w_in_of(e), w_out_of(e)
            proj = rmslin_fwd(h, nw_full[l, 1][None], w_in[e], zero_in)
            st["proj"] = proj
            st["qkvc"] = dnconv_fwd(proj, small_full["dn_conv_w"][e])
            st["sink_rows"] = jnp.repeat(attn_sinks[e], ATTN_BLOCK)[:, None]
            st["alog"] = dn_a_log[e].reshape(DN_HEADS, 1, 1)
            st["dtb"] = dn_dt_bias[e].reshape(DN_HEADS, 1, 1)
            st["dnw"] = dn_norm_w[e].reshape(1, 1, DN_D)
            units = fwd_carry[("A", l)]
            st["att"], gathered = attn_fwd(proj, st["sink_rows"], slope_rows, ag=ag_jobs(units))
            ag_done(units, gathered)
            units = fwd_carry[("E", l)]
            (st["og"], st["sall"]), gathered = dn_fwd(st["qkvc"], proj, st["alog"], st["dtb"], st["dnw"],
                                                      ag=ag_jobs(units))
            ag_done(units, gathered)
            h = lin_fwd(h, [st["att"], st["og"]], w_out[e], zero_d)
        else:
            units = fwd_carry[("O", l)]
            w_pw1[e], w_pw2[e] = w_pw1_of(e), w_pw2_of(e)
            st["ab"] = rmslin_fwd(h, nw_full[l, 1][None], w_pw1[e], small_full["conv_b_pw1"][e][None])
            (st["act"], st["cv"]), gathered = cv_fwd(st["ab"], small_full["conv_w_dw"][e], small_full["conv_b_dw"][e][None],
                                         small_full["conv_ln_w"][e][None], small_full["conv_ln_b"][e][None],
                                         ag=ag_jobs(units))
            ag_done(units, gathered)
            h = lin_fwd(h, [st["act"]], w_pw2[e], small_full["conv_b_pw2"][e][None])
        st["x2"] = h
        h = ffn_forward(h, l, 1)
        saved.append(st)

    loss_part, dh, dfinal = loss_fwd_bwd(h, final_norm_w[None], loss_target[0])
    loss = lax.psum(loss_part[0, 0], ("x", "y", "c"))

    d_norm = [[None] * 3 for _ in range(DEPTH)]
    d_small = {k: [None, None] for k in ("dn_conv_w", "conv_b_pw1", "conv_w_dw", "conv_b_dw", "conv_ln_w",
                                         "conv_ln_b", "conv_b_pw2", "attn_sinks", "dn_a_log", "dn_dt_bias",
                                         "dn_norm_w")}
    pending, slot = [], {}

    def take_pending(cap=None):
        n, used = 0, 0
        while n < len(pending) and (cap is None or used + pending[n][1].size * 2 <= cap):
            used += pending[n][1].size * 2
            n += 1
        units = pending[:n]
        del pending[:n]
        return [u for u, _ in units], [b for _, b in units]

    def ffn_backward(dh, l, half):
        i = 2 * l + half
        units, blocks = take_pending(BWD_CARRY_BYTES["F"])
        (dh, dg, du, dd, d_norm[l][2 * half]), slots = ffn_bwd(
            st["x2" if half else "x0"], dh, nw_full[l, 2 * half][None], *ffn_w(i), rs=blocks)
        slot.update(zip(units, slots))
        pending.extend(zip(ffn_unit(i), (dg, du, dd)))
        return dh

    for l in reversed(range(DEPTH)):
        e = l // 2
        st = saved[l]
        dh = ffn_backward(dh, l, 1)
        if l % 2 == 0:
            dmix, d_out, _ = lin_bwd([st["att"], st["og"]], dh, w_out[e])
            pending.append((("mix_w_out", e), d_out.reshape(N_DEV, D // N_DEV, D).astype(BF16)))
            units, blocks = take_pending(BWD_CARRY_BYTES["E"])
            (dqkvc, dzba, dalog, ddtb, ddnw), slots = dn_bwd(
                st["qkvc"], st["proj"], st["alog"], st["dtb"], st["dnw"], st["sall"], dmix, rs=blocks)
            slot.update(zip(units, slots))
            units, blocks = take_pending(BWD_CARRY_BYTES["A"])
            (dqa, dkva, dsink), slots = attn_bwd(st["proj"], st["sink_rows"], slope_rows, dmix, rs=blocks)
            slot.update(zip(units, slots))
            dqkv, d_small["dn_conv_w"][e] = dnconv_bwd(st["proj"], small_full["dn_conv_w"][e], dqkvc)
            dproj = jnp.concatenate([dqa, dkva, dqkv, dzba], axis=1)
            dh, d_in, _, d_norm[l][1] = rmslin_bwd(st["x1"], dh, dproj, nw_full[l, 1][None], w_in[e])
            pending.append((("mix_w_in", e), _to_blocks(d_in[:, :IN_COLS], 1).astype(BF16)))
            d_small["attn_sinks"][e] = jnp.sum(dsink.reshape(ATTN_HEADS, ATTN_BLOCK), axis=1)
            d_small["dn_a_log"][e] = dalog.reshape(DN_HEADS)
            d_small["dn_dt_bias"][e] = ddtb.reshape(DN_HEADS)
            d_small["dn_norm_w"][e] = ddnw.reshape(DN_D)
        else:
            dact, d_pw2, d_small["conv_b_pw2"][e] = lin_bwd([st["act"]], dh, w_pw2[e])
            pending.append((("conv_w_pw2", e), d_pw2.reshape(N_DEV, D // N_DEV, D).astype(BF16)))
            units, blocks = take_pending(BWD_CARRY_BYTES["O"])
            (dab, d_small["conv_w_dw"][e], d_small["conv_b_dw"][e], d_small["conv_ln_w"][e],
             d_small["conv_ln_b"][e]), slots = cv_bwd(
                st["ab"], st["cv"], small_full["conv_w_dw"][e],
                small_full["conv_ln_w"][e][None], small_full["conv_ln_b"][e][None], dact, rs=blocks)
            slot.update(zip(units, slots))
            dh, d_pw1, d_small["conv_b_pw1"][e], d_norm[l][1] = rmslin_bwd(
                st["x1"], dh, dab, nw_full[l, 1][None], w_pw1[e])
            pending.append((("conv_w_pw1", e), _to_blocks(d_pw1, 1).astype(BF16)))
        dh = ffn_backward(dh, l, 0)
    grad_x = dh[None]

    full_small = {"norm_w": jnp.stack([jnp.concatenate(r, axis=0) for r in d_norm]),
                  "final_norm_w": dfinal[0]}
    for k, pair in d_small.items():
        full_small[k] = jnp.stack([p.reshape(W[k].shape[1:-1] + (-1,)) if SMALL_AXIS[k] is not None
                                   else p for p in pair])
    rows = []
    for s in range(N_DEV):
        parts = [_to_blocks(full_small[k], ax)[s] if ax is not None else full_small[k] for k, ax in SMALL]
        rows.append(_pack(parts))
    send_small = jnp.stack(rows)[:, None, :]
    pending.append((("small", 0), send_small))

    res = {}
    waiting = lambda k: [(u, b) for u, b in pending if u[0] == k]
    adam_order = sorted(big, key=lambda k: len(waiting(k))) + ["small"]
    for n, k in enumerate(adam_order[:-1]):
        riders = next((waiting(kk) for kk in adam_order[n + 1:] if waiting(kk)), [])
        pending[:] = [p for p in pending if p[0] not in [u for u, _ in riders]]
        outs, slots = adamw(shard3[k], M[k].reshape(shard3[k].shape), V[k].reshape(shard3[k].shape),
                            [slot[(k, i)] for i in range(shard3[k].shape[0])], rs=[b for _, b in riders])
        slot.update(zip([u for u, _ in riders], slots))
        res[k] = [o.reshape(W[k].shape) for o in outs]
    pk = lambda d: _pack([d[k] for k, _ in SMALL])[None, None, :]
    outs, _ = adamw(pk(W), pk(M), pk(V), [slot[("small", 0)]])
    shapes = [W[k].shape for k, _ in SMALL]
    unp = [_unpack(o[0, 0], shapes) for o in outs]
    for i, (k, _) in enumerate(SMALL):
        res[k] = [u[i] for u in unp]

    order = ("norm_w", "ffn_w_gate", "ffn_w_up", "ffn_w_down", "mix_w_in", "dn_conv_w", "attn_sinks", "dn_a_log",
             "dn_dt_bias", "dn_norm_w", "mix_w_out", "conv_w_pw1", "conv_b_pw1", "conv_w_dw", "conv_b_dw",
             "conv_ln_w", "conv_ln_b", "conv_w_pw2", "conv_b_pw2", "final_norm_w")
    return (loss, grad_x, *[res[k][0] for k in order], *[res[k][1] for k in order],
            *[res[k][2] for k in order], *[res[k][3] for k in order])


SMALL_AXIS = dict(SMALL)
```

```python
import functools

import numpy as np
import jax
import jax.numpy as jnp
from jax import lax
from jax.experimental import pallas as pl
from jax.experimental.pallas import tpu as pltpu

F32 = jnp.float32
BF16 = jnp.bfloat16
EPS = 1e-6
N_DEV = 8
V7X_VMEM_LIMIT = 60 * 2**20
MESH = pl.DeviceIdType.MESH
LANES = 128
SUBLANES = 8

DEPTH = 4
D_MODEL = 1024
ATTN_HEADS, ATTN_KV_HEADS, HEAD_DIM, ATTN_BLOCK = 8, 2, 64, 128
DN_HEADS, DN_D, DN_CHUNK, DN_CONV = 8, 64, 64, 4
CONV_WIDTH = 31
Q_A, KV_A, QKV_B, V_B = 512, 128, 1536, 512
IN_COLS = 2832
IN_COLS_PAD = 3072
OFF_QKVB = Q_A + 2 * KV_A
OFF_Z = OFF_QKVB + QKV_B
OFF_BETA = OFF_Z + V_B
OFF_A = OFF_BETA + DN_HEADS

FWD_CARRY_BYTES = {"F": 12 * 2**20, "A": 6 * 2**20, "E": 18 * 2**20, "O": 12 * 2**20}
BWD_CARRY_BYTES = {"F": 14 * 2**20, "A": 6 * 2**20, "E": 18 * 2**20, "O": 12 * 2**20}

ADAM_SLOT_BLOCK = 3 * 2**19

ADAM_LR, ADAM_B1, ADAM_B2, ADAM_EPS, ADAM_WD, ADAM_STEP = 0.001, 0.9, 0.999, 1e-08, 0.01, 10


def _cparams(sem):
    return pltpu.CompilerParams(dimension_semantics=sem, vmem_limit_bytes=V7X_VMEM_LIMIT)


def _sigmoid(x):
    return 1.0 / (1.0 + jnp.exp(-x))


def _softplus(x):
    return jnp.maximum(x, 0.0) + jnp.log(1.0 + jnp.exp(-jnp.abs(x)))


def _dot(a, b):
    return jnp.dot(a, b, preferred_element_type=F32)


def _dot_nt(a, b):
    return lax.dot_general(a, b, (((1,), (1,)), ((), ())), preferred_element_type=F32)


def _dot_tn(a, b):
    return lax.dot_general(a, b, (((0,), (0,)), ((), ())), preferred_element_type=F32)


def _rms(x, w):
    return x * lax.rsqrt(jnp.mean(x * x, axis=-1, keepdims=True) + EPS) * w


def _rms_bwd(x, w, dxn):
    r = lax.rsqrt(jnp.mean(x * x, axis=-1, keepdims=True) + EPS)
    xh = x * r
    dxh = dxn * w
    dx = r * (dxh - xh * jnp.mean(dxh * xh, axis=-1, keepdims=True))
    return dx, jnp.sum(dxn * xh, axis=0, keepdims=True)


def _position():
    return lax.axis_index("x"), lax.axis_index("y"), lax.axis_index("c")


def _dev_index(px, py, pc):
    return 4 * px + 2 * py + pc


def _rcopy(src, dst, send_sem, recv_sem, to):
    return pltpu.make_async_remote_copy(src_ref=src, dst_ref=dst, send_sem=send_sem, recv_sem=recv_sem,
                                        device_id=to, device_id_type=MESH)


def _ag_start(srcs, outs, send, recv, local):
    x, y, c = _position()
    me = _dev_index(x, y, c)
    chips = [(1 - x, y), (x, 1 - y), (1 - x, 1 - y)]
    for a, (src, out) in enumerate(zip(srcs, outs)):
        pltpu.make_async_copy(src, out.at[me], local.at[a]).start()
        _rcopy(src, out.at[me], send.at[a, 0], recv.at[a, 0], (x, y, 1 - c)).start()
        for j, chip in enumerate(chips):
            _rcopy(src, out.at[me], send.at[a, 1 + j], recv.at[a, 1 + j], (*chip, c)).start()


def _ag_finish(srcs, outs, send, recv, local):
    x, y, c = _position()
    me = _dev_index(x, y, c)
    sibling = (x, y, 1 - c)
    chips = [(1 - x, y), (x, 1 - y), (1 - x, 1 - y)]
    for j, chip in enumerate(chips):
        for a, out in enumerate(outs):
            blk = out.at[_dev_index(*chip, c)]
            _rcopy(blk, blk, send.at[a, 1 + j], recv.at[a, 1 + j], (x, y, c)).wait_recv()
            _rcopy(blk, blk, send.at[a, 4 + j], recv.at[a, 4 + j], sibling).start()
    for a, (src, out) in enumerate(zip(srcs, outs)):
        blk = out.at[_dev_index(x, y, 1 - c)]
        _rcopy(blk, blk, send.at[a, 0], recv.at[a, 0], (x, y, c)).wait_recv()
        for j, chip in enumerate(chips):
            blk = out.at[_dev_index(*chip, 1 - c)]
            _rcopy(blk, blk, send.at[a, 4 + j], recv.at[a, 4 + j], (x, y, c)).wait_recv()
        for k in range(N_DEV - 1):
            _rcopy(out.at[me], out.at[me], send.at[a, k], recv.at[a, k], (x, y, c)).wait_send()
        pltpu.make_async_copy(src, out.at[me], local.at[a]).wait()


def _rs_peer(r):
    x, y, c = _position()
    return x ^ ((r >> 2) & 1), y ^ ((r >> 1) & 1), c ^ (r & 1)


def _rs_start(ins, outs, send, recv, local):
    me = _dev_index(*_position())
    for a, (src, out) in enumerate(zip(ins, outs)):
        pltpu.make_async_copy(src.at[me], out.at[me], local.at[a]).start()
        for r in range(1, N_DEV):
            p = _rs_peer(r)
            _rcopy(src.at[_dev_index(*p)], out.at[me], send.at[a, r - 1], recv.at[a, r - 1], p).start()


def _rs_finish(ins, outs, send, recv, local):
    pos = _position()
    me = _dev_index(*pos)
    for a, (src, out) in enumerate(zip(ins, outs)):
        for r in range(1, N_DEV):
            blk = out.at[_dev_index(*_rs_peer(r))]
            _rcopy(blk, blk, send.at[a, r - 1], recv.at[a, r - 1], pos).wait_recv()
        for r in range(1, N_DEV):
            _rcopy(src.at[me], out.at[me], send.at[a, r - 1], recv.at[a, r - 1], pos).wait_send()
        pltpu.make_async_copy(src.at[me], out.at[me], local.at[a]).wait()


def _pcall(body, args, *, name, grid, in_specs, out_specs, out_shape, sem, scratch_shapes=(), ag=(), rs=()):
    na, nr = len(ag), len(rs)
    if na + nr == 0:
        outs = pl.pallas_call(body, name=name, grid=grid, in_specs=in_specs, out_specs=out_specs,
                              out_shape=out_shape, scratch_shapes=list(scratch_shapes),
                              compiler_params=_cparams(sem))(*args)
        return list(outs), [], []
    n_in, n_out, n_scr = len(in_specs), len(out_specs), len(scratch_shapes)
    ag_idx = [i for _, i in ag]

    def wrapped(*refs):
        cin, refs = refs[:n_in], refs[n_in:]
        ag_in, refs = refs[:na], refs[na:]
        rs_in, refs = refs[:nr], refs[nr:]
        cout, refs = refs[:n_out], refs[n_out:]
        ag_out, refs = refs[:na], refs[na:]
        rs_out, refs = refs[:nr], refs[nr:]
        cscr, sems = refs[:n_scr], refs[n_scr:]
        ag_src = [r if i is None else r.at[i] for r, i in zip(ag_in, ag_idx)]
        ids = [pl.program_id(d) for d in range(len(grid))]
        first = functools.reduce(jnp.logical_and, [i == 0 for i in ids])
        last = functools.reduce(jnp.logical_and, [i == g - 1 for i, g in zip(ids, grid)])

        @pl.when(first)
        def _():
            if na:
                _ag_start(ag_src, ag_out, *sems[:3])
            if nr:
                _rs_start(rs_in, rs_out, *sems[-3:])

        body(*cin, *cout, *cscr)

        @pl.when(last)
        def _():
            if na:
                _ag_finish(ag_src, ag_out, *sems[:3])
            if nr:
                _rs_finish(rs_in, rs_out, *sems[-3:])

    hbm = pl.BlockSpec(memory_space=pl.ANY)
    sem_shapes = []
    for n in (na, nr):
        if n:
            sem_shapes += [pltpu.SemaphoreType.DMA((n, N_DEV - 1)), pltpu.SemaphoreType.DMA((n, N_DEV - 1)),
                           pltpu.SemaphoreType.DMA((n,))]
    outs = pl.pallas_call(
        wrapped, name=name, grid=grid,
        in_specs=list(in_specs) + [hbm] * (na + nr),
        out_specs=list(out_specs) + [hbm] * (na + nr),
        out_shape=list(out_shape)
        + [jax.ShapeDtypeStruct((N_DEV,) + a.shape[-2:], a.dtype) for a, _ in ag]
        + [jax.ShapeDtypeStruct(b.shape, b.dtype) for b in rs],
        scratch_shapes=list(scratch_shapes) + sem_shapes,
        compiler_params=_cparams(sem),
    )(*args, *[a for a, _ in ag], *rs)
    return list(outs[:n_out]), list(outs[n_out:n_out + na]), list(outs[n_out + na:])


def exchange(ag=(), rs=()):
    def body(o_ref):
        o_ref[...] = jnp.zeros_like(o_ref)

    _, gathered, slots = _pcall(body, (), name="exchange", grid=(1,), in_specs=[],
                                out_specs=[pl.BlockSpec((8, LANES), lambda i: (0, 0))],
                                out_shape=[jax.ShapeDtypeStruct((8, LANES), F32)], sem=("arbitrary",), ag=ag, rs=rs)
    return gathered, slots


FFN_PAIR = 2


def _pair_cols(w_ref):
    return jnp.concatenate([w_ref[p] for p in range(FFN_PAIR)], axis=1)


def ffn_fwd(x, nw, wg, wu, wd, ag=()):
    T, D = x.shape
    F = wg.shape[2]
    P = FFN_PAIR
    J = wg.shape[0] // P
    tm = min(T, 1024)

    def body(x_ref, nw_ref, wg_ref, wu_ref, wd_ref, o_ref, xn_ref, acc_ref):
        j = pl.program_id(1)

        @pl.when(j == 0)
        def _():
            xn_ref[...] = _rms(x_ref[...], nw_ref[...]).astype(BF16)
            acc_ref[...] = jnp.zeros_like(acc_ref)

        xn = xn_ref[...]
        g = _dot(xn, _pair_cols(wg_ref))
        u = _dot(xn, _pair_cols(wu_ref))
        h = (g * _sigmoid(g) * u).astype(BF16)
        acc_ref[...] += _dot(h, wd_ref[...].reshape(P * F, D))

        @pl.when(j == J - 1)
        def _():
            o_ref[...] = x_ref[...] + 0.5 * acc_ref[...]

    (out,), gathered, _ = _pcall(
        body, (x, nw, wg, wu, wd), name="ffn_fwd", grid=(T // tm, J),
        in_specs=[pl.BlockSpec((tm, D), lambda t, j: (t, 0)),
                  pl.BlockSpec((1, D), lambda t, j: (0, 0)),
                  pl.BlockSpec((P, D, F), lambda t, j: (j, 0, 0)),
                  pl.BlockSpec((P, D, F), lambda t, j: (j, 0, 0)),
                  pl.BlockSpec((P, F, D), lambda t, j: (j, 0, 0))],
        out_specs=[pl.BlockSpec((tm, D), lambda t, j: (t, 0))],
        out_shape=[jax.ShapeDtypeStruct((T, D), F32)],
        scratch_shapes=[pltpu.VMEM((tm, D), BF16), pltpu.VMEM((tm, D), F32)],
        sem=("arbitrary", "arbitrary"), ag=ag)
    return out, gathered


def ffn_bwd(x, dy, nw, wg, wu, wd, rs=()):
    T, D = x.shape
    F = wg.shape[2]
    P = FFN_PAIR
    J = wg.shape[0] // P
    tm = min(T, 256)
    nt = T // tm

    def body(x_ref, dy_ref, nw_ref, wg_ref, wu_ref, wd_ref,
             dx_ref, dwg_ref, dwu_ref, dwd_ref, dnw_ref,
             xn_ref, dyh_ref, dxn_ref, awg_ref, awu_ref, awd_ref):
        j = pl.program_id(0)
        t = pl.program_id(1)
        rows = pl.ds(pl.multiple_of(t * tm, tm), tm)

        @pl.when(j == 0)
        def _():
            xn_ref[rows, :] = _rms(x_ref[...], nw_ref[...]).astype(BF16)
            dyh_ref[rows, :] = (0.5 * dy_ref[...]).astype(BF16)
            dxn_ref[rows, :] = jnp.zeros((tm, D), F32)

        @pl.when((j == 0) & (t == 0))
        def _():
            dnw_ref[...] = jnp.zeros_like(dnw_ref)

        @pl.when(t == 0)
        def _():
            awg_ref[...] = jnp.zeros_like(awg_ref)
            awu_ref[...] = jnp.zeros_like(awu_ref)
            awd_ref[...] = jnp.zeros_like(awd_ref)

        xn = xn_ref[rows, :]
        dyh = dyh_ref[rows, :]
        wg2, wu2 = _pair_cols(wg_ref), _pair_cols(wu_ref)
        g = _dot(xn, wg2)
        u = _dot(xn, wu2)
        sg = _sigmoid(g)
        s = g * sg
        h = (s * u).astype(BF16)
        dh = _dot_nt(dyh, wd_ref[...].reshape(P * F, D))
        du = (dh * s).astype(BF16)
        dg = (dh * u * (sg * (1.0 + g * (1.0 - sg)))).astype(BF16)
        awd_ref[...] += _dot_tn(h, dyh)
        awg_ref[...] += _dot_tn(xn, dg)
        awu_ref[...] += _dot_tn(xn, du)
        dxn_ref[rows, :] += _dot_nt(dg, wg2) + _dot_nt(du, wu2)

        @pl.when(t == nt - 1)
        def _():
            for p in range(P):
                dwg_ref[p] = awg_ref[:, p * F:(p + 1) * F].astype(BF16)
                dwu_ref[p] = awu_ref[:, p * F:(p + 1) * F].astype(BF16)
            dwd_ref[...] = awd_ref[...].astype(BF16).reshape(P, F, D)

        @pl.when(j == J - 1)
        def _():
            dx, dnw = _rms_bwd(x_ref[...], nw_ref[...], dxn_ref[rows, :])
            dx_ref[...] = dy_ref[...] + dx
            dnw_ref[...] += dnw

    ends = lambda j, t: (jnp.where((j == 0) | (j == J - 1), t, 0), 0)
    last = lambda j, t: (jnp.where(j == J - 1, t, 0), 0)
    outs, _, slots = _pcall(
        body, (x, dy, nw, wg, wu, wd), name="ffn_bwd", grid=(J, nt),
        in_specs=[pl.BlockSpec((tm, D), ends), pl.BlockSpec((tm, D), ends),
                  pl.BlockSpec((1, D), lambda j, t: (0, 0)),
                  pl.BlockSpec((P, D, F), lambda j, t: (j, 0, 0)),
                  pl.BlockSpec((P, D, F), lambda j, t: (j, 0, 0)),
                  pl.BlockSpec((P, F, D), lambda j, t: (j, 0, 0))],
        out_specs=[pl.BlockSpec((tm, D), last),
                   pl.BlockSpec((P, D, F), lambda j, t: (j, 0, 0)),
                   pl.BlockSpec((P, D, F), lambda j, t: (j, 0, 0)),
                   pl.BlockSpec((P, F, D), lambda j, t: (j, 0, 0)),
                   pl.BlockSpec((1, D), lambda j, t: (0, 0))],
        out_shape=[jax.ShapeDtypeStruct((T, D), F32),
                   jax.ShapeDtypeStruct((P * J, D, F), BF16), jax.ShapeDtypeStruct((P * J, D, F), BF16),
                   jax.ShapeDtypeStruct((P * J, F, D), BF16), jax.ShapeDtypeStruct((1, D), F32)],
        scratch_shapes=[pltpu.VMEM((T, D), BF16), pltpu.VMEM((T, D), BF16), pltpu.VMEM((T, D), F32),
                        pltpu.VMEM((D, P * F), F32), pltpu.VMEM((D, P * F), F32), pltpu.VMEM((P * F, D), F32)],
        sem=("arbitrary", "arbitrary"), rs=rs)
    return outs, slots


def rmslin_fwd(x, nw, w, b):
    T, D = x.shape
    N = w.shape[1]
    tm = min(T, 256)

    def body(x_ref, nw_ref, w_ref, b_ref, o_ref):
        xn = _rms(x_ref[...], nw_ref[...]).astype(BF16)
        o_ref[...] = _dot(xn, w_ref[...]) + b_ref[...]

    return pl.pallas_call(
        body, name="rmslin_fwd", grid=(T // tm,),
        in_specs=[pl.BlockSpec((tm, D), lambda t: (t, 0)), pl.BlockSpec((1, D), lambda t: (0, 0)),
                  pl.BlockSpec((D, N), lambda t: (0, 0)), pl.BlockSpec((1, N), lambda t: (0, 0))],
        out_specs=pl.BlockSpec((tm, N), lambda t: (t, 0)),
        out_shape=jax.ShapeDtypeStruct((T, N), F32),
        compiler_params=_cparams(("parallel",)),
    )(x, nw, w, b)


def rmslin_bwd(x, dres, dproj, nw, w):
    T, D = x.shape
    N = w.shape[1]
    nb = 1024
    nc = N // nb
    tm = min(T, 256)
    nt = T // tm

    def body(x_ref, dres_ref, dp_ref, nw_ref, w_ref, dx_ref, dw_ref, db_ref, dnw_ref, xn_ref, dxn_ref):
        c = pl.program_id(0)
        t = pl.program_id(1)
        rows = pl.ds(pl.multiple_of(t * tm, tm), tm)

        @pl.when(c == 0)
        def _():
            xn_ref[rows, :] = _rms(x_ref[...], nw_ref[...]).astype(BF16)
            dxn_ref[rows, :] = jnp.zeros((tm, D), F32)

        @pl.when((c == 0) & (t == 0))
        def _():
            dnw_ref[...] = jnp.zeros_like(dnw_ref)

        @pl.when(t == 0)
        def _():
            dw_ref[...] = jnp.zeros_like(dw_ref)
            db_ref[...] = jnp.zeros_like(db_ref)

        dpf = dp_ref[...]
        dp = dpf.astype(BF16)
        dw_ref[...] += _dot_tn(xn_ref[rows, :], dp)
        db_ref[...] += jnp.sum(dpf, axis=0, keepdims=True)
        dxn_ref[rows, :] += _dot_nt(dp, w_ref[...])

        @pl.when(c == nc - 1)
        def _():
            dx, dnw = _rms_bwd(x_ref[...], nw_ref[...], dxn_ref[rows, :])
            dx_ref[...] = dres_ref[...] + dx
            dnw_ref[...] += dnw

    ends = lambda c, t: (jnp.where((c == 0) | (c == nc - 1), t, 0), 0)
    last = lambda c, t: (jnp.where(c == nc - 1, t, 0), 0)
    return pl.pallas_call(
        body, name="rmslin_bwd", grid=(nc, nt),
        in_specs=[pl.BlockSpec((tm, D), ends), pl.BlockSpec((tm, D), last),
                  pl.BlockSpec((tm, nb), lambda c, t: (t, c)),
                  pl.BlockSpec((1, D), lambda c, t: (0, 0)),
                  pl.BlockSpec((D, nb), lambda c, t: (0, c))],
        out_specs=[pl.BlockSpec((tm, D), last),
                   pl.BlockSpec((D, nb), lambda c, t: (0, c)),
                   pl.BlockSpec((1, nb), lambda c, t: (0, c)),
                   pl.BlockSpec((1, D), lambda c, t: (0, 0))],
        out_shape=[jax.ShapeDtypeStruct((T, D), F32), jax.ShapeDtypeStruct((D, N), F32),
                   jax.ShapeDtypeStruct((1, N), F32), jax.ShapeDtypeStruct((1, D), F32)],
        scratch_shapes=[pltpu.VMEM((T, D), BF16), pltpu.VMEM((T, D), F32)],
        compiler_params=_cparams(("arbitrary", "arbitrary")),
    )(x, dres, dproj, nw, w)


def lin_fwd(res, parts, w, b):
    T = res.shape[0]
    K, N = w.shape
    tm = min(T, 512)
    n = len(parts)
    offs = [sum(p.shape[1] for p in parts[:i]) for i in range(n + 1)]

    def body(res_ref, *refs):
        a_refs, (w_ref, b_ref, o_ref) = refs[:n], refs[n:]
        acc = res_ref[...] + b_ref[...]
        for i, a_ref in enumerate(a_refs):
            acc = acc + _dot(a_ref[...].astype(BF16), w_ref[offs[i]:offs[i + 1], :])
        o_ref[...] = acc

    return pl.pallas_call(
        body, name="lin_fwd", grid=(T // tm,),
        in_specs=[pl.BlockSpec((tm, N), lambda t: (t, 0))]
        + [pl.BlockSpec((tm, p.shape[1]), lambda t: (t, 0)) for p in parts]
        + [pl.BlockSpec((K, N), lambda t: (0, 0)), pl.BlockSpec((1, N), lambda t: (0, 0))],
        out_specs=pl.BlockSpec((tm, N), lambda t: (t, 0)),
        out_shape=jax.ShapeDtypeStruct((T, N), F32),
        compiler_params=_cparams(("parallel",)),
    )(res, *parts, w, b)


def lin_bwd(parts, dy, w):
    T = dy.shape[0]
    K, N = w.shape
    tm = min(T, 256)
    n = len(parts)
    offs = [sum(p.shape[1] for p in parts[:i]) for i in range(n + 1)]

    def body(*refs):
        a_refs, (dy_ref, w_ref, da_ref, dw_ref, db_ref) = refs[:n], refs[n:]

        @pl.when(pl.program_id(0) == 0)
        def _():
            dw_ref[...] = jnp.zeros_like(dw_ref)
            db_ref[...] = jnp.zeros_like(db_ref)

        dyf = dy_ref[...]
        dyb = dyf.astype(BF16)
        da_ref[...] = _dot_nt(dyb, w_ref[...])
        for i, a_ref in enumerate(a_refs):
            dw_ref[offs[i]:offs[i + 1], :] += _dot_tn(a_ref[...].astype(BF16), dyb)
        db_ref[...] += jnp.sum(dyf, axis=0, keepdims=True)

    return pl.pallas_call(
        body, name="lin_bwd", grid=(T // tm,),
        in_specs=[pl.BlockSpec((tm, p.shape[1]), lambda t: (t, 0)) for p in parts]
        + [pl.BlockSpec((tm, N), lambda t: (t, 0)), pl.BlockSpec((K, N), lambda t: (0, 0))],
        out_specs=[pl.BlockSpec((tm, K), lambda t: (t, 0)), pl.BlockSpec((K, N), lambda t: (0, 0)),
                   pl.BlockSpec((1, N), lambda t: (0, 0))],
        out_shape=[jax.ShapeDtypeStruct((T, K), F32), jax.ShapeDtypeStruct((K, N), F32),
                   jax.ShapeDtypeStruct((1, N), F32)],
        compiler_params=_cparams(("arbitrary",)),
    )(*parts, dy, w)


def loss_fwd_bwd(x, fw, target):
    T, D = x.shape
    tm = min(T, 256)

    def body(x_ref, fw_ref, tg_ref, loss_ref, dx_ref, dfw_ref):
        @pl.when(pl.program_id(0) == 0)
        def _():
            loss_ref[...] = jnp.zeros_like(loss_ref)
            dfw_ref[...] = jnp.zeros_like(dfw_ref)

        xv = x_ref[...]
        w = fw_ref[...]
        err = _rms(xv, w) - tg_ref[...]
        row = jnp.sum(err * err, axis=-1, keepdims=True)
        loss_ref[...] += (0.5 / D) * jnp.sum(row, axis=0, keepdims=True)
        dx, dfw = _rms_bwd(xv, w, err * (1.0 / D))
        dx_ref[...] = dx
        dfw_ref[...] += dfw

    return pl.pallas_call(
        body, name="loss_fwd_bwd", grid=(T // tm,),
        in_specs=[pl.BlockSpec((tm, D), lambda t: (t, 0)), pl.BlockSpec((1, D), lambda t: (0, 0)),
                  pl.BlockSpec((tm, D), lambda t: (t, 0))],
        out_specs=[pl.BlockSpec((1, 1), lambda t: (0, 0)), pl.BlockSpec((tm, D), lambda t: (t, 0)),
                   pl.BlockSpec((1, D), lambda t: (0, 0))],
        out_shape=[jax.ShapeDtypeStruct((1, 1), F32), jax.ShapeDtypeStruct((T, D), F32),
                   jax.ShapeDtypeStruct((1, D), F32)],
        compiler_params=_cparams(("arbitrary",)),
    )(x, fw, target)


def _attn_masks(n, rows, blk):
    r = lax.broadcasted_iota(jnp.int32, (rows, 2 * blk), 0)
    jj = lax.broadcasted_iota(jnp.int32, (rows, 2 * blk), 1)
    dist = (r % blk) + blk - jj
    valid = (dist >= 0) & (dist < blk) & ((n > 0) | (jj >= blk))
    return dist.astype(F32), valid


def _attn_block(q, kcat, vcat, sink, slope, dist, valid):
    d = q.shape[-1]
    s = _dot_nt(q.astype(BF16), kcat.astype(BF16)) * (d ** -0.5)
    s = jnp.where(valid, s - slope * dist, -1e30)
    m = lax.stop_gradient(jnp.maximum(jnp.max(s, axis=-1, keepdims=True), sink))
    e = jnp.exp(s - m)
    p = e / (jnp.sum(e, axis=-1, keepdims=True) + jnp.exp(sink - m))
    return _dot(p.astype(BF16), vcat.astype(BF16))


ATTN_G = ATTN_HEADS // ATTN_KV_HEADS
ATTN_QW = ATTN_G * HEAD_DIM
ATTN_KCOL = Q_A // KV_A


def _attn_specs():
    blk = ATTN_BLOCK
    qs = pl.BlockSpec((blk, ATTN_QW), lambda h, n: (n, h))
    prev = lambda c: pl.BlockSpec((blk, KV_A), lambda h, n: (jnp.maximum(n - 1, 0), c))
    cur = lambda c: pl.BlockSpec((blk, KV_A), lambda h, n: (n, c))
    rowp = pl.BlockSpec((ATTN_G * blk, 1), lambda h, n: (h, 0))
    return qs, [prev(ATTN_KCOL), cur(ATTN_KCOL), prev(ATTN_KCOL + 1), cur(ATTN_KCOL + 1)], rowp


def _attn_operands(h, q_ref, kp_ref, kc_ref, vp_ref, vc_ref):
    d = HEAD_DIM
    q = jnp.concatenate([q_ref[:, g * d:(g + 1) * d] for g in range(ATTN_G)], axis=0)
    pick = lambda r: jnp.where(h == 0, r[:, :d], r[:, d:])
    kcat = jnp.concatenate([pick(kp_ref[...]), pick(kc_ref[...])], axis=0)
    vcat = jnp.concatenate([pick(vp_ref[...]), pick(vc_ref[...])], axis=0)
    return q, kcat, vcat


def attn_fwd(proj, sink_rows, slope_rows, ag=()):
    T = proj.shape[0]
    blk, d = ATTN_BLOCK, HEAD_DIM

    def body(q_ref, kp_ref, kc_ref, vp_ref, vc_ref, sink_ref, slope_ref, o_ref):
        h, n = pl.program_id(0), pl.program_id(1)
        dist, valid = _attn_masks(n, ATTN_G * blk, blk)
        q, kcat, vcat = _attn_operands(h, q_ref, kp_ref, kc_ref, vp_ref, vc_ref)
        o = _attn_block(q, kcat, vcat, sink_ref[...], slope_ref[...], dist, valid)
        for g in range(ATTN_G):
            o_ref[:, g * d:(g + 1) * d] = o[g * blk:(g + 1) * blk]

    qs, kv, rowp = _attn_specs()
    (out,), gathered, _ = _pcall(
        body, (proj, proj, proj, proj, proj, sink_rows, slope_rows), name="attn_fwd",
        grid=(ATTN_KV_HEADS, T // blk), in_specs=[qs] + kv + [rowp, rowp], out_specs=[qs],
        out_shape=[jax.ShapeDtypeStruct((T, Q_A), F32)], sem=("arbitrary", "arbitrary"), ag=ag)
    return out, gathered


def attn_bwd(proj, sink_rows, slope_rows, dmix, rs=()):
    T = proj.shape[0]
    blk, d = ATTN_BLOCK, HEAD_DIM

    def body(q_ref, kp_ref, kc_ref, vp_ref, vc_ref, sink_ref, slope_ref, do_ref, dq_ref, dkv_ref, dsink_ref):
        h, n = pl.program_id(0), pl.program_id(1)

        @pl.when((h == 0) & (n == 0))
        def _():
            dkv_ref[...] = jnp.zeros_like(dkv_ref)

        @pl.when(n == 0)
        def _():
            dsink_ref[...] = jnp.zeros_like(dsink_ref)

        dist, valid = _attn_masks(n, ATTN_G * blk, blk)
        q, kcat, vcat = _attn_operands(h, q_ref, kp_ref, kc_ref, vp_ref, vc_ref)
        do = jnp.concatenate([do_ref[:, g * d:(g + 1) * d] for g in range(ATTN_G)], axis=0)
        fn = functools.partial(_attn_block, slope=slope_ref[...], dist=dist, valid=valid)
        _, vjp = jax.vjp(fn, q, kcat, vcat, sink_ref[...])
        dq, dkcat, dvcat, dsink = vjp(do)
        for g in range(ATTN_G):
            dq_ref[:, g * d:(g + 1) * d] = dq[g * blk:(g + 1) * blk]
        dsink_ref[...] += dsink
        lane = lax.broadcasted_iota(jnp.int32, (2 * blk, 2 * KV_A), 1)
        mine = (lane % KV_A) // d == h
        both = jnp.where(mine, jnp.concatenate([dkcat, dkcat, dvcat, dvcat], axis=1), 0.0)

        @pl.when(n == 0)
        def _():
            dkv_ref[0:blk, :] += both[blk:]

        @pl.when(n > 0)
        def _():
            rows = pl.ds(pl.multiple_of((n - 1) * blk, blk), 2 * blk)
            dkv_ref[rows, :] += both

    qs, kv, rowp = _attn_specs()
    outs, _, slots = _pcall(
        body, (proj, proj, proj, proj, proj, sink_rows, slope_rows, dmix), name="attn_bwd",
        grid=(ATTN_KV_HEADS, T // blk), in_specs=[qs] + kv + [rowp, rowp, qs],
        out_specs=[qs, pl.BlockSpec((T, 2 * KV_A), lambda h, n: (0, 0)), rowp],
        out_shape=[jax.ShapeDtypeStruct((T, Q_A), F32), jax.ShapeDtypeStruct((T, 2 * KV_A), F32),
                   jax.ShapeDtypeStruct((ATTN_HEADS * blk, 1), F32)],
        sem=("arbitrary", "arbitrary"), rs=rs)
    return outs, slots


_NN = (((2,), (1,)), ((0,), (0,)))
_NT = (((2,), (2,)), ((0,), (0,)))
_TN = (((1,), (1,)), ((0,), (0,)))


def _bmm(a, b, dims):
    return lax.dot_general(a.astype(BF16), b.astype(BF16), dims, preferred_element_type=F32)


def _split(x, terms):
    out = []
    for _ in range(terms):
        t = x.astype(BF16)
        out.append(t)
        x = x - t.astype(F32)
    return out


def _fine_product(a, b, dims):
    (ah, al), (bh, bl) = _split(a, 2), _split(b, 2)
    dot = lambda x, y: lax.dot_general(x, y, dims, preferred_element_type=F32)
    return dot(ah, bh) + (dot(ah, bl) + dot(al, bh))


def _mask_product(mask, x, dims):
    mb = mask.astype(BF16)
    parts = [lax.dot_general(mb, t, dims, preferred_element_type=F32) for t in _split(x, 3)]
    return parts[0] + (parts[1] + parts[2])


@jax.custom_vjp
def _fine_nt(a, b):
    return _fine_product(a, b, _NT)


_fine_nt.defvjp(lambda a, b: (_fine_product(a, b, _NT), (a, b)),
                lambda res, ct: (_fine_product(ct, res[1], _NN), _fine_product(ct, res[0], _TN)))


@jax.custom_vjp
def _mask_nn(mask, x):
    return _mask_product(mask, x, _NN)


_mask_nn.defvjp(lambda mask, x: (_mask_product(mask, x, _NN), mask),
                lambda mask, ct: (jnp.zeros_like(mask), _mask_product(mask, ct, _TN)))


@jax.custom_vjp
def _unit_lower_inverse(low):
    n = low.shape[-1]
    eye = (lax.broadcasted_iota(jnp.int32, low.shape, 1) == lax.broadcasted_iota(jnp.int32, low.shape, 2)).astype(F32)
    tinv = eye - low
    p = low
    for _ in range(n.bit_length() - 2):
        p = _bmm(p, p, _NN)
        tinv = tinv + _bmm(tinv, p, _NN)
    return tinv


def _unit_lower_inverse_fwd(low):
    tinv = _unit_lower_inverse(low)
    return tinv, tinv


_unit_lower_inverse.defvjp(_unit_lower_inverse_fwd, lambda tinv, ct: (-_bmm(_bmm(tinv, ct, _TN), tinv, _NT),))


def _dn_chunk(qc, kc, vc, zc, braw, araw, alog, dtb, nw, S):
    H, C, D = qc.shape
    row = lax.broadcasted_iota(jnp.int32, (H, C, C), 1)
    col = lax.broadcasted_iota(jnp.int32, (H, C, C), 2)
    causal = row >= col
    strict = row > col
    eye = (row == col).astype(F32)

    q = qc * lax.rsqrt(jnp.sum(qc * qc, axis=-1, keepdims=True) + EPS) * (D ** -0.5)
    k = kc * lax.rsqrt(jnp.sum(kc * kc, axis=-1, keepdims=True) + EPS)
    beta = _sigmoid(braw)
    g = -jnp.exp(alog) * _softplus(araw + dtb)
    a_col = _mask_nn(causal.astype(F32), jnp.broadcast_to(g, (H, C, C)))
    a_row = _mask_nn(jnp.ones((H, C, C), F32), eye * a_col)
    decay = jnp.where(causal, jnp.exp(jnp.where(causal, a_col - a_row, 0.0)), 0.0)
    kb = k * beta
    tinv = _unit_lower_inverse(jnp.where(strict, _fine_nt(kb, k) * decay, 0.0))
    e_col = jnp.exp(a_col)
    u = _bmm(tinv, vc * beta, _NN)
    w = _bmm(tinv, kb * e_col, _NN)
    attn = _fine_nt(q, k) * decay
    gl = a_col[:, C - 1:C, :]
    k_dec = k * jnp.exp(gl - a_col)
    v_new = u - _bmm(w, S, _NN)
    o = _bmm(q * e_col, S, _NN) + _bmm(attn, v_new, _NN)
    s_new = S * jnp.exp(jnp.broadcast_to(gl, (H, D, D))) + _bmm(k_dec, v_new, _TN)
    on = o * lax.rsqrt(jnp.mean(o * o, axis=-1, keepdims=True) + EPS) * nw
    return on * (zc * _sigmoid(zc)), s_new


DN_ZCOLS = IN_COLS_PAD - OFF_Z
DN_ZBLK = OFF_Z // DN_ZCOLS


def _dn_heads(a, off):
    return jnp.stack([a[:, off + h * DN_D:off + (h + 1) * DN_D] for h in range(DN_HEADS)])


def _dn_gate_cols(zb, off):
    return jnp.stack([zb[:, off + h:off + h + 1] for h in range(DN_HEADS)])


DN_STEP_CHUNKS = 4


def _dn_operands(x_ref, zb_ref, rows):
    x, zb = x_ref[rows, :], zb_ref[rows, :]
    return (_dn_heads(x, 0), _dn_heads(x, V_B), _dn_heads(x, 2 * V_B), _dn_heads(zb, 0),
            _dn_gate_cols(zb, V_B), _dn_gate_cols(zb, V_B + DN_HEADS))


def dn_fwd(qkvc, proj, alog, dtb, nw, ag=()):
    T = qkvc.shape[0]
    H, C, D, G = DN_HEADS, DN_CHUNK, DN_D, DN_STEP_CHUNKS
    N = T // C

    def body(x_ref, zb_ref, alog_ref, dtb_ref, nw_ref, o_ref, sall_ref, s_ref):
        @pl.when(pl.program_id(0) == 0)
        def _():
            s_ref[...] = jnp.zeros_like(s_ref)

        s = s_ref[...]
        for c in range(G):
            rows = slice(c * C, (c + 1) * C)
            sall_ref[c] = s
            on, s = _dn_chunk(*_dn_operands(x_ref, zb_ref, rows), alog_ref[...], dtb_ref[...], nw_ref[...], s)
            for h in range(H):
                o_ref[rows, h * D:(h + 1) * D] = on[h]
        s_ref[...] = s

    par = pl.BlockSpec((H, 1, 1), lambda n: (0, 0, 0))
    outs, gathered, _ = _pcall(
        body, (qkvc, proj, alog, dtb, nw), name="dn_fwd", grid=(N // G,),
        in_specs=[pl.BlockSpec((G * C, QKV_B), lambda n: (n, 0)),
                  pl.BlockSpec((G * C, DN_ZCOLS), lambda n: (n, DN_ZBLK)),
                  par, par, pl.BlockSpec((1, 1, D), lambda n: (0, 0, 0))],
        out_specs=[pl.BlockSpec((G * C, V_B), lambda n: (n, 0)), pl.BlockSpec((G, H, D, D), lambda n: (n, 0, 0, 0))],
        out_shape=[jax.ShapeDtypeStruct((T, V_B), F32), jax.ShapeDtypeStruct((N, H, D, D), F32)],
        scratch_shapes=[pltpu.VMEM((H, D, D), F32)], sem=("arbitrary",), ag=ag)
    return outs, gathered


def dn_bwd(qkvc, proj, alog, dtb, nw, sall, dmix, rs=()):
    T = qkvc.shape[0]
    H, C, D, G = DN_HEADS, DN_CHUNK, DN_D, DN_STEP_CHUNKS
    N = T // C // G

    def body(x_ref, zb_ref, alog_ref, dtb_ref, nw_ref, sall_ref, do_ref,
             dx_ref, dzb_ref, dalog_ref, ddtb_ref, dnw_ref, ds_ref):
        @pl.when(pl.program_id(0) == 0)
        def _():
            ds_ref[...] = jnp.zeros_like(ds_ref)
            dalog_ref[...] = jnp.zeros_like(dalog_ref)
            ddtb_ref[...] = jnp.zeros_like(ddtb_ref)
            dnw_ref[...] = jnp.zeros_like(dnw_ref)

        ds = ds_ref[...]
        lane = lax.broadcasted_iota(jnp.int32, (C, LANES), 1)
        for c in reversed(range(G)):
            rows = slice(c * C, (c + 1) * C)
            args = (*_dn_operands(x_ref, zb_ref, rows), alog_ref[...], dtb_ref[...], nw_ref[...], sall_ref[c])
            _, vjp = jax.vjp(_dn_chunk, *args)
            dq, dk, dv, dz, db, da, dalog, ddtb, dnw, ds = vjp((_dn_heads(do_ref[rows, :], 0), ds))
            for h in range(H):
                dx_ref[rows, h * D:(h + 1) * D] = dq[h]
                dx_ref[rows, V_B + h * D:V_B + (h + 1) * D] = dk[h]
                dx_ref[rows, 2 * V_B + h * D:2 * V_B + (h + 1) * D] = dv[h]
                dzb_ref[rows, h * D:(h + 1) * D] = dz[h]
            tail = jnp.zeros((C, LANES), F32)
            for h in range(H):
                tail = tail + jnp.where(lane == h, jnp.broadcast_to(db[h], (C, LANES)), 0.0)
                tail = tail + jnp.where(lane == H + h, jnp.broadcast_to(da[h], (C, LANES)), 0.0)
            dzb_ref[rows, V_B:V_B + LANES] = tail
            dzb_ref[rows, V_B + LANES:] = jnp.zeros((C, DN_ZCOLS - V_B - LANES), F32)
            dalog_ref[...] += dalog
            ddtb_ref[...] += ddtb
            dnw_ref[...] += dnw
        ds_ref[...] = ds

    par = pl.BlockSpec((H, 1, 1), lambda i: (0, 0, 0))
    nws = pl.BlockSpec((1, 1, D), lambda i: (0, 0, 0))
    outs, _, slots = _pcall(
        body, (qkvc, proj, alog, dtb, nw, sall, dmix), name="dn_bwd", grid=(N,),
        in_specs=[pl.BlockSpec((G * C, QKV_B), lambda i: (N - 1 - i, 0)),
                  pl.BlockSpec((G * C, DN_ZCOLS), lambda i: (N - 1 - i, DN_ZBLK)), par, par, nws,
                  pl.BlockSpec((G, H, D, D), lambda i: (N - 1 - i, 0, 0, 0)),
                  pl.BlockSpec((G * C, V_B), lambda i: (N - 1 - i, 1))],
        out_specs=[pl.BlockSpec((G * C, QKV_B), lambda i: (N - 1 - i, 0)),
                   pl.BlockSpec((G * C, DN_ZCOLS), lambda i: (N - 1 - i, 0)), par, par, nws],
        out_shape=[jax.ShapeDtypeStruct((T, QKV_B), F32), jax.ShapeDtypeStruct((T, DN_ZCOLS), F32)]
        + [jax.ShapeDtypeStruct((H, 1, 1), F32)] * 2 + [jax.ShapeDtypeStruct((1, 1, D), F32)],
        scratch_shapes=[pltpu.VMEM((H, D, D), F32)], sem=("arbitrary",), rs=rs)
    return outs, slots


def _conv_taps(buf_ref, w, width, halo, tm):
    acc = None
    for kk, win in _windows(buf_ref, [halo - (width - 1) + kk for kk in range(width)], tm):
        term = w[kk:kk + 1, :] * win
        acc = term if acc is None else acc + term
    return acc


def _windows(ref, offsets, tm):
    for res in range(SUBLANES):
        ks = [k for k, o in enumerate(offsets) if o % SUBLANES == res]
        if not ks:
            continue
        lo = min(offsets[k] for k in ks)
        hi = max(offsets[k] for k in ks)
        shifted = ref[pl.ds(lo, tm + hi - lo), :]
        for k in ks:
            yield k, shifted[offsets[k] - lo:offsets[k] - lo + tm]


def _conv_taps_bwd(dbuf_ref, w, width, tm):
    acc = None
    for kk, win in _windows(dbuf_ref, [width - 1 - kk for kk in range(width)], tm):
        term = w[kk:kk + 1, :] * win
        acc = term if acc is None else acc + term
    return acc


def _conv_dw_acc(dw_ref, dout, buf_ref, width, halo, tm):
    for kk, win in _windows(buf_ref, [halo - (width - 1) + kk for kk in range(width)], tm):
        dw_ref[pl.ds(kk, 1), :] += jnp.sum(dout * win, axis=0, keepdims=True)


DNC_HALO = 8
DNC_COLS = 768


def dnconv_fwd(proj, w):
    T = proj.shape[0]
    tm = min(T, 256)
    hb = tm // DNC_HALO

    def body(x_ref, h_ref, w_ref, o_ref, buf_ref):
        i = pl.program_id(0)
        buf_ref[0:DNC_HALO, :] = jnp.where(i > 0, h_ref[...], 0.0)
        buf_ref[DNC_HALO:, :] = x_ref[...]
        acc = _conv_taps(buf_ref, w_ref[...], DN_CONV, DNC_HALO, tm)
        o_ref[...] = acc * _sigmoid(acc)

    return pl.pallas_call(
        body, name="dnconv_fwd", grid=(T // tm, 2),
        in_specs=[pl.BlockSpec((tm, DNC_COLS), lambda i, c: (i, 1 + c)),
                  pl.BlockSpec((DNC_HALO, DNC_COLS), lambda i, c: (jnp.maximum(i * hb - 1, 0), 1 + c)),
                  pl.BlockSpec((DN_CONV, DNC_COLS), lambda i, c: (0, c))],
        out_specs=pl.BlockSpec((tm, DNC_COLS), lambda i, c: (i, c)),
        out_shape=jax.ShapeDtypeStruct((T, QKV_B), F32),
        scratch_shapes=[pltpu.VMEM((DNC_HALO + tm, DNC_COLS), F32)],
        compiler_params=_cparams(("parallel", "parallel")),
    )(proj, proj, w)


def dnconv_bwd(proj, w, dout):
    T = proj.shape[0]
    tm = min(T, 256)
    nt = T // tm
    hb = tm // DNC_HALO

    def body(x_ref, h_ref, w_ref, do_ref, dx_ref, dw_ref, buf_ref, dbuf_ref):
        r = pl.program_id(1)
        i = nt - 1 - r

        @pl.when(r == 0)
        def _():
            dw_ref[...] = jnp.zeros_like(dw_ref)
            dbuf_ref[tm:, :] = jnp.zeros((DNC_HALO, DNC_COLS), F32)

        buf_ref[0:DNC_HALO, :] = jnp.where(i > 0, h_ref[...], 0.0)
        buf_ref[DNC_HALO:, :] = x_ref[...]
        wv = w_ref[...]
        acc = _conv_taps(buf_ref, wv, DN_CONV, DNC_HALO, tm)
        sg = _sigmoid(acc)
        dacc = do_ref[...] * (sg * (1.0 + acc * (1.0 - sg)))
        dbuf_ref[0:tm, :] = dacc
        dx_ref[...] = _conv_taps_bwd(dbuf_ref, wv, DN_CONV, tm)
        _conv_dw_acc(dw_ref, dacc, buf_ref, DN_CONV, DNC_HALO, tm)
        dbuf_ref[tm:, :] = dacc[0:DNC_HALO, :]

    return pl.pallas_call(
        body, name="dnconv_bwd", grid=(2, nt),
        in_specs=[pl.BlockSpec((tm, DNC_COLS), lambda c, r: (nt - 1 - r, 1 + c)),
                  pl.BlockSpec((DNC_HALO, DNC_COLS), lambda c, r: (jnp.maximum((nt - 1 - r) * hb - 1, 0), 1 + c)),
                  pl.BlockSpec((DN_CONV, DNC_COLS), lambda c, r: (0, c)),
                  pl.BlockSpec((tm, DNC_COLS), lambda c, r: (nt - 1 - r, c))],
        out_specs=[pl.BlockSpec((tm, DNC_COLS), lambda c, r: (nt - 1 - r, c)),
                   pl.BlockSpec((DN_CONV, DNC_COLS), lambda c, r: (0, c))],
        out_shape=[jax.ShapeDtypeStruct((T, QKV_B), F32), jax.ShapeDtypeStruct((DN_CONV, QKV_B), F32)],
        scratch_shapes=[pltpu.VMEM((DNC_HALO + tm, DNC_COLS), F32), pltpu.VMEM((tm + DNC_HALO, DNC_COLS), F32)],
        compiler_params=_cparams(("parallel", "arbitrary")),
    )(proj, proj, w, dout)


CV_HALO = 32


def _cv_post(cv, lnw, lnb):
    mu = jnp.mean(cv, axis=-1, keepdims=True)
    xc = cv - mu
    y = xc * lax.rsqrt(jnp.mean(xc * xc, axis=-1, keepdims=True) + EPS) * lnw + lnb
    return y * _sigmoid(y)


def cv_fwd(ab, w, bdw, lnw, lnb, ag=()):
    T = ab.shape[0]
    D = ab.shape[1] // 2
    tm = min(T, 256)
    hb = tm // CV_HALO

    def body(a_ref, b_ref, ah_ref, bh_ref, w_ref, bdw_ref, lnw_ref, lnb_ref, o_ref, cv_ref, buf_ref):
        i = pl.program_id(0)
        buf_ref[0:CV_HALO, :] = jnp.where(i > 0, ah_ref[...] * _sigmoid(bh_ref[...]), 0.0)
        buf_ref[CV_HALO:, :] = a_ref[...] * _sigmoid(b_ref[...])
        cv = _conv_taps(buf_ref, w_ref[...], CONV_WIDTH, CV_HALO, tm) + bdw_ref[...]
        cv_ref[...] = cv
        o_ref[...] = _cv_post(cv, lnw_ref[...], lnb_ref[...])

    halo = lambda c: pl.BlockSpec((CV_HALO, D), lambda i: (jnp.maximum(i * hb - 1, 0), c))
    vec = pl.BlockSpec((1, D), lambda i: (0, 0))
    tile = pl.BlockSpec((tm, D), lambda i: (i, 0))
    outs, gathered, _ = _pcall(
        body, (ab, ab, ab, ab, w, bdw, lnw, lnb), name="cv_fwd", grid=(T // tm,),
        in_specs=[tile, pl.BlockSpec((tm, D), lambda i: (i, 1)),
                  halo(0), halo(1), pl.BlockSpec((CONV_WIDTH, D), lambda i: (0, 0)), vec, vec, vec],
        out_specs=[tile, tile],
        out_shape=[jax.ShapeDtypeStruct((T, D), F32), jax.ShapeDtypeStruct((T, D), F32)],
        scratch_shapes=[pltpu.VMEM((CV_HALO + tm, D), F32)], sem=("arbitrary",), ag=ag)
    return outs, gathered


def cv_bwd(ab, cv, w, lnw, lnb, dout, rs=()):
    T = ab.shape[0]
    D = ab.shape[1] // 2
    tm = min(T, 256)
    nt = T // tm
    hb = tm // CV_HALO

    def body(a_ref, b_ref, ah_ref, bh_ref, cv_ref, w_ref, lnw_ref, lnb_ref, do_ref,
             da_ref, db_ref, dw_ref, dbdw_ref, dlnw_ref, dlnb_ref, buf_ref, dbuf_ref):
        r = pl.program_id(0)
        i = nt - 1 - r

        @pl.when(r == 0)
        def _():
            dw_ref[...] = jnp.zeros_like(dw_ref)
            dbdw_ref[...] = jnp.zeros_like(dbdw_ref)
            dlnw_ref[...] = jnp.zeros_like(dlnw_ref)
            dlnb_ref[...] = jnp.zeros_like(dlnb_ref)
            dbuf_ref[tm:, :] = jnp.zeros((CV_HALO, D), F32)

        a = a_ref[...]
        sb = _sigmoid(b_ref[...])
        buf_ref[0:CV_HALO, :] = jnp.where(i > 0, ah_ref[...] * _sigmoid(bh_ref[...]), 0.0)
        buf_ref[CV_HALO:, :] = a * sb
        wv = w_ref[...]
        _, vjp = jax.vjp(_cv_post, cv_ref[...], lnw_ref[...], lnb_ref[...])
        dcv, dlnw, dlnb = vjp(do_ref[...])
        dlnw_ref[...] += dlnw
        dlnb_ref[...] += dlnb
        dbdw_ref[...] += jnp.sum(dcv, axis=0, keepdims=True)
        dbuf_ref[0:tm, :] = dcv
        du = _conv_taps_bwd(dbuf_ref, wv, CONV_WIDTH, tm)
        _conv_dw_acc(dw_ref, dcv, buf_ref, CONV_WIDTH, CV_HALO, tm)
        dbuf_ref[tm:, :] = dcv[0:CV_HALO, :]
        da_ref[...] = du * sb
        db_ref[...] = du * a * sb * (1.0 - sb)

    tile = lambda c: pl.BlockSpec((tm, D), lambda r: (nt - 1 - r, c))
    halo = lambda c: pl.BlockSpec((CV_HALO, D), lambda r: (jnp.maximum((nt - 1 - r) * hb - 1, 0), c))
    vec = pl.BlockSpec((1, D), lambda r: (0, 0))
    wsp = pl.BlockSpec((CONV_WIDTH, D), lambda r: (0, 0))
    (da, db, dw, dbdw, dlnw, dlnb), _, slots = _pcall(
        body, (ab, ab, ab, ab, cv, w, lnw, lnb, dout), name="cv_bwd", grid=(nt,),
        in_specs=[tile(0), tile(1), halo(0), halo(1), tile(0), wsp, vec, vec, tile(0)],
        out_specs=[tile(0), tile(0), wsp, vec, vec, vec],
        out_shape=[jax.ShapeDtypeStruct((T, D), F32), jax.ShapeDtypeStruct((T, D), F32),
                   jax.ShapeDtypeStruct((CONV_WIDTH, D), F32)] + [jax.ShapeDtypeStruct((1, D), F32)] * 3,
        scratch_shapes=[pltpu.VMEM((CV_HALO + tm, D), F32), pltpu.VMEM((tm + CV_HALO, D), F32)],
        sem=("arbitrary",), rs=rs)
    return (jnp.concatenate([da, db], axis=1), dw, dbdw, dlnw, dlnb), slots


def adamw(w, m, v, slots, rs=()):
    L, R, C = w.shape
    fits = lambda r, c: N_DEV * r * c * 2 <= ADAM_SLOT_BLOCK
    tiles = [(R, C)] if fits(R, C) else []
    tiles += [(d, C) for d in range(16, R, 16) if R % d == 0 and fits(d, C)]
    tiles += [(R, d) for d in range(LANES, C, LANES) if C % d == 0 and fits(R, d)]
    tr, tc = max(tiles, key=lambda t: t[0] * t[1])
    c1 =1.0 / (1.0 - ADAM_B1 ** ADAM_STEP)
    c2 = 1.0 / (1.0 - ADAM_B2 ** ADAM_STEP)

    def body(w_ref, m_ref, v_ref, *rest):
        s_refs = rest[:L]
        g_ref, d_ref, nm_ref, nv_ref = rest[L:]
        l = pl.program_id(0)
        for k in range(L):
            @pl.when(l == k)
            def _(s_ref=s_refs[k]):
                g = s_ref[0].astype(F32)
                for j in range(1, N_DEV):
                    g = g + s_ref[j].astype(F32)
                nm = ADAM_B1 * m_ref[0] + (1.0 - ADAM_B1) * g
                nv = ADAM_B2 * v_ref[0] + (1.0 - ADAM_B2) * (g * g)
                g_ref[0] = g
                nm_ref[0] = nm
                nv_ref[0] = nv
                d_ref[0] = -ADAM_LR * ((nm * c1) / (jnp.sqrt(nv * c2) + ADAM_EPS) + ADAM_WD * w_ref[0])

    nc = C // tc
    blk = pl.BlockSpec((1, tr, tc), lambda l, i: (l, i // nc, i % nc))
    slot = lambda k: pl.BlockSpec((N_DEV, tr, tc), lambda l, i: (0, jnp.where(l == k, i // nc, 0),
                                                                 jnp.where(l == k, i % nc, 0)))
    outs, _, landed = _pcall(
        body, (w, m, v, *slots), name="adamw", grid=(L, (R // tr) * nc),
        in_specs=[blk, blk, blk] + [slot(k) for k in range(L)],
        out_specs=[blk, blk, blk, blk],
        out_shape=[jax.ShapeDtypeStruct((L, R, C), F32)] * 4,
        sem=("arbitrary", "arbitrary"), rs=rs)
    return outs, landed


def _unshard(g, axis):
    g = jnp.moveaxis(g, 0, axis)
    s = g.shape
    return g.reshape(s[:axis] + (s[axis] * s[axis + 1],) + s[axis + 2:])


def _to_blocks(full, axis):
    s = full.shape
    g = full.reshape(s[:axis] + (N_DEV, s[axis] // N_DEV) + s[axis + 1:])
    return jnp.moveaxis(g, axis, 0)


def _heads(a, h):
    T = a.shape[0]
    return a.reshape(T, h, a.shape[1] // h).transpose(1, 0, 2)


def _unheads(a):
    h, T, d = a.shape
    return a.transpose(1, 0, 2).reshape(T, h * d)


SMALL = (("norm_w", 2), ("dn_conv_w", 2), ("conv_b_pw1", 1), ("conv_w_dw", 2), ("conv_b_dw", 1),
         ("conv_ln_w", 1), ("conv_ln_b", 1), ("conv_b_pw2", 1),
         ("attn_sinks", None), ("dn_a_log", None), ("dn_dt_bias", None), ("dn_norm_w", None), ("final_norm_w", None))

def _pack(parts):
    flat = jnp.concatenate([p.reshape(-1) for p in parts])
    pad = (-flat.shape[0]) % LANES
    return jnp.pad(flat, (0, pad))


def _unpack(flat, shapes):
    out, off = [], 0
    for s in shapes:
        n = int(np.prod(s))
        out.append(flat[off:off + n].reshape(s))
        off += n
    return out


def kernel(x, norm_w, ffn_w_gate, ffn_w_up, ffn_w_down, mix_w_in, dn_conv_w, attn_sinks, dn_a_log, dn_dt_bias, dn_norm_w, mix_w_out, conv_w_pw1, conv_b_pw1, conv_w_dw, conv_b_dw, conv_ln_w, conv_ln_b, conv_w_pw2, conv_b_pw2, final_norm_w, loss_target, m_norm_w, m_ffn_w_gate, m_ffn_w_up, m_ffn_w_down, m_mix_w_in, m_dn_conv_w, m_attn_sinks, m_dn_a_log, m_dn_dt_bias, m_dn_norm_w, m_mix_w_out, m_conv_w_pw1, m_conv_b_pw1, m_conv_w_dw, m_conv_b_dw, m_conv_ln_w, m_conv_ln_b, m_conv_w_pw2, m_conv_b_pw2, m_final_norm_w, v_norm_w, v_ffn_w_gate, v_ffn_w_up, v_ffn_w_down, v_mix_w_in, v_dn_conv_w, v_attn_sinks, v_dn_a_log, v_dn_dt_bias, v_dn_norm_w, v_mix_w_out, v_conv_w_pw1, v_conv_b_pw1, v_conv_w_dw, v_conv_b_dw, v_conv_ln_w, v_conv_ln_b, v_conv_w_pw2, v_conv_b_pw2, v_final_norm_w):
    W = dict(norm_w=norm_w, ffn_w_gate=ffn_w_gate, ffn_w_up=ffn_w_up, ffn_w_down=ffn_w_down, mix_w_in=mix_w_in,
             dn_conv_w=dn_conv_w, attn_sinks=attn_sinks, dn_a_log=dn_a_log, dn_dt_bias=dn_dt_bias,
             dn_norm_w=dn_norm_w, mix_w_out=mix_w_out, conv_w_pw1=conv_w_pw1, conv_b_pw1=conv_b_pw1,
             conv_w_dw=conv_w_dw, conv_b_dw=conv_b_dw, conv_ln_w=conv_ln_w, conv_ln_b=conv_ln_b,
             conv_w_pw2=conv_w_pw2, conv_b_pw2=conv_b_pw2, final_norm_w=final_norm_w)
    M = dict(norm_w=m_norm_w, ffn_w_gate=m_ffn_w_gate, ffn_w_up=m_ffn_w_up, ffn_w_down=m_ffn_w_down,
             mix_w_in=m_mix_w_in, dn_conv_w=m_dn_conv_w, attn_sinks=m_attn_sinks, dn_a_log=m_dn_a_log,
             dn_dt_bias=m_dn_dt_bias, dn_norm_w=m_dn_norm_w, mix_w_out=m_mix_w_out, conv_w_pw1=m_conv_w_pw1,
             conv_b_pw1=m_conv_b_pw1, conv_w_dw=m_conv_w_dw, conv_b_dw=m_conv_b_dw, conv_ln_w=m_conv_ln_w,
             conv_ln_b=m_conv_ln_b, conv_w_pw2=m_conv_w_pw2, conv_b_pw2=m_conv_b_pw2, final_norm_w=m_final_norm_w)
    V = dict(norm_w=v_norm_w, ffn_w_gate=v_ffn_w_gate, ffn_w_up=v_ffn_w_up, ffn_w_down=v_ffn_w_down,
             mix_w_in=v_mix_w_in, dn_conv_w=v_dn_conv_w, attn_sinks=v_attn_sinks, dn_a_log=v_dn_a_log,
             dn_dt_bias=v_dn_dt_bias, dn_norm_w=v_dn_norm_w, mix_w_out=v_mix_w_out, conv_w_pw1=v_conv_w_pw1,
             conv_b_pw1=v_conv_b_pw1, conv_w_dw=v_conv_w_dw, conv_b_dw=v_conv_b_dw, conv_ln_w=v_conv_ln_w,
             conv_ln_b=v_conv_ln_b, conv_w_pw2=v_conv_w_pw2, conv_b_pw2=v_conv_b_pw2, final_norm_w=v_final_norm_w)

    T, D = x.shape[1], x.shape[2]
    xs = x[0]
    F8 = ffn_w_gate.shape[-1]
    n_ffn = DEPTH * 2

    big = ("ffn_w_gate", "ffn_w_up", "ffn_w_down", "mix_w_in", "mix_w_out", "conv_w_pw1", "conv_w_pw2")
    shard3 = {k: W[k].reshape((-1,) + W[k].shape[-2:]) for k in big}
    shard_bf = {k: shard3[k].astype(BF16) for k in big}
    ffn_unit = lambda i: [("ffn_w_gate", i), ("ffn_w_up", i), ("ffn_w_down", i)]
    even_unit = lambda e: [("mix_w_in", e), ("mix_w_out", e)]
    odd_unit = lambda e: [("conv_w_pw1", e), ("conv_w_pw2", e)]
    have = {}

    def ag_jobs(units):
        return [(shard_bf[k], i) for k, i in units]

    def ag_done(units, gathered):
        have.update(zip(units, gathered))

    small_sharded = [(k, ax) for k, ax in SMALL if ax is not None]
    small_pack = _pack([W[k] for k, _ in small_sharded])[None, :]
    first_units = ffn_unit(0)
    gathered, _ = exchange(ag=ag_jobs(first_units) + [(small_pack, None)])
    ag_done(first_units, gathered[:-1])
    small_full = {}
    for (k, ax), parts in zip(small_sharded,
                              zip(*[_unpack(gathered[-1][s, 0], [W[k].shape for k, _ in small_sharded])
                                    for s in range(N_DEV)])):
        small_full[k] = _unshard(jnp.stack(parts), ax)
    nw_full = small_full["norm_w"]

    ffn_w = lambda i: [have[u] for u in ffn_unit(i)]
    w_in_of = lambda e: jnp.pad(_unshard(have[("mix_w_in", e)], 1), ((0, 0), (0, IN_COLS_PAD - IN_COLS)))
    w_out_of = lambda e: have[("mix_w_out", e)].reshape(D, D)
    w_pw1_of = lambda e: _unshard(have[("conv_w_pw1", e)], 1)
    w_pw2_of = lambda e: have[("conv_w_pw2", e)].reshape(D, D)
    fwd_order, needed = [], {}
    for l in range(DEPTH):
        mixer = [("A", l), ("E", l)] if l % 2 == 0 else [("O", l)]
        fwd_order += [("F", 2 * l)] + mixer + [("F", 2 * l + 1)]
        needed[("F", 2 * l)], needed[("F", 2 * l + 1)] = ffn_unit(2 * l), ffn_unit(2 * l + 1)
        needed[mixer[0]] = even_unit(l // 2) if l % 2 == 0 else odd_unit(l // 2)
    queue = [(u, pos) for pos, key in enumerate(fwd_order) for u in needed.get(key, []) if u not in first_units]
    unit_bytes = lambda u: N_DEV * shard_bf[u[0]][u[1]].size * 2
    fwd_carry, at = {}, 0
    for pos, key in enumerate(fwd_order):
        cap = FWD_CARRY_BYTES[key[0]]
        taken, used = [], 0
        while at < len(queue) and (queue[at][1] <= pos + 1 or used + unit_bytes(queue[at][0]) <= cap):
            taken.append(queue[at][0])
            used += unit_bytes(queue[at][0])
            at += 1
        fwd_carry[key] = taken
    zero_in = jnp.zeros((1, IN_COLS_PAD), F32)
    zero_d = jnp.zeros((1, D), F32)
    slope_rows = jnp.asarray(np.repeat(2.0 ** (-8.0 * np.arange(1, ATTN_HEADS + 1) / ATTN_HEADS), ATTN_BLOCK)
                             .astype(np.float32)[:, None])

    saved = []
    h = xs
    w_in, w_out, w_pw1, w_pw2 = {}, {}, {}, {}

    def ffn_forward(h, l, half):
        i = 2 * l + half
        units = fwd_carry.get(("F", i), [])
        h, gathered = ffn_fwd(h, nw_full[l, 2 * half][None], *ffn_w(i), ag=ag_jobs(units))
        ag_done(units, gathered)
        return h

    for l in range(DEPTH):
        e = l // 2
        st = {"x0": h}
        h = ffn_forward(h, l, 0)
        st["x1"] = h
        if l % 2 == 0:
            w_in[e], w_out[e] = w_in_of(e), w_out_of(e)
            proj = rmslin_fwd(h, nw_full[l, 1][None], w_in[e], zero_in)
            st["proj"] = proj
            st["qkvc"] = dnconv_fwd(proj, small_full["dn_conv_w"][e])
            st["sink_rows"] = jnp.repeat(attn_sinks[e], ATTN_BLOCK)[:, None]
            st["alog"] = dn_a_log[e].reshape(DN_HEADS, 1, 1)
            st["dtb"] = dn_dt_bias[e].reshape(DN_HEADS, 1, 1)
            st["dnw"] = dn_norm_w[e].reshape(1, 1, DN_D)
            units = fwd_carry[("A", l)]
            st["att"], gathered = attn_fwd(proj, st["sink_rows"], slope_rows, ag=ag_jobs(units))
            ag_done(units, gathered)
            units = fwd_carry[("E", l)]
            (st["og"], st["sall"]), gathered = dn_fwd(st["qkvc"], proj, st["alog"], st["dtb"], st["dnw"],
                                                      ag=ag_jobs(units))
            ag_done(units, gathered)
            h = lin_fwd(h, [st["att"], st["og"]], w_out[e], zero_d)
        else:
            units = fwd_carry[("O", l)]
            w_pw1[e], w_pw2[e] = w_pw1_of(e), w_pw2_of(e)
            st["ab"] = rmslin_fwd(h, nw_full[l, 1][None], w_pw1[e], small_full["conv_b_pw1"][e][None])
            (st["act"], st["cv"]), gathered = cv_fwd(st["ab"], small_full["conv_w_dw"][e], small_full["conv_b_dw"][e][None],
                                         small_full["conv_ln_w"][e][None], small_full["conv_ln_b"][e][None],
                                         ag=ag_jobs(units))
            ag_done(units, gathered)
            h = lin_fwd(h, [st["act"]], w_pw2[e], small_full["conv_b_pw2"][e][None])
        st["x2"] = h
        h = ffn_forward(h, l, 1)
        saved.append(st)

    loss_part, dh, dfinal = loss_fwd_bwd(h, final_norm_w[None], loss_target[0])
    loss = lax.psum(loss_part[0, 0], ("x", "y", "c"))

    d_norm = [[None] * 3 for _ in range(DEPTH)]
    d_small = {k: [None, None] for k in ("dn_conv_w", "conv_b_pw1", "conv_w_dw", "conv_b_dw", "conv_ln_w",
                                         "conv_ln_b", "conv_b_pw2", "attn_sinks", "dn_a_log", "dn_dt_bias",
                                         "dn_norm_w")}
    pending, slot = [], {}

    def take_pending(cap=None):
        n, used = 0, 0
        while n < len(pending) and (cap is None or used + pending[n][1].size * 2 <= cap):
            used += pending[n][1].size * 2
            n += 1
        units = pending[:n]
        del pending[:n]
        return [u for u, _ in units], [b for _, b in units]

    def ffn_backward(dh, l, half):
        i = 2 * l + half
        units, blocks = take_pending(BWD_CARRY_BYTES["F"])
        (dh, dg, du, dd, d_norm[l][2 * half]), slots = ffn_bwd(
            st["x2" if half else "x0"], dh, nw_full[l, 2 * half][None], *ffn_w(i), rs=blocks)
        slot.update(zip(units, slots))
        pending.extend(zip(ffn_unit(i), (dg, du, dd)))
        return dh

    for l in reversed(range(DEPTH)):
        e = l // 2
        st = saved[l]
        dh = ffn_backward(dh, l, 1)
        if l % 2 == 0:
            dmix, d_out, _ = lin_bwd([st["att"], st["og"]], dh, w_out[e])
            pending.append((("mix_w_out", e), d_out.reshape(N_DEV, D // N_DEV, D).astype(BF16)))
            units, blocks = take_pending(BWD_CARRY_BYTES["E"])
            (dqkvc, dzba, dalog, ddtb, ddnw), slots = dn_bwd(
                st["qkvc"], st["proj"], st["alog"], st["dtb"], st["dnw"], st["sall"], dmix, rs=blocks)
            slot.update(zip(units, slots))
            units, blocks = take_pending(BWD_CARRY_BYTES["A"])
            (dqa, dkva, dsink), slots = attn_bwd(st["proj"], st["sink_rows"], slope_rows, dmix, rs=blocks)
            slot.update(zip(units, slots))
            dqkv, d_small["dn_conv_w"][e] = dnconv_bwd(st["proj"], small_full["dn_conv_w"][e], dqkvc)
            dproj = jnp.concatenate([dqa, dkva, dqkv, dzba], axis=1)
            dh, d_in, _, d_norm[l][1] = rmslin_bwd(st["x1"], dh, dproj, nw_full[l, 1][None], w_in[e])
            pending.append((("mix_w_in", e), _to_blocks(d_in[:, :IN_COLS], 1).astype(BF16)))
            d_small["attn_sinks"][e] = jnp.sum(dsink.reshape(ATTN_HEADS, ATTN_BLOCK), axis=1)
            d_small["dn_a_log"][e] = dalog.reshape(DN_HEADS)
            d_small["dn_dt_bias"][e] = ddtb.reshape(DN_HEADS)
            d_small["dn_norm_w"][e] = ddnw.reshape(DN_D)
        else:
            dact, d_pw2, d_small["conv_b_pw2"][e] = lin_bwd([st["act"]], dh, w_pw2[e])
            pending.append((("conv_w_pw2", e), d_pw2.reshape(N_DEV, D // N_DEV, D).astype(BF16)))
            units, blocks = take_pending(BWD_CARRY_BYTES["O"])
            (dab, d_small["conv_w_dw"][e], d_small["conv_b_dw"][e], d_small["conv_ln_w"][e],
             d_small["conv_ln_b"][e]), slots = cv_bwd(
                st["ab"], st["cv"], small_full["conv_w_dw"][e],
                small_full["conv_ln_w"][e][None], small_full["conv_ln_b"][e][None], dact, rs=blocks)
            slot.update(zip(units, slots))
            dh, d_pw1, d_small["conv_b_pw1"][e], d_norm[l][1] = rmslin_bwd(
                st["x1"], dh, dab, nw_full[l, 1][None], w_pw1[e])
            pending.append((("conv_w_pw1", e), _to_blocks(d_pw1, 1).astype(BF16)))
        dh = ffn_backward(dh, l, 0)
    grad_x = dh[None]

    full_small = {"norm_w": jnp.stack([jnp.concatenate(r, axis=0) for r in d_norm]),
                  "final_norm_w": dfinal[0]}
    for k, pair in d_small.items():
        full_small[k] = jnp.stack([p.reshape(W[k].shape[1:-1] + (-1,)) if SMALL_AXIS[k] is not None
                                   else p for p in pair])
    rows = []
    for s in range(N_DEV):
        parts = [_to_blocks(full_small[k], ax)[s] if ax is not None else full_small[k] for k, ax in SMALL]
        rows.append(_pack(parts))
    send_small = jnp.stack(rows)[:, None, :]
    pending.append((("small", 0), send_small))

    res = {}
    waiting = lambda k: [(u, b) for u, b in pending if u[0] == k]
    adam_order = sorted(big, key=lambda k: len(waiting(k))) + ["small"]
    for n, k in enumerate(adam_order[:-1]):
        riders = next((waiting(kk) for kk in adam_order[n + 1:] if waiting(kk)), [])
        pending[:] = [p for p in pending if p[0] not in [u for u, _ in riders]]
        outs, slots = adamw(shard3[k], M[k].reshape(shard3[k].shape), V[k].reshape(shard3[k].shape),
                            [slot[(k, i)] for i in range(shard3[k].shape[0])], rs=[b for _, b in riders])
        slot.update(zip([u for u, _ in riders], slots))
        res[k] = [o.reshape(W[k].shape) for o in outs]
    pk = lambda d: _pack([d[k] for k, _ in SMALL])[None, None, :]
    outs, _ = adamw(pk(W), pk(M), pk(V), [slot[("small", 0)]])
    shapes = [W[k].shape for k, _ in SMALL]
    unp = [_unpack(o[0, 0], shapes) for o in outs]
    for i, (k, _) in enumerate(SMALL):
        res[k] = [u[i] for u in unp]

    order = ("norm_w", "ffn_w_gate", "ffn_w_up", "ffn_w_down", "mix_w_in", "dn_conv_w", "attn_sinks", "dn_a_log",
             "dn_dt_bias", "dn_norm_w", "mix_w_out", "conv_w_pw1", "conv_b_pw1", "conv_w_dw", "conv_b_dw",
             "conv_ln_w", "conv_ln_b", "conv_w_pw2", "conv_b_pw2", "final_norm_w")
    return (loss, grad_x, *[res[k][0] for k in order], *[res[k][1] for k in order],
            *[res[k][2] for k in order], *[res[k][3] for k in order])


SMALL_AXIS = dict(SMALL)
```

```python
import functools

import numpy as np
import jax
import jax.numpy as jnp
from jax import lax
from jax.experimental import pallas as pl
from jax.experimental.pallas import tpu as pltpu

F32 = jnp.float32
BF16 = jnp.bfloat16
EPS = 1e-6
N_DEV = 8
N_CHIP = 4
V7X_VMEM_LIMIT = 60 * 2**20
MESH = pl.DeviceIdType.MESH
LANES = 128
SUBLANES = 8

DEPTH = 4
D_MODEL = 1024
ATTN_HEADS, ATTN_KV_HEADS, HEAD_DIM, ATTN_BLOCK = 8, 2, 64, 128
DN_HEADS, DN_D, DN_CHUNK, DN_CONV = 8, 64, 64, 4
CONV_WIDTH = 31
Q_A, KV_A, QKV_B, V_B = 512, 128, 1536, 512
IN_COLS = 2832
IN_COLS_PAD = 3072
OFF_QKVB = Q_A + 2 * KV_A
OFF_Z = OFF_QKVB + QKV_B
OFF_BETA = OFF_Z + V_B
OFF_A = OFF_BETA + DN_HEADS

FWD_CARRY_BYTES = {"F": 12 * 2**20, "A": 6 * 2**20, "E": 18 * 2**20, "O": 12 * 2**20}
BWD_CARRY_BYTES = {"F": 11 * 2**20, "A": 6 * 2**20, "E": 13 * 2**20, "O": 10 * 2**20}

ADAM_SLOT_BLOCK = 3 * 2**19

ADAM_LR, ADAM_B1, ADAM_B2, ADAM_EPS, ADAM_WD, ADAM_STEP = 0.001, 0.9, 0.999, 1e-08, 0.01, 10


def _cparams(sem):
    return pltpu.CompilerParams(dimension_semantics=sem, vmem_limit_bytes=V7X_VMEM_LIMIT)


def _sigmoid(x):
    return 1.0 / (1.0 + jnp.exp(-x))


def _softplus(x):
    return jnp.maximum(x, 0.0) + jnp.log(1.0 + jnp.exp(-jnp.abs(x)))


def _dot(a, b):
    return jnp.dot(a, b, preferred_element_type=F32)


def _dot_nt(a, b):
    return lax.dot_general(a, b, (((1,), (1,)), ((), ())), preferred_element_type=F32)


def _dot_tn(a, b):
    return lax.dot_general(a, b, (((0,), (0,)), ((), ())), preferred_element_type=F32)


def _rms(x, w):
    return x * lax.rsqrt(jnp.mean(x * x, axis=-1, keepdims=True) + EPS) * w


def _rms_bwd(x, w, dxn):
    r = lax.rsqrt(jnp.mean(x * x, axis=-1, keepdims=True) + EPS)
    xh = x * r
    dxh = dxn * w
    dx = r * (dxh - xh * jnp.mean(dxh * xh, axis=-1, keepdims=True))
    return dx, jnp.sum(dxn * xh, axis=0, keepdims=True)


def _position():
    return lax.axis_index("x"), lax.axis_index("y"), lax.axis_index("c")


def _dev_index(px, py, pc):
    return 4 * px + 2 * py + pc


def _rcopy(src, dst, send_sem, recv_sem, to):
    return pltpu.make_async_remote_copy(src_ref=src, dst_ref=dst, send_sem=send_sem, recv_sem=recv_sem,
                                        device_id=to, device_id_type=MESH)


def _ag_start(srcs, outs, send, recv, local):
    x, y, c = _position()
    me = _dev_index(x, y, c)
    chips = [(1 - x, y), (x, 1 - y), (1 - x, 1 - y)]
    for a, (src, out) in enumerate(zip(srcs, outs)):
        pltpu.make_async_copy(src, out.at[me], local.at[a]).start()
        _rcopy(src, out.at[me], send.at[a, 0], recv.at[a, 0], (x, y, 1 - c)).start()
        for j, chip in enumerate(chips):
            _rcopy(src, out.at[me], send.at[a, 1 + j], recv.at[a, 1 + j], (*chip, c)).start()


def _ag_finish(srcs, outs, send, recv, local):
    x, y, c = _position()
    me = _dev_index(x, y, c)
    sibling = (x, y, 1 - c)
    chips = [(1 - x, y), (x, 1 - y), (1 - x, 1 - y)]
    for j, chip in enumerate(chips):
        for a, out in enumerate(outs):
            blk = out.at[_dev_index(*chip, c)]
            _rcopy(blk, blk, send.at[a, 1 + j], recv.at[a, 1 + j], (x, y, c)).wait_recv()
            _rcopy(blk, blk, send.at[a, 4 + j], recv.at[a, 4 + j], sibling).start()
    for a, (src, out) in enumerate(zip(srcs, outs)):
        blk = out.at[_dev_index(x, y, 1 - c)]
        _rcopy(blk, blk, send.at[a, 0], recv.at[a, 0], (x, y, c)).wait_recv()
        for j, chip in enumerate(chips):
            blk = out.at[_dev_index(*chip, 1 - c)]
            _rcopy(blk, blk, send.at[a, 4 + j], recv.at[a, 4 + j], (x, y, c)).wait_recv()
        for k in range(N_DEV - 1):
            _rcopy(out.at[me], out.at[me], send.at[a, k], recv.at[a, k], (x, y, c)).wait_send()
        pltpu.make_async_copy(src, out.at[me], local.at[a]).wait()


def _rs_peer(r):
    x, y, c = _position()
    return x ^ ((r >> 2) & 1), y ^ ((r >> 1) & 1), c ^ (r & 1)


def _rs_start(ins, outs, send, recv, local):
    me = _dev_index(*_position())
    for a, (src, out) in enumerate(zip(ins, outs)):
        pltpu.make_async_copy(src.at[me], out.at[me], local.at[a]).start()
        for r in range(1, N_DEV):
            p = _rs_peer(r)
            _rcopy(src.at[_dev_index(*p)], out.at[me], send.at[a, r - 1], recv.at[a, r - 1], p).start()


def _rs_finish(ins, outs, send, recv, local):
    pos = _position()
    me = _dev_index(*pos)
    for a, (src, out) in enumerate(zip(ins, outs)):
        for r in range(1, N_DEV):
            blk = out.at[_dev_index(*_rs_peer(r))]
            _rcopy(blk, blk, send.at[a, r - 1], recv.at[a, r - 1], pos).wait_recv()
        for r in range(1, N_DEV):
            _rcopy(src.at[me], out.at[me], send.at[a, r - 1], recv.at[a, r - 1], pos).wait_send()
        pltpu.make_async_copy(src.at[me], out.at[me], local.at[a]).wait()


def _sw_start(ins, outs, send, recv):
    x, y, c = _position()
    for a, (src, out) in enumerate(zip(ins, outs)):
        for q in range(N_CHIP):
            _rcopy(src.at[2 * q + (1 - c)], out.at[q], send.at[a, q], recv.at[a, q], (x, y, 1 - c)).start()


def _sw_finish(ins, outs, send, recv):
    pos = _position()
    for a, out in enumerate(outs):
        for q in range(N_CHIP):
            _rcopy(out.at[q], out.at[q], send.at[a, q], recv.at[a, q], pos).wait_recv()
        for q in range(N_CHIP):
            _rcopy(out.at[q], out.at[q], send.at[a, q], recv.at[a, q], pos).wait_send()


def _r4_peer(r):
    x, y, c = _position()
    return x ^ ((r >> 1) & 1), y ^ (r & 1), c


def _r4_start(ins, outs, send, recv, local):
    x, y, c = _position()
    mine = 2 * x + y
    for a, (src, out) in enumerate(zip(ins, outs)):
        pltpu.make_async_copy(src.at[mine], out.at[mine], local.at[a]).start()
        for r in range(1, N_CHIP):
            px, py, pc = _r4_peer(r)
            _rcopy(src.at[2 * px + py], out.at[mine], send.at[a, r - 1], recv.at[a, r - 1], (px, py, pc)).start()


def _r4_finish(ins, outs, send, recv, local):
    x, y, c = _position()
    mine = 2 * x + y
    for a, (src, out) in enumerate(zip(ins, outs)):
        for r in range(1, N_CHIP):
            px, py, _ = _r4_peer(r)
            blk = out.at[2 * px + py]
            _rcopy(blk, blk, send.at[a, r - 1], recv.at[a, r - 1], (x, y, c)).wait_recv()
        for r in range(1, N_CHIP):
            _rcopy(src.at[mine], out.at[mine], send.at[a, r - 1], recv.at[a, r - 1], (x, y, c)).wait_send()
        pltpu.make_async_copy(src.at[mine], out.at[mine], local.at[a]).wait()


_RS_KINDS = {
    "direct": (_rs_start, _rs_finish, lambda n: [(n, N_DEV - 1), (n, N_DEV - 1), (n,)], lambda s: s),
    "swap": (_sw_start, _sw_finish, lambda n: [(n, N_CHIP), (n, N_CHIP)], lambda s: (N_CHIP,) + s[1:]),
    "chips": (_r4_start, _r4_finish, lambda n: [(n, N_CHIP - 1), (n, N_CHIP - 1), (n,)], lambda s: s),
}


def _pcall(body, args, *, name, grid, in_specs, out_specs, out_shape, sem, scratch_shapes=(), ag=(), rs=()):
    na, nr = len(ag), len(rs)
    if na + nr == 0:
        outs = pl.pallas_call(body, name=name, grid=grid, in_specs=in_specs, out_specs=out_specs,
                              out_shape=out_shape, scratch_shapes=list(scratch_shapes),
                              compiler_params=_cparams(sem))(*args)
        return list(outs), [], []
    n_in, n_out, n_scr = len(in_specs), len(out_specs), len(scratch_shapes)
    ag_idx = [i for _, i in ag]
    groups = [(k, [i for i, (kk, _) in enumerate(rs) if kk == k]) for k in _RS_KINDS]
    groups = [(k, idx) for k, idx in groups if idx]
    sem_counts = ([3] if na else []) + [len(_RS_KINDS[k][2](1)) for k, _ in groups]

    def wrapped(*refs):
        cin, refs = refs[:n_in], refs[n_in:]
        ag_in, refs = refs[:na], refs[na:]
        rs_in, refs = refs[:nr], refs[nr:]
        cout, refs = refs[:n_out], refs[n_out:]
        ag_out, refs = refs[:na], refs[na:]
        rs_out, refs = refs[:nr], refs[nr:]
        cscr, sems = refs[:n_scr], list(refs[n_scr:])
        sem_sets = [[sems.pop(0) for _ in range(n)] for n in sem_counts]
        ag_sems = sem_sets.pop(0) if na else None
        ag_src = [r if i is None else r.at[i] for r, i in zip(ag_in, ag_idx)]
        ids = [pl.program_id(d) for d in range(len(grid))]
        first = functools.reduce(jnp.logical_and, [i == 0 for i in ids])
        last = functools.reduce(jnp.logical_and, [i == g - 1 for i, g in zip(ids, grid)])

        def run(phase):
            if na:
                (_ag_start, _ag_finish)[phase](ag_src, ag_out, *ag_sems)
            for (k, idx), ss in zip(groups, sem_sets):
                _RS_KINDS[k][phase]([rs_in[i] for i in idx], [rs_out[i] for i in idx], *ss)

        @pl.when(first)
        def _():
            run(0)

        body(*cin, *cout, *cscr)

        @pl.when(last)
        def _():
            run(1)

    hbm = pl.BlockSpec(memory_space=pl.ANY)
    sem_shapes = [pltpu.SemaphoreType.DMA(s) for s in ([(na, N_DEV - 1), (na, N_DEV - 1), (na,)] if na else [])]
    for k, idx in groups:
        sem_shapes += [pltpu.SemaphoreType.DMA(s) for s in _RS_KINDS[k][2](len(idx))]
    outs = pl.pallas_call(
        wrapped, name=name, grid=grid,
        in_specs=list(in_specs) + [hbm] * (na + nr),
        out_specs=list(out_specs) + [hbm] * (na + nr),
        out_shape=list(out_shape)
        + [jax.ShapeDtypeStruct((N_DEV,) + a.shape[-2:], a.dtype) for a, _ in ag]
        + [jax.ShapeDtypeStruct(_RS_KINDS[k][3](b.shape), b.dtype) for k, b in rs],
        scratch_shapes=list(scratch_shapes) + sem_shapes,
        compiler_params=_cparams(sem),
    )(*args, *[a for a, _ in ag], *[b for _, b in rs])
    return list(outs[:n_out]), list(outs[n_out:n_out + na]), list(outs[n_out + na:])


def exchange(ag):
    def body(o_ref):
        o_ref[...] = jnp.zeros_like(o_ref)

    _, gathered, _ = _pcall(body, (), name="exchange", grid=(1,), in_specs=[],
                            out_specs=[pl.BlockSpec((8, LANES), lambda i: (0, 0))],
                            out_shape=[jax.ShapeDtypeStruct((8, LANES), F32)], sem=("arbitrary",), ag=ag)
    return gathered


def pair_add(blocks, received):
    n = len(blocks)

    def body(*refs):
        c = lax.axis_index("c")
        for g_ref, p_ref, o_ref in zip(refs[:n], refs[n:2 * n], refs[2 * n:]):
            mine = jnp.where(c == 0, g_ref[0, 0], g_ref[0, 1])
            o_ref[0] = (mine.astype(F32) + p_ref[0].astype(F32)).astype(BF16)

    halves = 2
    g_specs = [pl.BlockSpec((1, 2, b.shape[1] // halves, b.shape[2]), lambda q, r: (q, 0, r, 0)) for b in blocks]
    p_specs = [pl.BlockSpec((1, b.shape[1] // halves, b.shape[2]), lambda q, r: (q, r, 0)) for b in blocks]
    return pl.pallas_call(
        body, name="pair_add", grid=(N_CHIP, halves),
        in_specs=g_specs + p_specs, out_specs=p_specs,
        out_shape=[jax.ShapeDtypeStruct(p.shape, BF16) for p in received],
        compiler_params=_cparams(("parallel", "parallel")),
    )(*[b.reshape((N_CHIP, 2) + b.shape[1:]) for b in blocks], *received)


FFN_PAIR = 2


def _pair_cols(w_ref):
    return jnp.concatenate([w_ref[p] for p in range(FFN_PAIR)], axis=1)


def ffn_fwd(x, nw, wg, wu, wd, ag=()):
    T, D = x.shape
    F = wg.shape[2]
    P = FFN_PAIR
    J = wg.shape[0] // P
    tm = min(T, 1024)

    def body(x_ref, nw_ref, wg_ref, wu_ref, wd_ref, o_ref, xn_ref, acc_ref):
        j = pl.program_id(1)

        @pl.when(j == 0)
        def _():
            xn_ref[...] = _rms(x_ref[...], nw_ref[...]).astype(BF16)
            acc_ref[...] = jnp.zeros_like(acc_ref)

        xn = xn_ref[...]
        g = _dot(xn, _pair_cols(wg_ref))
        u = _dot(xn, _pair_cols(wu_ref))
        h = (g * _sigmoid(g) * u).astype(BF16)
        acc_ref[...] += _dot(h, wd_ref[...].reshape(P * F, D))

        @pl.when(j == J - 1)
        def _():
            o_ref[...] = x_ref[...] + 0.5 * acc_ref[...]

    (out,), gathered, _ = _pcall(
        body, (x, nw, wg, wu, wd), name="ffn_fwd", grid=(T // tm, J),
        in_specs=[pl.BlockSpec((tm, D), lambda t, j: (t, 0)),
                  pl.BlockSpec((1, D), lambda t, j: (0, 0)),
                  pl.BlockSpec((P, D, F), lambda t, j: (j, 0, 0)),
                  pl.BlockSpec((P, D, F), lambda t, j: (j, 0, 0)),
                  pl.BlockSpec((P, F, D), lambda t, j: (j, 0, 0))],
        out_specs=[pl.BlockSpec((tm, D), lambda t, j: (t, 0))],
        out_shape=[jax.ShapeDtypeStruct((T, D), F32)],
        scratch_shapes=[pltpu.VMEM((tm, D), BF16), pltpu.VMEM((tm, D), F32)],
        sem=("arbitrary", "arbitrary"), ag=ag)
    return out, gathered


def ffn_bwd(x, dy, nw, wg, wu, wd, rs=()):
    T, D = x.shape
    F = wg.shape[2]
    P = FFN_PAIR
    J = wg.shape[0] // P
    tm = min(T, 256)
    nt = T // tm

    def body(x_ref, dy_ref, nw_ref, wg_ref, wu_ref, wd_ref,
             dx_ref, dwg_ref, dwu_ref, dwd_ref, dnw_ref,
             xn_ref, dyh_ref, dxn_ref, awg_ref, awu_ref, awd_ref):
        j = pl.program_id(0)
        t = pl.program_id(1)
        rows = pl.ds(pl.multiple_of(t * tm, tm), tm)

        @pl.when(j == 0)
        def _():
            xn_ref[rows, :] = _rms(x_ref[...], nw_ref[...]).astype(BF16)
            dyh_ref[rows, :] = (0.5 * dy_ref[...]).astype(BF16)
            dxn_ref[rows, :] = jnp.zeros((tm, D), F32)

        @pl.when((j == 0) & (t == 0))
        def _():
            dnw_ref[...] = jnp.zeros_like(dnw_ref)

        @pl.when(t == 0)
        def _():
            awg_ref[...] = jnp.zeros_like(awg_ref)
            awu_ref[...] = jnp.zeros_like(awu_ref)
            awd_ref[...] = jnp.zeros_like(awd_ref)

        xn = xn_ref[rows, :]
        dyh = dyh_ref[rows, :]
        wg2, wu2 = _pair_cols(wg_ref), _pair_cols(wu_ref)
        g = _dot(xn, wg2)
        u = _dot(xn, wu2)
        sg = _sigmoid(g)
        s = g * sg
        h = (s * u).astype(BF16)
        dh = _dot_nt(dyh, wd_ref[...].reshape(P * F, D))
        du = (dh * s).astype(BF16)
        dg = (dh * u * (sg * (1.0 + g * (1.0 - sg)))).astype(BF16)
        awd_ref[...] += _dot_tn(h, dyh)
        awg_ref[...] += _dot_tn(xn, dg)
        awu_ref[...] += _dot_tn(xn, du)
        dxn_ref[rows, :] += _dot_nt(dg, wg2) + _dot_nt(du, wu2)

        @pl.when(t == nt - 1)
        def _():
            for p in range(P):
                dwg_ref[p] = awg_ref[:, p * F:(p + 1) * F].astype(BF16)
                dwu_ref[p] = awu_ref[:, p * F:(p + 1) * F].astype(BF16)
            dwd_ref[...] = awd_ref[...].astype(BF16).reshape(P, F, D)

        @pl.when(j == J - 1)
        def _():
            dx, dnw = _rms_bwd(x_ref[...], nw_ref[...], dxn_ref[rows, :])
            dx_ref[...] = dy_ref[...] + dx
            dnw_ref[...] += dnw

    ends = lambda j, t: (jnp.where((j == 0) | (j == J - 1), t, 0), 0)
    last = lambda j, t: (jnp.where(j == J - 1, t, 0), 0)
    outs, _, slots = _pcall(
        body, (x, dy, nw, wg, wu, wd), name="ffn_bwd", grid=(J, nt),
        in_specs=[pl.BlockSpec((tm, D), ends), pl.BlockSpec((tm, D), ends),
                  pl.BlockSpec((1, D), lambda j, t: (0, 0)),
                  pl.BlockSpec((P, D, F), lambda j, t: (j, 0, 0)),
                  pl.BlockSpec((P, D, F), lambda j, t: (j, 0, 0)),
                  pl.BlockSpec((P, F, D), lambda j, t: (j, 0, 0))],
        out_specs=[pl.BlockSpec((tm, D), last),
                   pl.BlockSpec((P, D, F), lambda j, t: (j, 0, 0)),
                   pl.BlockSpec((P, D, F), lambda j, t: (j, 0, 0)),
                   pl.BlockSpec((P, F, D), lambda j, t: (j, 0, 0)),
                   pl.BlockSpec((1, D), lambda j, t: (0, 0))],
        out_shape=[jax.ShapeDtypeStruct((T, D), F32),
                   jax.ShapeDtypeStruct((P * J, D, F), BF16), jax.ShapeDtypeStruct((P * J, D, F), BF16),
                   jax.ShapeDtypeStruct((P * J, F, D), BF16), jax.ShapeDtypeStruct((1, D), F32)],
        scratch_shapes=[pltpu.VMEM((T, D), BF16), pltpu.VMEM((T, D), BF16), pltpu.VMEM((T, D), F32),
                        pltpu.VMEM((D, P * F), F32), pltpu.VMEM((D, P * F), F32), pltpu.VMEM((P * F, D), F32)],
        sem=("arbitrary", "arbitrary"), rs=rs)
    return outs, slots


def rmslin_fwd(x, nw, w, b):
    T, D = x.shape
    N = w.shape[1]
    tm = min(T, 256)

    def body(x_ref, nw_ref, w_ref, b_ref, o_ref):
        xn = _rms(x_ref[...], nw_ref[...]).astype(BF16)
        o_ref[...] = _dot(xn, w_ref[...]) + b_ref[...]

    return pl.pallas_call(
        body, name="rmslin_fwd", grid=(T // tm,),
        in_specs=[pl.BlockSpec((tm, D), lambda t: (t, 0)), pl.BlockSpec((1, D), lambda t: (0, 0)),
                  pl.BlockSpec((D, N), lambda t: (0, 0)), pl.BlockSpec((1, N), lambda t: (0, 0))],
        out_specs=pl.BlockSpec((tm, N), lambda t: (t, 0)),
        out_shape=jax.ShapeDtypeStruct((T, N), F32),
        compiler_params=_cparams(("parallel",)),
    )(x, nw, w, b)


def rmslin_bwd(x, dres, dproj, nw, w):
    T, D = x.shape
    N = w.shape[1]
    nb = 1024
    nc = N // nb
    tm = min(T, 256)
    nt = T // tm

    def body(x_ref, dres_ref, dp_ref, nw_ref, w_ref, dx_ref, dw_ref, db_ref, dnw_ref, xn_ref, dxn_ref):
        c = pl.program_id(0)
        t = pl.program_id(1)
        rows = pl.ds(pl.multiple_of(t * tm, tm), tm)

        @pl.when(c == 0)
        def _():
            xn_ref[rows, :] = _rms(x_ref[...], nw_ref[...]).astype(BF16)
            dxn_ref[rows, :] = jnp.zeros((tm, D), F32)

        @pl.when((c == 0) & (t == 0))
        def _():
            dnw_ref[...] = jnp.zeros_like(dnw_ref)

        @pl.when(t == 0)
        def _():
            dw_ref[...] = jnp.zeros_like(dw_ref)
            db_ref[...] = jnp.zeros_like(db_ref)

        dpf = dp_ref[...]
        dp = dpf.astype(BF16)
        dw_ref[...] += _dot_tn(xn_ref[rows, :], dp)
        db_ref[...] += jnp.sum(dpf, axis=0, keepdims=True)
        dxn_ref[rows, :] += _dot_nt(dp, w_ref[...])

        @pl.when(c == nc - 1)
        def _():
            dx, dnw = _rms_bwd(x_ref[...], nw_ref[...], dxn_ref[rows, :])
            dx_ref[...] = dres_ref[...] + dx
            dnw_ref[...] += dnw

    ends = lambda c, t: (jnp.where((c == 0) | (c == nc - 1), t, 0), 0)
    last = lambda c, t: (jnp.where(c == nc - 1, t, 0), 0)
    return pl.pallas_call(
        body, name="rmslin_bwd", grid=(nc, nt),
        in_specs=[pl.BlockSpec((tm, D), ends), pl.BlockSpec((tm, D), last),
                  pl.BlockSpec((tm, nb), lambda c, t: (t, c)),
                  pl.BlockSpec((1, D), lambda c, t: (0, 0)),
                  pl.BlockSpec((D, nb), lambda c, t: (0, c))],
        out_specs=[pl.BlockSpec((tm, D), last),
                   pl.BlockSpec((D, nb), lambda c, t: (0, c)),
                   pl.BlockSpec((1, nb), lambda c, t: (0, c)),
                   pl.BlockSpec((1, D), lambda c, t: (0, 0))],
        out_shape=[jax.ShapeDtypeStruct((T, D), F32), jax.ShapeDtypeStruct((D, N), F32),
                   jax.ShapeDtypeStruct((1, N), F32), jax.ShapeDtypeStruct((1, D), F32)],
        scratch_shapes=[pltpu.VMEM((T, D), BF16), pltpu.VMEM((T, D), F32)],
        compiler_params=_cparams(("arbitrary", "arbitrary")),
    )(x, dres, dproj, nw, w)


def lin_fwd(res, parts, w, b):
    T = res.shape[0]
    K, N = w.shape
    tm = min(T, 512)
    n = len(parts)
    offs = [sum(p.shape[1] for p in parts[:i]) for i in range(n + 1)]

    def body(res_ref, *refs):
        a_refs, (w_ref, b_ref, o_ref) = refs[:n], refs[n:]
        acc = res_ref[...] + b_ref[...]
        for i, a_ref in enumerate(a_refs):
            acc = acc + _dot(a_ref[...].astype(BF16), w_ref[offs[i]:offs[i + 1], :])
        o_ref[...] = acc

    return pl.pallas_call(
        body, name="lin_fwd", grid=(T // tm,),
        in_specs=[pl.BlockSpec((tm, N), lambda t: (t, 0))]
        + [pl.BlockSpec((tm, p.shape[1]), lambda t: (t, 0)) for p in parts]
        + [pl.BlockSpec((K, N), lambda t: (0, 0)), pl.BlockSpec((1, N), lambda t: (0, 0))],
        out_specs=pl.BlockSpec((tm, N), lambda t: (t, 0)),
        out_shape=jax.ShapeDtypeStruct((T, N), F32),
        compiler_params=_cparams(("parallel",)),
    )(res, *parts, w, b)


def lin_bwd(parts, dy, w):
    T = dy.shape[0]
    K, N = w.shape
    tm = min(T, 256)
    n = len(parts)
    offs = [sum(p.shape[1] for p in parts[:i]) for i in range(n + 1)]

    def body(*refs):
        a_refs, (dy_ref, w_ref, da_ref, dw_ref, db_ref) = refs[:n], refs[n:]

        @pl.when(pl.program_id(0) == 0)
        def _():
            dw_ref[...] = jnp.zeros_like(dw_ref)
            db_ref[...] = jnp.zeros_like(db_ref)

        dyf = dy_ref[...]
        dyb = dyf.astype(BF16)
        da_ref[...] = _dot_nt(dyb, w_ref[...])
        for i, a_ref in enumerate(a_refs):
            dw_ref[offs[i]:offs[i + 1], :] += _dot_tn(a_ref[...].astype(BF16), dyb)
        db_ref[...] += jnp.sum(dyf, axis=0, keepdims=True)

    return pl.pallas_call(
        body, name="lin_bwd", grid=(T // tm,),
        in_specs=[pl.BlockSpec((tm, p.shape[1]), lambda t: (t, 0)) for p in parts]
        + [pl.BlockSpec((tm, N), lambda t: (t, 0)), pl.BlockSpec((K, N), lambda t: (0, 0))],
        out_specs=[pl.BlockSpec((tm, K), lambda t: (t, 0)), pl.BlockSpec((K, N), lambda t: (0, 0)),
                   pl.BlockSpec((1, N), lambda t: (0, 0))],
        out_shape=[jax.ShapeDtypeStruct((T, K), F32), jax.ShapeDtypeStruct((K, N), F32),
                   jax.ShapeDtypeStruct((1, N), F32)],
        compiler_params=_cparams(("arbitrary",)),
    )(*parts, dy, w)


def loss_fwd_bwd(x, fw, target):
    T, D = x.shape
    tm = min(T, 256)

    def body(x_ref, fw_ref, tg_ref, loss_ref, dx_ref, dfw_ref):
        @pl.when(pl.program_id(0) == 0)
        def _():
            loss_ref[...] = jnp.zeros_like(loss_ref)
            dfw_ref[...] = jnp.zeros_like(dfw_ref)

        xv = x_ref[...]
        w = fw_ref[...]
        err = _rms(xv, w) - tg_ref[...]
        row = jnp.sum(err * err, axis=-1, keepdims=True)
        loss_ref[...] += (0.5 / D) * jnp.sum(row, axis=0, keepdims=True)
        dx, dfw = _rms_bwd(xv, w, err * (1.0 / D))
        dx_ref[...] = dx
        dfw_ref[...] += dfw

    return pl.pallas_call(
        body, name="loss_fwd_bwd", grid=(T // tm,),
        in_specs=[pl.BlockSpec((tm, D), lambda t: (t, 0)), pl.BlockSpec((1, D), lambda t: (0, 0)),
                  pl.BlockSpec((tm, D), lambda t: (t, 0))],
        out_specs=[pl.BlockSpec((1, 1), lambda t: (0, 0)), pl.BlockSpec((tm, D), lambda t: (t, 0)),
                   pl.BlockSpec((1, D), lambda t: (0, 0))],
        out_shape=[jax.ShapeDtypeStruct((1, 1), F32), jax.ShapeDtypeStruct((T, D), F32),
                   jax.ShapeDtypeStruct((1, D), F32)],
        compiler_params=_cparams(("arbitrary",)),
    )(x, fw, target)


def _attn_masks(n, rows, blk):
    r = lax.broadcasted_iota(jnp.int32, (rows, 2 * blk), 0)
    jj = lax.broadcasted_iota(jnp.int32, (rows, 2 * blk), 1)
    dist = (r % blk) + blk - jj
    valid = (dist >= 0) & (dist < blk) & ((n > 0) | (jj >= blk))
    return dist.astype(F32), valid


def _attn_block(q, kcat, vcat, sink, slope, dist, valid):
    d = q.shape[-1]
    s = _dot_nt(q.astype(BF16), kcat.astype(BF16)) * (d ** -0.5)
    s = jnp.where(valid, s - slope * dist, -1e30)
    m = lax.stop_gradient(jnp.maximum(jnp.max(s, axis=-1, keepdims=True), sink))
    e = jnp.exp(s - m)
    p = e / (jnp.sum(e, axis=-1, keepdims=True) + jnp.exp(sink - m))
    return _dot(p.astype(BF16), vcat.astype(BF16))


ATTN_G = ATTN_HEADS // ATTN_KV_HEADS
ATTN_QW = ATTN_G * HEAD_DIM
ATTN_KCOL = Q_A // KV_A


def _attn_specs():
    blk = ATTN_BLOCK
    qs = pl.BlockSpec((blk, ATTN_QW), lambda h, n: (n, h))
    prev = lambda c: pl.BlockSpec((blk, KV_A), lambda h, n: (jnp.maximum(n - 1, 0), c))
    cur = lambda c: pl.BlockSpec((blk, KV_A), lambda h, n: (n, c))
    rowp = pl.BlockSpec((ATTN_G * blk, 1), lambda h, n: (h, 0))
    return qs, [prev(ATTN_KCOL), cur(ATTN_KCOL), prev(ATTN_KCOL + 1), cur(ATTN_KCOL + 1)], rowp


def _attn_operands(h, q_ref, kp_ref, kc_ref, vp_ref, vc_ref):
    d = HEAD_DIM
    q = jnp.concatenate([q_ref[:, g * d:(g + 1) * d] for g in range(ATTN_G)], axis=0)
    pick = lambda r: jnp.where(h == 0, r[:, :d], r[:, d:])
    kcat = jnp.concatenate([pick(kp_ref[...]), pick(kc_ref[...])], axis=0)
    vcat = jnp.concatenate([pick(vp_ref[...]), pick(vc_ref[...])], axis=0)
    return q, kcat, vcat


def attn_fwd(proj, sink_rows, slope_rows, ag=()):
    T = proj.shape[0]
    blk, d = ATTN_BLOCK, HEAD_DIM

    def body(q_ref, kp_ref, kc_ref, vp_ref, vc_ref, sink_ref, slope_ref, o_ref):
        h, n = pl.program_id(0), pl.program_id(1)
        dist, valid = _attn_masks(n, ATTN_G * blk, blk)
        q, kcat, vcat = _attn_operands(h, q_ref, kp_ref, kc_ref, vp_ref, vc_ref)
        o = _attn_block(q, kcat, vcat, sink_ref[...], slope_ref[...], dist, valid)
        for g in range(ATTN_G):
            o_ref[:, g * d:(g + 1) * d] = o[g * blk:(g + 1) * blk]

    qs, kv, rowp = _attn_specs()
    (out,), gathered, _ = _pcall(
        body, (proj, proj, proj, proj, proj, sink_rows, slope_rows), name="attn_fwd",
        grid=(ATTN_KV_HEADS, T // blk), in_specs=[qs] + kv + [rowp, rowp], out_specs=[qs],
        out_shape=[jax.ShapeDtypeStruct((T, Q_A), F32)], sem=("arbitrary", "arbitrary"), ag=ag)
    return out, gathered


def attn_bwd(proj, sink_rows, slope_rows, dmix, rs=()):
    T = proj.shape[0]
    blk, d = ATTN_BLOCK, HEAD_DIM

    def body(q_ref, kp_ref, kc_ref, vp_ref, vc_ref, sink_ref, slope_ref, do_ref, dq_ref, dkv_ref, dsink_ref):
        h, n = pl.program_id(0), pl.program_id(1)

        @pl.when((h == 0) & (n == 0))
        def _():
            dkv_ref[...] = jnp.zeros_like(dkv_ref)

        @pl.when(n == 0)
        def _():
            dsink_ref[...] = jnp.zeros_like(dsink_ref)

        dist, valid = _attn_masks(n, ATTN_G * blk, blk)
        q, kcat, vcat = _attn_operands(h, q_ref, kp_ref, kc_ref, vp_ref, vc_ref)
        do = jnp.concatenate([do_ref[:, g * d:(g + 1) * d] for g in range(ATTN_G)], axis=0)
        fn = functools.partial(_attn_block, slope=slope_ref[...], dist=dist, valid=valid)
        _, vjp = jax.vjp(fn, q, kcat, vcat, sink_ref[...])
        dq, dkcat, dvcat, dsink = vjp(do)
        for g in range(ATTN_G):
            dq_ref[:, g * d:(g + 1) * d] = dq[g * blk:(g + 1) * blk]
        dsink_ref[...] += dsink
        lane = lax.broadcasted_iota(jnp.int32, (2 * blk, 2 * KV_A), 1)
        mine = (lane % KV_A) // d == h
        both = jnp.where(mine, jnp.concatenate([dkcat, dkcat, dvcat, dvcat], axis=1), 0.0)

        @pl.when(n == 0)
        def _():
            dkv_ref[0:blk, :] += both[blk:]

        @pl.when(n > 0)
        def _():
            rows = pl.ds(pl.multiple_of((n - 1) * blk, blk), 2 * blk)
            dkv_ref[rows, :] += both

    qs, kv, rowp = _attn_specs()
    outs, _, slots = _pcall(
        body, (proj, proj, proj, proj, proj, sink_rows, slope_rows, dmix), name="attn_bwd",
        grid=(ATTN_KV_HEADS, T // blk), in_specs=[qs] + kv + [rowp, rowp, qs],
        out_specs=[qs, pl.BlockSpec((T, 2 * KV_A), lambda h, n: (0, 0)), rowp],
        out_shape=[jax.ShapeDtypeStruct((T, Q_A), F32), jax.ShapeDtypeStruct((T, 2 * KV_A), F32),
                   jax.ShapeDtypeStruct((ATTN_HEADS * blk, 1), F32)],
        sem=("arbitrary", "arbitrary"), rs=rs)
    return outs, slots


_NN = (((2,), (1,)), ((0,), (0,)))
_NT = (((2,), (2,)), ((0,), (0,)))
_TN = (((1,), (1,)), ((0,), (0,)))


def _bmm(a, b, dims):
    return lax.dot_general(a.astype(BF16), b.astype(BF16), dims, preferred_element_type=F32)


def _split(x, terms):
    out = []
    for _ in range(terms):
        t = x.astype(BF16)
        out.append(t)
        x = x - t.astype(F32)
    return out


def _fine_product(a, b, dims):
    (ah, al), (bh, bl) = _split(a, 2), _split(b, 2)
    dot = lambda x, y: lax.dot_general(x, y, dims, preferred_element_type=F32)
    return dot(ah, bh) + (dot(ah, bl) + dot(al, bh))


def _mask_product(mask, x, dims):
    mb = mask.astype(BF16)
    parts = [lax.dot_general(mb, t, dims, preferred_element_type=F32) for t in _split(x, 3)]
    return parts[0] + (parts[1] + parts[2])


@jax.custom_vjp
def _fine_nt(a, b):
    return _fine_product(a, b, _NT)


_fine_nt.defvjp(lambda a, b: (_fine_product(a, b, _NT), (a, b)),
                lambda res, ct: (_fine_product(ct, res[1], _NN), _fine_product(ct, res[0], _TN)))


@jax.custom_vjp
def _mask_nn(mask, x):
    return _mask_product(mask, x, _NN)


_mask_nn.defvjp(lambda mask, x: (_mask_product(mask, x, _NN), mask),
                lambda mask, ct: (jnp.zeros_like(mask), _mask_product(mask, ct, _TN)))


@jax.custom_vjp
def _unit_lower_inverse(low):
    n = low.shape[-1]
    eye = (lax.broadcasted_iota(jnp.int32, low.shape, 1) == lax.broadcasted_iota(jnp.int32, low.shape, 2)).astype(F32)
    tinv = eye - low
    p = low
    for _ in range(n.bit_length() - 2):
        p = _bmm(p, p, _NN)
        tinv = tinv + _bmm(tinv, p, _NN)
    return tinv


def _unit_lower_inverse_fwd(low):
    tinv = _unit_lower_inverse(low)
    return tinv, tinv


_unit_lower_inverse.defvjp(_unit_lower_inverse_fwd, lambda tinv, ct: (-_bmm(_bmm(tinv, ct, _TN), tinv, _NT),))


def _dn_chunk(qc, kc, vc, zc, braw, araw, alog, dtb, nw, S):
    H, C, D = qc.shape
    row = lax.broadcasted_iota(jnp.int32, (H, C, C), 1)
    col = lax.broadcasted_iota(jnp.int32, (H, C, C), 2)
    causal = row >= col
    strict = row > col
    eye = (row == col).astype(F32)

    q = qc * lax.rsqrt(jnp.sum(qc * qc, axis=-1, keepdims=True) + EPS) * (D ** -0.5)
    k = kc * lax.rsqrt(jnp.sum(kc * kc, axis=-1, keepdims=True) + EPS)
    beta = _sigmoid(braw)
    g = -jnp.exp(alog) * _softplus(araw + dtb)
    a_col = _mask_nn(causal.astype(F32), jnp.broadcast_to(g, (H, C, C)))
    a_row = _mask_nn(jnp.ones((H, C, C), F32), eye * a_col)
    decay = jnp.where(causal, jnp.exp(jnp.where(causal, a_col - a_row, 0.0)), 0.0)
    kb = k * beta
    tinv = _unit_lower_inverse(jnp.where(strict, _fine_nt(kb, k) * decay, 0.0))
    e_col = jnp.exp(a_col)
    u = _bmm(tinv, vc * beta, _NN)
    w = _bmm(tinv, kb * e_col, _NN)
    attn = _fine_nt(q, k) * decay
    gl = a_col[:, C - 1:C, :]
    k_dec = k * jnp.exp(gl - a_col)
    v_new = u - _bmm(w, S, _NN)
    o = _bmm(q * e_col, S, _NN) + _bmm(attn, v_new, _NN)
    s_new = S * jnp.exp(jnp.broadcast_to(gl, (H, D, D))) + _bmm(k_dec, v_new, _TN)
    on = o * lax.rsqrt(jnp.mean(o * o, axis=-1, keepdims=True) + EPS) * nw
    return on * (zc * _sigmoid(zc)), s_new


DN_ZCOLS = IN_COLS_PAD - OFF_Z
DN_ZBLK = OFF_Z // DN_ZCOLS


def _dn_heads(a, off):
    return jnp.stack([a[:, off + h * DN_D:off + (h + 1) * DN_D] for h in range(DN_HEADS)])


def _dn_gate_cols(zb, off):
    return jnp.stack([zb[:, off + h:off + h + 1] for h in range(DN_HEADS)])


DN_STEP_CHUNKS = 4


def _dn_operands(x_ref, zb_ref, rows):
    x, zb = x_ref[rows, :], zb_ref[rows, :]
    return (_dn_heads(x, 0), _dn_heads(x, V_B), _dn_heads(x, 2 * V_B), _dn_heads(zb, 0),
            _dn_gate_cols(zb, V_B), _dn_gate_cols(zb, V_B + DN_HEADS))


def dn_fwd(qkvc, proj, alog, dtb, nw, ag=()):
    T = qkvc.shape[0]
    H, C, D, G = DN_HEADS, DN_CHUNK, DN_D, DN_STEP_CHUNKS
    N = T // C

    def body(x_ref, zb_ref, alog_ref, dtb_ref, nw_ref, o_ref, sall_ref, s_ref):
        @pl.when(pl.program_id(0) == 0)
        def _():
            s_ref[...] = jnp.zeros_like(s_ref)

        s = s_ref[...]
        for c in range(G):
            rows = slice(c * C, (c + 1) * C)
            sall_ref[c] = s
            on, s = _dn_chunk(*_dn_operands(x_ref, zb_ref, rows), alog_ref[...], dtb_ref[...], nw_ref[...], s)
            for h in range(H):
                o_ref[rows, h * D:(h + 1) * D] = on[h]
        s_ref[...] = s

    par = pl.BlockSpec((H, 1, 1), lambda n: (0, 0, 0))
    outs, gathered, _ = _pcall(
        body, (qkvc, proj, alog, dtb, nw), name="dn_fwd", grid=(N // G,),
        in_specs=[pl.BlockSpec((G * C, QKV_B), lambda n: (n, 0)),
                  pl.BlockSpec((G * C, DN_ZCOLS), lambda n: (n, DN_ZBLK)),
                  par, par, pl.BlockSpec((1, 1, D), lambda n: (0, 0, 0))],
        out_specs=[pl.BlockSpec((G * C, V_B), lambda n: (n, 0)), pl.BlockSpec((G, H, D, D), lambda n: (n, 0, 0, 0))],
        out_shape=[jax.ShapeDtypeStruct((T, V_B), F32), jax.ShapeDtypeStruct((N, H, D, D), F32)],
        scratch_shapes=[pltpu.VMEM((H, D, D), F32)], sem=("arbitrary",), ag=ag)
    return outs, gathered


def dn_bwd(qkvc, proj, alog, dtb, nw, sall, dmix, rs=()):
    T = qkvc.shape[0]
    H, C, D, G = DN_HEADS, DN_CHUNK, DN_D, DN_STEP_CHUNKS
    N = T // C // G

    def body(x_ref, zb_ref, alog_ref, dtb_ref, nw_ref, sall_ref, do_ref,
             dx_ref, dzb_ref, dalog_ref, ddtb_ref, dnw_ref, ds_ref):
        @pl.when(pl.program_id(0) == 0)
        def _():
            ds_ref[...] = jnp.zeros_like(ds_ref)
            dalog_ref[...] = jnp.zeros_like(dalog_ref)
            ddtb_ref[...] = jnp.zeros_like(ddtb_ref)
            dnw_ref[...] = jnp.zeros_like(dnw_ref)

        ds = ds_ref[...]
        lane = lax.broadcasted_iota(jnp.int32, (C, LANES), 1)
        for c in reversed(range(G)):
            rows = slice(c * C, (c + 1) * C)
            args = (*_dn_operands(x_ref, zb_ref, rows), alog_ref[...], dtb_ref[...], nw_ref[...], sall_ref[c])
            _, vjp = jax.vjp(_dn_chunk, *args)
            dq, dk, dv, dz, db, da, dalog, ddtb, dnw, ds = vjp((_dn_heads(do_ref[rows, :], 0), ds))
            for h in range(H):
                dx_ref[rows, h * D:(h + 1) * D] = dq[h]
                dx_ref[rows, V_B + h * D:V_B + (h + 1) * D] = dk[h]
                dx_ref[rows, 2 * V_B + h * D:2 * V_B + (h + 1) * D] = dv[h]
                dzb_ref[rows, h * D:(h + 1) * D] = dz[h]
            tail = jnp.zeros((C, LANES), F32)
            for h in range(H):
                tail = tail + jnp.where(lane == h, jnp.broadcast_to(db[h], (C, LANES)), 0.0)
                tail = tail + jnp.where(lane == H + h, jnp.broadcast_to(da[h], (C, LANES)), 0.0)
            dzb_ref[rows, V_B:V_B + LANES] = tail
            dzb_ref[rows, V_B + LANES:] = jnp.zeros((C, DN_ZCOLS - V_B - LANES), F32)
            dalog_ref[...] += dalog
            ddtb_ref[...] += ddtb
            dnw_ref[...] += dnw
        ds_ref[...] = ds

    par = pl.BlockSpec((H, 1, 1), lambda i: (0, 0, 0))
    nws = pl.BlockSpec((1, 1, D), lambda i: (0, 0, 0))
    outs, _, slots = _pcall(
        body, (qkvc, proj, alog, dtb, nw, sall, dmix), name="dn_bwd", grid=(N,),
        in_specs=[pl.BlockSpec((G * C, QKV_B), lambda i: (N - 1 - i, 0)),
                  pl.BlockSpec((G * C, DN_ZCOLS), lambda i: (N - 1 - i, DN_ZBLK)), par, par, nws,
                  pl.BlockSpec((G, H, D, D), lambda i: (N - 1 - i, 0, 0, 0)),
                  pl.BlockSpec((G * C, V_B), lambda i: (N - 1 - i, 1))],
        out_specs=[pl.BlockSpec((G * C, QKV_B), lambda i: (N - 1 - i, 0)),
                   pl.BlockSpec((G * C, DN_ZCOLS), lambda i: (N - 1 - i, 0)), par, par, nws],
        out_shape=[jax.ShapeDtypeStruct((T, QKV_B), F32), jax.ShapeDtypeStruct((T, DN_ZCOLS), F32)]
        + [jax.ShapeDtypeStruct((H, 1, 1), F32)] * 2 + [jax.ShapeDtypeStruct((1, 1, D), F32)],
        scratch_shapes=[pltpu.VMEM((H, D, D), F32)], sem=("arbitrary",), rs=rs)
    return outs, slots


def _conv_taps(buf_ref, w, width, halo, tm):
    acc = None
    for kk, win in _windows(buf_ref, [halo - (width - 1) + kk for kk in range(width)], tm):
        term = w[kk:kk + 1, :] * win
        acc = term if acc is None else acc + term
    return acc


def _windows(ref, offsets, tm):
    for res in range(SUBLANES):
        ks = [k for k, o in enumerate(offsets) if o % SUBLANES == res]
        if not ks:
            continue
        lo = min(offsets[k] for k in ks)
        hi = max(offsets[k] for k in ks)
        shifted = ref[pl.ds(lo, tm + hi - lo), :]
        for k in ks:
            yield k, shifted[offsets[k] - lo:offsets[k] - lo + tm]


def _conv_taps_bwd(dbuf_ref, w, width, tm):
    acc = None
    for kk, win in _windows(dbuf_ref, [width - 1 - kk for kk in range(width)], tm):
        term = w[kk:kk + 1, :] * win
        acc = term if acc is None else acc + term
    return acc


def _conv_dw_acc(dw_ref, dout, buf_ref, width, halo, tm):
    for kk, win in _windows(buf_ref, [halo - (width - 1) + kk for kk in range(width)], tm):
        dw_ref[pl.ds(kk, 1), :] += jnp.sum(dout * win, axis=0, keepdims=True)


DNC_HALO = 8
DNC_COLS = 768


def dnconv_fwd(proj, w):
    T = proj.shape[0]
    tm = min(T, 256)
    hb = tm // DNC_HALO

    def body(x_ref, h_ref, w_ref, o_ref, buf_ref):
        i = pl.program_id(0)
        buf_ref[0:DNC_HALO, :] = jnp.where(i > 0, h_ref[...], 0.0)
        buf_ref[DNC_HALO:, :] = x_ref[...]
        acc = _conv_taps(buf_ref, w_ref[...], DN_CONV, DNC_HALO, tm)
        o_ref[...] = acc * _sigmoid(acc)

    return pl.pallas_call(
        body, name="dnconv_fwd", grid=(T // tm, 2),
        in_specs=[pl.BlockSpec((tm, DNC_COLS), lambda i, c: (i, 1 + c)),
                  pl.BlockSpec((DNC_HALO, DNC_COLS), lambda i, c: (jnp.maximum(i * hb - 1, 0), 1 + c)),
                  pl.BlockSpec((DN_CONV, DNC_COLS), lambda i, c: (0, c))],
        out_specs=pl.BlockSpec((tm, DNC_COLS), lambda i, c: (i, c)),
        out_shape=jax.ShapeDtypeStruct((T, QKV_B), F32),
        scratch_shapes=[pltpu.VMEM((DNC_HALO + tm, DNC_COLS), F32)],
        compiler_params=_cparams(("parallel", "parallel")),
    )(proj, proj, w)


def dnconv_bwd(proj, w, dout):
    T = proj.shape[0]
    tm = min(T, 256)
    nt = T // tm
    hb = tm // DNC_HALO

    def body(x_ref, h_ref, w_ref, do_ref, dx_ref, dw_ref, buf_ref, dbuf_ref):
        r = pl.program_id(1)
        i = nt - 1 - r

        @pl.when(r == 0)
        def _():
            dw_ref[...] = jnp.zeros_like(dw_ref)
            dbuf_ref[tm:, :] = jnp.zeros((DNC_HALO, DNC_COLS), F32)

        buf_ref[0:DNC_HALO, :] = jnp.where(i > 0, h_ref[...], 0.0)
        buf_ref[DNC_HALO:, :] = x_ref[...]
        wv = w_ref[...]
        acc = _conv_taps(buf_ref, wv, DN_CONV, DNC_HALO, tm)
        sg = _sigmoid(acc)
        dacc = do_ref[...] * (sg * (1.0 + acc * (1.0 - sg)))
        dbuf_ref[0:tm, :] = dacc
        dx_ref[...] = _conv_taps_bwd(dbuf_ref, wv, DN_CONV, tm)
        _conv_dw_acc(dw_ref, dacc, buf_ref, DN_CONV, DNC_HALO, tm)
        dbuf_ref[tm:, :] = dacc[0:DNC_HALO, :]

    return pl.pallas_call(
        body, name="dnconv_bwd", grid=(2, nt),
        in_specs=[pl.BlockSpec((tm, DNC_COLS), lambda c, r: (nt - 1 - r, 1 + c)),
                  pl.BlockSpec((DNC_HALO, DNC_COLS), lambda c, r: (jnp.maximum((nt - 1 - r) * hb - 1, 0), 1 + c)),
                  pl.BlockSpec((DN_CONV, DNC_COLS), lambda c, r: (0, c)),
                  pl.BlockSpec((tm, DNC_COLS), lambda c, r: (nt - 1 - r, c))],
        out_specs=[pl.BlockSpec((tm, DNC_COLS), lambda c, r: (nt - 1 - r, c)),
                   pl.BlockSpec((DN_CONV, DNC_COLS), lambda c, r: (0, c))],
        out_shape=[jax.ShapeDtypeStruct((T, QKV_B), F32), jax.ShapeDtypeStruct((DN_CONV, QKV_B), F32)],
        scratch_shapes=[pltpu.VMEM((DNC_HALO + tm, DNC_COLS), F32), pltpu.VMEM((tm + DNC_HALO, DNC_COLS), F32)],
        compiler_params=_cparams(("parallel", "arbitrary")),
    )(proj, proj, w, dout)


CV_HALO = 32


def _cv_post(cv, lnw, lnb):
    mu = jnp.mean(cv, axis=-1, keepdims=True)
    xc = cv - mu
    y = xc * lax.rsqrt(jnp.mean(xc * xc, axis=-1, keepdims=True) + EPS) * lnw + lnb
    return y * _sigmoid(y)


def cv_fwd(ab, w, bdw, lnw, lnb, ag=()):
    T = ab.shape[0]
    D = ab.shape[1] // 2
    tm = min(T, 256)
    hb = tm // CV_HALO

    def body(a_ref, b_ref, ah_ref, bh_ref, w_ref, bdw_ref, lnw_ref, lnb_ref, o_ref, cv_ref, buf_ref):
        i = pl.program_id(0)
        buf_ref[0:CV_HALO, :] = jnp.where(i > 0, ah_ref[...] * _sigmoid(bh_ref[...]), 0.0)
        buf_ref[CV_HALO:, :] = a_ref[...] * _sigmoid(b_ref[...])
        cv = _conv_taps(buf_ref, w_ref[...], CONV_WIDTH, CV_HALO, tm) + bdw_ref[...]
        cv_ref[...] = cv
        o_ref[...] = _cv_post(cv, lnw_ref[...], lnb_ref[...])

    halo = lambda c: pl.BlockSpec((CV_HALO, D), lambda i: (jnp.maximum(i * hb - 1, 0), c))
    vec = pl.BlockSpec((1, D), lambda i: (0, 0))
    tile = pl.BlockSpec((tm, D), lambda i: (i, 0))
    outs, gathered, _ = _pcall(
        body, (ab, ab, ab, ab, w, bdw, lnw, lnb), name="cv_fwd", grid=(T // tm,),
        in_specs=[tile, pl.BlockSpec((tm, D), lambda i: (i, 1)),
                  halo(0), halo(1), pl.BlockSpec((CONV_WIDTH, D), lambda i: (0, 0)), vec, vec, vec],
        out_specs=[tile, tile],
        out_shape=[jax.ShapeDtypeStruct((T, D), F32), jax.ShapeDtypeStruct((T, D), F32)],
        scratch_shapes=[pltpu.VMEM((CV_HALO + tm, D), F32)], sem=("arbitrary",), ag=ag)
    return outs, gathered


def cv_bwd(ab, cv, w, lnw, lnb, dout, rs=()):
    T = ab.shape[0]
    D = ab.shape[1] // 2
    tm = min(T, 256)
    nt = T // tm
    hb = tm // CV_HALO

    def body(a_ref, b_ref, ah_ref, bh_ref, cv_ref, w_ref, lnw_ref, lnb_ref, do_ref,
             da_ref, db_ref, dw_ref, dbdw_ref, dlnw_ref, dlnb_ref, buf_ref, dbuf_ref):
        r = pl.program_id(0)
        i = nt - 1 - r

        @pl.when(r == 0)
        def _():
            dw_ref[...] = jnp.zeros_like(dw_ref)
            dbdw_ref[...] = jnp.zeros_like(dbdw_ref)
            dlnw_ref[...] = jnp.zeros_like(dlnw_ref)
            dlnb_ref[...] = jnp.zeros_like(dlnb_ref)
            dbuf_ref[tm:, :] = jnp.zeros((CV_HALO, D), F32)

        a = a_ref[...]
        sb = _sigmoid(b_ref[...])
        buf_ref[0:CV_HALO, :] = jnp.where(i > 0, ah_ref[...] * _sigmoid(bh_ref[...]), 0.0)
        buf_ref[CV_HALO:, :] = a * sb
        wv = w_ref[...]
        _, vjp = jax.vjp(_cv_post, cv_ref[...], lnw_ref[...], lnb_ref[...])
        dcv, dlnw, dlnb = vjp(do_ref[...])
        dlnw_ref[...] += dlnw
        dlnb_ref[...] += dlnb
        dbdw_ref[...] += jnp.sum(dcv, axis=0, keepdims=True)
        dbuf_ref[0:tm, :] = dcv
        du = _conv_taps_bwd(dbuf_ref, wv, CONV_WIDTH, tm)
        _conv_dw_acc(dw_ref, dcv, buf_ref, CONV_WIDTH, CV_HALO, tm)
        dbuf_ref[tm:, :] = dcv[0:CV_HALO, :]
        da_ref[...] = du * sb
        db_ref[...] = du * a * sb * (1.0 - sb)

    tile = lambda c: pl.BlockSpec((tm, D), lambda r: (nt - 1 - r, c))
    halo = lambda c: pl.BlockSpec((CV_HALO, D), lambda r: (jnp.maximum((nt - 1 - r) * hb - 1, 0), c))
    vec = pl.BlockSpec((1, D), lambda r: (0, 0))
    wsp = pl.BlockSpec((CONV_WIDTH, D), lambda r: (0, 0))
    (da, db, dw, dbdw, dlnw, dlnb), _, slots = _pcall(
        body, (ab, ab, ab, ab, cv, w, lnw, lnb, dout), name="cv_bwd", grid=(nt,),
        in_specs=[tile(0), tile(1), halo(0), halo(1), tile(0), wsp, vec, vec, tile(0)],
        out_specs=[tile(0), tile(0), wsp, vec, vec, vec],
        out_shape=[jax.ShapeDtypeStruct((T, D), F32), jax.ShapeDtypeStruct((T, D), F32),
                   jax.ShapeDtypeStruct((CONV_WIDTH, D), F32)] + [jax.ShapeDtypeStruct((1, D), F32)] * 3,
        scratch_shapes=[pltpu.VMEM((CV_HALO + tm, D), F32), pltpu.VMEM((tm + CV_HALO, D), F32)],
        sem=("arbitrary",), rs=rs)
    return (jnp.concatenate([da, db], axis=1), dw, dbdw, dlnw, dlnb), slots


def adamw(w, m, v, slots, rs=()):
    L, R, C = w.shape
    ns = slots[0].shape[0]
    fits = lambda r, c: ns * r * c * 2 <= ADAM_SLOT_BLOCK
    tiles = [(R, C)] if fits(R, C) else []
    tiles += [(d, C) for d in range(16, R, 16) if R % d == 0 and fits(d, C)]
    tiles += [(R, d) for d in range(LANES, C, LANES) if C % d == 0 and fits(R, d)]
    tr, tc = max(tiles, key=lambda t: t[0] * t[1])
    c1 = 1.0 / (1.0 - ADAM_B1 ** ADAM_STEP)
    c2 = 1.0 / (1.0 - ADAM_B2 ** ADAM_STEP)

    def body(w_ref, m_ref, v_ref, *rest):
        s_refs = rest[:L]
        g_ref, d_ref, nm_ref, nv_ref = rest[L:]
        l = pl.program_id(0)
        for k in range(L):
            @pl.when(l == k)
            def _(s_ref=s_refs[k]):
                g = s_ref[0].astype(F32)
                for j in range(1, ns):
                    g = g + s_ref[j].astype(F32)
                nm = ADAM_B1 * m_ref[0] + (1.0 - ADAM_B1) * g
                nv = ADAM_B2 * v_ref[0] + (1.0 - ADAM_B2) * (g * g)
                g_ref[0] = g
                nm_ref[0] = nm
                nv_ref[0] = nv
                d_ref[0] = -ADAM_LR * ((nm * c1) / (jnp.sqrt(nv * c2) + ADAM_EPS) + ADAM_WD * w_ref[0])

    nc = C // tc
    blk = pl.BlockSpec((1, tr, tc), lambda l, i: (l, i // nc, i % nc))
    slot = lambda k: pl.BlockSpec((ns, tr, tc), lambda l, i: (0, jnp.where(l == k, i // nc, 0),
                                                                 jnp.where(l == k, i % nc, 0)))
    outs, _, landed = _pcall(
        body, (w, m, v, *slots), name="adamw", grid=(L, (R // tr) * nc),
        in_specs=[blk, blk, blk] + [slot(k) for k in range(L)],
        out_specs=[blk, blk, blk, blk],
        out_shape=[jax.ShapeDtypeStruct((L, R, C), F32)] * 4,
        sem=("arbitrary", "arbitrary"), rs=rs)
    return outs, landed


def _unshard(g, axis):
    g = jnp.moveaxis(g, 0, axis)
    s = g.shape
    return g.reshape(s[:axis] + (s[axis] * s[axis + 1],) + s[axis + 2:])


def _to_blocks(full, axis):
    s = full.shape
    g = full.reshape(s[:axis] + (N_DEV, s[axis] // N_DEV) + s[axis + 1:])
    return jnp.moveaxis(g, axis, 0)


SMALL = (("norm_w", 2), ("dn_conv_w", 2), ("conv_b_pw1", 1), ("conv_w_dw", 2), ("conv_b_dw", 1),
         ("conv_ln_w", 1), ("conv_ln_b", 1), ("conv_b_pw2", 1),
         ("attn_sinks", None), ("dn_a_log", None), ("dn_dt_bias", None), ("dn_norm_w", None), ("final_norm_w", None))
SMALL_AXIS = dict(SMALL)


def _pack(parts):
    flat = jnp.concatenate([p.reshape(-1) for p in parts])
    pad = (-flat.shape[0]) % LANES
    return jnp.pad(flat, (0, pad))


def _unpack(flat, shapes):
    out, off = [], 0
    for s in shapes:
        n = int(np.prod(s))
        out.append(flat[off:off + n].reshape(s))
        off += n
    return out


def kernel(x, norm_w, ffn_w_gate, ffn_w_up, ffn_w_down, mix_w_in, dn_conv_w, attn_sinks, dn_a_log, dn_dt_bias, dn_norm_w, mix_w_out, conv_w_pw1, conv_b_pw1, conv_w_dw, conv_b_dw, conv_ln_w, conv_ln_b, conv_w_pw2, conv_b_pw2, final_norm_w, loss_target, m_norm_w, m_ffn_w_gate, m_ffn_w_up, m_ffn_w_down, m_mix_w_in, m_dn_conv_w, m_attn_sinks, m_dn_a_log, m_dn_dt_bias, m_dn_norm_w, m_mix_w_out, m_conv_w_pw1, m_conv_b_pw1, m_conv_w_dw, m_conv_b_dw, m_conv_ln_w, m_conv_ln_b, m_conv_w_pw2, m_conv_b_pw2, m_final_norm_w, v_norm_w, v_ffn_w_gate, v_ffn_w_up, v_ffn_w_down, v_mix_w_in, v_dn_conv_w, v_attn_sinks, v_dn_a_log, v_dn_dt_bias, v_dn_norm_w, v_mix_w_out, v_conv_w_pw1, v_conv_b_pw1, v_conv_w_dw, v_conv_b_dw, v_conv_ln_w, v_conv_ln_b, v_conv_w_pw2, v_conv_b_pw2, v_final_norm_w):
    W = dict(norm_w=norm_w, ffn_w_gate=ffn_w_gate, ffn_w_up=ffn_w_up, ffn_w_down=ffn_w_down, mix_w_in=mix_w_in,
             dn_conv_w=dn_conv_w, attn_sinks=attn_sinks, dn_a_log=dn_a_log, dn_dt_bias=dn_dt_bias,
             dn_norm_w=dn_norm_w, mix_w_out=mix_w_out, conv_w_pw1=conv_w_pw1, conv_b_pw1=conv_b_pw1,
             conv_w_dw=conv_w_dw, conv_b_dw=conv_b_dw, conv_ln_w=conv_ln_w, conv_ln_b=conv_ln_b,
             conv_w_pw2=conv_w_pw2, conv_b_pw2=conv_b_pw2, final_norm_w=final_norm_w)
    M = dict(norm_w=m_norm_w, ffn_w_gate=m_ffn_w_gate, ffn_w_up=m_ffn_w_up, ffn_w_down=m_ffn_w_down,
             mix_w_in=m_mix_w_in, dn_conv_w=m_dn_conv_w, attn_sinks=m_attn_sinks, dn_a_log=m_dn_a_log,
             dn_dt_bias=m_dn_dt_bias, dn_norm_w=m_dn_norm_w, mix_w_out=m_mix_w_out, conv_w_pw1=m_conv_w_pw1,
             conv_b_pw1=m_conv_b_pw1, conv_w_dw=m_conv_w_dw, conv_b_dw=m_conv_b_dw, conv_ln_w=m_conv_ln_w,
             conv_ln_b=m_conv_ln_b, conv_w_pw2=m_conv_w_pw2, conv_b_pw2=m_conv_b_pw2, final_norm_w=m_final_norm_w)
    V = dict(norm_w=v_norm_w, ffn_w_gate=v_ffn_w_gate, ffn_w_up=v_ffn_w_up, ffn_w_down=v_ffn_w_down,
             mix_w_in=v_mix_w_in, dn_conv_w=v_dn_conv_w, attn_sinks=v_attn_sinks, dn_a_log=v_dn_a_log,
             dn_dt_bias=v_dn_dt_bias, dn_norm_w=v_dn_norm_w, mix_w_out=v_mix_w_out, conv_w_pw1=v_conv_w_pw1,
             conv_b_pw1=v_conv_b_pw1, conv_w_dw=v_conv_w_dw, conv_b_dw=v_conv_b_dw, conv_ln_w=v_conv_ln_w,
             conv_ln_b=v_conv_ln_b, conv_w_pw2=v_conv_w_pw2, conv_b_pw2=v_conv_b_pw2, final_norm_w=v_final_norm_w)

    T, D = x.shape[1], x.shape[2]
    xs = x[0]

    big = ("ffn_w_gate", "ffn_w_up", "ffn_w_down", "mix_w_in", "mix_w_out", "conv_w_pw1", "conv_w_pw2")
    shard3 = {k: W[k].reshape((-1,) + W[k].shape[-2:]) for k in big}
    shard_bf = {k: shard3[k].astype(BF16) for k in big}
    ffn_unit = lambda i: [("ffn_w_gate", i), ("ffn_w_up", i), ("ffn_w_down", i)]
    even_unit = lambda e: [("mix_w_in", e), ("mix_w_out", e)]
    odd_unit = lambda e: [("conv_w_pw1", e), ("conv_w_pw2", e)]
    have = {}

    def ag_jobs(units):
        return [(shard_bf[k], i) for k, i in units]

    def ag_done(units, gathered):
        have.update(zip(units, gathered))

    small_sharded = [(k, ax) for k, ax in SMALL if ax is not None]
    small_pack = _pack([W[k] for k, _ in small_sharded])[None, :]
    first_units = ffn_unit(0)
    gathered = exchange(ag_jobs(first_units) + [(small_pack, None)])
    ag_done(first_units, gathered[:-1])
    small_full = {}
    for (k, ax), parts in zip(small_sharded,
                              zip(*[_unpack(gathered[-1][s, 0], [W[k].shape for k, _ in small_sharded])
                                    for s in range(N_DEV)])):
        small_full[k] = _unshard(jnp.stack(parts), ax)
    nw_full = small_full["norm_w"]

    ffn_w = lambda i: [have[u] for u in ffn_unit(i)]
    w_in_of = lambda e: jnp.pad(_unshard(have[("mix_w_in", e)], 1), ((0, 0), (0, IN_COLS_PAD - IN_COLS)))
    w_out_of = lambda e: have[("mix_w_out", e)].reshape(D, D)
    w_pw1_of = lambda e: _unshard(have[("conv_w_pw1", e)], 1)
    w_pw2_of = lambda e: have[("conv_w_pw2", e)].reshape(D, D)
    fwd_order, needed = [], {}
    for l in range(DEPTH):
        mixer = [("A", l), ("E", l)] if l % 2 == 0 else [("O", l)]
        fwd_order += [("F", 2 * l)] + mixer + [("F", 2 * l + 1)]
        needed[("F", 2 * l)], needed[("F", 2 * l + 1)] = ffn_unit(2 * l), ffn_unit(2 * l + 1)
        needed[mixer[0]] = even_unit(l // 2) if l % 2 == 0 else odd_unit(l // 2)
    queue = [(u, pos) for pos, key in enumerate(fwd_order) for u in needed.get(key, []) if u not in first_units]
    unit_bytes = lambda u: N_DEV * shard_bf[u[0]][u[1]].size * 2
    fwd_carry, at = {}, 0
    for pos, key in enumerate(fwd_order):
        cap = FWD_CARRY_BYTES[key[0]]
        taken, used = [], 0
        while at < len(queue) and (queue[at][1] <= pos + 1 or used + unit_bytes(queue[at][0]) <= cap):
            taken.append(queue[at][0])
            used += unit_bytes(queue[at][0])
            at += 1
        fwd_carry[key] = taken
    zero_in = jnp.zeros((1, IN_COLS_PAD), F32)
    zero_d = jnp.zeros((1, D), F32)
    slope_rows = jnp.asarray(np.repeat(2.0 ** (-8.0 * np.arange(1, ATTN_HEADS + 1) / ATTN_HEADS), ATTN_BLOCK)
                             .astype(np.float32)[:, None])

    saved = []
    h = xs
    w_in, w_out, w_pw1, w_pw2 = {}, {}, {}, {}

    def ffn_forward(h, l, half):
        i = 2 * l + half
        units = fwd_carry.get(("F", i), [])
        h, gathered = ffn_fwd(h, nw_full[l, 2 * half][None], *ffn_w(i), ag=ag_jobs(units))
        ag_done(units, gathered)
        return h

    for l in range(DEPTH):
        e = l // 2
        st = {"x0": h}
        h = ffn_forward(h, l, 0)
        st["x1"] = h
        if l % 2 == 0:
            w_in[e], w_out[e] = w_in_of(e), w_out_of(e)
            proj = rmslin_fwd(h, nw_full[l, 1][None], w_in[e], zero_in)
            st["proj"] = proj
            st["qkvc"] = dnconv_fwd(proj, small_full["dn_conv_w"][e])
            st["sink_rows"] = jnp.repeat(attn_sinks[e], ATTN_BLOCK)[:, None]
            st["alog"] = dn_a_log[e].reshape(DN_HEADS, 1, 1)
            st["dtb"] = dn_dt_bias[e].reshape(DN_HEADS, 1, 1)
            st["dnw"] = dn_norm_w[e].reshape(1, 1, DN_D)
            units = fwd_carry[("A", l)]
            st["att"], gathered = attn_fwd(proj, st["sink_rows"], slope_rows, ag=ag_jobs(units))
            ag_done(units, gathered)
            units = fwd_carry[("E", l)]
            (st["og"], st["sall"]), gathered = dn_fwd(st["qkvc"], proj, st["alog"], st["dtb"], st["dnw"],
                                                      ag=ag_jobs(units))
            ag_done(units, gathered)
            h = lin_fwd(h, [st["att"], st["og"]], w_out[e], zero_d)
        else:
            units = fwd_carry[("O", l)]
            w_pw1[e], w_pw2[e] = w_pw1_of(e), w_pw2_of(e)
            st["ab"] = rmslin_fwd(h, nw_full[l, 1][None], w_pw1[e], small_full["conv_b_pw1"][e][None])
            (st["act"], st["cv"]), gathered = cv_fwd(st["ab"], small_full["conv_w_dw"][e], small_full["conv_b_dw"][e][None],
                                         small_full["conv_ln_w"][e][None], small_full["conv_ln_b"][e][None],
                                         ag=ag_jobs(units))
            ag_done(units, gathered)
            h = lin_fwd(h, [st["act"]], w_pw2[e], small_full["conv_b_pw2"][e][None])
        st["x2"] = h
        h = ffn_forward(h, l, 1)
        saved.append(st)

    loss_part, dh, dfinal = loss_fwd_bwd(h, final_norm_w[None], loss_target[0])
    loss = lax.psum(loss_part[0, 0], ("x", "y", "c"))

    d_norm = [[None] * 3 for _ in range(DEPTH)]
    d_small = {k: [None, None] for k in ("dn_conv_w", "conv_b_pw1", "conv_w_dw", "conv_b_dw", "conv_ln_w",
                                         "conv_ln_b", "conv_b_pw2", "attn_sinks", "dn_a_log", "dn_dt_bias",
                                         "dn_norm_w")}
    pending, slot = [], {}

    def take_jobs(cap=None, only=None):
        taken = [p for p in pending if p[1] == "swap"]
        used = 0
        for p in pending:
            if p[1] == "swap" or (only is not None and p[0][0] != only):
                continue
            if cap is not None and used + p[2].size * 2 > cap:
                break
            taken.append(p)
            used += p[2].size * 2
        pending[:] = [p for p in pending if all(p is not t for t in taken)]
        return taken, [(kind, arr) for _, kind, arr in taken]

    def land(taken, results):
        swapped = [(unit, arr, res) for (unit, kind, arr), res in zip(taken, results) if kind == "swap"]
        slot.update({unit: res for (unit, kind, _), res in zip(taken, results) if kind != "swap"})
        if swapped:
            sums = pair_add([g for _, g, _ in swapped], [r for _, _, r in swapped])
            pending.extend((unit, "chips", h) for (unit, _, _), h in zip(swapped, sums))

    def ffn_backward(dh, l, half):
        i = 2 * l + half
        taken, jobs = take_jobs(BWD_CARRY_BYTES["F"])
        (dh, dg, du, dd, d_norm[l][2 * half]), results = ffn_bwd(
            st["x2" if half else "x0"], dh, nw_full[l, 2 * half][None], *ffn_w(i), rs=jobs)
        land(taken, results)
        pending.extend((u, "swap", g) for u, g in zip(ffn_unit(i), (dg, du, dd)))
        return dh

    for l in reversed(range(DEPTH)):
        e = l // 2
        st = saved[l]
        dh = ffn_backward(dh, l, 1)
        if l % 2 == 0:
            dmix, d_out, _ = lin_bwd([st["att"], st["og"]], dh, w_out[e])
            pending.append((("mix_w_out", e), "direct", d_out.reshape(N_DEV, D // N_DEV, D).astype(BF16)))
            taken, jobs = take_jobs(BWD_CARRY_BYTES["E"])
            (dqkvc, dzba, dalog, ddtb, ddnw), results = dn_bwd(
                st["qkvc"], st["proj"], st["alog"], st["dtb"], st["dnw"], st["sall"], dmix, rs=jobs)
            land(taken, results)
            taken, jobs = take_jobs(BWD_CARRY_BYTES["A"])
            (dqa, dkva, dsink), results = attn_bwd(st["proj"], st["sink_rows"], slope_rows, dmix, rs=jobs)
            land(taken, results)
            dqkv, d_small["dn_conv_w"][e] = dnconv_bwd(st["proj"], small_full["dn_conv_w"][e], dqkvc)
            dproj = jnp.concatenate([dqa, dkva, dqkv, dzba], axis=1)
            dh, d_in, _, d_norm[l][1] = rmslin_bwd(st["x1"], dh, dproj, nw_full[l, 1][None], w_in[e])
            pending.append((("mix_w_in", e), "direct", _to_blocks(d_in[:, :IN_COLS], 1).astype(BF16)))
            d_small["attn_sinks"][e] = jnp.sum(dsink.reshape(ATTN_HEADS, ATTN_BLOCK), axis=1)
            d_small["dn_a_log"][e] = dalog.reshape(DN_HEADS)
            d_small["dn_dt_bias"][e] = ddtb.reshape(DN_HEADS)
            d_small["dn_norm_w"][e] = ddnw.reshape(DN_D)
        else:
            dact, d_pw2, d_small["conv_b_pw2"][e] = lin_bwd([st["act"]], dh, w_pw2[e])
            pending.append((("conv_w_pw2", e), "direct", d_pw2.reshape(N_DEV, D // N_DEV, D).astype(BF16)))
            taken, jobs = take_jobs(BWD_CARRY_BYTES["O"])
            (dab, d_small["conv_w_dw"][e], d_small["conv_b_dw"][e], d_small["conv_ln_w"][e],
             d_small["conv_ln_b"][e]), results = cv_bwd(
                st["ab"], st["cv"], small_full["conv_w_dw"][e],
                small_full["conv_ln_w"][e][None], small_full["conv_ln_b"][e][None], dact, rs=jobs)
            land(taken, results)
            dh, d_pw1, d_small["conv_b_pw1"][e], d_norm[l][1] = rmslin_bwd(
                st["x1"], dh, dab, nw_full[l, 1][None], w_pw1[e])
            pending.append((("conv_w_pw1", e), "direct", _to_blocks(d_pw1, 1).astype(BF16)))
        dh = ffn_backward(dh, l, 0)
    grad_x = dh[None]

    full_small = {"norm_w": jnp.stack([jnp.concatenate(r, axis=0) for r in d_norm]),
                  "final_norm_w": dfinal[0]}
    for k, pair in d_small.items():
        full_small[k] = jnp.stack([p.reshape(W[k].shape[1:-1] + (-1,)) if SMALL_AXIS[k] is not None
                                   else p for p in pair])
    rows = []
    for s in range(N_DEV):
        parts = [_to_blocks(full_small[k], ax)[s] if ax is not None else full_small[k] for k, ax in SMALL]
        rows.append(_pack(parts))
    send_small = jnp.stack(rows)[:, None, :]
    pending.append((("small", 0), "direct", send_small))

    res = {}
    waiting = lambda k: [p for p in pending if p[0][0] == k]
    adam_order = sorted(big, key=lambda k: len(waiting(k))) + ["small"]
    for n, k in enumerate(adam_order[:-1]):
        taken, jobs = take_jobs(only=next((kk for kk in adam_order[n + 1:] if waiting(kk)), "small"))
        outs, results = adamw(shard3[k], M[k].reshape(shard3[k].shape), V[k].reshape(shard3[k].shape),
                              [slot[(k, i)] for i in range(shard3[k].shape[0])], rs=jobs)
        land(taken, results)
        res[k] = [o.reshape(W[k].shape) for o in outs]
    pk = lambda d: _pack([d[k] for k, _ in SMALL])[None, None, :]
    outs, _ = adamw(pk(W), pk(M), pk(V), [slot[("small", 0)]])
    shapes = [W[k].shape for k, _ in SMALL]
    unp = [_unpack(o[0, 0], shapes) for o in outs]
    for i, (k, _) in enumerate(SMALL):
        res[k] = [u[i] for u in unp]

    order = ("norm_w", "ffn_w_gate", "ffn_w_up", "ffn_w_down", "mix_w_in", "dn_conv_w", "attn_sinks", "dn_a_log",
             "dn_dt_bias", "dn_norm_w", "mix_w_out", "conv_w_pw1", "conv_b_pw1", "conv_w_dw", "conv_b_dw",
             "conv_ln_w", "conv_ln_b", "conv_w_pw2", "conv_b_pw2", "final_norm_w")
    return (loss, grad_x, *[res[k][0] for k in order], *[res[k][1] for k in order],
            *[res[k][2] for k in order], *[res[k][3] for k in order])
```

```python
import functools

import numpy as np
import jax
import jax.numpy as jnp
from jax import lax
from jax.experimental import pallas as pl
from jax.experimental.pallas import tpu as pltpu

F32 = jnp.float32
BF16 = jnp.bfloat16
EPS = 1e-6
N_DEV = 8
N_CHIP = 4
V7X_VMEM_LIMIT = 60 * 2**20
MESH = pl.DeviceIdType.MESH
LANES = 128
SUBLANES = 8

DEPTH = 4
D_MODEL = 1024
ATTN_HEADS, ATTN_KV_HEADS, HEAD_DIM, ATTN_BLOCK = 8, 2, 64, 128
DN_HEADS, DN_D, DN_CHUNK, DN_CONV = 8, 64, 64, 4
CONV_WIDTH = 31
Q_A, KV_A, QKV_B, V_B = 512, 128, 1536, 512
IN_COLS = 2832
IN_COLS_PAD = 3072
OFF_QKVB = Q_A + 2 * KV_A
OFF_Z = OFF_QKVB + QKV_B
OFF_BETA = OFF_Z + V_B
OFF_A = OFF_BETA + DN_HEADS

FWD_CARRY_BYTES = {"F": 12 * 2**20, "A": 6 * 2**20, "E": 18 * 2**20, "O": 12 * 2**20}
BWD_CARRY_BYTES = {"F": 11 * 2**20, "A": 6 * 2**20, "E": 13 * 2**20, "O": 10 * 2**20}

ADAM_SLOT_BLOCK = 3 * 2**19

ADAM_LR, ADAM_B1, ADAM_B2, ADAM_EPS, ADAM_WD, ADAM_STEP = 0.001, 0.9, 0.999, 1e-08, 0.01, 10


def _cparams(sem):
    return pltpu.CompilerParams(dimension_semantics=sem, vmem_limit_bytes=V7X_VMEM_LIMIT)


def _sigmoid(x):
    return 1.0 / (1.0 + jnp.exp(-x))


def _softplus(x):
    return jnp.maximum(x, 0.0) + jnp.log(1.0 + jnp.exp(-jnp.abs(x)))


def _dot(a, b):
    return jnp.dot(a, b, preferred_element_type=F32)


def _dot_nt(a, b):
    return lax.dot_general(a, b, (((1,), (1,)), ((), ())), preferred_element_type=F32)


def _dot_tn(a, b):
    return lax.dot_general(a, b, (((0,), (0,)), ((), ())), preferred_element_type=F32)


def _rms(x, w):
    return x * lax.rsqrt(jnp.mean(x * x, axis=-1, keepdims=True) + EPS) * w


def _rms_bwd(x, w, dxn):
    r = lax.rsqrt(jnp.mean(x * x, axis=-1, keepdims=True) + EPS)
    xh = x * r
    dxh = dxn * w
    dx = r * (dxh - xh * jnp.mean(dxh * xh, axis=-1, keepdims=True))
    return dx, jnp.sum(dxn * xh, axis=0, keepdims=True)


def _position():
    return lax.axis_index("x"), lax.axis_index("y"), lax.axis_index("c")


def _dev_index(px, py, pc):
    return 4 * px + 2 * py + pc


def _rcopy(src, dst, send_sem, recv_sem, to):
    return pltpu.make_async_remote_copy(src_ref=src, dst_ref=dst, send_sem=send_sem, recv_sem=recv_sem,
                                        device_id=to, device_id_type=MESH)


def _ag_start(srcs, outs, send, recv, local):
    x, y, c = _position()
    me = _dev_index(x, y, c)
    chips = [(1 - x, y), (x, 1 - y), (1 - x, 1 - y)]
    for a, (src, out) in enumerate(zip(srcs, outs)):
        pltpu.make_async_copy(src, out.at[me], local.at[a]).start()
        _rcopy(src, out.at[me], send.at[a, 0], recv.at[a, 0], (x, y, 1 - c)).start()
        for j, chip in enumerate(chips):
            _rcopy(src, out.at[me], send.at[a, 1 + j], recv.at[a, 1 + j], (*chip, c)).start()


def _ag_finish(srcs, outs, send, recv, local):
    x, y, c = _position()
    me = _dev_index(x, y, c)
    sibling = (x, y, 1 - c)
    chips = [(1 - x, y), (x, 1 - y), (1 - x, 1 - y)]
    for j, chip in enumerate(chips):
        for a, out in enumerate(outs):
            blk = out.at[_dev_index(*chip, c)]
            _rcopy(blk, blk, send.at[a, 1 + j], recv.at[a, 1 + j], (x, y, c)).wait_recv()
            _rcopy(blk, blk, send.at[a, 4 + j], recv.at[a, 4 + j], sibling).start()
    for a, (src, out) in enumerate(zip(srcs, outs)):
        blk = out.at[_dev_index(x, y, 1 - c)]
        _rcopy(blk, blk, send.at[a, 0], recv.at[a, 0], (x, y, c)).wait_recv()
        for j, chip in enumerate(chips):
            blk = out.at[_dev_index(*chip, 1 - c)]
            _rcopy(blk, blk, send.at[a, 4 + j], recv.at[a, 4 + j], (x, y, c)).wait_recv()
        for k in range(N_DEV - 1):
            _rcopy(out.at[me], out.at[me], send.at[a, k], recv.at[a, k], (x, y, c)).wait_send()
        pltpu.make_async_copy(src, out.at[me], local.at[a]).wait()


def _rs_peer(r):
    x, y, c = _position()
    return x ^ ((r >> 2) & 1), y ^ ((r >> 1) & 1), c ^ (r & 1)


def _rs_start(ins, outs, send, recv, local):
    me = _dev_index(*_position())
    for a, (src, out) in enumerate(zip(ins, outs)):
        pltpu.make_async_copy(src.at[me], out.at[me], local.at[a]).start()
        for r in range(1, N_DEV):
            p = _rs_peer(r)
            _rcopy(src.at[_dev_index(*p)], out.at[me], send.at[a, r - 1], recv.at[a, r - 1], p).start()


def _rs_finish(ins, outs, send, recv, local):
    pos = _position()
    me = _dev_index(*pos)
    for a, (src, out) in enumerate(zip(ins, outs)):
        for r in range(1, N_DEV):
            blk = out.at[_dev_index(*_rs_peer(r))]
            _rcopy(blk, blk, send.at[a, r - 1], recv.at[a, r - 1], pos).wait_recv()
        for r in range(1, N_DEV):
            _rcopy(src.at[me], out.at[me], send.at[a, r - 1], recv.at[a, r - 1], pos).wait_send()
        pltpu.make_async_copy(src.at[me], out.at[me], local.at[a]).wait()


def _sw_start(ins, outs, send, recv):
    x, y, c = _position()
    for a, (src, out) in enumerate(zip(ins, outs)):
        for q in range(N_CHIP):
            _rcopy(src.at[2 * q + (1 - c)], out.at[q], send.at[a, q], recv.at[a, q], (x, y, 1 - c)).start()


def _sw_finish(ins, outs, send, recv):
    pos = _position()
    for a, out in enumerate(outs):
        for q in range(N_CHIP):
            _rcopy(out.at[q], out.at[q], send.at[a, q], recv.at[a, q], pos).wait_recv()
        for q in range(N_CHIP):
            _rcopy(out.at[q], out.at[q], send.at[a, q], recv.at[a, q], pos).wait_send()


def _r4_peer(r):
    x, y, c = _position()
    return x ^ ((r >> 1) & 1), y ^ (r & 1), c


def _r4_start(ins, outs, send, recv, local):
    x, y, c = _position()
    mine = 2 * x + y
    for a, (src, out) in enumerate(zip(ins, outs)):
        pltpu.make_async_copy(src.at[mine], out.at[mine], local.at[a]).start()
        for r in range(1, N_CHIP):
            px, py, pc = _r4_peer(r)
            _rcopy(src.at[2 * px + py], out.at[mine], send.at[a, r - 1], recv.at[a, r - 1], (px, py, pc)).start()


def _r4_finish(ins, outs, send, recv, local):
    x, y, c = _position()
    mine = 2 * x + y
    for a, (src, out) in enumerate(zip(ins, outs)):
        for r in range(1, N_CHIP):
            px, py, _ = _r4_peer(r)
            blk = out.at[2 * px + py]
            _rcopy(blk, blk, send.at[a, r - 1], recv.at[a, r - 1], (x, y, c)).wait_recv()
        for r in range(1, N_CHIP):
            _rcopy(src.at[mine], out.at[mine], send.at[a, r - 1], recv.at[a, r - 1], (x, y, c)).wait_send()
        pltpu.make_async_copy(src.at[mine], out.at[mine], local.at[a]).wait()


_RS_KINDS = {
    "direct": (_rs_start, _rs_finish, lambda n: [(n, N_DEV - 1), (n, N_DEV - 1), (n,)], lambda s: s),
    "swap": (_sw_start, _sw_finish, lambda n: [(n, N_CHIP), (n, N_CHIP)], lambda s: (N_CHIP,) + s[1:]),
    "chips": (_r4_start, _r4_finish, lambda n: [(n, N_CHIP - 1), (n, N_CHIP - 1), (n,)], lambda s: s),
}


def _pcall(body, args, *, name, grid, in_specs, out_specs, out_shape, sem, scratch_shapes=(), ag=(), rs=()):
    na, nr = len(ag), len(rs)
    if na + nr == 0:
        outs = pl.pallas_call(body, name=name, grid=grid, in_specs=in_specs, out_specs=out_specs,
                              out_shape=out_shape, scratch_shapes=list(scratch_shapes),
                              compiler_params=_cparams(sem))(*args)
        return list(outs), [], []
    n_in, n_out, n_scr = len(in_specs), len(out_specs), len(scratch_shapes)
    ag_idx = [i for _, i in ag]
    groups = [(k, [i for i, (kk, _) in enumerate(rs) if kk == k]) for k in _RS_KINDS]
    groups = [(k, idx) for k, idx in groups if idx]
    sem_counts = ([3] if na else []) + [len(_RS_KINDS[k][2](1)) for k, _ in groups]

    def wrapped(*refs):
        cin, refs = refs[:n_in], refs[n_in:]
        ag_in, refs = refs[:na], refs[na:]
        rs_in, refs = refs[:nr], refs[nr:]
        cout, refs = refs[:n_out], refs[n_out:]
        ag_out, refs = refs[:na], refs[na:]
        rs_out, refs = refs[:nr], refs[nr:]
        cscr, sems = refs[:n_scr], list(refs[n_scr:])
        sem_sets = [[sems.pop(0) for _ in range(n)] for n in sem_counts]
        ag_sems = sem_sets.pop(0) if na else None
        ag_src = [r if i is None else r.at[i] for r, i in zip(ag_in, ag_idx)]
        ids = [pl.program_id(d) for d in range(len(grid))]
        first = functools.reduce(jnp.logical_and, [i == 0 for i in ids])
        last = functools.reduce(jnp.logical_and, [i == g - 1 for i, g in zip(ids, grid)])

        def run(phase):
            if na:
                (_ag_start, _ag_finish)[phase](ag_src, ag_out, *ag_sems)
            for (k, idx), ss in zip(groups, sem_sets):
                _RS_KINDS[k][phase]([rs_in[i] for i in idx], [rs_out[i] for i in idx], *ss)

        @pl.when(first)
        def _():
            run(0)

        body(*cin, *cout, *cscr)

        @pl.when(last)
        def _():
            run(1)

    hbm = pl.BlockSpec(memory_space=pl.ANY)
    sem_shapes = [pltpu.SemaphoreType.DMA(s) for s in ([(na, N_DEV - 1), (na, N_DEV - 1), (na,)] if na else [])]
    for k, idx in groups:
        sem_shapes += [pltpu.SemaphoreType.DMA(s) for s in _RS_KINDS[k][2](len(idx))]
    outs = pl.pallas_call(
        wrapped, name=name, grid=grid,
        in_specs=list(in_specs) + [hbm] * (na + nr),
        out_specs=list(out_specs) + [hbm] * (na + nr),
        out_shape=list(out_shape)
        + [jax.ShapeDtypeStruct((N_DEV,) + a.shape[-2:], a.dtype) for a, _ in ag]
        + [jax.ShapeDtypeStruct(_RS_KINDS[k][3](b.shape), b.dtype) for k, b in rs],
        scratch_shapes=list(scratch_shapes) + sem_shapes,
        compiler_params=_cparams(sem),
    )(*args, *[a for a, _ in ag], *[b for _, b in rs])
    return list(outs[:n_out]), list(outs[n_out:n_out + na]), list(outs[n_out + na:])


def exchange(ag):
    def body(o_ref):
        o_ref[...] = jnp.zeros_like(o_ref)

    _, gathered, _ = _pcall(body, (), name="exchange", grid=(1,), in_specs=[],
                            out_specs=[pl.BlockSpec((8, LANES), lambda i: (0, 0))],
                            out_shape=[jax.ShapeDtypeStruct((8, LANES), F32)], sem=("arbitrary",), ag=ag)
    return gathered


def pair_add(blocks, received):
    n = len(blocks)

    def body(*refs):
        c = lax.axis_index("c")
        for g_ref, p_ref, o_ref in zip(refs[:n], refs[n:2 * n], refs[2 * n:]):
            mine = jnp.where(c == 0, g_ref[0, 0], g_ref[0, 1])
            o_ref[0] = (mine.astype(F32) + p_ref[0].astype(F32)).astype(BF16)

    halves = 2
    g_specs = [pl.BlockSpec((1, 2, b.shape[1] // halves, b.shape[2]), lambda q, r: (q, 0, r, 0)) for b in blocks]
    p_specs = [pl.BlockSpec((1, b.shape[1] // halves, b.shape[2]), lambda q, r: (q, r, 0)) for b in blocks]
    return pl.pallas_call(
        body, name="pair_add", grid=(N_CHIP, halves),
        in_specs=g_specs + p_specs, out_specs=p_specs,
        out_shape=[jax.ShapeDtypeStruct(p.shape, BF16) for p in received],
        compiler_params=_cparams(("parallel", "parallel")),
    )(*[b.reshape((N_CHIP, 2) + b.shape[1:]) for b in blocks], *received)


FFN_PAIR = 2


def _pair_cols(w_ref):
    return jnp.concatenate([w_ref[p] for p in range(FFN_PAIR)], axis=1)


def ffn_fwd(x, nw, wg, wu, wd, ag=()):
    T, D = x.shape
    F = wg.shape[2]
    P = FFN_PAIR
    J = wg.shape[0] // P
    tm = min(T, 1024)

    def body(x_ref, nw_ref, wg_ref, wu_ref, wd_ref, o_ref, xn_ref, acc_ref):
        j = pl.program_id(1)

        @pl.when(j == 0)
        def _():
            xn_ref[...] = _rms(x_ref[...], nw_ref[...]).astype(BF16)
            acc_ref[...] = jnp.zeros_like(acc_ref)

        xn = xn_ref[...]
        g = _dot(xn, _pair_cols(wg_ref))
        u = _dot(xn, _pair_cols(wu_ref))
        h = (g * _sigmoid(g) * u).astype(BF16)
        acc_ref[...] += _dot(h, wd_ref[...].reshape(P * F, D))

        @pl.when(j == J - 1)
        def _():
            o_ref[...] = x_ref[...] + 0.5 * acc_ref[...]

    (out,), gathered, _ = _pcall(
        body, (x, nw, wg, wu, wd), name="ffn_fwd", grid=(T // tm, J),
        in_specs=[pl.BlockSpec((tm, D), lambda t, j: (t, 0)),
                  pl.BlockSpec((1, D), lambda t, j: (0, 0)),
                  pl.BlockSpec((P, D, F), lambda t, j: (j, 0, 0)),
                  pl.BlockSpec((P, D, F), lambda t, j: (j, 0, 0)),
                  pl.BlockSpec((P, F, D), lambda t, j: (j, 0, 0))],
        out_specs=[pl.BlockSpec((tm, D), lambda t, j: (t, 0))],
        out_shape=[jax.ShapeDtypeStruct((T, D), F32)],
        scratch_shapes=[pltpu.VMEM((tm, D), BF16), pltpu.VMEM((tm, D), F32)],
        sem=("arbitrary", "arbitrary"), ag=ag)
    return out, gathered


def ffn_bwd(x, dy, nw, wg, wu, wd, rs=()):
    T, D = x.shape
    F = wg.shape[2]
    P = FFN_PAIR
    J = wg.shape[0] // P
    tm = min(T, 256)
    nt = T // tm

    def body(x_ref, dy_ref, nw_ref, wg_ref, wu_ref, wd_ref,
             dx_ref, dwg_ref, dwu_ref, dwd_ref, dnw_ref,
             xn_ref, dyh_ref, dxn_ref, awg_ref, awu_ref, awd_ref):
        j = pl.program_id(0)
        t = pl.program_id(1)
        rows = pl.ds(pl.multiple_of(t * tm, tm), tm)

        @pl.when(j == 0)
        def _():
            xn_ref[rows, :] = _rms(x_ref[...], nw_ref[...]).astype(BF16)
            dyh_ref[rows, :] = (0.5 * dy_ref[...]).astype(BF16)
            dxn_ref[rows, :] = jnp.zeros((tm, D), F32)

        @pl.when((j == 0) & (t == 0))
        def _():
            dnw_ref[...] = jnp.zeros_like(dnw_ref)

        @pl.when(t == 0)
        def _():
            awg_ref[...] = jnp.zeros_like(awg_ref)
            awu_ref[...] = jnp.zeros_like(awu_ref)
            awd_ref[...] = jnp.zeros_like(awd_ref)

        xn = xn_ref[rows, :]
        dyh = dyh_ref[rows, :]
        wg2, wu2 = _pair_cols(wg_ref), _pair_cols(wu_ref)
        g = _dot(xn, wg2)
        u = _dot(xn, wu2)
        sg = _sigmoid(g)
        s = g * sg
        h = (s * u).astype(BF16)
        dh = _dot_nt(dyh, wd_ref[...].reshape(P * F, D))
        du = (dh * s).astype(BF16)
        dg = (dh * u * (sg * (1.0 + g * (1.0 - sg)))).astype(BF16)
        awd_ref[...] += _dot_tn(h, dyh)
        awg_ref[...] += _dot_tn(dg, xn)
        awu_ref[...] += _dot_tn(du, xn)
        dxn_ref[rows, :] += _dot_nt(dg, wg2) + _dot_nt(du, wu2)

        @pl.when(t == nt - 1)
        def _():
            dwg_ref[...] = awg_ref[...].astype(BF16).reshape(P, F, D)
            dwu_ref[...] = awu_ref[...].astype(BF16).reshape(P, F, D)
            dwd_ref[...] = awd_ref[...].astype(BF16).reshape(P, F, D)

        @pl.when(j == J - 1)
        def _():
            dx, dnw = _rms_bwd(x_ref[...], nw_ref[...], dxn_ref[rows, :])
            dx_ref[...] = dy_ref[...] + dx
            dnw_ref[...] += dnw

    ends = lambda j, t: (jnp.where((j == 0) | (j == J - 1), t, 0), 0)
    last = lambda j, t: (jnp.where(j == J - 1, t, 0), 0)
    outs, _, slots = _pcall(
        body, (x, dy, nw, wg, wu, wd), name="ffn_bwd", grid=(J, nt),
        in_specs=[pl.BlockSpec((tm, D), ends), pl.BlockSpec((tm, D), ends),
                  pl.BlockSpec((1, D), lambda j, t: (0, 0)),
                  pl.BlockSpec((P, D, F), lambda j, t: (j, 0, 0)),
                  pl.BlockSpec((P, D, F), lambda j, t: (j, 0, 0)),
                  pl.BlockSpec((P, F, D), lambda j, t: (j, 0, 0))],
        out_specs=[pl.BlockSpec((tm, D), last),
                   pl.BlockSpec((P, F, D), lambda j, t: (j, 0, 0)),
                   pl.BlockSpec((P, F, D), lambda j, t: (j, 0, 0)),
                   pl.BlockSpec((P, F, D), lambda j, t: (j, 0, 0)),
                   pl.BlockSpec((1, D), lambda j, t: (0, 0))],
        out_shape=[jax.ShapeDtypeStruct((T, D), F32)] + [jax.ShapeDtypeStruct((P * J, F, D), BF16)] * 3
        + [jax.ShapeDtypeStruct((1, D), F32)],
        scratch_shapes=[pltpu.VMEM((T, D), BF16), pltpu.VMEM((T, D), BF16), pltpu.VMEM((T, D), F32)]
        + [pltpu.VMEM((P * F, D), F32)] * 3,
        sem=("arbitrary", "arbitrary"), rs=rs)
    return outs, slots


def rmslin_fwd(x, nw, w, b):
    T, D = x.shape
    N = w.shape[1]
    tm = min(T, 256)

    def body(x_ref, nw_ref, w_ref, b_ref, o_ref):
        xn = _rms(x_ref[...], nw_ref[...]).astype(BF16)
        o_ref[...] = _dot(xn, w_ref[...]) + b_ref[...]

    return pl.pallas_call(
        body, name="rmslin_fwd", grid=(T // tm,),
        in_specs=[pl.BlockSpec((tm, D), lambda t: (t, 0)), pl.BlockSpec((1, D), lambda t: (0, 0)),
                  pl.BlockSpec((D, N), lambda t: (0, 0)), pl.BlockSpec((1, N), lambda t: (0, 0))],
        out_specs=pl.BlockSpec((tm, N), lambda t: (t, 0)),
        out_shape=jax.ShapeDtypeStruct((T, N), F32),
        compiler_params=_cparams(("parallel",)),
    )(x, nw, w, b)


def rmslin_bwd(x, dres, dproj, nw, w):
    T, D = x.shape
    N = w.shape[1]
    nb = 1024
    nc = N // nb
    tm = min(T, 256)
    nt = T // tm

    def body(x_ref, dres_ref, dp_ref, nw_ref, w_ref, dx_ref, dw_ref, db_ref, dnw_ref, xn_ref, dxn_ref):
        c = pl.program_id(0)
        t = pl.program_id(1)
        rows = pl.ds(pl.multiple_of(t * tm, tm), tm)

        @pl.when(c == 0)
        def _():
            xn_ref[rows, :] = _rms(x_ref[...], nw_ref[...]).astype(BF16)
            dxn_ref[rows, :] = jnp.zeros((tm, D), F32)

        @pl.when((c == 0) & (t == 0))
        def _():
            dnw_ref[...] = jnp.zeros_like(dnw_ref)

        @pl.when(t == 0)
        def _():
            dw_ref[...] = jnp.zeros_like(dw_ref)
            db_ref[...] = jnp.zeros_like(db_ref)

        dpf = dp_ref[...]
        dp = dpf.astype(BF16)
        dw_ref[...] += _dot_tn(xn_ref[rows, :], dp)
        db_ref[...] += jnp.sum(dpf, axis=0, keepdims=True)
        dxn_ref[rows, :] += _dot_nt(dp, w_ref[...])

        @pl.when(c == nc - 1)
        def _():
            dx, dnw = _rms_bwd(x_ref[...], nw_ref[...], dxn_ref[rows, :])
            dx_ref[...] = dres_ref[...] + dx
            dnw_ref[...] += dnw

    ends = lambda c, t: (jnp.where((c == 0) | (c == nc - 1), t, 0), 0)
    last = lambda c, t: (jnp.where(c == nc - 1, t, 0), 0)
    return pl.pallas_call(
        body, name="rmslin_bwd", grid=(nc, nt),
        in_specs=[pl.BlockSpec((tm, D), ends), pl.BlockSpec((tm, D), last),
                  pl.BlockSpec((tm, nb), lambda c, t: (t, c)),
                  pl.BlockSpec((1, D), lambda c, t: (0, 0)),
                  pl.BlockSpec((D, nb), lambda c, t: (0, c))],
        out_specs=[pl.BlockSpec((tm, D), last),
                   pl.BlockSpec((D, nb), lambda c, t: (0, c)),
                   pl.BlockSpec((1, nb), lambda c, t: (0, c)),
                   pl.BlockSpec((1, D), lambda c, t: (0, 0))],
        out_shape=[jax.ShapeDtypeStruct((T, D), F32), jax.ShapeDtypeStruct((D, N), F32),
                   jax.ShapeDtypeStruct((1, N), F32), jax.ShapeDtypeStruct((1, D), F32)],
        scratch_shapes=[pltpu.VMEM((T, D), BF16), pltpu.VMEM((T, D), F32)],
        compiler_params=_cparams(("arbitrary", "arbitrary")),
    )(x, dres, dproj, nw, w)


def lin_fwd(res, parts, w, b):
    T = res.shape[0]
    K, N = w.shape
    tm = min(T, 512)
    n = len(parts)
    offs = [sum(p.shape[1] for p in parts[:i]) for i in range(n + 1)]

    def body(res_ref, *refs):
        a_refs, (w_ref, b_ref, o_ref) = refs[:n], refs[n:]
        acc = res_ref[...] + b_ref[...]
        for i, a_ref in enumerate(a_refs):
            acc = acc + _dot(a_ref[...].astype(BF16), w_ref[offs[i]:offs[i + 1], :])
        o_ref[...] = acc

    return pl.pallas_call(
        body, name="lin_fwd", grid=(T // tm,),
        in_specs=[pl.BlockSpec((tm, N), lambda t: (t, 0))]
        + [pl.BlockSpec((tm, p.shape[1]), lambda t: (t, 0)) for p in parts]
        + [pl.BlockSpec((K, N), lambda t: (0, 0)), pl.BlockSpec((1, N), lambda t: (0, 0))],
        out_specs=pl.BlockSpec((tm, N), lambda t: (t, 0)),
        out_shape=jax.ShapeDtypeStruct((T, N), F32),
        compiler_params=_cparams(("parallel",)),
    )(res, *parts, w, b)


def lin_bwd(parts, dy, w):
    T = dy.shape[0]
    K, N = w.shape
    tm = min(T, 256)
    n = len(parts)
    offs = [sum(p.shape[1] for p in parts[:i]) for i in range(n + 1)]

    def body(*refs):
        a_refs, (dy_ref, w_ref, da_ref, dw_ref, db_ref) = refs[:n], refs[n:]

        @pl.when(pl.program_id(0) == 0)
        def _():
            dw_ref[...] = jnp.zeros_like(dw_ref)
            db_ref[...] = jnp.zeros_like(db_ref)

        dyf = dy_ref[...]
        dyb = dyf.astype(BF16)
        da_ref[...] = _dot_nt(dyb, w_ref[...])
        for i, a_ref in enumerate(a_refs):
            dw_ref[offs[i]:offs[i + 1], :] += _dot_tn(a_ref[...].astype(BF16), dyb)
        db_ref[...] += jnp.sum(dyf, axis=0, keepdims=True)

    return pl.pallas_call(
        body, name="lin_bwd", grid=(T // tm,),
        in_specs=[pl.BlockSpec((tm, p.shape[1]), lambda t: (t, 0)) for p in parts]
        + [pl.BlockSpec((tm, N), lambda t: (t, 0)), pl.BlockSpec((K, N), lambda t: (0, 0))],
        out_specs=[pl.BlockSpec((tm, K), lambda t: (t, 0)), pl.BlockSpec((K, N), lambda t: (0, 0)),
                   pl.BlockSpec((1, N), lambda t: (0, 0))],
        out_shape=[jax.ShapeDtypeStruct((T, K), F32), jax.ShapeDtypeStruct((K, N), F32),
                   jax.ShapeDtypeStruct((1, N), F32)],
        compiler_params=_cparams(("arbitrary",)),
    )(*parts, dy, w)


def loss_fwd_bwd(x, fw, target):
    T, D = x.shape
    tm = min(T, 256)

    def body(x_ref, fw_ref, tg_ref, loss_ref, dx_ref, dfw_ref):
        @pl.when(pl.program_id(0) == 0)
        def _():
            loss_ref[...] = jnp.zeros_like(loss_ref)
            dfw_ref[...] = jnp.zeros_like(dfw_ref)

        xv = x_ref[...]
        w = fw_ref[...]
        err = _rms(xv, w) - tg_ref[...]
        row = jnp.sum(err * err, axis=-1, keepdims=True)
        loss_ref[...] += (0.5 / D) * jnp.sum(row, axis=0, keepdims=True)
        dx, dfw = _rms_bwd(xv, w, err * (1.0 / D))
        dx_ref[...] = dx
        dfw_ref[...] += dfw

    return pl.pallas_call(
        body, name="loss_fwd_bwd", grid=(T // tm,),
        in_specs=[pl.BlockSpec((tm, D), lambda t: (t, 0)), pl.BlockSpec((1, D), lambda t: (0, 0)),
                  pl.BlockSpec((tm, D), lambda t: (t, 0))],
        out_specs=[pl.BlockSpec((1, 1), lambda t: (0, 0)), pl.BlockSpec((tm, D), lambda t: (t, 0)),
                   pl.BlockSpec((1, D), lambda t: (0, 0))],
        out_shape=[jax.ShapeDtypeStruct((1, 1), F32), jax.ShapeDtypeStruct((T, D), F32),
                   jax.ShapeDtypeStruct((1, D), F32)],
        compiler_params=_cparams(("arbitrary",)),
    )(x, fw, target)


def _attn_masks(n, rows, blk):
    r = lax.broadcasted_iota(jnp.int32, (rows, 2 * blk), 0)
    jj = lax.broadcasted_iota(jnp.int32, (rows, 2 * blk), 1)
    dist = (r % blk) + blk - jj
    valid = (dist >= 0) & (dist < blk) & ((n > 0) | (jj >= blk))
    return dist.astype(F32), valid


def _attn_block(q, kcat, vcat, sink, slope, dist, valid):
    d = q.shape[-1]
    s = _dot_nt(q.astype(BF16), kcat.astype(BF16)) * (d ** -0.5)
    s = jnp.where(valid, s - slope * dist, -1e30)
    m = lax.stop_gradient(jnp.maximum(jnp.max(s, axis=-1, keepdims=True), sink))
    e = jnp.exp(s - m)
    p = e / (jnp.sum(e, axis=-1, keepdims=True) + jnp.exp(sink - m))
    return _dot(p.astype(BF16), vcat.astype(BF16))


ATTN_G = ATTN_HEADS // ATTN_KV_HEADS
ATTN_QW = ATTN_G * HEAD_DIM
ATTN_KCOL = Q_A // KV_A


def _attn_specs():
    blk = ATTN_BLOCK
    qs = pl.BlockSpec((blk, ATTN_QW), lambda h, n: (n, h))
    prev = lambda c: pl.BlockSpec((blk, KV_A), lambda h, n: (jnp.maximum(n - 1, 0), c))
    cur = lambda c: pl.BlockSpec((blk, KV_A), lambda h, n: (n, c))
    rowp = pl.BlockSpec((ATTN_G * blk, 1), lambda h, n: (h, 0))
    return qs, [prev(ATTN_KCOL), cur(ATTN_KCOL), prev(ATTN_KCOL + 1), cur(ATTN_KCOL + 1)], rowp


def _attn_operands(h, q_ref, kp_ref, kc_ref, vp_ref, vc_ref):
    d = HEAD_DIM
    q = jnp.concatenate([q_ref[:, g * d:(g + 1) * d] for g in range(ATTN_G)], axis=0)
    pick = lambda r: jnp.where(h == 0, r[:, :d], r[:, d:])
    kcat = jnp.concatenate([pick(kp_ref[...]), pick(kc_ref[...])], axis=0)
    vcat = jnp.concatenate([pick(vp_ref[...]), pick(vc_ref[...])], axis=0)
    return q, kcat, vcat


def attn_fwd(proj, sink_rows, slope_rows, ag=()):
    T = proj.shape[0]
    blk, d = ATTN_BLOCK, HEAD_DIM

    def body(q_ref, kp_ref, kc_ref, vp_ref, vc_ref, sink_ref, slope_ref, o_ref):
        h, n = pl.program_id(0), pl.program_id(1)
        dist, valid = _attn_masks(n, ATTN_G * blk, blk)
        q, kcat, vcat = _attn_operands(h, q_ref, kp_ref, kc_ref, vp_ref, vc_ref)
        o = _attn_block(q, kcat, vcat, sink_ref[...], slope_ref[...], dist, valid)
        for g in range(ATTN_G):
            o_ref[:, g * d:(g + 1) * d] = o[g * blk:(g + 1) * blk]

    qs, kv, rowp = _attn_specs()
    (out,), gathered, _ = _pcall(
        body, (proj, proj, proj, proj, proj, sink_rows, slope_rows), name="attn_fwd",
        grid=(ATTN_KV_HEADS, T // blk), in_specs=[qs] + kv + [rowp, rowp], out_specs=[qs],
        out_shape=[jax.ShapeDtypeStruct((T, Q_A), F32)], sem=("arbitrary", "arbitrary"), ag=ag)
    return out, gathered


def attn_bwd(proj, sink_rows, slope_rows, dmix, rs=()):
    T = proj.shape[0]
    blk, d = ATTN_BLOCK, HEAD_DIM

    def body(q_ref, kp_ref, kc_ref, vp_ref, vc_ref, sink_ref, slope_ref, do_ref, dq_ref, dkv_ref, dsink_ref):
        h, n = pl.program_id(0), pl.program_id(1)

        @pl.when((h == 0) & (n == 0))
        def _():
            dkv_ref[...] = jnp.zeros_like(dkv_ref)

        @pl.when(n == 0)
        def _():
            dsink_ref[...] = jnp.zeros_like(dsink_ref)

        dist, valid = _attn_masks(n, ATTN_G * blk, blk)
        q, kcat, vcat = _attn_operands(h, q_ref, kp_ref, kc_ref, vp_ref, vc_ref)
        do = jnp.concatenate([do_ref[:, g * d:(g + 1) * d] for g in range(ATTN_G)], axis=0)
        fn = functools.partial(_attn_block, slope=slope_ref[...], dist=dist, valid=valid)
        _, vjp = jax.vjp(fn, q, kcat, vcat, sink_ref[...])
        dq, dkcat, dvcat, dsink = vjp(do)
        for g in range(ATTN_G):
            dq_ref[:, g * d:(g + 1) * d] = dq[g * blk:(g + 1) * blk]
        dsink_ref[...] += dsink
        lane = lax.broadcasted_iota(jnp.int32, (2 * blk, 2 * KV_A), 1)
        mine = (lane % KV_A) // d == h
        both = jnp.where(mine, jnp.concatenate([dkcat, dkcat, dvcat, dvcat], axis=1), 0.0)

        @pl.when(n == 0)
        def _():
            dkv_ref[0:blk, :] += both[blk:]

        @pl.when(n > 0)
        def _():
            rows = pl.ds(pl.multiple_of((n - 1) * blk, blk), 2 * blk)
            dkv_ref[rows, :] += both

    qs, kv, rowp = _attn_specs()
    outs, _, slots = _pcall(
        body, (proj, proj, proj, proj, proj, sink_rows, slope_rows, dmix), name="attn_bwd",
        grid=(ATTN_KV_HEADS, T // blk), in_specs=[qs] + kv + [rowp, rowp, qs],
        out_specs=[qs, pl.BlockSpec((T, 2 * KV_A), lambda h, n: (0, 0)), rowp],
        out_shape=[jax.ShapeDtypeStruct((T, Q_A), F32), jax.ShapeDtypeStruct((T, 2 * KV_A), F32),
                   jax.ShapeDtypeStruct((ATTN_HEADS * blk, 1), F32)],
        sem=("arbitrary", "arbitrary"), rs=rs)
    return outs, slots


_NN = (((2,), (1,)), ((0,), (0,)))
_NT = (((2,), (2,)), ((0,), (0,)))
_TN = (((1,), (1,)), ((0,), (0,)))


def _bmm(a, b, dims):
    return lax.dot_general(a.astype(BF16), b.astype(BF16), dims, preferred_element_type=F32)


def _split(x, terms):
    out = []
    for _ in range(terms):
        t = x.astype(BF16)
        out.append(t)
        x = x - t.astype(F32)
    return out


def _fine_product(a, b, dims):
    (ah, al), (bh, bl) = _split(a, 2), _split(b, 2)
    dot = lambda x, y: lax.dot_general(x, y, dims, preferred_element_type=F32)
    return dot(ah, bh) + (dot(ah, bl) + dot(al, bh))


def _mask_product(mask, x, dims):
    mb = mask.astype(BF16)
    parts = [lax.dot_general(mb, t, dims, preferred_element_type=F32) for t in _split(x, 3)]
    return parts[0] + (parts[1] + parts[2])


@jax.custom_vjp
def _fine_nt(a, b):
    return _fine_product(a, b, _NT)


_fine_nt.defvjp(lambda a, b: (_fine_product(a, b, _NT), (a, b)),
                lambda res, ct: (_fine_product(ct, res[1], _NN), _fine_product(ct, res[0], _TN)))


@jax.custom_vjp
def _mask_nn(mask, x):
    return _mask_product(mask, x, _NN)


_mask_nn.defvjp(lambda mask, x: (_mask_product(mask, x, _NN), mask),
                lambda mask, ct: (jnp.zeros_like(mask), _mask_product(mask, ct, _TN)))


@jax.custom_vjp
def _unit_lower_inverse(low):
    n = low.shape[-1]
    eye = (lax.broadcasted_iota(jnp.int32, low.shape, 1) == lax.broadcasted_iota(jnp.int32, low.shape, 2)).astype(F32)
    tinv = eye - low
    p = low
    for _ in range(n.bit_length() - 2):
        p = _bmm(p, p, _NN)
        tinv = tinv + _bmm(tinv, p, _NN)
    return tinv


def _unit_lower_inverse_fwd(low):
    tinv = _unit_lower_inverse(low)
    return tinv, tinv


_unit_lower_inverse.defvjp(_unit_lower_inverse_fwd, lambda tinv, ct: (-_bmm(_bmm(tinv, ct, _TN), tinv, _NT),))


def _dn_chunk(qc, kc, vc, zc, braw, araw, alog, dtb, nw, S):
    H, C, D = qc.shape
    row = lax.broadcasted_iota(jnp.int32, (H, C, C), 1)
    col = lax.broadcasted_iota(jnp.int32, (H, C, C), 2)
    causal = row >= col
    strict = row > col
    eye = (row == col).astype(F32)

    q = qc * lax.rsqrt(jnp.sum(qc * qc, axis=-1, keepdims=True) + EPS) * (D ** -0.5)
    k = kc * lax.rsqrt(jnp.sum(kc * kc, axis=-1, keepdims=True) + EPS)
    beta = _sigmoid(braw)
    g = -jnp.exp(alog) * _softplus(araw + dtb)
    a_col = _mask_nn(causal.astype(F32), jnp.broadcast_to(g, (H, C, C)))
    a_row = _mask_nn(jnp.ones((H, C, C), F32), eye * a_col)
    decay = jnp.where(causal, jnp.exp(jnp.where(causal, a_col - a_row, 0.0)), 0.0)
    kb = k * beta
    tinv = _unit_lower_inverse(jnp.where(strict, _fine_nt(kb, k) * decay, 0.0))
    e_col = jnp.exp(a_col)
    u = _bmm(tinv, vc * beta, _NN)
    w = _bmm(tinv, kb * e_col, _NN)
    attn = _fine_nt(q, k) * decay
    gl = a_col[:, C - 1:C, :]
    k_dec = k * jnp.exp(gl - a_col)
    v_new = u - _bmm(w, S, _NN)
    o = _bmm(q * e_col, S, _NN) + _bmm(attn, v_new, _NN)
    s_new = S * jnp.exp(jnp.broadcast_to(gl, (H, D, D))) + _bmm(k_dec, v_new, _TN)
    on = o * lax.rsqrt(jnp.mean(o * o, axis=-1, keepdims=True) + EPS) * nw
    return on * (zc * _sigmoid(zc)), s_new


DN_ZCOLS = IN_COLS_PAD - OFF_Z
DN_ZBLK = OFF_Z // DN_ZCOLS


def _dn_heads(a, off):
    return jnp.stack([a[:, off + h * DN_D:off + (h + 1) * DN_D] for h in range(DN_HEADS)])


def _dn_gate_cols(zb, off):
    return jnp.stack([zb[:, off + h:off + h + 1] for h in range(DN_HEADS)])


DN_STEP_CHUNKS = 4


def _dn_operands(x_ref, zb_ref, rows):
    x, zb = x_ref[rows, :], zb_ref[rows, :]
    return (_dn_heads(x, 0), _dn_heads(x, V_B), _dn_heads(x, 2 * V_B), _dn_heads(zb, 0),
            _dn_gate_cols(zb, V_B), _dn_gate_cols(zb, V_B + DN_HEADS))


def dn_fwd(qkvc, proj, alog, dtb, nw, ag=()):
    T = qkvc.shape[0]
    H, C, D, G = DN_HEADS, DN_CHUNK, DN_D, DN_STEP_CHUNKS
    N = T // C

    def body(x_ref, zb_ref, alog_ref, dtb_ref, nw_ref, o_ref, sall_ref, s_ref):
        @pl.when(pl.program_id(0) == 0)
        def _():
            s_ref[...] = jnp.zeros_like(s_ref)

        s = s_ref[...]
        for c in range(G):
            rows = slice(c * C, (c + 1) * C)
            sall_ref[c] = s
            on, s = _dn_chunk(*_dn_operands(x_ref, zb_ref, rows), alog_ref[...], dtb_ref[...], nw_ref[...], s)
            for h in range(H):
                o_ref[rows, h * D:(h + 1) * D] = on[h]
        s_ref[...] = s

    par = pl.BlockSpec((H, 1, 1), lambda n: (0, 0, 0))
    outs, gathered, _ = _pcall(
        body, (qkvc, proj, alog, dtb, nw), name="dn_fwd", grid=(N // G,),
        in_specs=[pl.BlockSpec((G * C, QKV_B), lambda n: (n, 0)),
                  pl.BlockSpec((G * C, DN_ZCOLS), lambda n: (n, DN_ZBLK)),
                  par, par, pl.BlockSpec((1, 1, D), lambda n: (0, 0, 0))],
        out_specs=[pl.BlockSpec((G * C, V_B), lambda n: (n, 0)), pl.BlockSpec((G, H, D, D), lambda n: (n, 0, 0, 0))],
        out_shape=[jax.ShapeDtypeStruct((T, V_B), F32), jax.ShapeDtypeStruct((N, H, D, D), F32)],
        scratch_shapes=[pltpu.VMEM((H, D, D), F32)], sem=("arbitrary",), ag=ag)
    return outs, gathered


def dn_bwd(qkvc, proj, alog, dtb, nw, sall, dmix, rs=()):
    T = qkvc.shape[0]
    H, C, D, G = DN_HEADS, DN_CHUNK, DN_D, DN_STEP_CHUNKS
    N = T // C // G

    def body(x_ref, zb_ref, alog_ref, dtb_ref, nw_ref, sall_ref, do_ref,
             dx_ref, dzb_ref, dalog_ref, ddtb_ref, dnw_ref, ds_ref):
        @pl.when(pl.program_id(0) == 0)
        def _():
            ds_ref[...] = jnp.zeros_like(ds_ref)
            dalog_ref[...] = jnp.zeros_like(dalog_ref)
            ddtb_ref[...] = jnp.zeros_like(ddtb_ref)
            dnw_ref[...] = jnp.zeros_like(dnw_ref)

        ds = ds_ref[...]
        lane = lax.broadcasted_iota(jnp.int32, (C, LANES), 1)
        for c in reversed(range(G)):
            rows = slice(c * C, (c + 1) * C)
            args = (*_dn_operands(x_ref, zb_ref, rows), alog_ref[...], dtb_ref[...], nw_ref[...], sall_ref[c])
            _, vjp = jax.vjp(_dn_chunk, *args)
            dq, dk, dv, dz, db, da, dalog, ddtb, dnw, ds = vjp((_dn_heads(do_ref[rows, :], 0), ds))
            for h in range(H):
                dx_ref[rows, h * D:(h + 1) * D] = dq[h]
                dx_ref[rows, V_B + h * D:V_B + (h + 1) * D] = dk[h]
                dx_ref[rows, 2 * V_B + h * D:2 * V_B + (h + 1) * D] = dv[h]
                dzb_ref[rows, h * D:(h + 1) * D] = dz[h]
            tail = jnp.zeros((C, LANES), F32)
            for h in range(H):
                tail = tail + jnp.where(lane == h, jnp.broadcast_to(db[h], (C, LANES)), 0.0)
                tail = tail + jnp.where(lane == H + h, jnp.broadcast_to(da[h], (C, LANES)), 0.0)
            dzb_ref[rows, V_B:V_B + LANES] = tail
            dzb_ref[rows, V_B + LANES:] = jnp.zeros((C, DN_ZCOLS - V_B - LANES), F32)
            dalog_ref[...] += dalog
            ddtb_ref[...] += ddtb
            dnw_ref[...] += dnw
        ds_ref[...] = ds

    par = pl.BlockSpec((H, 1, 1), lambda i: (0, 0, 0))
    nws = pl.BlockSpec((1, 1, D), lambda i: (0, 0, 0))
    outs, _, slots = _pcall(
        body, (qkvc, proj, alog, dtb, nw, sall, dmix), name="dn_bwd", grid=(N,),
        in_specs=[pl.BlockSpec((G * C, QKV_B), lambda i: (N - 1 - i, 0)),
                  pl.BlockSpec((G * C, DN_ZCOLS), lambda i: (N - 1 - i, DN_ZBLK)), par, par, nws,
                  pl.BlockSpec((G, H, D, D), lambda i: (N - 1 - i, 0, 0, 0)),
                  pl.BlockSpec((G * C, V_B), lambda i: (N - 1 - i, 1))],
        out_specs=[pl.BlockSpec((G * C, QKV_B), lambda i: (N - 1 - i, 0)),
                   pl.BlockSpec((G * C, DN_ZCOLS), lambda i: (N - 1 - i, 0)), par, par, nws],
        out_shape=[jax.ShapeDtypeStruct((T, QKV_B), F32), jax.ShapeDtypeStruct((T, DN_ZCOLS), F32)]
        + [jax.ShapeDtypeStruct((H, 1, 1), F32)] * 2 + [jax.ShapeDtypeStruct((1, 1, D), F32)],
        scratch_shapes=[pltpu.VMEM((H, D, D), F32)], sem=("arbitrary",), rs=rs)
    return outs, slots


def _conv_taps(buf_ref, w, width, halo, tm):
    acc = None
    for kk, win in _windows(buf_ref, [halo - (width - 1) + kk for kk in range(width)], tm):
        term = w[kk:kk + 1, :] * win
        acc = term if acc is None else acc + term
    return acc


def _windows(ref, offsets, tm):
    for res in range(SUBLANES):
        ks = [k for k, o in enumerate(offsets) if o % SUBLANES == res]
        if not ks:
            continue
        lo = min(offsets[k] for k in ks)
        hi = max(offsets[k] for k in ks)
        shifted = ref[pl.ds(lo, tm + hi - lo), :]
        for k in ks:
            yield k, shifted[offsets[k] - lo:offsets[k] - lo + tm]


def _conv_taps_bwd(dbuf_ref, w, width, tm):
    acc = None
    for kk, win in _windows(dbuf_ref, [width - 1 - kk for kk in range(width)], tm):
        term = w[kk:kk + 1, :] * win
        acc = term if acc is None else acc + term
    return acc


def _conv_dw_acc(dw_ref, dout, buf_ref, width, halo, tm):
    for kk, win in _windows(buf_ref, [halo - (width - 1) + kk for kk in range(width)], tm):
        dw_ref[pl.ds(kk, 1), :] += jnp.sum(dout * win, axis=0, keepdims=True)


DNC_HALO = 8
DNC_COLS = 768


def dnconv_fwd(proj, w):
    T = proj.shape[0]
    tm = min(T, 256)
    hb = tm // DNC_HALO

    def body(x_ref, h_ref, w_ref, o_ref, buf_ref):
        i = pl.program_id(0)
        buf_ref[0:DNC_HALO, :] = jnp.where(i > 0, h_ref[...], 0.0)
        buf_ref[DNC_HALO:, :] = x_ref[...]
        acc = _conv_taps(buf_ref, w_ref[...], DN_CONV, DNC_HALO, tm)
        o_ref[...] = acc * _sigmoid(acc)

    return pl.pallas_call(
        body, name="dnconv_fwd", grid=(T // tm, 2),
        in_specs=[pl.BlockSpec((tm, DNC_COLS), lambda i, c: (i, 1 + c)),
                  pl.BlockSpec((DNC_HALO, DNC_COLS), lambda i, c: (jnp.maximum(i * hb - 1, 0), 1 + c)),
                  pl.BlockSpec((DN_CONV, DNC_COLS), lambda i, c: (0, c))],
        out_specs=pl.BlockSpec((tm, DNC_COLS), lambda i, c: (i, c)),
        out_shape=jax.ShapeDtypeStruct((T, QKV_B), F32),
        scratch_shapes=[pltpu.VMEM((DNC_HALO + tm, DNC_COLS), F32)],
        compiler_params=_cparams(("parallel", "parallel")),
    )(proj, proj, w)


def dnconv_bwd(proj, w, dout):
    T = proj.shape[0]
    tm = min(T, 256)
    nt = T // tm
    hb = tm // DNC_HALO

    def body(x_ref, h_ref, w_ref, do_ref, dx_ref, dw_ref, buf_ref, dbuf_ref):
        r = pl.program_id(1)
        i = nt - 1 - r

        @pl.when(r == 0)
        def _():
            dw_ref[...] = jnp.zeros_like(dw_ref)
            dbuf_ref[tm:, :] = jnp.zeros((DNC_HALO, DNC_COLS), F32)

        buf_ref[0:DNC_HALO, :] = jnp.where(i > 0, h_ref[...], 0.0)
        buf_ref[DNC_HALO:, :] = x_ref[...]
        wv = w_ref[...]
        acc = _conv_taps(buf_ref, wv, DN_CONV, DNC_HALO, tm)
        sg = _sigmoid(acc)
        dacc = do_ref[...] * (sg * (1.0 + acc * (1.0 - sg)))
        dbuf_ref[0:tm, :] = dacc
        dx_ref[...] = _conv_taps_bwd(dbuf_ref, wv, DN_CONV, tm)
        _conv_dw_acc(dw_ref, dacc, buf_ref, DN_CONV, DNC_HALO, tm)
        dbuf_ref[tm:, :] = dacc[0:DNC_HALO, :]

    return pl.pallas_call(
        body, name="dnconv_bwd", grid=(2, nt),
        in_specs=[pl.BlockSpec((tm, DNC_COLS), lambda c, r: (nt - 1 - r, 1 + c)),
                  pl.BlockSpec((DNC_HALO, DNC_COLS), lambda c, r: (jnp.maximum((nt - 1 - r) * hb - 1, 0), 1 + c)),
                  pl.BlockSpec((DN_CONV, DNC_COLS), lambda c, r: (0, c)),
                  pl.BlockSpec((tm, DNC_COLS), lambda c, r: (nt - 1 - r, c))],
        out_specs=[pl.BlockSpec((tm, DNC_COLS), lambda c, r: (nt - 1 - r, c)),
                   pl.BlockSpec((DN_CONV, DNC_COLS), lambda c, r: (0, c))],
        out_shape=[jax.ShapeDtypeStruct((T, QKV_B), F32), jax.ShapeDtypeStruct((DN_CONV, QKV_B), F32)],
        scratch_shapes=[pltpu.VMEM((DNC_HALO + tm, DNC_COLS), F32), pltpu.VMEM((tm + DNC_HALO, DNC_COLS), F32)],
        compiler_params=_cparams(("parallel", "arbitrary")),
    )(proj, proj, w, dout)


CV_HALO = 32


def _cv_post(cv, lnw, lnb):
    mu = jnp.mean(cv, axis=-1, keepdims=True)
    xc = cv - mu
    y = xc * lax.rsqrt(jnp.mean(xc * xc, axis=-1, keepdims=True) + EPS) * lnw + lnb
    return y * _sigmoid(y)


def cv_fwd(ab, w, bdw, lnw, lnb, ag=()):
    T = ab.shape[0]
    D = ab.shape[1] // 2
    tm = min(T, 256)
    hb = tm // CV_HALO

    def body(a_ref, b_ref, ah_ref, bh_ref, w_ref, bdw_ref, lnw_ref, lnb_ref, o_ref, cv_ref, buf_ref):
        i = pl.program_id(0)
        buf_ref[0:CV_HALO, :] = jnp.where(i > 0, ah_ref[...] * _sigmoid(bh_ref[...]), 0.0)
        buf_ref[CV_HALO:, :] = a_ref[...] * _sigmoid(b_ref[...])
        cv = _conv_taps(buf_ref, w_ref[...], CONV_WIDTH, CV_HALO, tm) + bdw_ref[...]
        cv_ref[...] = cv
        o_ref[...] = _cv_post(cv, lnw_ref[...], lnb_ref[...])

    halo = lambda c: pl.BlockSpec((CV_HALO, D), lambda i: (jnp.maximum(i * hb - 1, 0), c))
    vec = pl.BlockSpec((1, D), lambda i: (0, 0))
    tile = pl.BlockSpec((tm, D), lambda i: (i, 0))
    outs, gathered, _ = _pcall(
        body, (ab, ab, ab, ab, w, bdw, lnw, lnb), name="cv_fwd", grid=(T // tm,),
        in_specs=[tile, pl.BlockSpec((tm, D), lambda i: (i, 1)),
                  halo(0), halo(1), pl.BlockSpec((CONV_WIDTH, D), lambda i: (0, 0)), vec, vec, vec],
        out_specs=[tile, tile],
        out_shape=[jax.ShapeDtypeStruct((T, D), F32), jax.ShapeDtypeStruct((T, D), F32)],
        scratch_shapes=[pltpu.VMEM((CV_HALO + tm, D), F32)], sem=("arbitrary",), ag=ag)
    return outs, gathered


def cv_bwd(ab, cv, w, lnw, lnb, dout, rs=()):
    T = ab.shape[0]
    D = ab.shape[1] // 2
    tm = min(T, 256)
    nt = T // tm
    hb = tm // CV_HALO

    def body(a_ref, b_ref, ah_ref, bh_ref, cv_ref, w_ref, lnw_ref, lnb_ref, do_ref,
             da_ref, db_ref, dw_ref, dbdw_ref, dlnw_ref, dlnb_ref, buf_ref, dbuf_ref):
        r = pl.program_id(0)
        i = nt - 1 - r

        @pl.when(r == 0)
        def _():
            dw_ref[...] = jnp.zeros_like(dw_ref)
            dbdw_ref[...] = jnp.zeros_like(dbdw_ref)
            dlnw_ref[...] = jnp.zeros_like(dlnw_ref)
            dlnb_ref[...] = jnp.zeros_like(dlnb_ref)
            dbuf_ref[tm:, :] = jnp.zeros((CV_HALO, D), F32)

        a = a_ref[...]
        sb = _sigmoid(b_ref[...])
        buf_ref[0:CV_HALO, :] = jnp.where(i > 0, ah_ref[...] * _sigmoid(bh_ref[...]), 0.0)
        buf_ref[CV_HALO:, :] = a * sb
        wv = w_ref[...]
        _, vjp = jax.vjp(_cv_post, cv_ref[...], lnw_ref[...], lnb_ref[...])
        dcv, dlnw, dlnb = vjp(do_ref[...])
        dlnw_ref[...] += dlnw
        dlnb_ref[...] += dlnb
        dbdw_ref[...] += jnp.sum(dcv, axis=0, keepdims=True)
        dbuf_ref[0:tm, :] = dcv
        du = _conv_taps_bwd(dbuf_ref, wv, CONV_WIDTH, tm)
        _conv_dw_acc(dw_ref, dcv, buf_ref, CONV_WIDTH, CV_HALO, tm)
        dbuf_ref[tm:, :] = dcv[0:CV_HALO, :]
        da_ref[...] = du * sb
        db_ref[...] = du * a * sb * (1.0 - sb)

    tile = lambda c: pl.BlockSpec((tm, D), lambda r: (nt - 1 - r, c))
    halo = lambda c: pl.BlockSpec((CV_HALO, D), lambda r: (jnp.maximum((nt - 1 - r) * hb - 1, 0), c))
    vec = pl.BlockSpec((1, D), lambda r: (0, 0))
    wsp = pl.BlockSpec((CONV_WIDTH, D), lambda r: (0, 0))
    (da, db, dw, dbdw, dlnw, dlnb), _, slots = _pcall(
        body, (ab, ab, ab, ab, cv, w, lnw, lnb, dout), name="cv_bwd", grid=(nt,),
        in_specs=[tile(0), tile(1), halo(0), halo(1), tile(0), wsp, vec, vec, tile(0)],
        out_specs=[tile(0), tile(0), wsp, vec, vec, vec],
        out_shape=[jax.ShapeDtypeStruct((T, D), F32), jax.ShapeDtypeStruct((T, D), F32),
                   jax.ShapeDtypeStruct((CONV_WIDTH, D), F32)] + [jax.ShapeDtypeStruct((1, D), F32)] * 3,
        scratch_shapes=[pltpu.VMEM((CV_HALO + tm, D), F32), pltpu.VMEM((tm + CV_HALO, D), F32)],
        sem=("arbitrary",), rs=rs)
    return (jnp.concatenate([da, db], axis=1), dw, dbdw, dlnw, dlnb), slots


def adamw(w, m, v, slots, rs=()):
    L, R, C = w.shape
    ns = slots[0].shape[0]
    fits = lambda r, c: ns * r * c * 2 <= ADAM_SLOT_BLOCK
    tiles = [(R, C)] if fits(R, C) else []
    tiles += [(d, C) for d in range(16, R, 16) if R % d == 0 and fits(d, C)]
    tiles += [(R, d) for d in range(LANES, C, LANES) if C % d == 0 and fits(R, d)]
    tr, tc = max(tiles, key=lambda t: t[0] * t[1])
    c1 = 1.0 / (1.0 - ADAM_B1 ** ADAM_STEP)
    c2 = 1.0 / (1.0 - ADAM_B2 ** ADAM_STEP)

    def body(w_ref, m_ref, v_ref, *rest):
        s_refs = rest[:L]
        g_ref, d_ref, nm_ref, nv_ref = rest[L:]
        l = pl.program_id(0)
        for k in range(L):
            @pl.when(l == k)
            def _(s_ref=s_refs[k]):
                g = s_ref[0].astype(F32)
                for j in range(1, ns):
                    g = g + s_ref[j].astype(F32)
                nm = ADAM_B1 * m_ref[0] + (1.0 - ADAM_B1) * g
                nv = ADAM_B2 * v_ref[0] + (1.0 - ADAM_B2) * (g * g)
                g_ref[0] = g
                nm_ref[0] = nm
                nv_ref[0] = nv
                d_ref[0] = -ADAM_LR * ((nm * c1) / (jnp.sqrt(nv * c2) + ADAM_EPS) + ADAM_WD * w_ref[0])

    nc = C // tc
    blk = pl.BlockSpec((1, tr, tc), lambda l, i: (l, i // nc, i % nc))
    slot = lambda k: pl.BlockSpec((ns, tr, tc), lambda l, i: (0, jnp.where(l == k, i // nc, 0),
                                                                 jnp.where(l == k, i % nc, 0)))
    outs, _, landed = _pcall(
        body, (w, m, v, *slots), name="adamw", grid=(L, (R // tr) * nc),
        in_specs=[blk, blk, blk] + [slot(k) for k in range(L)],
        out_specs=[blk, blk, blk, blk],
        out_shape=[jax.ShapeDtypeStruct((L, R, C), F32)] * 4,
        sem=("arbitrary", "arbitrary"), rs=rs)
    return outs, landed


def _unshard(g, axis):
    g = jnp.moveaxis(g, 0, axis)
    s = g.shape
    return g.reshape(s[:axis] + (s[axis] * s[axis + 1],) + s[axis + 2:])


def _to_blocks(full, axis):
    s = full.shape
    g = full.reshape(s[:axis] + (N_DEV, s[axis] // N_DEV) + s[axis + 1:])
    return jnp.moveaxis(g, axis, 0)


SMALL = (("norm_w", 2), ("dn_conv_w", 2), ("conv_b_pw1", 1), ("conv_w_dw", 2), ("conv_b_dw", 1),
         ("conv_ln_w", 1), ("conv_ln_b", 1), ("conv_b_pw2", 1),
         ("attn_sinks", None), ("dn_a_log", None), ("dn_dt_bias", None), ("dn_norm_w", None), ("final_norm_w", None))
SMALL_AXIS = dict(SMALL)


def _pack(parts):
    flat = jnp.concatenate([p.reshape(-1) for p in parts])
    pad = (-flat.shape[0]) % LANES
    return jnp.pad(flat, (0, pad))


def _unpack(flat, shapes):
    out, off = [], 0
    for s in shapes:
        n = int(np.prod(s))
        out.append(flat[off:off + n].reshape(s))
        off += n
    return out


def kernel(x, norm_w, ffn_w_gate, ffn_w_up, ffn_w_down, mix_w_in, dn_conv_w, attn_sinks, dn_a_log, dn_dt_bias, dn_norm_w, mix_w_out, conv_w_pw1, conv_b_pw1, conv_w_dw, conv_b_dw, conv_ln_w, conv_ln_b, conv_w_pw2, conv_b_pw2, final_norm_w, loss_target, m_norm_w, m_ffn_w_gate, m_ffn_w_up, m_ffn_w_down, m_mix_w_in, m_dn_conv_w, m_attn_sinks, m_dn_a_log, m_dn_dt_bias, m_dn_norm_w, m_mix_w_out, m_conv_w_pw1, m_conv_b_pw1, m_conv_w_dw, m_conv_b_dw, m_conv_ln_w, m_conv_ln_b, m_conv_w_pw2, m_conv_b_pw2, m_final_norm_w, v_norm_w, v_ffn_w_gate, v_ffn_w_up, v_ffn_w_down, v_mix_w_in, v_dn_conv_w, v_attn_sinks, v_dn_a_log, v_dn_dt_bias, v_dn_norm_w, v_mix_w_out, v_conv_w_pw1, v_conv_b_pw1, v_conv_w_dw, v_conv_b_dw, v_conv_ln_w, v_conv_ln_b, v_conv_w_pw2, v_conv_b_pw2, v_final_norm_w):
    W = dict(norm_w=norm_w, ffn_w_gate=ffn_w_gate, ffn_w_up=ffn_w_up, ffn_w_down=ffn_w_down, mix_w_in=mix_w_in,
             dn_conv_w=dn_conv_w, attn_sinks=attn_sinks, dn_a_log=dn_a_log, dn_dt_bias=dn_dt_bias,
             dn_norm_w=dn_norm_w, mix_w_out=mix_w_out, conv_w_pw1=conv_w_pw1, conv_b_pw1=conv_b_pw1,
             conv_w_dw=conv_w_dw, conv_b_dw=conv_b_dw, conv_ln_w=conv_ln_w, conv_ln_b=conv_ln_b,
             conv_w_pw2=conv_w_pw2, conv_b_pw2=conv_b_pw2, final_norm_w=final_norm_w)
    M = dict(norm_w=m_norm_w, ffn_w_gate=m_ffn_w_gate, ffn_w_up=m_ffn_w_up, ffn_w_down=m_ffn_w_down,
             mix_w_in=m_mix_w_in, dn_conv_w=m_dn_conv_w, attn_sinks=m_attn_sinks, dn_a_log=m_dn_a_log,
             dn_dt_bias=m_dn_dt_bias, dn_norm_w=m_dn_norm_w, mix_w_out=m_mix_w_out, conv_w_pw1=m_conv_w_pw1,
             conv_b_pw1=m_conv_b_pw1, conv_w_dw=m_conv_w_dw, conv_b_dw=m_conv_b_dw, conv_ln_w=m_conv_ln_w,
             conv_ln_b=m_conv_ln_b, conv_w_pw2=m_conv_w_pw2, conv_b_pw2=m_conv_b_pw2, final_norm_w=m_final_norm_w)
    V = dict(norm_w=v_norm_w, ffn_w_gate=v_ffn_w_gate, ffn_w_up=v_ffn_w_up, ffn_w_down=v_ffn_w_down,
             mix_w_in=v_mix_w_in, dn_conv_w=v_dn_conv_w, attn_sinks=v_attn_sinks, dn_a_log=v_dn_a_log,
             dn_dt_bias=v_dn_dt_bias, dn_norm_w=v_dn_norm_w, mix_w_out=v_mix_w_out, conv_w_pw1=v_conv_w_pw1,
             conv_b_pw1=v_conv_b_pw1, conv_w_dw=v_conv_w_dw, conv_b_dw=v_conv_b_dw, conv_ln_w=v_conv_ln_w,
             conv_ln_b=v_conv_ln_b, conv_w_pw2=v_conv_w_pw2, conv_b_pw2=v_conv_b_pw2, final_norm_w=v_final_norm_w)

    T, D = x.shape[1], x.shape[2]
    xs = x[0]

    big = ("ffn_w_gate", "ffn_w_up", "ffn_w_down", "mix_w_in", "mix_w_out", "conv_w_pw1", "conv_w_pw2")
    shard3 = {k: W[k].reshape((-1,) + W[k].shape[-2:]) for k in big}
    shard_bf = {k: shard3[k].astype(BF16) for k in big}
    ffn_unit = lambda i: [("ffn_w_gate", i), ("ffn_w_up", i), ("ffn_w_down", i)]
    even_unit = lambda e: [("mix_w_in", e), ("mix_w_out", e)]
    odd_unit = lambda e: [("conv_w_pw1", e), ("conv_w_pw2", e)]
    have = {}

    def ag_jobs(units):
        return [(shard_bf[k], i) for k, i in units]

    def ag_done(units, gathered):
        have.update(zip(units, gathered))

    small_sharded = [(k, ax) for k, ax in SMALL if ax is not None]
    small_pack = _pack([W[k] for k, _ in small_sharded])[None, :]
    first_units = ffn_unit(0)
    gathered = exchange(ag_jobs(first_units) + [(small_pack, None)])
    ag_done(first_units, gathered[:-1])
    small_full = {}
    for (k, ax), parts in zip(small_sharded,
                              zip(*[_unpack(gathered[-1][s, 0], [W[k].shape for k, _ in small_sharded])
                                    for s in range(N_DEV)])):
        small_full[k] = _unshard(jnp.stack(parts), ax)
    nw_full = small_full["norm_w"]

    ffn_w = lambda i: [have[u] for u in ffn_unit(i)]
    w_in_of = lambda e: jnp.pad(_unshard(have[("mix_w_in", e)], 1), ((0, 0), (0, IN_COLS_PAD - IN_COLS)))
    w_out_of = lambda e: have[("mix_w_out", e)].reshape(D, D)
    w_pw1_of = lambda e: _unshard(have[("conv_w_pw1", e)], 1)
    w_pw2_of = lambda e: have[("conv_w_pw2", e)].reshape(D, D)
    fwd_order, needed = [], {}
    for l in range(DEPTH):
        mixer = [("A", l), ("E", l)] if l % 2 == 0 else [("O", l)]
        fwd_order += [("F", 2 * l)] + mixer + [("F", 2 * l + 1)]
        needed[("F", 2 * l)], needed[("F", 2 * l + 1)] = ffn_unit(2 * l), ffn_unit(2 * l + 1)
        needed[mixer[0]] = even_unit(l // 2) if l % 2 == 0 else odd_unit(l // 2)
    queue = [(u, pos) for pos, key in enumerate(fwd_order) for u in needed.get(key, []) if u not in first_units]
    unit_bytes = lambda u: N_DEV * shard_bf[u[0]][u[1]].size * 2
    fwd_carry, at = {}, 0
    for pos, key in enumerate(fwd_order):
        cap = FWD_CARRY_BYTES[key[0]]
        taken, used = [], 0
        while at < len(queue) and (queue[at][1] <= pos + 1 or used + unit_bytes(queue[at][0]) <= cap):
            taken.append(queue[at][0])
            used += unit_bytes(queue[at][0])
            at += 1
        fwd_carry[key] = taken
    zero_in = jnp.zeros((1, IN_COLS_PAD), F32)
    zero_d = jnp.zeros((1, D), F32)
    slope_rows = jnp.asarray(np.repeat(2.0 ** (-8.0 * np.arange(1, ATTN_HEADS + 1) / ATTN_HEADS), ATTN_BLOCK)
                             .astype(np.float32)[:, None])

    saved = []
    h = xs
    w_in, w_out, w_pw1, w_pw2 = {}, {}, {}, {}

    def ffn_forward(h, l, half):
        i = 2 * l + half
        units = fwd_carry.get(("F", i), [])
        h, gathered = ffn_fwd(h, nw_full[l, 2 * half][None], *ffn_w(i), ag=ag_jobs(units))
        ag_done(units, gathered)
        return h

    for l in range(DEPTH):
        e = l // 2
        st = {"x0": h}
        h = ffn_forward(h, l, 0)
        st["x1"] = h
        if l % 2 == 0:
            w_in[e], w_out[e] = w_in_of(e), w_out_of(e)
            proj = rmslin_fwd(h, nw_full[l, 1][None], w_in[e], zero_in)
            st["proj"] = proj
            st["qkvc"] = dnconv_fwd(proj, small_full["dn_conv_w"][e])
            st["sink_rows"] = jnp.repeat(attn_sinks[e], ATTN_BLOCK)[:, None]
            st["alog"] = dn_a_log[e].reshape(DN_HEADS, 1, 1)
            st["dtb"] = dn_dt_bias[e].reshape(DN_HEADS, 1, 1)
            st["dnw"] = dn_norm_w[e].reshape(1, 1, DN_D)
            units = fwd_carry[("A", l)]
            st["att"], gathered = attn_fwd(proj, st["sink_rows"], slope_rows, ag=ag_jobs(units))
            ag_done(units, gathered)
            units = fwd_carry[("E", l)]
            (st["og"], st["sall"]), gathered = dn_fwd(st["qkvc"], proj, st["alog"], st["dtb"], st["dnw"],
                                                      ag=ag_jobs(units))
            ag_done(units, gathered)
            h = lin_fwd(h, [st["att"], st["og"]], w_out[e], zero_d)
        else:
            units = fwd_carry[("O", l)]
            w_pw1[e], w_pw2[e] = w_pw1_of(e), w_pw2_of(e)
            st["ab"] = rmslin_fwd(h, nw_full[l, 1][None], w_pw1[e], small_full["conv_b_pw1"][e][None])
            (st["act"], st["cv"]), gathered = cv_fwd(st["ab"], small_full["conv_w_dw"][e], small_full["conv_b_dw"][e][None],
                                         small_full["conv_ln_w"][e][None], small_full["conv_ln_b"][e][None],
                                         ag=ag_jobs(units))
            ag_done(units, gathered)
            h = lin_fwd(h, [st["act"]], w_pw2[e], small_full["conv_b_pw2"][e][None])
        st["x2"] = h
        h = ffn_forward(h, l, 1)
        saved.append(st)

    loss_part, dh, dfinal = loss_fwd_bwd(h, final_norm_w[None], loss_target[0])
    loss = lax.psum(loss_part[0, 0], ("x", "y", "c"))

    d_norm = [[None] * 3 for _ in range(DEPTH)]
    d_small = {k: [None, None] for k in ("dn_conv_w", "conv_b_pw1", "conv_w_dw", "conv_b_dw", "conv_ln_w",
                                         "conv_ln_b", "conv_b_pw2", "attn_sinks", "dn_a_log", "dn_dt_bias",
                                         "dn_norm_w")}
    pending, slot = [], {}

    def take_jobs(cap=None, only=None):
        taken = [p for p in pending if p[1] == "swap"]
        used = 0
        for p in pending:
            if p[1] == "swap" or (only is not None and p[0][0] != only):
                continue
            if cap is not None and used + p[2].size * 2 > cap:
                break
            taken.append(p)
            used += p[2].size * 2
        pending[:] = [p for p in pending if all(p is not t for t in taken)]
        return taken, [(kind, arr) for _, kind, arr in taken]

    def land(taken, results):
        swapped = [(unit, arr, res) for (unit, kind, arr), res in zip(taken, results) if kind == "swap"]
        slot.update({unit: res for (unit, kind, _), res in zip(taken, results) if kind != "swap"})
        if swapped:
            sums = pair_add([g for _, g, _ in swapped], [r for _, _, r in swapped])
            pending.extend((unit, "chips", h) for (unit, _, _), h in zip(swapped, sums))

    def ffn_backward(dh, l, half):
        i = 2 * l + half
        taken, jobs = take_jobs(BWD_CARRY_BYTES["F"])
        (dh, dg, du, dd, d_norm[l][2 * half]), results = ffn_bwd(
            st["x2" if half else "x0"], dh, nw_full[l, 2 * half][None], *ffn_w(i), rs=jobs)
        land(taken, results)
        pending.extend((u, "swap", g) for u, g in zip(ffn_unit(i), (dg, du, dd)))
        return dh

    for l in reversed(range(DEPTH)):
        e = l // 2
        st = saved[l]
        dh = ffn_backward(dh, l, 1)
        if l % 2 == 0:
            dmix, d_out, _ = lin_bwd([st["att"], st["og"]], dh, w_out[e])
            pending.append((("mix_w_out", e), "direct", d_out.reshape(N_DEV, D // N_DEV, D).astype(BF16)))
            taken, jobs = take_jobs(BWD_CARRY_BYTES["E"])
            (dqkvc, dzba, dalog, ddtb, ddnw), results = dn_bwd(
                st["qkvc"], st["proj"], st["alog"], st["dtb"], st["dnw"], st["sall"], dmix, rs=jobs)
            land(taken, results)
            taken, jobs = take_jobs(BWD_CARRY_BYTES["A"])
            (dqa, dkva, dsink), results = attn_bwd(st["proj"], st["sink_rows"], slope_rows, dmix, rs=jobs)
            land(taken, results)
            dqkv, d_small["dn_conv_w"][e] = dnconv_bwd(st["proj"], small_full["dn_conv_w"][e], dqkvc)
            dproj = jnp.concatenate([dqa, dkva, dqkv, dzba], axis=1)
            dh, d_in, _, d_norm[l][1] = rmslin_bwd(st["x1"], dh, dproj, nw_full[l, 1][None], w_in[e])
            pending.append((("mix_w_in", e), "direct", _to_blocks(d_in[:, :IN_COLS], 1).astype(BF16)))
            d_small["attn_sinks"][e] = jnp.sum(dsink.reshape(ATTN_HEADS, ATTN_BLOCK), axis=1)
            d_small["dn_a_log"][e] = dalog.reshape(DN_HEADS)
            d_small["dn_dt_bias"][e] = ddtb.reshape(DN_HEADS)
            d_small["dn_norm_w"][e] = ddnw.reshape(DN_D)
        else:
            dact, d_pw2, d_small["conv_b_pw2"][e] = lin_bwd([st["act"]], dh, w_pw2[e])
            pending.append((("conv_w_pw2", e), "direct", d_pw2.reshape(N_DEV, D // N_DEV, D).astype(BF16)))
            taken, jobs = take_jobs(BWD_CARRY_BYTES["O"])
            (dab, d_small["conv_w_dw"][e], d_small["conv_b_dw"][e], d_small["conv_ln_w"][e],
             d_small["conv_ln_b"][e]), results = cv_bwd(
                st["ab"], st["cv"], small_full["conv_w_dw"][e],
                small_full["conv_ln_w"][e][None], small_full["conv_ln_b"][e][None], dact, rs=jobs)
            land(taken, results)
            dh, d_pw1, d_small["conv_b_pw1"][e], d_norm[l][1] = rmslin_bwd(
                st["x1"], dh, dab, nw_full[l, 1][None], w_pw1[e])
            pending.append((("conv_w_pw1", e), "direct", _to_blocks(d_pw1, 1).astype(BF16)))
        dh = ffn_backward(dh, l, 0)
    grad_x = dh[None]

    full_small = {"norm_w": jnp.stack([jnp.concatenate(r, axis=0) for r in d_norm]),
                  "final_norm_w": dfinal[0]}
    for k, pair in d_small.items():
        full_small[k] = jnp.stack([p.reshape(W[k].shape[1:-1] + (-1,)) if SMALL_AXIS[k] is not None
                                   else p for p in pair])
    rows = []
    for s in range(N_DEV):
        parts = [_to_blocks(full_small[k], ax)[s] if ax is not None else full_small[k] for k, ax in SMALL]
        rows.append(_pack(parts))
    send_small = jnp.stack(rows)[:, None, :]
    pending.append((("small", 0), "direct", send_small))

    res = {}
    waiting = lambda k: [p for p in pending if p[0][0] == k]
    adam_order = sorted(big, key=lambda k: len(waiting(k))) + ["small"]
    for n, k in enumerate(adam_order[:-1]):
        taken, jobs = take_jobs(only=next((kk for kk in adam_order[n + 1:] if waiting(kk)), "small"))
        turned = k in ("ffn_w_gate", "ffn_w_up")
        view = lambda a: jnp.swapaxes(a.reshape(shard3[k].shape), 1, 2) if turned else a.reshape(shard3[k].shape)
        outs, results = adamw(view(W[k]), view(M[k]), view(V[k]),
                              [slot[(k, i)] for i in range(shard3[k].shape[0])], rs=jobs)
        land(taken, results)
        res[k] = [(jnp.swapaxes(o, 1, 2) if turned else o).reshape(W[k].shape) for o in outs]
    pk = lambda d: _pack([d[k] for k, _ in SMALL])[None, None, :]
    outs, _ = adamw(pk(W), pk(M), pk(V), [slot[("small", 0)]])
    shapes = [W[k].shape for k, _ in SMALL]
    unp = [_unpack(o[0, 0], shapes) for o in outs]
    for i, (k, _) in enumerate(SMALL):
        res[k] = [u[i] for u in unp]

    order = ("norm_w", "ffn_w_gate", "ffn_w_up", "ffn_w_down", "mix_w_in", "dn_conv_w", "attn_sinks", "dn_a_log",
             "dn_dt_bias", "dn_norm_w", "mix_w_out", "conv_w_pw1", "conv_b_pw1", "conv_w_dw", "conv_b_dw",
             "conv_ln_w", "conv_ln_b", "conv_w_pw2", "conv_b_pw2", "final_norm_w")
    return (loss, grad_x, *[res[k][0] for k in order], *[res[k][1] for k in order],
            *[res[k][2] for k in order], *[res[k][3] for k in order])
```

```python
import functools

import numpy as np
import jax
import jax.numpy as jnp
from jax import lax
from jax.experimental import pallas as pl
from jax.experimental.pallas import tpu as pltpu

F32 = jnp.float32
BF16 = jnp.bfloat16
EPS = 1e-6
N_DEV = 8
N_CHIP = 4
V7X_VMEM_LIMIT = 60 * 2**20
MESH = pl.DeviceIdType.MESH
LANES = 128
SUBLANES = 8

DEPTH = 4
D_MODEL = 1024
ATTN_HEADS, ATTN_KV_HEADS, HEAD_DIM, ATTN_BLOCK = 8, 2, 64, 128
DN_HEADS, DN_D, DN_CHUNK, DN_CONV = 8, 64, 64, 4
CONV_WIDTH = 31
Q_A, KV_A, QKV_B, V_B = 512, 128, 1536, 512
IN_COLS = 2832
IN_COLS_PAD = 3072
OFF_QKVB = Q_A + 2 * KV_A
OFF_Z = OFF_QKVB + QKV_B
OFF_BETA = OFF_Z + V_B
OFF_A = OFF_BETA + DN_HEADS

FWD_CARRY_BYTES = {"F": 12 * 2**20, "A": 6 * 2**20, "E": 18 * 2**20, "O": 12 * 2**20}
BWD_CARRY_BYTES = {"F": 11 * 2**20, "A": 6 * 2**20, "E": 13 * 2**20, "O": 10 * 2**20}

ADAM_SLOT_BLOCK = 3 * 2**19

ADAM_LR, ADAM_B1, ADAM_B2, ADAM_EPS, ADAM_WD, ADAM_STEP = 0.001, 0.9, 0.999, 1e-08, 0.01, 10


def _cparams(sem):
    return pltpu.CompilerParams(dimension_semantics=sem, vmem_limit_bytes=V7X_VMEM_LIMIT)


def _sigmoid(x):
    return 1.0 / (1.0 + jnp.exp(-x))


def _softplus(x):
    return jnp.maximum(x, 0.0) + jnp.log(1.0 + jnp.exp(-jnp.abs(x)))


def _dot(a, b):
    return jnp.dot(a, b, preferred_element_type=F32)


def _dot_nt(a, b):
    return lax.dot_general(a, b, (((1,), (1,)), ((), ())), preferred_element_type=F32)


def _dot_tn(a, b):
    return lax.dot_general(a, b, (((0,), (0,)), ((), ())), preferred_element_type=F32)


def _rms(x, w):
    return x * lax.rsqrt(jnp.mean(x * x, axis=-1, keepdims=True) + EPS) * w


def _rms_bwd(x, w, dxn):
    r = lax.rsqrt(jnp.mean(x * x, axis=-1, keepdims=True) + EPS)
    xh = x * r
    dxh = dxn * w
    dx = r * (dxh - xh * jnp.mean(dxh * xh, axis=-1, keepdims=True))
    return dx, jnp.sum(dxn * xh, axis=0, keepdims=True)


def _position():
    return lax.axis_index("x"), lax.axis_index("y"), lax.axis_index("c")


def _dev_index(px, py, pc):
    return 4 * px + 2 * py + pc


def _rcopy(src, dst, send_sem, recv_sem, to):
    return pltpu.make_async_remote_copy(src_ref=src, dst_ref=dst, send_sem=send_sem, recv_sem=recv_sem,
                                        device_id=to, device_id_type=MESH)


def _ag_start(srcs, outs, send, recv, local):
    x, y, c = _position()
    me = _dev_index(x, y, c)
    chips = [(1 - x, y), (x, 1 - y), (1 - x, 1 - y)]
    for a, (src, out) in enumerate(zip(srcs, outs)):
        pltpu.make_async_copy(src, out.at[me], local.at[a]).start()
        _rcopy(src, out.at[me], send.at[a, 0], recv.at[a, 0], (x, y, 1 - c)).start()
        for j, chip in enumerate(chips):
            _rcopy(src, out.at[me], send.at[a, 1 + j], recv.at[a, 1 + j], (*chip, c)).start()


def _ag_finish(srcs, outs, send, recv, local):
    x, y, c = _position()
    me = _dev_index(x, y, c)
    sibling = (x, y, 1 - c)
    chips = [(1 - x, y), (x, 1 - y), (1 - x, 1 - y)]
    for j, chip in enumerate(chips):
        for a, out in enumerate(outs):
            blk = out.at[_dev_index(*chip, c)]
            _rcopy(blk, blk, send.at[a, 1 + j], recv.at[a, 1 + j], (x, y, c)).wait_recv()
            _rcopy(blk, blk, send.at[a, 4 + j], recv.at[a, 4 + j], sibling).start()
    for a, (src, out) in enumerate(zip(srcs, outs)):
        blk = out.at[_dev_index(x, y, 1 - c)]
        _rcopy(blk, blk, send.at[a, 0], recv.at[a, 0], (x, y, c)).wait_recv()
        for j, chip in enumerate(chips):
            blk = out.at[_dev_index(*chip, 1 - c)]
            _rcopy(blk, blk, send.at[a, 4 + j], recv.at[a, 4 + j], (x, y, c)).wait_recv()
        for k in range(N_DEV - 1):
            _rcopy(out.at[me], out.at[me], send.at[a, k], recv.at[a, k], (x, y, c)).wait_send()
        pltpu.make_async_copy(src, out.at[me], local.at[a]).wait()


def _rs_peer(r):
    x, y, c = _position()
    return x ^ ((r >> 2) & 1), y ^ ((r >> 1) & 1), c ^ (r & 1)


def _rs_start(ins, outs, send, recv, local):
    me = _dev_index(*_position())
    for a, (src, out) in enumerate(zip(ins, outs)):
        pltpu.make_async_copy(src.at[me], out.at[me], local.at[a]).start()
        for r in range(1, N_DEV):
            p = _rs_peer(r)
            _rcopy(src.at[_dev_index(*p)], out.at[me], send.at[a, r - 1], recv.at[a, r - 1], p).start()


def _rs_finish(ins, outs, send, recv, local):
    pos = _position()
    me = _dev_index(*pos)
    for a, (src, out) in enumerate(zip(ins, outs)):
        for r in range(1, N_DEV):
            blk = out.at[_dev_index(*_rs_peer(r))]
            _rcopy(blk, blk, send.at[a, r - 1], recv.at[a, r - 1], pos).wait_recv()
        for r in range(1, N_DEV):
            _rcopy(src.at[me], out.at[me], send.at[a, r - 1], recv.at[a, r - 1], pos).wait_send()
        pltpu.make_async_copy(src.at[me], out.at[me], local.at[a]).wait()


def _sw_start(ins, outs, send, recv):
    x, y, c = _position()
    for a, (src, out) in enumerate(zip(ins, outs)):
        for q in range(N_CHIP):
            _rcopy(src.at[2 * q + (1 - c)], out.at[q], send.at[a, q], recv.at[a, q], (x, y, 1 - c)).start()


def _sw_finish(ins, outs, send, recv):
    pos = _position()
    for a, out in enumerate(outs):
        for q in range(N_CHIP):
            _rcopy(out.at[q], out.at[q], send.at[a, q], recv.at[a, q], pos).wait_recv()
        for q in range(N_CHIP):
            _rcopy(out.at[q], out.at[q], send.at[a, q], recv.at[a, q], pos).wait_send()


def _r4_peer(r):
    x, y, c = _position()
    return x ^ ((r >> 1) & 1), y ^ (r & 1), c


def _r4_start(ins, outs, send, recv, local):
    x, y, c = _position()
    mine = 2 * x + y
    for a, (src, out) in enumerate(zip(ins, outs)):
        pltpu.make_async_copy(src.at[mine], out.at[mine], local.at[a]).start()
        for r in range(1, N_CHIP):
            px, py, pc = _r4_peer(r)
            _rcopy(src.at[2 * px + py], out.at[mine], send.at[a, r - 1], recv.at[a, r - 1], (px, py, pc)).start()


def _r4_finish(ins, outs, send, recv, local):
    x, y, c = _position()
    mine = 2 * x + y
    for a, (src, out) in enumerate(zip(ins, outs)):
        for r in range(1, N_CHIP):
            px, py, _ = _r4_peer(r)
            blk = out.at[2 * px + py]
            _rcopy(blk, blk, send.at[a, r - 1], recv.at[a, r - 1], (x, y, c)).wait_recv()
        for r in range(1, N_CHIP):
            _rcopy(src.at[mine], out.at[mine], send.at[a, r - 1], recv.at[a, r - 1], (x, y, c)).wait_send()
        pltpu.make_async_copy(src.at[mine], out.at[mine], local.at[a]).wait()


_RS_KINDS = {
    "direct": (_rs_start, _rs_finish, lambda n: [(n, N_DEV - 1), (n, N_DEV - 1), (n,)], lambda s: s),
    "swap": (_sw_start, _sw_finish, lambda n: [(n, N_CHIP), (n, N_CHIP)], lambda s: (N_CHIP,) + s[1:]),
    "chips": (_r4_start, _r4_finish, lambda n: [(n, N_CHIP - 1), (n, N_CHIP - 1), (n,)], lambda s: s),
}


def _pcall(body, args, *, name, grid, in_specs, out_specs, out_shape, sem, scratch_shapes=(), ag=(), rs=()):
    na, nr = len(ag), len(rs)
    if na + nr == 0:
        outs = pl.pallas_call(body, name=name, grid=grid, in_specs=in_specs, out_specs=out_specs,
                              out_shape=out_shape, scratch_shapes=list(scratch_shapes),
                              compiler_params=_cparams(sem))(*args)
        return list(outs), [], []
    n_in, n_out, n_scr = len(in_specs), len(out_specs), len(scratch_shapes)
    ag_idx = [i for _, i in ag]
    groups = [(k, [i for i, (kk, _) in enumerate(rs) if kk == k]) for k in _RS_KINDS]
    groups = [(k, idx) for k, idx in groups if idx]
    sem_counts = ([3] if na else []) + [len(_RS_KINDS[k][2](1)) for k, _ in groups]

    def wrapped(*refs):
        cin, refs = refs[:n_in], refs[n_in:]
        ag_in, refs = refs[:na], refs[na:]
        rs_in, refs = refs[:nr], refs[nr:]
        cout, refs = refs[:n_out], refs[n_out:]
        ag_out, refs = refs[:na], refs[na:]
        rs_out, refs = refs[:nr], refs[nr:]
        cscr, sems = refs[:n_scr], list(refs[n_scr:])
        sem_sets = [[sems.pop(0) for _ in range(n)] for n in sem_counts]
        ag_sems = sem_sets.pop(0) if na else None
        ag_src = [r if i is None else r.at[i] for r, i in zip(ag_in, ag_idx)]
        ids = [pl.program_id(d) for d in range(len(grid))]
        first = functools.reduce(jnp.logical_and, [i == 0 for i in ids])
        last = functools.reduce(jnp.logical_and, [i == g - 1 for i, g in zip(ids, grid)])

        def run(phase):
            if na:
                (_ag_start, _ag_finish)[phase](ag_src, ag_out, *ag_sems)
            for (k, idx), ss in zip(groups, sem_sets):
                _RS_KINDS[k][phase]([rs_in[i] for i in idx], [rs_out[i] for i in idx], *ss)

        @pl.when(first)
        def _():
            run(0)

        body(*cin, *cout, *cscr)

        @pl.when(last)
        def _():
            run(1)

    hbm = pl.BlockSpec(memory_space=pl.ANY)
    sem_shapes = [pltpu.SemaphoreType.DMA(s) for s in ([(na, N_DEV - 1), (na, N_DEV - 1), (na,)] if na else [])]
    for k, idx in groups:
        sem_shapes += [pltpu.SemaphoreType.DMA(s) for s in _RS_KINDS[k][2](len(idx))]
    outs = pl.pallas_call(
        wrapped, name=name, grid=grid,
        in_specs=list(in_specs) + [hbm] * (na + nr),
        out_specs=list(out_specs) + [hbm] * (na + nr),
        out_shape=list(out_shape)
        + [jax.ShapeDtypeStruct((N_DEV,) + a.shape[-2:], a.dtype) for a, _ in ag]
        + [jax.ShapeDtypeStruct(_RS_KINDS[k][3](b.shape), b.dtype) for k, b in rs],
        scratch_shapes=list(scratch_shapes) + sem_shapes,
        compiler_params=_cparams(sem),
    )(*args, *[a for a, _ in ag], *[b for _, b in rs])
    return list(outs[:n_out]), list(outs[n_out:n_out + na]), list(outs[n_out + na:])


def exchange(ag):
    def body(o_ref):
        o_ref[...] = jnp.zeros_like(o_ref)

    _, gathered, _ = _pcall(body, (), name="exchange", grid=(1,), in_specs=[],
                            out_specs=[pl.BlockSpec((8, LANES), lambda i: (0, 0))],
                            out_shape=[jax.ShapeDtypeStruct((8, LANES), F32)], sem=("arbitrary",), ag=ag)
    return gathered


def pair_add(blocks, received):
    n = len(blocks)

    def body(*refs):
        c = lax.axis_index("c")
        for g_ref, p_ref, o_ref in zip(refs[:n], refs[n:2 * n], refs[2 * n:]):
            mine = jnp.where(c == 0, g_ref[0, 0], g_ref[0, 1])
            o_ref[0] = (mine.astype(F32) + p_ref[0].astype(F32)).astype(BF16)

    halves = 2
    g_specs = [pl.BlockSpec((1, 2, b.shape[1] // halves, b.shape[2]), lambda q, r: (q, 0, r, 0)) for b in blocks]
    p_specs = [pl.BlockSpec((1, b.shape[1] // halves, b.shape[2]), lambda q, r: (q, r, 0)) for b in blocks]
    return pl.pallas_call(
        body, name="pair_add", grid=(N_CHIP, halves),
        in_specs=g_specs + p_specs, out_specs=p_specs,
        out_shape=[jax.ShapeDtypeStruct(p.shape, BF16) for p in received],
        compiler_params=_cparams(("parallel", "parallel")),
    )(*[b.reshape((N_CHIP, 2) + b.shape[1:]) for b in blocks], *received)


FFN_PAIR = 2


def _pair_cols(w_ref):
    return jnp.concatenate([w_ref[p] for p in range(FFN_PAIR)], axis=1)


def ffn_fwd(x, nw, wg, wu, wd, ag=()):
    T, D = x.shape
    F = wg.shape[2]
    P = FFN_PAIR
    J = wg.shape[0] // P
    tm = min(T, 1024)

    def body(x_ref, nw_ref, wg_ref, wu_ref, wd_ref, o_ref, xn_ref, acc_ref):
        j = pl.program_id(1)

        @pl.when(j == 0)
        def _():
            xn_ref[...] = _rms(x_ref[...], nw_ref[...]).astype(BF16)
            acc_ref[...] = jnp.zeros_like(acc_ref)

        xn = xn_ref[...]
        g = _dot(xn, _pair_cols(wg_ref))
        u = _dot(xn, _pair_cols(wu_ref))
        h = (g * _sigmoid(g) * u).astype(BF16)
        acc_ref[...] += _dot(h, wd_ref[...].reshape(P * F, D))

        @pl.when(j == J - 1)
        def _():
            o_ref[...] = x_ref[...] + 0.5 * acc_ref[...]

    (out,), gathered, _ = _pcall(
        body, (x, nw, wg, wu, wd), name="ffn_fwd", grid=(T // tm, J),
        in_specs=[pl.BlockSpec((tm, D), lambda t, j: (t, 0)),
                  pl.BlockSpec((1, D), lambda t, j: (0, 0)),
                  pl.BlockSpec((P, D, F), lambda t, j: (j, 0, 0)),
                  pl.BlockSpec((P, D, F), lambda t, j: (j, 0, 0)),
                  pl.BlockSpec((P, F, D), lambda t, j: (j, 0, 0))],
        out_specs=[pl.BlockSpec((tm, D), lambda t, j: (t, 0))],
        out_shape=[jax.ShapeDtypeStruct((T, D), F32)],
        scratch_shapes=[pltpu.VMEM((tm, D), BF16), pltpu.VMEM((tm, D), F32)],
        sem=("arbitrary", "arbitrary"), ag=ag)
    return out, gathered


def ffn_bwd(x, dy, nw, wg, wu, wd, rs=()):
    T, D = x.shape
    F = wg.shape[2]
    P = FFN_PAIR
    J = wg.shape[0] // P
    tm = min(T, 256)
    nt = T // tm

    def body(x_ref, dy_ref, nw_ref, wg_ref, wu_ref, wd_ref,
             dx_ref, dwg_ref, dwu_ref, dwd_ref, dnw_ref,
             xn_ref, dyh_ref, dxn_ref, awg_ref, awu_ref, awd_ref):
        j = pl.program_id(0)
        t = pl.program_id(1)
        rows = pl.ds(pl.multiple_of(t * tm, tm), tm)

        @pl.when(j == 0)
        def _():
            xn_ref[rows, :] = _rms(x_ref[...], nw_ref[...]).astype(BF16)
            dyh_ref[rows, :] = (0.5 * dy_ref[...]).astype(BF16)
            dxn_ref[rows, :] = jnp.zeros((tm, D), F32)

        @pl.when((j == 0) & (t == 0))
        def _():
            dnw_ref[...] = jnp.zeros_like(dnw_ref)

        @pl.when(t == 0)
        def _():
            awg_ref[...] = jnp.zeros_like(awg_ref)
            awu_ref[...] = jnp.zeros_like(awu_ref)
            awd_ref[...] = jnp.zeros_like(awd_ref)

        xn = xn_ref[rows, :]
        dyh = dyh_ref[rows, :]
        wg2, wu2 = _pair_cols(wg_ref), _pair_cols(wu_ref)
        g = _dot(xn, wg2)
        u = _dot(xn, wu2)
        sg = _sigmoid(g)
        s = g * sg
        h = (s * u).astype(BF16)
        dh = _dot_nt(dyh, wd_ref[...].reshape(P * F, D))
        du = (dh * s).astype(BF16)
        dg = (dh * u * (sg * (1.0 + g * (1.0 - sg)))).astype(BF16)
        awd_ref[...] += _dot_tn(h, dyh)
        awg_ref[...] += _dot_tn(dg, xn)
        awu_ref[...] += _dot_tn(du, xn)
        dxn_ref[rows, :] += _dot_nt(dg, wg2) + _dot_nt(du, wu2)

        @pl.when(t == nt - 1)
        def _():
            dwg_ref[...] = awg_ref[...].astype(BF16).reshape(P, F, D)
            dwu_ref[...] = awu_ref[...].astype(BF16).reshape(P, F, D)
            dwd_ref[...] = awd_ref[...].astype(BF16).reshape(P, F, D)

        @pl.when(j == J - 1)
        def _():
            dx, dnw = _rms_bwd(x_ref[...], nw_ref[...], dxn_ref[rows, :])
            dx_ref[...] = dy_ref[...] + dx
            dnw_ref[...] += dnw

    ends = lambda j, t: (jnp.where((j == 0) | (j == J - 1), t, 0), 0)
    last = lambda j, t: (jnp.where(j == J - 1, t, 0), 0)
    outs, _, slots = _pcall(
        body, (x, dy, nw, wg, wu, wd), name="ffn_bwd", grid=(J, nt),
        in_specs=[pl.BlockSpec((tm, D), ends), pl.BlockSpec((tm, D), ends),
                  pl.BlockSpec((1, D), lambda j, t: (0, 0)),
                  pl.BlockSpec((P, D, F), lambda j, t: (j, 0, 0)),
                  pl.BlockSpec((P, D, F), lambda j, t: (j, 0, 0)),
                  pl.BlockSpec((P, F, D), lambda j, t: (j, 0, 0))],
        out_specs=[pl.BlockSpec((tm, D), last),
                   pl.BlockSpec((P, F, D), lambda j, t: (j, 0, 0)),
                   pl.BlockSpec((P, F, D), lambda j, t: (j, 0, 0)),
                   pl.BlockSpec((P, F, D), lambda j, t: (j, 0, 0)),
                   pl.BlockSpec((1, D), lambda j, t: (0, 0))],
        out_shape=[jax.ShapeDtypeStruct((T, D), F32)] + [jax.ShapeDtypeStruct((P * J, F, D), BF16)] * 3
        + [jax.ShapeDtypeStruct((1, D), F32)],
        scratch_shapes=[pltpu.VMEM((T, D), BF16), pltpu.VMEM((T, D), BF16), pltpu.VMEM((T, D), F32)]
        + [pltpu.VMEM((P * F, D), F32)] * 3,
        sem=("arbitrary", "arbitrary"), rs=rs)
    return outs, slots


def rmslin_fwd(x, nw, w, b):
    T, D = x.shape
    N = w.shape[1]
    tm = min(T, 256)

    def body(x_ref, nw_ref, w_ref, b_ref, o_ref):
        xn = _rms(x_ref[...], nw_ref[...]).astype(BF16)
        o_ref[...] = _dot(xn, w_ref[...]) + b_ref[...]

    return pl.pallas_call(
        body, name="rmslin_fwd", grid=(T // tm,),
        in_specs=[pl.BlockSpec((tm, D), lambda t: (t, 0)), pl.BlockSpec((1, D), lambda t: (0, 0)),
                  pl.BlockSpec((D, N), lambda t: (0, 0)), pl.BlockSpec((1, N), lambda t: (0, 0))],
        out_specs=pl.BlockSpec((tm, N), lambda t: (t, 0)),
        out_shape=jax.ShapeDtypeStruct((T, N), F32),
        compiler_params=_cparams(("parallel",)),
    )(x, nw, w, b)


def rmslin_bwd(x, dres, dproj, nw, w):
    T, D = x.shape
    N = w.shape[1]
    nb = 1024
    nc = N // nb
    tm = min(T, 256)
    nt = T // tm

    def body(x_ref, dres_ref, dp_ref, nw_ref, w_ref, dx_ref, dw_ref, db_ref, dnw_ref, xn_ref, dxn_ref, acc_ref):
        c = pl.program_id(0)
        t = pl.program_id(1)
        rows = pl.ds(pl.multiple_of(t * tm, tm), tm)

        @pl.when(c == 0)
        def _():
            xn_ref[rows, :] = _rms(x_ref[...], nw_ref[...]).astype(BF16)
            dxn_ref[rows, :] = jnp.zeros((tm, D), F32)

        @pl.when((c == 0) & (t == 0))
        def _():
            dnw_ref[...] = jnp.zeros_like(dnw_ref)

        @pl.when(t == 0)
        def _():
            acc_ref[...] = jnp.zeros_like(acc_ref)
            db_ref[...] = jnp.zeros_like(db_ref)

        dpf = dp_ref[...]
        dp = dpf.astype(BF16)
        acc_ref[...] += _dot_tn(xn_ref[rows, :], dp)
        db_ref[...] += jnp.sum(dpf, axis=0, keepdims=True)
        dxn_ref[rows, :] += _dot_nt(dp, w_ref[...])

        @pl.when(t == nt - 1)
        def _():
            dw_ref[...] = acc_ref[...].astype(BF16)

        @pl.when(c == nc - 1)
        def _():
            dx, dnw = _rms_bwd(x_ref[...], nw_ref[...], dxn_ref[rows, :])
            dx_ref[...] = dres_ref[...] + dx
            dnw_ref[...] += dnw

    ends = lambda c, t: (jnp.where((c == 0) | (c == nc - 1), t, 0), 0)
    last = lambda c, t: (jnp.where(c == nc - 1, t, 0), 0)
    return pl.pallas_call(
        body, name="rmslin_bwd", grid=(nc, nt),
        in_specs=[pl.BlockSpec((tm, D), ends), pl.BlockSpec((tm, D), last),
                  pl.BlockSpec((tm, nb), lambda c, t: (t, c)),
                  pl.BlockSpec((1, D), lambda c, t: (0, 0)),
                  pl.BlockSpec((D, nb), lambda c, t: (0, c))],
        out_specs=[pl.BlockSpec((tm, D), last),
                   pl.BlockSpec((D, nb), lambda c, t: (0, c)),
                   pl.BlockSpec((1, nb), lambda c, t: (0, c)),
                   pl.BlockSpec((1, D), lambda c, t: (0, 0))],
        out_shape=[jax.ShapeDtypeStruct((T, D), F32), jax.ShapeDtypeStruct((D, N), BF16),
                   jax.ShapeDtypeStruct((1, N), F32), jax.ShapeDtypeStruct((1, D), F32)],
        scratch_shapes=[pltpu.VMEM((T, D), BF16), pltpu.VMEM((T, D), F32), pltpu.VMEM((D, nb), F32)],
        compiler_params=_cparams(("arbitrary", "arbitrary")),
    )(x, dres, dproj, nw, w)


def lin_fwd(res, parts, w, b):
    T = res.shape[0]
    K, N = w.shape
    tm = min(T, 512)
    n = len(parts)
    offs = [sum(p.shape[1] for p in parts[:i]) for i in range(n + 1)]

    def body(res_ref, *refs):
        a_refs, (w_ref, b_ref, o_ref) = refs[:n], refs[n:]
        acc = res_ref[...] + b_ref[...]
        for i, a_ref in enumerate(a_refs):
            acc = acc + _dot(a_ref[...].astype(BF16), w_ref[offs[i]:offs[i + 1], :])
        o_ref[...] = acc

    return pl.pallas_call(
        body, name="lin_fwd", grid=(T // tm,),
        in_specs=[pl.BlockSpec((tm, N), lambda t: (t, 0))]
        + [pl.BlockSpec((tm, p.shape[1]), lambda t: (t, 0)) for p in parts]
        + [pl.BlockSpec((K, N), lambda t: (0, 0)), pl.BlockSpec((1, N), lambda t: (0, 0))],
        out_specs=pl.BlockSpec((tm, N), lambda t: (t, 0)),
        out_shape=jax.ShapeDtypeStruct((T, N), F32),
        compiler_params=_cparams(("parallel",)),
    )(res, *parts, w, b)


def lin_bwd(parts, dy, w):
    T = dy.shape[0]
    K, N = w.shape
    tm = min(T, 256)
    n = len(parts)
    offs = [sum(p.shape[1] for p in parts[:i]) for i in range(n + 1)]

    def body(*refs):
        a_refs, (dy_ref, w_ref, da_ref, dw_ref, db_ref) = refs[:n], refs[n:]

        @pl.when(pl.program_id(0) == 0)
        def _():
            dw_ref[...] = jnp.zeros_like(dw_ref)
            db_ref[...] = jnp.zeros_like(db_ref)

        dyf = dy_ref[...]
        dyb = dyf.astype(BF16)
        da_ref[...] = _dot_nt(dyb, w_ref[...])
        for i, a_ref in enumerate(a_refs):
            dw_ref[offs[i]:offs[i + 1], :] += _dot_tn(a_ref[...].astype(BF16), dyb)
        db_ref[...] += jnp.sum(dyf, axis=0, keepdims=True)

    return pl.pallas_call(
        body, name="lin_bwd", grid=(T // tm,),
        in_specs=[pl.BlockSpec((tm, p.shape[1]), lambda t: (t, 0)) for p in parts]
        + [pl.BlockSpec((tm, N), lambda t: (t, 0)), pl.BlockSpec((K, N), lambda t: (0, 0))],
        out_specs=[pl.BlockSpec((tm, K), lambda t: (t, 0)), pl.BlockSpec((K, N), lambda t: (0, 0)),
                   pl.BlockSpec((1, N), lambda t: (0, 0))],
        out_shape=[jax.ShapeDtypeStruct((T, K), F32), jax.ShapeDtypeStruct((K, N), F32),
                   jax.ShapeDtypeStruct((1, N), F32)],
        compiler_params=_cparams(("arbitrary",)),
    )(*parts, dy, w)


def loss_fwd_bwd(x, fw, target):
    T, D = x.shape
    tm = min(T, 256)

    def body(x_ref, fw_ref, tg_ref, loss_ref, dx_ref, dfw_ref):
        @pl.when(pl.program_id(0) == 0)
        def _():
            loss_ref[...] = jnp.zeros_like(loss_ref)
            dfw_ref[...] = jnp.zeros_like(dfw_ref)

        xv = x_ref[...]
        w = fw_ref[...]
        err = _rms(xv, w) - tg_ref[...]
        row = jnp.sum(err * err, axis=-1, keepdims=True)
        loss_ref[...] += (0.5 / D) * jnp.sum(row, axis=0, keepdims=True)
        dx, dfw = _rms_bwd(xv, w, err * (1.0 / D))
        dx_ref[...] = dx
        dfw_ref[...] += dfw

    return pl.pallas_call(
        body, name="loss_fwd_bwd", grid=(T // tm,),
        in_specs=[pl.BlockSpec((tm, D), lambda t: (t, 0)), pl.BlockSpec((1, D), lambda t: (0, 0)),
                  pl.BlockSpec((tm, D), lambda t: (t, 0))],
        out_specs=[pl.BlockSpec((1, 1), lambda t: (0, 0)), pl.BlockSpec((tm, D), lambda t: (t, 0)),
                   pl.BlockSpec((1, D), lambda t: (0, 0))],
        out_shape=[jax.ShapeDtypeStruct((1, 1), F32), jax.ShapeDtypeStruct((T, D), F32),
                   jax.ShapeDtypeStruct((1, D), F32)],
        compiler_params=_cparams(("arbitrary",)),
    )(x, fw, target)


def _attn_masks(n, rows, blk):
    r = lax.broadcasted_iota(jnp.int32, (rows, 2 * blk), 0)
    jj = lax.broadcasted_iota(jnp.int32, (rows, 2 * blk), 1)
    dist = (r % blk) + blk - jj
    valid = (dist >= 0) & (dist < blk) & ((n > 0) | (jj >= blk))
    return dist.astype(F32), valid


def _attn_block(q, kcat, vcat, sink, slope, dist, valid):
    d = q.shape[-1]
    s = _dot_nt(q.astype(BF16), kcat.astype(BF16)) * (d ** -0.5)
    s = jnp.where(valid, s - slope * dist, -1e30)
    m = lax.stop_gradient(jnp.maximum(jnp.max(s, axis=-1, keepdims=True), sink))
    e = jnp.exp(s - m)
    p = e / (jnp.sum(e, axis=-1, keepdims=True) + jnp.exp(sink - m))
    return _dot(p.astype(BF16), vcat.astype(BF16))


ATTN_G = ATTN_HEADS // ATTN_KV_HEADS
ATTN_QW = ATTN_G * HEAD_DIM
ATTN_KCOL = Q_A // KV_A


def _attn_specs():
    blk = ATTN_BLOCK
    qs = pl.BlockSpec((blk, ATTN_QW), lambda h, n: (n, h))
    prev = lambda c: pl.BlockSpec((blk, KV_A), lambda h, n: (jnp.maximum(n - 1, 0), c))
    cur = lambda c: pl.BlockSpec((blk, KV_A), lambda h, n: (n, c))
    rowp = pl.BlockSpec((ATTN_G * blk, 1), lambda h, n: (h, 0))
    return qs, [prev(ATTN_KCOL), cur(ATTN_KCOL), prev(ATTN_KCOL + 1), cur(ATTN_KCOL + 1)], rowp


def _attn_operands(h, q_ref, kp_ref, kc_ref, vp_ref, vc_ref):
    d = HEAD_DIM
    q = jnp.concatenate([q_ref[:, g * d:(g + 1) * d] for g in range(ATTN_G)], axis=0)
    pick = lambda r: jnp.where(h == 0, r[:, :d], r[:, d:])
    kcat = jnp.concatenate([pick(kp_ref[...]), pick(kc_ref[...])], axis=0)
    vcat = jnp.concatenate([pick(vp_ref[...]), pick(vc_ref[...])], axis=0)
    return q, kcat, vcat


def attn_fwd(proj, sink_rows, slope_rows, ag=()):
    T = proj.shape[0]
    blk, d = ATTN_BLOCK, HEAD_DIM

    def body(q_ref, kp_ref, kc_ref, vp_ref, vc_ref, sink_ref, slope_ref, o_ref):
        h, n = pl.program_id(0), pl.program_id(1)
        dist, valid = _attn_masks(n, ATTN_G * blk, blk)
        q, kcat, vcat = _attn_operands(h, q_ref, kp_ref, kc_ref, vp_ref, vc_ref)
        o = _attn_block(q, kcat, vcat, sink_ref[...], slope_ref[...], dist, valid)
        for g in range(ATTN_G):
            o_ref[:, g * d:(g + 1) * d] = o[g * blk:(g + 1) * blk]

    qs, kv, rowp = _attn_specs()
    (out,), gathered, _ = _pcall(
        body, (proj, proj, proj, proj, proj, sink_rows, slope_rows), name="attn_fwd",
        grid=(ATTN_KV_HEADS, T // blk), in_specs=[qs] + kv + [rowp, rowp], out_specs=[qs],
        out_shape=[jax.ShapeDtypeStruct((T, Q_A), F32)], sem=("arbitrary", "arbitrary"), ag=ag)
    return out, gathered


def attn_bwd(proj, sink_rows, slope_rows, dmix, rs=()):
    T = proj.shape[0]
    blk, d = ATTN_BLOCK, HEAD_DIM

    def body(q_ref, kp_ref, kc_ref, vp_ref, vc_ref, sink_ref, slope_ref, do_ref, dq_ref, dkv_ref, dsink_ref):
        h, n = pl.program_id(0), pl.program_id(1)

        @pl.when((h == 0) & (n == 0))
        def _():
            dkv_ref[...] = jnp.zeros_like(dkv_ref)

        @pl.when(n == 0)
        def _():
            dsink_ref[...] = jnp.zeros_like(dsink_ref)

        dist, valid = _attn_masks(n, ATTN_G * blk, blk)
        q, kcat, vcat = _attn_operands(h, q_ref, kp_ref, kc_ref, vp_ref, vc_ref)
        do = jnp.concatenate([do_ref[:, g * d:(g + 1) * d] for g in range(ATTN_G)], axis=0)
        fn = functools.partial(_attn_block, slope=slope_ref[...], dist=dist, valid=valid)
        _, vjp = jax.vjp(fn, q, kcat, vcat, sink_ref[...])
        dq, dkcat, dvcat, dsink = vjp(do)
        for g in range(ATTN_G):
            dq_ref[:, g * d:(g + 1) * d] = dq[g * blk:(g + 1) * blk]
        dsink_ref[...] += dsink
        lane = lax.broadcasted_iota(jnp.int32, (2 * blk, 2 * KV_A), 1)
        mine = (lane % KV_A) // d == h
        both = jnp.where(mine, jnp.concatenate([dkcat, dkcat, dvcat, dvcat], axis=1), 0.0)

        @pl.when(n == 0)
        def _():
            dkv_ref[0:blk, :] += both[blk:]

        @pl.when(n > 0)
        def _():
            rows = pl.ds(pl.multiple_of((n - 1) * blk, blk), 2 * blk)
            dkv_ref[rows, :] += both

    qs, kv, rowp = _attn_specs()
    outs, _, slots = _pcall(
        body, (proj, proj, proj, proj, proj, sink_rows, slope_rows, dmix), name="attn_bwd",
        grid=(ATTN_KV_HEADS, T // blk), in_specs=[qs] + kv + [rowp, rowp, qs],
        out_specs=[qs, pl.BlockSpec((T, 2 * KV_A), lambda h, n: (0, 0)), rowp],
        out_shape=[jax.ShapeDtypeStruct((T, Q_A), F32), jax.ShapeDtypeStruct((T, 2 * KV_A), F32),
                   jax.ShapeDtypeStruct((ATTN_HEADS * blk, 1), F32)],
        sem=("arbitrary", "arbitrary"), rs=rs)
    return outs, slots


_NN = (((2,), (1,)), ((0,), (0,)))
_NT = (((2,), (2,)), ((0,), (0,)))
_TN = (((1,), (1,)), ((0,), (0,)))


def _bmm(a, b, dims):
    return lax.dot_general(a.astype(BF16), b.astype(BF16), dims, preferred_element_type=F32)


def _split(x, terms):
    out = []
    for _ in range(terms):
        t = x.astype(BF16)
        out.append(t)
        x = x - t.astype(F32)
    return out


def _fine_product(a, b, dims):
    (ah, al), (bh, bl) = _split(a, 2), _split(b, 2)
    dot = lambda x, y: lax.dot_general(x, y, dims, preferred_element_type=F32)
    return dot(ah, bh) + (dot(ah, bl) + dot(al, bh))


def _mask_product(mask, x, dims):
    mb = mask.astype(BF16)
    parts = [lax.dot_general(mb, t, dims, preferred_element_type=F32) for t in _split(x, 3)]
    return parts[0] + (parts[1] + parts[2])


@jax.custom_vjp
def _fine_nt(a, b):
    return _fine_product(a, b, _NT)


_fine_nt.defvjp(lambda a, b: (_fine_product(a, b, _NT), (a, b)),
                lambda res, ct: (_fine_product(ct, res[1], _NN), _fine_product(ct, res[0], _TN)))


@jax.custom_vjp
def _mask_nn(mask, x):
    return _mask_product(mask, x, _NN)


_mask_nn.defvjp(lambda mask, x: (_mask_product(mask, x, _NN), mask),
                lambda mask, ct: (jnp.zeros_like(mask), _mask_product(mask, ct, _TN)))


@jax.custom_vjp
def _unit_lower_inverse(low):
    n = low.shape[-1]
    eye = (lax.broadcasted_iota(jnp.int32, low.shape, 1) == lax.broadcasted_iota(jnp.int32, low.shape, 2)).astype(F32)
    tinv = eye - low
    p = low
    for _ in range(n.bit_length() - 2):
        p = _bmm(p, p, _NN)
        tinv = tinv + _bmm(tinv, p, _NN)
    return tinv


def _unit_lower_inverse_fwd(low):
    tinv = _unit_lower_inverse(low)
    return tinv, tinv


_unit_lower_inverse.defvjp(_unit_lower_inverse_fwd, lambda tinv, ct: (-_bmm(_bmm(tinv, ct, _TN), tinv, _NT),))


def _dn_chunk(qc, kc, vc, zc, braw, araw, alog, dtb, nw, S):
    H, C, D = qc.shape
    row = lax.broadcasted_iota(jnp.int32, (H, C, C), 1)
    col = lax.broadcasted_iota(jnp.int32, (H, C, C), 2)
    causal = row >= col
    strict = row > col
    eye = (row == col).astype(F32)

    q = qc * lax.rsqrt(jnp.sum(qc * qc, axis=-1, keepdims=True) + EPS) * (D ** -0.5)
    k = kc * lax.rsqrt(jnp.sum(kc * kc, axis=-1, keepdims=True) + EPS)
    beta = _sigmoid(braw)
    g = -jnp.exp(alog) * _softplus(araw + dtb)
    a_col = _mask_nn(causal.astype(F32), jnp.broadcast_to(g, (H, C, C)))
    a_row = _mask_nn(jnp.ones((H, C, C), F32), eye * a_col)
    decay = jnp.where(causal, jnp.exp(jnp.where(causal, a_col - a_row, 0.0)), 0.0)
    kb = k * beta
    tinv = _unit_lower_inverse(jnp.where(strict, _fine_nt(kb, k) * decay, 0.0))
    e_col = jnp.exp(a_col)
    u = _bmm(tinv, vc * beta, _NN)
    w = _bmm(tinv, kb * e_col, _NN)
    attn = _fine_nt(q, k) * decay
    gl = a_col[:, C - 1:C, :]
    k_dec = k * jnp.exp(gl - a_col)
    v_new = u - _bmm(w, S, _NN)
    o = _bmm(q * e_col, S, _NN) + _bmm(attn, v_new, _NN)
    s_new = S * jnp.exp(jnp.broadcast_to(gl, (H, D, D))) + _bmm(k_dec, v_new, _TN)
    on = o * lax.rsqrt(jnp.mean(o * o, axis=-1, keepdims=True) + EPS) * nw
    return on * (zc * _sigmoid(zc)), s_new


DN_ZCOLS = IN_COLS_PAD - OFF_Z
DN_ZBLK = OFF_Z // DN_ZCOLS


def _dn_heads(a, off):
    return jnp.stack([a[:, off + h * DN_D:off + (h + 1) * DN_D] for h in range(DN_HEADS)])


def _dn_gate_cols(zb, off):
    return jnp.stack([zb[:, off + h:off + h + 1] for h in range(DN_HEADS)])


DN_STEP_CHUNKS = 4


def _dn_operands(x_ref, zb_ref, rows):
    x, zb = x_ref[rows, :], zb_ref[rows, :]
    return (_dn_heads(x, 0), _dn_heads(x, V_B), _dn_heads(x, 2 * V_B), _dn_heads(zb, 0),
            _dn_gate_cols(zb, V_B), _dn_gate_cols(zb, V_B + DN_HEADS))


def dn_fwd(qkvc, proj, alog, dtb, nw, ag=()):
    T = qkvc.shape[0]
    H, C, D, G = DN_HEADS, DN_CHUNK, DN_D, DN_STEP_CHUNKS
    N = T // C

    def body(x_ref, zb_ref, alog_ref, dtb_ref, nw_ref, o_ref, sall_ref, s_ref):
        @pl.when(pl.program_id(0) == 0)
        def _():
            s_ref[...] = jnp.zeros_like(s_ref)

        s = s_ref[...]
        for c in range(G):
            rows = slice(c * C, (c + 1) * C)
            sall_ref[c] = s
            on, s = _dn_chunk(*_dn_operands(x_ref, zb_ref, rows), alog_ref[...], dtb_ref[...], nw_ref[...], s)
            for h in range(H):
                o_ref[rows, h * D:(h + 1) * D] = on[h]
        s_ref[...] = s

    par = pl.BlockSpec((H, 1, 1), lambda n: (0, 0, 0))
    outs, gathered, _ = _pcall(
        body, (qkvc, proj, alog, dtb, nw), name="dn_fwd", grid=(N // G,),
        in_specs=[pl.BlockSpec((G * C, QKV_B), lambda n: (n, 0)),
                  pl.BlockSpec((G * C, DN_ZCOLS), lambda n: (n, DN_ZBLK)),
                  par, par, pl.BlockSpec((1, 1, D), lambda n: (0, 0, 0))],
        out_specs=[pl.BlockSpec((G * C, V_B), lambda n: (n, 0)), pl.BlockSpec((G, H, D, D), lambda n: (n, 0, 0, 0))],
        out_shape=[jax.ShapeDtypeStruct((T, V_B), F32), jax.ShapeDtypeStruct((N, H, D, D), F32)],
        scratch_shapes=[pltpu.VMEM((H, D, D), F32)], sem=("arbitrary",), ag=ag)
    return outs, gathered


def dn_bwd(qkvc, proj, alog, dtb, nw, sall, dmix, rs=()):
    T = qkvc.shape[0]
    H, C, D, G = DN_HEADS, DN_CHUNK, DN_D, DN_STEP_CHUNKS
    N = T // C // G

    def body(x_ref, zb_ref, alog_ref, dtb_ref, nw_ref, sall_ref, do_ref,
             dx_ref, dzb_ref, dalog_ref, ddtb_ref, dnw_ref, ds_ref):
        @pl.when(pl.program_id(0) == 0)
        def _():
            ds_ref[...] = jnp.zeros_like(ds_ref)
            dalog_ref[...] = jnp.zeros_like(dalog_ref)
            ddtb_ref[...] = jnp.zeros_like(ddtb_ref)
            dnw_ref[...] = jnp.zeros_like(dnw_ref)

        ds = ds_ref[...]
        lane = lax.broadcasted_iota(jnp.int32, (C, LANES), 1)
        for c in reversed(range(G)):
            rows = slice(c * C, (c + 1) * C)
            args = (*_dn_operands(x_ref, zb_ref, rows), alog_ref[...], dtb_ref[...], nw_ref[...], sall_ref[c])
            _, vjp = jax.vjp(_dn_chunk, *args)
            dq, dk, dv, dz, db, da, dalog, ddtb, dnw, ds = vjp((_dn_heads(do_ref[rows, :], 0), ds))
            for h in range(H):
                dx_ref[rows, h * D:(h + 1) * D] = dq[h]
                dx_ref[rows, V_B + h * D:V_B + (h + 1) * D] = dk[h]
                dx_ref[rows, 2 * V_B + h * D:2 * V_B + (h + 1) * D] = dv[h]
                dzb_ref[rows, h * D:(h + 1) * D] = dz[h]
            tail = jnp.zeros((C, LANES), F32)
            for h in range(H):
                tail = tail + jnp.where(lane == h, jnp.broadcast_to(db[h], (C, LANES)), 0.0)
                tail = tail + jnp.where(lane == H + h, jnp.broadcast_to(da[h], (C, LANES)), 0.0)
            dzb_ref[rows, V_B:V_B + LANES] = tail
            dzb_ref[rows, V_B + LANES:] = jnp.zeros((C, DN_ZCOLS - V_B - LANES), F32)
            dalog_ref[...] += dalog
            ddtb_ref[...] += ddtb
            dnw_ref[...] += dnw
        ds_ref[...] = ds

    par = pl.BlockSpec((H, 1, 1), lambda i: (0, 0, 0))
    nws = pl.BlockSpec((1, 1, D), lambda i: (0, 0, 0))
    outs, _, slots = _pcall(
        body, (qkvc, proj, alog, dtb, nw, sall, dmix), name="dn_bwd", grid=(N,),
        in_specs=[pl.BlockSpec((G * C, QKV_B), lambda i: (N - 1 - i, 0)),
                  pl.BlockSpec((G * C, DN_ZCOLS), lambda i: (N - 1 - i, DN_ZBLK)), par, par, nws,
                  pl.BlockSpec((G, H, D, D), lambda i: (N - 1 - i, 0, 0, 0)),
                  pl.BlockSpec((G * C, V_B), lambda i: (N - 1 - i, 1))],
        out_specs=[pl.BlockSpec((G * C, QKV_B), lambda i: (N - 1 - i, 0)),
                   pl.BlockSpec((G * C, DN_ZCOLS), lambda i: (N - 1 - i, 0)), par, par, nws],
        out_shape=[jax.ShapeDtypeStruct((T, QKV_B), F32), jax.ShapeDtypeStruct((T, DN_ZCOLS), F32)]
        + [jax.ShapeDtypeStruct((H, 1, 1), F32)] * 2 + [jax.ShapeDtypeStruct((1, 1, D), F32)],
        scratch_shapes=[pltpu.VMEM((H, D, D), F32)], sem=("arbitrary",), rs=rs)
    return outs, slots


def _conv_taps(buf_ref, w, width, halo, tm):
    acc = None
    for kk, win in _windows(buf_ref, [halo - (width - 1) + kk for kk in range(width)], tm):
        term = w[kk:kk + 1, :] * win
        acc = term if acc is None else acc + term
    return acc


def _windows(ref, offsets, tm):
    for res in range(SUBLANES):
        ks = [k for k, o in enumerate(offsets) if o % SUBLANES == res]
        if not ks:
            continue
        lo = min(offsets[k] for k in ks)
        hi = max(offsets[k] for k in ks)
        shifted = ref[pl.ds(lo, tm + hi - lo), :]
        for k in ks:
            yield k, shifted[offsets[k] - lo:offsets[k] - lo + tm]


def _conv_taps_bwd(dbuf_ref, w, width, tm):
    acc = None
    for kk, win in _windows(dbuf_ref, [width - 1 - kk for kk in range(width)], tm):
        term = w[kk:kk + 1, :] * win
        acc = term if acc is None else acc + term
    return acc


def _conv_dw_acc(dw_ref, dout, buf_ref, width, halo, tm):
    for kk, win in _windows(buf_ref, [halo - (width - 1) + kk for kk in range(width)], tm):
        dw_ref[pl.ds(kk, 1), :] += jnp.sum(dout * win, axis=0, keepdims=True)


DNC_HALO = 8
DNC_COLS = 768


def dnconv_fwd(proj, w):
    T = proj.shape[0]
    tm = min(T, 256)
    hb = tm // DNC_HALO

    def body(x_ref, h_ref, w_ref, o_ref, buf_ref):
        i = pl.program_id(0)
        buf_ref[0:DNC_HALO, :] = jnp.where(i > 0, h_ref[...], 0.0)
        buf_ref[DNC_HALO:, :] = x_ref[...]
        acc = _conv_taps(buf_ref, w_ref[...], DN_CONV, DNC_HALO, tm)
        o_ref[...] = acc * _sigmoid(acc)

    return pl.pallas_call(
        body, name="dnconv_fwd", grid=(T // tm, 2),
        in_specs=[pl.BlockSpec((tm, DNC_COLS), lambda i, c: (i, 1 + c)),
                  pl.BlockSpec((DNC_HALO, DNC_COLS), lambda i, c: (jnp.maximum(i * hb - 1, 0), 1 + c)),
                  pl.BlockSpec((DN_CONV, DNC_COLS), lambda i, c: (0, c))],
        out_specs=pl.BlockSpec((tm, DNC_COLS), lambda i, c: (i, c)),
        out_shape=jax.ShapeDtypeStruct((T, QKV_B), F32),
        scratch_shapes=[pltpu.VMEM((DNC_HALO + tm, DNC_COLS), F32)],
        compiler_params=_cparams(("parallel", "parallel")),
    )(proj, proj, w)


def dnconv_bwd(proj, w, dout):
    T = proj.shape[0]
    tm = min(T, 256)
    nt = T // tm
    hb = tm // DNC_HALO

    def body(x_ref, h_ref, w_ref, do_ref, dx_ref, dw_ref, buf_ref, dbuf_ref):
        r = pl.program_id(1)
        i = nt - 1 - r

        @pl.when(r == 0)
        def _():
            dw_ref[...] = jnp.zeros_like(dw_ref)
            dbuf_ref[tm:, :] = jnp.zeros((DNC_HALO, DNC_COLS), F32)

        buf_ref[0:DNC_HALO, :] = jnp.where(i > 0, h_ref[...], 0.0)
        buf_ref[DNC_HALO:, :] = x_ref[...]
        wv = w_ref[...]
        acc = _conv_taps(buf_ref, wv, DN_CONV, DNC_HALO, tm)
        sg = _sigmoid(acc)
        dacc = do_ref[...] * (sg * (1.0 + acc * (1.0 - sg)))
        dbuf_ref[0:tm, :] = dacc
        dx_ref[...] = _conv_taps_bwd(dbuf_ref, wv, DN_CONV, tm)
        _conv_dw_acc(dw_ref, dacc, buf_ref, DN_CONV, DNC_HALO, tm)
        dbuf_ref[tm:, :] = dacc[0:DNC_HALO, :]

    return pl.pallas_call(
        body, name="dnconv_bwd", grid=(2, nt),
        in_specs=[pl.BlockSpec((tm, DNC_COLS), lambda c, r: (nt - 1 - r, 1 + c)),
                  pl.BlockSpec((DNC_HALO, DNC_COLS), lambda c, r: (jnp.maximum((nt - 1 - r) * hb - 1, 0), 1 + c)),
                  pl.BlockSpec((DN_CONV, DNC_COLS), lambda c, r: (0, c)),
                  pl.BlockSpec((tm, DNC_COLS), lambda c, r: (nt - 1 - r, c))],
        out_specs=[pl.BlockSpec((tm, DNC_COLS), lambda c, r: (nt - 1 - r, c)),
                   pl.BlockSpec((DN_CONV, DNC_COLS), lambda c, r: (0, c))],
        out_shape=[jax.ShapeDtypeStruct((T, QKV_B), F32), jax.ShapeDtypeStruct((DN_CONV, QKV_B), F32)],
        scratch_shapes=[pltpu.VMEM((DNC_HALO + tm, DNC_COLS), F32), pltpu.VMEM((tm + DNC_HALO, DNC_COLS), F32)],
        compiler_params=_cparams(("parallel", "arbitrary")),
    )(proj, proj, w, dout)


CV_HALO = 32


def _cv_post(cv, lnw, lnb):
    mu = jnp.mean(cv, axis=-1, keepdims=True)
    xc = cv - mu
    y = xc * lax.rsqrt(jnp.mean(xc * xc, axis=-1, keepdims=True) + EPS) * lnw + lnb
    return y * _sigmoid(y)


def cv_fwd(ab, w, bdw, lnw, lnb, ag=()):
    T = ab.shape[0]
    D = ab.shape[1] // 2
    tm = min(T, 256)
    hb = tm // CV_HALO

    def body(a_ref, b_ref, ah_ref, bh_ref, w_ref, bdw_ref, lnw_ref, lnb_ref, o_ref, cv_ref, buf_ref):
        i = pl.program_id(0)
        buf_ref[0:CV_HALO, :] = jnp.where(i > 0, ah_ref[...] * _sigmoid(bh_ref[...]), 0.0)
        buf_ref[CV_HALO:, :] = a_ref[...] * _sigmoid(b_ref[...])
        cv = _conv_taps(buf_ref, w_ref[...], CONV_WIDTH, CV_HALO, tm) + bdw_ref[...]
        cv_ref[...] = cv
        o_ref[...] = _cv_post(cv, lnw_ref[...], lnb_ref[...])

    halo = lambda c: pl.BlockSpec((CV_HALO, D), lambda i: (jnp.maximum(i * hb - 1, 0), c))
    vec = pl.BlockSpec((1, D), lambda i: (0, 0))
    tile = pl.BlockSpec((tm, D), lambda i: (i, 0))
    outs, gathered, _ = _pcall(
        body, (ab, ab, ab, ab, w, bdw, lnw, lnb), name="cv_fwd", grid=(T // tm,),
        in_specs=[tile, pl.BlockSpec((tm, D), lambda i: (i, 1)),
                  halo(0), halo(1), pl.BlockSpec((CONV_WIDTH, D), lambda i: (0, 0)), vec, vec, vec],
        out_specs=[tile, tile],
        out_shape=[jax.ShapeDtypeStruct((T, D), F32), jax.ShapeDtypeStruct((T, D), F32)],
        scratch_shapes=[pltpu.VMEM((CV_HALO + tm, D), F32)], sem=("arbitrary",), ag=ag)
    return outs, gathered


def cv_bwd(ab, cv, w, lnw, lnb, dout, rs=()):
    T = ab.shape[0]
    D = ab.shape[1] // 2
    tm = min(T, 256)
    nt = T // tm
    hb = tm // CV_HALO

    def body(a_ref, b_ref, ah_ref, bh_ref, cv_ref, w_ref, lnw_ref, lnb_ref, do_ref,
             da_ref, db_ref, dw_ref, dbdw_ref, dlnw_ref, dlnb_ref, buf_ref, dbuf_ref):
        r = pl.program_id(0)
        i = nt - 1 - r

        @pl.when(r == 0)
        def _():
            dw_ref[...] = jnp.zeros_like(dw_ref)
            dbdw_ref[...] = jnp.zeros_like(dbdw_ref)
            dlnw_ref[...] = jnp.zeros_like(dlnw_ref)
            dlnb_ref[...] = jnp.zeros_like(dlnb_ref)
            dbuf_ref[tm:, :] = jnp.zeros((CV_HALO, D), F32)

        a = a_ref[...]
        sb = _sigmoid(b_ref[...])
        buf_ref[0:CV_HALO, :] = jnp.where(i > 0, ah_ref[...] * _sigmoid(bh_ref[...]), 0.0)
        buf_ref[CV_HALO:, :] = a * sb
        wv = w_ref[...]
        _, vjp = jax.vjp(_cv_post, cv_ref[...], lnw_ref[...], lnb_ref[...])
        dcv, dlnw, dlnb = vjp(do_ref[...])
        dlnw_ref[...] += dlnw
        dlnb_ref[...] += dlnb
        dbdw_ref[...] += jnp.sum(dcv, axis=0, keepdims=True)
        dbuf_ref[0:tm, :] = dcv
        du = _conv_taps_bwd(dbuf_ref, wv, CONV_WIDTH, tm)
        _conv_dw_acc(dw_ref, dcv, buf_ref, CONV_WIDTH, CV_HALO, tm)
        dbuf_ref[tm:, :] = dcv[0:CV_HALO, :]
        da_ref[...] = du * sb
        db_ref[...] = du * a * sb * (1.0 - sb)

    tile = lambda c: pl.BlockSpec((tm, D), lambda r: (nt - 1 - r, c))
    halo = lambda c: pl.BlockSpec((CV_HALO, D), lambda r: (jnp.maximum((nt - 1 - r) * hb - 1, 0), c))
    vec = pl.BlockSpec((1, D), lambda r: (0, 0))
    wsp = pl.BlockSpec((CONV_WIDTH, D), lambda r: (0, 0))
    (da, db, dw, dbdw, dlnw, dlnb), _, slots = _pcall(
        body, (ab, ab, ab, ab, cv, w, lnw, lnb, dout), name="cv_bwd", grid=(nt,),
        in_specs=[tile(0), tile(1), halo(0), halo(1), tile(0), wsp, vec, vec, tile(0)],
        out_specs=[tile(0), tile(0), wsp, vec, vec, vec],
        out_shape=[jax.ShapeDtypeStruct((T, D), F32), jax.ShapeDtypeStruct((T, D), F32),
                   jax.ShapeDtypeStruct((CONV_WIDTH, D), F32)] + [jax.ShapeDtypeStruct((1, D), F32)] * 3,
        scratch_shapes=[pltpu.VMEM((CV_HALO + tm, D), F32), pltpu.VMEM((tm + CV_HALO, D), F32)],
        sem=("arbitrary",), rs=rs)
    return (jnp.concatenate([da, db], axis=1), dw, dbdw, dlnw, dlnb), slots


def adamw(w, m, v, slots, rs=()):
    L, R, C = w.shape
    ns = slots[0].shape[0]
    fits = lambda r, c: ns * r * c * 2 <= ADAM_SLOT_BLOCK
    tiles = [(R, C)] if fits(R, C) else []
    tiles += [(d, C) for d in range(16, R, 16) if R % d == 0 and fits(d, C)]
    tiles += [(R, d) for d in range(LANES, C, LANES) if C % d == 0 and fits(R, d)]
    tr, tc = max(tiles, key=lambda t: t[0] * t[1])
    c1 = 1.0 / (1.0 - ADAM_B1 ** ADAM_STEP)
    c2 = 1.0 / (1.0 - ADAM_B2 ** ADAM_STEP)

    def body(w_ref, m_ref, v_ref, *rest):
        s_refs = rest[:L]
        g_ref, d_ref, nm_ref, nv_ref = rest[L:]
        l = pl.program_id(0)
        for k in range(L):
            @pl.when(l == k)
            def _(s_ref=s_refs[k]):
                g = s_ref[0].astype(F32)
                for j in range(1, ns):
                    g = g + s_ref[j].astype(F32)
                nm = ADAM_B1 * m_ref[0] + (1.0 - ADAM_B1) * g
                nv = ADAM_B2 * v_ref[0] + (1.0 - ADAM_B2) * (g * g)
                g_ref[0] = g
                nm_ref[0] = nm
                nv_ref[0] = nv
                d_ref[0] = -ADAM_LR * ((nm * c1) / (jnp.sqrt(nv * c2) + ADAM_EPS) + ADAM_WD * w_ref[0])

    nc = C // tc
    blk = pl.BlockSpec((1, tr, tc), lambda l, i: (l, i // nc, i % nc))
    slot = lambda k: pl.BlockSpec((ns, tr, tc), lambda l, i: (0, jnp.where(l == k, i // nc, 0),
                                                                 jnp.where(l == k, i % nc, 0)))
    outs, _, landed = _pcall(
        body, (w, m, v, *slots), name="adamw", grid=(L, (R // tr) * nc),
        in_specs=[blk, blk, blk] + [slot(k) for k in range(L)],
        out_specs=[blk, blk, blk, blk],
        out_shape=[jax.ShapeDtypeStruct((L, R, C), F32)] * 4,
        sem=("arbitrary", "arbitrary"), rs=rs)
    return outs, landed


def _unshard(g, axis):
    g = jnp.moveaxis(g, 0, axis)
    s = g.shape
    return g.reshape(s[:axis] + (s[axis] * s[axis + 1],) + s[axis + 2:])


def _to_blocks(full, axis):
    s = full.shape
    g = full.reshape(s[:axis] + (N_DEV, s[axis] // N_DEV) + s[axis + 1:])
    return jnp.moveaxis(g, axis, 0)


SMALL = (("norm_w", 2), ("dn_conv_w", 2), ("conv_b_pw1", 1), ("conv_w_dw", 2), ("conv_b_dw", 1),
         ("conv_ln_w", 1), ("conv_ln_b", 1), ("conv_b_pw2", 1),
         ("attn_sinks", None), ("dn_a_log", None), ("dn_dt_bias", None), ("dn_norm_w", None), ("final_norm_w", None))
SMALL_AXIS = dict(SMALL)


def _pack(parts):
    flat = jnp.concatenate([p.reshape(-1) for p in parts])
    pad = (-flat.shape[0]) % LANES
    return jnp.pad(flat, (0, pad))


def _unpack(flat, shapes):
    out, off = [], 0
    for s in shapes:
        n = int(np.prod(s))
        out.append(flat[off:off + n].reshape(s))
        off += n
    return out


def kernel(x, norm_w, ffn_w_gate, ffn_w_up, ffn_w_down, mix_w_in, dn_conv_w, attn_sinks, dn_a_log, dn_dt_bias, dn_norm_w, mix_w_out, conv_w_pw1, conv_b_pw1, conv_w_dw, conv_b_dw, conv_ln_w, conv_ln_b, conv_w_pw2, conv_b_pw2, final_norm_w, loss_target, m_norm_w, m_ffn_w_gate, m_ffn_w_up, m_ffn_w_down, m_mix_w_in, m_dn_conv_w, m_attn_sinks, m_dn_a_log, m_dn_dt_bias, m_dn_norm_w, m_mix_w_out, m_conv_w_pw1, m_conv_b_pw1, m_conv_w_dw, m_conv_b_dw, m_conv_ln_w, m_conv_ln_b, m_conv_w_pw2, m_conv_b_pw2, m_final_norm_w, v_norm_w, v_ffn_w_gate, v_ffn_w_up, v_ffn_w_down, v_mix_w_in, v_dn_conv_w, v_attn_sinks, v_dn_a_log, v_dn_dt_bias, v_dn_norm_w, v_mix_w_out, v_conv_w_pw1, v_conv_b_pw1, v_conv_w_dw, v_conv_b_dw, v_conv_ln_w, v_conv_ln_b, v_conv_w_pw2, v_conv_b_pw2, v_final_norm_w):
    W = dict(norm_w=norm_w, ffn_w_gate=ffn_w_gate, ffn_w_up=ffn_w_up, ffn_w_down=ffn_w_down, mix_w_in=mix_w_in,
             dn_conv_w=dn_conv_w, attn_sinks=attn_sinks, dn_a_log=dn_a_log, dn_dt_bias=dn_dt_bias,
             dn_norm_w=dn_norm_w, mix_w_out=mix_w_out, conv_w_pw1=conv_w_pw1, conv_b_pw1=conv_b_pw1,
             conv_w_dw=conv_w_dw, conv_b_dw=conv_b_dw, conv_ln_w=conv_ln_w, conv_ln_b=conv_ln_b,
             conv_w_pw2=conv_w_pw2, conv_b_pw2=conv_b_pw2, final_norm_w=final_norm_w)
    M = dict(norm_w=m_norm_w, ffn_w_gate=m_ffn_w_gate, ffn_w_up=m_ffn_w_up, ffn_w_down=m_ffn_w_down,
             mix_w_in=m_mix_w_in, dn_conv_w=m_dn_conv_w, attn_sinks=m_attn_sinks, dn_a_log=m_dn_a_log,
             dn_dt_bias=m_dn_dt_bias, dn_norm_w=m_dn_norm_w, mix_w_out=m_mix_w_out, conv_w_pw1=m_conv_w_pw1,
             conv_b_pw1=m_conv_b_pw1, conv_w_dw=m_conv_w_dw, conv_b_dw=m_conv_b_dw, conv_ln_w=m_conv_ln_w,
             conv_ln_b=m_conv_ln_b, conv_w_pw2=m_conv_w_pw2, conv_b_pw2=m_conv_b_pw2, final_norm_w=m_final_norm_w)
    V = dict(norm_w=v_norm_w, ffn_w_gate=v_ffn_w_gate, ffn_w_up=v_ffn_w_up, ffn_w_down=v_ffn_w_down,
             mix_w_in=v_mix_w_in, dn_conv_w=v_dn_conv_w, attn_sinks=v_attn_sinks, dn_a_log=v_dn_a_log,
             dn_dt_bias=v_dn_dt_bias, dn_norm_w=v_dn_norm_w, mix_w_out=v_mix_w_out, conv_w_pw1=v_conv_w_pw1,
             conv_b_pw1=v_conv_b_pw1, conv_w_dw=v_conv_w_dw, conv_b_dw=v_conv_b_dw, conv_ln_w=v_conv_ln_w,
             conv_ln_b=v_conv_ln_b, conv_w_pw2=v_conv_w_pw2, conv_b_pw2=v_conv_b_pw2, final_norm_w=v_final_norm_w)

    T, D = x.shape[1], x.shape[2]
    xs = x[0]

    big = ("ffn_w_gate", "ffn_w_up", "ffn_w_down", "mix_w_in", "mix_w_out", "conv_w_pw1", "conv_w_pw2")
    shard3 = {k: W[k].reshape((-1,) + W[k].shape[-2:]) for k in big}
    shard_bf = {k: shard3[k].astype(BF16) for k in big}
    ffn_unit = lambda i: [("ffn_w_gate", i), ("ffn_w_up", i), ("ffn_w_down", i)]
    even_unit = lambda e: [("mix_w_in", e), ("mix_w_out", e)]
    odd_unit = lambda e: [("conv_w_pw1", e), ("conv_w_pw2", e)]
    have = {}

    def ag_jobs(units):
        return [(shard_bf[k], i) for k, i in units]

    def ag_done(units, gathered):
        have.update(zip(units, gathered))

    small_sharded = [(k, ax) for k, ax in SMALL if ax is not None]
    small_pack = _pack([W[k] for k, _ in small_sharded])[None, :]
    first_units = ffn_unit(0)
    gathered = exchange(ag_jobs(first_units) + [(small_pack, None)])
    ag_done(first_units, gathered[:-1])
    small_full = {}
    for (k, ax), parts in zip(small_sharded,
                              zip(*[_unpack(gathered[-1][s, 0], [W[k].shape for k, _ in small_sharded])
                                    for s in range(N_DEV)])):
        small_full[k] = _unshard(jnp.stack(parts), ax)
    nw_full = small_full["norm_w"]

    ffn_w = lambda i: [have[u] for u in ffn_unit(i)]
    w_in_of = lambda e: jnp.pad(_unshard(have[("mix_w_in", e)], 1), ((0, 0), (0, IN_COLS_PAD - IN_COLS)))
    w_out_of = lambda e: have[("mix_w_out", e)].reshape(D, D)
    w_pw1_of = lambda e: _unshard(have[("conv_w_pw1", e)], 1)
    w_pw2_of = lambda e: have[("conv_w_pw2", e)].reshape(D, D)
    fwd_order, needed = [], {}
    for l in range(DEPTH):
        mixer = [("A", l), ("E", l)] if l % 2 == 0 else [("O", l)]
        fwd_order += [("F", 2 * l)] + mixer + [("F", 2 * l + 1)]
        needed[("F", 2 * l)], needed[("F", 2 * l + 1)] = ffn_unit(2 * l), ffn_unit(2 * l + 1)
        needed[mixer[0]] = even_unit(l // 2) if l % 2 == 0 else odd_unit(l // 2)
    queue = [(u, pos) for pos, key in enumerate(fwd_order) for u in needed.get(key, []) if u not in first_units]
    unit_bytes = lambda u: N_DEV * shard_bf[u[0]][u[1]].size * 2
    fwd_carry, at = {}, 0
    for pos, key in enumerate(fwd_order):
        cap = FWD_CARRY_BYTES[key[0]]
        taken, used = [], 0
        while at < len(queue) and (queue[at][1] <= pos + 1 or used + unit_bytes(queue[at][0]) <= cap):
            taken.append(queue[at][0])
            used += unit_bytes(queue[at][0])
            at += 1
        fwd_carry[key] = taken
    zero_in = jnp.zeros((1, IN_COLS_PAD), F32)
    zero_d = jnp.zeros((1, D), F32)
    slope_rows = jnp.asarray(np.repeat(2.0 ** (-8.0 * np.arange(1, ATTN_HEADS + 1) / ATTN_HEADS), ATTN_BLOCK)
                             .astype(np.float32)[:, None])

    saved = []
    h = xs
    w_in, w_out, w_pw1, w_pw2 = {}, {}, {}, {}

    def ffn_forward(h, l, half):
        i = 2 * l + half
        units = fwd_carry.get(("F", i), [])
        h, gathered = ffn_fwd(h, nw_full[l, 2 * half][None], *ffn_w(i), ag=ag_jobs(units))
        ag_done(units, gathered)
        return h

    for l in range(DEPTH):
        e = l // 2
        st = {"x0": h}
        h = ffn_forward(h, l, 0)
        st["x1"] = h
        if l % 2 == 0:
            w_in[e], w_out[e] = w_in_of(e), w_out_of(e)
            proj = rmslin_fwd(h, nw_full[l, 1][None], w_in[e], zero_in)
            st["proj"] = proj
            st["qkvc"] = dnconv_fwd(proj, small_full["dn_conv_w"][e])
            st["sink_rows"] = jnp.repeat(attn_sinks[e], ATTN_BLOCK)[:, None]
            st["alog"] = dn_a_log[e].reshape(DN_HEADS, 1, 1)
            st["dtb"] = dn_dt_bias[e].reshape(DN_HEADS, 1, 1)
            st["dnw"] = dn_norm_w[e].reshape(1, 1, DN_D)
            units = fwd_carry[("A", l)]
            st["att"], gathered = attn_fwd(proj, st["sink_rows"], slope_rows, ag=ag_jobs(units))
            ag_done(units, gathered)
            units = fwd_carry[("E", l)]
            (st["og"], st["sall"]), gathered = dn_fwd(st["qkvc"], proj, st["alog"], st["dtb"], st["dnw"],
                                                      ag=ag_jobs(units))
            ag_done(units, gathered)
            h = lin_fwd(h, [st["att"], st["og"]], w_out[e], zero_d)
        else:
            units = fwd_carry[("O", l)]
            w_pw1[e], w_pw2[e] = w_pw1_of(e), w_pw2_of(e)
            st["ab"] = rmslin_fwd(h, nw_full[l, 1][None], w_pw1[e], small_full["conv_b_pw1"][e][None])
            (st["act"], st["cv"]), gathered = cv_fwd(st["ab"], small_full["conv_w_dw"][e], small_full["conv_b_dw"][e][None],
                                         small_full["conv_ln_w"][e][None], small_full["conv_ln_b"][e][None],
                                         ag=ag_jobs(units))
            ag_done(units, gathered)
            h = lin_fwd(h, [st["act"]], w_pw2[e], small_full["conv_b_pw2"][e][None])
        st["x2"] = h
        h = ffn_forward(h, l, 1)
        saved.append(st)

    loss_part, dh, dfinal = loss_fwd_bwd(h, final_norm_w[None], loss_target[0])
    loss = lax.psum(loss_part[0, 0], ("x", "y", "c"))

    d_norm = [[None] * 3 for _ in range(DEPTH)]
    d_small = {k: [None, None] for k in ("dn_conv_w", "conv_b_pw1", "conv_w_dw", "conv_b_dw", "conv_ln_w",
                                         "conv_ln_b", "conv_b_pw2", "attn_sinks", "dn_a_log", "dn_dt_bias",
                                         "dn_norm_w")}
    pending, slot = [], {}

    def take_jobs(cap=None, only=None):
        taken = [p for p in pending if p[1] == "swap"]
        used = 0
        for p in pending:
            if p[1] == "swap" or (only is not None and p[0][0] not in only):
                continue
            if cap is not None and used + p[2].size * 2 > cap:
                break
            taken.append(p)
            used += p[2].size * 2
        pending[:] = [p for p in pending if all(p is not t for t in taken)]
        return taken, [(kind, arr) for _, kind, arr in taken]

    def land(taken, results):
        swapped = [(unit, arr, res) for (unit, kind, arr), res in zip(taken, results) if kind == "swap"]
        slot.update({unit: res for (unit, kind, _), res in zip(taken, results) if kind != "swap"})
        if swapped:
            sums = pair_add([g for _, g, _ in swapped], [r for _, _, r in swapped])
            pending.extend((unit, "chips", h) for (unit, _, _), h in zip(swapped, sums))

    def ffn_backward(dh, l, half):
        i = 2 * l + half
        taken, jobs = take_jobs(BWD_CARRY_BYTES["F"])
        (dh, dg, du, dd, d_norm[l][2 * half]), results = ffn_bwd(
            st["x2" if half else "x0"], dh, nw_full[l, 2 * half][None], *ffn_w(i), rs=jobs)
        land(taken, results)
        pending.extend((u, "swap", g) for u, g in zip(ffn_unit(i), (dg, du, dd)))
        return dh

    for l in reversed(range(DEPTH)):
        e = l // 2
        st = saved[l]
        dh = ffn_backward(dh, l, 1)
        if l % 2 == 0:
            dmix, d_out, _ = lin_bwd([st["att"], st["og"]], dh, w_out[e])
            pending.append((("mix_w_out", e), "direct", d_out.reshape(N_DEV, D // N_DEV, D).astype(BF16)))
            taken, jobs = take_jobs(BWD_CARRY_BYTES["E"])
            (dqkvc, dzba, dalog, ddtb, ddnw), results = dn_bwd(
                st["qkvc"], st["proj"], st["alog"], st["dtb"], st["dnw"], st["sall"], dmix, rs=jobs)
            land(taken, results)
            taken, jobs = take_jobs(BWD_CARRY_BYTES["A"])
            (dqa, dkva, dsink), results = attn_bwd(st["proj"], st["sink_rows"], slope_rows, dmix, rs=jobs)
            land(taken, results)
            dqkv, d_small["dn_conv_w"][e] = dnconv_bwd(st["proj"], small_full["dn_conv_w"][e], dqkvc)
            dproj = jnp.concatenate([dqa, dkva, dqkv, dzba], axis=1)
            dh, d_in, _, d_norm[l][1] = rmslin_bwd(st["x1"], dh, dproj, nw_full[l, 1][None], w_in[e])
            pending.append((("mix_w_in", e), "direct", _to_blocks(d_in[:, :IN_COLS], 1)))
            d_small["attn_sinks"][e] = jnp.sum(dsink.reshape(ATTN_HEADS, ATTN_BLOCK), axis=1)
            d_small["dn_a_log"][e] = dalog.reshape(DN_HEADS)
            d_small["dn_dt_bias"][e] = ddtb.reshape(DN_HEADS)
            d_small["dn_norm_w"][e] = ddnw.reshape(DN_D)
        else:
            dact, d_pw2, d_small["conv_b_pw2"][e] = lin_bwd([st["act"]], dh, w_pw2[e])
            pending.append((("conv_w_pw2", e), "direct", d_pw2.reshape(N_DEV, D // N_DEV, D).astype(BF16)))
            taken, jobs = take_jobs(BWD_CARRY_BYTES["O"])
            (dab, d_small["conv_w_dw"][e], d_small["conv_b_dw"][e], d_small["conv_ln_w"][e],
             d_small["conv_ln_b"][e]), results = cv_bwd(
                st["ab"], st["cv"], small_full["conv_w_dw"][e],
                small_full["conv_ln_w"][e][None], small_full["conv_ln_b"][e][None], dact, rs=jobs)
            land(taken, results)
            dh, d_pw1, d_small["conv_b_pw1"][e], d_norm[l][1] = rmslin_bwd(
                st["x1"], dh, dab, nw_full[l, 1][None], w_pw1[e])
            pending.append((("conv_w_pw1", e), "direct", _to_blocks(d_pw1, 1)))
        dh = ffn_backward(dh, l, 0)
    grad_x = dh[None]

    full_small = {"norm_w": jnp.stack([jnp.concatenate(r, axis=0) for r in d_norm]),
                  "final_norm_w": dfinal[0]}
    for k, pair in d_small.items():
        full_small[k] = jnp.stack([p.reshape(W[k].shape[1:-1] + (-1,)) if SMALL_AXIS[k] is not None
                                   else p for p in pair])
    rows = []
    for s in range(N_DEV):
        parts = [_to_blocks(full_small[k], ax)[s] if ax is not None else full_small[k] for k, ax in SMALL]
        rows.append(_pack(parts))
    send_small = jnp.stack(rows)[:, None, :]
    pending.append((("small", 0), "direct", send_small))

    res = {}
    waiting = lambda k: [p for p in pending if p[0][0] == k]
    adam_order = sorted(big, key=lambda k: len(waiting(k))) + ["small"]
    for n, k in enumerate(adam_order[:-1]):
        nxt = next((kk for kk in adam_order[n + 1:] if waiting(kk)), "small")
        taken, jobs = take_jobs(only=(nxt, "small") if n == 1 else (nxt,))
        turned = k in ("ffn_w_gate", "ffn_w_up")
        view = lambda a: jnp.swapaxes(a.reshape(shard3[k].shape), 1, 2) if turned else a.reshape(shard3[k].shape)
        outs, results = adamw(view(W[k]), view(M[k]), view(V[k]),
                              [slot[(k, i)] for i in range(shard3[k].shape[0])], rs=jobs)
        land(taken, results)
        res[k] = [(jnp.swapaxes(o, 1, 2) if turned else o).reshape(W[k].shape) for o in outs]
    pk = lambda d: _pack([d[k] for k, _ in SMALL])[None, None, :]
    outs, _ = adamw(pk(W), pk(M), pk(V), [slot[("small", 0)]])
    shapes = [W[k].shape for k, _ in SMALL]
    unp = [_unpack(o[0, 0], shapes) for o in outs]
    for i, (k, _) in enumerate(SMALL):
        res[k] = [u[i] for u in unp]

    order = ("norm_w", "ffn_w_gate", "ffn_w_up", "ffn_w_down", "mix_w_in", "dn_conv_w", "attn_sinks", "dn_a_log",
             "dn_dt_bias", "dn_norm_w", "mix_w_out", "conv_w_pw1", "conv_b_pw1", "conv_w_dw", "conv_b_dw",
             "conv_ln_w", "conv_ln_b", "conv_w_pw2", "conv_b_pw2", "final_norm_w")
    return (loss, grad_x, *[res[k][0] for k in order], *[res[k][1] for k in order],
            *[res[k][2] for k in order], *[res[k][3] for k in order])
```

```python
import functools

import numpy as np
import jax
import jax.numpy as jnp
from jax import lax
from jax.experimental import pallas as pl
from jax.experimental.pallas import tpu as pltpu

F32 = jnp.float32
BF16 = jnp.bfloat16
EPS = 1e-6
N_DEV = 8
N_CHIP = 4
V7X_VMEM_LIMIT = 60 * 2**20
MESH = pl.DeviceIdType.MESH
LANES = 128
SUBLANES = 8

DEPTH = 4
D_MODEL = 1024
ATTN_HEADS, ATTN_KV_HEADS, HEAD_DIM, ATTN_BLOCK = 8, 2, 64, 128
DN_HEADS, DN_D, DN_CHUNK, DN_CONV = 8, 64, 64, 4
CONV_WIDTH = 31
Q_A, KV_A, QKV_B, V_B = 512, 128, 1536, 512
IN_COLS = 2832
IN_COLS_PAD = 3072
OFF_QKVB = Q_A + 2 * KV_A
OFF_Z = OFF_QKVB + QKV_B
OFF_BETA = OFF_Z + V_B
OFF_A = OFF_BETA + DN_HEADS

FWD_CARRY_BYTES = {"F": 12 * 2**20, "A": 6 * 2**20, "E": 18 * 2**20, "O": 12 * 2**20}
BWD_CARRY_BYTES = {"F": 11 * 2**20, "A": 6 * 2**20, "E": 13 * 2**20, "O": 10 * 2**20}

ADAM_SLOT_BLOCK = 3 * 2**19

ADAM_LR, ADAM_B1, ADAM_B2, ADAM_EPS, ADAM_WD, ADAM_STEP = 0.001, 0.9, 0.999, 1e-08, 0.01, 10


def _cparams(sem):
    return pltpu.CompilerParams(dimension_semantics=sem, vmem_limit_bytes=V7X_VMEM_LIMIT)


def _sigmoid(x):
    return 1.0 / (1.0 + jnp.exp(-x))


def _softplus(x):
    return jnp.maximum(x, 0.0) + jnp.log(1.0 + jnp.exp(-jnp.abs(x)))


def _dot(a, b):
    return jnp.dot(a, b, preferred_element_type=F32)


def _dot_nt(a, b):
    return lax.dot_general(a, b, (((1,), (1,)), ((), ())), preferred_element_type=F32)


def _dot_tn(a, b):
    return lax.dot_general(a, b, (((0,), (0,)), ((), ())), preferred_element_type=F32)


def _rms(x, w):
    return x * lax.rsqrt(jnp.mean(x * x, axis=-1, keepdims=True) + EPS) * w


def _rms_bwd(x, w, dxn):
    r = lax.rsqrt(jnp.mean(x * x, axis=-1, keepdims=True) + EPS)
    xh = x * r
    dxh = dxn * w
    dx = r * (dxh - xh * jnp.mean(dxh * xh, axis=-1, keepdims=True))
    return dx, jnp.sum(dxn * xh, axis=0, keepdims=True)


def _position():
    return lax.axis_index("x"), lax.axis_index("y"), lax.axis_index("c")


def _dev_index(px, py, pc):
    return 4 * px + 2 * py + pc


def _rcopy(src, dst, send_sem, recv_sem, to):
    return pltpu.make_async_remote_copy(src_ref=src, dst_ref=dst, send_sem=send_sem, recv_sem=recv_sem,
                                        device_id=to, device_id_type=MESH)


def _ag_start(srcs, outs, send, recv, local):
    x, y, c = _position()
    me = _dev_index(x, y, c)
    chips = [(1 - x, y), (x, 1 - y), (1 - x, 1 - y)]
    for a, (src, out) in enumerate(zip(srcs, outs)):
        pltpu.make_async_copy(src, out.at[me], local.at[a]).start()
        _rcopy(src, out.at[me], send.at[a, 0], recv.at[a, 0], (x, y, 1 - c)).start()
        for j, chip in enumerate(chips):
            _rcopy(src, out.at[me], send.at[a, 1 + j], recv.at[a, 1 + j], (*chip, c)).start()


def _ag_finish(srcs, outs, send, recv, local):
    x, y, c = _position()
    me = _dev_index(x, y, c)
    sibling = (x, y, 1 - c)
    chips = [(1 - x, y), (x, 1 - y), (1 - x, 1 - y)]
    for j, chip in enumerate(chips):
        for a, out in enumerate(outs):
            blk = out.at[_dev_index(*chip, c)]
            _rcopy(blk, blk, send.at[a, 1 + j], recv.at[a, 1 + j], (x, y, c)).wait_recv()
            _rcopy(blk, blk, send.at[a, 4 + j], recv.at[a, 4 + j], sibling).start()
    for a, (src, out) in enumerate(zip(srcs, outs)):
        blk = out.at[_dev_index(x, y, 1 - c)]
        _rcopy(blk, blk, send.at[a, 0], recv.at[a, 0], (x, y, c)).wait_recv()
        for j, chip in enumerate(chips):
            blk = out.at[_dev_index(*chip, 1 - c)]
            _rcopy(blk, blk, send.at[a, 4 + j], recv.at[a, 4 + j], (x, y, c)).wait_recv()
        for k in range(N_DEV - 1):
            _rcopy(out.at[me], out.at[me], send.at[a, k], recv.at[a, k], (x, y, c)).wait_send()
        pltpu.make_async_copy(src, out.at[me], local.at[a]).wait()


def _rs_peer(r):
    x, y, c = _position()
    return x ^ ((r >> 2) & 1), y ^ ((r >> 1) & 1), c ^ (r & 1)


def _rs_start(ins, outs, send, recv, local):
    me = _dev_index(*_position())
    for a, (src, out) in enumerate(zip(ins, outs)):
        pltpu.make_async_copy(src.at[me], out.at[me], local.at[a]).start()
        for r in range(1, N_DEV):
            p = _rs_peer(r)
            _rcopy(src.at[_dev_index(*p)], out.at[me], send.at[a, r - 1], recv.at[a, r - 1], p).start()


def _rs_finish(ins, outs, send, recv, local):
    pos = _position()
    me = _dev_index(*pos)
    for a, (src, out) in enumerate(zip(ins, outs)):
        for r in range(1, N_DEV):
            blk = out.at[_dev_index(*_rs_peer(r))]
            _rcopy(blk, blk, send.at[a, r - 1], recv.at[a, r - 1], pos).wait_recv()
        for r in range(1, N_DEV):
            _rcopy(src.at[me], out.at[me], send.at[a, r - 1], recv.at[a, r - 1], pos).wait_send()
        pltpu.make_async_copy(src.at[me], out.at[me], local.at[a]).wait()


def _sw_start(ins, outs, send, recv):
    x, y, c = _position()
    for a, (src, out) in enumerate(zip(ins, outs)):
        for q in range(N_CHIP):
            _rcopy(src.at[2 * q + (1 - c)], out.at[q], send.at[a, q], recv.at[a, q], (x, y, 1 - c)).start()


def _sw_finish(ins, outs, send, recv):
    pos = _position()
    for a, out in enumerate(outs):
        for q in range(N_CHIP):
            _rcopy(out.at[q], out.at[q], send.at[a, q], recv.at[a, q], pos).wait_recv()
        for q in range(N_CHIP):
            _rcopy(out.at[q], out.at[q], send.at[a, q], recv.at[a, q], pos).wait_send()


def _r4_peer(r):
    x, y, c = _position()
    return x ^ ((r >> 1) & 1), y ^ (r & 1), c


def _r4_start(ins, outs, send, recv, local):
    x, y, c = _position()
    mine = 2 * x + y
    for a, (src, out) in enumerate(zip(ins, outs)):
        pltpu.make_async_copy(src.at[mine], out.at[mine], local.at[a]).start()
        for r in range(1, N_CHIP):
            px, py, pc = _r4_peer(r)
            _rcopy(src.at[2 * px + py], out.at[mine], send.at[a, r - 1], recv.at[a, r - 1], (px, py, pc)).start()


def _r4_finish(ins, outs, send, recv, local):
    x, y, c = _position()
    mine = 2 * x + y
    for a, (src, out) in enumerate(zip(ins, outs)):
        for r in range(1, N_CHIP):
            px, py, _ = _r4_peer(r)
            blk = out.at[2 * px + py]
            _rcopy(blk, blk, send.at[a, r - 1], recv.at[a, r - 1], (x, y, c)).wait_recv()
        for r in range(1, N_CHIP):
            _rcopy(src.at[mine], out.at[mine], send.at[a, r - 1], recv.at[a, r - 1], (x, y, c)).wait_send()
        pltpu.make_async_copy(src.at[mine], out.at[mine], local.at[a]).wait()


_RS_KINDS = {
    "direct": (_rs_start, _rs_finish, lambda n: [(n, N_DEV - 1), (n, N_DEV - 1), (n,)], lambda s: s),
    "swap": (_sw_start, _sw_finish, lambda n: [(n, N_CHIP), (n, N_CHIP)], lambda s: (N_CHIP,) + s[1:]),
    "chips": (_r4_start, _r4_finish, lambda n: [(n, N_CHIP - 1), (n, N_CHIP - 1), (n,)], lambda s: s),
}


def _pcall(body, args, *, name, grid, in_specs, out_specs, out_shape, sem, scratch_shapes=(), ag=(), rs=()):
    na, nr = len(ag), len(rs)
    if na + nr == 0:
        outs = pl.pallas_call(body, name=name, grid=grid, in_specs=in_specs, out_specs=out_specs,
                              out_shape=out_shape, scratch_shapes=list(scratch_shapes),
                              compiler_params=_cparams(sem))(*args)
        return list(outs), [], []
    n_in, n_out, n_scr = len(in_specs), len(out_specs), len(scratch_shapes)
    ag_idx = [i for _, i in ag]
    groups = [(k, [i for i, (kk, _) in enumerate(rs) if kk == k]) for k in _RS_KINDS]
    groups = [(k, idx) for k, idx in groups if idx]
    sem_counts = ([3] if na else []) + [len(_RS_KINDS[k][2](1)) for k, _ in groups]

    def wrapped(*refs):
        cin, refs = refs[:n_in], refs[n_in:]
        ag_in, refs = refs[:na], refs[na:]
        rs_in, refs = refs[:nr], refs[nr:]
        cout, refs = refs[:n_out], refs[n_out:]
        ag_out, refs = refs[:na], refs[na:]
        rs_out, refs = refs[:nr], refs[nr:]
        cscr, sems = refs[:n_scr], list(refs[n_scr:])
        sem_sets = [[sems.pop(0) for _ in range(n)] for n in sem_counts]
        ag_sems = sem_sets.pop(0) if na else None
        ag_src = [r if i is None else r.at[i] for r, i in zip(ag_in, ag_idx)]
        ids = [pl.program_id(d) for d in range(len(grid))]
        first = functools.reduce(jnp.logical_and, [i == 0 for i in ids])
        last = functools.reduce(jnp.logical_and, [i == g - 1 for i, g in zip(ids, grid)])

        def run(phase):
            if na:
                (_ag_start, _ag_finish)[phase](ag_src, ag_out, *ag_sems)
            for (k, idx), ss in zip(groups, sem_sets):
                _RS_KINDS[k][phase]([rs_in[i] for i in idx], [rs_out[i] for i in idx], *ss)

        @pl.when(first)
        def _():
            run(0)

        body(*cin, *cout, *cscr)

        @pl.when(last)
        def _():
            run(1)

    hbm = pl.BlockSpec(memory_space=pl.ANY)
    sem_shapes = [pltpu.SemaphoreType.DMA(s) for s in ([(na, N_DEV - 1), (na, N_DEV - 1), (na,)] if na else [])]
    for k, idx in groups:
        sem_shapes += [pltpu.SemaphoreType.DMA(s) for s in _RS_KINDS[k][2](len(idx))]
    outs = pl.pallas_call(
        wrapped, name=name, grid=grid,
        in_specs=list(in_specs) + [hbm] * (na + nr),
        out_specs=list(out_specs) + [hbm] * (na + nr),
        out_shape=list(out_shape)
        + [jax.ShapeDtypeStruct((N_DEV,) + a.shape[-2:], a.dtype) for a, _ in ag]
        + [jax.ShapeDtypeStruct(_RS_KINDS[k][3](b.shape), b.dtype) for k, b in rs],
        scratch_shapes=list(scratch_shapes) + sem_shapes,
        compiler_params=_cparams(sem),
    )(*args, *[a for a, _ in ag], *[b for _, b in rs])
    return list(outs[:n_out]), list(outs[n_out:n_out + na]), list(outs[n_out + na:])


def exchange(ag):
    def body(o_ref):
        o_ref[...] = jnp.zeros_like(o_ref)

    _, gathered, _ = _pcall(body, (), name="exchange", grid=(1,), in_specs=[],
                            out_specs=[pl.BlockSpec((8, LANES), lambda i: (0, 0))],
                            out_shape=[jax.ShapeDtypeStruct((8, LANES), F32)], sem=("arbitrary",), ag=ag)
    return gathered


def pair_add(blocks, received):
    n = len(blocks)

    def body(*refs):
        c = lax.axis_index("c")
        for g_ref, p_ref, o_ref in zip(refs[:n], refs[n:2 * n], refs[2 * n:]):
            mine = jnp.where(c == 0, g_ref[0, 0], g_ref[0, 1])
            o_ref[0] = (mine.astype(F32) + p_ref[0].astype(F32)).astype(BF16)

    halves = 2
    g_specs = [pl.BlockSpec((1, 2, b.shape[1] // halves, b.shape[2]), lambda q, r: (q, 0, r, 0)) for b in blocks]
    p_specs = [pl.BlockSpec((1, b.shape[1] // halves, b.shape[2]), lambda q, r: (q, r, 0)) for b in blocks]
    return pl.pallas_call(
        body, name="pair_add", grid=(N_CHIP, halves),
        in_specs=g_specs + p_specs, out_specs=p_specs,
        out_shape=[jax.ShapeDtypeStruct(p.shape, BF16) for p in received],
        compiler_params=_cparams(("parallel", "parallel")),
    )(*[b.reshape((N_CHIP, 2) + b.shape[1:]) for b in blocks], *received)


FFN_PAIR = 2


def _pair_cols(w_ref):
    return jnp.concatenate([w_ref[p] for p in range(FFN_PAIR)], axis=1)


def ffn_fwd(x, nw, wg, wu, wd, ag=()):
    T, D = x.shape
    F = wg.shape[2]
    P = FFN_PAIR
    J = wg.shape[0] // P
    tm = min(T, 1024)

    def body(x_ref, nw_ref, wg_ref, wu_ref, wd_ref, o_ref, xn_ref, acc_ref):
        j = pl.program_id(1)

        @pl.when(j == 0)
        def _():
            xn_ref[...] = _rms(x_ref[...], nw_ref[...]).astype(BF16)
            acc_ref[...] = jnp.zeros_like(acc_ref)

        xn = xn_ref[...]
        g = _dot(xn, _pair_cols(wg_ref))
        u = _dot(xn, _pair_cols(wu_ref))
        h = (g * _sigmoid(g) * u).astype(BF16)
        acc_ref[...] += _dot(h, wd_ref[...].reshape(P * F, D))

        @pl.when(j == J - 1)
        def _():
            o_ref[...] = x_ref[...] + 0.5 * acc_ref[...]

    (out,), gathered, _ = _pcall(
        body, (x, nw, wg, wu, wd), name="ffn_fwd", grid=(T // tm, J),
        in_specs=[pl.BlockSpec((tm, D), lambda t, j: (t, 0)),
                  pl.BlockSpec((1, D), lambda t, j: (0, 0)),
                  pl.BlockSpec((P, D, F), lambda t, j: (j, 0, 0)),
                  pl.BlockSpec((P, D, F), lambda t, j: (j, 0, 0)),
                  pl.BlockSpec((P, F, D), lambda t, j: (j, 0, 0))],
        out_specs=[pl.BlockSpec((tm, D), lambda t, j: (t, 0))],
        out_shape=[jax.ShapeDtypeStruct((T, D), F32)],
        scratch_shapes=[pltpu.VMEM((tm, D), BF16), pltpu.VMEM((tm, D), F32)],
        sem=("arbitrary", "arbitrary"), ag=ag)
    return out, gathered


def ffn_bwd(x, dy, nw, wg, wu, wd, rs=()):
    T, D = x.shape
    F = wg.shape[2]
    P = FFN_PAIR
    J = wg.shape[0] // P
    tm = min(T, 256)
    nt = T // tm

    def body(x_ref, dy_ref, nw_ref, wg_ref, wu_ref, wd_ref,
             dx_ref, dwg_ref, dwu_ref, dwd_ref, dnw_ref,
             xn_ref, dyh_ref, dxn_ref, awg_ref, awu_ref, awd_ref):
        j = pl.program_id(0)
        t = pl.program_id(1)
        rows = pl.ds(pl.multiple_of(t * tm, tm), tm)

        @pl.when(j == 0)
        def _():
            xn_ref[rows, :] = _rms(x_ref[...], nw_ref[...]).astype(BF16)
            dyh_ref[rows, :] = (0.5 * dy_ref[...]).astype(BF16)
            dxn_ref[rows, :] = jnp.zeros((tm, D), F32)

        @pl.when((j == 0) & (t == 0))
        def _():
            dnw_ref[...] = jnp.zeros_like(dnw_ref)

        @pl.when(t == 0)
        def _():
            awg_ref[...] = jnp.zeros_like(awg_ref)
            awu_ref[...] = jnp.zeros_like(awu_ref)
            awd_ref[...] = jnp.zeros_like(awd_ref)

        xn = xn_ref[rows, :]
        dyh = dyh_ref[rows, :]
        wg2, wu2 = _pair_cols(wg_ref), _pair_cols(wu_ref)
        g = _dot(xn, wg2)
        u = _dot(xn, wu2)
        sg = _sigmoid(g)
        s = g * sg
        h = (s * u).astype(BF16)
        dh = _dot_nt(dyh, wd_ref[...].reshape(P * F, D))
        du = (dh * s).astype(BF16)
        dg = (dh * u * (sg * (1.0 + g * (1.0 - sg)))).astype(BF16)
        awd_ref[...] += _dot_tn(h, dyh)
        awg_ref[...] += _dot_tn(dg, xn)
        awu_ref[...] += _dot_tn(du, xn)
        dxn_ref[rows, :] += _dot_nt(dg, wg2) + _dot_nt(du, wu2)

        @pl.when(t == nt - 1)
        def _():
            dwg_ref[...] = awg_ref[...].astype(BF16).reshape(P, F, D)
            dwu_ref[...] = awu_ref[...].astype(BF16).reshape(P, F, D)
            dwd_ref[...] = awd_ref[...].astype(BF16).reshape(P, F, D)

        @pl.when(j == J - 1)
        def _():
            dx, dnw = _rms_bwd(x_ref[...], nw_ref[...], dxn_ref[rows, :])
            dx_ref[...] = dy_ref[...] + dx
            dnw_ref[...] += dnw

    ends = lambda j, t: (jnp.where((j == 0) | (j == J - 1), t, 0), 0)
    last = lambda j, t: (jnp.where(j == J - 1, t, 0), 0)
    outs, _, slots = _pcall(
        body, (x, dy, nw, wg, wu, wd), name="ffn_bwd", grid=(J, nt),
        in_specs=[pl.BlockSpec((tm, D), ends), pl.BlockSpec((tm, D), ends),
                  pl.BlockSpec((1, D), lambda j, t: (0, 0)),
                  pl.BlockSpec((P, D, F), lambda j, t: (j, 0, 0)),
                  pl.BlockSpec((P, D, F), lambda j, t: (j, 0, 0)),
                  pl.BlockSpec((P, F, D), lambda j, t: (j, 0, 0))],
        out_specs=[pl.BlockSpec((tm, D), last),
                   pl.BlockSpec((P, F, D), lambda j, t: (j, 0, 0)),
                   pl.BlockSpec((P, F, D), lambda j, t: (j, 0, 0)),
                   pl.BlockSpec((P, F, D), lambda j, t: (j, 0, 0)),
                   pl.BlockSpec((1, D), lambda j, t: (0, 0))],
        out_shape=[jax.ShapeDtypeStruct((T, D), F32)] + [jax.ShapeDtypeStruct((P * J, F, D), BF16)] * 3
        + [jax.ShapeDtypeStruct((1, D), F32)],
        scratch_shapes=[pltpu.VMEM((T, D), BF16), pltpu.VMEM((T, D), BF16), pltpu.VMEM((T, D), F32)]
        + [pltpu.VMEM((P * F, D), F32)] * 3,
        sem=("arbitrary", "arbitrary"), rs=rs)
    return outs, slots


def rmslin_fwd(x, nw, w, b):
    T, D = x.shape
    N = w.shape[1]
    tm = min(T, 256)

    def body(x_ref, nw_ref, w_ref, b_ref, o_ref):
        xn = _rms(x_ref[...], nw_ref[...]).astype(BF16)
        o_ref[...] = _dot(xn, w_ref[...]) + b_ref[...]

    return pl.pallas_call(
        body, name="rmslin_fwd", grid=(T // tm,),
        in_specs=[pl.BlockSpec((tm, D), lambda t: (t, 0)), pl.BlockSpec((1, D), lambda t: (0, 0)),
                  pl.BlockSpec((D, N), lambda t: (0, 0)), pl.BlockSpec((1, N), lambda t: (0, 0))],
        out_specs=pl.BlockSpec((tm, N), lambda t: (t, 0)),
        out_shape=jax.ShapeDtypeStruct((T, N), F32),
        compiler_params=_cparams(("parallel",)),
    )(x, nw, w, b)


def rmslin_bwd(x, dres, dproj, nw, w):
    T, D = x.shape
    N = w.shape[1]
    pieces = list(dproj) if isinstance(dproj, (list, tuple)) else [dproj]
    nb = DN_ZCOLS if len(pieces) > 1 else 1024
    nc = N // nb
    tm = min(T, 256)
    nt = T // tm
    n_p = len(pieces)

    def body(x_ref, dres_ref, *refs):
        p_refs, (nw_ref, w_ref, dx_ref, dw_ref, db_ref, dnw_ref, xn_ref, dxn_ref, acc_ref) = refs[:n_p], refs[n_p:]
        c = pl.program_id(0)
        t = pl.program_id(1)
        rows = pl.ds(pl.multiple_of(t * tm, tm), tm)

        @pl.when(c == 0)
        def _():
            xn_ref[rows, :] = _rms(x_ref[...], nw_ref[...]).astype(BF16)
            dxn_ref[rows, :] = jnp.zeros((tm, D), F32)

        @pl.when((c == 0) & (t == 0))
        def _():
            dnw_ref[...] = jnp.zeros_like(dnw_ref)

        @pl.when(t == 0)
        def _():
            acc_ref[...] = jnp.zeros_like(acc_ref)
            db_ref[...] = jnp.zeros_like(db_ref)

        if n_p == 1:
            dpf = p_refs[0][...]
        else:
            dq_ref, dkv_ref, dqkv_ref, dz_ref = p_refs
            dpf = jnp.where(c == 0, jnp.concatenate([dq_ref[...], dkv_ref[...]], axis=1),
                            jnp.where(c == nc - 1, dz_ref[...], dqkv_ref[...]))
        dp = dpf.astype(BF16)
        acc_ref[...] += _dot_tn(xn_ref[rows, :], dp)
        db_ref[...] += jnp.sum(dpf, axis=0, keepdims=True)
        dxn_ref[rows, :] += _dot_nt(dp, w_ref[...])

        @pl.when(t == nt - 1)
        def _():
            dw_ref[...] = acc_ref[...].astype(BF16)

        @pl.when(c == nc - 1)
        def _():
            dx, dnw = _rms_bwd(x_ref[...], nw_ref[...], dxn_ref[rows, :])
            dx_ref[...] = dres_ref[...] + dx
            dnw_ref[...] += dnw

    ends = lambda c, t: (jnp.where((c == 0) | (c == nc - 1), t, 0), 0)
    last = lambda c, t: (jnp.where(c == nc - 1, t, 0), 0)
    first = lambda c, t: (jnp.where(c == 0, t, 0), 0)
    if n_p == 1:
        p_specs = [pl.BlockSpec((tm, nb), lambda c, t: (t, c))]
    else:
        p_specs = [pl.BlockSpec((tm, Q_A), first), pl.BlockSpec((tm, 2 * KV_A), first),
                   pl.BlockSpec((tm, nb), lambda c, t: (jnp.where((c > 0) & (c < nc - 1), t, 0),
                                                        jnp.clip(c - 1, 0, 1))),
                   pl.BlockSpec((tm, nb), last)]
    return pl.pallas_call(
        body, name="rmslin_bwd", grid=(nc, nt),
        in_specs=[pl.BlockSpec((tm, D), ends), pl.BlockSpec((tm, D), last)] + p_specs
        + [pl.BlockSpec((1, D), lambda c, t: (0, 0)), pl.BlockSpec((D, nb), lambda c, t: (0, c))],
        out_specs=[pl.BlockSpec((tm, D), last),
                   pl.BlockSpec((D, nb), lambda c, t: (0, c)),
                   pl.BlockSpec((1, nb), lambda c, t: (0, c)),
                   pl.BlockSpec((1, D), lambda c, t: (0, 0))],
        out_shape=[jax.ShapeDtypeStruct((T, D), F32), jax.ShapeDtypeStruct((D, N), BF16),
                   jax.ShapeDtypeStruct((1, N), F32), jax.ShapeDtypeStruct((1, D), F32)],
        scratch_shapes=[pltpu.VMEM((T, D), BF16), pltpu.VMEM((T, D), F32), pltpu.VMEM((D, nb), F32)],
        compiler_params=_cparams(("arbitrary", "arbitrary")),
    )(x, dres, *pieces, nw, w)


def lin_fwd(res, parts, w, b):
    T = res.shape[0]
    K, N = w.shape
    tm = min(T, 512)
    n = len(parts)
    offs = [sum(p.shape[1] for p in parts[:i]) for i in range(n + 1)]

    def body(res_ref, *refs):
        a_refs, (w_ref, b_ref, o_ref) = refs[:n], refs[n:]
        acc = res_ref[...] + b_ref[...]
        for i, a_ref in enumerate(a_refs):
            acc = acc + _dot(a_ref[...].astype(BF16), w_ref[offs[i]:offs[i + 1], :])
        o_ref[...] = acc

    return pl.pallas_call(
        body, name="lin_fwd", grid=(T // tm,),
        in_specs=[pl.BlockSpec((tm, N), lambda t: (t, 0))]
        + [pl.BlockSpec((tm, p.shape[1]), lambda t: (t, 0)) for p in parts]
        + [pl.BlockSpec((K, N), lambda t: (0, 0)), pl.BlockSpec((1, N), lambda t: (0, 0))],
        out_specs=pl.BlockSpec((tm, N), lambda t: (t, 0)),
        out_shape=jax.ShapeDtypeStruct((T, N), F32),
        compiler_params=_cparams(("parallel",)),
    )(res, *parts, w, b)


def lin_bwd(parts, dy, w):
    T = dy.shape[0]
    K, N = w.shape
    tm = min(T, 256)
    n = len(parts)
    offs = [sum(p.shape[1] for p in parts[:i]) for i in range(n + 1)]

    def body(*refs):
        a_refs, (dy_ref, w_ref, da_ref, dw_ref, db_ref) = refs[:n], refs[n:]

        @pl.when(pl.program_id(0) == 0)
        def _():
            dw_ref[...] = jnp.zeros_like(dw_ref)
            db_ref[...] = jnp.zeros_like(db_ref)

        dyf = dy_ref[...]
        dyb = dyf.astype(BF16)
        da_ref[...] = _dot_nt(dyb, w_ref[...])
        for i, a_ref in enumerate(a_refs):
            dw_ref[offs[i]:offs[i + 1], :] += _dot_tn(a_ref[...].astype(BF16), dyb)
        db_ref[...] += jnp.sum(dyf, axis=0, keepdims=True)

    return pl.pallas_call(
        body, name="lin_bwd", grid=(T // tm,),
        in_specs=[pl.BlockSpec((tm, p.shape[1]), lambda t: (t, 0)) for p in parts]
        + [pl.BlockSpec((tm, N), lambda t: (t, 0)), pl.BlockSpec((K, N), lambda t: (0, 0))],
        out_specs=[pl.BlockSpec((tm, K), lambda t: (t, 0)), pl.BlockSpec((K, N), lambda t: (0, 0)),
                   pl.BlockSpec((1, N), lambda t: (0, 0))],
        out_shape=[jax.ShapeDtypeStruct((T, K), F32), jax.ShapeDtypeStruct((K, N), F32),
                   jax.ShapeDtypeStruct((1, N), F32)],
        compiler_params=_cparams(("arbitrary",)),
    )(*parts, dy, w)


def loss_fwd_bwd(x, fw, target):
    T, D = x.shape
    tm = min(T, 256)

    def body(x_ref, fw_ref, tg_ref, loss_ref, dx_ref, dfw_ref):
        @pl.when(pl.program_id(0) == 0)
        def _():
            loss_ref[...] = jnp.zeros_like(loss_ref)
            dfw_ref[...] = jnp.zeros_like(dfw_ref)

        xv = x_ref[...]
        w = fw_ref[...]
        err = _rms(xv, w) - tg_ref[...]
        row = jnp.sum(err * err, axis=-1, keepdims=True)
        loss_ref[...] += (0.5 / D) * jnp.sum(row, axis=0, keepdims=True)
        dx, dfw = _rms_bwd(xv, w, err * (1.0 / D))
        dx_ref[...] = dx
        dfw_ref[...] += dfw

    return pl.pallas_call(
        body, name="loss_fwd_bwd", grid=(T // tm,),
        in_specs=[pl.BlockSpec((tm, D), lambda t: (t, 0)), pl.BlockSpec((1, D), lambda t: (0, 0)),
                  pl.BlockSpec((tm, D), lambda t: (t, 0))],
        out_specs=[pl.BlockSpec((1, 1), lambda t: (0, 0)), pl.BlockSpec((tm, D), lambda t: (t, 0)),
                   pl.BlockSpec((1, D), lambda t: (0, 0))],
        out_shape=[jax.ShapeDtypeStruct((1, 1), F32), jax.ShapeDtypeStruct((T, D), F32),
                   jax.ShapeDtypeStruct((1, D), F32)],
        compiler_params=_cparams(("arbitrary",)),
    )(x, fw, target)


def _attn_masks(n, rows, blk):
    r = lax.broadcasted_iota(jnp.int32, (rows, 2 * blk), 0)
    jj = lax.broadcasted_iota(jnp.int32, (rows, 2 * blk), 1)
    dist = (r % blk) + blk - jj
    valid = (dist >= 0) & (dist < blk) & ((n > 0) | (jj >= blk))
    return dist.astype(F32), valid


def _attn_block(q, kcat, vcat, sink, slope, dist, valid):
    d = q.shape[-1]
    s = _dot_nt(q.astype(BF16), kcat.astype(BF16)) * (d ** -0.5)
    s = jnp.where(valid, s - slope * dist, -1e30)
    m = lax.stop_gradient(jnp.maximum(jnp.max(s, axis=-1, keepdims=True), sink))
    e = jnp.exp(s - m)
    p = e / (jnp.sum(e, axis=-1, keepdims=True) + jnp.exp(sink - m))
    return _dot(p.astype(BF16), vcat.astype(BF16))


ATTN_G = ATTN_HEADS // ATTN_KV_HEADS
ATTN_QW = ATTN_G * HEAD_DIM
ATTN_KCOL = Q_A // KV_A


def _attn_specs():
    blk = ATTN_BLOCK
    qs = pl.BlockSpec((blk, ATTN_QW), lambda h, n: (n, h))
    prev = lambda c: pl.BlockSpec((blk, KV_A), lambda h, n: (jnp.maximum(n - 1, 0), c))
    cur = lambda c: pl.BlockSpec((blk, KV_A), lambda h, n: (n, c))
    rowp = pl.BlockSpec((ATTN_G * blk, 1), lambda h, n: (h, 0))
    return qs, [prev(ATTN_KCOL), cur(ATTN_KCOL), prev(ATTN_KCOL + 1), cur(ATTN_KCOL + 1)], rowp


def _attn_operands(h, q_ref, kp_ref, kc_ref, vp_ref, vc_ref):
    d = HEAD_DIM
    q = jnp.concatenate([q_ref[:, g * d:(g + 1) * d] for g in range(ATTN_G)], axis=0)
    pick = lambda r: jnp.where(h == 0, r[:, :d], r[:, d:])
    kcat = jnp.concatenate([pick(kp_ref[...]), pick(kc_ref[...])], axis=0)
    vcat = jnp.concatenate([pick(vp_ref[...]), pick(vc_ref[...])], axis=0)
    return q, kcat, vcat


def attn_fwd(proj, sink_rows, slope_rows, ag=()):
    T = proj.shape[0]
    blk, d = ATTN_BLOCK, HEAD_DIM

    def body(q_ref, kp_ref, kc_ref, vp_ref, vc_ref, sink_ref, slope_ref, o_ref):
        h, n = pl.program_id(0), pl.program_id(1)
        dist, valid = _attn_masks(n, ATTN_G * blk, blk)
        q, kcat, vcat = _attn_operands(h, q_ref, kp_ref, kc_ref, vp_ref, vc_ref)
        o = _attn_block(q, kcat, vcat, sink_ref[...], slope_ref[...], dist, valid)
        for g in range(ATTN_G):
            o_ref[:, g * d:(g + 1) * d] = o[g * blk:(g + 1) * blk]

    qs, kv, rowp = _attn_specs()
    (out,), gathered, _ = _pcall(
        body, (proj, proj, proj, proj, proj, sink_rows, slope_rows), name="attn_fwd",
        grid=(ATTN_KV_HEADS, T // blk), in_specs=[qs] + kv + [rowp, rowp], out_specs=[qs],
        out_shape=[jax.ShapeDtypeStruct((T, Q_A), F32)], sem=("arbitrary", "arbitrary"), ag=ag)
    return out, gathered


def attn_bwd(proj, sink_rows, slope_rows, dmix, rs=()):
    T = proj.shape[0]
    blk, d = ATTN_BLOCK, HEAD_DIM

    def body(q_ref, kp_ref, kc_ref, vp_ref, vc_ref, sink_ref, slope_ref, do_ref, dq_ref, dkv_ref, dsink_ref):
        h, n = pl.program_id(0), pl.program_id(1)

        @pl.when((h == 0) & (n == 0))
        def _():
            dkv_ref[...] = jnp.zeros_like(dkv_ref)

        @pl.when(n == 0)
        def _():
            dsink_ref[...] = jnp.zeros_like(dsink_ref)

        dist, valid = _attn_masks(n, ATTN_G * blk, blk)
        q, kcat, vcat = _attn_operands(h, q_ref, kp_ref, kc_ref, vp_ref, vc_ref)
        do = jnp.concatenate([do_ref[:, g * d:(g + 1) * d] for g in range(ATTN_G)], axis=0)
        fn = functools.partial(_attn_block, slope=slope_ref[...], dist=dist, valid=valid)
        _, vjp = jax.vjp(fn, q, kcat, vcat, sink_ref[...])
        dq, dkcat, dvcat, dsink = vjp(do)
        for g in range(ATTN_G):
            dq_ref[:, g * d:(g + 1) * d] = dq[g * blk:(g + 1) * blk]
        dsink_ref[...] += dsink
        lane = lax.broadcasted_iota(jnp.int32, (2 * blk, 2 * KV_A), 1)
        mine = (lane % KV_A) // d == h
        both = jnp.where(mine, jnp.concatenate([dkcat, dkcat, dvcat, dvcat], axis=1), 0.0)

        @pl.when(n == 0)
        def _():
            dkv_ref[0:blk, :] += both[blk:]

        @pl.when(n > 0)
        def _():
            rows = pl.ds(pl.multiple_of((n - 1) * blk, blk), 2 * blk)
            dkv_ref[rows, :] += both

    qs, kv, rowp = _attn_specs()
    outs, _, slots = _pcall(
        body, (proj, proj, proj, proj, proj, sink_rows, slope_rows, dmix), name="attn_bwd",
        grid=(ATTN_KV_HEADS, T // blk), in_specs=[qs] + kv + [rowp, rowp, qs],
        out_specs=[qs, pl.BlockSpec((T, 2 * KV_A), lambda h, n: (0, 0)), rowp],
        out_shape=[jax.ShapeDtypeStruct((T, Q_A), F32), jax.ShapeDtypeStruct((T, 2 * KV_A), F32),
                   jax.ShapeDtypeStruct((ATTN_HEADS * blk, 1), F32)],
        sem=("arbitrary", "arbitrary"), rs=rs)
    return outs, slots


_NN = (((2,), (1,)), ((0,), (0,)))
_NT = (((2,), (2,)), ((0,), (0,)))
_TN = (((1,), (1,)), ((0,), (0,)))


def _bmm(a, b, dims):
    return lax.dot_general(a.astype(BF16), b.astype(BF16), dims, preferred_element_type=F32)


def _split(x, terms):
    out = []
    for _ in range(terms):
        t = x.astype(BF16)
        out.append(t)
        x = x - t.astype(F32)
    return out


def _fine_product(a, b, dims):
    (ah, al), (bh, bl) = _split(a, 2), _split(b, 2)
    dot = lambda x, y: lax.dot_general(x, y, dims, preferred_element_type=F32)
    return dot(ah, bh) + (dot(ah, bl) + dot(al, bh))


def _mask_product(mask, x, dims):
    mb = mask.astype(BF16)
    parts = [lax.dot_general(mb, t, dims, preferred_element_type=F32) for t in _split(x, 3)]
    return parts[0] + (parts[1] + parts[2])


@jax.custom_vjp
def _fine_nt(a, b):
    return _fine_product(a, b, _NT)


_fine_nt.defvjp(lambda a, b: (_fine_product(a, b, _NT), (a, b)),
                lambda res, ct: (_fine_product(ct, res[1], _NN), _fine_product(ct, res[0], _TN)))


@jax.custom_vjp
def _mask_nn(mask, x):
    return _mask_product(mask, x, _NN)


_mask_nn.defvjp(lambda mask, x: (_mask_product(mask, x, _NN), mask),
                lambda mask, ct: (jnp.zeros_like(mask), _mask_product(mask, ct, _TN)))


@jax.custom_vjp
def _unit_lower_inverse(low):
    n = low.shape[-1]
    eye = (lax.broadcasted_iota(jnp.int32, low.shape, 1) == lax.broadcasted_iota(jnp.int32, low.shape, 2)).astype(F32)
    tinv = eye - low
    p = low
    for _ in range(n.bit_length() - 2):
        p = _bmm(p, p, _NN)
        tinv = tinv + _bmm(tinv, p, _NN)
    return tinv


def _unit_lower_inverse_fwd(low):
    tinv = _unit_lower_inverse(low)
    return tinv, tinv


_unit_lower_inverse.defvjp(_unit_lower_inverse_fwd, lambda tinv, ct: (-_bmm(_bmm(tinv, ct, _TN), tinv, _NT),))


def _dn_chunk(qc, kc, vc, zc, braw, araw, alog, dtb, nw, S):
    H, C, D = qc.shape
    row = lax.broadcasted_iota(jnp.int32, (H, C, C), 1)
    col = lax.broadcasted_iota(jnp.int32, (H, C, C), 2)
    causal = row >= col
    strict = row > col
    eye = (row == col).astype(F32)

    q = qc * lax.rsqrt(jnp.sum(qc * qc, axis=-1, keepdims=True) + EPS) * (D ** -0.5)
    k = kc * lax.rsqrt(jnp.sum(kc * kc, axis=-1, keepdims=True) + EPS)
    beta = _sigmoid(braw)
    g = -jnp.exp(alog) * _softplus(araw + dtb)
    a_col = _mask_nn(causal.astype(F32), jnp.broadcast_to(g, (H, C, C)))
    a_row = _mask_nn(jnp.ones((H, C, C), F32), eye * a_col)
    decay = jnp.where(causal, jnp.exp(jnp.where(causal, a_col - a_row, 0.0)), 0.0)
    kb = k * beta
    tinv = _unit_lower_inverse(jnp.where(strict, _fine_nt(kb, k) * decay, 0.0))
    e_col = jnp.exp(a_col)
    u = _bmm(tinv, vc * beta, _NN)
    w = _bmm(tinv, kb * e_col, _NN)
    attn = _fine_nt(q, k) * decay
    gl = a_col[:, C - 1:C, :]
    k_dec = k * jnp.exp(gl - a_col)
    v_new = u - _bmm(w, S, _NN)
    o = _bmm(q * e_col, S, _NN) + _bmm(attn, v_new, _NN)
    s_new = S * jnp.exp(jnp.broadcast_to(gl, (H, D, D))) + _bmm(k_dec, v_new, _TN)
    on = o * lax.rsqrt(jnp.mean(o * o, axis=-1, keepdims=True) + EPS) * nw
    return on * (zc * _sigmoid(zc)), s_new


DN_ZCOLS = IN_COLS_PAD - OFF_Z
DN_ZBLK = OFF_Z // DN_ZCOLS


def _dn_heads(a, off):
    return jnp.stack([a[:, off + h * DN_D:off + (h + 1) * DN_D] for h in range(DN_HEADS)])


def _dn_gate_cols(zb, off):
    return jnp.stack([zb[:, off + h:off + h + 1] for h in range(DN_HEADS)])


DN_STEP_CHUNKS = 4


def _dn_operands(x_ref, zb_ref, rows):
    x, zb = x_ref[rows, :], zb_ref[rows, :]
    return (_dn_heads(x, 0), _dn_heads(x, V_B), _dn_heads(x, 2 * V_B), _dn_heads(zb, 0),
            _dn_gate_cols(zb, V_B), _dn_gate_cols(zb, V_B + DN_HEADS))


def dn_fwd(qkvc, proj, alog, dtb, nw, ag=()):
    T = qkvc.shape[0]
    H, C, D, G = DN_HEADS, DN_CHUNK, DN_D, DN_STEP_CHUNKS
    N = T // C

    def body(x_ref, zb_ref, alog_ref, dtb_ref, nw_ref, o_ref, sall_ref, s_ref):
        @pl.when(pl.program_id(0) == 0)
        def _():
            s_ref[...] = jnp.zeros_like(s_ref)

        s = s_ref[...]
        for c in range(G):
            rows = slice(c * C, (c + 1) * C)
            sall_ref[c] = s
            on, s = _dn_chunk(*_dn_operands(x_ref, zb_ref, rows), alog_ref[...], dtb_ref[...], nw_ref[...], s)
            for h in range(H):
                o_ref[rows, h * D:(h + 1) * D] = on[h]
        s_ref[...] = s

    par = pl.BlockSpec((H, 1, 1), lambda n: (0, 0, 0))
    outs, gathered, _ = _pcall(
        body, (qkvc, proj, alog, dtb, nw), name="dn_fwd", grid=(N // G,),
        in_specs=[pl.BlockSpec((G * C, QKV_B), lambda n: (n, 0)),
                  pl.BlockSpec((G * C, DN_ZCOLS), lambda n: (n, DN_ZBLK)),
                  par, par, pl.BlockSpec((1, 1, D), lambda n: (0, 0, 0))],
        out_specs=[pl.BlockSpec((G * C, V_B), lambda n: (n, 0)), pl.BlockSpec((G, H, D, D), lambda n: (n, 0, 0, 0))],
        out_shape=[jax.ShapeDtypeStruct((T, V_B), F32), jax.ShapeDtypeStruct((N, H, D, D), F32)],
        scratch_shapes=[pltpu.VMEM((H, D, D), F32)], sem=("arbitrary",), ag=ag)
    return outs, gathered


def dn_bwd(qkvc, proj, alog, dtb, nw, sall, dmix, rs=()):
    T = qkvc.shape[0]
    H, C, D, G = DN_HEADS, DN_CHUNK, DN_D, DN_STEP_CHUNKS
    N = T // C // G

    def body(x_ref, zb_ref, alog_ref, dtb_ref, nw_ref, sall_ref, do_ref,
             dx_ref, dzb_ref, dalog_ref, ddtb_ref, dnw_ref, ds_ref):
        @pl.when(pl.program_id(0) == 0)
        def _():
            ds_ref[...] = jnp.zeros_like(ds_ref)
            dalog_ref[...] = jnp.zeros_like(dalog_ref)
            ddtb_ref[...] = jnp.zeros_like(ddtb_ref)
            dnw_ref[...] = jnp.zeros_like(dnw_ref)

        ds = ds_ref[...]
        lane = lax.broadcasted_iota(jnp.int32, (C, LANES), 1)
        for c in reversed(range(G)):
            rows = slice(c * C, (c + 1) * C)
            args = (*_dn_operands(x_ref, zb_ref, rows), alog_ref[...], dtb_ref[...], nw_ref[...], sall_ref[c])
            _, vjp = jax.vjp(_dn_chunk, *args)
            dq, dk, dv, dz, db, da, dalog, ddtb, dnw, ds = vjp((_dn_heads(do_ref[rows, :], 0), ds))
            for h in range(H):
                dx_ref[rows, h * D:(h + 1) * D] = dq[h]
                dx_ref[rows, V_B + h * D:V_B + (h + 1) * D] = dk[h]
                dx_ref[rows, 2 * V_B + h * D:2 * V_B + (h + 1) * D] = dv[h]
                dzb_ref[rows, h * D:(h + 1) * D] = dz[h]
            tail = jnp.zeros((C, LANES), F32)
            for h in range(H):
                tail = tail + jnp.where(lane == h, jnp.broadcast_to(db[h], (C, LANES)), 0.0)
                tail = tail + jnp.where(lane == H + h, jnp.broadcast_to(da[h], (C, LANES)), 0.0)
            dzb_ref[rows, V_B:V_B + LANES] = tail
            dzb_ref[rows, V_B + LANES:] = jnp.zeros((C, DN_ZCOLS - V_B - LANES), F32)
            dalog_ref[...] += dalog
            ddtb_ref[...] += ddtb
            dnw_ref[...] += dnw
        ds_ref[...] = ds

    par = pl.BlockSpec((H, 1, 1), lambda i: (0, 0, 0))
    nws = pl.BlockSpec((1, 1, D), lambda i: (0, 0, 0))
    outs, _, slots = _pcall(
        body, (qkvc, proj, alog, dtb, nw, sall, dmix), name="dn_bwd", grid=(N,),
        in_specs=[pl.BlockSpec((G * C, QKV_B), lambda i: (N - 1 - i, 0)),
                  pl.BlockSpec((G * C, DN_ZCOLS), lambda i: (N - 1 - i, DN_ZBLK)), par, par, nws,
                  pl.BlockSpec((G, H, D, D), lambda i: (N - 1 - i, 0, 0, 0)),
                  pl.BlockSpec((G * C, V_B), lambda i: (N - 1 - i, 1))],
        out_specs=[pl.BlockSpec((G * C, QKV_B), lambda i: (N - 1 - i, 0)),
                   pl.BlockSpec((G * C, DN_ZCOLS), lambda i: (N - 1 - i, 0)), par, par, nws],
        out_shape=[jax.ShapeDtypeStruct((T, QKV_B), F32), jax.ShapeDtypeStruct((T, DN_ZCOLS), F32)]
        + [jax.ShapeDtypeStruct((H, 1, 1), F32)] * 2 + [jax.ShapeDtypeStruct((1, 1, D), F32)],
        scratch_shapes=[pltpu.VMEM((H, D, D), F32)], sem=("arbitrary",), rs=rs)
    return outs, slots


def _conv_taps(buf_ref, w, width, halo, tm):
    acc = None
    for kk, win in _windows(buf_ref, [halo - (width - 1) + kk for kk in range(width)], tm):
        term = w[kk:kk + 1, :] * win
        acc = term if acc is None else acc + term
    return acc


def _windows(ref, offsets, tm):
    for res in range(SUBLANES):
        ks = [k for k, o in enumerate(offsets) if o % SUBLANES == res]
        if not ks:
            continue
        lo = min(offsets[k] for k in ks)
        hi = max(offsets[k] for k in ks)
        shifted = ref[pl.ds(lo, tm + hi - lo), :]
        for k in ks:
            yield k, shifted[offsets[k] - lo:offsets[k] - lo + tm]


def _conv_taps_bwd(dbuf_ref, w, width, tm):
    acc = None
    for kk, win in _windows(dbuf_ref, [width - 1 - kk for kk in range(width)], tm):
        term = w[kk:kk + 1, :] * win
        acc = term if acc is None else acc + term
    return acc


def _conv_dw_acc(dw_ref, dout, buf_ref, width, halo, tm):
    for kk, win in _windows(buf_ref, [halo - (width - 1) + kk for kk in range(width)], tm):
        dw_ref[pl.ds(kk, 1), :] += jnp.sum(dout * win, axis=0, keepdims=True)


DNC_HALO = 8
DNC_COLS = 768


def dnconv_fwd(proj, w):
    T = proj.shape[0]
    tm = min(T, 256)
    hb = tm // DNC_HALO

    def body(x_ref, h_ref, w_ref, o_ref, buf_ref):
        i = pl.program_id(0)
        buf_ref[0:DNC_HALO, :] = jnp.where(i > 0, h_ref[...], 0.0)
        buf_ref[DNC_HALO:, :] = x_ref[...]
        acc = _conv_taps(buf_ref, w_ref[...], DN_CONV, DNC_HALO, tm)
        o_ref[...] = acc * _sigmoid(acc)

    return pl.pallas_call(
        body, name="dnconv_fwd", grid=(T // tm, 2),
        in_specs=[pl.BlockSpec((tm, DNC_COLS), lambda i, c: (i, 1 + c)),
                  pl.BlockSpec((DNC_HALO, DNC_COLS), lambda i, c: (jnp.maximum(i * hb - 1, 0), 1 + c)),
                  pl.BlockSpec((DN_CONV, DNC_COLS), lambda i, c: (0, c))],
        out_specs=pl.BlockSpec((tm, DNC_COLS), lambda i, c: (i, c)),
        out_shape=jax.ShapeDtypeStruct((T, QKV_B), F32),
        scratch_shapes=[pltpu.VMEM((DNC_HALO + tm, DNC_COLS), F32)],
        compiler_params=_cparams(("parallel", "parallel")),
    )(proj, proj, w)


def dnconv_bwd(proj, w, dout):
    T = proj.shape[0]
    tm = min(T, 256)
    nt = T // tm
    hb = tm // DNC_HALO

    def body(x_ref, h_ref, w_ref, do_ref, dx_ref, dw_ref, buf_ref, dbuf_ref):
        r = pl.program_id(1)
        i = nt - 1 - r

        @pl.when(r == 0)
        def _():
            dw_ref[...] = jnp.zeros_like(dw_ref)
            dbuf_ref[tm:, :] = jnp.zeros((DNC_HALO, DNC_COLS), F32)

        buf_ref[0:DNC_HALO, :] = jnp.where(i > 0, h_ref[...], 0.0)
        buf_ref[DNC_HALO:, :] = x_ref[...]
        wv = w_ref[...]
        acc = _conv_taps(buf_ref, wv, DN_CONV, DNC_HALO, tm)
        sg = _sigmoid(acc)
        dacc = do_ref[...] * (sg * (1.0 + acc * (1.0 - sg)))
        dbuf_ref[0:tm, :] = dacc
        dx_ref[...] = _conv_taps_bwd(dbuf_ref, wv, DN_CONV, tm)
        _conv_dw_acc(dw_ref, dacc, buf_ref, DN_CONV, DNC_HALO, tm)
        dbuf_ref[tm:, :] = dacc[0:DNC_HALO, :]

    return pl.pallas_call(
        body, name="dnconv_bwd", grid=(2, nt),
        in_specs=[pl.BlockSpec((tm, DNC_COLS), lambda c, r: (nt - 1 - r, 1 + c)),
                  pl.BlockSpec((DNC_HALO, DNC_COLS), lambda c, r: (jnp.maximum((nt - 1 - r) * hb - 1, 0), 1 + c)),
                  pl.BlockSpec((DN_CONV, DNC_COLS), lambda c, r: (0, c)),
                  pl.BlockSpec((tm, DNC_COLS), lambda c, r: (nt - 1 - r, c))],
        out_specs=[pl.BlockSpec((tm, DNC_COLS), lambda c, r: (nt - 1 - r, c)),
                   pl.BlockSpec((DN_CONV, DNC_COLS), lambda c, r: (0, c))],
        out_shape=[jax.ShapeDtypeStruct((T, QKV_B), F32), jax.ShapeDtypeStruct((DN_CONV, QKV_B), F32)],
        scratch_shapes=[pltpu.VMEM((DNC_HALO + tm, DNC_COLS), F32), pltpu.VMEM((tm + DNC_HALO, DNC_COLS), F32)],
        compiler_params=_cparams(("parallel", "arbitrary")),
    )(proj, proj, w, dout)


CV_HALO = 32


def _cv_post(cv, lnw, lnb):
    mu = jnp.mean(cv, axis=-1, keepdims=True)
    xc = cv - mu
    y = xc * lax.rsqrt(jnp.mean(xc * xc, axis=-1, keepdims=True) + EPS) * lnw + lnb
    return y * _sigmoid(y)


def cv_fwd(ab, w, bdw, lnw, lnb, ag=()):
    T = ab.shape[0]
    D = ab.shape[1] // 2
    tm = min(T, 256)
    hb = tm // CV_HALO

    def body(a_ref, b_ref, ah_ref, bh_ref, w_ref, bdw_ref, lnw_ref, lnb_ref, o_ref, cv_ref, buf_ref):
        i = pl.program_id(0)
        buf_ref[0:CV_HALO, :] = jnp.where(i > 0, ah_ref[...] * _sigmoid(bh_ref[...]), 0.0)
        buf_ref[CV_HALO:, :] = a_ref[...] * _sigmoid(b_ref[...])
        cv = _conv_taps(buf_ref, w_ref[...], CONV_WIDTH, CV_HALO, tm) + bdw_ref[...]
        cv_ref[...] = cv
        o_ref[...] = _cv_post(cv, lnw_ref[...], lnb_ref[...])

    halo = lambda c: pl.BlockSpec((CV_HALO, D), lambda i: (jnp.maximum(i * hb - 1, 0), c))
    vec = pl.BlockSpec((1, D), lambda i: (0, 0))
    tile = pl.BlockSpec((tm, D), lambda i: (i, 0))
    outs, gathered, _ = _pcall(
        body, (ab, ab, ab, ab, w, bdw, lnw, lnb), name="cv_fwd", grid=(T // tm,),
        in_specs=[tile, pl.BlockSpec((tm, D), lambda i: (i, 1)),
                  halo(0), halo(1), pl.BlockSpec((CONV_WIDTH, D), lambda i: (0, 0)), vec, vec, vec],
        out_specs=[tile, tile],
        out_shape=[jax.ShapeDtypeStruct((T, D), F32), jax.ShapeDtypeStruct((T, D), F32)],
        scratch_shapes=[pltpu.VMEM((CV_HALO + tm, D), F32)], sem=("arbitrary",), ag=ag)
    return outs, gathered


def cv_bwd(ab, cv, w, lnw, lnb, dout, rs=()):
    T = ab.shape[0]
    D = ab.shape[1] // 2
    tm = min(T, 256)
    nt = T // tm
    hb = tm // CV_HALO

    def body(a_ref, b_ref, ah_ref, bh_ref, cv_ref, w_ref, lnw_ref, lnb_ref, do_ref,
             da_ref, db_ref, dw_ref, dbdw_ref, dlnw_ref, dlnb_ref, buf_ref, dbuf_ref):
        r = pl.program_id(0)
        i = nt - 1 - r

        @pl.when(r == 0)
        def _():
            dw_ref[...] = jnp.zeros_like(dw_ref)
            dbdw_ref[...] = jnp.zeros_like(dbdw_ref)
            dlnw_ref[...] = jnp.zeros_like(dlnw_ref)
            dlnb_ref[...] = jnp.zeros_like(dlnb_ref)
            dbuf_ref[tm:, :] = jnp.zeros((CV_HALO, D), F32)

        a = a_ref[...]
        sb = _sigmoid(b_ref[...])
        buf_ref[0:CV_HALO, :] = jnp.where(i > 0, ah_ref[...] * _sigmoid(bh_ref[...]), 0.0)
        buf_ref[CV_HALO:, :] = a * sb
        wv = w_ref[...]
        _, vjp = jax.vjp(_cv_post, cv_ref[...], lnw_ref[...], lnb_ref[...])
        dcv, dlnw, dlnb = vjp(do_ref[...])
        dlnw_ref[...] += dlnw
        dlnb_ref[...] += dlnb
        dbdw_ref[...] += jnp.sum(dcv, axis=0, keepdims=True)
        dbuf_ref[0:tm, :] = dcv
        du = _conv_taps_bwd(dbuf_ref, wv, CONV_WIDTH, tm)
        _conv_dw_acc(dw_ref, dcv, buf_ref, CONV_WIDTH, CV_HALO, tm)
        dbuf_ref[tm:, :] = dcv[0:CV_HALO, :]
        da_ref[...] = du * sb
        db_ref[...] = du * a * sb * (1.0 - sb)

    tile = lambda c: pl.BlockSpec((tm, D), lambda r: (nt - 1 - r, c))
    halo = lambda c: pl.BlockSpec((CV_HALO, D), lambda r: (jnp.maximum((nt - 1 - r) * hb - 1, 0), c))
    vec = pl.BlockSpec((1, D), lambda r: (0, 0))
    wsp = pl.BlockSpec((CONV_WIDTH, D), lambda r: (0, 0))
    (da, db, dw, dbdw, dlnw, dlnb), _, slots = _pcall(
        body, (ab, ab, ab, ab, cv, w, lnw, lnb, dout), name="cv_bwd", grid=(nt,),
        in_specs=[tile(0), tile(1), halo(0), halo(1), tile(0), wsp, vec, vec, tile(0)],
        out_specs=[tile(0), tile(0), wsp, vec, vec, vec],
        out_shape=[jax.ShapeDtypeStruct((T, D), F32), jax.ShapeDtypeStruct((T, D), F32),
                   jax.ShapeDtypeStruct((CONV_WIDTH, D), F32)] + [jax.ShapeDtypeStruct((1, D), F32)] * 3,
        scratch_shapes=[pltpu.VMEM((CV_HALO + tm, D), F32), pltpu.VMEM((tm + CV_HALO, D), F32)],
        sem=("arbitrary",), rs=rs)
    return (jnp.concatenate([da, db], axis=1), dw, dbdw, dlnw, dlnb), slots


def adamw(w, m, v, slots, rs=()):
    L, R, C = w.shape
    ns = slots[0].shape[0]
    fits = lambda r, c: ns * r * c * 2 <= ADAM_SLOT_BLOCK
    tiles = [(R, C)] if fits(R, C) else []
    tiles += [(d, C) for d in range(16, R, 16) if R % d == 0 and fits(d, C)]
    tiles += [(R, d) for d in range(LANES, C, LANES) if C % d == 0 and fits(R, d)]
    tr, tc = max(tiles, key=lambda t: t[0] * t[1])
    c1 = 1.0 / (1.0 - ADAM_B1 ** ADAM_STEP)
    c2 = 1.0 / (1.0 - ADAM_B2 ** ADAM_STEP)

    def body(w_ref, m_ref, v_ref, *rest):
        s_refs = rest[:L]
        g_ref, d_ref, nm_ref, nv_ref = rest[L:]
        l = pl.program_id(0)
        for k in range(L):
            @pl.when(l == k)
            def _(s_ref=s_refs[k]):
                g = s_ref[0].astype(F32)
                for j in range(1, ns):
                    g = g + s_ref[j].astype(F32)
                nm = ADAM_B1 * m_ref[0] + (1.0 - ADAM_B1) * g
                nv = ADAM_B2 * v_ref[0] + (1.0 - ADAM_B2) * (g * g)
                g_ref[0] = g
                nm_ref[0] = nm
                nv_ref[0] = nv
                d_ref[0] = -ADAM_LR * ((nm * c1) / (jnp.sqrt(nv * c2) + ADAM_EPS) + ADAM_WD * w_ref[0])

    nc = C // tc
    blk = pl.BlockSpec((1, tr, tc), lambda l, i: (l, i // nc, i % nc))
    slot = lambda k: pl.BlockSpec((ns, tr, tc), lambda l, i: (0, jnp.where(l == k, i // nc, 0),
                                                                 jnp.where(l == k, i % nc, 0)))
    outs, _, landed = _pcall(
        body, (w, m, v, *slots), name="adamw", grid=(L, (R // tr) * nc),
        in_specs=[blk, blk, blk] + [slot(k) for k in range(L)],
        out_specs=[blk, blk, blk, blk],
        out_shape=[jax.ShapeDtypeStruct((L, R, C), F32)] * 4,
        sem=("arbitrary", "arbitrary"), rs=rs)
    return outs, landed


def _unshard(g, axis):
    g = jnp.moveaxis(g, 0, axis)
    s = g.shape
    return g.reshape(s[:axis] + (s[axis] * s[axis + 1],) + s[axis + 2:])


def _to_blocks(full, axis):
    s = full.shape
    g = full.reshape(s[:axis] + (N_DEV, s[axis] // N_DEV) + s[axis + 1:])
    return jnp.moveaxis(g, axis, 0)


SMALL = (("norm_w", 2), ("dn_conv_w", 2), ("conv_b_pw1", 1), ("conv_w_dw", 2), ("conv_b_dw", 1),
         ("conv_ln_w", 1), ("conv_ln_b", 1), ("conv_b_pw2", 1),
         ("attn_sinks", None), ("dn_a_log", None), ("dn_dt_bias", None), ("dn_norm_w", None), ("final_norm_w", None))
SMALL_AXIS = dict(SMALL)


def _pack(parts):
    flat = jnp.concatenate([p.reshape(-1) for p in parts])
    pad = (-flat.shape[0]) % LANES
    return jnp.pad(flat, (0, pad))


def _unpack(flat, shapes):
    out, off = [], 0
    for s in shapes:
        n = int(np.prod(s))
        out.append(flat[off:off + n].reshape(s))
        off += n
    return out


def kernel(x, norm_w, ffn_w_gate, ffn_w_up, ffn_w_down, mix_w_in, dn_conv_w, attn_sinks, dn_a_log, dn_dt_bias, dn_norm_w, mix_w_out, conv_w_pw1, conv_b_pw1, conv_w_dw, conv_b_dw, conv_ln_w, conv_ln_b, conv_w_pw2, conv_b_pw2, final_norm_w, loss_target, m_norm_w, m_ffn_w_gate, m_ffn_w_up, m_ffn_w_down, m_mix_w_in, m_dn_conv_w, m_attn_sinks, m_dn_a_log, m_dn_dt_bias, m_dn_norm_w, m_mix_w_out, m_conv_w_pw1, m_conv_b_pw1, m_conv_w_dw, m_conv_b_dw, m_conv_ln_w, m_conv_ln_b, m_conv_w_pw2, m_conv_b_pw2, m_final_norm_w, v_norm_w, v_ffn_w_gate, v_ffn_w_up, v_ffn_w_down, v_mix_w_in, v_dn_conv_w, v_attn_sinks, v_dn_a_log, v_dn_dt_bias, v_dn_norm_w, v_mix_w_out, v_conv_w_pw1, v_conv_b_pw1, v_conv_w_dw, v_conv_b_dw, v_conv_ln_w, v_conv_ln_b, v_conv_w_pw2, v_conv_b_pw2, v_final_norm_w):
    W = dict(norm_w=norm_w, ffn_w_gate=ffn_w_gate, ffn_w_up=ffn_w_up, ffn_w_down=ffn_w_down, mix_w_in=mix_w_in,
             dn_conv_w=dn_conv_w, attn_sinks=attn_sinks, dn_a_log=dn_a_log, dn_dt_bias=dn_dt_bias,
             dn_norm_w=dn_norm_w, mix_w_out=mix_w_out, conv_w_pw1=conv_w_pw1, conv_b_pw1=conv_b_pw1,
             conv_w_dw=conv_w_dw, conv_b_dw=conv_b_dw, conv_ln_w=conv_ln_w, conv_ln_b=conv_ln_b,
             conv_w_pw2=conv_w_pw2, conv_b_pw2=conv_b_pw2, final_norm_w=final_norm_w)
    M = dict(norm_w=m_norm_w, ffn_w_gate=m_ffn_w_gate, ffn_w_up=m_ffn_w_up, ffn_w_down=m_ffn_w_down,
             mix_w_in=m_mix_w_in, dn_conv_w=m_dn_conv_w, attn_sinks=m_attn_sinks, dn_a_log=m_dn_a_log,
             dn_dt_bias=m_dn_dt_bias, dn_norm_w=m_dn_norm_w, mix_w_out=m_mix_w_out, conv_w_pw1=m_conv_w_pw1,
             conv_b_pw1=m_conv_b_pw1, conv_w_dw=m_conv_w_dw, conv_b_dw=m_conv_b_dw, conv_ln_w=m_conv_ln_w,
             conv_ln_b=m_conv_ln_b, conv_w_pw2=m_conv_w_pw2, conv_b_pw2=m_conv_b_pw2, final_norm_w=m_final_norm_w)
    V = dict(norm_w=v_norm_w, ffn_w_gate=v_ffn_w_gate, ffn_w_up=v_ffn_w_up, ffn_w_down=v_ffn_w_down,
             mix_w_in=v_mix_w_in, dn_conv_w=v_dn_conv_w, attn_sinks=v_attn_sinks, dn_a_log=v_dn_a_log,
             dn_dt_bias=v_dn_dt_bias, dn_norm_w=v_dn_norm_w, mix_w_out=v_mix_w_out, conv_w_pw1=v_conv_w_pw1,
             conv_b_pw1=v_conv_b_pw1, conv_w_dw=v_conv_w_dw, conv_b_dw=v_conv_b_dw, conv_ln_w=v_conv_ln_w,
             conv_ln_b=v_conv_ln_b, conv_w_pw2=v_conv_w_pw2, conv_b_pw2=v_conv_b_pw2, final_norm_w=v_final_norm_w)

    T, D = x.shape[1], x.shape[2]
    xs = x[0]

    big = ("ffn_w_gate", "ffn_w_up", "ffn_w_down", "mix_w_in", "mix_w_out", "conv_w_pw1", "conv_w_pw2")
    shard3 = {k: W[k].reshape((-1,) + W[k].shape[-2:]) for k in big}
    shard_bf = {k: shard3[k].astype(BF16) for k in big}
    ffn_unit = lambda i: [("ffn_w_gate", i), ("ffn_w_up", i), ("ffn_w_down", i)]
    even_unit = lambda e: [("mix_w_in", e), ("mix_w_out", e)]
    odd_unit = lambda e: [("conv_w_pw1", e), ("conv_w_pw2", e)]
    have = {}

    def ag_jobs(units):
        return [(shard_bf[k], i) for k, i in units]

    def ag_done(units, gathered):
        have.update(zip(units, gathered))

    small_sharded = [(k, ax) for k, ax in SMALL if ax is not None]
    small_pack = _pack([W[k] for k, _ in small_sharded])[None, :]
    first_units = ffn_unit(0)
    gathered = exchange(ag_jobs(first_units) + [(small_pack, None)])
    ag_done(first_units, gathered[:-1])
    small_full = {}
    for (k, ax), parts in zip(small_sharded,
                              zip(*[_unpack(gathered[-1][s, 0], [W[k].shape for k, _ in small_sharded])
                                    for s in range(N_DEV)])):
        small_full[k] = _unshard(jnp.stack(parts), ax)
    nw_full = small_full["norm_w"]

    ffn_w = lambda i: [have[u] for u in ffn_unit(i)]
    w_in_of = lambda e: jnp.pad(_unshard(have[("mix_w_in", e)], 1), ((0, 0), (0, IN_COLS_PAD - IN_COLS)))
    w_out_of = lambda e: have[("mix_w_out", e)].reshape(D, D)
    w_pw1_of = lambda e: _unshard(have[("conv_w_pw1", e)], 1)
    w_pw2_of = lambda e: have[("conv_w_pw2", e)].reshape(D, D)
    fwd_order, needed = [], {}
    for l in range(DEPTH):
        mixer = [("A", l), ("E", l)] if l % 2 == 0 else [("O", l)]
        fwd_order += [("F", 2 * l)] + mixer + [("F", 2 * l + 1)]
        needed[("F", 2 * l)], needed[("F", 2 * l + 1)] = ffn_unit(2 * l), ffn_unit(2 * l + 1)
        needed[mixer[0]] = even_unit(l // 2) if l % 2 == 0 else odd_unit(l // 2)
    queue = [(u, pos) for pos, key in enumerate(fwd_order) for u in needed.get(key, []) if u not in first_units]
    unit_bytes = lambda u: N_DEV * shard_bf[u[0]][u[1]].size * 2
    fwd_carry, at = {}, 0
    for pos, key in enumerate(fwd_order):
        cap = FWD_CARRY_BYTES[key[0]]
        taken, used = [], 0
        while at < len(queue) and (queue[at][1] <= pos + 1 or used + unit_bytes(queue[at][0]) <= cap):
            taken.append(queue[at][0])
            used += unit_bytes(queue[at][0])
            at += 1
        fwd_carry[key] = taken
    zero_in = jnp.zeros((1, IN_COLS_PAD), F32)
    zero_d = jnp.zeros((1, D), F32)
    slope_rows = jnp.asarray(np.repeat(2.0 ** (-8.0 * np.arange(1, ATTN_HEADS + 1) / ATTN_HEADS), ATTN_BLOCK)
                             .astype(np.float32)[:, None])

    saved = []
    h = xs
    w_in, w_out, w_pw1, w_pw2 = {}, {}, {}, {}

    def ffn_forward(h, l, half):
        i = 2 * l + half
        units = fwd_carry.get(("F", i), [])
        h, gathered = ffn_fwd(h, nw_full[l, 2 * half][None], *ffn_w(i), ag=ag_jobs(units))
        ag_done(units, gathered)
        return h

    for l in range(DEPTH):
        e = l // 2
        st = {"x0": h}
        h = ffn_forward(h, l, 0)
        st["x1"] = h
        if l % 2 == 0:
            w_in[e], w_out[e] = w_in_of(e), w_out_of(e)
            proj = rmslin_fwd(h, nw_full[l, 1][None], w_in[e], zero_in)
            st["proj"] = proj
            st["qkvc"] = dnconv_fwd(proj, small_full["dn_conv_w"][e])
            st["sink_rows"] = jnp.repeat(attn_sinks[e], ATTN_BLOCK)[:, None]
            st["alog"] = dn_a_log[e].reshape(DN_HEADS, 1, 1)
            st["dtb"] = dn_dt_bias[e].reshape(DN_HEADS, 1, 1)
            st["dnw"] = dn_norm_w[e].reshape(1, 1, DN_D)
            units = fwd_carry[("A", l)]
            st["att"], gathered = attn_fwd(proj, st["sink_rows"], slope_rows, ag=ag_jobs(units))
            ag_done(units, gathered)
            units = fwd_carry[("E", l)]
            (st["og"], st["sall"]), gathered = dn_fwd(st["qkvc"], proj, st["alog"], st["dtb"], st["dnw"],
                                                      ag=ag_jobs(units))
            ag_done(units, gathered)
            h = lin_fwd(h, [st["att"], st["og"]], w_out[e], zero_d)
        else:
            units = fwd_carry[("O", l)]
            w_pw1[e], w_pw2[e] = w_pw1_of(e), w_pw2_of(e)
            st["ab"] = rmslin_fwd(h, nw_full[l, 1][None], w_pw1[e], small_full["conv_b_pw1"][e][None])
            (st["act"], st["cv"]), gathered = cv_fwd(st["ab"], small_full["conv_w_dw"][e], small_full["conv_b_dw"][e][None],
                                         small_full["conv_ln_w"][e][None], small_full["conv_ln_b"][e][None],
                                         ag=ag_jobs(units))
            ag_done(units, gathered)
            h = lin_fwd(h, [st["act"]], w_pw2[e], small_full["conv_b_pw2"][e][None])
        st["x2"] = h
        h = ffn_forward(h, l, 1)
        saved.append(st)

    loss_part, dh, dfinal = loss_fwd_bwd(h, final_norm_w[None], loss_target[0])
    loss = lax.psum(loss_part[0, 0], ("x", "y", "c"))

    d_norm = [[None] * 3 for _ in range(DEPTH)]
    d_small = {k: [None, None] for k in ("dn_conv_w", "conv_b_pw1", "conv_w_dw", "conv_b_dw", "conv_ln_w",
                                         "conv_ln_b", "conv_b_pw2", "attn_sinks", "dn_a_log", "dn_dt_bias",
                                         "dn_norm_w")}
    pending, slot = [], {}

    def take_jobs(cap=None, only=None):
        taken = [p for p in pending if p[1] == "swap"]
        used = 0
        for p in pending:
            if p[1] == "swap" or (only is not None and p[0][0] not in only):
                continue
            if cap is not None and used + p[2].size * 2 > cap:
                break
            taken.append(p)
            used += p[2].size * 2
        pending[:] = [p for p in pending if all(p is not t for t in taken)]
        return taken, [(kind, arr) for _, kind, arr in taken]

    def land(taken, results):
        swapped = [(unit, arr, res) for (unit, kind, arr), res in zip(taken, results) if kind == "swap"]
        slot.update({unit: res for (unit, kind, _), res in zip(taken, results) if kind != "swap"})
        if swapped:
            sums = pair_add([g for _, g, _ in swapped], [r for _, _, r in swapped])
            pending.extend((unit, "chips", h) for (unit, _, _), h in zip(swapped, sums))

    def ffn_backward(dh, l, half):
        i = 2 * l + half
        taken, jobs = take_jobs(BWD_CARRY_BYTES["F"])
        (dh, dg, du, dd, d_norm[l][2 * half]), results = ffn_bwd(
            st["x2" if half else "x0"], dh, nw_full[l, 2 * half][None], *ffn_w(i), rs=jobs)
        land(taken, results)
        pending.extend((u, "swap", g) for u, g in zip(ffn_unit(i), (dg, du, dd)))
        return dh

    for l in reversed(range(DEPTH)):
        e = l // 2
        st = saved[l]
        dh = ffn_backward(dh, l, 1)
        if l % 2 == 0:
            dmix, d_out, _ = lin_bwd([st["att"], st["og"]], dh, w_out[e])
            pending.append((("mix_w_out", e), "direct", d_out.reshape(N_DEV, D // N_DEV, D).astype(BF16)))
            taken, jobs = take_jobs(BWD_CARRY_BYTES["E"])
            (dqkvc, dzba, dalog, ddtb, ddnw), results = dn_bwd(
                st["qkvc"], st["proj"], st["alog"], st["dtb"], st["dnw"], st["sall"], dmix, rs=jobs)
            land(taken, results)
            taken, jobs = take_jobs(BWD_CARRY_BYTES["A"])
            (dqa, dkva, dsink), results = attn_bwd(st["proj"], st["sink_rows"], slope_rows, dmix, rs=jobs)
            land(taken, results)
            dqkv, d_small["dn_conv_w"][e] = dnconv_bwd(st["proj"], small_full["dn_conv_w"][e], dqkvc)
            dh, d_in, _, d_norm[l][1] = rmslin_bwd(st["x1"], dh, [dqa, dkva, dqkv, dzba], nw_full[l, 1][None],
                                                   w_in[e])
            pending.append((("mix_w_in", e), "direct", _to_blocks(d_in[:, :IN_COLS], 1)))
            d_small["attn_sinks"][e] = jnp.sum(dsink.reshape(ATTN_HEADS, ATTN_BLOCK), axis=1)
            d_small["dn_a_log"][e] = dalog.reshape(DN_HEADS)
            d_small["dn_dt_bias"][e] = ddtb.reshape(DN_HEADS)
            d_small["dn_norm_w"][e] = ddnw.reshape(DN_D)
        else:
            dact, d_pw2, d_small["conv_b_pw2"][e] = lin_bwd([st["act"]], dh, w_pw2[e])
            pending.append((("conv_w_pw2", e), "direct", d_pw2.reshape(N_DEV, D // N_DEV, D).astype(BF16)))
            taken, jobs = take_jobs(BWD_CARRY_BYTES["O"])
            (dab, d_small["conv_w_dw"][e], d_small["conv_b_dw"][e], d_small["conv_ln_w"][e],
             d_small["conv_ln_b"][e]), results = cv_bwd(
                st["ab"], st["cv"], small_full["conv_w_dw"][e],
                small_full["conv_ln_w"][e][None], small_full["conv_ln_b"][e][None], dact, rs=jobs)
            land(taken, results)
            dh, d_pw1, d_small["conv_b_pw1"][e], d_norm[l][1] = rmslin_bwd(
                st["x1"], dh, dab, nw_full[l, 1][None], w_pw1[e])
            pending.append((("conv_w_pw1", e), "direct", _to_blocks(d_pw1, 1)))
        dh = ffn_backward(dh, l, 0)
    grad_x = dh[None]

    full_small = {"norm_w": jnp.stack([jnp.concatenate(r, axis=0) for r in d_norm]),
                  "final_norm_w": dfinal[0]}
    for k, pair in d_small.items():
        full_small[k] = jnp.stack([p.reshape(W[k].shape[1:-1] + (-1,)) if SMALL_AXIS[k] is not None
                                   else p for p in pair])
    rows = []
    for s in range(N_DEV):
        parts = [_to_blocks(full_small[k], ax)[s] if ax is not None else full_small[k] for k, ax in SMALL]
        rows.append(_pack(parts))
    send_small = jnp.stack(rows)[:, None, :]
    pending.append((("small", 0), "direct", send_small))

    res = {}
    waiting = lambda k: [p for p in pending if p[0][0] == k]
    adam_order = sorted(big, key=lambda k: len(waiting(k))) + ["small"]
    for n, k in enumerate(adam_order[:-1]):
        nxt = next((kk for kk in adam_order[n + 1:] if waiting(kk)), "small")
        taken, jobs = take_jobs(only=(nxt, "small") if n == 1 else (nxt,))
        turned = k in ("ffn_w_gate", "ffn_w_up")
        view = lambda a: jnp.swapaxes(a.reshape(shard3[k].shape), 1, 2) if turned else a.reshape(shard3[k].shape)
        outs, results = adamw(view(W[k]), view(M[k]), view(V[k]),
                              [slot[(k, i)] for i in range(shard3[k].shape[0])], rs=jobs)
        land(taken, results)
        res[k] = [(jnp.swapaxes(o, 1, 2) if turned else o).reshape(W[k].shape) for o in outs]
    pk = lambda d: _pack([d[k] for k, _ in SMALL])[None, None, :]
    outs, _ = adamw(pk(W), pk(M), pk(V), [slot[("small", 0)]])
    shapes = [W[k].shape for k, _ in SMALL]
    unp = [_unpack(o[0, 0], shapes) for o in outs]
    for i, (k, _) in enumerate(SMALL):
        res[k] = [u[i] for u in unp]

    order = ("norm_w", "ffn_w_gate", "ffn_w_up", "ffn_w_down", "mix_w_in", "dn_conv_w", "attn_sinks", "dn_a_log",
             "dn_dt_bias", "dn_norm_w", "mix_w_out", "conv_w_pw1", "conv_b_pw1", "conv_w_dw", "conv_b_dw",
             "conv_ln_w", "conv_ln_b", "conv_w_pw2", "conv_b_pw2", "final_norm_w")
    return (loss, grad_x, *[res[k][0] for k in order], *[res[k][1] for k in order],
            *[res[k][2] for k in order], *[res[k][3] for k in order])
```

```python
import functools

import numpy as np
import jax
import jax.numpy as jnp
from jax import lax
from jax.experimental import pallas as pl
from jax.experimental.pallas import tpu as pltpu

F32 = jnp.float32
BF16 = jnp.bfloat16
EPS = 1e-6
N_DEV = 8
N_CHIP = 4
V7X_VMEM_LIMIT = 60 * 2**20
MESH = pl.DeviceIdType.MESH
LANES = 128
SUBLANES = 8

DEPTH = 4
D_MODEL = 1024
ATTN_HEADS, ATTN_KV_HEADS, HEAD_DIM, ATTN_BLOCK = 8, 2, 64, 128
DN_HEADS, DN_D, DN_CHUNK, DN_CONV = 8, 64, 64, 4
CONV_WIDTH = 31
Q_A, KV_A, QKV_B, V_B = 512, 128, 1536, 512
IN_COLS = 2832
IN_COLS_PAD = 3072
OFF_QKVB = Q_A + 2 * KV_A
OFF_Z = OFF_QKVB + QKV_B
OFF_BETA = OFF_Z + V_B
OFF_A = OFF_BETA + DN_HEADS

FWD_CARRY_BYTES = {"F": 12 * 2**20, "A": 6 * 2**20, "E": 18 * 2**20, "O": 12 * 2**20}
BWD_CARRY_BYTES = {"F": 11 * 2**20, "A": 6 * 2**20, "E": 13 * 2**20, "O": 10 * 2**20}

ADAM_SLOT_BLOCK = 3 * 2**19

ADAM_LR, ADAM_B1, ADAM_B2, ADAM_EPS, ADAM_WD, ADAM_STEP = 0.001, 0.9, 0.999, 1e-08, 0.01, 10


def _cparams(sem):
    return pltpu.CompilerParams(dimension_semantics=sem, vmem_limit_bytes=V7X_VMEM_LIMIT)


def _sigmoid(x):
    return 1.0 / (1.0 + jnp.exp(-x))


def _softplus(x):
    return jnp.maximum(x, 0.0) + jnp.log(1.0 + jnp.exp(-jnp.abs(x)))


def _dot(a, b):
    return jnp.dot(a, b, preferred_element_type=F32)


def _dot_nt(a, b):
    return lax.dot_general(a, b, (((1,), (1,)), ((), ())), preferred_element_type=F32)


def _dot_tn(a, b):
    return lax.dot_general(a, b, (((0,), (0,)), ((), ())), preferred_element_type=F32)


def _rms(x, w):
    return x * lax.rsqrt(jnp.mean(x * x, axis=-1, keepdims=True) + EPS) * w


def _rms_bwd(x, w, dxn):
    r = lax.rsqrt(jnp.mean(x * x, axis=-1, keepdims=True) + EPS)
    xh = x * r
    dxh = dxn * w
    dx = r * (dxh - xh * jnp.mean(dxh * xh, axis=-1, keepdims=True))
    return dx, jnp.sum(dxn * xh, axis=0, keepdims=True)


def _position():
    return lax.axis_index("x"), lax.axis_index("y"), lax.axis_index("c")


def _dev_index(px, py, pc):
    return 4 * px + 2 * py + pc


def _rcopy(src, dst, send_sem, recv_sem, to):
    return pltpu.make_async_remote_copy(src_ref=src, dst_ref=dst, send_sem=send_sem, recv_sem=recv_sem,
                                        device_id=to, device_id_type=MESH)


def _ag_start(srcs, outs, send, recv, local):
    x, y, c = _position()
    me = _dev_index(x, y, c)
    chips = [(1 - x, y), (x, 1 - y), (1 - x, 1 - y)]
    for a, (src, out) in enumerate(zip(srcs, outs)):
        pltpu.make_async_copy(src, out.at[me], local.at[a]).start()
        _rcopy(src, out.at[me], send.at[a, 0], recv.at[a, 0], (x, y, 1 - c)).start()
        for j, chip in enumerate(chips):
            _rcopy(src, out.at[me], send.at[a, 1 + j], recv.at[a, 1 + j], (*chip, c)).start()


def _ag_finish(srcs, outs, send, recv, local):
    x, y, c = _position()
    me = _dev_index(x, y, c)
    sibling = (x, y, 1 - c)
    chips = [(1 - x, y), (x, 1 - y), (1 - x, 1 - y)]
    for j, chip in enumerate(chips):
        for a, out in enumerate(outs):
            blk = out.at[_dev_index(*chip, c)]
            _rcopy(blk, blk, send.at[a, 1 + j], recv.at[a, 1 + j], (x, y, c)).wait_recv()
            _rcopy(blk, blk, send.at[a, 4 + j], recv.at[a, 4 + j], sibling).start()
    for a, (src, out) in enumerate(zip(srcs, outs)):
        blk = out.at[_dev_index(x, y, 1 - c)]
        _rcopy(blk, blk, send.at[a, 0], recv.at[a, 0], (x, y, c)).wait_recv()
        for j, chip in enumerate(chips):
            blk = out.at[_dev_index(*chip, 1 - c)]
            _rcopy(blk, blk, send.at[a, 4 + j], recv.at[a, 4 + j], (x, y, c)).wait_recv()
        for k in range(N_DEV - 1):
            _rcopy(out.at[me], out.at[me], send.at[a, k], recv.at[a, k], (x, y, c)).wait_send()
        pltpu.make_async_copy(src, out.at[me], local.at[a]).wait()


def _rs_peer(r):
    x, y, c = _position()
    return x ^ ((r >> 2) & 1), y ^ ((r >> 1) & 1), c ^ (r & 1)


def _rs_start(ins, outs, send, recv, local):
    me = _dev_index(*_position())
    for a, (src, out) in enumerate(zip(ins, outs)):
        pltpu.make_async_copy(src.at[me], out.at[me], local.at[a]).start()
        for r in range(1, N_DEV):
            p = _rs_peer(r)
            _rcopy(src.at[_dev_index(*p)], out.at[me], send.at[a, r - 1], recv.at[a, r - 1], p).start()


def _rs_finish(ins, outs, send, recv, local):
    pos = _position()
    me = _dev_index(*pos)
    for a, (src, out) in enumerate(zip(ins, outs)):
        for r in range(1, N_DEV):
            blk = out.at[_dev_index(*_rs_peer(r))]
            _rcopy(blk, blk, send.at[a, r - 1], recv.at[a, r - 1], pos).wait_recv()
        for r in range(1, N_DEV):
            _rcopy(src.at[me], out.at[me], send.at[a, r - 1], recv.at[a, r - 1], pos).wait_send()
        pltpu.make_async_copy(src.at[me], out.at[me], local.at[a]).wait()


def _sw_start(ins, outs, send, recv):
    x, y, c = _position()
    for a, (src, out) in enumerate(zip(ins, outs)):
        for q in range(N_CHIP):
            _rcopy(src.at[2 * q + (1 - c)], out.at[q], send.at[a, q], recv.at[a, q], (x, y, 1 - c)).start()


def _sw_finish(ins, outs, send, recv):
    pos = _position()
    for a, out in enumerate(outs):
        for q in range(N_CHIP):
            _rcopy(out.at[q], out.at[q], send.at[a, q], recv.at[a, q], pos).wait_recv()
        for q in range(N_CHIP):
            _rcopy(out.at[q], out.at[q], send.at[a, q], recv.at[a, q], pos).wait_send()


def _r4_peer(r):
    x, y, c = _position()
    return x ^ ((r >> 1) & 1), y ^ (r & 1), c


def _r4_start(ins, outs, send, recv, local):
    x, y, c = _position()
    mine = 2 * x + y
    for a, (src, out) in enumerate(zip(ins, outs)):
        pltpu.make_async_copy(src.at[mine], out.at[mine], local.at[a]).start()
        for r in range(1, N_CHIP):
            px, py, pc = _r4_peer(r)
            _rcopy(src.at[2 * px + py], out.at[mine], send.at[a, r - 1], recv.at[a, r - 1], (px, py, pc)).start()


def _r4_finish(ins, outs, send, recv, local):
    x, y, c = _position()
    mine = 2 * x + y
    for a, (src, out) in enumerate(zip(ins, outs)):
        for r in range(1, N_CHIP):
            px, py, _ = _r4_peer(r)
            blk = out.at[2 * px + py]
            _rcopy(blk, blk, send.at[a, r - 1], recv.at[a, r - 1], (x, y, c)).wait_recv()
        for r in range(1, N_CHIP):
            _rcopy(src.at[mine], out.at[mine], send.at[a, r - 1], recv.at[a, r - 1], (x, y, c)).wait_send()
        pltpu.make_async_copy(src.at[mine], out.at[mine], local.at[a]).wait()


_RS_KINDS = {
    "direct": (_rs_start, _rs_finish, lambda n: [(n, N_DEV - 1), (n, N_DEV - 1), (n,)], lambda s: s),
    "swap": (_sw_start, _sw_finish, lambda n: [(n, N_CHIP), (n, N_CHIP)], lambda s: (N_CHIP,) + s[1:]),
    "chips": (_r4_start, _r4_finish, lambda n: [(n, N_CHIP - 1), (n, N_CHIP - 1), (n,)], lambda s: s),
}


def _pcall(body, args, *, name, grid, in_specs, out_specs, out_shape, sem, scratch_shapes=(), ag=(), rs=()):
    na, nr = len(ag), len(rs)
    if na + nr == 0:
        outs = pl.pallas_call(body, name=name, grid=grid, in_specs=in_specs, out_specs=out_specs,
                              out_shape=out_shape, scratch_shapes=list(scratch_shapes),
                              compiler_params=_cparams(sem))(*args)
        return list(outs), [], []
    n_in, n_out, n_scr = len(in_specs), len(out_specs), len(scratch_shapes)
    ag_idx = [i for _, i in ag]
    groups = [(k, [i for i, (kk, _) in enumerate(rs) if kk == k]) for k in _RS_KINDS]
    groups = [(k, idx) for k, idx in groups if idx]
    sem_counts = ([3] if na else []) + [len(_RS_KINDS[k][2](1)) for k, _ in groups]

    def wrapped(*refs):
        cin, refs = refs[:n_in], refs[n_in:]
        ag_in, refs = refs[:na], refs[na:]
        rs_in, refs = refs[:nr], refs[nr:]
        cout, refs = refs[:n_out], refs[n_out:]
        ag_out, refs = refs[:na], refs[na:]
        rs_out, refs = refs[:nr], refs[nr:]
        cscr, sems = refs[:n_scr], list(refs[n_scr:])
        sem_sets = [[sems.pop(0) for _ in range(n)] for n in sem_counts]
        ag_sems = sem_sets.pop(0) if na else None
        ag_src = [r if i is None else r.at[i] for r, i in zip(ag_in, ag_idx)]
        ids = [pl.program_id(d) for d in range(len(grid))]
        first = functools.reduce(jnp.logical_and, [i == 0 for i in ids])
        last = functools.reduce(jnp.logical_and, [i == g - 1 for i, g in zip(ids, grid)])

        def run(phase):
            if na:
                (_ag_start, _ag_finish)[phase](ag_src, ag_out, *ag_sems)
            for (k, idx), ss in zip(groups, sem_sets):
                _RS_KINDS[k][phase]([rs_in[i] for i in idx], [rs_out[i] for i in idx], *ss)

        @pl.when(first)
        def _():
            run(0)

        body(*cin, *cout, *cscr)

        @pl.when(last)
        def _():
            run(1)

    hbm = pl.BlockSpec(memory_space=pl.ANY)
    sem_shapes = [pltpu.SemaphoreType.DMA(s) for s in ([(na, N_DEV - 1), (na, N_DEV - 1), (na,)] if na else [])]
    for k, idx in groups:
        sem_shapes += [pltpu.SemaphoreType.DMA(s) for s in _RS_KINDS[k][2](len(idx))]
    outs = pl.pallas_call(
        wrapped, name=name, grid=grid,
        in_specs=list(in_specs) + [hbm] * (na + nr),
        out_specs=list(out_specs) + [hbm] * (na + nr),
        out_shape=list(out_shape)
        + [jax.ShapeDtypeStruct((N_DEV,) + a.shape[-2:], a.dtype) for a, _ in ag]
        + [jax.ShapeDtypeStruct(_RS_KINDS[k][3](b.shape), b.dtype) for k, b in rs],
        scratch_shapes=list(scratch_shapes) + sem_shapes,
        compiler_params=_cparams(sem),
    )(*args, *[a for a, _ in ag], *[b for _, b in rs])
    return list(outs[:n_out]), list(outs[n_out:n_out + na]), list(outs[n_out + na:])


def exchange(ag):
    def body(o_ref):
        o_ref[...] = jnp.zeros_like(o_ref)

    _, gathered, _ = _pcall(body, (), name="exchange", grid=(1,), in_specs=[],
                            out_specs=[pl.BlockSpec((8, LANES), lambda i: (0, 0))],
                            out_shape=[jax.ShapeDtypeStruct((8, LANES), F32)], sem=("arbitrary",), ag=ag)
    return gathered


def pair_add(blocks, received):
    n = len(blocks)

    def body(core_ref, *refs):
        for g_ref, p_ref, o_ref in zip(refs[:n], refs[n:2 * n], refs[2 * n:]):
            o_ref[0] = (g_ref[0, 0].astype(F32) + p_ref[0].astype(F32)).astype(BF16)

    halves = 2
    g_specs = [pl.BlockSpec((1, 1, b.shape[1] // halves, b.shape[2]), lambda q, r, core: (q, core[0], r, 0))
               for b in blocks]
    p_specs = [pl.BlockSpec((1, b.shape[1] // halves, b.shape[2]), lambda q, r, core: (q, r, 0)) for b in blocks]
    return pl.pallas_call(
        body, name="pair_add",
        grid_spec=pltpu.PrefetchScalarGridSpec(num_scalar_prefetch=1, grid=(N_CHIP, halves),
                                               in_specs=g_specs + p_specs, out_specs=p_specs),
        out_shape=[jax.ShapeDtypeStruct(p.shape, BF16) for p in received],
        compiler_params=_cparams(("parallel", "parallel")),
    )(lax.axis_index("c").astype(jnp.int32)[None], *[b.reshape((N_CHIP, 2) + b.shape[1:]) for b in blocks], *received)


FFN_PAIR = 2


def _pair_cols(w_ref):
    return jnp.concatenate([w_ref[p] for p in range(FFN_PAIR)], axis=1)


def ffn_fwd(x, nw, wg, wu, wd, ag=()):
    T, D = x.shape
    F = wg.shape[2]
    P = FFN_PAIR
    J = wg.shape[0] // P
    tm = min(T, 1024)

    def body(x_ref, nw_ref, wg_ref, wu_ref, wd_ref, o_ref, xn_ref, acc_ref):
        j = pl.program_id(1)

        @pl.when(j == 0)
        def _():
            xn_ref[...] = _rms(x_ref[...], nw_ref[...]).astype(BF16)
            acc_ref[...] = jnp.zeros_like(acc_ref)

        xn = xn_ref[...]
        g = _dot(xn, _pair_cols(wg_ref))
        u = _dot(xn, _pair_cols(wu_ref))
        h = (g * _sigmoid(g) * u).astype(BF16)
        acc_ref[...] += _dot(h, wd_ref[...].reshape(P * F, D))

        @pl.when(j == J - 1)
        def _():
            o_ref[...] = x_ref[...] + 0.5 * acc_ref[...]

    (out,), gathered, _ = _pcall(
        body, (x, nw, wg, wu, wd), name="ffn_fwd", grid=(T // tm, J),
        in_specs=[pl.BlockSpec((tm, D), lambda t, j: (t, 0)),
                  pl.BlockSpec((1, D), lambda t, j: (0, 0)),
                  pl.BlockSpec((P, D, F), lambda t, j: (j, 0, 0)),
                  pl.BlockSpec((P, D, F), lambda t, j: (j, 0, 0)),
                  pl.BlockSpec((P, F, D), lambda t, j: (j, 0, 0))],
        out_specs=[pl.BlockSpec((tm, D), lambda t, j: (t, 0))],
        out_shape=[jax.ShapeDtypeStruct((T, D), F32)],
        scratch_shapes=[pltpu.VMEM((tm, D), BF16), pltpu.VMEM((tm, D), F32)],
        sem=("arbitrary", "arbitrary"), ag=ag)
    return out, gathered


def ffn_bwd(x, dy, nw, wg, wu, wd, rs=()):
    T, D = x.shape
    F = wg.shape[2]
    P = FFN_PAIR
    J = wg.shape[0] // P
    tm = min(T, 256)
    nt = T // tm

    def body(x_ref, dy_ref, nw_ref, wg_ref, wu_ref, wd_ref,
             dx_ref, dwg_ref, dwu_ref, dwd_ref, dnw_ref,
             xn_ref, dyh_ref, dxn_ref, awg_ref, awu_ref, awd_ref):
        j = pl.program_id(0)
        t = pl.program_id(1)
        rows = pl.ds(pl.multiple_of(t * tm, tm), tm)

        @pl.when(j == 0)
        def _():
            xn_ref[rows, :] = _rms(x_ref[...], nw_ref[...]).astype(BF16)
            dyh_ref[rows, :] = (0.5 * dy_ref[...]).astype(BF16)
            dxn_ref[rows, :] = jnp.zeros((tm, D), F32)

        @pl.when((j == 0) & (t == 0))
        def _():
            dnw_ref[...] = jnp.zeros_like(dnw_ref)

        @pl.when(t == 0)
        def _():
            awg_ref[...] = jnp.zeros_like(awg_ref)
            awu_ref[...] = jnp.zeros_like(awu_ref)
            awd_ref[...] = jnp.zeros_like(awd_ref)

        xn = xn_ref[rows, :]
        dyh = dyh_ref[rows, :]
        wg2, wu2 = _pair_cols(wg_ref), _pair_cols(wu_ref)
        g = _dot(xn, wg2)
        u = _dot(xn, wu2)
        sg = _sigmoid(g)
        s = g * sg
        h = (s * u).astype(BF16)
        dh = _dot_nt(dyh, wd_ref[...].reshape(P * F, D))
        du = (dh * s).astype(BF16)
        dg = (dh * u * (sg * (1.0 + g * (1.0 - sg)))).astype(BF16)
        awd_ref[...] += _dot_tn(h, dyh)
        awg_ref[...] += _dot_tn(dg, xn)
        awu_ref[...] += _dot_tn(du, xn)
        dxn_ref[rows, :] += _dot_nt(dg, wg2) + _dot_nt(du, wu2)

        @pl.when(t == nt - 1)
        def _():
            dwg_ref[...] = awg_ref[...].astype(BF16).reshape(P, F, D)
            dwu_ref[...] = awu_ref[...].astype(BF16).reshape(P, F, D)
            dwd_ref[...] = awd_ref[...].astype(BF16).reshape(P, F, D)

        @pl.when(j == J - 1)
        def _():
            dx, dnw = _rms_bwd(x_ref[...], nw_ref[...], dxn_ref[rows, :])
            dx_ref[...] = dy_ref[...] + dx
            dnw_ref[...] += dnw

    ends = lambda j, t: (jnp.where((j == 0) | (j == J - 1), t, 0), 0)
    last = lambda j, t: (jnp.where(j == J - 1, t, 0), 0)
    outs, _, slots = _pcall(
        body, (x, dy, nw, wg, wu, wd), name="ffn_bwd", grid=(J, nt),
        in_specs=[pl.BlockSpec((tm, D), ends), pl.BlockSpec((tm, D), ends),
                  pl.BlockSpec((1, D), lambda j, t: (0, 0)),
                  pl.BlockSpec((P, D, F), lambda j, t: (j, 0, 0)),
                  pl.BlockSpec((P, D, F), lambda j, t: (j, 0, 0)),
                  pl.BlockSpec((P, F, D), lambda j, t: (j, 0, 0))],
        out_specs=[pl.BlockSpec((tm, D), last),
                   pl.BlockSpec((P, F, D), lambda j, t: (j, 0, 0)),
                   pl.BlockSpec((P, F, D), lambda j, t: (j, 0, 0)),
                   pl.BlockSpec((P, F, D), lambda j, t: (j, 0, 0)),
                   pl.BlockSpec((1, D), lambda j, t: (0, 0))],
        out_shape=[jax.ShapeDtypeStruct((T, D), F32)] + [jax.ShapeDtypeStruct((P * J, F, D), BF16)] * 3
        + [jax.ShapeDtypeStruct((1, D), F32)],
        scratch_shapes=[pltpu.VMEM((T, D), BF16), pltpu.VMEM((T, D), BF16), pltpu.VMEM((T, D), F32)]
        + [pltpu.VMEM((P * F, D), F32)] * 3,
        sem=("arbitrary", "arbitrary"), rs=rs)
    return outs, slots


def rmslin_fwd(x, nw, w, b):
    T, D = x.shape
    N = w.shape[1]
    tm = min(T, 256)

    def body(x_ref, nw_ref, w_ref, b_ref, o_ref):
        xn = _rms(x_ref[...], nw_ref[...]).astype(BF16)
        o_ref[...] = _dot(xn, w_ref[...]) + b_ref[...]

    return pl.pallas_call(
        body, name="rmslin_fwd", grid=(T // tm,),
        in_specs=[pl.BlockSpec((tm, D), lambda t: (t, 0)), pl.BlockSpec((1, D), lambda t: (0, 0)),
                  pl.BlockSpec((D, N), lambda t: (0, 0)), pl.BlockSpec((1, N), lambda t: (0, 0))],
        out_specs=pl.BlockSpec((tm, N), lambda t: (t, 0)),
        out_shape=jax.ShapeDtypeStruct((T, N), F32),
        compiler_params=_cparams(("parallel",)),
    )(x, nw, w, b)


def rmslin_bwd(x, dres, dproj, nw, w):
    T, D = x.shape
    N = w.shape[1]
    pieces = list(dproj) if isinstance(dproj, (list, tuple)) else [dproj]
    nb = DN_ZCOLS if len(pieces) > 1 else 1024
    nc = N // nb
    tm = min(T, 256)
    nt = T // tm
    n_p = len(pieces)

    def body(x_ref, dres_ref, *refs):
        p_refs, (nw_ref, w_ref, dx_ref, dw_ref, db_ref, dnw_ref, xn_ref, dxn_ref, acc_ref) = refs[:n_p], refs[n_p:]
        c = pl.program_id(0)
        t = pl.program_id(1)
        rows = pl.ds(pl.multiple_of(t * tm, tm), tm)

        @pl.when(c == 0)
        def _():
            xn_ref[rows, :] = _rms(x_ref[...], nw_ref[...]).astype(BF16)
            dxn_ref[rows, :] = jnp.zeros((tm, D), F32)

        @pl.when((c == 0) & (t == 0))
        def _():
            dnw_ref[...] = jnp.zeros_like(dnw_ref)

        @pl.when(t == 0)
        def _():
            acc_ref[...] = jnp.zeros_like(acc_ref)
            db_ref[...] = jnp.zeros_like(db_ref)

        if n_p == 1:
            dpf = p_refs[0][...]
        else:
            dq_ref, dkv_ref, dqkv_ref, dz_ref = p_refs
            dpf = jnp.where(c == 0, jnp.concatenate([dq_ref[...], dkv_ref[...]], axis=1),
                            jnp.where(c == nc - 1, dz_ref[...], dqkv_ref[...]))
        dp = dpf.astype(BF16)
        acc_ref[...] += _dot_tn(xn_ref[rows, :], dp)
        db_ref[...] += jnp.sum(dpf, axis=0, keepdims=True)
        dxn_ref[rows, :] += _dot_nt(dp, w_ref[...])

        @pl.when(t == nt - 1)
        def _():
            dw_ref[...] = acc_ref[...].astype(BF16)

        @pl.when(c == nc - 1)
        def _():
            dx, dnw = _rms_bwd(x_ref[...], nw_ref[...], dxn_ref[rows, :])
            dx_ref[...] = dres_ref[...] + dx
            dnw_ref[...] += dnw

    ends = lambda c, t: (jnp.where((c == 0) | (c == nc - 1), t, 0), 0)
    last = lambda c, t: (jnp.where(c == nc - 1, t, 0), 0)
    first = lambda c, t: (jnp.where(c == 0, t, 0), 0)
    if n_p == 1:
        p_specs = [pl.BlockSpec((tm, nb), lambda c, t: (t, c))]
    else:
        p_specs = [pl.BlockSpec((tm, Q_A), first), pl.BlockSpec((tm, 2 * KV_A), first),
                   pl.BlockSpec((tm, nb), lambda c, t: (jnp.where((c > 0) & (c < nc - 1), t, 0),
                                                        jnp.clip(c - 1, 0, 1))),
                   pl.BlockSpec((tm, nb), last)]
    return pl.pallas_call(
        body, name="rmslin_bwd", grid=(nc, nt),
        in_specs=[pl.BlockSpec((tm, D), ends), pl.BlockSpec((tm, D), last)] + p_specs
        + [pl.BlockSpec((1, D), lambda c, t: (0, 0)), pl.BlockSpec((D, nb), lambda c, t: (0, c))],
        out_specs=[pl.BlockSpec((tm, D), last),
                   pl.BlockSpec((D, nb), lambda c, t: (0, c)),
                   pl.BlockSpec((1, nb), lambda c, t: (0, c)),
                   pl.BlockSpec((1, D), lambda c, t: (0, 0))],
        out_shape=[jax.ShapeDtypeStruct((T, D), F32), jax.ShapeDtypeStruct((D, N), BF16),
                   jax.ShapeDtypeStruct((1, N), F32), jax.ShapeDtypeStruct((1, D), F32)],
        scratch_shapes=[pltpu.VMEM((T, D), BF16), pltpu.VMEM((T, D), F32), pltpu.VMEM((D, nb), F32)],
        compiler_params=_cparams(("arbitrary", "arbitrary")),
    )(x, dres, *pieces, nw, w)


def lin_fwd(res, parts, w, b):
    T = res.shape[0]
    K, N = w.shape
    tm = min(T, 512)
    n = len(parts)
    offs = [sum(p.shape[1] for p in parts[:i]) for i in range(n + 1)]

    def body(res_ref, *refs):
        a_refs, (w_ref, b_ref, o_ref) = refs[:n], refs[n:]
        acc = res_ref[...] + b_ref[...]
        for i, a_ref in enumerate(a_refs):
            acc = acc + _dot(a_ref[...].astype(BF16), w_ref[offs[i]:offs[i + 1], :])
        o_ref[...] = acc

    return pl.pallas_call(
        body, name="lin_fwd", grid=(T // tm,),
        in_specs=[pl.BlockSpec((tm, N), lambda t: (t, 0))]
        + [pl.BlockSpec((tm, p.shape[1]), lambda t: (t, 0)) for p in parts]
        + [pl.BlockSpec((K, N), lambda t: (0, 0)), pl.BlockSpec((1, N), lambda t: (0, 0))],
        out_specs=pl.BlockSpec((tm, N), lambda t: (t, 0)),
        out_shape=jax.ShapeDtypeStruct((T, N), F32),
        compiler_params=_cparams(("parallel",)),
    )(res, *parts, w, b)


def lin_bwd(parts, dy, w):
    T = dy.shape[0]
    K, N = w.shape
    tm = min(T, 256)
    n = len(parts)
    offs = [sum(p.shape[1] for p in parts[:i]) for i in range(n + 1)]

    def body(*refs):
        a_refs, (dy_ref, w_ref, da_ref, dw_ref, db_ref) = refs[:n], refs[n:]

        @pl.when(pl.program_id(0) == 0)
        def _():
            dw_ref[...] = jnp.zeros_like(dw_ref)
            db_ref[...] = jnp.zeros_like(db_ref)

        dyf = dy_ref[...]
        dyb = dyf.astype(BF16)
        da_ref[...] = _dot_nt(dyb, w_ref[...])
        for i, a_ref in enumerate(a_refs):
            dw_ref[offs[i]:offs[i + 1], :] += _dot_tn(a_ref[...].astype(BF16), dyb)
        db_ref[...] += jnp.sum(dyf, axis=0, keepdims=True)

    return pl.pallas_call(
        body, name="lin_bwd", grid=(T // tm,),
        in_specs=[pl.BlockSpec((tm, p.shape[1]), lambda t: (t, 0)) for p in parts]
        + [pl.BlockSpec((tm, N), lambda t: (t, 0)), pl.BlockSpec((K, N), lambda t: (0, 0))],
        out_specs=[pl.BlockSpec((tm, K), lambda t: (t, 0)), pl.BlockSpec((K, N), lambda t: (0, 0)),
                   pl.BlockSpec((1, N), lambda t: (0, 0))],
        out_shape=[jax.ShapeDtypeStruct((T, K), F32), jax.ShapeDtypeStruct((K, N), F32),
                   jax.ShapeDtypeStruct((1, N), F32)],
        compiler_params=_cparams(("arbitrary",)),
    )(*parts, dy, w)


def loss_fwd_bwd(x, fw, target):
    T, D = x.shape
    tm = min(T, 256)

    def body(x_ref, fw_ref, tg_ref, loss_ref, dx_ref, dfw_ref):
        @pl.when(pl.program_id(0) == 0)
        def _():
            loss_ref[...] = jnp.zeros_like(loss_ref)
            dfw_ref[...] = jnp.zeros_like(dfw_ref)

        xv = x_ref[...]
        w = fw_ref[...]
        err = _rms(xv, w) - tg_ref[...]
        row = jnp.sum(err * err, axis=-1, keepdims=True)
        loss_ref[...] += (0.5 / D) * jnp.sum(row, axis=0, keepdims=True)
        dx, dfw = _rms_bwd(xv, w, err * (1.0 / D))
        dx_ref[...] = dx
        dfw_ref[...] += dfw

    return pl.pallas_call(
        body, name="loss_fwd_bwd", grid=(T // tm,),
        in_specs=[pl.BlockSpec((tm, D), lambda t: (t, 0)), pl.BlockSpec((1, D), lambda t: (0, 0)),
                  pl.BlockSpec((tm, D), lambda t: (t, 0))],
        out_specs=[pl.BlockSpec((1, 1), lambda t: (0, 0)), pl.BlockSpec((tm, D), lambda t: (t, 0)),
                   pl.BlockSpec((1, D), lambda t: (0, 0))],
        out_shape=[jax.ShapeDtypeStruct((1, 1), F32), jax.ShapeDtypeStruct((T, D), F32),
                   jax.ShapeDtypeStruct((1, D), F32)],
        compiler_params=_cparams(("arbitrary",)),
    )(x, fw, target)


def _attn_masks(n, rows, blk):
    r = lax.broadcasted_iota(jnp.int32, (rows, 2 * blk), 0)
    jj = lax.broadcasted_iota(jnp.int32, (rows, 2 * blk), 1)
    dist = (r % blk) + blk - jj
    valid = (dist >= 0) & (dist < blk) & ((n > 0) | (jj >= blk))
    return dist.astype(F32), valid


def _attn_block(q, kcat, vcat, sink, slope, dist, valid):
    d = q.shape[-1]
    s = _dot_nt(q.astype(BF16), kcat.astype(BF16)) * (d ** -0.5)
    s = jnp.where(valid, s - slope * dist, -1e30)
    m = lax.stop_gradient(jnp.maximum(jnp.max(s, axis=-1, keepdims=True), sink))
    e = jnp.exp(s - m)
    p = e / (jnp.sum(e, axis=-1, keepdims=True) + jnp.exp(sink - m))
    return _dot(p.astype(BF16), vcat.astype(BF16))


ATTN_G = ATTN_HEADS // ATTN_KV_HEADS
ATTN_QW = ATTN_G * HEAD_DIM
ATTN_KCOL = Q_A // KV_A


def _attn_specs():
    blk = ATTN_BLOCK
    qs = pl.BlockSpec((blk, ATTN_QW), lambda h, n: (n, h))
    prev = lambda c: pl.BlockSpec((blk, KV_A), lambda h, n: (jnp.maximum(n - 1, 0), c))
    cur = lambda c: pl.BlockSpec((blk, KV_A), lambda h, n: (n, c))
    rowp = pl.BlockSpec((ATTN_G * blk, 1), lambda h, n: (h, 0))
    return qs, [prev(ATTN_KCOL), cur(ATTN_KCOL), prev(ATTN_KCOL + 1), cur(ATTN_KCOL + 1)], rowp


def _attn_operands(h, q_ref, kp_ref, kc_ref, vp_ref, vc_ref):
    d = HEAD_DIM
    q = jnp.concatenate([q_ref[:, g * d:(g + 1) * d] for g in range(ATTN_G)], axis=0)
    pick = lambda r: jnp.where(h == 0, r[:, :d], r[:, d:])
    kcat = jnp.concatenate([pick(kp_ref[...]), pick(kc_ref[...])], axis=0)
    vcat = jnp.concatenate([pick(vp_ref[...]), pick(vc_ref[...])], axis=0)
    return q, kcat, vcat


def attn_fwd(proj, sink_rows, slope_rows, ag=()):
    T = proj.shape[0]
    blk, d = ATTN_BLOCK, HEAD_DIM

    def body(q_ref, kp_ref, kc_ref, vp_ref, vc_ref, sink_ref, slope_ref, o_ref):
        h, n = pl.program_id(0), pl.program_id(1)
        dist, valid = _attn_masks(n, ATTN_G * blk, blk)
        q, kcat, vcat = _attn_operands(h, q_ref, kp_ref, kc_ref, vp_ref, vc_ref)
        o = _attn_block(q, kcat, vcat, sink_ref[...], slope_ref[...], dist, valid)
        for g in range(ATTN_G):
            o_ref[:, g * d:(g + 1) * d] = o[g * blk:(g + 1) * blk]

    qs, kv, rowp = _attn_specs()
    (out,), gathered, _ = _pcall(
        body, (proj, proj, proj, proj, proj, sink_rows, slope_rows), name="attn_fwd",
        grid=(ATTN_KV_HEADS, T // blk), in_specs=[qs] + kv + [rowp, rowp], out_specs=[qs],
        out_shape=[jax.ShapeDtypeStruct((T, Q_A), F32)], sem=("arbitrary", "arbitrary"), ag=ag)
    return out, gathered


def attn_bwd(proj, sink_rows, slope_rows, dmix, rs=()):
    T = proj.shape[0]
    blk, d = ATTN_BLOCK, HEAD_DIM

    def body(q_ref, kp_ref, kc_ref, vp_ref, vc_ref, sink_ref, slope_ref, do_ref, dq_ref, dkv_ref, dsink_ref):
        h, n = pl.program_id(0), pl.program_id(1)

        @pl.when((h == 0) & (n == 0))
        def _():
            dkv_ref[...] = jnp.zeros_like(dkv_ref)

        @pl.when(n == 0)
        def _():
            dsink_ref[...] = jnp.zeros_like(dsink_ref)

        dist, valid = _attn_masks(n, ATTN_G * blk, blk)
        q, kcat, vcat = _attn_operands(h, q_ref, kp_ref, kc_ref, vp_ref, vc_ref)
        do = jnp.concatenate([do_ref[:, g * d:(g + 1) * d] for g in range(ATTN_G)], axis=0)
        fn = functools.partial(_attn_block, slope=slope_ref[...], dist=dist, valid=valid)
        _, vjp = jax.vjp(fn, q, kcat, vcat, sink_ref[...])
        dq, dkcat, dvcat, dsink = vjp(do)
        for g in range(ATTN_G):
            dq_ref[:, g * d:(g + 1) * d] = dq[g * blk:(g + 1) * blk]
        dsink_ref[...] += dsink
        lane = lax.broadcasted_iota(jnp.int32, (2 * blk, 2 * KV_A), 1)
        mine = (lane % KV_A) // d == h
        both = jnp.where(mine, jnp.concatenate([dkcat, dkcat, dvcat, dvcat], axis=1), 0.0)

        @pl.when(n == 0)
        def _():
            dkv_ref[0:blk, :] += both[blk:]

        @pl.when(n > 0)
        def _():
            rows = pl.ds(pl.multiple_of((n - 1) * blk, blk), 2 * blk)
            dkv_ref[rows, :] += both

    qs, kv, rowp = _attn_specs()
    outs, _, slots = _pcall(
        body, (proj, proj, proj, proj, proj, sink_rows, slope_rows, dmix), name="attn_bwd",
        grid=(ATTN_KV_HEADS, T // blk), in_specs=[qs] + kv + [rowp, rowp, qs],
        out_specs=[qs, pl.BlockSpec((T, 2 * KV_A), lambda h, n: (0, 0)), rowp],
        out_shape=[jax.ShapeDtypeStruct((T, Q_A), F32), jax.ShapeDtypeStruct((T, 2 * KV_A), F32),
                   jax.ShapeDtypeStruct((ATTN_HEADS * blk, 1), F32)],
        sem=("arbitrary", "arbitrary"), rs=rs)
    return outs, slots


_NN = (((2,), (1,)), ((0,), (0,)))
_NT = (((2,), (2,)), ((0,), (0,)))
_TN = (((1,), (1,)), ((0,), (0,)))


def _bmm(a, b, dims):
    return lax.dot_general(a.astype(BF16), b.astype(BF16), dims, preferred_element_type=F32)


def _split(x, terms):
    out = []
    for _ in range(terms):
        t = x.astype(BF16)
        out.append(t)
        x = x - t.astype(F32)
    return out


def _fine_product(a, b, dims):
    (ah, al), (bh, bl) = _split(a, 2), _split(b, 2)
    dot = lambda x, y: lax.dot_general(x, y, dims, preferred_element_type=F32)
    return dot(ah, bh) + (dot(ah, bl) + dot(al, bh))


def _mask_product(mask, x, dims):
    mb = mask.astype(BF16)
    parts = [lax.dot_general(mb, t, dims, preferred_element_type=F32) for t in _split(x, 3)]
    return parts[0] + (parts[1] + parts[2])


@jax.custom_vjp
def _fine_nt(a, b):
    return _fine_product(a, b, _NT)


_fine_nt.defvjp(lambda a, b: (_fine_product(a, b, _NT), (a, b)),
                lambda res, ct: (_fine_product(ct, res[1], _NN), _fine_product(ct, res[0], _TN)))


@jax.custom_vjp
def _mask_nn(mask, x):
    return _mask_product(mask, x, _NN)


_mask_nn.defvjp(lambda mask, x: (_mask_product(mask, x, _NN), mask),
                lambda mask, ct: (jnp.zeros_like(mask), _mask_product(mask, ct, _TN)))


@jax.custom_vjp
def _unit_lower_inverse(low):
    n = low.shape[-1]
    eye = (lax.broadcasted_iota(jnp.int32, low.shape, 1) == lax.broadcasted_iota(jnp.int32, low.shape, 2)).astype(F32)
    tinv = eye - low
    p = low
    for _ in range(n.bit_length() - 2):
        p = _bmm(p, p, _NN)
        tinv = tinv + _bmm(tinv, p, _NN)
    return tinv


def _unit_lower_inverse_fwd(low):
    tinv = _unit_lower_inverse(low)
    return tinv, tinv


_unit_lower_inverse.defvjp(_unit_lower_inverse_fwd, lambda tinv, ct: (-_bmm(_bmm(tinv, ct, _TN), tinv, _NT),))


def _dn_chunk(qc, kc, vc, zc, braw, araw, alog, dtb, nw, S):
    H, C, D = qc.shape
    row = lax.broadcasted_iota(jnp.int32, (H, C, C), 1)
    col = lax.broadcasted_iota(jnp.int32, (H, C, C), 2)
    causal = row >= col
    strict = row > col
    eye = (row == col).astype(F32)

    q = qc * lax.rsqrt(jnp.sum(qc * qc, axis=-1, keepdims=True) + EPS) * (D ** -0.5)
    k = kc * lax.rsqrt(jnp.sum(kc * kc, axis=-1, keepdims=True) + EPS)
    beta = _sigmoid(braw)
    g = -jnp.exp(alog) * _softplus(araw + dtb)
    a_col = _mask_nn(causal.astype(F32), jnp.broadcast_to(g, (H, C, C)))
    a_row = _mask_nn(jnp.ones((H, C, C), F32), eye * a_col)
    decay = jnp.where(causal, jnp.exp(jnp.where(causal, a_col - a_row, 0.0)), 0.0)
    kb = k * beta
    tinv = _unit_lower_inverse(jnp.where(strict, _fine_nt(kb, k) * decay, 0.0))
    e_col = jnp.exp(a_col)
    u = _bmm(tinv, vc * beta, _NN)
    w = _bmm(tinv, kb * e_col, _NN)
    attn = _fine_nt(q, k) * decay
    gl = a_col[:, C - 1:C, :]
    k_dec = k * jnp.exp(gl - a_col)
    v_new = u - _bmm(w, S, _NN)
    o = _bmm(q * e_col, S, _NN) + _bmm(attn, v_new, _NN)
    s_new = S * jnp.exp(jnp.broadcast_to(gl, (H, D, D))) + _bmm(k_dec, v_new, _TN)
    on = o * lax.rsqrt(jnp.mean(o * o, axis=-1, keepdims=True) + EPS) * nw
    return on * (zc * _sigmoid(zc)), s_new


DN_ZCOLS = IN_COLS_PAD - OFF_Z
DN_ZBLK = OFF_Z // DN_ZCOLS


def _dn_heads(a, off):
    return jnp.stack([a[:, off + h * DN_D:off + (h + 1) * DN_D] for h in range(DN_HEADS)])


def _dn_gate_cols(zb, off):
    return jnp.stack([zb[:, off + h:off + h + 1] for h in range(DN_HEADS)])


DN_STEP_CHUNKS = 4


def _dn_operands(x_ref, zb_ref, rows):
    x, zb = x_ref[rows, :], zb_ref[rows, :]
    return (_dn_heads(x, 0), _dn_heads(x, V_B), _dn_heads(x, 2 * V_B), _dn_heads(zb, 0),
            _dn_gate_cols(zb, V_B), _dn_gate_cols(zb, V_B + DN_HEADS))


def dn_fwd(qkvc, proj, alog, dtb, nw, ag=()):
    T = qkvc.shape[0]
    H, C, D, G = DN_HEADS, DN_CHUNK, DN_D, DN_STEP_CHUNKS
    N = T // C

    def body(x_ref, zb_ref, alog_ref, dtb_ref, nw_ref, o_ref, sall_ref, s_ref):
        @pl.when(pl.program_id(0) == 0)
        def _():
            s_ref[...] = jnp.zeros_like(s_ref)

        s = s_ref[...]
        for c in range(G):
            rows = slice(c * C, (c + 1) * C)
            sall_ref[c] = s
            on, s = _dn_chunk(*_dn_operands(x_ref, zb_ref, rows), alog_ref[...], dtb_ref[...], nw_ref[...], s)
            for h in range(H):
                o_ref[rows, h * D:(h + 1) * D] = on[h]
        s_ref[...] = s

    par = pl.BlockSpec((H, 1, 1), lambda n: (0, 0, 0))
    outs, gathered, _ = _pcall(
        body, (qkvc, proj, alog, dtb, nw), name="dn_fwd", grid=(N // G,),
        in_specs=[pl.BlockSpec((G * C, QKV_B), lambda n: (n, 0)),
                  pl.BlockSpec((G * C, DN_ZCOLS), lambda n: (n, DN_ZBLK)),
                  par, par, pl.BlockSpec((1, 1, D), lambda n: (0, 0, 0))],
        out_specs=[pl.BlockSpec((G * C, V_B), lambda n: (n, 0)), pl.BlockSpec((G, H, D, D), lambda n: (n, 0, 0, 0))],
        out_shape=[jax.ShapeDtypeStruct((T, V_B), F32), jax.ShapeDtypeStruct((N, H, D, D), F32)],
        scratch_shapes=[pltpu.VMEM((H, D, D), F32)], sem=("arbitrary",), ag=ag)
    return outs, gathered


def dn_bwd(qkvc, proj, alog, dtb, nw, sall, dmix, rs=()):
    T = qkvc.shape[0]
    H, C, D, G = DN_HEADS, DN_CHUNK, DN_D, DN_STEP_CHUNKS
    N = T // C // G

    def body(x_ref, zb_ref, alog_ref, dtb_ref, nw_ref, sall_ref, do_ref,
             dx_ref, dzb_ref, dalog_ref, ddtb_ref, dnw_ref, ds_ref):
        @pl.when(pl.program_id(0) == 0)
        def _():
            ds_ref[...] = jnp.zeros_like(ds_ref)
            dalog_ref[...] = jnp.zeros_like(dalog_ref)
            ddtb_ref[...] = jnp.zeros_like(ddtb_ref)
            dnw_ref[...] = jnp.zeros_like(dnw_ref)

        ds = ds_ref[...]
        lane = lax.broadcasted_iota(jnp.int32, (C, LANES), 1)
        for c in reversed(range(G)):
            rows = slice(c * C, (c + 1) * C)
            args = (*_dn_operands(x_ref, zb_ref, rows), alog_ref[...], dtb_ref[...], nw_ref[...], sall_ref[c])
            _, vjp = jax.vjp(_dn_chunk, *args)
            dq, dk, dv, dz, db, da, dalog, ddtb, dnw, ds = vjp((_dn_heads(do_ref[rows, :], 0), ds))
            for h in range(H):
                dx_ref[rows, h * D:(h + 1) * D] = dq[h]
                dx_ref[rows, V_B + h * D:V_B + (h + 1) * D] = dk[h]
                dx_ref[rows, 2 * V_B + h * D:2 * V_B + (h + 1) * D] = dv[h]
                dzb_ref[rows, h * D:(h + 1) * D] = dz[h]
            tail = jnp.zeros((C, LANES), F32)
            for h in range(H):
                tail = tail + jnp.where(lane == h, jnp.broadcast_to(db[h], (C, LANES)), 0.0)
                tail = tail + jnp.where(lane == H + h, jnp.broadcast_to(da[h], (C, LANES)), 0.0)
            dzb_ref[rows, V_B:V_B + LANES] = tail
            dzb_ref[rows, V_B + LANES:] = jnp.zeros((C, DN_ZCOLS - V_B - LANES), F32)
            dalog_ref[...] += dalog
            ddtb_ref[...] += ddtb
            dnw_ref[...] += dnw
        ds_ref[...] = ds

    par = pl.BlockSpec((H, 1, 1), lambda i: (0, 0, 0))
    nws = pl.BlockSpec((1, 1, D), lambda i: (0, 0, 0))
    outs, _, slots = _pcall(
        body, (qkvc, proj, alog, dtb, nw, sall, dmix), name="dn_bwd", grid=(N,),
        in_specs=[pl.BlockSpec((G * C, QKV_B), lambda i: (N - 1 - i, 0)),
                  pl.BlockSpec((G * C, DN_ZCOLS), lambda i: (N - 1 - i, DN_ZBLK)), par, par, nws,
                  pl.BlockSpec((G, H, D, D), lambda i: (N - 1 - i, 0, 0, 0)),
                  pl.BlockSpec((G * C, V_B), lambda i: (N - 1 - i, 1))],
        out_specs=[pl.BlockSpec((G * C, QKV_B), lambda i: (N - 1 - i, 0)),
                   pl.BlockSpec((G * C, DN_ZCOLS), lambda i: (N - 1 - i, 0)), par, par, nws],
        out_shape=[jax.ShapeDtypeStruct((T, QKV_B), F32), jax.ShapeDtypeStruct((T, DN_ZCOLS), F32)]
        + [jax.ShapeDtypeStruct((H, 1, 1), F32)] * 2 + [jax.ShapeDtypeStruct((1, 1, D), F32)],
        scratch_shapes=[pltpu.VMEM((H, D, D), F32)], sem=("arbitrary",), rs=rs)
    return outs, slots


def _conv_taps(buf_ref, w, width, halo, tm):
    acc = None
    for kk, win in _windows(buf_ref, [halo - (width - 1) + kk for kk in range(width)], tm):
        term = w[kk:kk + 1, :] * win
        acc = term if acc is None else acc + term
    return acc


def _windows(ref, offsets, tm):
    for res in range(SUBLANES):
        ks = [k for k, o in enumerate(offsets) if o % SUBLANES == res]
        if not ks:
            continue
        lo = min(offsets[k] for k in ks)
        hi = max(offsets[k] for k in ks)
        shifted = ref[pl.ds(lo, tm + hi - lo), :]
        for k in ks:
            yield k, shifted[offsets[k] - lo:offsets[k] - lo + tm]


def _conv_taps_bwd(dbuf_ref, w, width, tm):
    acc = None
    for kk, win in _windows(dbuf_ref, [width - 1 - kk for kk in range(width)], tm):
        term = w[kk:kk + 1, :] * win
        acc = term if acc is None else acc + term
    return acc


def _conv_dw_acc(dw_ref, dout, buf_ref, width, halo, tm):
    for kk, win in _windows(buf_ref, [halo - (width - 1) + kk for kk in range(width)], tm):
        dw_ref[pl.ds(kk, 1), :] += jnp.sum(dout * win, axis=0, keepdims=True)


DNC_HALO = 8
DNC_COLS = 768


def dnconv_fwd(proj, w):
    T = proj.shape[0]
    tm = min(T, 256)
    hb = tm // DNC_HALO

    def body(x_ref, h_ref, w_ref, o_ref, buf_ref):
        i = pl.program_id(0)
        buf_ref[0:DNC_HALO, :] = jnp.where(i > 0, h_ref[...], 0.0)
        buf_ref[DNC_HALO:, :] = x_ref[...]
        acc = _conv_taps(buf_ref, w_ref[...], DN_CONV, DNC_HALO, tm)
        o_ref[...] = acc * _sigmoid(acc)

    return pl.pallas_call(
        body, name="dnconv_fwd", grid=(T // tm, 2),
        in_specs=[pl.BlockSpec((tm, DNC_COLS), lambda i, c: (i, 1 + c)),
                  pl.BlockSpec((DNC_HALO, DNC_COLS), lambda i, c: (jnp.maximum(i * hb - 1, 0), 1 + c)),
                  pl.BlockSpec((DN_CONV, DNC_COLS), lambda i, c: (0, c))],
        out_specs=pl.BlockSpec((tm, DNC_COLS), lambda i, c: (i, c)),
        out_shape=jax.ShapeDtypeStruct((T, QKV_B), F32),
        scratch_shapes=[pltpu.VMEM((DNC_HALO + tm, DNC_COLS), F32)],
        compiler_params=_cparams(("parallel", "parallel")),
    )(proj, proj, w)


def dnconv_bwd(proj, w, dout):
    T = proj.shape[0]
    tm = min(T, 256)
    nt = T // tm
    hb = tm // DNC_HALO

    def body(x_ref, h_ref, w_ref, do_ref, dx_ref, dw_ref, buf_ref, dbuf_ref):
        r = pl.program_id(1)
        i = nt - 1 - r

        @pl.when(r == 0)
        def _():
            dw_ref[...] = jnp.zeros_like(dw_ref)
            dbuf_ref[tm:, :] = jnp.zeros((DNC_HALO, DNC_COLS), F32)

        buf_ref[0:DNC_HALO, :] = jnp.where(i > 0, h_ref[...], 0.0)
        buf_ref[DNC_HALO:, :] = x_ref[...]
        wv = w_ref[...]
        acc = _conv_taps(buf_ref, wv, DN_CONV, DNC_HALO, tm)
        sg = _sigmoid(acc)
        dacc = do_ref[...] * (sg * (1.0 + acc * (1.0 - sg)))
        dbuf_ref[0:tm, :] = dacc
        dx_ref[...] = _conv_taps_bwd(dbuf_ref, wv, DN_CONV, tm)
        _conv_dw_acc(dw_ref, dacc, buf_ref, DN_CONV, DNC_HALO, tm)
        dbuf_ref[tm:, :] = dacc[0:DNC_HALO, :]

    return pl.pallas_call(
        body, name="dnconv_bwd", grid=(2, nt),
        in_specs=[pl.BlockSpec((tm, DNC_COLS), lambda c, r: (nt - 1 - r, 1 + c)),
                  pl.BlockSpec((DNC_HALO, DNC_COLS), lambda c, r: (jnp.maximum((nt - 1 - r) * hb - 1, 0), 1 + c)),
                  pl.BlockSpec((DN_CONV, DNC_COLS), lambda c, r: (0, c)),
                  pl.BlockSpec((tm, DNC_COLS), lambda c, r: (nt - 1 - r, c))],
        out_specs=[pl.BlockSpec((tm, DNC_COLS), lambda c, r: (nt - 1 - r, c)),
                   pl.BlockSpec((DN_CONV, DNC_COLS), lambda c, r: (0, c))],
        out_shape=[jax.ShapeDtypeStruct((T, QKV_B), F32), jax.ShapeDtypeStruct((DN_CONV, QKV_B), F32)],
        scratch_shapes=[pltpu.VMEM((DNC_HALO + tm, DNC_COLS), F32), pltpu.VMEM((tm + DNC_HALO, DNC_COLS), F32)],
        compiler_params=_cparams(("parallel", "arbitrary")),
    )(proj, proj, w, dout)


CV_HALO = 32


def _cv_post(cv, lnw, lnb):
    mu = jnp.mean(cv, axis=-1, keepdims=True)
    xc = cv - mu
    y = xc * lax.rsqrt(jnp.mean(xc * xc, axis=-1, keepdims=True) + EPS) * lnw + lnb
    return y * _sigmoid(y)


def cv_fwd(ab, w, bdw, lnw, lnb, ag=()):
    T = ab.shape[0]
    D = ab.shape[1] // 2
    tm = min(T, 256)
    hb = tm // CV_HALO

    def body(a_ref, b_ref, ah_ref, bh_ref, w_ref, bdw_ref, lnw_ref, lnb_ref, o_ref, cv_ref, buf_ref):
        i = pl.program_id(0)
        buf_ref[0:CV_HALO, :] = jnp.where(i > 0, ah_ref[...] * _sigmoid(bh_ref[...]), 0.0)
        buf_ref[CV_HALO:, :] = a_ref[...] * _sigmoid(b_ref[...])
        cv = _conv_taps(buf_ref, w_ref[...], CONV_WIDTH, CV_HALO, tm) + bdw_ref[...]
        cv_ref[...] = cv
        o_ref[...] = _cv_post(cv, lnw_ref[...], lnb_ref[...])

    halo = lambda c: pl.BlockSpec((CV_HALO, D), lambda i: (jnp.maximum(i * hb - 1, 0), c))
    vec = pl.BlockSpec((1, D), lambda i: (0, 0))
    tile = pl.BlockSpec((tm, D), lambda i: (i, 0))
    outs, gathered, _ = _pcall(
        body, (ab, ab, ab, ab, w, bdw, lnw, lnb), name="cv_fwd", grid=(T // tm,),
        in_specs=[tile, pl.BlockSpec((tm, D), lambda i: (i, 1)),
                  halo(0), halo(1), pl.BlockSpec((CONV_WIDTH, D), lambda i: (0, 0)), vec, vec, vec],
        out_specs=[tile, tile],
        out_shape=[jax.ShapeDtypeStruct((T, D), F32), jax.ShapeDtypeStruct((T, D), F32)],
        scratch_shapes=[pltpu.VMEM((CV_HALO + tm, D), F32)], sem=("arbitrary",), ag=ag)
    return outs, gathered


def cv_bwd(ab, cv, w, lnw, lnb, dout, rs=()):
    T = ab.shape[0]
    D = ab.shape[1] // 2
    tm = min(T, 256)
    nt = T // tm
    hb = tm // CV_HALO

    def body(a_ref, b_ref, ah_ref, bh_ref, cv_ref, w_ref, lnw_ref, lnb_ref, do_ref,
             da_ref, db_ref, dw_ref, dbdw_ref, dlnw_ref, dlnb_ref, buf_ref, dbuf_ref):
        r = pl.program_id(0)
        i = nt - 1 - r

        @pl.when(r == 0)
        def _():
            dw_ref[...] = jnp.zeros_like(dw_ref)
            dbdw_ref[...] = jnp.zeros_like(dbdw_ref)
            dlnw_ref[...] = jnp.zeros_like(dlnw_ref)
            dlnb_ref[...] = jnp.zeros_like(dlnb_ref)
            dbuf_ref[tm:, :] = jnp.zeros((CV_HALO, D), F32)

        a = a_ref[...]
        sb = _sigmoid(b_ref[...])
        buf_ref[0:CV_HALO, :] = jnp.where(i > 0, ah_ref[...] * _sigmoid(bh_ref[...]), 0.0)
        buf_ref[CV_HALO:, :] = a * sb
        wv = w_ref[...]
        _, vjp = jax.vjp(_cv_post, cv_ref[...], lnw_ref[...], lnb_ref[...])
        dcv, dlnw, dlnb = vjp(do_ref[...])
        dlnw_ref[...] += dlnw
        dlnb_ref[...] += dlnb
        dbdw_ref[...] += jnp.sum(dcv, axis=0, keepdims=True)
        dbuf_ref[0:tm, :] = dcv
        du = _conv_taps_bwd(dbuf_ref, wv, CONV_WIDTH, tm)
        _conv_dw_acc(dw_ref, dcv, buf_ref, CONV_WIDTH, CV_HALO, tm)
        dbuf_ref[tm:, :] = dcv[0:CV_HALO, :]
        da_ref[...] = du * sb
        db_ref[...] = du * a * sb * (1.0 - sb)

    tile = lambda c: pl.BlockSpec((tm, D), lambda r: (nt - 1 - r, c))
    halo = lambda c: pl.BlockSpec((CV_HALO, D), lambda r: (jnp.maximum((nt - 1 - r) * hb - 1, 0), c))
    vec = pl.BlockSpec((1, D), lambda r: (0, 0))
    wsp = pl.BlockSpec((CONV_WIDTH, D), lambda r: (0, 0))
    (da, db, dw, dbdw, dlnw, dlnb), _, slots = _pcall(
        body, (ab, ab, ab, ab, cv, w, lnw, lnb, dout), name="cv_bwd", grid=(nt,),
        in_specs=[tile(0), tile(1), halo(0), halo(1), tile(0), wsp, vec, vec, tile(0)],
        out_specs=[tile(0), tile(0), wsp, vec, vec, vec],
        out_shape=[jax.ShapeDtypeStruct((T, D), F32), jax.ShapeDtypeStruct((T, D), F32),
                   jax.ShapeDtypeStruct((CONV_WIDTH, D), F32)] + [jax.ShapeDtypeStruct((1, D), F32)] * 3,
        scratch_shapes=[pltpu.VMEM((CV_HALO + tm, D), F32), pltpu.VMEM((tm + CV_HALO, D), F32)],
        sem=("arbitrary",), rs=rs)
    return (jnp.concatenate([da, db], axis=1), dw, dbdw, dlnw, dlnb), slots


def adamw(w, m, v, slots, rs=()):
    L, R, C = w.shape
    ns = slots[0].shape[0]
    fits = lambda r, c: ns * r * c * 2 <= ADAM_SLOT_BLOCK
    tiles = [(R, C)] if fits(R, C) else []
    tiles += [(d, C) for d in range(16, R, 16) if R % d == 0 and fits(d, C)]
    tiles += [(R, d) for d in range(LANES, C, LANES) if C % d == 0 and fits(R, d)]
    tr, tc = max(tiles, key=lambda t: t[0] * t[1])
    c1 = 1.0 / (1.0 - ADAM_B1 ** ADAM_STEP)
    c2 = 1.0 / (1.0 - ADAM_B2 ** ADAM_STEP)

    def body(w_ref, m_ref, v_ref, *rest):
        s_refs = rest[:L]
        g_ref, d_ref, nm_ref, nv_ref = rest[L:]
        l = pl.program_id(0)
        for k in range(L):
            @pl.when(l == k)
            def _(s_ref=s_refs[k]):
                g = s_ref[0].astype(F32)
                for j in range(1, ns):
                    g = g + s_ref[j].astype(F32)
                nm = ADAM_B1 * m_ref[0] + (1.0 - ADAM_B1) * g
                nv = ADAM_B2 * v_ref[0] + (1.0 - ADAM_B2) * (g * g)
                g_ref[0] = g
                nm_ref[0] = nm
                nv_ref[0] = nv
                d_ref[0] = -ADAM_LR * ((nm * c1) / (jnp.sqrt(nv * c2) + ADAM_EPS) + ADAM_WD * w_ref[0])

    nc = C // tc
    blk = pl.BlockSpec((1, tr, tc), lambda l, i: (l, i // nc, i % nc))
    slot = lambda k: pl.BlockSpec((ns, tr, tc), lambda l, i: (0, jnp.where(l == k, i // nc, 0),
                                                                 jnp.where(l == k, i % nc, 0)))
    outs, _, landed = _pcall(
        body, (w, m, v, *slots), name="adamw", grid=(L, (R // tr) * nc),
        in_specs=[blk, blk, blk] + [slot(k) for k in range(L)],
        out_specs=[blk, blk, blk, blk],
        out_shape=[jax.ShapeDtypeStruct((L, R, C), F32)] * 4,
        sem=("arbitrary", "arbitrary"), rs=rs)
    return outs, landed


def _unshard(g, axis):
    g = jnp.moveaxis(g, 0, axis)
    s = g.shape
    return g.reshape(s[:axis] + (s[axis] * s[axis + 1],) + s[axis + 2:])


def _to_blocks(full, axis):
    s = full.shape
    g = full.reshape(s[:axis] + (N_DEV, s[axis] // N_DEV) + s[axis + 1:])
    return jnp.moveaxis(g, axis, 0)


SMALL = (("norm_w", 2), ("dn_conv_w", 2), ("conv_b_pw1", 1), ("conv_w_dw", 2), ("conv_b_dw", 1),
         ("conv_ln_w", 1), ("conv_ln_b", 1), ("conv_b_pw2", 1),
         ("attn_sinks", None), ("dn_a_log", None), ("dn_dt_bias", None), ("dn_norm_w", None), ("final_norm_w", None))
SMALL_AXIS = dict(SMALL)


def _pack(parts):
    flat = jnp.concatenate([p.reshape(-1) for p in parts])
    pad = (-flat.shape[0]) % LANES
    return jnp.pad(flat, (0, pad))


def _unpack(flat, shapes):
    out, off = [], 0
    for s in shapes:
        n = int(np.prod(s))
        out.append(flat[off:off + n].reshape(s))
        off += n
    return out


def kernel(x, norm_w, ffn_w_gate, ffn_w_up, ffn_w_down, mix_w_in, dn_conv_w, attn_sinks, dn_a_log, dn_dt_bias, dn_norm_w, mix_w_out, conv_w_pw1, conv_b_pw1, conv_w_dw, conv_b_dw, conv_ln_w, conv_ln_b, conv_w_pw2, conv_b_pw2, final_norm_w, loss_target, m_norm_w, m_ffn_w_gate, m_ffn_w_up, m_ffn_w_down, m_mix_w_in, m_dn_conv_w, m_attn_sinks, m_dn_a_log, m_dn_dt_bias, m_dn_norm_w, m_mix_w_out, m_conv_w_pw1, m_conv_b_pw1, m_conv_w_dw, m_conv_b_dw, m_conv_ln_w, m_conv_ln_b, m_conv_w_pw2, m_conv_b_pw2, m_final_norm_w, v_norm_w, v_ffn_w_gate, v_ffn_w_up, v_ffn_w_down, v_mix_w_in, v_dn_conv_w, v_attn_sinks, v_dn_a_log, v_dn_dt_bias, v_dn_norm_w, v_mix_w_out, v_conv_w_pw1, v_conv_b_pw1, v_conv_w_dw, v_conv_b_dw, v_conv_ln_w, v_conv_ln_b, v_conv_w_pw2, v_conv_b_pw2, v_final_norm_w):
    W = dict(norm_w=norm_w, ffn_w_gate=ffn_w_gate, ffn_w_up=ffn_w_up, ffn_w_down=ffn_w_down, mix_w_in=mix_w_in,
             dn_conv_w=dn_conv_w, attn_sinks=attn_sinks, dn_a_log=dn_a_log, dn_dt_bias=dn_dt_bias,
             dn_norm_w=dn_norm_w, mix_w_out=mix_w_out, conv_w_pw1=conv_w_pw1, conv_b_pw1=conv_b_pw1,
             conv_w_dw=conv_w_dw, conv_b_dw=conv_b_dw, conv_ln_w=conv_ln_w, conv_ln_b=conv_ln_b,
             conv_w_pw2=conv_w_pw2, conv_b_pw2=conv_b_pw2, final_norm_w=final_norm_w)
    M = dict(norm_w=m_norm_w, ffn_w_gate=m_ffn_w_gate, ffn_w_up=m_ffn_w_up, ffn_w_down=m_ffn_w_down,
             mix_w_in=m_mix_w_in, dn_conv_w=m_dn_conv_w, attn_sinks=m_attn_sinks, dn_a_log=m_dn_a_log,
             dn_dt_bias=m_dn_dt_bias, dn_norm_w=m_dn_norm_w, mix_w_out=m_mix_w_out, conv_w_pw1=m_conv_w_pw1,
             conv_b_pw1=m_conv_b_pw1, conv_w_dw=m_conv_w_dw, conv_b_dw=m_conv_b_dw, conv_ln_w=m_conv_ln_w,
             conv_ln_b=m_conv_ln_b, conv_w_pw2=m_conv_w_pw2, conv_b_pw2=m_conv_b_pw2, final_norm_w=m_final_norm_w)
    V = dict(norm_w=v_norm_w, ffn_w_gate=v_ffn_w_gate, ffn_w_up=v_ffn_w_up, ffn_w_down=v_ffn_w_down,
             mix_w_in=v_mix_w_in, dn_conv_w=v_dn_conv_w, attn_sinks=v_attn_sinks, dn_a_log=v_dn_a_log,
             dn_dt_bias=v_dn_dt_bias, dn_norm_w=v_dn_norm_w, mix_w_out=v_mix_w_out, conv_w_pw1=v_conv_w_pw1,
             conv_b_pw1=v_conv_b_pw1, conv_w_dw=v_conv_w_dw, conv_b_dw=v_conv_b_dw, conv_ln_w=v_conv_ln_w,
             conv_ln_b=v_conv_ln_b, conv_w_pw2=v_conv_w_pw2, conv_b_pw2=v_conv_b_pw2, final_norm_w=v_final_norm_w)

    T, D = x.shape[1], x.shape[2]
    xs = x[0]

    big = ("ffn_w_gate", "ffn_w_up", "ffn_w_down", "mix_w_in", "mix_w_out", "conv_w_pw1", "conv_w_pw2")
    shard3 = {k: W[k].reshape((-1,) + W[k].shape[-2:]) for k in big}
    shard_bf = {k: shard3[k].astype(BF16) for k in big}
    ffn_unit = lambda i: [("ffn_w_gate", i), ("ffn_w_up", i), ("ffn_w_down", i)]
    even_unit = lambda e: [("mix_w_in", e), ("mix_w_out", e)]
    odd_unit = lambda e: [("conv_w_pw1", e), ("conv_w_pw2", e)]
    have = {}

    def ag_jobs(units):
        return [(shard_bf[k], i) for k, i in units]

    def ag_done(units, gathered):
        have.update(zip(units, gathered))

    small_sharded = [(k, ax) for k, ax in SMALL if ax is not None]
    small_pack = _pack([W[k] for k, _ in small_sharded])[None, :]
    first_units = ffn_unit(0)
    gathered = exchange(ag_jobs(first_units) + [(small_pack, None)])
    ag_done(first_units, gathered[:-1])
    small_full = {}
    for (k, ax), parts in zip(small_sharded,
                              zip(*[_unpack(gathered[-1][s, 0], [W[k].shape for k, _ in small_sharded])
                                    for s in range(N_DEV)])):
        small_full[k] = _unshard(jnp.stack(parts), ax)
    nw_full = small_full["norm_w"]

    ffn_w = lambda i: [have[u] for u in ffn_unit(i)]
    w_in_of = lambda e: jnp.pad(_unshard(have[("mix_w_in", e)], 1), ((0, 0), (0, IN_COLS_PAD - IN_COLS)))
    w_out_of = lambda e: have[("mix_w_out", e)].reshape(D, D)
    w_pw1_of = lambda e: _unshard(have[("conv_w_pw1", e)], 1)
    w_pw2_of = lambda e: have[("conv_w_pw2", e)].reshape(D, D)
    fwd_order, needed = [], {}
    for l in range(DEPTH):
        mixer = [("A", l), ("E", l)] if l % 2 == 0 else [("O", l)]
        fwd_order += [("F", 2 * l)] + mixer + [("F", 2 * l + 1)]
        needed[("F", 2 * l)], needed[("F", 2 * l + 1)] = ffn_unit(2 * l), ffn_unit(2 * l + 1)
        needed[mixer[0]] = even_unit(l // 2) if l % 2 == 0 else odd_unit(l // 2)
    queue = [(u, pos) for pos, key in enumerate(fwd_order) for u in needed.get(key, []) if u not in first_units]
    unit_bytes = lambda u: N_DEV * shard_bf[u[0]][u[1]].size * 2
    fwd_carry, at = {}, 0
    for pos, key in enumerate(fwd_order):
        cap = FWD_CARRY_BYTES[key[0]]
        taken, used = [], 0
        while at < len(queue) and (queue[at][1] <= pos + 1 or used + unit_bytes(queue[at][0]) <= cap):
            taken.append(queue[at][0])
            used += unit_bytes(queue[at][0])
            at += 1
        fwd_carry[key] = taken
    zero_in = jnp.zeros((1, IN_COLS_PAD), F32)
    zero_d = jnp.zeros((1, D), F32)
    slope_rows = jnp.asarray(np.repeat(2.0 ** (-8.0 * np.arange(1, ATTN_HEADS + 1) / ATTN_HEADS), ATTN_BLOCK)
                             .astype(np.float32)[:, None])

    saved = []
    h = xs
    w_in, w_out, w_pw1, w_pw2 = {}, {}, {}, {}

    def ffn_forward(h, l, half):
        i = 2 * l + half
        units = fwd_carry.get(("F", i), [])
        h, gathered = ffn_fwd(h, nw_full[l, 2 * half][None], *ffn_w(i), ag=ag_jobs(units))
        ag_done(units, gathered)
        return h

    for l in range(DEPTH):
        e = l // 2
        st = {"x0": h}
        h = ffn_forward(h, l, 0)
        st["x1"] = h
        if l % 2 == 0:
            w_in[e], w_out[e] = w_in_of(e), w_out_of(e)
            proj = rmslin_fwd(h, nw_full[l, 1][None], w_in[e], zero_in)
            st["proj"] = proj
            st["qkvc"] = dnconv_fwd(proj, small_full["dn_conv_w"][e])
            st["sink_rows"] = jnp.repeat(attn_sinks[e], ATTN_BLOCK)[:, None]
            st["alog"] = dn_a_log[e].reshape(DN_HEADS, 1, 1)
            st["dtb"] = dn_dt_bias[e].reshape(DN_HEADS, 1, 1)
            st["dnw"] = dn_norm_w[e].reshape(1, 1, DN_D)
            units = fwd_carry[("A", l)]
            st["att"], gathered = attn_fwd(proj, st["sink_rows"], slope_rows, ag=ag_jobs(units))
            ag_done(units, gathered)
            units = fwd_carry[("E", l)]
            (st["og"], st["sall"]), gathered = dn_fwd(st["qkvc"], proj, st["alog"], st["dtb"], st["dnw"],
                                                      ag=ag_jobs(units))
            ag_done(units, gathered)
            h = lin_fwd(h, [st["att"], st["og"]], w_out[e], zero_d)
        else:
            units = fwd_carry[("O", l)]
            w_pw1[e], w_pw2[e] = w_pw1_of(e), w_pw2_of(e)
            st["ab"] = rmslin_fwd(h, nw_full[l, 1][None], w_pw1[e], small_full["conv_b_pw1"][e][None])
            (st["act"], st["cv"]), gathered = cv_fwd(st["ab"], small_full["conv_w_dw"][e], small_full["conv_b_dw"][e][None],
                                         small_full["conv_ln_w"][e][None], small_full["conv_ln_b"][e][None],
                                         ag=ag_jobs(units))
            ag_done(units, gathered)
            h = lin_fwd(h, [st["act"]], w_pw2[e], small_full["conv_b_pw2"][e][None])
        st["x2"] = h
        h = ffn_forward(h, l, 1)
        saved.append(st)

    loss_part, dh, dfinal = loss_fwd_bwd(h, final_norm_w[None], loss_target[0])
    loss = lax.psum(loss_part[0, 0], ("x", "y", "c"))

    d_norm = [[None] * 3 for _ in range(DEPTH)]
    d_small = {k: [None, None] for k in ("dn_conv_w", "conv_b_pw1", "conv_w_dw", "conv_b_dw", "conv_ln_w",
                                         "conv_ln_b", "conv_b_pw2", "attn_sinks", "dn_a_log", "dn_dt_bias",
                                         "dn_norm_w")}
    pending, slot = [], {}

    def take_jobs(cap=None, only=None):
        taken = [p for p in pending if p[1] == "swap"]
        used = 0
        for p in pending:
            if p[1] == "swap" or (only is not None and p[0][0] not in only):
                continue
            if cap is not None and used + p[2].size * 2 > cap:
                break
            taken.append(p)
            used += p[2].size * 2
        pending[:] = [p for p in pending if all(p is not t for t in taken)]
        return taken, [(kind, arr) for _, kind, arr in taken]

    def land(taken, results):
        swapped = [(unit, arr, res) for (unit, kind, arr), res in zip(taken, results) if kind == "swap"]
        slot.update({unit: res for (unit, kind, _), res in zip(taken, results) if kind != "swap"})
        if swapped:
            sums = pair_add([g for _, g, _ in swapped], [r for _, _, r in swapped])
            pending.extend((unit, "chips", h) for (unit, _, _), h in zip(swapped, sums))

    def ffn_backward(dh, l, half):
        i = 2 * l + half
        taken, jobs = take_jobs(BWD_CARRY_BYTES["F"])
        (dh, dg, du, dd, d_norm[l][2 * half]), results = ffn_bwd(
            st["x2" if half else "x0"], dh, nw_full[l, 2 * half][None], *ffn_w(i), rs=jobs)
        land(taken, results)
        pending.extend((u, "swap", g) for u, g in zip(ffn_unit(i), (dg, du, dd)))
        return dh

    for l in reversed(range(DEPTH)):
        e = l // 2
        st = saved[l]
        dh = ffn_backward(dh, l, 1)
        if l % 2 == 0:
            dmix, d_out, _ = lin_bwd([st["att"], st["og"]], dh, w_out[e])
            pending.append((("mix_w_out", e), "direct", d_out.reshape(N_DEV, D // N_DEV, D).astype(BF16)))
            taken, jobs = take_jobs(BWD_CARRY_BYTES["E"])
            (dqkvc, dzba, dalog, ddtb, ddnw), results = dn_bwd(
                st["qkvc"], st["proj"], st["alog"], st["dtb"], st["dnw"], st["sall"], dmix, rs=jobs)
            land(taken, results)
            taken, jobs = take_jobs(BWD_CARRY_BYTES["A"])
            (dqa, dkva, dsink), results = attn_bwd(st["proj"], st["sink_rows"], slope_rows, dmix, rs=jobs)
            land(taken, results)
            dqkv, d_small["dn_conv_w"][e] = dnconv_bwd(st["proj"], small_full["dn_conv_w"][e], dqkvc)
            dh, d_in, _, d_norm[l][1] = rmslin_bwd(st["x1"], dh, [dqa, dkva, dqkv, dzba], nw_full[l, 1][None],
                                                   w_in[e])
            pending.append((("mix_w_in", e), "direct", _to_blocks(d_in[:, :IN_COLS], 1)))
            d_small["attn_sinks"][e] = jnp.sum(dsink.reshape(ATTN_HEADS, ATTN_BLOCK), axis=1)
            d_small["dn_a_log"][e] = dalog.reshape(DN_HEADS)
            d_small["dn_dt_bias"][e] = ddtb.reshape(DN_HEADS)
            d_small["dn_norm_w"][e] = ddnw.reshape(DN_D)
        else:
            dact, d_pw2, d_small["conv_b_pw2"][e] = lin_bwd([st["act"]], dh, w_pw2[e])
            pending.append((("conv_w_pw2", e), "direct", d_pw2.reshape(N_DEV, D // N_DEV, D).astype(BF16)))
            taken, jobs = take_jobs(BWD_CARRY_BYTES["O"])
            (dab, d_small["conv_w_dw"][e], d_small["conv_b_dw"][e], d_small["conv_ln_w"][e],
             d_small["conv_ln_b"][e]), results = cv_bwd(
                st["ab"], st["cv"], small_full["conv_w_dw"][e],
                small_full["conv_ln_w"][e][None], small_full["conv_ln_b"][e][None], dact, rs=jobs)
            land(taken, results)
            dh, d_pw1, d_small["conv_b_pw1"][e], d_norm[l][1] = rmslin_bwd(
                st["x1"], dh, dab, nw_full[l, 1][None], w_pw1[e])
            pending.append((("conv_w_pw1", e), "direct", _to_blocks(d_pw1, 1)))
        dh = ffn_backward(dh, l, 0)
    grad_x = dh[None]

    full_small = {"norm_w": jnp.stack([jnp.concatenate(r, axis=0) for r in d_norm]),
                  "final_norm_w": dfinal[0]}
    for k, pair in d_small.items():
        full_small[k] = jnp.stack([p.reshape(W[k].shape[1:-1] + (-1,)) if SMALL_AXIS[k] is not None
                                   else p for p in pair])
    rows = []
    for s in range(N_DEV):
        parts = [_to_blocks(full_small[k], ax)[s] if ax is not None else full_small[k] for k, ax in SMALL]
        rows.append(_pack(parts))
    send_small = jnp.stack(rows)[:, None, :]
    pending.append((("small", 0), "direct", send_small))

    res = {}
    waiting = lambda k: [p for p in pending if p[0][0] == k]
    adam_order = sorted(big, key=lambda k: len(waiting(k))) + ["small"]
    for n, k in enumerate(adam_order[:-1]):
        nxt = next((kk for kk in adam_order[n + 1:] if waiting(kk)), "small")
        taken, jobs = take_jobs(only=(nxt, "small") if n == 1 else (nxt,))
        turned = k in ("ffn_w_gate", "ffn_w_up")
        view = lambda a: jnp.swapaxes(a.reshape(shard3[k].shape), 1, 2) if turned else a.reshape(shard3[k].shape)
        outs, results = adamw(view(W[k]), view(M[k]), view(V[k]),
                              [slot[(k, i)] for i in range(shard3[k].shape[0])], rs=jobs)
        land(taken, results)
        res[k] = [(jnp.swapaxes(o, 1, 2) if turned else o).reshape(W[k].shape) for o in outs]
    pk = lambda d: _pack([d[k] for k, _ in SMALL])[None, None, :]
    outs, _ = adamw(pk(W), pk(M), pk(V), [slot[("small", 0)]])
    shapes = [W[k].shape for k, _ in SMALL]
    unp = [_unpack(o[0, 0], shapes) for o in outs]
    for i, (k, _) in enumerate(SMALL):
        res[k] = [u[i] for u in unp]

    order = ("norm_w", "ffn_w_gate", "ffn_w_up", "ffn_w_down", "mix_w_in", "dn_conv_w", "attn_sinks", "dn_a_log",
             "dn_dt_bias", "dn_norm_w", "mix_w_out", "conv_w_pw1", "conv_b_pw1", "conv_w_dw", "conv_b_dw",
             "conv_ln_w", "conv_ln_b", "conv_w_pw2", "conv_b_pw2", "final_norm_w")
    return (loss, grad_x, *[res[k][0] for k in order], *[res[k][1] for k in order],
            *[res[k][2] for k in order], *[res[k][3] for k in order])
```

```python
import functools

import numpy as np
import jax
import jax.numpy as jnp
from jax import lax
from jax.experimental import pallas as pl
from jax.experimental.pallas import tpu as pltpu

F32 = jnp.float32
BF16 = jnp.bfloat16
EPS = 1e-6
N_DEV = 8
N_CHIP = 4
V7X_VMEM_LIMIT = 60 * 2**20
MESH = pl.DeviceIdType.MESH
LANES = 128
SUBLANES = 8

DEPTH = 4
D_MODEL = 1024
ATTN_HEADS, ATTN_KV_HEADS, HEAD_DIM, ATTN_BLOCK = 8, 2, 64, 128
DN_HEADS, DN_D, DN_CHUNK, DN_CONV = 8, 64, 64, 4
CONV_WIDTH = 31
Q_A, KV_A, QKV_B, V_B = 512, 128, 1536, 512
IN_COLS = 2832
IN_COLS_PAD = 3072
OFF_QKVB = Q_A + 2 * KV_A
OFF_Z = OFF_QKVB + QKV_B
OFF_BETA = OFF_Z + V_B
OFF_A = OFF_BETA + DN_HEADS

FWD_CARRY_BYTES = {"F": 12 * 2**20, "A": 6 * 2**20, "E": 18 * 2**20, "O": 12 * 2**20}
BWD_CARRY_BYTES = {"F": 11 * 2**20, "A": 6 * 2**20, "E": 13 * 2**20, "O": 10 * 2**20}

ADAM_SLOT_BLOCK = 3 * 2**19

ADAM_LR, ADAM_B1, ADAM_B2, ADAM_EPS, ADAM_WD, ADAM_STEP = 0.001, 0.9, 0.999, 1e-08, 0.01, 10


def _cparams(sem):
    return pltpu.CompilerParams(dimension_semantics=sem, vmem_limit_bytes=V7X_VMEM_LIMIT)


def _sigmoid(x):
    return 1.0 / (1.0 + jnp.exp(-x))


def _softplus(x):
    return jnp.maximum(x, 0.0) + jnp.log(1.0 + jnp.exp(-jnp.abs(x)))


def _dot(a, b):
    return jnp.dot(a, b, preferred_element_type=F32)


def _dot_nt(a, b):
    return lax.dot_general(a, b, (((1,), (1,)), ((), ())), preferred_element_type=F32)


def _dot_tn(a, b):
    return lax.dot_general(a, b, (((0,), (0,)), ((), ())), preferred_element_type=F32)


def _rms(x, w):
    return x * lax.rsqrt(jnp.mean(x * x, axis=-1, keepdims=True) + EPS) * w


def _rms_bwd(x, w, dxn):
    r = lax.rsqrt(jnp.mean(x * x, axis=-1, keepdims=True) + EPS)
    xh = x * r
    dxh = dxn * w
    dx = r * (dxh - xh * jnp.mean(dxh * xh, axis=-1, keepdims=True))
    return dx, jnp.sum(dxn * xh, axis=0, keepdims=True)


def _position():
    return lax.axis_index("x"), lax.axis_index("y"), lax.axis_index("c")


def _dev_index(px, py, pc):
    return 4 * px + 2 * py + pc


def _rcopy(src, dst, send_sem, recv_sem, to):
    return pltpu.make_async_remote_copy(src_ref=src, dst_ref=dst, send_sem=send_sem, recv_sem=recv_sem,
                                        device_id=to, device_id_type=MESH)


def _ag_start(srcs, outs, send, recv, local):
    x, y, c = _position()
    me = _dev_index(x, y, c)
    chips = [(1 - x, y), (x, 1 - y), (1 - x, 1 - y)]
    for a, (src, out) in enumerate(zip(srcs, outs)):
        pltpu.make_async_copy(src, out.at[me], local.at[a]).start()
        _rcopy(src, out.at[me], send.at[a, 0], recv.at[a, 0], (x, y, 1 - c)).start()
        for j, chip in enumerate(chips):
            _rcopy(src, out.at[me], send.at[a, 1 + j], recv.at[a, 1 + j], (*chip, c)).start()


def _ag_finish(srcs, outs, send, recv, local):
    x, y, c = _position()
    me = _dev_index(x, y, c)
    sibling = (x, y, 1 - c)
    chips = [(1 - x, y), (x, 1 - y), (1 - x, 1 - y)]
    for j, chip in enumerate(chips):
        for a, out in enumerate(outs):
            blk = out.at[_dev_index(*chip, c)]
            _rcopy(blk, blk, send.at[a, 1 + j], recv.at[a, 1 + j], (x, y, c)).wait_recv()
            _rcopy(blk, blk, send.at[a, 4 + j], recv.at[a, 4 + j], sibling).start()
    for a, (src, out) in enumerate(zip(srcs, outs)):
        blk = out.at[_dev_index(x, y, 1 - c)]
        _rcopy(blk, blk, send.at[a, 0], recv.at[a, 0], (x, y, c)).wait_recv()
        for j, chip in enumerate(chips):
            blk = out.at[_dev_index(*chip, 1 - c)]
            _rcopy(blk, blk, send.at[a, 4 + j], recv.at[a, 4 + j], (x, y, c)).wait_recv()
        for k in range(N_DEV - 1):
            _rcopy(out.at[me], out.at[me], send.at[a, k], recv.at[a, k], (x, y, c)).wait_send()
        pltpu.make_async_copy(src, out.at[me], local.at[a]).wait()


def _rs_peer(r):
    x, y, c = _position()
    return x ^ ((r >> 2) & 1), y ^ ((r >> 1) & 1), c ^ (r & 1)


def _rs_start(ins, outs, send, recv, local):
    me = _dev_index(*_position())
    for a, (src, out) in enumerate(zip(ins, outs)):
        pltpu.make_async_copy(src.at[me], out.at[me], local.at[a]).start()
        for r in range(1, N_DEV):
            p = _rs_peer(r)
            _rcopy(src.at[_dev_index(*p)], out.at[me], send.at[a, r - 1], recv.at[a, r - 1], p).start()


def _rs_finish(ins, outs, send, recv, local):
    pos = _position()
    me = _dev_index(*pos)
    for a, (src, out) in enumerate(zip(ins, outs)):
        for r in range(1, N_DEV):
            blk = out.at[_dev_index(*_rs_peer(r))]
            _rcopy(blk, blk, send.at[a, r - 1], recv.at[a, r - 1], pos).wait_recv()
        for r in range(1, N_DEV):
            _rcopy(src.at[me], out.at[me], send.at[a, r - 1], recv.at[a, r - 1], pos).wait_send()
        pltpu.make_async_copy(src.at[me], out.at[me], local.at[a]).wait()


def _sw_start(ins, outs, send, recv):
    x, y, c = _position()
    for a, (src, out) in enumerate(zip(ins, outs)):
        for q in range(N_CHIP):
            _rcopy(src.at[2 * q + (1 - c)], out.at[q], send.at[a, q], recv.at[a, q], (x, y, 1 - c)).start()


def _sw_finish(ins, outs, send, recv):
    pos = _position()
    for a, out in enumerate(outs):
        for q in range(N_CHIP):
            _rcopy(out.at[q], out.at[q], send.at[a, q], recv.at[a, q], pos).wait_recv()
        for q in range(N_CHIP):
            _rcopy(out.at[q], out.at[q], send.at[a, q], recv.at[a, q], pos).wait_send()


def _r4_peer(r):
    x, y, c = _position()
    return x ^ ((r >> 1) & 1), y ^ (r & 1), c


def _r4_start(ins, outs, send, recv, local):
    x, y, c = _position()
    mine = 2 * x + y
    for a, (src, out) in enumerate(zip(ins, outs)):
        pltpu.make_async_copy(src.at[mine], out.at[mine], local.at[a]).start()
        for r in range(1, N_CHIP):
            px, py, pc = _r4_peer(r)
            _rcopy(src.at[2 * px + py], out.at[mine], send.at[a, r - 1], recv.at[a, r - 1], (px, py, pc)).start()


def _r4_finish(ins, outs, send, recv, local):
    x, y, c = _position()
    mine = 2 * x + y
    for a, (src, out) in enumerate(zip(ins, outs)):
        for r in range(1, N_CHIP):
            px, py, _ = _r4_peer(r)
            blk = out.at[2 * px + py]
            _rcopy(blk, blk, send.at[a, r - 1], recv.at[a, r - 1], (x, y, c)).wait_recv()
        for r in range(1, N_CHIP):
            _rcopy(src.at[mine], out.at[mine], send.at[a, r - 1], recv.at[a, r - 1], (x, y, c)).wait_send()
        pltpu.make_async_copy(src.at[mine], out.at[mine], local.at[a]).wait()


_RS_KINDS = {
    "direct": (_rs_start, _rs_finish, lambda n: [(n, N_DEV - 1), (n, N_DEV - 1), (n,)], lambda s: s),
    "swap": (_sw_start, _sw_finish, lambda n: [(n, N_CHIP), (n, N_CHIP)], lambda s: (N_CHIP,) + s[1:]),
    "chips": (_r4_start, _r4_finish, lambda n: [(n, N_CHIP - 1), (n, N_CHIP - 1), (n,)], lambda s: s),
}


def _pcall(body, args, *, name, grid, in_specs, out_specs, out_shape, sem, scratch_shapes=(), ag=(), rs=()):
    na, nr = len(ag), len(rs)
    if na + nr == 0:
        outs = pl.pallas_call(body, name=name, grid=grid, in_specs=in_specs, out_specs=out_specs,
                              out_shape=out_shape, scratch_shapes=list(scratch_shapes),
                              compiler_params=_cparams(sem))(*args)
        return list(outs), [], []
    n_in, n_out, n_scr = len(in_specs), len(out_specs), len(scratch_shapes)
    ag_idx = [i for _, i in ag]
    groups = [(k, [i for i, (kk, _) in enumerate(rs) if kk == k]) for k in _RS_KINDS]
    groups = [(k, idx) for k, idx in groups if idx]
    sem_counts = ([3] if na else []) + [len(_RS_KINDS[k][2](1)) for k, _ in groups]

    def wrapped(*refs):
        cin, refs = refs[:n_in], refs[n_in:]
        ag_in, refs = refs[:na], refs[na:]
        rs_in, refs = refs[:nr], refs[nr:]
        cout, refs = refs[:n_out], refs[n_out:]
        ag_out, refs = refs[:na], refs[na:]
        rs_out, refs = refs[:nr], refs[nr:]
        cscr, sems = refs[:n_scr], list(refs[n_scr:])
        sem_sets = [[sems.pop(0) for _ in range(n)] for n in sem_counts]
        ag_sems = sem_sets.pop(0) if na else None
        ag_src = [r if i is None else r.at[i] for r, i in zip(ag_in, ag_idx)]
        ids = [pl.program_id(d) for d in range(len(grid))]
        first = functools.reduce(jnp.logical_and, [i == 0 for i in ids])
        last = functools.reduce(jnp.logical_and, [i == g - 1 for i, g in zip(ids, grid)])

        def run(phase):
            if na:
                (_ag_start, _ag_finish)[phase](ag_src, ag_out, *ag_sems)
            for (k, idx), ss in zip(groups, sem_sets):
                _RS_KINDS[k][phase]([rs_in[i] for i in idx], [rs_out[i] for i in idx], *ss)

        @pl.when(first)
        def _():
            run(0)

        body(*cin, *cout, *cscr)

        @pl.when(last)
        def _():
            run(1)

    hbm = pl.BlockSpec(memory_space=pl.ANY)
    sem_shapes = [pltpu.SemaphoreType.DMA(s) for s in ([(na, N_DEV - 1), (na, N_DEV - 1), (na,)] if na else [])]
    for k, idx in groups:
        sem_shapes += [pltpu.SemaphoreType.DMA(s) for s in _RS_KINDS[k][2](len(idx))]
    outs = pl.pallas_call(
        wrapped, name=name, grid=grid,
        in_specs=list(in_specs) + [hbm] * (na + nr),
        out_specs=list(out_specs) + [hbm] * (na + nr),
        out_shape=list(out_shape)
        + [jax.ShapeDtypeStruct((N_DEV,) + a.shape[-2:], a.dtype) for a, _ in ag]
        + [jax.ShapeDtypeStruct(_RS_KINDS[k][3](b.shape), b.dtype) for k, b in rs],
        scratch_shapes=list(scratch_shapes) + sem_shapes,
        compiler_params=_cparams(sem),
    )(*args, *[a for a, _ in ag], *[b for _, b in rs])
    return list(outs[:n_out]), list(outs[n_out:n_out + na]), list(outs[n_out + na:])


def exchange(ag):
    def body(o_ref):
        o_ref[...] = jnp.zeros_like(o_ref)

    _, gathered, _ = _pcall(body, (), name="exchange", grid=(1,), in_specs=[],
                            out_specs=[pl.BlockSpec((8, LANES), lambda i: (0, 0))],
                            out_shape=[jax.ShapeDtypeStruct((8, LANES), F32)], sem=("arbitrary",), ag=ag)
    return gathered


def pair_add(blocks, received):
    n = len(blocks)

    def body(core_ref, *refs):
        for g_ref, p_ref, o_ref in zip(refs[:n], refs[n:2 * n], refs[2 * n:]):
            o_ref[0] = (g_ref[0, 0].astype(F32) + p_ref[0].astype(F32)).astype(BF16)

    halves = 2
    g_specs = [pl.BlockSpec((1, 1, b.shape[1] // halves, b.shape[2]), lambda q, r, core: (q, core[0], r, 0))
               for b in blocks]
    p_specs = [pl.BlockSpec((1, b.shape[1] // halves, b.shape[2]), lambda q, r, core: (q, r, 0)) for b in blocks]
    return pl.pallas_call(
        body, name="pair_add",
        grid_spec=pltpu.PrefetchScalarGridSpec(num_scalar_prefetch=1, grid=(N_CHIP, halves),
                                               in_specs=g_specs + p_specs, out_specs=p_specs),
        out_shape=[jax.ShapeDtypeStruct(p.shape, BF16) for p in received],
        compiler_params=_cparams(("parallel", "parallel")),
    )(lax.axis_index("c").astype(jnp.int32)[None], *[b.reshape((N_CHIP, 2) + b.shape[1:]) for b in blocks], *received)


FFN_PAIR = 2


def _pair_cols(w_ref):
    return jnp.concatenate([w_ref[p] for p in range(FFN_PAIR)], axis=1)


def ffn_fwd(x, nw, wg, wu, wd, ag=()):
    T, D = x.shape
    F = wg.shape[2]
    P = FFN_PAIR
    J = wg.shape[0] // P
    tm = min(T, 1024)

    def body(x_ref, nw_ref, wg_ref, wu_ref, wd_ref, o_ref, xn_ref, acc_ref):
        j = pl.program_id(1)

        @pl.when(j == 0)
        def _():
            xn_ref[...] = _rms(x_ref[...], nw_ref[...]).astype(BF16)
            acc_ref[...] = jnp.zeros_like(acc_ref)

        xn = xn_ref[...]
        g = _dot(xn, _pair_cols(wg_ref))
        u = _dot(xn, _pair_cols(wu_ref))
        h = (g * _sigmoid(g) * u).astype(BF16)
        acc_ref[...] += _dot(h, wd_ref[...].reshape(P * F, D))

        @pl.when(j == J - 1)
        def _():
            o_ref[...] = x_ref[...] + 0.5 * acc_ref[...]

    (out,), gathered, _ = _pcall(
        body, (x, nw, wg, wu, wd), name="ffn_fwd", grid=(T // tm, J),
        in_specs=[pl.BlockSpec((tm, D), lambda t, j: (t, 0)),
                  pl.BlockSpec((1, D), lambda t, j: (0, 0)),
                  pl.BlockSpec((P, D, F), lambda t, j: (j, 0, 0)),
                  pl.BlockSpec((P, D, F), lambda t, j: (j, 0, 0)),
                  pl.BlockSpec((P, F, D), lambda t, j: (j, 0, 0))],
        out_specs=[pl.BlockSpec((tm, D), lambda t, j: (t, 0))],
        out_shape=[jax.ShapeDtypeStruct((T, D), F32)],
        scratch_shapes=[pltpu.VMEM((tm, D), BF16), pltpu.VMEM((tm, D), F32)],
        sem=("arbitrary", "arbitrary"), ag=ag)
    return out, gathered


def ffn_bwd(x, dy, nw, wg, wu, wd, rs=()):
    T, D = x.shape
    F = wg.shape[2]
    P = FFN_PAIR
    J = wg.shape[0] // P
    tm = min(T, 256)
    nt = T // tm

    def body(x_ref, dy_ref, nw_ref, wg_ref, wu_ref, wd_ref,
             dx_ref, dwg_ref, dwu_ref, dwd_ref, dnw_ref,
             xn_ref, dyh_ref, dxn_ref, awg_ref, awu_ref, awd_ref):
        j = pl.program_id(0)
        t = pl.program_id(1)
        rows = pl.ds(pl.multiple_of(t * tm, tm), tm)

        @pl.when(j == 0)
        def _():
            xn_ref[rows, :] = _rms(x_ref[...], nw_ref[...]).astype(BF16)
            dyh_ref[rows, :] = (0.5 * dy_ref[...]).astype(BF16)
            dxn_ref[rows, :] = jnp.zeros((tm, D), F32)

        @pl.when((j == 0) & (t == 0))
        def _():
            dnw_ref[...] = jnp.zeros_like(dnw_ref)

        @pl.when(t == 0)
        def _():
            awg_ref[...] = jnp.zeros_like(awg_ref)
            awu_ref[...] = jnp.zeros_like(awu_ref)
            awd_ref[...] = jnp.zeros_like(awd_ref)

        xn = xn_ref[rows, :]
        dyh = dyh_ref[rows, :]
        wg2, wu2 = _pair_cols(wg_ref), _pair_cols(wu_ref)
        g = _dot(xn, wg2)
        u = _dot(xn, wu2)
        sg = _sigmoid(g)
        s = g * sg
        h = (s * u).astype(BF16)
        dh = _dot_nt(dyh, wd_ref[...].reshape(P * F, D))
        du = (dh * s).astype(BF16)
        dg = (dh * u * (sg * (1.0 + g * (1.0 - sg)))).astype(BF16)
        awd_ref[...] += _dot_tn(h, dyh)
        awg_ref[...] += _dot_tn(dg, xn)
        awu_ref[...] += _dot_tn(du, xn)
        dxn_ref[rows, :] += _dot_nt(dg, wg2) + _dot_nt(du, wu2)

        @pl.when(t == nt - 1)
        def _():
            dwg_ref[...] = awg_ref[...].astype(BF16).reshape(P, F, D)
            dwu_ref[...] = awu_ref[...].astype(BF16).reshape(P, F, D)
            dwd_ref[...] = awd_ref[...].astype(BF16).reshape(P, F, D)

        @pl.when(j == J - 1)
        def _():
            dx, dnw = _rms_bwd(x_ref[...], nw_ref[...], dxn_ref[rows, :])
            dx_ref[...] = dy_ref[...] + dx
            dnw_ref[...] += dnw

    ends = lambda j, t: (jnp.where((j == 0) | (j == J - 1), t, 0), 0)
    last = lambda j, t: (jnp.where(j == J - 1, t, 0), 0)
    outs, _, slots = _pcall(
        body, (x, dy, nw, wg, wu, wd), name="ffn_bwd", grid=(J, nt),
        in_specs=[pl.BlockSpec((tm, D), ends), pl.BlockSpec((tm, D), ends),
                  pl.BlockSpec((1, D), lambda j, t: (0, 0)),
                  pl.BlockSpec((P, D, F), lambda j, t: (j, 0, 0)),
                  pl.BlockSpec((P, D, F), lambda j, t: (j, 0, 0)),
                  pl.BlockSpec((P, F, D), lambda j, t: (j, 0, 0))],
        out_specs=[pl.BlockSpec((tm, D), last),
                   pl.BlockSpec((P, F, D), lambda j, t: (j, 0, 0)),
                   pl.BlockSpec((P, F, D), lambda j, t: (j, 0, 0)),
                   pl.BlockSpec((P, F, D), lambda j, t: (j, 0, 0)),
                   pl.BlockSpec((1, D), lambda j, t: (0, 0))],
        out_shape=[jax.ShapeDtypeStruct((T, D), F32)] + [jax.ShapeDtypeStruct((P * J, F, D), BF16)] * 3
        + [jax.ShapeDtypeStruct((1, D), F32)],
        scratch_shapes=[pltpu.VMEM((T, D), BF16), pltpu.VMEM((T, D), BF16), pltpu.VMEM((T, D), F32)]
        + [pltpu.VMEM((P * F, D), F32)] * 3,
        sem=("arbitrary", "arbitrary"), rs=rs)
    return outs, slots


def rmslin_fwd(x, nw, w, b):
    T, D = x.shape
    N = w.shape[1]
    tm = min(T, 256)

    def body(x_ref, nw_ref, w_ref, b_ref, o_ref):
        xn = _rms(x_ref[...], nw_ref[...]).astype(BF16)
        o_ref[...] = _dot(xn, w_ref[...]) + b_ref[...]

    return pl.pallas_call(
        body, name="rmslin_fwd", grid=(T // tm,),
        in_specs=[pl.BlockSpec((tm, D), lambda t: (t, 0)), pl.BlockSpec((1, D), lambda t: (0, 0)),
                  pl.BlockSpec((D, N), lambda t: (0, 0)), pl.BlockSpec((1, N), lambda t: (0, 0))],
        out_specs=pl.BlockSpec((tm, N), lambda t: (t, 0)),
        out_shape=jax.ShapeDtypeStruct((T, N), F32),
        compiler_params=_cparams(("parallel",)),
    )(x, nw, w, b)


def rmslin_bwd(x, dres, dproj, nw, w):
    T, D = x.shape
    N = w.shape[1]
    pieces = list(dproj) if isinstance(dproj, (list, tuple)) else [dproj]
    nb = DN_ZCOLS if len(pieces) > 1 else 1024
    nc = N // nb
    tm = min(T, 256)
    nt = T // tm
    n_p = len(pieces)

    def body(x_ref, dres_ref, *refs):
        p_refs, (nw_ref, w_ref, dx_ref, dw_ref, db_ref, dnw_ref, xn_ref, dxn_ref, acc_ref) = refs[:n_p], refs[n_p:]
        c = pl.program_id(0)
        t = pl.program_id(1)
        rows = pl.ds(pl.multiple_of(t * tm, tm), tm)

        @pl.when(c == 0)
        def _():
            xn_ref[rows, :] = _rms(x_ref[...], nw_ref[...]).astype(BF16)
            dxn_ref[rows, :] = jnp.zeros((tm, D), F32)

        @pl.when((c == 0) & (t == 0))
        def _():
            dnw_ref[...] = jnp.zeros_like(dnw_ref)

        @pl.when(t == 0)
        def _():
            acc_ref[...] = jnp.zeros_like(acc_ref)
            db_ref[...] = jnp.zeros_like(db_ref)

        if n_p == 1:
            dpf = p_refs[0][...]
        else:
            dq_ref, dkv_ref, dqkv_ref, dz_ref = p_refs
            dpf = jnp.where(c == 0, jnp.concatenate([dq_ref[...], dkv_ref[...]], axis=1),
                            jnp.where(c == nc - 1, dz_ref[...], dqkv_ref[...]))
        dp = dpf.astype(BF16)
        acc_ref[...] += _dot_tn(xn_ref[rows, :], dp)
        db_ref[...] += jnp.sum(dpf, axis=0, keepdims=True)
        dxn_ref[rows, :] += _dot_nt(dp, w_ref[...])

        @pl.when(t == nt - 1)
        def _():
            dw_ref[...] = acc_ref[...].astype(BF16)

        @pl.when(c == nc - 1)
        def _():
            dx, dnw = _rms_bwd(x_ref[...], nw_ref[...], dxn_ref[rows, :])
            dx_ref[...] = dres_ref[...] + dx
            dnw_ref[...] += dnw

    ends = lambda c, t: (jnp.where((c == 0) | (c == nc - 1), t, 0), 0)
    last = lambda c, t: (jnp.where(c == nc - 1, t, 0), 0)
    first = lambda c, t: (jnp.where(c == 0, t, 0), 0)
    if n_p == 1:
        p_specs = [pl.BlockSpec((tm, nb), lambda c, t: (t, c))]
    else:
        p_specs = [pl.BlockSpec((tm, Q_A), first), pl.BlockSpec((tm, 2 * KV_A), first),
                   pl.BlockSpec((tm, nb), lambda c, t: (jnp.where((c > 0) & (c < nc - 1), t, 0),
                                                        jnp.clip(c - 1, 0, 1))),
                   pl.BlockSpec((tm, nb), last)]
    return pl.pallas_call(
        body, name="rmslin_bwd", grid=(nc, nt),
        in_specs=[pl.BlockSpec((tm, D), ends), pl.BlockSpec((tm, D), last)] + p_specs
        + [pl.BlockSpec((1, D), lambda c, t: (0, 0)), pl.BlockSpec((D, nb), lambda c, t: (0, c))],
        out_specs=[pl.BlockSpec((tm, D), last),
                   pl.BlockSpec((D, nb), lambda c, t: (0, c)),
                   pl.BlockSpec((1, nb), lambda c, t: (0, c)),
                   pl.BlockSpec((1, D), lambda c, t: (0, 0))],
        out_shape=[jax.ShapeDtypeStruct((T, D), F32), jax.ShapeDtypeStruct((D, N), BF16),
                   jax.ShapeDtypeStruct((1, N), F32), jax.ShapeDtypeStruct((1, D), F32)],
        scratch_shapes=[pltpu.VMEM((T, D), BF16), pltpu.VMEM((T, D), F32), pltpu.VMEM((D, nb), F32)],
        compiler_params=_cparams(("arbitrary", "arbitrary")),
    )(x, dres, *pieces, nw, w)


def lin_fwd(res, parts, w, b):
    T = res.shape[0]
    K, N = w.shape
    tm = min(T, 512)
    n = len(parts)
    offs = [sum(p.shape[1] for p in parts[:i]) for i in range(n + 1)]

    def body(res_ref, *refs):
        a_refs, (w_ref, b_ref, o_ref) = refs[:n], refs[n:]
        acc = res_ref[...] + b_ref[...]
        for i, a_ref in enumerate(a_refs):
            acc = acc + _dot(a_ref[...].astype(BF16), w_ref[offs[i]:offs[i + 1], :])
        o_ref[...] = acc

    return pl.pallas_call(
        body, name="lin_fwd", grid=(T // tm,),
        in_specs=[pl.BlockSpec((tm, N), lambda t: (t, 0))]
        + [pl.BlockSpec((tm, p.shape[1]), lambda t: (t, 0)) for p in parts]
        + [pl.BlockSpec((K, N), lambda t: (0, 0)), pl.BlockSpec((1, N), lambda t: (0, 0))],
        out_specs=pl.BlockSpec((tm, N), lambda t: (t, 0)),
        out_shape=jax.ShapeDtypeStruct((T, N), F32),
        compiler_params=_cparams(("parallel",)),
    )(res, *parts, w, b)


def lin_bwd(parts, dy, w):
    T = dy.shape[0]
    K, N = w.shape
    tm = min(T, 256)
    nt = T // tm
    n = len(parts)
    offs = [sum(p.shape[1] for p in parts[:i]) for i in range(n + 1)]

    def body(*refs):
        a_refs, (dy_ref, w_ref, da_ref, dw_ref, db_ref, acc_ref) = refs[:n], refs[n:]

        @pl.when(pl.program_id(0) == 0)
        def _():
            acc_ref[...] = jnp.zeros_like(acc_ref)
            db_ref[...] = jnp.zeros_like(db_ref)

        dyf = dy_ref[...]
        dyb = dyf.astype(BF16)
        da_ref[...] = _dot_nt(dyb, w_ref[...])
        for i, a_ref in enumerate(a_refs):
            acc_ref[offs[i]:offs[i + 1], :] += _dot_tn(a_ref[...].astype(BF16), dyb)
        db_ref[...] += jnp.sum(dyf, axis=0, keepdims=True)

        @pl.when(pl.program_id(0) == nt - 1)
        def _():
            dw_ref[...] = acc_ref[...].astype(BF16)

    return pl.pallas_call(
        body, name="lin_bwd", grid=(nt,),
        in_specs=[pl.BlockSpec((tm, p.shape[1]), lambda t: (t, 0)) for p in parts]
        + [pl.BlockSpec((tm, N), lambda t: (t, 0)), pl.BlockSpec((K, N), lambda t: (0, 0))],
        out_specs=[pl.BlockSpec((tm, K), lambda t: (t, 0)), pl.BlockSpec((K, N), lambda t: (0, 0)),
                   pl.BlockSpec((1, N), lambda t: (0, 0))],
        out_shape=[jax.ShapeDtypeStruct((T, K), F32), jax.ShapeDtypeStruct((K, N), BF16),
                   jax.ShapeDtypeStruct((1, N), F32)],
        scratch_shapes=[pltpu.VMEM((K, N), F32)],
        compiler_params=_cparams(("arbitrary",)),
    )(*parts, dy, w)


def loss_fwd_bwd(x, fw, target):
    T, D = x.shape
    tm = min(T, 256)

    def body(x_ref, fw_ref, tg_ref, loss_ref, dx_ref, dfw_ref):
        @pl.when(pl.program_id(0) == 0)
        def _():
            loss_ref[...] = jnp.zeros_like(loss_ref)
            dfw_ref[...] = jnp.zeros_like(dfw_ref)

        xv = x_ref[...]
        w = fw_ref[...]
        err = _rms(xv, w) - tg_ref[...]
        row = jnp.sum(err * err, axis=-1, keepdims=True)
        loss_ref[...] += (0.5 / D) * jnp.sum(row, axis=0, keepdims=True)
        dx, dfw = _rms_bwd(xv, w, err * (1.0 / D))
        dx_ref[...] = dx
        dfw_ref[...] += dfw

    return pl.pallas_call(
        body, name="loss_fwd_bwd", grid=(T // tm,),
        in_specs=[pl.BlockSpec((tm, D), lambda t: (t, 0)), pl.BlockSpec((1, D), lambda t: (0, 0)),
                  pl.BlockSpec((tm, D), lambda t: (t, 0))],
        out_specs=[pl.BlockSpec((1, 1), lambda t: (0, 0)), pl.BlockSpec((tm, D), lambda t: (t, 0)),
                   pl.BlockSpec((1, D), lambda t: (0, 0))],
        out_shape=[jax.ShapeDtypeStruct((1, 1), F32), jax.ShapeDtypeStruct((T, D), F32),
                   jax.ShapeDtypeStruct((1, D), F32)],
        compiler_params=_cparams(("arbitrary",)),
    )(x, fw, target)


def _attn_masks(n, rows, blk):
    r = lax.broadcasted_iota(jnp.int32, (rows, 2 * blk), 0)
    jj = lax.broadcasted_iota(jnp.int32, (rows, 2 * blk), 1)
    dist = (r % blk) + blk - jj
    valid = (dist >= 0) & (dist < blk) & ((n > 0) | (jj >= blk))
    return dist.astype(F32), valid


def _attn_block(q, kcat, vcat, sink, slope, dist, valid):
    d = q.shape[-1]
    s = _dot_nt(q.astype(BF16), kcat.astype(BF16)) * (d ** -0.5)
    s = jnp.where(valid, s - slope * dist, -1e30)
    m = lax.stop_gradient(jnp.maximum(jnp.max(s, axis=-1, keepdims=True), sink))
    e = jnp.exp(s - m)
    p = e / (jnp.sum(e, axis=-1, keepdims=True) + jnp.exp(sink - m))
    return _dot(p.astype(BF16), vcat.astype(BF16))


ATTN_G = ATTN_HEADS // ATTN_KV_HEADS
ATTN_QW = ATTN_G * HEAD_DIM
ATTN_KCOL = Q_A // KV_A


def _attn_specs():
    blk = ATTN_BLOCK
    qs = pl.BlockSpec((blk, ATTN_QW), lambda h, n: (n, h))
    prev = lambda c: pl.BlockSpec((blk, KV_A), lambda h, n: (jnp.maximum(n - 1, 0), c))
    cur = lambda c: pl.BlockSpec((blk, KV_A), lambda h, n: (n, c))
    rowp = pl.BlockSpec((ATTN_G * blk, 1), lambda h, n: (h, 0))
    return qs, [prev(ATTN_KCOL), cur(ATTN_KCOL), prev(ATTN_KCOL + 1), cur(ATTN_KCOL + 1)], rowp


def _attn_operands(h, q_ref, kp_ref, kc_ref, vp_ref, vc_ref):
    d = HEAD_DIM
    q = jnp.concatenate([q_ref[:, g * d:(g + 1) * d] for g in range(ATTN_G)], axis=0)
    pick = lambda r: jnp.where(h == 0, r[:, :d], r[:, d:])
    kcat = jnp.concatenate([pick(kp_ref[...]), pick(kc_ref[...])], axis=0)
    vcat = jnp.concatenate([pick(vp_ref[...]), pick(vc_ref[...])], axis=0)
    return q, kcat, vcat


def attn_fwd(proj, sink_rows, slope_rows, ag=()):
    T = proj.shape[0]
    blk, d = ATTN_BLOCK, HEAD_DIM

    def body(q_ref, kp_ref, kc_ref, vp_ref, vc_ref, sink_ref, slope_ref, o_ref):
        h, n = pl.program_id(0), pl.program_id(1)
        dist, valid = _attn_masks(n, ATTN_G * blk, blk)
        q, kcat, vcat = _attn_operands(h, q_ref, kp_ref, kc_ref, vp_ref, vc_ref)
        o = _attn_block(q, kcat, vcat, sink_ref[...], slope_ref[...], dist, valid)
        for g in range(ATTN_G):
            o_ref[:, g * d:(g + 1) * d] = o[g * blk:(g + 1) * blk]

    qs, kv, rowp = _attn_specs()
    (out,), gathered, _ = _pcall(
        body, (proj, proj, proj, proj, proj, sink_rows, slope_rows), name="attn_fwd",
        grid=(ATTN_KV_HEADS, T // blk), in_specs=[qs] + kv + [rowp, rowp], out_specs=[qs],
        out_shape=[jax.ShapeDtypeStruct((T, Q_A), F32)], sem=("arbitrary", "arbitrary"), ag=ag)
    return out, gathered


def attn_bwd(proj, sink_rows, slope_rows, dmix, rs=()):
    T = proj.shape[0]
    blk, d = ATTN_BLOCK, HEAD_DIM

    def body(q_ref, kp_ref, kc_ref, vp_ref, vc_ref, sink_ref, slope_ref, do_ref, dq_ref, dkv_ref, dsink_ref):
        h, n = pl.program_id(0), pl.program_id(1)

        @pl.when((h == 0) & (n == 0))
        def _():
            dkv_ref[...] = jnp.zeros_like(dkv_ref)

        @pl.when(n == 0)
        def _():
            dsink_ref[...] = jnp.zeros_like(dsink_ref)

        dist, valid = _attn_masks(n, ATTN_G * blk, blk)
        q, kcat, vcat = _attn_operands(h, q_ref, kp_ref, kc_ref, vp_ref, vc_ref)
        do = jnp.concatenate([do_ref[:, g * d:(g + 1) * d] for g in range(ATTN_G)], axis=0)
        fn = functools.partial(_attn_block, slope=slope_ref[...], dist=dist, valid=valid)
        _, vjp = jax.vjp(fn, q, kcat, vcat, sink_ref[...])
        dq, dkcat, dvcat, dsink = vjp(do)
        for g in range(ATTN_G):
            dq_ref[:, g * d:(g + 1) * d] = dq[g * blk:(g + 1) * blk]
        dsink_ref[...] += dsink
        lane = lax.broadcasted_iota(jnp.int32, (2 * blk, 2 * KV_A), 1)
        mine = (lane % KV_A) // d == h
        both = jnp.where(mine, jnp.concatenate([dkcat, dkcat, dvcat, dvcat], axis=1), 0.0)

        @pl.when(n == 0)
        def _():
            dkv_ref[0:blk, :] += both[blk:]

        @pl.when(n > 0)
        def _():
            rows = pl.ds(pl.multiple_of((n - 1) * blk, blk), 2 * blk)
            dkv_ref[rows, :] += both

    qs, kv, rowp = _attn_specs()
    outs, _, slots = _pcall(
        body, (proj, proj, proj, proj, proj, sink_rows, slope_rows, dmix), name="attn_bwd",
        grid=(ATTN_KV_HEADS, T // blk), in_specs=[qs] + kv + [rowp, rowp, qs],
        out_specs=[qs, pl.BlockSpec((T, 2 * KV_A), lambda h, n: (0, 0)), rowp],
        out_shape=[jax.ShapeDtypeStruct((T, Q_A), F32), jax.ShapeDtypeStruct((T, 2 * KV_A), F32),
                   jax.ShapeDtypeStruct((ATTN_HEADS * blk, 1), F32)],
        sem=("arbitrary", "arbitrary"), rs=rs)
    return outs, slots


_NN = (((2,), (1,)), ((0,), (0,)))
_NT = (((2,), (2,)), ((0,), (0,)))
_TN = (((1,), (1,)), ((0,), (0,)))


def _bmm(a, b, dims):
    return lax.dot_general(a.astype(BF16), b.astype(BF16), dims, preferred_element_type=F32)


def _split(x, terms):
    out = []
    for _ in range(terms):
        t = x.astype(BF16)
        out.append(t)
        x = x - t.astype(F32)
    return out


def _fine_product(a, b, dims):
    (ah, al), (bh, bl) = _split(a, 2), _split(b, 2)
    dot = lambda x, y: lax.dot_general(x, y, dims, preferred_element_type=F32)
    return dot(ah, bh) + (dot(ah, bl) + dot(al, bh))


def _mask_product(mask, x, dims):
    mb = mask.astype(BF16)
    parts = [lax.dot_general(mb, t, dims, preferred_element_type=F32) for t in _split(x, 3)]
    return parts[0] + (parts[1] + parts[2])


@jax.custom_vjp
def _fine_nt(a, b):
    return _fine_product(a, b, _NT)


_fine_nt.defvjp(lambda a, b: (_fine_product(a, b, _NT), (a, b)),
                lambda res, ct: (_fine_product(ct, res[1], _NN), _fine_product(ct, res[0], _TN)))


@jax.custom_vjp
def _mask_nn(mask, x):
    return _mask_product(mask, x, _NN)


_mask_nn.defvjp(lambda mask, x: (_mask_product(mask, x, _NN), mask),
                lambda mask, ct: (jnp.zeros_like(mask), _mask_product(mask, ct, _TN)))


@jax.custom_vjp
def _unit_lower_inverse(low):
    n = low.shape[-1]
    eye = (lax.broadcasted_iota(jnp.int32, low.shape, 1) == lax.broadcasted_iota(jnp.int32, low.shape, 2)).astype(F32)
    tinv = eye - low
    p = low
    for _ in range(n.bit_length() - 2):
        p = _bmm(p, p, _NN)
        tinv = tinv + _bmm(tinv, p, _NN)
    return tinv


def _unit_lower_inverse_fwd(low):
    tinv = _unit_lower_inverse(low)
    return tinv, tinv


_unit_lower_inverse.defvjp(_unit_lower_inverse_fwd, lambda tinv, ct: (-_bmm(_bmm(tinv, ct, _TN), tinv, _NT),))


def _dn_chunk(qc, kc, vc, zc, braw, araw, alog, dtb, nw, S):
    H, C, D = qc.shape
    row = lax.broadcasted_iota(jnp.int32, (H, C, C), 1)
    col = lax.broadcasted_iota(jnp.int32, (H, C, C), 2)
    causal = row >= col
    strict = row > col
    eye = (row == col).astype(F32)

    q = qc * lax.rsqrt(jnp.sum(qc * qc, axis=-1, keepdims=True) + EPS) * (D ** -0.5)
    k = kc * lax.rsqrt(jnp.sum(kc * kc, axis=-1, keepdims=True) + EPS)
    beta = _sigmoid(braw)
    g = -jnp.exp(alog) * _softplus(araw + dtb)
    a_col = _mask_nn(causal.astype(F32), jnp.broadcast_to(g, (H, C, C)))
    a_row = _mask_nn(jnp.ones((H, C, C), F32), eye * a_col)
    decay = jnp.where(causal, jnp.exp(jnp.where(causal, a_col - a_row, 0.0)), 0.0)
    kb = k * beta
    tinv = _unit_lower_inverse(jnp.where(strict, _fine_nt(kb, k) * decay, 0.0))
    e_col = jnp.exp(a_col)
    u = _bmm(tinv, vc * beta, _NN)
    w = _bmm(tinv, kb * e_col, _NN)
    attn = _fine_nt(q, k) * decay
    gl = a_col[:, C - 1:C, :]
    k_dec = k * jnp.exp(gl - a_col)
    v_new = u - _bmm(w, S, _NN)
    o = _bmm(q * e_col, S, _NN) + _bmm(attn, v_new, _NN)
    s_new = S * jnp.exp(jnp.broadcast_to(gl, (H, D, D))) + _bmm(k_dec, v_new, _TN)
    on = o * lax.rsqrt(jnp.mean(o * o, axis=-1, keepdims=True) + EPS) * nw
    return on * (zc * _sigmoid(zc)), s_new


DN_ZCOLS = IN_COLS_PAD - OFF_Z
DN_ZBLK = OFF_Z // DN_ZCOLS


def _dn_heads(a, off):
    return jnp.stack([a[:, off + h * DN_D:off + (h + 1) * DN_D] for h in range(DN_HEADS)])


def _dn_gate_cols(zb, off):
    return jnp.stack([zb[:, off + h:off + h + 1] for h in range(DN_HEADS)])


DN_STEP_CHUNKS = 4


def _dn_operands(x_ref, zb_ref, rows):
    x, zb = x_ref[rows, :], zb_ref[rows, :]
    return (_dn_heads(x, 0), _dn_heads(x, V_B), _dn_heads(x, 2 * V_B), _dn_heads(zb, 0),
            _dn_gate_cols(zb, V_B), _dn_gate_cols(zb, V_B + DN_HEADS))


def dn_fwd(qkvc, proj, alog, dtb, nw, ag=()):
    T = qkvc.shape[0]
    H, C, D, G = DN_HEADS, DN_CHUNK, DN_D, DN_STEP_CHUNKS
    N = T // C

    def body(x_ref, zb_ref, alog_ref, dtb_ref, nw_ref, o_ref, sall_ref, s_ref):
        @pl.when(pl.program_id(0) == 0)
        def _():
            s_ref[...] = jnp.zeros_like(s_ref)

        s = s_ref[...]
        for c in range(G):
            rows = slice(c * C, (c + 1) * C)
            sall_ref[c] = s
            on, s = _dn_chunk(*_dn_operands(x_ref, zb_ref, rows), alog_ref[...], dtb_ref[...], nw_ref[...], s)
            for h in range(H):
                o_ref[rows, h * D:(h + 1) * D] = on[h]
        s_ref[...] = s

    par = pl.BlockSpec((H, 1, 1), lambda n: (0, 0, 0))
    outs, gathered, _ = _pcall(
        body, (qkvc, proj, alog, dtb, nw), name="dn_fwd", grid=(N // G,),
        in_specs=[pl.BlockSpec((G * C, QKV_B), lambda n: (n, 0)),
                  pl.BlockSpec((G * C, DN_ZCOLS), lambda n: (n, DN_ZBLK)),
                  par, par, pl.BlockSpec((1, 1, D), lambda n: (0, 0, 0))],
        out_specs=[pl.BlockSpec((G * C, V_B), lambda n: (n, 0)), pl.BlockSpec((G, H, D, D), lambda n: (n, 0, 0, 0))],
        out_shape=[jax.ShapeDtypeStruct((T, V_B), F32), jax.ShapeDtypeStruct((N, H, D, D), F32)],
        scratch_shapes=[pltpu.VMEM((H, D, D), F32)], sem=("arbitrary",), ag=ag)
    return outs, gathered


def dn_bwd(qkvc, proj, alog, dtb, nw, sall, dmix, rs=()):
    T = qkvc.shape[0]
    H, C, D, G = DN_HEADS, DN_CHUNK, DN_D, DN_STEP_CHUNKS
    N = T // C // G

    def body(x_ref, zb_ref, alog_ref, dtb_ref, nw_ref, sall_ref, do_ref,
             dx_ref, dzb_ref, dalog_ref, ddtb_ref, dnw_ref, ds_ref):
        @pl.when(pl.program_id(0) == 0)
        def _():
            ds_ref[...] = jnp.zeros_like(ds_ref)
            dalog_ref[...] = jnp.zeros_like(dalog_ref)
            ddtb_ref[...] = jnp.zeros_like(ddtb_ref)
            dnw_ref[...] = jnp.zeros_like(dnw_ref)

        ds = ds_ref[...]
        lane = lax.broadcasted_iota(jnp.int32, (C, LANES), 1)
        for c in reversed(range(G)):
            rows = slice(c * C, (c + 1) * C)
            args = (*_dn_operands(x_ref, zb_ref, rows), alog_ref[...], dtb_ref[...], nw_ref[...], sall_ref[c])
            _, vjp = jax.vjp(_dn_chunk, *args)
            dq, dk, dv, dz, db, da, dalog, ddtb, dnw, ds = vjp((_dn_heads(do_ref[rows, :], 0), ds))
            for h in range(H):
                dx_ref[rows, h * D:(h + 1) * D] = dq[h]
                dx_ref[rows, V_B + h * D:V_B + (h + 1) * D] = dk[h]
                dx_ref[rows, 2 * V_B + h * D:2 * V_B + (h + 1) * D] = dv[h]
                dzb_ref[rows, h * D:(h + 1) * D] = dz[h]
            tail = jnp.zeros((C, LANES), F32)
            for h in range(H):
                tail = tail + jnp.where(lane == h, jnp.broadcast_to(db[h], (C, LANES)), 0.0)
                tail = tail + jnp.where(lane == H + h, jnp.broadcast_to(da[h], (C, LANES)), 0.0)
            dzb_ref[rows, V_B:V_B + LANES] = tail
            dzb_ref[rows, V_B + LANES:] = jnp.zeros((C, DN_ZCOLS - V_B - LANES), F32)
            dalog_ref[...] += dalog
            ddtb_ref[...] += ddtb
            dnw_ref[...] += dnw
        ds_ref[...] = ds

    par = pl.BlockSpec((H, 1, 1), lambda i: (0, 0, 0))
    nws = pl.BlockSpec((1, 1, D), lambda i: (0, 0, 0))
    outs, _, slots = _pcall(
        body, (qkvc, proj, alog, dtb, nw, sall, dmix), name="dn_bwd", grid=(N,),
        in_specs=[pl.BlockSpec((G * C, QKV_B), lambda i: (N - 1 - i, 0)),
                  pl.BlockSpec((G * C, DN_ZCOLS), lambda i: (N - 1 - i, DN_ZBLK)), par, par, nws,
                  pl.BlockSpec((G, H, D, D), lambda i: (N - 1 - i, 0, 0, 0)),
                  pl.BlockSpec((G * C, V_B), lambda i: (N - 1 - i, 1))],
        out_specs=[pl.BlockSpec((G * C, QKV_B), lambda i: (N - 1 - i, 0)),
                   pl.BlockSpec((G * C, DN_ZCOLS), lambda i: (N - 1 - i, 0)), par, par, nws],
        out_shape=[jax.ShapeDtypeStruct((T, QKV_B), F32), jax.ShapeDtypeStruct((T, DN_ZCOLS), F32)]
        + [jax.ShapeDtypeStruct((H, 1, 1), F32)] * 2 + [jax.ShapeDtypeStruct((1, 1, D), F32)],
        scratch_shapes=[pltpu.VMEM((H, D, D), F32)], sem=("arbitrary",), rs=rs)
    return outs, slots


def _conv_taps(buf_ref, w, width, halo, tm):
    acc = None
    for kk, win in _windows(buf_ref, [halo - (width - 1) + kk for kk in range(width)], tm):
        term = w[kk:kk + 1, :] * win
        acc = term if acc is None else acc + term
    return acc


def _windows(ref, offsets, tm):
    for res in range(SUBLANES):
        ks = [k for k, o in enumerate(offsets) if o % SUBLANES == res]
        if not ks:
            continue
        lo = min(offsets[k] for k in ks)
        hi = max(offsets[k] for k in ks)
        shifted = ref[pl.ds(lo, tm + hi - lo), :]
        for k in ks:
            yield k, shifted[offsets[k] - lo:offsets[k] - lo + tm]


def _conv_taps_bwd(dbuf_ref, w, width, tm):
    acc = None
    for kk, win in _windows(dbuf_ref, [width - 1 - kk for kk in range(width)], tm):
        term = w[kk:kk + 1, :] * win
        acc = term if acc is None else acc + term
    return acc


def _conv_dw_acc(dw_ref, dout, buf_ref, width, halo, tm):
    for kk, win in _windows(buf_ref, [halo - (width - 1) + kk for kk in range(width)], tm):
        dw_ref[pl.ds(kk, 1), :] += jnp.sum(dout * win, axis=0, keepdims=True)


DNC_HALO = 8
DNC_COLS = 768


def dnconv_fwd(proj, w):
    T = proj.shape[0]
    tm = min(T, 256)
    hb = tm // DNC_HALO

    def body(x_ref, h_ref, w_ref, o_ref, buf_ref):
        i = pl.program_id(0)
        buf_ref[0:DNC_HALO, :] = jnp.where(i > 0, h_ref[...], 0.0)
        buf_ref[DNC_HALO:, :] = x_ref[...]
        acc = _conv_taps(buf_ref, w_ref[...], DN_CONV, DNC_HALO, tm)
        o_ref[...] = acc * _sigmoid(acc)

    return pl.pallas_call(
        body, name="dnconv_fwd", grid=(T // tm, 2),
        in_specs=[pl.BlockSpec((tm, DNC_COLS), lambda i, c: (i, 1 + c)),
                  pl.BlockSpec((DNC_HALO, DNC_COLS), lambda i, c: (jnp.maximum(i * hb - 1, 0), 1 + c)),
                  pl.BlockSpec((DN_CONV, DNC_COLS), lambda i, c: (0, c))],
        out_specs=pl.BlockSpec((tm, DNC_COLS), lambda i, c: (i, c)),
        out_shape=jax.ShapeDtypeStruct((T, QKV_B), F32),
        scratch_shapes=[pltpu.VMEM((DNC_HALO + tm, DNC_COLS), F32)],
        compiler_params=_cparams(("parallel", "parallel")),
    )(proj, proj, w)


def dnconv_bwd(proj, w, dout):
    T = proj.shape[0]
    tm = min(T, 256)
    nt = T // tm
    hb = tm // DNC_HALO

    def body(x_ref, h_ref, w_ref, do_ref, dx_ref, dw_ref, buf_ref, dbuf_ref):
        r = pl.program_id(1)
        i = nt - 1 - r

        @pl.when(r == 0)
        def _():
            dw_ref[...] = jnp.zeros_like(dw_ref)
            dbuf_ref[tm:, :] = jnp.zeros((DNC_HALO, DNC_COLS), F32)

        buf_ref[0:DNC_HALO, :] = jnp.where(i > 0, h_ref[...], 0.0)
        buf_ref[DNC_HALO:, :] = x_ref[...]
        wv = w_ref[...]
        acc = _conv_taps(buf_ref, wv, DN_CONV, DNC_HALO, tm)
        sg = _sigmoid(acc)
        dacc = do_ref[...] * (sg * (1.0 + acc * (1.0 - sg)))
        dbuf_ref[0:tm, :] = dacc
        dx_ref[...] = _conv_taps_bwd(dbuf_ref, wv, DN_CONV, tm)
        _conv_dw_acc(dw_ref, dacc, buf_ref, DN_CONV, DNC_HALO, tm)
        dbuf_ref[tm:, :] = dacc[0:DNC_HALO, :]

    return pl.pallas_call(
        body, name="dnconv_bwd", grid=(2, nt),
        in_specs=[pl.BlockSpec((tm, DNC_COLS), lambda c, r: (nt - 1 - r, 1 + c)),
                  pl.BlockSpec((DNC_HALO, DNC_COLS), lambda c, r: (jnp.maximum((nt - 1 - r) * hb - 1, 0), 1 + c)),
                  pl.BlockSpec((DN_CONV, DNC_COLS), lambda c, r: (0, c)),
                  pl.BlockSpec((tm, DNC_COLS), lambda c, r: (nt - 1 - r, c))],
        out_specs=[pl.BlockSpec((tm, DNC_COLS), lambda c, r: (nt - 1 - r, c)),
                   pl.BlockSpec((DN_CONV, DNC_COLS), lambda c, r: (0, c))],
        out_shape=[jax.ShapeDtypeStruct((T, QKV_B), F32), jax.ShapeDtypeStruct((DN_CONV, QKV_B), F32)],
        scratch_shapes=[pltpu.VMEM((DNC_HALO + tm, DNC_COLS), F32), pltpu.VMEM((tm + DNC_HALO, DNC_COLS), F32)],
        compiler_params=_cparams(("parallel", "arbitrary")),
    )(proj, proj, w, dout)


CV_HALO = 32


def _cv_post(cv, lnw, lnb):
    mu = jnp.mean(cv, axis=-1, keepdims=True)
    xc = cv - mu
    y = xc * lax.rsqrt(jnp.mean(xc * xc, axis=-1, keepdims=True) + EPS) * lnw + lnb
    return y * _sigmoid(y)


def cv_fwd(ab, w, bdw, lnw, lnb, ag=()):
    T = ab.shape[0]
    D = ab.shape[1] // 2
    tm = min(T, 256)
    hb = tm // CV_HALO

    def body(a_ref, b_ref, ah_ref, bh_ref, w_ref, bdw_ref, lnw_ref, lnb_ref, o_ref, cv_ref, buf_ref):
        i = pl.program_id(0)
        buf_ref[0:CV_HALO, :] = jnp.where(i > 0, ah_ref[...] * _sigmoid(bh_ref[...]), 0.0)
        buf_ref[CV_HALO:, :] = a_ref[...] * _sigmoid(b_ref[...])
        cv = _conv_taps(buf_ref, w_ref[...], CONV_WIDTH, CV_HALO, tm) + bdw_ref[...]
        cv_ref[...] = cv
        o_ref[...] = _cv_post(cv, lnw_ref[...], lnb_ref[...])

    halo = lambda c: pl.BlockSpec((CV_HALO, D), lambda i: (jnp.maximum(i * hb - 1, 0), c))
    vec = pl.BlockSpec((1, D), lambda i: (0, 0))
    tile = pl.BlockSpec((tm, D), lambda i: (i, 0))
    outs, gathered, _ = _pcall(
        body, (ab, ab, ab, ab, w, bdw, lnw, lnb), name="cv_fwd", grid=(T // tm,),
        in_specs=[tile, pl.BlockSpec((tm, D), lambda i: (i, 1)),
                  halo(0), halo(1), pl.BlockSpec((CONV_WIDTH, D), lambda i: (0, 0)), vec, vec, vec],
        out_specs=[tile, tile],
        out_shape=[jax.ShapeDtypeStruct((T, D), F32), jax.ShapeDtypeStruct((T, D), F32)],
        scratch_shapes=[pltpu.VMEM((CV_HALO + tm, D), F32)], sem=("arbitrary",), ag=ag)
    return outs, gathered


def cv_bwd(ab, cv, w, lnw, lnb, dout, rs=()):
    T = ab.shape[0]
    D = ab.shape[1] // 2
    tm = min(T, 256)
    nt = T // tm
    hb = tm // CV_HALO

    def body(a_ref, b_ref, ah_ref, bh_ref, cv_ref, w_ref, lnw_ref, lnb_ref, do_ref,
             da_ref, db_ref, dw_ref, dbdw_ref, dlnw_ref, dlnb_ref, buf_ref, dbuf_ref):
        r = pl.program_id(0)
        i = nt - 1 - r

        @pl.when(r == 0)
        def _():
            dw_ref[...] = jnp.zeros_like(dw_ref)
            dbdw_ref[...] = jnp.zeros_like(dbdw_ref)
            dlnw_ref[...] = jnp.zeros_like(dlnw_ref)
            dlnb_ref[...] = jnp.zeros_like(dlnb_ref)
            dbuf_ref[tm:, :] = jnp.zeros((CV_HALO, D), F32)

        a = a_ref[...]
        sb = _sigmoid(b_ref[...])
        buf_ref[0:CV_HALO, :] = jnp.where(i > 0, ah_ref[...] * _sigmoid(bh_ref[...]), 0.0)
        buf_ref[CV_HALO:, :] = a * sb
        wv = w_ref[...]
        _, vjp = jax.vjp(_cv_post, cv_ref[...], lnw_ref[...], lnb_ref[...])
        dcv, dlnw, dlnb = vjp(do_ref[...])
        dlnw_ref[...] += dlnw
        dlnb_ref[...] += dlnb
        dbdw_ref[...] += jnp.sum(dcv, axis=0, keepdims=True)
        dbuf_ref[0:tm, :] = dcv
        du = _conv_taps_bwd(dbuf_ref, wv, CONV_WIDTH, tm)
        _conv_dw_acc(dw_ref, dcv, buf_ref, CONV_WIDTH, CV_HALO, tm)
        dbuf_ref[tm:, :] = dcv[0:CV_HALO, :]
        da_ref[...] = du * sb
        db_ref[...] = du * a * sb * (1.0 - sb)

    tile = lambda c: pl.BlockSpec((tm, D), lambda r: (nt - 1 - r, c))
    halo = lambda c: pl.BlockSpec((CV_HALO, D), lambda r: (jnp.maximum((nt - 1 - r) * hb - 1, 0), c))
    vec = pl.BlockSpec((1, D), lambda r: (0, 0))
    wsp = pl.BlockSpec((CONV_WIDTH, D), lambda r: (0, 0))
    (da, db, dw, dbdw, dlnw, dlnb), _, slots = _pcall(
        body, (ab, ab, ab, ab, cv, w, lnw, lnb, dout), name="cv_bwd", grid=(nt,),
        in_specs=[tile(0), tile(1), halo(0), halo(1), tile(0), wsp, vec, vec, tile(0)],
        out_specs=[tile(0), tile(0), wsp, vec, vec, vec],
        out_shape=[jax.ShapeDtypeStruct((T, D), F32), jax.ShapeDtypeStruct((T, D), F32),
                   jax.ShapeDtypeStruct((CONV_WIDTH, D), F32)] + [jax.ShapeDtypeStruct((1, D), F32)] * 3,
        scratch_shapes=[pltpu.VMEM((CV_HALO + tm, D), F32), pltpu.VMEM((tm + CV_HALO, D), F32)],
        sem=("arbitrary",), rs=rs)
    return (jnp.concatenate([da, db], axis=1), dw, dbdw, dlnw, dlnb), slots


def adamw(w, m, v, slots, rs=()):
    L, R, C = w.shape
    ns = slots[0].shape[0]
    fits = lambda r, c: ns * r * c * 2 <= ADAM_SLOT_BLOCK
    tiles = [(R, C)] if fits(R, C) else []
    tiles += [(d, C) for d in range(16, R, 16) if R % d == 0 and fits(d, C)]
    tiles += [(R, d) for d in range(LANES, C, LANES) if C % d == 0 and fits(R, d)]
    tr, tc = max(tiles, key=lambda t: t[0] * t[1])
    c1 = 1.0 / (1.0 - ADAM_B1 ** ADAM_STEP)
    c2 = 1.0 / (1.0 - ADAM_B2 ** ADAM_STEP)

    def body(w_ref, m_ref, v_ref, *rest):
        s_refs = rest[:L]
        g_ref, d_ref, nm_ref, nv_ref = rest[L:]
        l = pl.program_id(0)
        for k in range(L):
            @pl.when(l == k)
            def _(s_ref=s_refs[k]):
                g = s_ref[0].astype(F32)
                for j in range(1, ns):
                    g = g + s_ref[j].astype(F32)
                nm = ADAM_B1 * m_ref[0] + (1.0 - ADAM_B1) * g
                nv = ADAM_B2 * v_ref[0] + (1.0 - ADAM_B2) * (g * g)
                g_ref[0] = g
                nm_ref[0] = nm
                nv_ref[0] = nv
                d_ref[0] = -ADAM_LR * ((nm * c1) / (jnp.sqrt(nv * c2) + ADAM_EPS) + ADAM_WD * w_ref[0])

    nc = C // tc
    blk = pl.BlockSpec((1, tr, tc), lambda l, i: (l, i // nc, i % nc))
    slot = lambda k: pl.BlockSpec((ns, tr, tc), lambda l, i: (0, jnp.where(l == k, i // nc, 0),
                                                                 jnp.where(l == k, i % nc, 0)))
    outs, _, landed = _pcall(
        body, (w, m, v, *slots), name="adamw", grid=(L, (R // tr) * nc),
        in_specs=[blk, blk, blk] + [slot(k) for k in range(L)],
        out_specs=[blk, blk, blk, blk],
        out_shape=[jax.ShapeDtypeStruct((L, R, C), F32)] * 4,
        sem=("arbitrary", "arbitrary"), rs=rs)
    return outs, landed


def _unshard(g, axis):
    g = jnp.moveaxis(g, 0, axis)
    s = g.shape
    return g.reshape(s[:axis] + (s[axis] * s[axis + 1],) + s[axis + 2:])


def _to_blocks(full, axis):
    s = full.shape
    g = full.reshape(s[:axis] + (N_DEV, s[axis] // N_DEV) + s[axis + 1:])
    return jnp.moveaxis(g, axis, 0)


SMALL = (("norm_w", 2), ("dn_conv_w", 2), ("conv_b_pw1", 1), ("conv_w_dw", 2), ("conv_b_dw", 1),
         ("conv_ln_w", 1), ("conv_ln_b", 1), ("conv_b_pw2", 1),
         ("attn_sinks", None), ("dn_a_log", None), ("dn_dt_bias", None), ("dn_norm_w", None), ("final_norm_w", None))
SMALL_AXIS = dict(SMALL)


def _pack(parts):
    flat = jnp.concatenate([p.reshape(-1) for p in parts])
    pad = (-flat.shape[0]) % LANES
    return jnp.pad(flat, (0, pad))


def _unpack(flat, shapes):
    out, off = [], 0
    for s in shapes:
        n = int(np.prod(s))
        out.append(flat[off:off + n].reshape(s))
        off += n
    return out


def kernel(x, norm_w, ffn_w_gate, ffn_w_up, ffn_w_down, mix_w_in, dn_conv_w, attn_sinks, dn_a_log, dn_dt_bias, dn_norm_w, mix_w_out, conv_w_pw1, conv_b_pw1, conv_w_dw, conv_b_dw, conv_ln_w, conv_ln_b, conv_w_pw2, conv_b_pw2, final_norm_w, loss_target, m_norm_w, m_ffn_w_gate, m_ffn_w_up, m_ffn_w_down, m_mix_w_in, m_dn_conv_w, m_attn_sinks, m_dn_a_log, m_dn_dt_bias, m_dn_norm_w, m_mix_w_out, m_conv_w_pw1, m_conv_b_pw1, m_conv_w_dw, m_conv_b_dw, m_conv_ln_w, m_conv_ln_b, m_conv_w_pw2, m_conv_b_pw2, m_final_norm_w, v_norm_w, v_ffn_w_gate, v_ffn_w_up, v_ffn_w_down, v_mix_w_in, v_dn_conv_w, v_attn_sinks, v_dn_a_log, v_dn_dt_bias, v_dn_norm_w, v_mix_w_out, v_conv_w_pw1, v_conv_b_pw1, v_conv_w_dw, v_conv_b_dw, v_conv_ln_w, v_conv_ln_b, v_conv_w_pw2, v_conv_b_pw2, v_final_norm_w):
    W = dict(norm_w=norm_w, ffn_w_gate=ffn_w_gate, ffn_w_up=ffn_w_up, ffn_w_down=ffn_w_down, mix_w_in=mix_w_in,
             dn_conv_w=dn_conv_w, attn_sinks=attn_sinks, dn_a_log=dn_a_log, dn_dt_bias=dn_dt_bias,
             dn_norm_w=dn_norm_w, mix_w_out=mix_w_out, conv_w_pw1=conv_w_pw1, conv_b_pw1=conv_b_pw1,
             conv_w_dw=conv_w_dw, conv_b_dw=conv_b_dw, conv_ln_w=conv_ln_w, conv_ln_b=conv_ln_b,
             conv_w_pw2=conv_w_pw2, conv_b_pw2=conv_b_pw2, final_norm_w=final_norm_w)
    M = dict(norm_w=m_norm_w, ffn_w_gate=m_ffn_w_gate, ffn_w_up=m_ffn_w_up, ffn_w_down=m_ffn_w_down,
             mix_w_in=m_mix_w_in, dn_conv_w=m_dn_conv_w, attn_sinks=m_attn_sinks, dn_a_log=m_dn_a_log,
             dn_dt_bias=m_dn_dt_bias, dn_norm_w=m_dn_norm_w, mix_w_out=m_mix_w_out, conv_w_pw1=m_conv_w_pw1,
             conv_b_pw1=m_conv_b_pw1, conv_w_dw=m_conv_w_dw, conv_b_dw=m_conv_b_dw, conv_ln_w=m_conv_ln_w,
             conv_ln_b=m_conv_ln_b, conv_w_pw2=m_conv_w_pw2, conv_b_pw2=m_conv_b_pw2, final_norm_w=m_final_norm_w)
    V = dict(norm_w=v_norm_w, ffn_w_gate=v_ffn_w_gate, ffn_w_up=v_ffn_w_up, ffn_w_down=v_ffn_w_down,
             mix_w_in=v_mix_w_in, dn_conv_w=v_dn_conv_w, attn_sinks=v_attn_sinks, dn_a_log=v_dn_a_log,
             dn_dt_bias=v_dn_dt_bias, dn_norm_w=v_dn_norm_w, mix_w_out=v_mix_w_out, conv_w_pw1=v_conv_w_pw1,
             conv_b_pw1=v_conv_b_pw1, conv_w_dw=v_conv_w_dw, conv_b_dw=v_conv_b_dw, conv_ln_w=v_conv_ln_w,
             conv_ln_b=v_conv_ln_b, conv_w_pw2=v_conv_w_pw2, conv_b_pw2=v_conv_b_pw2, final_norm_w=v_final_norm_w)

    T, D = x.shape[1], x.shape[2]
    xs = x[0]

    big = ("ffn_w_gate", "ffn_w_up", "ffn_w_down", "mix_w_in", "mix_w_out", "conv_w_pw1", "conv_w_pw2")
    shard3 = {k: W[k].reshape((-1,) + W[k].shape[-2:]) for k in big}
    shard_bf = {k: shard3[k].astype(BF16) for k in big}
    ffn_unit = lambda i: [("ffn_w_gate", i), ("ffn_w_up", i), ("ffn_w_down", i)]
    even_unit = lambda e: [("mix_w_in", e), ("mix_w_out", e)]
    odd_unit = lambda e: [("conv_w_pw1", e), ("conv_w_pw2", e)]
    have = {}

    def ag_jobs(units):
        return [(shard_bf[k], i) for k, i in units]

    def ag_done(units, gathered):
        have.update(zip(units, gathered))

    small_sharded = [(k, ax) for k, ax in SMALL if ax is not None]
    small_pack = _pack([W[k] for k, _ in small_sharded])[None, :]
    first_units = ffn_unit(0)
    gathered = exchange(ag_jobs(first_units) + [(small_pack, None)])
    ag_done(first_units, gathered[:-1])
    small_full = {}
    for (k, ax), parts in zip(small_sharded,
                              zip(*[_unpack(gathered[-1][s, 0], [W[k].shape for k, _ in small_sharded])
                                    for s in range(N_DEV)])):
        small_full[k] = _unshard(jnp.stack(parts), ax)
    nw_full = small_full["norm_w"]

    ffn_w = lambda i: [have[u] for u in ffn_unit(i)]
    w_in_of = lambda e: jnp.pad(_unshard(have[("mix_w_in", e)], 1), ((0, 0), (0, IN_COLS_PAD - IN_COLS)))
    w_out_of = lambda e: have[("mix_w_out", e)].reshape(D, D)
    w_pw1_of = lambda e: _unshard(have[("conv_w_pw1", e)], 1)
    w_pw2_of = lambda e: have[("conv_w_pw2", e)].reshape(D, D)
    fwd_order, needed = [], {}
    for l in range(DEPTH):
        mixer = [("A", l), ("E", l)] if l % 2 == 0 else [("O", l)]
        fwd_order += [("F", 2 * l)] + mixer + [("F", 2 * l + 1)]
        needed[("F", 2 * l)], needed[("F", 2 * l + 1)] = ffn_unit(2 * l), ffn_unit(2 * l + 1)
        needed[mixer[0]] = even_unit(l // 2) if l % 2 == 0 else odd_unit(l // 2)
    queue = [(u, pos) for pos, key in enumerate(fwd_order) for u in needed.get(key, []) if u not in first_units]
    unit_bytes = lambda u: N_DEV * shard_bf[u[0]][u[1]].size * 2
    fwd_carry, at = {}, 0
    for pos, key in enumerate(fwd_order):
        cap = FWD_CARRY_BYTES[key[0]]
        taken, used = [], 0
        while at < len(queue) and (queue[at][1] <= pos + 1 or used + unit_bytes(queue[at][0]) <= cap):
            taken.append(queue[at][0])
            used += unit_bytes(queue[at][0])
            at += 1
        fwd_carry[key] = taken
    zero_in = jnp.zeros((1, IN_COLS_PAD), F32)
    zero_d = jnp.zeros((1, D), F32)
    slope_rows = jnp.asarray(np.repeat(2.0 ** (-8.0 * np.arange(1, ATTN_HEADS + 1) / ATTN_HEADS), ATTN_BLOCK)
                             .astype(np.float32)[:, None])

    saved = []
    h = xs
    w_in, w_out, w_pw1, w_pw2 = {}, {}, {}, {}

    def ffn_forward(h, l, half):
        i = 2 * l + half
        units = fwd_carry.get(("F", i), [])
        h, gathered = ffn_fwd(h, nw_full[l, 2 * half][None], *ffn_w(i), ag=ag_jobs(units))
        ag_done(units, gathered)
        return h

    for l in range(DEPTH):
        e = l // 2
        st = {"x0": h}
        h = ffn_forward(h, l, 0)
        st["x1"] = h
        if l % 2 == 0:
            w_in[e], w_out[e] = w_in_of(e), w_out_of(e)
            proj = rmslin_fwd(h, nw_full[l, 1][None], w_in[e], zero_in)
            st["proj"] = proj
            st["qkvc"] = dnconv_fwd(proj, small_full["dn_conv_w"][e])
            st["sink_rows"] = jnp.repeat(attn_sinks[e], ATTN_BLOCK)[:, None]
            st["alog"] = dn_a_log[e].reshape(DN_HEADS, 1, 1)
            st["dtb"] = dn_dt_bias[e].reshape(DN_HEADS, 1, 1)
            st["dnw"] = dn_norm_w[e].reshape(1, 1, DN_D)
            units = fwd_carry[("A", l)]
            st["att"], gathered = attn_fwd(proj, st["sink_rows"], slope_rows, ag=ag_jobs(units))
            ag_done(units, gathered)
            units = fwd_carry[("E", l)]
            (st["og"], st["sall"]), gathered = dn_fwd(st["qkvc"], proj, st["alog"], st["dtb"], st["dnw"],
                                                      ag=ag_jobs(units))
            ag_done(units, gathered)
            h = lin_fwd(h, [st["att"], st["og"]], w_out[e], zero_d)
        else:
            units = fwd_carry[("O", l)]
            w_pw1[e], w_pw2[e] = w_pw1_of(e), w_pw2_of(e)
            st["ab"] = rmslin_fwd(h, nw_full[l, 1][None], w_pw1[e], small_full["conv_b_pw1"][e][None])
            (st["act"], st["cv"]), gathered = cv_fwd(st["ab"], small_full["conv_w_dw"][e], small_full["conv_b_dw"][e][None],
                                         small_full["conv_ln_w"][e][None], small_full["conv_ln_b"][e][None],
                                         ag=ag_jobs(units))
            ag_done(units, gathered)
            h = lin_fwd(h, [st["act"]], w_pw2[e], small_full["conv_b_pw2"][e][None])
        st["x2"] = h
        h = ffn_forward(h, l, 1)
        saved.append(st)

    loss_part, dh, dfinal = loss_fwd_bwd(h, final_norm_w[None], loss_target[0])
    loss = lax.psum(loss_part[0, 0], ("x", "y", "c"))

    d_norm = [[None] * 3 for _ in range(DEPTH)]
    d_small = {k: [None, None] for k in ("dn_conv_w", "conv_b_pw1", "conv_w_dw", "conv_b_dw", "conv_ln_w",
                                         "conv_ln_b", "conv_b_pw2", "attn_sinks", "dn_a_log", "dn_dt_bias",
                                         "dn_norm_w")}
    pending, slot = [], {}

    def take_jobs(cap=None, only=None):
        taken = [p for p in pending if p[1] == "swap"]
        used = 0
        for p in pending:
            if p[1] == "swap" or (only is not None and p[0][0] not in only):
                continue
            if cap is not None and used + p[2].size * 2 > cap:
                break
            taken.append(p)
            used += p[2].size * 2
        pending[:] = [p for p in pending if all(p is not t for t in taken)]
        return taken, [(kind, arr) for _, kind, arr in taken]

    def land(taken, results):
        swapped = [(unit, arr, res) for (unit, kind, arr), res in zip(taken, results) if kind == "swap"]
        slot.update({unit: res for (unit, kind, _), res in zip(taken, results) if kind != "swap"})
        if swapped:
            sums = pair_add([g for _, g, _ in swapped], [r for _, _, r in swapped])
            pending.extend((unit, "chips", h) for (unit, _, _), h in zip(swapped, sums))

    def ffn_backward(dh, l, half):
        i = 2 * l + half
        taken, jobs = take_jobs(BWD_CARRY_BYTES["F"])
        (dh, dg, du, dd, d_norm[l][2 * half]), results = ffn_bwd(
            st["x2" if half else "x0"], dh, nw_full[l, 2 * half][None], *ffn_w(i), rs=jobs)
        land(taken, results)
        pending.extend((u, "swap", g) for u, g in zip(ffn_unit(i), (dg, du, dd)))
        return dh

    for l in reversed(range(DEPTH)):
        e = l // 2
        st = saved[l]
        dh = ffn_backward(dh, l, 1)
        if l % 2 == 0:
            dmix, d_out, _ = lin_bwd([st["att"], st["og"]], dh, w_out[e])
            pending.append((("mix_w_out", e), "direct", d_out.reshape(N_DEV, D // N_DEV, D)))
            taken, jobs = take_jobs(BWD_CARRY_BYTES["E"])
            (dqkvc, dzba, dalog, ddtb, ddnw), results = dn_bwd(
                st["qkvc"], st["proj"], st["alog"], st["dtb"], st["dnw"], st["sall"], dmix, rs=jobs)
            land(taken, results)
            taken, jobs = take_jobs(BWD_CARRY_BYTES["A"])
            (dqa, dkva, dsink), results = attn_bwd(st["proj"], st["sink_rows"], slope_rows, dmix, rs=jobs)
            land(taken, results)
            dqkv, d_small["dn_conv_w"][e] = dnconv_bwd(st["proj"], small_full["dn_conv_w"][e], dqkvc)
            dh, d_in, _, d_norm[l][1] = rmslin_bwd(st["x1"], dh, [dqa, dkva, dqkv, dzba], nw_full[l, 1][None],
                                                   w_in[e])
            pending.append((("mix_w_in", e), "direct", _to_blocks(d_in[:, :IN_COLS], 1)))
            d_small["attn_sinks"][e] = jnp.sum(dsink.reshape(ATTN_HEADS, ATTN_BLOCK), axis=1)
            d_small["dn_a_log"][e] = dalog.reshape(DN_HEADS)
            d_small["dn_dt_bias"][e] = ddtb.reshape(DN_HEADS)
            d_small["dn_norm_w"][e] = ddnw.reshape(DN_D)
        else:
            dact, d_pw2, d_small["conv_b_pw2"][e] = lin_bwd([st["act"]], dh, w_pw2[e])
            pending.append((("conv_w_pw2", e), "direct", d_pw2.reshape(N_DEV, D // N_DEV, D)))
            taken, jobs = take_jobs(BWD_CARRY_BYTES["O"])
            (dab, d_small["conv_w_dw"][e], d_small["conv_b_dw"][e], d_small["conv_ln_w"][e],
             d_small["conv_ln_b"][e]), results = cv_bwd(
                st["ab"], st["cv"], small_full["conv_w_dw"][e],
                small_full["conv_ln_w"][e][None], small_full["conv_ln_b"][e][None], dact, rs=jobs)
            land(taken, results)
            dh, d_pw1, d_small["conv_b_pw1"][e], d_norm[l][1] = rmslin_bwd(
                st["x1"], dh, dab, nw_full[l, 1][None], w_pw1[e])
            pending.append((("conv_w_pw1", e), "direct", _to_blocks(d_pw1, 1)))
        dh = ffn_backward(dh, l, 0)
    grad_x = dh[None]

    full_small = {"norm_w": jnp.stack([jnp.concatenate(r, axis=0) for r in d_norm]),
                  "final_norm_w": dfinal[0]}
    for k, pair in d_small.items():
        full_small[k] = jnp.stack([p.reshape(W[k].shape[1:-1] + (-1,)) if SMALL_AXIS[k] is not None
                                   else p for p in pair])
    rows = []
    for s in range(N_DEV):
        parts = [_to_blocks(full_small[k], ax)[s] if ax is not None else full_small[k] for k, ax in SMALL]
        rows.append(_pack(parts))
    send_small = jnp.stack(rows)[:, None, :]
    pending.append((("small", 0), "direct", send_small))

    res = {}
    waiting = lambda k: [p for p in pending if p[0][0] == k]
    adam_order = sorted(big, key=lambda k: len(waiting(k))) + ["small"]
    for n, k in enumerate(adam_order[:-1]):
        nxt = next((kk for kk in adam_order[n + 1:] if waiting(kk)), "small")
        taken, jobs = take_jobs(only=(nxt, "small") if n == 1 else (nxt,))
        turned = k in ("ffn_w_gate", "ffn_w_up")
        view = lambda a: jnp.swapaxes(a.reshape(shard3[k].shape), 1, 2) if turned else a.reshape(shard3[k].shape)
        outs, results = adamw(view(W[k]), view(M[k]), view(V[k]),
                              [slot[(k, i)] for i in range(shard3[k].shape[0])], rs=jobs)
        land(taken, results)
        res[k] = [(jnp.swapaxes(o, 1, 2) if turned else o).reshape(W[k].shape) for o in outs]
    pk = lambda d: _pack([d[k] for k, _ in SMALL])[None, None, :]
    outs, _ = adamw(pk(W), pk(M), pk(V), [slot[("small", 0)]])
    shapes = [W[k].shape for k, _ in SMALL]
    unp = [_unpack(o[0, 0], shapes) for o in outs]
    for i, (k, _) in enumerate(SMALL):
        res[k] = [u[i] for u in unp]

    order = ("norm_w", "ffn_w_gate", "ffn_w_up", "ffn_w_down", "mix_w_in", "dn_conv_w", "attn_sinks", "dn_a_log",
             "dn_dt_bias", "dn_norm_w", "mix_w_out", "conv_w_pw1", "conv_b_pw1", "conv_w_dw", "conv_b_dw",
             "conv_ln_w", "conv_ln_b", "conv_w_pw2", "conv_b_pw2", "final_norm_w")
    return (loss, grad_x, *[res[k][0] for k in order], *[res[k][1] for k in order],
            *[res[k][2] for k in order], *[res[k][3] for k in order])
```

```python
import functools

import numpy as np
import jax
import jax.numpy as jnp
from jax import lax
from jax.experimental import pallas as pl
from jax.experimental.pallas import tpu as pltpu

F32 = jnp.float32
BF16 = jnp.bfloat16
EPS = 1e-6
N_DEV = 8
N_CHIP = 4
V7X_VMEM_LIMIT = 60 * 2**20
MESH = pl.DeviceIdType.MESH
LANES = 128
SUBLANES = 8

DEPTH = 4
D_MODEL = 1024
ATTN_HEADS, ATTN_KV_HEADS, HEAD_DIM, ATTN_BLOCK = 8, 2, 64, 128
DN_HEADS, DN_D, DN_CHUNK, DN_CONV = 8, 64, 64, 4
CONV_WIDTH = 31
Q_A, KV_A, QKV_B, V_B = 512, 128, 1536, 512
IN_COLS = 2832
IN_COLS_PAD = 3072
OFF_QKVB = Q_A + 2 * KV_A
OFF_Z = OFF_QKVB + QKV_B
OFF_BETA = OFF_Z + V_B
OFF_A = OFF_BETA + DN_HEADS

FWD_CARRY_BYTES = {"F": 12 * 2**20, "A": 6 * 2**20, "E": 18 * 2**20, "O": 12 * 2**20}
BWD_CARRY_BYTES = {"F": 11 * 2**20, "A": 6 * 2**20, "E": 13 * 2**20, "O": 10 * 2**20}

ADAM_SLOT_BLOCK = 3 * 2**19

ADAM_LR, ADAM_B1, ADAM_B2, ADAM_EPS, ADAM_WD, ADAM_STEP = 0.001, 0.9, 0.999, 1e-08, 0.01, 10


def _cparams(sem):
    return pltpu.CompilerParams(dimension_semantics=sem, vmem_limit_bytes=V7X_VMEM_LIMIT)


def _sigmoid(x):
    return 1.0 / (1.0 + jnp.exp(-x))


def _softplus(x):
    return jnp.maximum(x, 0.0) + jnp.log(1.0 + jnp.exp(-jnp.abs(x)))


def _dot(a, b):
    return jnp.dot(a, b, preferred_element_type=F32)


def _dot_nt(a, b):
    return lax.dot_general(a, b, (((1,), (1,)), ((), ())), preferred_element_type=F32)


def _dot_tn(a, b):
    return lax.dot_general(a, b, (((0,), (0,)), ((), ())), preferred_element_type=F32)


def _rms(x, w):
    return x * lax.rsqrt(jnp.mean(x * x, axis=-1, keepdims=True) + EPS) * w


def _rms_bwd(x, w, dxn):
    r = lax.rsqrt(jnp.mean(x * x, axis=-1, keepdims=True) + EPS)
    xh = x * r
    dxh = dxn * w
    dx = r * (dxh - xh * jnp.mean(dxh * xh, axis=-1, keepdims=True))
    return dx, jnp.sum(dxn * xh, axis=0, keepdims=True)


def _position():
    return lax.axis_index("x"), lax.axis_index("y"), lax.axis_index("c")


def _dev_index(px, py, pc):
    return 4 * px + 2 * py + pc


def _rcopy(src, dst, send_sem, recv_sem, to):
    return pltpu.make_async_remote_copy(src_ref=src, dst_ref=dst, send_sem=send_sem, recv_sem=recv_sem,
                                        device_id=to, device_id_type=MESH)


def _ag_start(srcs, outs, send, recv, local):
    x, y, c = _position()
    me = _dev_index(x, y, c)
    chips = [(1 - x, y), (x, 1 - y), (1 - x, 1 - y)]
    for a, (src, out) in enumerate(zip(srcs, outs)):
        pltpu.make_async_copy(src, out.at[me], local.at[a]).start()
        _rcopy(src, out.at[me], send.at[a, 0], recv.at[a, 0], (x, y, 1 - c)).start()
        for j, chip in enumerate(chips):
            _rcopy(src, out.at[me], send.at[a, 1 + j], recv.at[a, 1 + j], (*chip, c)).start()


def _ag_finish(srcs, outs, send, recv, local):
    x, y, c = _position()
    me = _dev_index(x, y, c)
    sibling = (x, y, 1 - c)
    chips = [(1 - x, y), (x, 1 - y), (1 - x, 1 - y)]
    for j, chip in enumerate(chips):
        for a, out in enumerate(outs):
            blk = out.at[_dev_index(*chip, c)]
            _rcopy(blk, blk, send.at[a, 1 + j], recv.at[a, 1 + j], (x, y, c)).wait_recv()
            _rcopy(blk, blk, send.at[a, 4 + j], recv.at[a, 4 + j], sibling).start()
    for a, (src, out) in enumerate(zip(srcs, outs)):
        blk = out.at[_dev_index(x, y, 1 - c)]
        _rcopy(blk, blk, send.at[a, 0], recv.at[a, 0], (x, y, c)).wait_recv()
        for j, chip in enumerate(chips):
            blk = out.at[_dev_index(*chip, 1 - c)]
            _rcopy(blk, blk, send.at[a, 4 + j], recv.at[a, 4 + j], (x, y, c)).wait_recv()
        for k in range(N_DEV - 1):
            _rcopy(out.at[me], out.at[me], send.at[a, k], recv.at[a, k], (x, y, c)).wait_send()
        pltpu.make_async_copy(src, out.at[me], local.at[a]).wait()


def _rs_peer(r):
    x, y, c = _position()
    return x ^ ((r >> 2) & 1), y ^ ((r >> 1) & 1), c ^ (r & 1)


def _rs_start(ins, outs, send, recv, local):
    me = _dev_index(*_position())
    for a, (src, out) in enumerate(zip(ins, outs)):
        pltpu.make_async_copy(src.at[me], out.at[me], local.at[a]).start()
        for r in range(1, N_DEV):
            p = _rs_peer(r)
            _rcopy(src.at[_dev_index(*p)], out.at[me], send.at[a, r - 1], recv.at[a, r - 1], p).start()


def _rs_finish(ins, outs, send, recv, local):
    pos = _position()
    me = _dev_index(*pos)
    for a, (src, out) in enumerate(zip(ins, outs)):
        for r in range(1, N_DEV):
            blk = out.at[_dev_index(*_rs_peer(r))]
            _rcopy(blk, blk, send.at[a, r - 1], recv.at[a, r - 1], pos).wait_recv()
        for r in range(1, N_DEV):
            _rcopy(src.at[me], out.at[me], send.at[a, r - 1], recv.at[a, r - 1], pos).wait_send()
        pltpu.make_async_copy(src.at[me], out.at[me], local.at[a]).wait()


def _sw_start(ins, outs, send, recv):
    x, y, c = _position()
    for a, (src, out) in enumerate(zip(ins, outs)):
        for q in range(N_CHIP):
            _rcopy(src.at[2 * q + (1 - c)], out.at[q], send.at[a, q], recv.at[a, q], (x, y, 1 - c)).start()


def _sw_finish(ins, outs, send, recv):
    pos = _position()
    for a, out in enumerate(outs):
        for q in range(N_CHIP):
            _rcopy(out.at[q], out.at[q], send.at[a, q], recv.at[a, q], pos).wait_recv()
        for q in range(N_CHIP):
            _rcopy(out.at[q], out.at[q], send.at[a, q], recv.at[a, q], pos).wait_send()


def _r4_peer(r):
    x, y, c = _position()
    return x ^ ((r >> 1) & 1), y ^ (r & 1), c


def _r4_start(ins, outs, send, recv, local):
    x, y, c = _position()
    mine = 2 * x + y
    for a, (src, out) in enumerate(zip(ins, outs)):
        pltpu.make_async_copy(src.at[mine], out.at[mine], local.at[a]).start()
        for r in range(1, N_CHIP):
            px, py, pc = _r4_peer(r)
            _rcopy(src.at[2 * px + py], out.at[mine], send.at[a, r - 1], recv.at[a, r - 1], (px, py, pc)).start()


def _r4_finish(ins, outs, send, recv, local):
    x, y, c = _position()
    mine = 2 * x + y
    for a, (src, out) in enumerate(zip(ins, outs)):
        for r in range(1, N_CHIP):
            px, py, _ = _r4_peer(r)
            blk = out.at[2 * px + py]
            _rcopy(blk, blk, send.at[a, r - 1], recv.at[a, r - 1], (x, y, c)).wait_recv()
        for r in range(1, N_CHIP):
            _rcopy(src.at[mine], out.at[mine], send.at[a, r - 1], recv.at[a, r - 1], (x, y, c)).wait_send()
        pltpu.make_async_copy(src.at[mine], out.at[mine], local.at[a]).wait()


_RS_KINDS = {
    "direct": (_rs_start, _rs_finish, lambda n: [(n, N_DEV - 1), (n, N_DEV - 1), (n,)], lambda s: s),
    "swap": (_sw_start, _sw_finish, lambda n: [(n, N_CHIP), (n, N_CHIP)], lambda s: (N_CHIP,) + s[1:]),
    "chips": (_r4_start, _r4_finish, lambda n: [(n, N_CHIP - 1), (n, N_CHIP - 1), (n,)], lambda s: s),
}


def _pcall(body, args, *, name, grid, in_specs, out_specs, out_shape, sem, scratch_shapes=(), ag=(), rs=()):
    na, nr = len(ag), len(rs)
    if na + nr == 0:
        outs = pl.pallas_call(body, name=name, grid=grid, in_specs=in_specs, out_specs=out_specs,
                              out_shape=out_shape, scratch_shapes=list(scratch_shapes),
                              compiler_params=_cparams(sem))(*args)
        return list(outs), [], []
    n_in, n_out, n_scr = len(in_specs), len(out_specs), len(scratch_shapes)
    ag_idx = [i for _, i in ag]
    groups = [(k, [i for i, (kk, _) in enumerate(rs) if kk == k]) for k in _RS_KINDS]
    groups = [(k, idx) for k, idx in groups if idx]
    sem_counts = ([3] if na else []) + [len(_RS_KINDS[k][2](1)) for k, _ in groups]

    def wrapped(*refs):
        cin, refs = refs[:n_in], refs[n_in:]
        ag_in, refs = refs[:na], refs[na:]
        rs_in, refs = refs[:nr], refs[nr:]
        cout, refs = refs[:n_out], refs[n_out:]
        ag_out, refs = refs[:na], refs[na:]
        rs_out, refs = refs[:nr], refs[nr:]
        cscr, sems = refs[:n_scr], list(refs[n_scr:])
        sem_sets = [[sems.pop(0) for _ in range(n)] for n in sem_counts]
        ag_sems = sem_sets.pop(0) if na else None
        ag_src = [r if i is None else r.at[i] for r, i in zip(ag_in, ag_idx)]
        ids = [pl.program_id(d) for d in range(len(grid))]
        first = functools.reduce(jnp.logical_and, [i == 0 for i in ids])
        last = functools.reduce(jnp.logical_and, [i == g - 1 for i, g in zip(ids, grid)])

        def run(phase):
            if na:
                (_ag_start, _ag_finish)[phase](ag_src, ag_out, *ag_sems)
            for (k, idx), ss in zip(groups, sem_sets):
                _RS_KINDS[k][phase]([rs_in[i] for i in idx], [rs_out[i] for i in idx], *ss)

        @pl.when(first)
        def _():
            run(0)

        body(*cin, *cout, *cscr)

        @pl.when(last)
        def _():
            run(1)

    hbm = pl.BlockSpec(memory_space=pl.ANY)
    sem_shapes = [pltpu.SemaphoreType.DMA(s) for s in ([(na, N_DEV - 1), (na, N_DEV - 1), (na,)] if na else [])]
    for k, idx in groups:
        sem_shapes += [pltpu.SemaphoreType.DMA(s) for s in _RS_KINDS[k][2](len(idx))]
    outs = pl.pallas_call(
        wrapped, name=name, grid=grid,
        in_specs=list(in_specs) + [hbm] * (na + nr),
        out_specs=list(out_specs) + [hbm] * (na + nr),
        out_shape=list(out_shape)
        + [jax.ShapeDtypeStruct((N_DEV,) + a.shape[-2:], a.dtype) for a, _ in ag]
        + [jax.ShapeDtypeStruct(_RS_KINDS[k][3](b.shape), b.dtype) for k, b in rs],
        scratch_shapes=list(scratch_shapes) + sem_shapes,
        compiler_params=_cparams(sem),
    )(*args, *[a for a, _ in ag], *[b for _, b in rs])
    return list(outs[:n_out]), list(outs[n_out:n_out + na]), list(outs[n_out + na:])


def exchange(ag):
    def body(o_ref):
        o_ref[...] = jnp.zeros_like(o_ref)

    _, gathered, _ = _pcall(body, (), name="exchange", grid=(1,), in_specs=[],
                            out_specs=[pl.BlockSpec((8, LANES), lambda i: (0, 0))],
                            out_shape=[jax.ShapeDtypeStruct((8, LANES), F32)], sem=("arbitrary",), ag=ag)
    return gathered


def pair_add(blocks, received):
    n = len(blocks)

    def body(core_ref, *refs):
        for g_ref, p_ref, o_ref in zip(refs[:n], refs[n:2 * n], refs[2 * n:]):
            o_ref[0] = (g_ref[0, 0].astype(F32) + p_ref[0].astype(F32)).astype(BF16)

    halves = 2
    g_specs = [pl.BlockSpec((1, 1, b.shape[1] // halves, b.shape[2]), lambda q, r, core: (q, core[0], r, 0))
               for b in blocks]
    p_specs = [pl.BlockSpec((1, b.shape[1] // halves, b.shape[2]), lambda q, r, core: (q, r, 0)) for b in blocks]
    return pl.pallas_call(
        body, name="pair_add",
        grid_spec=pltpu.PrefetchScalarGridSpec(num_scalar_prefetch=1, grid=(N_CHIP, halves),
                                               in_specs=g_specs + p_specs, out_specs=p_specs),
        out_shape=[jax.ShapeDtypeStruct(p.shape, BF16) for p in received],
        compiler_params=_cparams(("parallel", "parallel")),
    )(lax.axis_index("c").astype(jnp.int32)[None], *[b.reshape((N_CHIP, 2) + b.shape[1:]) for b in blocks], *received)


FFN_PAIR = 2


def _pair_cols(w_ref):
    return jnp.concatenate([w_ref[p] for p in range(FFN_PAIR)], axis=1)


def ffn_fwd(x, nw, wg, wu, wd, ag=()):
    T, D = x.shape
    F = wg.shape[2]
    P = FFN_PAIR
    J = wg.shape[0] // P
    tm = min(T, 1024)

    def body(x_ref, nw_ref, wg_ref, wu_ref, wd_ref, o_ref, xn_ref, acc_ref):
        j = pl.program_id(1)

        @pl.when(j == 0)
        def _():
            xn_ref[...] = _rms(x_ref[...], nw_ref[...]).astype(BF16)
            acc_ref[...] = jnp.zeros_like(acc_ref)

        xn = xn_ref[...]
        g = _dot(xn, _pair_cols(wg_ref))
        u = _dot(xn, _pair_cols(wu_ref))
        h = (g * _sigmoid(g) * u).astype(BF16)
        acc_ref[...] += _dot(h, wd_ref[...].reshape(P * F, D))

        @pl.when(j == J - 1)
        def _():
            o_ref[...] = x_ref[...] + 0.5 * acc_ref[...]

    (out,), gathered, _ = _pcall(
        body, (x, nw, wg, wu, wd), name="ffn_fwd", grid=(T // tm, J),
        in_specs=[pl.BlockSpec((tm, D), lambda t, j: (t, 0)),
                  pl.BlockSpec((1, D), lambda t, j: (0, 0)),
                  pl.BlockSpec((P, D, F), lambda t, j: (j, 0, 0)),
                  pl.BlockSpec((P, D, F), lambda t, j: (j, 0, 0)),
                  pl.BlockSpec((P, F, D), lambda t, j: (j, 0, 0))],
        out_specs=[pl.BlockSpec((tm, D), lambda t, j: (t, 0))],
        out_shape=[jax.ShapeDtypeStruct((T, D), F32)],
        scratch_shapes=[pltpu.VMEM((tm, D), BF16), pltpu.VMEM((tm, D), F32)],
        sem=("arbitrary", "arbitrary"), ag=ag)
    return out, gathered


def ffn_bwd(x, dy, nw, wg, wu, wd, rs=()):
    T, D = x.shape
    F = wg.shape[2]
    P = FFN_PAIR
    J = wg.shape[0] // P
    tm = min(T, 256)
    nt = T // tm

    def body(x_ref, dy_ref, nw_ref, wg_ref, wu_ref, wd_ref,
             dx_ref, dwg_ref, dwu_ref, dwd_ref, dnw_ref,
             xn_ref, dyh_ref, dxn_ref, awg_ref, awu_ref, awd_ref):
        j = pl.program_id(0)
        t = pl.program_id(1)
        rows = pl.ds(pl.multiple_of(t * tm, tm), tm)

        @pl.when(j == 0)
        def _():
            xn_ref[rows, :] = _rms(x_ref[...], nw_ref[...]).astype(BF16)
            dyh_ref[rows, :] = (0.5 * dy_ref[...]).astype(BF16)
            dxn_ref[rows, :] = jnp.zeros((tm, D), F32)

        @pl.when((j == 0) & (t == 0))
        def _():
            dnw_ref[...] = jnp.zeros_like(dnw_ref)

        @pl.when(t == 0)
        def _():
            awg_ref[...] = jnp.zeros_like(awg_ref)
            awu_ref[...] = jnp.zeros_like(awu_ref)
            awd_ref[...] = jnp.zeros_like(awd_ref)

        xn = xn_ref[rows, :]
        dyh = dyh_ref[rows, :]
        wg2, wu2 = _pair_cols(wg_ref), _pair_cols(wu_ref)
        g = _dot(xn, wg2)
        u = _dot(xn, wu2)
        sg = _sigmoid(g)
        s = g * sg
        h = (s * u).astype(BF16)
        dh = _dot_nt(dyh, wd_ref[...].reshape(P * F, D))
        du = (dh * s).astype(BF16)
        dg = (dh * u * (sg * (1.0 + g * (1.0 - sg)))).astype(BF16)
        awd_ref[...] += _dot_tn(h, dyh)
        awg_ref[...] += _dot_tn(dg, xn)
        awu_ref[...] += _dot_tn(du, xn)
        dxn_ref[rows, :] += _dot_nt(dg, wg2) + _dot_nt(du, wu2)

        @pl.when(t == nt - 1)
        def _():
            dwg_ref[...] = awg_ref[...].astype(BF16).reshape(P, F, D)
            dwu_ref[...] = awu_ref[...].astype(BF16).reshape(P, F, D)
            dwd_ref[...] = awd_ref[...].astype(BF16).reshape(P, F, D)

        @pl.when(j == J - 1)
        def _():
            dx, dnw = _rms_bwd(x_ref[...], nw_ref[...], dxn_ref[rows, :])
            dx_ref[...] = dy_ref[...] + dx
            dnw_ref[...] += dnw

    ends = lambda j, t: (jnp.where((j == 0) | (j == J - 1), t, 0), 0)
    last = lambda j, t: (jnp.where(j == J - 1, t, 0), 0)
    outs, _, slots = _pcall(
        body, (x, dy, nw, wg, wu, wd), name="ffn_bwd", grid=(J, nt),
        in_specs=[pl.BlockSpec((tm, D), ends), pl.BlockSpec((tm, D), ends),
                  pl.BlockSpec((1, D), lambda j, t: (0, 0)),
                  pl.BlockSpec((P, D, F), lambda j, t: (j, 0, 0)),
                  pl.BlockSpec((P, D, F), lambda j, t: (j, 0, 0)),
                  pl.BlockSpec((P, F, D), lambda j, t: (j, 0, 0))],
        out_specs=[pl.BlockSpec((tm, D), last),
                   pl.BlockSpec((P, F, D), lambda j, t: (j, 0, 0)),
                   pl.BlockSpec((P, F, D), lambda j, t: (j, 0, 0)),
                   pl.BlockSpec((P, F, D), lambda j, t: (j, 0, 0)),
                   pl.BlockSpec((1, D), lambda j, t: (0, 0))],
        out_shape=[jax.ShapeDtypeStruct((T, D), F32)] + [jax.ShapeDtypeStruct((P * J, F, D), BF16)] * 3
        + [jax.ShapeDtypeStruct((1, D), F32)],
        scratch_shapes=[pltpu.VMEM((T, D), BF16), pltpu.VMEM((T, D), BF16), pltpu.VMEM((T, D), F32)]
        + [pltpu.VMEM((P * F, D), F32)] * 3,
        sem=("arbitrary", "arbitrary"), rs=rs)
    return outs, slots


def rmslin_fwd(x, nw, w, b):
    T, D = x.shape
    N = w.shape[1]
    tm = min(T, 256)

    def body(x_ref, nw_ref, w_ref, b_ref, o_ref):
        xn = _rms(x_ref[...], nw_ref[...]).astype(BF16)
        o_ref[...] = _dot(xn, w_ref[...]) + b_ref[...]

    return pl.pallas_call(
        body, name="rmslin_fwd", grid=(T // tm,),
        in_specs=[pl.BlockSpec((tm, D), lambda t: (t, 0)), pl.BlockSpec((1, D), lambda t: (0, 0)),
                  pl.BlockSpec((D, N), lambda t: (0, 0)), pl.BlockSpec((1, N), lambda t: (0, 0))],
        out_specs=pl.BlockSpec((tm, N), lambda t: (t, 0)),
        out_shape=jax.ShapeDtypeStruct((T, N), F32),
        compiler_params=_cparams(("parallel",)),
    )(x, nw, w, b)


def rmslin_bwd(x, dres, dproj, nw, w):
    T, D = x.shape
    N = w.shape[1]
    pieces = list(dproj) if isinstance(dproj, (list, tuple)) else [dproj]
    nb = DN_ZCOLS if len(pieces) > 1 else 1024
    nc = N // nb
    tm = min(T, 256)
    nt = T // tm
    n_p = len(pieces)

    def body(x_ref, dres_ref, *refs):
        p_refs, (nw_ref, w_ref, dx_ref, dw_ref, db_ref, dnw_ref, xn_ref, dxn_ref, acc_ref) = refs[:n_p], refs[n_p:]
        c = pl.program_id(0)
        t = pl.program_id(1)
        rows = pl.ds(pl.multiple_of(t * tm, tm), tm)

        @pl.when(c == 0)
        def _():
            xn_ref[rows, :] = _rms(x_ref[...], nw_ref[...]).astype(BF16)
            dxn_ref[rows, :] = jnp.zeros((tm, D), F32)

        @pl.when((c == 0) & (t == 0))
        def _():
            dnw_ref[...] = jnp.zeros_like(dnw_ref)

        @pl.when(t == 0)
        def _():
            acc_ref[...] = jnp.zeros_like(acc_ref)
            db_ref[...] = jnp.zeros_like(db_ref)

        if n_p == 1:
            dpf = p_refs[0][...]
        else:
            dq_ref, dkv_ref, dqkv_ref, dz_ref = p_refs
            dpf = jnp.where(c == 0, jnp.concatenate([dq_ref[...], dkv_ref[...]], axis=1),
                            jnp.where(c == nc - 1, dz_ref[...], dqkv_ref[...]))
        dp = dpf.astype(BF16)
        acc_ref[...] += _dot_tn(xn_ref[rows, :], dp)
        db_ref[...] += jnp.sum(dpf, axis=0, keepdims=True)
        dxn_ref[rows, :] += _dot_nt(dp, w_ref[...])

        @pl.when(t == nt - 1)
        def _():
            dw_ref[...] = acc_ref[...].astype(BF16)

        @pl.when(c == nc - 1)
        def _():
            dx, dnw = _rms_bwd(x_ref[...], nw_ref[...], dxn_ref[rows, :])
            dx_ref[...] = dres_ref[...] + dx
            dnw_ref[...] += dnw

    ends = lambda c, t: (jnp.where((c == 0) | (c == nc - 1), t, 0), 0)
    last = lambda c, t: (jnp.where(c == nc - 1, t, 0), 0)
    first = lambda c, t: (jnp.where(c == 0, t, 0), 0)
    if n_p == 1:
        p_specs = [pl.BlockSpec((tm, nb), lambda c, t: (t, c))]
    else:
        p_specs = [pl.BlockSpec((tm, Q_A), first), pl.BlockSpec((tm, 2 * KV_A), first),
                   pl.BlockSpec((tm, nb), lambda c, t: (jnp.where((c > 0) & (c < nc - 1), t, 0),
                                                        jnp.clip(c - 1, 0, 1))),
                   pl.BlockSpec((tm, nb), last)]
    return pl.pallas_call(
        body, name="rmslin_bwd", grid=(nc, nt),
        in_specs=[pl.BlockSpec((tm, D), ends), pl.BlockSpec((tm, D), last)] + p_specs
        + [pl.BlockSpec((1, D), lambda c, t: (0, 0)), pl.BlockSpec((D, nb), lambda c, t: (0, c))],
        out_specs=[pl.BlockSpec((tm, D), last),
                   pl.BlockSpec((D, nb), lambda c, t: (0, c)),
                   pl.BlockSpec((1, nb), lambda c, t: (0, c)),
                   pl.BlockSpec((1, D), lambda c, t: (0, 0))],
        out_shape=[jax.ShapeDtypeStruct((T, D), F32), jax.ShapeDtypeStruct((D, N), BF16),
                   jax.ShapeDtypeStruct((1, N), F32), jax.ShapeDtypeStruct((1, D), F32)],
        scratch_shapes=[pltpu.VMEM((T, D), BF16), pltpu.VMEM((T, D), F32), pltpu.VMEM((D, nb), F32)],
        compiler_params=_cparams(("arbitrary", "arbitrary")),
    )(x, dres, *pieces, nw, w)


def lin_fwd(res, parts, w, b):
    T = res.shape[0]
    K, N = w.shape
    tm = min(T, 512)
    n = len(parts)
    offs = [sum(p.shape[1] for p in parts[:i]) for i in range(n + 1)]

    def body(res_ref, *refs):
        a_refs, (w_ref, b_ref, o_ref) = refs[:n], refs[n:]
        acc = res_ref[...] + b_ref[...]
        for i, a_ref in enumerate(a_refs):
            acc = acc + _dot(a_ref[...].astype(BF16), w_ref[offs[i]:offs[i + 1], :])
        o_ref[...] = acc

    return pl.pallas_call(
        body, name="lin_fwd", grid=(T // tm,),
        in_specs=[pl.BlockSpec((tm, N), lambda t: (t, 0))]
        + [pl.BlockSpec((tm, p.shape[1]), lambda t: (t, 0)) for p in parts]
        + [pl.BlockSpec((K, N), lambda t: (0, 0)), pl.BlockSpec((1, N), lambda t: (0, 0))],
        out_specs=pl.BlockSpec((tm, N), lambda t: (t, 0)),
        out_shape=jax.ShapeDtypeStruct((T, N), F32),
        compiler_params=_cparams(("parallel",)),
    )(res, *parts, w, b)


def lin_bwd(parts, dy, w):
    T = dy.shape[0]
    K, N = w.shape
    tm = min(T, 256)
    nt = T // tm
    n = len(parts)
    offs = [sum(p.shape[1] for p in parts[:i]) for i in range(n + 1)]

    def body(*refs):
        a_refs, (dy_ref, w_ref, da_ref, dw_ref, db_ref, acc_ref) = refs[:n], refs[n:]

        @pl.when(pl.program_id(0) == 0)
        def _():
            acc_ref[...] = jnp.zeros_like(acc_ref)
            db_ref[...] = jnp.zeros_like(db_ref)

        dyf = dy_ref[...]
        dyb = dyf.astype(BF16)
        da_ref[...] = _dot_nt(dyb, w_ref[...])
        for i, a_ref in enumerate(a_refs):
            acc_ref[offs[i]:offs[i + 1], :] += _dot_tn(a_ref[...].astype(BF16), dyb)
        db_ref[...] += jnp.sum(dyf, axis=0, keepdims=True)

        @pl.when(pl.program_id(0) == nt - 1)
        def _():
            dw_ref[...] = acc_ref[...].astype(BF16)

    return pl.pallas_call(
        body, name="lin_bwd", grid=(nt,),
        in_specs=[pl.BlockSpec((tm, p.shape[1]), lambda t: (t, 0)) for p in parts]
        + [pl.BlockSpec((tm, N), lambda t: (t, 0)), pl.BlockSpec((K, N), lambda t: (0, 0))],
        out_specs=[pl.BlockSpec((tm, K), lambda t: (t, 0)), pl.BlockSpec((K, N), lambda t: (0, 0)),
                   pl.BlockSpec((1, N), lambda t: (0, 0))],
        out_shape=[jax.ShapeDtypeStruct((T, K), F32), jax.ShapeDtypeStruct((K, N), BF16),
                   jax.ShapeDtypeStruct((1, N), F32)],
        scratch_shapes=[pltpu.VMEM((K, N), F32)],
        compiler_params=_cparams(("arbitrary",)),
    )(*parts, dy, w)


def loss_fwd_bwd(x, fw, target):
    T, D = x.shape
    tm = min(T, 256)

    def body(x_ref, fw_ref, tg_ref, loss_ref, dx_ref, dfw_ref):
        @pl.when(pl.program_id(0) == 0)
        def _():
            loss_ref[...] = jnp.zeros_like(loss_ref)
            dfw_ref[...] = jnp.zeros_like(dfw_ref)

        xv = x_ref[...]
        w = fw_ref[...]
        err = _rms(xv, w) - tg_ref[...]
        row = jnp.sum(err * err, axis=-1, keepdims=True)
        loss_ref[...] += (0.5 / D) * jnp.sum(row, axis=0, keepdims=True)
        dx, dfw = _rms_bwd(xv, w, err * (1.0 / D))
        dx_ref[...] = dx
        dfw_ref[...] += dfw

    return pl.pallas_call(
        body, name="loss_fwd_bwd", grid=(T // tm,),
        in_specs=[pl.BlockSpec((tm, D), lambda t: (t, 0)), pl.BlockSpec((1, D), lambda t: (0, 0)),
                  pl.BlockSpec((tm, D), lambda t: (t, 0))],
        out_specs=[pl.BlockSpec((1, 1), lambda t: (0, 0)), pl.BlockSpec((tm, D), lambda t: (t, 0)),
                   pl.BlockSpec((1, D), lambda t: (0, 0))],
        out_shape=[jax.ShapeDtypeStruct((1, 1), F32), jax.ShapeDtypeStruct((T, D), F32),
                   jax.ShapeDtypeStruct((1, D), F32)],
        compiler_params=_cparams(("arbitrary",)),
    )(x, fw, target)


def _attn_masks(n, rows, blk):
    r = lax.broadcasted_iota(jnp.int32, (rows, 2 * blk), 0)
    jj = lax.broadcasted_iota(jnp.int32, (rows, 2 * blk), 1)
    dist = (r % blk) + blk - jj
    valid = (dist >= 0) & (dist < blk) & ((n > 0) | (jj >= blk))
    return dist.astype(F32), valid


def _attn_block(q, kcat, vcat, sink, slope, dist, valid):
    d = q.shape[-1]
    s = _dot_nt(q.astype(BF16), kcat.astype(BF16)) * (d ** -0.5)
    s = jnp.where(valid, s - slope * dist, -1e30)
    m = lax.stop_gradient(jnp.maximum(jnp.max(s, axis=-1, keepdims=True), sink))
    e = jnp.exp(s - m)
    p = e / (jnp.sum(e, axis=-1, keepdims=True) + jnp.exp(sink - m))
    return _dot(p.astype(BF16), vcat.astype(BF16))


ATTN_G = ATTN_HEADS // ATTN_KV_HEADS
ATTN_QW = ATTN_G * HEAD_DIM
ATTN_KCOL = Q_A // KV_A


def _attn_specs():
    blk = ATTN_BLOCK
    qs = pl.BlockSpec((blk, ATTN_QW), lambda h, n: (n, h))
    prev = lambda c: pl.BlockSpec((blk, KV_A), lambda h, n: (jnp.maximum(n - 1, 0), c))
    cur = lambda c: pl.BlockSpec((blk, KV_A), lambda h, n: (n, c))
    rowp = pl.BlockSpec((ATTN_G * blk, 1), lambda h, n: (h, 0))
    return qs, [prev(ATTN_KCOL), cur(ATTN_KCOL), prev(ATTN_KCOL + 1), cur(ATTN_KCOL + 1)], rowp


def _attn_operands(h, q_ref, kp_ref, kc_ref, vp_ref, vc_ref):
    d = HEAD_DIM
    q = jnp.concatenate([q_ref[:, g * d:(g + 1) * d] for g in range(ATTN_G)], axis=0)
    pick = lambda r: jnp.where(h == 0, r[:, :d], r[:, d:])
    kcat = jnp.concatenate([pick(kp_ref[...]), pick(kc_ref[...])], axis=0)
    vcat = jnp.concatenate([pick(vp_ref[...]), pick(vc_ref[...])], axis=0)
    return q, kcat, vcat


def attn_fwd(proj, sink_rows, slope_rows, ag=()):
    T = proj.shape[0]
    blk, d = ATTN_BLOCK, HEAD_DIM

    def body(q_ref, kp_ref, kc_ref, vp_ref, vc_ref, sink_ref, slope_ref, o_ref):
        h, n = pl.program_id(0), pl.program_id(1)
        dist, valid = _attn_masks(n, ATTN_G * blk, blk)
        q, kcat, vcat = _attn_operands(h, q_ref, kp_ref, kc_ref, vp_ref, vc_ref)
        o = _attn_block(q, kcat, vcat, sink_ref[...], slope_ref[...], dist, valid)
        for g in range(ATTN_G):
            o_ref[:, g * d:(g + 1) * d] = o[g * blk:(g + 1) * blk]

    qs, kv, rowp = _attn_specs()
    (out,), gathered, _ = _pcall(
        body, (proj, proj, proj, proj, proj, sink_rows, slope_rows), name="attn_fwd",
        grid=(ATTN_KV_HEADS, T // blk), in_specs=[qs] + kv + [rowp, rowp], out_specs=[qs],
        out_shape=[jax.ShapeDtypeStruct((T, Q_A), F32)], sem=("arbitrary", "arbitrary"), ag=ag)
    return out, gathered


def attn_bwd(proj, sink_rows, slope_rows, dmix, rs=()):
    T = proj.shape[0]
    blk, d = ATTN_BLOCK, HEAD_DIM

    def body(q_ref, kp_ref, kc_ref, vp_ref, vc_ref, sink_ref, slope_ref, do_ref, dq_ref, dkv_ref, dsink_ref):
        h, n = pl.program_id(0), pl.program_id(1)

        @pl.when((h == 0) & (n == 0))
        def _():
            dkv_ref[...] = jnp.zeros_like(dkv_ref)

        @pl.when(n == 0)
        def _():
            dsink_ref[...] = jnp.zeros_like(dsink_ref)

        dist, valid = _attn_masks(n, ATTN_G * blk, blk)
        q, kcat, vcat = _attn_operands(h, q_ref, kp_ref, kc_ref, vp_ref, vc_ref)
        do = jnp.concatenate([do_ref[:, g * d:(g + 1) * d] for g in range(ATTN_G)], axis=0)
        fn = functools.partial(_attn_block, slope=slope_ref[...], dist=dist, valid=valid)
        _, vjp = jax.vjp(fn, q, kcat, vcat, sink_ref[...])
        dq, dkcat, dvcat, dsink = vjp(do)
        for g in range(ATTN_G):
            dq_ref[:, g * d:(g + 1) * d] = dq[g * blk:(g + 1) * blk]
        dsink_ref[...] += dsink
        lane = lax.broadcasted_iota(jnp.int32, (2 * blk, 2 * KV_A), 1)
        mine = (lane % KV_A) // d == h
        both = jnp.where(mine, jnp.concatenate([dkcat, dkcat, dvcat, dvcat], axis=1), 0.0)

        @pl.when(n == 0)
        def _():
            dkv_ref[0:blk, :] += both[blk:]

        @pl.when(n > 0)
        def _():
            rows = pl.ds(pl.multiple_of((n - 1) * blk, blk), 2 * blk)
            dkv_ref[rows, :] += both

    qs, kv, rowp = _attn_specs()
    outs, _, slots = _pcall(
        body, (proj, proj, proj, proj, proj, sink_rows, slope_rows, dmix), name="attn_bwd",
        grid=(ATTN_KV_HEADS, T // blk), in_specs=[qs] + kv + [rowp, rowp, qs],
        out_specs=[qs, pl.BlockSpec((T, 2 * KV_A), lambda h, n: (0, 0)), rowp],
        out_shape=[jax.ShapeDtypeStruct((T, Q_A), F32), jax.ShapeDtypeStruct((T, 2 * KV_A), F32),
                   jax.ShapeDtypeStruct((ATTN_HEADS * blk, 1), F32)],
        sem=("arbitrary", "arbitrary"), rs=rs)
    return outs, slots


_NN = (((2,), (1,)), ((0,), (0,)))
_NT = (((2,), (2,)), ((0,), (0,)))
_TN = (((1,), (1,)), ((0,), (0,)))


def _bmm(a, b, dims):
    return lax.dot_general(a.astype(BF16), b.astype(BF16), dims, preferred_element_type=F32)


def _split(x, terms):
    out = []
    for _ in range(terms):
        t = x.astype(BF16)
        out.append(t)
        x = x - t.astype(F32)
    return out


def _fine_product(a, b, dims):
    (ah, al), (bh, bl) = _split(a, 2), _split(b, 2)
    dot = lambda x, y: lax.dot_general(x, y, dims, preferred_element_type=F32)
    return dot(ah, bh) + (dot(ah, bl) + dot(al, bh))


def _mask_product(mask, x, dims):
    mb = mask.astype(BF16)
    parts = [lax.dot_general(mb, t, dims, preferred_element_type=F32) for t in _split(x, 3)]
    return parts[0] + (parts[1] + parts[2])


@jax.custom_vjp
def _fine_nt(a, b):
    return _fine_product(a, b, _NT)


_fine_nt.defvjp(lambda a, b: (_fine_product(a, b, _NT), (a, b)),
                lambda res, ct: (_fine_product(ct, res[1], _NN), _fine_product(ct, res[0], _TN)))


@jax.custom_vjp
def _mask_nn(mask, x):
    return _mask_product(mask, x, _NN)


_mask_nn.defvjp(lambda mask, x: (_mask_product(mask, x, _NN), mask),
                lambda mask, ct: (jnp.zeros_like(mask), _mask_product(mask, ct, _TN)))


@jax.custom_vjp
def _unit_lower_inverse(low):
    n = low.shape[-1]
    eye = (lax.broadcasted_iota(jnp.int32, low.shape, 1) == lax.broadcasted_iota(jnp.int32, low.shape, 2)).astype(F32)
    tinv = eye - low
    p = low
    for _ in range(n.bit_length() - 2):
        p = _bmm(p, p, _NN)
        tinv = tinv + _bmm(tinv, p, _NN)
    return tinv


def _unit_lower_inverse_fwd(low):
    tinv = _unit_lower_inverse(low)
    return tinv, tinv


_unit_lower_inverse.defvjp(_unit_lower_inverse_fwd, lambda tinv, ct: (-_bmm(_bmm(tinv, ct, _TN), tinv, _NT),))


def _dn_chunk(qc, kc, vc, zc, braw, araw, alog, dtb, nw, S):
    H, C, D = qc.shape
    row = lax.broadcasted_iota(jnp.int32, (H, C, C), 1)
    col = lax.broadcasted_iota(jnp.int32, (H, C, C), 2)
    causal = row >= col
    strict = row > col
    eye = (row == col).astype(F32)

    q = qc * lax.rsqrt(jnp.sum(qc * qc, axis=-1, keepdims=True) + EPS) * (D ** -0.5)
    k = kc * lax.rsqrt(jnp.sum(kc * kc, axis=-1, keepdims=True) + EPS)
    beta = _sigmoid(braw)
    g = -jnp.exp(alog) * _softplus(araw + dtb)
    a_col = _mask_nn(causal.astype(F32), jnp.broadcast_to(g, (H, C, C)))
    a_row = _mask_nn(jnp.ones((H, C, C), F32), eye * a_col)
    decay = jnp.where(causal, jnp.exp(jnp.where(causal, a_col - a_row, 0.0)), 0.0)
    kb = k * beta
    tinv = _unit_lower_inverse(jnp.where(strict, _fine_nt(kb, k) * decay, 0.0))
    e_col = jnp.exp(a_col)
    u = _bmm(tinv, vc * beta, _NN)
    w = _bmm(tinv, kb * e_col, _NN)
    attn = _fine_nt(q, k) * decay
    gl = a_col[:, C - 1:C, :]
    k_dec = k * jnp.exp(gl - a_col)
    v_new = u - _bmm(w, S, _NN)
    o = _bmm(q * e_col, S, _NN) + _bmm(attn, v_new, _NN)
    s_new = S * jnp.exp(jnp.broadcast_to(gl, (H, D, D))) + _bmm(k_dec, v_new, _TN)
    on = o * lax.rsqrt(jnp.mean(o * o, axis=-1, keepdims=True) + EPS) * nw
    return on * (zc * _sigmoid(zc)), s_new


DN_ZCOLS = IN_COLS_PAD - OFF_Z
DN_ZBLK = OFF_Z // DN_ZCOLS


def _dn_heads(a, off):
    return jnp.stack([a[:, off + h * DN_D:off + (h + 1) * DN_D] for h in range(DN_HEADS)])


def _dn_gate_cols(zb, off):
    return jnp.stack([zb[:, off + h:off + h + 1] for h in range(DN_HEADS)])


DN_STEP_CHUNKS = 4


def _dn_operands(x_ref, zb_ref, rows):
    x, zb = x_ref[rows, :], zb_ref[rows, :]
    return (_dn_heads(x, 0), _dn_heads(x, V_B), _dn_heads(x, 2 * V_B), _dn_heads(zb, 0),
            _dn_gate_cols(zb, V_B), _dn_gate_cols(zb, V_B + DN_HEADS))


def dn_fwd(qkvc, proj, alog, dtb, nw, ag=()):
    T = qkvc.shape[0]
    H, C, D, G = DN_HEADS, DN_CHUNK, DN_D, DN_STEP_CHUNKS
    N = T // C

    def body(x_ref, zb_ref, alog_ref, dtb_ref, nw_ref, o_ref, sall_ref, s_ref):
        @pl.when(pl.program_id(0) == 0)
        def _():
            s_ref[...] = jnp.zeros_like(s_ref)

        s = s_ref[...]
        for c in range(G):
            rows = slice(c * C, (c + 1) * C)
            sall_ref[c] = s
            on, s = _dn_chunk(*_dn_operands(x_ref, zb_ref, rows), alog_ref[...], dtb_ref[...], nw_ref[...], s)
            for h in range(H):
                o_ref[rows, h * D:(h + 1) * D] = on[h]
        s_ref[...] = s

    par = pl.BlockSpec((H, 1, 1), lambda n: (0, 0, 0))
    outs, gathered, _ = _pcall(
        body, (qkvc, proj, alog, dtb, nw), name="dn_fwd", grid=(N // G,),
        in_specs=[pl.BlockSpec((G * C, QKV_B), lambda n: (n, 0)),
                  pl.BlockSpec((G * C, DN_ZCOLS), lambda n: (n, DN_ZBLK)),
                  par, par, pl.BlockSpec((1, 1, D), lambda n: (0, 0, 0))],
        out_specs=[pl.BlockSpec((G * C, V_B), lambda n: (n, 0)), pl.BlockSpec((G, H, D, D), lambda n: (n, 0, 0, 0))],
        out_shape=[jax.ShapeDtypeStruct((T, V_B), F32), jax.ShapeDtypeStruct((N, H, D, D), F32)],
        scratch_shapes=[pltpu.VMEM((H, D, D), F32)], sem=("arbitrary",), ag=ag)
    return outs, gathered


def dn_bwd(qkvc, proj, alog, dtb, nw, sall, dmix, rs=()):
    T = qkvc.shape[0]
    H, C, D, G = DN_HEADS, DN_CHUNK, DN_D, DN_STEP_CHUNKS
    N = T // C // G

    def body(x_ref, zb_ref, alog_ref, dtb_ref, nw_ref, sall_ref, do_ref,
             dx_ref, dzb_ref, dalog_ref, ddtb_ref, dnw_ref, ds_ref):
        @pl.when(pl.program_id(0) == 0)
        def _():
            ds_ref[...] = jnp.zeros_like(ds_ref)
            dalog_ref[...] = jnp.zeros_like(dalog_ref)
            ddtb_ref[...] = jnp.zeros_like(ddtb_ref)
            dnw_ref[...] = jnp.zeros_like(dnw_ref)

        ds = ds_ref[...]
        lane = lax.broadcasted_iota(jnp.int32, (C, LANES), 1)
        for c in reversed(range(G)):
            rows = slice(c * C, (c + 1) * C)
            args = (*_dn_operands(x_ref, zb_ref, rows), alog_ref[...], dtb_ref[...], nw_ref[...], sall_ref[c])
            _, vjp = jax.vjp(_dn_chunk, *args)
            dq, dk, dv, dz, db, da, dalog, ddtb, dnw, ds = vjp((_dn_heads(do_ref[rows, :], 0), ds))
            for h in range(H):
                dx_ref[rows, h * D:(h + 1) * D] = dq[h]
                dx_ref[rows, V_B + h * D:V_B + (h + 1) * D] = dk[h]
                dx_ref[rows, 2 * V_B + h * D:2 * V_B + (h + 1) * D] = dv[h]
                dzb_ref[rows, h * D:(h + 1) * D] = dz[h]
            tail = jnp.zeros((C, LANES), F32)
            for h in range(H):
                tail = tail + jnp.where(lane == h, jnp.broadcast_to(db[h], (C, LANES)), 0.0)
                tail = tail + jnp.where(lane == H + h, jnp.broadcast_to(da[h], (C, LANES)), 0.0)
            dzb_ref[rows, V_B:V_B + LANES] = tail
            dzb_ref[rows, V_B + LANES:] = jnp.zeros((C, DN_ZCOLS - V_B - LANES), F32)
            dalog_ref[...] += dalog
            ddtb_ref[...] += ddtb
            dnw_ref[...] += dnw
        ds_ref[...] = ds

    par = pl.BlockSpec((H, 1, 1), lambda i: (0, 0, 0))
    nws = pl.BlockSpec((1, 1, D), lambda i: (0, 0, 0))
    outs, _, slots = _pcall(
        body, (qkvc, proj, alog, dtb, nw, sall, dmix), name="dn_bwd", grid=(N,),
        in_specs=[pl.BlockSpec((G * C, QKV_B), lambda i: (N - 1 - i, 0)),
                  pl.BlockSpec((G * C, DN_ZCOLS), lambda i: (N - 1 - i, DN_ZBLK)), par, par, nws,
                  pl.BlockSpec((G, H, D, D), lambda i: (N - 1 - i, 0, 0, 0)),
                  pl.BlockSpec((G * C, V_B), lambda i: (N - 1 - i, 1))],
        out_specs=[pl.BlockSpec((G * C, QKV_B), lambda i: (N - 1 - i, 0)),
                   pl.BlockSpec((G * C, DN_ZCOLS), lambda i: (N - 1 - i, 0)), par, par, nws],
        out_shape=[jax.ShapeDtypeStruct((T, QKV_B), F32), jax.ShapeDtypeStruct((T, DN_ZCOLS), F32)]
        + [jax.ShapeDtypeStruct((H, 1, 1), F32)] * 2 + [jax.ShapeDtypeStruct((1, 1, D), F32)],
        scratch_shapes=[pltpu.VMEM((H, D, D), F32)], sem=("arbitrary",), rs=rs)
    return outs, slots


def _conv_taps(buf_ref, w, width, halo, tm):
    acc = None
    for kk, win in _windows(buf_ref, [halo - (width - 1) + kk for kk in range(width)], tm):
        term = w[kk:kk + 1, :] * win
        acc = term if acc is None else acc + term
    return acc


def _windows(ref, offsets, tm):
    for res in range(SUBLANES):
        ks = [k for k, o in enumerate(offsets) if o % SUBLANES == res]
        if not ks:
            continue
        lo = min(offsets[k] for k in ks)
        hi = max(offsets[k] for k in ks)
        shifted = ref[pl.ds(lo, tm + hi - lo), :]
        for k in ks:
            yield k, shifted[offsets[k] - lo:offsets[k] - lo + tm]


def _conv_taps_bwd(dbuf_ref, w, width, tm):
    acc = None
    for kk, win in _windows(dbuf_ref, [width - 1 - kk for kk in range(width)], tm):
        term = w[kk:kk + 1, :] * win
        acc = term if acc is None else acc + term
    return acc


def _conv_dw_acc(dw_ref, dout, buf_ref, width, halo, tm):
    for kk, win in _windows(buf_ref, [halo - (width - 1) + kk for kk in range(width)], tm):
        dw_ref[pl.ds(kk, 1), :] += jnp.sum(dout * win, axis=0, keepdims=True)


DNC_HALO = 8
DNC_COLS = 768


def dnconv_fwd(proj, w):
    T = proj.shape[0]
    tm = min(T, 256)
    hb = tm // DNC_HALO

    def body(x_ref, h_ref, w_ref, o_ref, buf_ref):
        i = pl.program_id(0)
        buf_ref[0:DNC_HALO, :] = jnp.where(i > 0, h_ref[...], 0.0)
        buf_ref[DNC_HALO:, :] = x_ref[...]
        acc = _conv_taps(buf_ref, w_ref[...], DN_CONV, DNC_HALO, tm)
        o_ref[...] = acc * _sigmoid(acc)

    return pl.pallas_call(
        body, name="dnconv_fwd", grid=(T // tm, 2),
        in_specs=[pl.BlockSpec((tm, DNC_COLS), lambda i, c: (i, 1 + c)),
                  pl.BlockSpec((DNC_HALO, DNC_COLS), lambda i, c: (jnp.maximum(i * hb - 1, 0), 1 + c)),
                  pl.BlockSpec((DN_CONV, DNC_COLS), lambda i, c: (0, c))],
        out_specs=pl.BlockSpec((tm, DNC_COLS), lambda i, c: (i, c)),
        out_shape=jax.ShapeDtypeStruct((T, QKV_B), F32),
        scratch_shapes=[pltpu.VMEM((DNC_HALO + tm, DNC_COLS), F32)],
        compiler_params=_cparams(("parallel", "parallel")),
    )(proj, proj, w)


def dnconv_bwd(proj, w, dout):
    T = proj.shape[0]
    tm = min(T, 256)
    nt = T // tm
    hb = tm // DNC_HALO

    def body(x_ref, h_ref, w_ref, do_ref, dx_ref, dw_ref, buf_ref, dbuf_ref):
        r = pl.program_id(1)
        i = nt - 1 - r

        @pl.when(r == 0)
        def _():
            dw_ref[...] = jnp.zeros_like(dw_ref)
            dbuf_ref[tm:, :] = jnp.zeros((DNC_HALO, DNC_COLS), F32)

        buf_ref[0:DNC_HALO, :] = jnp.where(i > 0, h_ref[...], 0.0)
        buf_ref[DNC_HALO:, :] = x_ref[...]
        wv = w_ref[...]
        acc = _conv_taps(buf_ref, wv, DN_CONV, DNC_HALO, tm)
        sg = _sigmoid(acc)
        dacc = do_ref[...] * (sg * (1.0 + acc * (1.0 - sg)))
        dbuf_ref[0:tm, :] = dacc
        dx_ref[...] = _conv_taps_bwd(dbuf_ref, wv, DN_CONV, tm)
        _conv_dw_acc(dw_ref, dacc, buf_ref, DN_CONV, DNC_HALO, tm)
        dbuf_ref[tm:, :] = dacc[0:DNC_HALO, :]

    return pl.pallas_call(
        body, name="dnconv_bwd", grid=(2, nt),
        in_specs=[pl.BlockSpec((tm, DNC_COLS), lambda c, r: (nt - 1 - r, 1 + c)),
                  pl.BlockSpec((DNC_HALO, DNC_COLS), lambda c, r: (jnp.maximum((nt - 1 - r) * hb - 1, 0), 1 + c)),
                  pl.BlockSpec((DN_CONV, DNC_COLS), lambda c, r: (0, c)),
                  pl.BlockSpec((tm, DNC_COLS), lambda c, r: (nt - 1 - r, c))],
        out_specs=[pl.BlockSpec((tm, DNC_COLS), lambda c, r: (nt - 1 - r, c)),
                   pl.BlockSpec((DN_CONV, DNC_COLS), lambda c, r: (0, c))],
        out_shape=[jax.ShapeDtypeStruct((T, QKV_B), F32), jax.ShapeDtypeStruct((DN_CONV, QKV_B), F32)],
        scratch_shapes=[pltpu.VMEM((DNC_HALO + tm, DNC_COLS), F32), pltpu.VMEM((tm + DNC_HALO, DNC_COLS), F32)],
        compiler_params=_cparams(("parallel", "arbitrary")),
    )(proj, proj, w, dout)


CV_HALO = 32


def _cv_post(cv, lnw, lnb):
    mu = jnp.mean(cv, axis=-1, keepdims=True)
    xc = cv - mu
    y = xc * lax.rsqrt(jnp.mean(xc * xc, axis=-1, keepdims=True) + EPS) * lnw + lnb
    return y * _sigmoid(y)


def cv_fwd(ab, w, bdw, lnw, lnb, ag=()):
    T = ab.shape[0]
    D = ab.shape[1] // 2
    tm = min(T, 256)
    hb = tm // CV_HALO

    def body(a_ref, b_ref, ah_ref, bh_ref, w_ref, bdw_ref, lnw_ref, lnb_ref, o_ref, cv_ref, buf_ref):
        i = pl.program_id(0)
        buf_ref[0:CV_HALO, :] = jnp.where(i > 0, ah_ref[...] * _sigmoid(bh_ref[...]), 0.0)
        buf_ref[CV_HALO:, :] = a_ref[...] * _sigmoid(b_ref[...])
        cv = _conv_taps(buf_ref, w_ref[...], CONV_WIDTH, CV_HALO, tm) + bdw_ref[...]
        cv_ref[...] = cv
        o_ref[...] = _cv_post(cv, lnw_ref[...], lnb_ref[...])

    halo = lambda c: pl.BlockSpec((CV_HALO, D), lambda i: (jnp.maximum(i * hb - 1, 0), c))
    vec = pl.BlockSpec((1, D), lambda i: (0, 0))
    tile = pl.BlockSpec((tm, D), lambda i: (i, 0))
    outs, gathered, _ = _pcall(
        body, (ab, ab, ab, ab, w, bdw, lnw, lnb), name="cv_fwd", grid=(T // tm,),
        in_specs=[tile, pl.BlockSpec((tm, D), lambda i: (i, 1)),
                  halo(0), halo(1), pl.BlockSpec((CONV_WIDTH, D), lambda i: (0, 0)), vec, vec, vec],
        out_specs=[tile, tile],
        out_shape=[jax.ShapeDtypeStruct((T, D), F32), jax.ShapeDtypeStruct((T, D), F32)],
        scratch_shapes=[pltpu.VMEM((CV_HALO + tm, D), F32)], sem=("arbitrary",), ag=ag)
    return outs, gathered


def cv_bwd(ab, cv, w, lnw, lnb, dout, rs=()):
    T = ab.shape[0]
    D = ab.shape[1] // 2
    tm = min(T, 256)
    nt = T // tm
    hb = tm // CV_HALO

    def body(a_ref, b_ref, ah_ref, bh_ref, cv_ref, w_ref, lnw_ref, lnb_ref, do_ref,
             da_ref, db_ref, dw_ref, dbdw_ref, dlnw_ref, dlnb_ref, buf_ref, dbuf_ref):
        r = pl.program_id(0)
        i = nt - 1 - r

        @pl.when(r == 0)
        def _():
            dw_ref[...] = jnp.zeros_like(dw_ref)
            dbdw_ref[...] = jnp.zeros_like(dbdw_ref)
            dlnw_ref[...] = jnp.zeros_like(dlnw_ref)
            dlnb_ref[...] = jnp.zeros_like(dlnb_ref)
            dbuf_ref[tm:, :] = jnp.zeros((CV_HALO, D), F32)

        a = a_ref[...]
        sb = _sigmoid(b_ref[...])
        buf_ref[0:CV_HALO, :] = jnp.where(i > 0, ah_ref[...] * _sigmoid(bh_ref[...]), 0.0)
        buf_ref[CV_HALO:, :] = a * sb
        wv = w_ref[...]
        _, vjp = jax.vjp(_cv_post, cv_ref[...], lnw_ref[...], lnb_ref[...])
        dcv, dlnw, dlnb = vjp(do_ref[...])
        dlnw_ref[...] += dlnw
        dlnb_ref[...] += dlnb
        dbdw_ref[...] += jnp.sum(dcv, axis=0, keepdims=True)
        dbuf_ref[0:tm, :] = dcv
        du = _conv_taps_bwd(dbuf_ref, wv, CONV_WIDTH, tm)
        _conv_dw_acc(dw_ref, dcv, buf_ref, CONV_WIDTH, CV_HALO, tm)
        dbuf_ref[tm:, :] = dcv[0:CV_HALO, :]
        da_ref[...] = du * sb
        db_ref[...] = du * a * sb * (1.0 - sb)

    tile = lambda c: pl.BlockSpec((tm, D), lambda r: (nt - 1 - r, c))
    halo = lambda c: pl.BlockSpec((CV_HALO, D), lambda r: (jnp.maximum((nt - 1 - r) * hb - 1, 0), c))
    vec = pl.BlockSpec((1, D), lambda r: (0, 0))
    wsp = pl.BlockSpec((CONV_WIDTH, D), lambda r: (0, 0))
    (da, db, dw, dbdw, dlnw, dlnb), _, slots = _pcall(
        body, (ab, ab, ab, ab, cv, w, lnw, lnb, dout), name="cv_bwd", grid=(nt,),
        in_specs=[tile(0), tile(1), halo(0), halo(1), tile(0), wsp, vec, vec, tile(0)],
        out_specs=[tile(0), tile(0), wsp, vec, vec, vec],
        out_shape=[jax.ShapeDtypeStruct((T, D), F32), jax.ShapeDtypeStruct((T, D), F32),
                   jax.ShapeDtypeStruct((CONV_WIDTH, D), F32)] + [jax.ShapeDtypeStruct((1, D), F32)] * 3,
        scratch_shapes=[pltpu.VMEM((CV_HALO + tm, D), F32), pltpu.VMEM((tm + CV_HALO, D), F32)],
        sem=("arbitrary",), rs=rs)
    return (jnp.concatenate([da, db], axis=1), dw, dbdw, dlnw, dlnb), slots


def adamw(w, m, v, slots, rs=()):
    L, R, C = w.shape
    ns = slots[0].shape[0]
    fits = lambda r, c: ns * r * c * 2 <= ADAM_SLOT_BLOCK
    tiles = [(R, C)] if fits(R, C) else []
    tiles += [(d, C) for d in range(16, R, 16) if R % d == 0 and fits(d, C)]
    tiles += [(R, d) for d in range(LANES, C, LANES) if C % d == 0 and fits(R, d)]
    tr, tc = max(tiles, key=lambda t: t[0] * t[1])
    c1 = 1.0 / (1.0 - ADAM_B1 ** ADAM_STEP)
    c2 = 1.0 / (1.0 - ADAM_B2 ** ADAM_STEP)

    def body(w_ref, m_ref, v_ref, *rest):
        s_refs = rest[:L]
        g_ref, d_ref, nm_ref, nv_ref = rest[L:]
        l = pl.program_id(0)
        for k in range(L):
            @pl.when(l == k)
            def _(s_ref=s_refs[k]):
                g = s_ref[0].astype(F32)
                for j in range(1, ns):
                    g = g + s_ref[j].astype(F32)
                nm = ADAM_B1 * m_ref[0] + (1.0 - ADAM_B1) * g
                nv = ADAM_B2 * v_ref[0] + (1.0 - ADAM_B2) * (g * g)
                g_ref[0] = g
                nm_ref[0] = nm
                nv_ref[0] = nv
                d_ref[0] = -ADAM_LR * ((nm * c1) / (jnp.sqrt(nv * c2) + ADAM_EPS) + ADAM_WD * w_ref[0])

    nc = C // tc
    blk = pl.BlockSpec((1, tr, tc), lambda l, i: (l, i // nc, i % nc))
    slot = lambda k: pl.BlockSpec((ns, tr, tc), lambda l, i: (0, jnp.where(l == k, i // nc, 0),
                                                                 jnp.where(l == k, i % nc, 0)))
    outs, _, landed = _pcall(
        body, (w, m, v, *slots), name="adamw", grid=(L, (R // tr) * nc),
        in_specs=[blk, blk, blk] + [slot(k) for k in range(L)],
        out_specs=[blk, blk, blk, blk],
        out_shape=[jax.ShapeDtypeStruct((L, R, C), F32)] * 4,
        sem=("arbitrary", "arbitrary"), rs=rs)
    return outs, landed


def _unshard(g, axis):
    g = jnp.moveaxis(g, 0, axis)
    s = g.shape
    return g.reshape(s[:axis] + (s[axis] * s[axis + 1],) + s[axis + 2:])


def _to_blocks(full, axis):
    s = full.shape
    g = full.reshape(s[:axis] + (N_DEV, s[axis] // N_DEV) + s[axis + 1:])
    return jnp.moveaxis(g, axis, 0)


SMALL = (("norm_w", 2), ("dn_conv_w", 2), ("conv_b_pw1", 1), ("conv_w_dw", 2), ("conv_b_dw", 1),
         ("conv_ln_w", 1), ("conv_ln_b", 1), ("conv_b_pw2", 1),
         ("attn_sinks", None), ("dn_a_log", None), ("dn_dt_bias", None), ("dn_norm_w", None), ("final_norm_w", None))
SMALL_AXIS = dict(SMALL)


def _pack(parts):
    flat = jnp.concatenate([p.reshape(-1) for p in parts])
    pad = (-flat.shape[0]) % LANES
    return jnp.pad(flat, (0, pad))


def _unpack(flat, shapes):
    out, off = [], 0
    for s in shapes:
        n = int(np.prod(s))
        out.append(flat[off:off + n].reshape(s))
        off += n
    return out


def kernel(x, norm_w, ffn_w_gate, ffn_w_up, ffn_w_down, mix_w_in, dn_conv_w, attn_sinks, dn_a_log, dn_dt_bias, dn_norm_w, mix_w_out, conv_w_pw1, conv_b_pw1, conv_w_dw, conv_b_dw, conv_ln_w, conv_ln_b, conv_w_pw2, conv_b_pw2, final_norm_w, loss_target, m_norm_w, m_ffn_w_gate, m_ffn_w_up, m_ffn_w_down, m_mix_w_in, m_dn_conv_w, m_attn_sinks, m_dn_a_log, m_dn_dt_bias, m_dn_norm_w, m_mix_w_out, m_conv_w_pw1, m_conv_b_pw1, m_conv_w_dw, m_conv_b_dw, m_conv_ln_w, m_conv_ln_b, m_conv_w_pw2, m_conv_b_pw2, m_final_norm_w, v_norm_w, v_ffn_w_gate, v_ffn_w_up, v_ffn_w_down, v_mix_w_in, v_dn_conv_w, v_attn_sinks, v_dn_a_log, v_dn_dt_bias, v_dn_norm_w, v_mix_w_out, v_conv_w_pw1, v_conv_b_pw1, v_conv_w_dw, v_conv_b_dw, v_conv_ln_w, v_conv_ln_b, v_conv_w_pw2, v_conv_b_pw2, v_final_norm_w):
    W = dict(norm_w=norm_w, ffn_w_gate=ffn_w_gate, ffn_w_up=ffn_w_up, ffn_w_down=ffn_w_down, mix_w_in=mix_w_in,
             dn_conv_w=dn_conv_w, attn_sinks=attn_sinks, dn_a_log=dn_a_log, dn_dt_bias=dn_dt_bias,
             dn_norm_w=dn_norm_w, mix_w_out=mix_w_out, conv_w_pw1=conv_w_pw1, conv_b_pw1=conv_b_pw1,
             conv_w_dw=conv_w_dw, conv_b_dw=conv_b_dw, conv_ln_w=conv_ln_w, conv_ln_b=conv_ln_b,
             conv_w_pw2=conv_w_pw2, conv_b_pw2=conv_b_pw2, final_norm_w=final_norm_w)
    M = dict(norm_w=m_norm_w, ffn_w_gate=m_ffn_w_gate, ffn_w_up=m_ffn_w_up, ffn_w_down=m_ffn_w_down,
             mix_w_in=m_mix_w_in, dn_conv_w=m_dn_conv_w, attn_sinks=m_attn_sinks, dn_a_log=m_dn_a_log,
             dn_dt_bias=m_dn_dt_bias, dn_norm_w=m_dn_norm_w, mix_w_out=m_mix_w_out, conv_w_pw1=m_conv_w_pw1,
             conv_b_pw1=m_conv_b_pw1, conv_w_dw=m_conv_w_dw, conv_b_dw=m_conv_b_dw, conv_ln_w=m_conv_ln_w,
             conv_ln_b=m_conv_ln_b, conv_w_pw2=m_conv_w_pw2, conv_b_pw2=m_conv_b_pw2, final_norm_w=m_final_norm_w)
    V = dict(norm_w=v_norm_w, ffn_w_gate=v_ffn_w_gate, ffn_w_up=v_ffn_w_up, ffn_w_down=v_ffn_w_down,
             mix_w_in=v_mix_w_in, dn_conv_w=v_dn_conv_w, attn_sinks=v_attn_sinks, dn_a_log=v_dn_a_log,
             dn_dt_bias=v_dn_dt_bias, dn_norm_w=v_dn_norm_w, mix_w_out=v_mix_w_out, conv_w_pw1=v_conv_w_pw1,
             conv_b_pw1=v_conv_b_pw1, conv_w_dw=v_conv_w_dw, conv_b_dw=v_conv_b_dw, conv_ln_w=v_conv_ln_w,
             conv_ln_b=v_conv_ln_b, conv_w_pw2=v_conv_w_pw2, conv_b_pw2=v_conv_b_pw2, final_norm_w=v_final_norm_w)

    T, D = x.shape[1], x.shape[2]
    xs = x[0]

    big = ("ffn_w_gate", "ffn_w_up", "ffn_w_down", "mix_w_in", "mix_w_out", "conv_w_pw1", "conv_w_pw2")
    shard3 = {k: W[k].reshape((-1,) + W[k].shape[-2:]) for k in big}
    shard_bf = {k: shard3[k].astype(BF16) for k in big}
    ffn_unit = lambda i: [("ffn_w_gate", i), ("ffn_w_up", i), ("ffn_w_down", i)]
    even_unit = lambda e: [("mix_w_in", e), ("mix_w_out", e)]
    odd_unit = lambda e: [("conv_w_pw1", e), ("conv_w_pw2", e)]
    have = {}

    def ag_jobs(units):
        return [(shard_bf[k], i) for k, i in units]

    def ag_done(units, gathered):
        have.update(zip(units, gathered))

    small_sharded = [(k, ax) for k, ax in SMALL if ax is not None]
    small_pack = _pack([W[k] for k, _ in small_sharded])[None, :]
    first_units = ffn_unit(0)
    gathered = exchange(ag_jobs(first_units) + [(small_pack, None)])
    ag_done(first_units, gathered[:-1])
    small_full = {}
    for (k, ax), parts in zip(small_sharded,
                              zip(*[_unpack(gathered[-1][s, 0], [W[k].shape for k, _ in small_sharded])
                                    for s in range(N_DEV)])):
        small_full[k] = _unshard(jnp.stack(parts), ax)
    nw_full = small_full["norm_w"]

    ffn_w = lambda i: [have[u] for u in ffn_unit(i)]
    w_in_of = lambda e: jnp.pad(_unshard(have[("mix_w_in", e)], 1), ((0, 0), (0, IN_COLS_PAD - IN_COLS)))
    w_out_of = lambda e: have[("mix_w_out", e)].reshape(D, D)
    w_pw1_of = lambda e: _unshard(have[("conv_w_pw1", e)], 1)
    w_pw2_of = lambda e: have[("conv_w_pw2", e)].reshape(D, D)
    fwd_order, needed = [], {}
    for l in range(DEPTH):
        mixer = [("A", l), ("E", l)] if l % 2 == 0 else [("O", l)]
        fwd_order += [("F", 2 * l)] + mixer + [("F", 2 * l + 1)]
        needed[("F", 2 * l)], needed[("F", 2 * l + 1)] = ffn_unit(2 * l), ffn_unit(2 * l + 1)
        needed[mixer[0]] = even_unit(l // 2) if l % 2 == 0 else odd_unit(l // 2)
    queue = [(u, pos) for pos, key in enumerate(fwd_order) for u in needed.get(key, []) if u not in first_units]
    unit_bytes = lambda u: N_DEV * shard_bf[u[0]][u[1]].size * 2
    fwd_carry, at = {}, 0
    for pos, key in enumerate(fwd_order):
        cap = FWD_CARRY_BYTES[key[0]]
        taken, used = [], 0
        while at < len(queue) and (queue[at][1] <= pos + 1 or used + unit_bytes(queue[at][0]) <= cap):
            taken.append(queue[at][0])
            used += unit_bytes(queue[at][0])
            at += 1
        fwd_carry[key] = taken
    zero_in = jnp.zeros((1, IN_COLS_PAD), F32)
    zero_d = jnp.zeros((1, D), F32)
    slope_rows = jnp.asarray(np.repeat(2.0 ** (-8.0 * np.arange(1, ATTN_HEADS + 1) / ATTN_HEADS), ATTN_BLOCK)
                             .astype(np.float32)[:, None])

    saved = []
    h = xs
    w_in, w_out, w_pw1, w_pw2 = {}, {}, {}, {}

    def ffn_forward(h, l, half):
        i = 2 * l + half
        units = fwd_carry.get(("F", i), [])
        h, gathered = ffn_fwd(h, nw_full[l, 2 * half][None], *ffn_w(i), ag=ag_jobs(units))
        ag_done(units, gathered)
        return h

    for l in range(DEPTH):
        e = l // 2
        st = {"x0": h}
        h = ffn_forward(h, l, 0)
        st["x1"] = h
        if l % 2 == 0:
            w_in[e], w_out[e] = w_in_of(e), w_out_of(e)
            proj = rmslin_fwd(h, nw_full[l, 1][None], w_in[e], zero_in)
            st["proj"] = proj
            st["qkvc"] = dnconv_fwd(proj, small_full["dn_conv_w"][e])
            st["sink_rows"] = jnp.repeat(attn_sinks[e], ATTN_BLOCK)[:, None]
            st["alog"] = dn_a_log[e].reshape(DN_HEADS, 1, 1)
            st["dtb"] = dn_dt_bias[e].reshape(DN_HEADS, 1, 1)
            st["dnw"] = dn_norm_w[e].reshape(1, 1, DN_D)
            units = fwd_carry[("A", l)]
            st["att"], gathered = attn_fwd(proj, st["sink_rows"], slope_rows, ag=ag_jobs(units))
            ag_done(units, gathered)
            units = fwd_carry[("E", l)]
            (st["og"], st["sall"]), gathered = dn_fwd(st["qkvc"], proj, st["alog"], st["dtb"], st["dnw"],
                                                      ag=ag_jobs(units))
            ag_done(units, gathered)
            h = lin_fwd(h, [st["att"], st["og"]], w_out[e], zero_d)
        else:
            units = fwd_carry[("O", l)]
            w_pw1[e], w_pw2[e] = w_pw1_of(e), w_pw2_of(e)
            st["ab"] = rmslin_fwd(h, nw_full[l, 1][None], w_pw1[e], small_full["conv_b_pw1"][e][None])
            (st["act"], st["cv"]), gathered = cv_fwd(st["ab"], small_full["conv_w_dw"][e], small_full["conv_b_dw"][e][None],
                                         small_full["conv_ln_w"][e][None], small_full["conv_ln_b"][e][None],
                                         ag=ag_jobs(units))
            ag_done(units, gathered)
            h = lin_fwd(h, [st["act"]], w_pw2[e], small_full["conv_b_pw2"][e][None])
        st["x2"] = h
        h = ffn_forward(h, l, 1)
        saved.append(st)

    loss_part, dh, dfinal = loss_fwd_bwd(h, final_norm_w[None], loss_target[0])
    loss = lax.psum(loss_part[0, 0], ("x", "y", "c"))

    d_norm = [[None] * 3 for _ in range(DEPTH)]
    d_small = {k: [None, None] for k in ("dn_conv_w", "conv_b_pw1", "conv_w_dw", "conv_b_dw", "conv_ln_w",
                                         "conv_ln_b", "conv_b_pw2", "attn_sinks", "dn_a_log", "dn_dt_bias",
                                         "dn_norm_w")}
    pending, slot = [], {}

    def take_jobs(cap=None, only=None):
        taken = [p for p in pending if p[1] == "swap"]
        used = 0
        for p in pending:
            if p[1] == "swap" or (only is not None and p[0][0] not in only):
                continue
            if cap is not None and used + p[2].size * 2 > cap:
                break
            taken.append(p)
            used += p[2].size * 2
        pending[:] = [p for p in pending if all(p is not t for t in taken)]
        return taken, [(kind, arr) for _, kind, arr in taken]

    def land(taken, results):
        swapped = [(unit, arr, res) for (unit, kind, arr), res in zip(taken, results) if kind == "swap"]
        slot.update({unit: res for (unit, kind, _), res in zip(taken, results) if kind != "swap"})
        if swapped:
            sums = pair_add([g for _, g, _ in swapped], [r for _, _, r in swapped])
            pending.extend((unit, "chips", h) for (unit, _, _), h in zip(swapped, sums))

    def ffn_backward(dh, l, half):
        i = 2 * l + half
        taken, jobs = take_jobs(BWD_CARRY_BYTES["F"])
        (dh, dg, du, dd, d_norm[l][2 * half]), results = ffn_bwd(
            st["x2" if half else "x0"], dh, nw_full[l, 2 * half][None], *ffn_w(i), rs=jobs)
        land(taken, results)
        pending.extend((u, "swap", g) for u, g in zip(ffn_unit(i), (dg, du, dd)))
        return dh

    for l in reversed(range(DEPTH)):
        e = l // 2
        st = saved[l]
        dh = ffn_backward(dh, l, 1)
        if l % 2 == 0:
            dmix, d_out, _ = lin_bwd([st["att"], st["og"]], dh, w_out[e])
            pending.append((("mix_w_out", e), "swap", d_out.reshape(N_DEV, D // N_DEV, D)))
            taken, jobs = take_jobs(BWD_CARRY_BYTES["E"])
            (dqkvc, dzba, dalog, ddtb, ddnw), results = dn_bwd(
                st["qkvc"], st["proj"], st["alog"], st["dtb"], st["dnw"], st["sall"], dmix, rs=jobs)
            land(taken, results)
            taken, jobs = take_jobs(BWD_CARRY_BYTES["A"])
            (dqa, dkva, dsink), results = attn_bwd(st["proj"], st["sink_rows"], slope_rows, dmix, rs=jobs)
            land(taken, results)
            dqkv, d_small["dn_conv_w"][e] = dnconv_bwd(st["proj"], small_full["dn_conv_w"][e], dqkvc)
            dh, d_in, _, d_norm[l][1] = rmslin_bwd(st["x1"], dh, [dqa, dkva, dqkv, dzba], nw_full[l, 1][None],
                                                   w_in[e])
            pending.append((("mix_w_in", e), "direct", _to_blocks(d_in[:, :IN_COLS], 1)))
            d_small["attn_sinks"][e] = jnp.sum(dsink.reshape(ATTN_HEADS, ATTN_BLOCK), axis=1)
            d_small["dn_a_log"][e] = dalog.reshape(DN_HEADS)
            d_small["dn_dt_bias"][e] = ddtb.reshape(DN_HEADS)
            d_small["dn_norm_w"][e] = ddnw.reshape(DN_D)
        else:
            dact, d_pw2, d_small["conv_b_pw2"][e] = lin_bwd([st["act"]], dh, w_pw2[e])
            pending.append((("conv_w_pw2", e), "swap", d_pw2.reshape(N_DEV, D // N_DEV, D)))
            taken, jobs = take_jobs(BWD_CARRY_BYTES["O"])
            (dab, d_small["conv_w_dw"][e], d_small["conv_b_dw"][e], d_small["conv_ln_w"][e],
             d_small["conv_ln_b"][e]), results = cv_bwd(
                st["ab"], st["cv"], small_full["conv_w_dw"][e],
                small_full["conv_ln_w"][e][None], small_full["conv_ln_b"][e][None], dact, rs=jobs)
            land(taken, results)
            dh, d_pw1, d_small["conv_b_pw1"][e], d_norm[l][1] = rmslin_bwd(
                st["x1"], dh, dab, nw_full[l, 1][None], w_pw1[e])
            pending.append((("conv_w_pw1", e), "swap", _to_blocks(d_pw1, 1)))
        dh = ffn_backward(dh, l, 0)
    grad_x = dh[None]

    full_small = {"norm_w": jnp.stack([jnp.concatenate(r, axis=0) for r in d_norm]),
                  "final_norm_w": dfinal[0]}
    for k, pair in d_small.items():
        full_small[k] = jnp.stack([p.reshape(W[k].shape[1:-1] + (-1,)) if SMALL_AXIS[k] is not None
                                   else p for p in pair])
    rows = []
    for s in range(N_DEV):
        parts = [_to_blocks(full_small[k], ax)[s] if ax is not None else full_small[k] for k, ax in SMALL]
        rows.append(_pack(parts))
    send_small = jnp.stack(rows)[:, None, :]
    pending.append((("small", 0), "direct", send_small))

    res = {}
    waiting = lambda k: [p for p in pending if p[0][0] == k]
    adam_order = sorted(big, key=lambda k: len(waiting(k))) + ["small"]
    for n, k in enumerate(adam_order[:-1]):
        nxt = next((kk for kk in adam_order[n + 1:] if waiting(kk)), "small")
        taken, jobs = take_jobs(only=(nxt, "small") if n == 1 else (nxt,))
        turned = k in ("ffn_w_gate", "ffn_w_up")
        view = lambda a: jnp.swapaxes(a.reshape(shard3[k].shape), 1, 2) if turned else a.reshape(shard3[k].shape)
        outs, results = adamw(view(W[k]), view(M[k]), view(V[k]),
                              [slot[(k, i)] for i in range(shard3[k].shape[0])], rs=jobs)
        land(taken, results)
        res[k] = [(jnp.swapaxes(o, 1, 2) if turned else o).reshape(W[k].shape) for o in outs]
    pk = lambda d: _pack([d[k] for k, _ in SMALL])[None, None, :]
    outs, _ = adamw(pk(W), pk(M), pk(V), [slot[("small", 0)]])
    shapes = [W[k].shape for k, _ in SMALL]
    unp = [_unpack(o[0, 0], shapes) for o in outs]
    for i, (k, _) in enumerate(SMALL):
        res[k] = [u[i] for u in unp]

    order = ("norm_w", "ffn_w_gate", "ffn_w_up", "ffn_w_down", "mix_w_in", "dn_conv_w", "attn_sinks", "dn_a_log",
             "dn_dt_bias", "dn_norm_w", "mix_w_out", "conv_w_pw1", "conv_b_pw1", "conv_w_dw", "conv_b_dw",
             "conv_ln_w", "conv_ln_b", "conv_w_pw2", "conv_b_pw2", "final_norm_w")
    return (loss, grad_x, *[res[k][0] for k in order], *[res[k][1] for k in order],
            *[res[k][2] for k in order], *[res[k][3] for k in order])
```

```python
import functools

import numpy as np
import jax
import jax.numpy as jnp
from jax import lax
from jax.experimental import pallas as pl
from jax.experimental.pallas import tpu as pltpu

F32 = jnp.float32
BF16 = jnp.bfloat16
EPS = 1e-6
N_DEV = 8
N_CHIP = 4
V7X_VMEM_LIMIT = 60 * 2**20
MESH = pl.DeviceIdType.MESH
LANES = 128
SUBLANES = 8

DEPTH = 4
D_MODEL = 1024
ATTN_HEADS, ATTN_KV_HEADS, HEAD_DIM, ATTN_BLOCK = 8, 2, 64, 128
DN_HEADS, DN_D, DN_CHUNK, DN_CONV = 8, 64, 64, 4
CONV_WIDTH = 31
Q_A, KV_A, QKV_B, V_B = 512, 128, 1536, 512
IN_COLS = 2832
IN_COLS_PAD = 3072
OFF_QKVB = Q_A + 2 * KV_A
OFF_Z = OFF_QKVB + QKV_B
OFF_BETA = OFF_Z + V_B
OFF_A = OFF_BETA + DN_HEADS

FWD_CARRY_BYTES = {"F": 12 * 2**20, "A": 6 * 2**20, "E": 18 * 2**20, "O": 12 * 2**20}
BWD_CARRY_BYTES = {"F": 11 * 2**20, "A": 6 * 2**20, "E": 13 * 2**20, "O": 10 * 2**20}

ADAM_SLOT_BLOCK = 3 * 2**19

ADAM_LR, ADAM_B1, ADAM_B2, ADAM_EPS, ADAM_WD, ADAM_STEP = 0.001, 0.9, 0.999, 1e-08, 0.01, 10


def _cparams(sem):
    return pltpu.CompilerParams(dimension_semantics=sem, vmem_limit_bytes=V7X_VMEM_LIMIT)


def _sigmoid(x):
    return 1.0 / (1.0 + jnp.exp(-x))


def _softplus(x):
    return jnp.maximum(x, 0.0) + jnp.log(1.0 + jnp.exp(-jnp.abs(x)))


def _dot(a, b):
    return jnp.dot(a, b, preferred_element_type=F32)


def _dot_nt(a, b):
    return lax.dot_general(a, b, (((1,), (1,)), ((), ())), preferred_element_type=F32)


def _dot_tn(a, b):
    return lax.dot_general(a, b, (((0,), (0,)), ((), ())), preferred_element_type=F32)


def _rms(x, w):
    return x * lax.rsqrt(jnp.mean(x * x, axis=-1, keepdims=True) + EPS) * w


def _rms_bwd(x, w, dxn):
    r = lax.rsqrt(jnp.mean(x * x, axis=-1, keepdims=True) + EPS)
    xh = x * r
    dxh = dxn * w
    dx = r * (dxh - xh * jnp.mean(dxh * xh, axis=-1, keepdims=True))
    return dx, jnp.sum(dxn * xh, axis=0, keepdims=True)


def _position():
    return lax.axis_index("x"), lax.axis_index("y"), lax.axis_index("c")


def _dev_index(px, py, pc):
    return 4 * px + 2 * py + pc


def _rcopy(src, dst, send_sem, recv_sem, to):
    return pltpu.make_async_remote_copy(src_ref=src, dst_ref=dst, send_sem=send_sem, recv_sem=recv_sem,
                                        device_id=to, device_id_type=MESH)


def _ag_start(srcs, outs, send, recv, local):
    x, y, c = _position()
    me = _dev_index(x, y, c)
    chips = [(1 - x, y), (x, 1 - y), (1 - x, 1 - y)]
    for a, (src, out) in enumerate(zip(srcs, outs)):
        pltpu.make_async_copy(src, out.at[me], local.at[a]).start()
        _rcopy(src, out.at[me], send.at[a, 0], recv.at[a, 0], (x, y, 1 - c)).start()
        for j, chip in enumerate(chips):
            _rcopy(src, out.at[me], send.at[a, 1 + j], recv.at[a, 1 + j], (*chip, c)).start()


def _ag_finish(srcs, outs, send, recv, local):
    x, y, c = _position()
    me = _dev_index(x, y, c)
    sibling = (x, y, 1 - c)
    chips = [(1 - x, y), (x, 1 - y), (1 - x, 1 - y)]
    for j, chip in enumerate(chips):
        for a, out in enumerate(outs):
            blk = out.at[_dev_index(*chip, c)]
            _rcopy(blk, blk, send.at[a, 1 + j], recv.at[a, 1 + j], (x, y, c)).wait_recv()
            _rcopy(blk, blk, send.at[a, 4 + j], recv.at[a, 4 + j], sibling).start()
    for a, (src, out) in enumerate(zip(srcs, outs)):
        blk = out.at[_dev_index(x, y, 1 - c)]
        _rcopy(blk, blk, send.at[a, 0], recv.at[a, 0], (x, y, c)).wait_recv()
        for j, chip in enumerate(chips):
            blk = out.at[_dev_index(*chip, 1 - c)]
            _rcopy(blk, blk, send.at[a, 4 + j], recv.at[a, 4 + j], (x, y, c)).wait_recv()
        for k in range(N_DEV - 1):
            _rcopy(out.at[me], out.at[me], send.at[a, k], recv.at[a, k], (x, y, c)).wait_send()
        pltpu.make_async_copy(src, out.at[me], local.at[a]).wait()


def _rs_peer(r):
    x, y, c = _position()
    return x ^ ((r >> 2) & 1), y ^ ((r >> 1) & 1), c ^ (r & 1)


def _rs_start(ins, outs, send, recv, local):
    me = _dev_index(*_position())
    for a, (src, out) in enumerate(zip(ins, outs)):
        pltpu.make_async_copy(src.at[me], out.at[me], local.at[a]).start()
        for r in range(1, N_DEV):
            p = _rs_peer(r)
            _rcopy(src.at[_dev_index(*p)], out.at[me], send.at[a, r - 1], recv.at[a, r - 1], p).start()


def _rs_finish(ins, outs, send, recv, local):
    pos = _position()
    me = _dev_index(*pos)
    for a, (src, out) in enumerate(zip(ins, outs)):
        for r in range(1, N_DEV):
            blk = out.at[_dev_index(*_rs_peer(r))]
            _rcopy(blk, blk, send.at[a, r - 1], recv.at[a, r - 1], pos).wait_recv()
        for r in range(1, N_DEV):
            _rcopy(src.at[me], out.at[me], send.at[a, r - 1], recv.at[a, r - 1], pos).wait_send()
        pltpu.make_async_copy(src.at[me], out.at[me], local.at[a]).wait()


def _sw_start(ins, outs, send, recv):
    x, y, c = _position()
    for a, (src, out) in enumerate(zip(ins, outs)):
        for q in range(N_CHIP):
            _rcopy(src.at[2 * q + (1 - c)], out.at[q], send.at[a, q], recv.at[a, q], (x, y, 1 - c)).start()


def _sw_finish(ins, outs, send, recv):
    pos = _position()
    for a, out in enumerate(outs):
        for q in range(N_CHIP):
            _rcopy(out.at[q], out.at[q], send.at[a, q], recv.at[a, q], pos).wait_recv()
        for q in range(N_CHIP):
            _rcopy(out.at[q], out.at[q], send.at[a, q], recv.at[a, q], pos).wait_send()


def _r4_peer(r):
    x, y, c = _position()
    return x ^ ((r >> 1) & 1), y ^ (r & 1), c


def _r4_start(ins, outs, send, recv, local):
    x, y, c = _position()
    mine = 2 * x + y
    for a, (src, out) in enumerate(zip(ins, outs)):
        pltpu.make_async_copy(src.at[mine], out.at[mine], local.at[a]).start()
        for r in range(1, N_CHIP):
            px, py, pc = _r4_peer(r)
            _rcopy(src.at[2 * px + py], out.at[mine], send.at[a, r - 1], recv.at[a, r - 1], (px, py, pc)).start()


def _r4_finish(ins, outs, send, recv, local):
    x, y, c = _position()
    mine = 2 * x + y
    for a, (src, out) in enumerate(zip(ins, outs)):
        for r in range(1, N_CHIP):
            px, py, _ = _r4_peer(r)
            blk = out.at[2 * px + py]
            _rcopy(blk, blk, send.at[a, r - 1], recv.at[a, r - 1], (x, y, c)).wait_recv()
        for r in range(1, N_CHIP):
            _rcopy(src.at[mine], out.at[mine], send.at[a, r - 1], recv.at[a, r - 1], (x, y, c)).wait_send()
        pltpu.make_async_copy(src.at[mine], out.at[mine], local.at[a]).wait()


_RS_KINDS = {
    "direct": (_rs_start, _rs_finish, lambda n: [(n, N_DEV - 1), (n, N_DEV - 1), (n,)], lambda s: s),
    "swap": (_sw_start, _sw_finish, lambda n: [(n, N_CHIP), (n, N_CHIP)], lambda s: (N_CHIP,) + s[1:]),
    "chips": (_r4_start, _r4_finish, lambda n: [(n, N_CHIP - 1), (n, N_CHIP - 1), (n,)], lambda s: s),
}


def _pcall(body, args, *, name, grid, in_specs, out_specs, out_shape, sem, scratch_shapes=(), ag=(), rs=()):
    na, nr = len(ag), len(rs)
    if na + nr == 0:
        outs = pl.pallas_call(body, name=name, grid=grid, in_specs=in_specs, out_specs=out_specs,
                              out_shape=out_shape, scratch_shapes=list(scratch_shapes),
                              compiler_params=_cparams(sem))(*args)
        return list(outs), [], []
    n_in, n_out, n_scr = len(in_specs), len(out_specs), len(scratch_shapes)
    ag_idx = [i for _, i in ag]
    groups = [(k, [i for i, (kk, _) in enumerate(rs) if kk == k]) for k in _RS_KINDS]
    groups = [(k, idx) for k, idx in groups if idx]
    sem_counts = ([3] if na else []) + [len(_RS_KINDS[k][2](1)) for k, _ in groups]

    def wrapped(*refs):
        cin, refs = refs[:n_in], refs[n_in:]
        ag_in, refs = refs[:na], refs[na:]
        rs_in, refs = refs[:nr], refs[nr:]
        cout, refs = refs[:n_out], refs[n_out:]
        ag_out, refs = refs[:na], refs[na:]
        rs_out, refs = refs[:nr], refs[nr:]
        cscr, sems = refs[:n_scr], list(refs[n_scr:])
        sem_sets = [[sems.pop(0) for _ in range(n)] for n in sem_counts]
        ag_sems = sem_sets.pop(0) if na else None
        ag_src = [r if i is None else r.at[i] for r, i in zip(ag_in, ag_idx)]
        ids = [pl.program_id(d) for d in range(len(grid))]
        first = functools.reduce(jnp.logical_and, [i == 0 for i in ids])
        last = functools.reduce(jnp.logical_and, [i == g - 1 for i, g in zip(ids, grid)])

        def run(phase):
            if na:
                (_ag_start, _ag_finish)[phase](ag_src, ag_out, *ag_sems)
            for (k, idx), ss in zip(groups, sem_sets):
                _RS_KINDS[k][phase]([rs_in[i] for i in idx], [rs_out[i] for i in idx], *ss)

        @pl.when(first)
        def _():
            run(0)

        body(*cin, *cout, *cscr)

        @pl.when(last)
        def _():
            run(1)

    hbm = pl.BlockSpec(memory_space=pl.ANY)
    sem_shapes = [pltpu.SemaphoreType.DMA(s) for s in ([(na, N_DEV - 1), (na, N_DEV - 1), (na,)] if na else [])]
    for k, idx in groups:
        sem_shapes += [pltpu.SemaphoreType.DMA(s) for s in _RS_KINDS[k][2](len(idx))]
    outs = pl.pallas_call(
        wrapped, name=name, grid=grid,
        in_specs=list(in_specs) + [hbm] * (na + nr),
        out_specs=list(out_specs) + [hbm] * (na + nr),
        out_shape=list(out_shape)
        + [jax.ShapeDtypeStruct((N_DEV,) + a.shape[-2:], a.dtype) for a, _ in ag]
        + [jax.ShapeDtypeStruct(_RS_KINDS[k][3](b.shape), b.dtype) for k, b in rs],
        scratch_shapes=list(scratch_shapes) + sem_shapes,
        compiler_params=_cparams(sem),
    )(*args, *[a for a, _ in ag], *[b for _, b in rs])
    return list(outs[:n_out]), list(outs[n_out:n_out + na]), list(outs[n_out + na:])


def exchange(ag):
    def body(o_ref):
        o_ref[...] = jnp.zeros_like(o_ref)

    _, gathered, _ = _pcall(body, (), name="exchange", grid=(1,), in_specs=[],
                            out_specs=[pl.BlockSpec((8, LANES), lambda i: (0, 0))],
                            out_shape=[jax.ShapeDtypeStruct((8, LANES), F32)], sem=("arbitrary",), ag=ag)
    return gathered


def pair_add(blocks, received):
    n = len(blocks)

    def body(core_ref, *refs):
        for g_ref, p_ref, o_ref in zip(refs[:n], refs[n:2 * n], refs[2 * n:]):
            o_ref[0] = (g_ref[0, 0].astype(F32) + p_ref[0].astype(F32)).astype(BF16)

    halves = 2
    g_specs = [pl.BlockSpec((1, 1, b.shape[1] // halves, b.shape[2]), lambda q, r, core: (q, core[0], r, 0))
               for b in blocks]
    p_specs = [pl.BlockSpec((1, b.shape[1] // halves, b.shape[2]), lambda q, r, core: (q, r, 0)) for b in blocks]
    return pl.pallas_call(
        body, name="pair_add",
        grid_spec=pltpu.PrefetchScalarGridSpec(num_scalar_prefetch=1, grid=(N_CHIP, halves),
                                               in_specs=g_specs + p_specs, out_specs=p_specs),
        out_shape=[jax.ShapeDtypeStruct(p.shape, BF16) for p in received],
        compiler_params=_cparams(("parallel", "parallel")),
    )(lax.axis_index("c").astype(jnp.int32)[None], *[b.reshape((N_CHIP, 2) + b.shape[1:]) for b in blocks], *received)


FFN_PAIR = 2


def _pair_cols(w_ref):
    return jnp.concatenate([w_ref[p] for p in range(FFN_PAIR)], axis=1)


def ffn_fwd(x, nw, wg, wu, wd, ag=()):
    T, D = x.shape
    F = wg.shape[2]
    P = FFN_PAIR
    J = wg.shape[0] // P
    tm = min(T, 1024)

    def body(x_ref, nw_ref, wg_ref, wu_ref, wd_ref, o_ref, xn_ref, acc_ref):
        j = pl.program_id(1)

        @pl.when(j == 0)
        def _():
            xn_ref[...] = _rms(x_ref[...], nw_ref[...]).astype(BF16)
            acc_ref[...] = jnp.zeros_like(acc_ref)

        xn = xn_ref[...]
        g = _dot(xn, _pair_cols(wg_ref))
        u = _dot(xn, _pair_cols(wu_ref))
        h = (g * _sigmoid(g) * u).astype(BF16)
        acc_ref[...] += _dot(h, wd_ref[...].reshape(P * F, D))

        @pl.when(j == J - 1)
        def _():
            o_ref[...] = x_ref[...] + 0.5 * acc_ref[...]

    (out,), gathered, _ = _pcall(
        body, (x, nw, wg, wu, wd), name="ffn_fwd", grid=(T // tm, J),
        in_specs=[pl.BlockSpec((tm, D), lambda t, j: (t, 0)),
                  pl.BlockSpec((1, D), lambda t, j: (0, 0)),
                  pl.BlockSpec((P, D, F), lambda t, j: (j, 0, 0)),
                  pl.BlockSpec((P, D, F), lambda t, j: (j, 0, 0)),
                  pl.BlockSpec((P, F, D), lambda t, j: (j, 0, 0))],
        out_specs=[pl.BlockSpec((tm, D), lambda t, j: (t, 0))],
        out_shape=[jax.ShapeDtypeStruct((T, D), F32)],
        scratch_shapes=[pltpu.VMEM((tm, D), BF16), pltpu.VMEM((tm, D), F32)],
        sem=("arbitrary", "arbitrary"), ag=ag)
    return out, gathered


def ffn_bwd(x, dy, nw, wg, wu, wd, rs=()):
    T, D = x.shape
    F = wg.shape[2]
    P = FFN_PAIR
    J = wg.shape[0] // P
    tm = min(T, 256)
    nt = T // tm

    def body(x_ref, dy_ref, nw_ref, wg_ref, wu_ref, wd_ref,
             dx_ref, dwg_ref, dwu_ref, dwd_ref, dnw_ref,
             xn_ref, dyh_ref, dxn_ref, awg_ref, awu_ref, awd_ref):
        j = pl.program_id(0)
        t = pl.program_id(1)
        rows = pl.ds(pl.multiple_of(t * tm, tm), tm)

        @pl.when(j == 0)
        def _():
            xn_ref[rows, :] = _rms(x_ref[...], nw_ref[...]).astype(BF16)
            dyh_ref[rows, :] = (0.5 * dy_ref[...]).astype(BF16)
            dxn_ref[rows, :] = jnp.zeros((tm, D), F32)

        @pl.when((j == 0) & (t == 0))
        def _():
            dnw_ref[...] = jnp.zeros_like(dnw_ref)

        @pl.when(t == 0)
        def _():
            awg_ref[...] = jnp.zeros_like(awg_ref)
            awu_ref[...] = jnp.zeros_like(awu_ref)
            awd_ref[...] = jnp.zeros_like(awd_ref)

        xn = xn_ref[rows, :]
        dyh = dyh_ref[rows, :]
        wg2, wu2 = _pair_cols(wg_ref), _pair_cols(wu_ref)
        g = _dot(xn, wg2)
        u = _dot(xn, wu2)
        sg = _sigmoid(g)
        s = g * sg
        h = (s * u).astype(BF16)
        dh = _dot_nt(dyh, wd_ref[...].reshape(P * F, D))
        du = (dh * s).astype(BF16)
        dg = (dh * u * (sg * (1.0 + g * (1.0 - sg)))).astype(BF16)
        awd_ref[...] += _dot_tn(h, dyh)
        awg_ref[...] += _dot_tn(dg, xn)
        awu_ref[...] += _dot_tn(du, xn)
        dxn_ref[rows, :] += _dot_nt(dg, wg2) + _dot_nt(du, wu2)

        @pl.when(t == nt - 1)
        def _():
            dwg_ref[...] = awg_ref[...].astype(BF16).reshape(P, F, D)
            dwu_ref[...] = awu_ref[...].astype(BF16).reshape(P, F, D)
            dwd_ref[...] = awd_ref[...].astype(BF16).reshape(P, F, D)

        @pl.when(j == J - 1)
        def _():
            dx, dnw = _rms_bwd(x_ref[...], nw_ref[...], dxn_ref[rows, :])
            dx_ref[...] = dy_ref[...] + dx
            dnw_ref[...] += dnw

    ends = lambda j, t: (jnp.where((j == 0) | (j == J - 1), t, 0), 0)
    last = lambda j, t: (jnp.where(j == J - 1, t, 0), 0)
    outs, _, slots = _pcall(
        body, (x, dy, nw, wg, wu, wd), name="ffn_bwd", grid=(J, nt),
        in_specs=[pl.BlockSpec((tm, D), ends), pl.BlockSpec((tm, D), ends),
                  pl.BlockSpec((1, D), lambda j, t: (0, 0)),
                  pl.BlockSpec((P, D, F), lambda j, t: (j, 0, 0)),
                  pl.BlockSpec((P, D, F), lambda j, t: (j, 0, 0)),
                  pl.BlockSpec((P, F, D), lambda j, t: (j, 0, 0))],
        out_specs=[pl.BlockSpec((tm, D), last),
                   pl.BlockSpec((P, F, D), lambda j, t: (j, 0, 0)),
                   pl.BlockSpec((P, F, D), lambda j, t: (j, 0, 0)),
                   pl.BlockSpec((P, F, D), lambda j, t: (j, 0, 0)),
                   pl.BlockSpec((1, D), lambda j, t: (0, 0))],
        out_shape=[jax.ShapeDtypeStruct((T, D), F32)] + [jax.ShapeDtypeStruct((P * J, F, D), BF16)] * 3
        + [jax.ShapeDtypeStruct((1, D), F32)],
        scratch_shapes=[pltpu.VMEM((T, D), BF16), pltpu.VMEM((T, D), BF16), pltpu.VMEM((T, D), F32)]
        + [pltpu.VMEM((P * F, D), F32)] * 3,
        sem=("arbitrary", "arbitrary"), rs=rs)
    return outs, slots


def rmslin_fwd(x, nw, w, b):
    T, D = x.shape
    N = w.shape[1]
    tm = min(T, 256)

    def body(x_ref, nw_ref, w_ref, b_ref, o_ref):
        xn = _rms(x_ref[...], nw_ref[...]).astype(BF16)
        o_ref[...] = _dot(xn, w_ref[...]) + b_ref[...]

    return pl.pallas_call(
        body, name="rmslin_fwd", grid=(T // tm,),
        in_specs=[pl.BlockSpec((tm, D), lambda t: (t, 0)), pl.BlockSpec((1, D), lambda t: (0, 0)),
                  pl.BlockSpec((D, N), lambda t: (0, 0)), pl.BlockSpec((1, N), lambda t: (0, 0))],
        out_specs=pl.BlockSpec((tm, N), lambda t: (t, 0)),
        out_shape=jax.ShapeDtypeStruct((T, N), F32),
        compiler_params=_cparams(("parallel",)),
    )(x, nw, w, b)


def rmslin_bwd(x, dres, dproj, nw, w):
    T, D = x.shape
    N = w.shape[1]
    pieces = list(dproj) if isinstance(dproj, (list, tuple)) else [dproj]
    nb = DN_ZCOLS if len(pieces) > 1 else 1024
    nc = N // nb
    tm = min(T, 512)
    nt = T // tm
    n_p = len(pieces)

    def body(x_ref, dres_ref, *refs):
        p_refs, (nw_ref, w_ref, dx_ref, dw_ref, db_ref, dnw_ref, xn_ref, dxn_ref, acc_ref) = refs[:n_p], refs[n_p:]
        c = pl.program_id(0)
        t = pl.program_id(1)
        rows = pl.ds(pl.multiple_of(t * tm, tm), tm)

        @pl.when(c == 0)
        def _():
            xn_ref[rows, :] = _rms(x_ref[...], nw_ref[...]).astype(BF16)
            dxn_ref[rows, :] = jnp.zeros((tm, D), F32)

        @pl.when((c == 0) & (t == 0))
        def _():
            dnw_ref[...] = jnp.zeros_like(dnw_ref)

        @pl.when(t == 0)
        def _():
            acc_ref[...] = jnp.zeros_like(acc_ref)
            db_ref[...] = jnp.zeros_like(db_ref)

        if n_p == 1:
            dpf = p_refs[0][...]
        else:
            dq_ref, dkv_ref, dqkv_ref, dz_ref = p_refs
            dpf = jnp.where(c == 0, jnp.concatenate([dq_ref[...], dkv_ref[...]], axis=1),
                            jnp.where(c == nc - 1, dz_ref[...], dqkv_ref[...]))
        dp = dpf.astype(BF16)
        acc_ref[...] += _dot_tn(xn_ref[rows, :], dp)
        db_ref[...] += jnp.sum(dpf, axis=0, keepdims=True)
        dxn_ref[rows, :] += _dot_nt(dp, w_ref[...])

        @pl.when(t == nt - 1)
        def _():
            dw_ref[...] = acc_ref[...].astype(BF16)

        @pl.when(c == nc - 1)
        def _():
            dx, dnw = _rms_bwd(x_ref[...], nw_ref[...], dxn_ref[rows, :])
            dx_ref[...] = dres_ref[...] + dx
            dnw_ref[...] += dnw

    ends = lambda c, t: (jnp.where((c == 0) | (c == nc - 1), t, 0), 0)
    last = lambda c, t: (jnp.where(c == nc - 1, t, 0), 0)
    first = lambda c, t: (jnp.where(c == 0, t, 0), 0)
    if n_p == 1:
        p_specs = [pl.BlockSpec((tm, nb), lambda c, t: (t, c))]
    else:
        p_specs = [pl.BlockSpec((tm, Q_A), first), pl.BlockSpec((tm, 2 * KV_A), first),
                   pl.BlockSpec((tm, nb), lambda c, t: (jnp.where((c > 0) & (c < nc - 1), t, 0),
                                                        jnp.clip(c - 1, 0, 1))),
                   pl.BlockSpec((tm, nb), last)]
    return pl.pallas_call(
        body, name="rmslin_bwd", grid=(nc, nt),
        in_specs=[pl.BlockSpec((tm, D), ends), pl.BlockSpec((tm, D), last)] + p_specs
        + [pl.BlockSpec((1, D), lambda c, t: (0, 0)), pl.BlockSpec((D, nb), lambda c, t: (0, c))],
        out_specs=[pl.BlockSpec((tm, D), last),
                   pl.BlockSpec((D, nb), lambda c, t: (0, c)),
                   pl.BlockSpec((1, nb), lambda c, t: (0, c)),
                   pl.BlockSpec((1, D), lambda c, t: (0, 0))],
        out_shape=[jax.ShapeDtypeStruct((T, D), F32), jax.ShapeDtypeStruct((D, N), BF16),
                   jax.ShapeDtypeStruct((1, N), F32), jax.ShapeDtypeStruct((1, D), F32)],
        scratch_shapes=[pltpu.VMEM((T, D), BF16), pltpu.VMEM((T, D), F32), pltpu.VMEM((D, nb), F32)],
        compiler_params=_cparams(("arbitrary", "arbitrary")),
    )(x, dres, *pieces, nw, w)


def lin_fwd(res, parts, w, b):
    T = res.shape[0]
    K, N = w.shape
    tm = min(T, 512)
    n = len(parts)
    offs = [sum(p.shape[1] for p in parts[:i]) for i in range(n + 1)]

    def body(res_ref, *refs):
        a_refs, (w_ref, b_ref, o_ref) = refs[:n], refs[n:]
        acc = res_ref[...] + b_ref[...]
        for i, a_ref in enumerate(a_refs):
            acc = acc + _dot(a_ref[...].astype(BF16), w_ref[offs[i]:offs[i + 1], :])
        o_ref[...] = acc

    return pl.pallas_call(
        body, name="lin_fwd", grid=(T // tm,),
        in_specs=[pl.BlockSpec((tm, N), lambda t: (t, 0))]
        + [pl.BlockSpec((tm, p.shape[1]), lambda t: (t, 0)) for p in parts]
        + [pl.BlockSpec((K, N), lambda t: (0, 0)), pl.BlockSpec((1, N), lambda t: (0, 0))],
        out_specs=pl.BlockSpec((tm, N), lambda t: (t, 0)),
        out_shape=jax.ShapeDtypeStruct((T, N), F32),
        compiler_params=_cparams(("parallel",)),
    )(res, *parts, w, b)


def lin_bwd(parts, dy, w):
    T = dy.shape[0]
    K, N = w.shape
    tm = min(T, 512)
    nt = T // tm
    n = len(parts)
    offs = [sum(p.shape[1] for p in parts[:i]) for i in range(n + 1)]

    def body(*refs):
        a_refs, (dy_ref, w_ref, da_ref, dw_ref, db_ref, acc_ref) = refs[:n], refs[n:]

        @pl.when(pl.program_id(0) == 0)
        def _():
            acc_ref[...] = jnp.zeros_like(acc_ref)
            db_ref[...] = jnp.zeros_like(db_ref)

        dyf = dy_ref[...]
        dyb = dyf.astype(BF16)
        da_ref[...] = _dot_nt(dyb, w_ref[...])
        for i, a_ref in enumerate(a_refs):
            acc_ref[offs[i]:offs[i + 1], :] += _dot_tn(a_ref[...].astype(BF16), dyb)
        db_ref[...] += jnp.sum(dyf, axis=0, keepdims=True)

        @pl.when(pl.program_id(0) == nt - 1)
        def _():
            dw_ref[...] = acc_ref[...].astype(BF16)

    return pl.pallas_call(
        body, name="lin_bwd", grid=(nt,),
        in_specs=[pl.BlockSpec((tm, p.shape[1]), lambda t: (t, 0)) for p in parts]
        + [pl.BlockSpec((tm, N), lambda t: (t, 0)), pl.BlockSpec((K, N), lambda t: (0, 0))],
        out_specs=[pl.BlockSpec((tm, K), lambda t: (t, 0)), pl.BlockSpec((K, N), lambda t: (0, 0)),
                   pl.BlockSpec((1, N), lambda t: (0, 0))],
        out_shape=[jax.ShapeDtypeStruct((T, K), F32), jax.ShapeDtypeStruct((K, N), BF16),
                   jax.ShapeDtypeStruct((1, N), F32)],
        scratch_shapes=[pltpu.VMEM((K, N), F32)],
        compiler_params=_cparams(("arbitrary",)),
    )(*parts, dy, w)


def loss_fwd_bwd(x, fw, target):
    T, D = x.shape
    tm = min(T, 256)

    def body(x_ref, fw_ref, tg_ref, loss_ref, dx_ref, dfw_ref):
        @pl.when(pl.program_id(0) == 0)
        def _():
            loss_ref[...] = jnp.zeros_like(loss_ref)
            dfw_ref[...] = jnp.zeros_like(dfw_ref)

        xv = x_ref[...]
        w = fw_ref[...]
        err = _rms(xv, w) - tg_ref[...]
        row = jnp.sum(err * err, axis=-1, keepdims=True)
        loss_ref[...] += (0.5 / D) * jnp.sum(row, axis=0, keepdims=True)
        dx, dfw = _rms_bwd(xv, w, err * (1.0 / D))
        dx_ref[...] = dx
        dfw_ref[...] += dfw

    return pl.pallas_call(
        body, name="loss_fwd_bwd", grid=(T // tm,),
        in_specs=[pl.BlockSpec((tm, D), lambda t: (t, 0)), pl.BlockSpec((1, D), lambda t: (0, 0)),
                  pl.BlockSpec((tm, D), lambda t: (t, 0))],
        out_specs=[pl.BlockSpec((1, 1), lambda t: (0, 0)), pl.BlockSpec((tm, D), lambda t: (t, 0)),
                   pl.BlockSpec((1, D), lambda t: (0, 0))],
        out_shape=[jax.ShapeDtypeStruct((1, 1), F32), jax.ShapeDtypeStruct((T, D), F32),
                   jax.ShapeDtypeStruct((1, D), F32)],
        compiler_params=_cparams(("arbitrary",)),
    )(x, fw, target)


def _attn_masks(n, rows, blk):
    r = lax.broadcasted_iota(jnp.int32, (rows, 2 * blk), 0)
    jj = lax.broadcasted_iota(jnp.int32, (rows, 2 * blk), 1)
    dist = (r % blk) + blk - jj
    valid = (dist >= 0) & (dist < blk) & ((n > 0) | (jj >= blk))
    return dist.astype(F32), valid


def _attn_block(q, kcat, vcat, sink, slope, dist, valid):
    d = q.shape[-1]
    s = _dot_nt(q.astype(BF16), kcat.astype(BF16)) * (d ** -0.5)
    s = jnp.where(valid, s - slope * dist, -1e30)
    m = lax.stop_gradient(jnp.maximum(jnp.max(s, axis=-1, keepdims=True), sink))
    e = jnp.exp(s - m)
    p = e / (jnp.sum(e, axis=-1, keepdims=True) + jnp.exp(sink - m))
    return _dot(p.astype(BF16), vcat.astype(BF16))


ATTN_G = ATTN_HEADS // ATTN_KV_HEADS
ATTN_QW = ATTN_G * HEAD_DIM
ATTN_KCOL = Q_A // KV_A


def _attn_specs():
    blk = ATTN_BLOCK
    qs = pl.BlockSpec((blk, ATTN_QW), lambda h, n: (n, h))
    prev = lambda c: pl.BlockSpec((blk, KV_A), lambda h, n: (jnp.maximum(n - 1, 0), c))
    cur = lambda c: pl.BlockSpec((blk, KV_A), lambda h, n: (n, c))
    rowp = pl.BlockSpec((ATTN_G * blk, 1), lambda h, n: (h, 0))
    return qs, [prev(ATTN_KCOL), cur(ATTN_KCOL), prev(ATTN_KCOL + 1), cur(ATTN_KCOL + 1)], rowp


def _attn_operands(h, q_ref, kp_ref, kc_ref, vp_ref, vc_ref):
    d = HEAD_DIM
    q = jnp.concatenate([q_ref[:, g * d:(g + 1) * d] for g in range(ATTN_G)], axis=0)
    pick = lambda r: jnp.where(h == 0, r[:, :d], r[:, d:])
    kcat = jnp.concatenate([pick(kp_ref[...]), pick(kc_ref[...])], axis=0)
    vcat = jnp.concatenate([pick(vp_ref[...]), pick(vc_ref[...])], axis=0)
    return q, kcat, vcat


def attn_fwd(proj, sink_rows, slope_rows, ag=()):
    T = proj.shape[0]
    blk, d = ATTN_BLOCK, HEAD_DIM

    def body(q_ref, kp_ref, kc_ref, vp_ref, vc_ref, sink_ref, slope_ref, o_ref):
        h, n = pl.program_id(0), pl.program_id(1)
        dist, valid = _attn_masks(n, ATTN_G * blk, blk)
        q, kcat, vcat = _attn_operands(h, q_ref, kp_ref, kc_ref, vp_ref, vc_ref)
        o = _attn_block(q, kcat, vcat, sink_ref[...], slope_ref[...], dist, valid)
        for g in range(ATTN_G):
            o_ref[:, g * d:(g + 1) * d] = o[g * blk:(g + 1) * blk]

    qs, kv, rowp = _attn_specs()
    (out,), gathered, _ = _pcall(
        body, (proj, proj, proj, proj, proj, sink_rows, slope_rows), name="attn_fwd",
        grid=(ATTN_KV_HEADS, T // blk), in_specs=[qs] + kv + [rowp, rowp], out_specs=[qs],
        out_shape=[jax.ShapeDtypeStruct((T, Q_A), F32)], sem=("arbitrary", "arbitrary"), ag=ag)
    return out, gathered


def attn_bwd(proj, sink_rows, slope_rows, dmix, rs=()):
    T = proj.shape[0]
    blk, d = ATTN_BLOCK, HEAD_DIM

    def body(q_ref, kp_ref, kc_ref, vp_ref, vc_ref, sink_ref, slope_ref, do_ref, dq_ref, dkv_ref, dsink_ref):
        h, n = pl.program_id(0), pl.program_id(1)

        @pl.when((h == 0) & (n == 0))
        def _():
            dkv_ref[...] = jnp.zeros_like(dkv_ref)

        @pl.when(n == 0)
        def _():
            dsink_ref[...] = jnp.zeros_like(dsink_ref)

        dist, valid = _attn_masks(n, ATTN_G * blk, blk)
        q, kcat, vcat = _attn_operands(h, q_ref, kp_ref, kc_ref, vp_ref, vc_ref)
        do = jnp.concatenate([do_ref[:, g * d:(g + 1) * d] for g in range(ATTN_G)], axis=0)
        fn = functools.partial(_attn_block, slope=slope_ref[...], dist=dist, valid=valid)
        _, vjp = jax.vjp(fn, q, kcat, vcat, sink_ref[...])
        dq, dkcat, dvcat, dsink = vjp(do)
        for g in range(ATTN_G):
            dq_ref[:, g * d:(g + 1) * d] = dq[g * blk:(g + 1) * blk]
        dsink_ref[...] += dsink
        lane = lax.broadcasted_iota(jnp.int32, (2 * blk, 2 * KV_A), 1)
        mine = (lane % KV_A) // d == h
        both = jnp.where(mine, jnp.concatenate([dkcat, dkcat, dvcat, dvcat], axis=1), 0.0)

        @pl.when(n == 0)
        def _():
            dkv_ref[0:blk, :] += both[blk:]

        @pl.when(n > 0)
        def _():
            rows = pl.ds(pl.multiple_of((n - 1) * blk, blk), 2 * blk)
            dkv_ref[rows, :] += both

    qs, kv, rowp = _attn_specs()
    outs, _, slots = _pcall(
        body, (proj, proj, proj, proj, proj, sink_rows, slope_rows, dmix), name="attn_bwd",
        grid=(ATTN_KV_HEADS, T // blk), in_specs=[qs] + kv + [rowp, rowp, qs],
        out_specs=[qs, pl.BlockSpec((T, 2 * KV_A), lambda h, n: (0, 0)), rowp],
        out_shape=[jax.ShapeDtypeStruct((T, Q_A), F32), jax.ShapeDtypeStruct((T, 2 * KV_A), F32),
                   jax.ShapeDtypeStruct((ATTN_HEADS * blk, 1), F32)],
        sem=("arbitrary", "arbitrary"), rs=rs)
    return outs, slots


_NN = (((2,), (1,)), ((0,), (0,)))
_NT = (((2,), (2,)), ((0,), (0,)))
_TN = (((1,), (1,)), ((0,), (0,)))


def _bmm(a, b, dims):
    return lax.dot_general(a.astype(BF16), b.astype(BF16), dims, preferred_element_type=F32)


def _split(x, terms):
    out = []
    for _ in range(terms):
        t = x.astype(BF16)
        out.append(t)
        x = x - t.astype(F32)
    return out


def _fine_product(a, b, dims):
    (ah, al), (bh, bl) = _split(a, 2), _split(b, 2)
    dot = lambda x, y: lax.dot_general(x, y, dims, preferred_element_type=F32)
    return dot(ah, bh) + (dot(ah, bl) + dot(al, bh))


def _mask_product(mask, x, dims):
    mb = mask.astype(BF16)
    parts = [lax.dot_general(mb, t, dims, preferred_element_type=F32) for t in _split(x, 3)]
    return parts[0] + (parts[1] + parts[2])


@jax.custom_vjp
def _fine_nt(a, b):
    return _fine_product(a, b, _NT)


_fine_nt.defvjp(lambda a, b: (_fine_product(a, b, _NT), (a, b)),
                lambda res, ct: (_fine_product(ct, res[1], _NN), _fine_product(ct, res[0], _TN)))


@jax.custom_vjp
def _mask_nn(mask, x):
    return _mask_product(mask, x, _NN)


_mask_nn.defvjp(lambda mask, x: (_mask_product(mask, x, _NN), mask),
                lambda mask, ct: (jnp.zeros_like(mask), _mask_product(mask, ct, _TN)))


@jax.custom_vjp
def _unit_lower_inverse(low):
    n = low.shape[-1]
    eye = (lax.broadcasted_iota(jnp.int32, low.shape, 1) == lax.broadcasted_iota(jnp.int32, low.shape, 2)).astype(F32)
    tinv = eye - low
    p = low
    for _ in range(n.bit_length() - 2):
        p = _bmm(p, p, _NN)
        tinv = tinv + _bmm(tinv, p, _NN)
    return tinv


def _unit_lower_inverse_fwd(low):
    tinv = _unit_lower_inverse(low)
    return tinv, tinv


_unit_lower_inverse.defvjp(_unit_lower_inverse_fwd, lambda tinv, ct: (-_bmm(_bmm(tinv, ct, _TN), tinv, _NT),))


def _dn_chunk(qc, kc, vc, zc, braw, araw, alog, dtb, nw, S):
    H, C, D = qc.shape
    row = lax.broadcasted_iota(jnp.int32, (H, C, C), 1)
    col = lax.broadcasted_iota(jnp.int32, (H, C, C), 2)
    causal = row >= col
    strict = row > col
    eye = (row == col).astype(F32)

    q = qc * lax.rsqrt(jnp.sum(qc * qc, axis=-1, keepdims=True) + EPS) * (D ** -0.5)
    k = kc * lax.rsqrt(jnp.sum(kc * kc, axis=-1, keepdims=True) + EPS)
    beta = _sigmoid(braw)
    g = -jnp.exp(alog) * _softplus(araw + dtb)
    a_col = _mask_nn(causal.astype(F32), jnp.broadcast_to(g, (H, C, C)))
    a_row = _mask_nn(jnp.ones((H, C, C), F32), eye * a_col)
    decay = jnp.where(causal, jnp.exp(jnp.where(causal, a_col - a_row, 0.0)), 0.0)
    kb = k * beta
    tinv = _unit_lower_inverse(jnp.where(strict, _fine_nt(kb, k) * decay, 0.0))
    e_col = jnp.exp(a_col)
    u = _bmm(tinv, vc * beta, _NN)
    w = _bmm(tinv, kb * e_col, _NN)
    attn = _fine_nt(q, k) * decay
    gl = a_col[:, C - 1:C, :]
    k_dec = k * jnp.exp(gl - a_col)
    v_new = u - _bmm(w, S, _NN)
    o = _bmm(q * e_col, S, _NN) + _bmm(attn, v_new, _NN)
    s_new = S * jnp.exp(jnp.broadcast_to(gl, (H, D, D))) + _bmm(k_dec, v_new, _TN)
    on = o * lax.rsqrt(jnp.mean(o * o, axis=-1, keepdims=True) + EPS) * nw
    return on * (zc * _sigmoid(zc)), s_new


DN_ZCOLS = IN_COLS_PAD - OFF_Z
DN_ZBLK = OFF_Z // DN_ZCOLS


def _dn_heads(a, off):
    return jnp.stack([a[:, off + h * DN_D:off + (h + 1) * DN_D] for h in range(DN_HEADS)])


def _dn_gate_cols(zb, off):
    return jnp.stack([zb[:, off + h:off + h + 1] for h in range(DN_HEADS)])


DN_STEP_CHUNKS = 4


def _dn_operands(x_ref, zb_ref, rows):
    x, zb = x_ref[rows, :], zb_ref[rows, :]
    return (_dn_heads(x, 0), _dn_heads(x, V_B), _dn_heads(x, 2 * V_B), _dn_heads(zb, 0),
            _dn_gate_cols(zb, V_B), _dn_gate_cols(zb, V_B + DN_HEADS))


def dn_fwd(qkvc, proj, alog, dtb, nw, ag=()):
    T = qkvc.shape[0]
    H, C, D, G = DN_HEADS, DN_CHUNK, DN_D, DN_STEP_CHUNKS
    N = T // C

    def body(x_ref, zb_ref, alog_ref, dtb_ref, nw_ref, o_ref, sall_ref, s_ref):
        @pl.when(pl.program_id(0) == 0)
        def _():
            s_ref[...] = jnp.zeros_like(s_ref)

        s = s_ref[...]
        for c in range(G):
            rows = slice(c * C, (c + 1) * C)
            sall_ref[c] = s
            on, s = _dn_chunk(*_dn_operands(x_ref, zb_ref, rows), alog_ref[...], dtb_ref[...], nw_ref[...], s)
            for h in range(H):
                o_ref[rows, h * D:(h + 1) * D] = on[h]
        s_ref[...] = s

    par = pl.BlockSpec((H, 1, 1), lambda n: (0, 0, 0))
    outs, gathered, _ = _pcall(
        body, (qkvc, proj, alog, dtb, nw), name="dn_fwd", grid=(N // G,),
        in_specs=[pl.BlockSpec((G * C, QKV_B), lambda n: (n, 0)),
                  pl.BlockSpec((G * C, DN_ZCOLS), lambda n: (n, DN_ZBLK)),
                  par, par, pl.BlockSpec((1, 1, D), lambda n: (0, 0, 0))],
        out_specs=[pl.BlockSpec((G * C, V_B), lambda n: (n, 0)), pl.BlockSpec((G, H, D, D), lambda n: (n, 0, 0, 0))],
        out_shape=[jax.ShapeDtypeStruct((T, V_B), F32), jax.ShapeDtypeStruct((N, H, D, D), F32)],
        scratch_shapes=[pltpu.VMEM((H, D, D), F32)], sem=("arbitrary",), ag=ag)
    return outs, gathered


def dn_bwd(qkvc, proj, alog, dtb, nw, sall, dmix, rs=()):
    T = qkvc.shape[0]
    H, C, D, G = DN_HEADS, DN_CHUNK, DN_D, DN_STEP_CHUNKS
    N = T // C // G

    def body(x_ref, zb_ref, alog_ref, dtb_ref, nw_ref, sall_ref, do_ref,
             dx_ref, dzb_ref, dalog_ref, ddtb_ref, dnw_ref, ds_ref):
        @pl.when(pl.program_id(0) == 0)
        def _():
            ds_ref[...] = jnp.zeros_like(ds_ref)
            dalog_ref[...] = jnp.zeros_like(dalog_ref)
            ddtb_ref[...] = jnp.zeros_like(ddtb_ref)
            dnw_ref[...] = jnp.zeros_like(dnw_ref)

        ds = ds_ref[...]
        lane = lax.broadcasted_iota(jnp.int32, (C, LANES), 1)
        for c in reversed(range(G)):
            rows = slice(c * C, (c + 1) * C)
            args = (*_dn_operands(x_ref, zb_ref, rows), alog_ref[...], dtb_ref[...], nw_ref[...], sall_ref[c])
            _, vjp = jax.vjp(_dn_chunk, *args)
            dq, dk, dv, dz, db, da, dalog, ddtb, dnw, ds = vjp((_dn_heads(do_ref[rows, :], 0), ds))
            for h in range(H):
                dx_ref[rows, h * D:(h + 1) * D] = dq[h]
                dx_ref[rows, V_B + h * D:V_B + (h + 1) * D] = dk[h]
                dx_ref[rows, 2 * V_B + h * D:2 * V_B + (h + 1) * D] = dv[h]
                dzb_ref[rows, h * D:(h + 1) * D] = dz[h]
            tail = jnp.zeros((C, LANES), F32)
            for h in range(H):
                tail = tail + jnp.where(lane == h, jnp.broadcast_to(db[h], (C, LANES)), 0.0)
                tail = tail + jnp.where(lane == H + h, jnp.broadcast_to(da[h], (C, LANES)), 0.0)
            dzb_ref[rows, V_B:V_B + LANES] = tail
            dzb_ref[rows, V_B + LANES:] = jnp.zeros((C, DN_ZCOLS - V_B - LANES), F32)
            dalog_ref[...] += dalog
            ddtb_ref[...] += ddtb
            dnw_ref[...] += dnw
        ds_ref[...] = ds

    par = pl.BlockSpec((H, 1, 1), lambda i: (0, 0, 0))
    nws = pl.BlockSpec((1, 1, D), lambda i: (0, 0, 0))
    outs, _, slots = _pcall(
        body, (qkvc, proj, alog, dtb, nw, sall, dmix), name="dn_bwd", grid=(N,),
        in_specs=[pl.BlockSpec((G * C, QKV_B), lambda i: (N - 1 - i, 0)),
                  pl.BlockSpec((G * C, DN_ZCOLS), lambda i: (N - 1 - i, DN_ZBLK)), par, par, nws,
                  pl.BlockSpec((G, H, D, D), lambda i: (N - 1 - i, 0, 0, 0)),
                  pl.BlockSpec((G * C, V_B), lambda i: (N - 1 - i, 1))],
        out_specs=[pl.BlockSpec((G * C, QKV_B), lambda i: (N - 1 - i, 0)),
                   pl.BlockSpec((G * C, DN_ZCOLS), lambda i: (N - 1 - i, 0)), par, par, nws],
        out_shape=[jax.ShapeDtypeStruct((T, QKV_B), F32), jax.ShapeDtypeStruct((T, DN_ZCOLS), F32)]
        + [jax.ShapeDtypeStruct((H, 1, 1), F32)] * 2 + [jax.ShapeDtypeStruct((1, 1, D), F32)],
        scratch_shapes=[pltpu.VMEM((H, D, D), F32)], sem=("arbitrary",), rs=rs)
    return outs, slots


def _conv_taps(buf_ref, w, width, halo, tm):
    acc = None
    for kk, win in _windows(buf_ref, [halo - (width - 1) + kk for kk in range(width)], tm):
        term = w[kk:kk + 1, :] * win
        acc = term if acc is None else acc + term
    return acc


def _windows(ref, offsets, tm):
    for res in range(SUBLANES):
        ks = [k for k, o in enumerate(offsets) if o % SUBLANES == res]
        if not ks:
            continue
        lo = min(offsets[k] for k in ks)
        hi = max(offsets[k] for k in ks)
        shifted = ref[pl.ds(lo, tm + hi - lo), :]
        for k in ks:
            yield k, shifted[offsets[k] - lo:offsets[k] - lo + tm]


def _conv_taps_bwd(dbuf_ref, w, width, tm):
    acc = None
    for kk, win in _windows(dbuf_ref, [width - 1 - kk for kk in range(width)], tm):
        term = w[kk:kk + 1, :] * win
        acc = term if acc is None else acc + term
    return acc


def _conv_dw_acc(dw_ref, dout, buf_ref, width, halo, tm):
    for kk, win in _windows(buf_ref, [halo - (width - 1) + kk for kk in range(width)], tm):
        dw_ref[pl.ds(kk, 1), :] += jnp.sum(dout * win, axis=0, keepdims=True)


DNC_HALO = 8
DNC_COLS = 768


def dnconv_fwd(proj, w):
    T = proj.shape[0]
    tm = min(T, 256)
    hb = tm // DNC_HALO

    def body(x_ref, h_ref, w_ref, o_ref, buf_ref):
        i = pl.program_id(0)
        buf_ref[0:DNC_HALO, :] = jnp.where(i > 0, h_ref[...], 0.0)
        buf_ref[DNC_HALO:, :] = x_ref[...]
        acc = _conv_taps(buf_ref, w_ref[...], DN_CONV, DNC_HALO, tm)
        o_ref[...] = acc * _sigmoid(acc)

    return pl.pallas_call(
        body, name="dnconv_fwd", grid=(T // tm, 2),
        in_specs=[pl.BlockSpec((tm, DNC_COLS), lambda i, c: (i, 1 + c)),
                  pl.BlockSpec((DNC_HALO, DNC_COLS), lambda i, c: (jnp.maximum(i * hb - 1, 0), 1 + c)),
                  pl.BlockSpec((DN_CONV, DNC_COLS), lambda i, c: (0, c))],
        out_specs=pl.BlockSpec((tm, DNC_COLS), lambda i, c: (i, c)),
        out_shape=jax.ShapeDtypeStruct((T, QKV_B), F32),
        scratch_shapes=[pltpu.VMEM((DNC_HALO + tm, DNC_COLS), F32)],
        compiler_params=_cparams(("parallel", "parallel")),
    )(proj, proj, w)


def dnconv_bwd(proj, w, dout):
    T = proj.shape[0]
    tm = min(T, 256)
    nt = T // tm
    hb = tm // DNC_HALO

    def body(x_ref, h_ref, w_ref, do_ref, dx_ref, dw_ref, buf_ref, dbuf_ref):
        r = pl.program_id(1)
        i = nt - 1 - r

        @pl.when(r == 0)
        def _():
            dw_ref[...] = jnp.zeros_like(dw_ref)
            dbuf_ref[tm:, :] = jnp.zeros((DNC_HALO, DNC_COLS), F32)

        buf_ref[0:DNC_HALO, :] = jnp.where(i > 0, h_ref[...], 0.0)
        buf_ref[DNC_HALO:, :] = x_ref[...]
        wv = w_ref[...]
        acc = _conv_taps(buf_ref, wv, DN_CONV, DNC_HALO, tm)
        sg = _sigmoid(acc)
        dacc = do_ref[...] * (sg * (1.0 + acc * (1.0 - sg)))
        dbuf_ref[0:tm, :] = dacc
        dx_ref[...] = _conv_taps_bwd(dbuf_ref, wv, DN_CONV, tm)
        _conv_dw_acc(dw_ref, dacc, buf_ref, DN_CONV, DNC_HALO, tm)
        dbuf_ref[tm:, :] = dacc[0:DNC_HALO, :]

    return pl.pallas_call(
        body, name="dnconv_bwd", grid=(2, nt),
        in_specs=[pl.BlockSpec((tm, DNC_COLS), lambda c, r: (nt - 1 - r, 1 + c)),
                  pl.BlockSpec((DNC_HALO, DNC_COLS), lambda c, r: (jnp.maximum((nt - 1 - r) * hb - 1, 0), 1 + c)),
                  pl.BlockSpec((DN_CONV, DNC_COLS), lambda c, r: (0, c)),
                  pl.BlockSpec((tm, DNC_COLS), lambda c, r: (nt - 1 - r, c))],
        out_specs=[pl.BlockSpec((tm, DNC_COLS), lambda c, r: (nt - 1 - r, c)),
                   pl.BlockSpec((DN_CONV, DNC_COLS), lambda c, r: (0, c))],
        out_shape=[jax.ShapeDtypeStruct((T, QKV_B), F32), jax.ShapeDtypeStruct((DN_CONV, QKV_B), F32)],
        scratch_shapes=[pltpu.VMEM((DNC_HALO + tm, DNC_COLS), F32), pltpu.VMEM((tm + DNC_HALO, DNC_COLS), F32)],
        compiler_params=_cparams(("parallel", "arbitrary")),
    )(proj, proj, w, dout)


CV_HALO = 32


def _cv_post(cv, lnw, lnb):
    mu = jnp.mean(cv, axis=-1, keepdims=True)
    xc = cv - mu
    y = xc * lax.rsqrt(jnp.mean(xc * xc, axis=-1, keepdims=True) + EPS) * lnw + lnb
    return y * _sigmoid(y)


def cv_fwd(ab, w, bdw, lnw, lnb, ag=()):
    T = ab.shape[0]
    D = ab.shape[1] // 2
    tm = min(T, 256)
    hb = tm // CV_HALO

    def body(a_ref, b_ref, ah_ref, bh_ref, w_ref, bdw_ref, lnw_ref, lnb_ref, o_ref, cv_ref, buf_ref):
        i = pl.program_id(0)
        buf_ref[0:CV_HALO, :] = jnp.where(i > 0, ah_ref[...] * _sigmoid(bh_ref[...]), 0.0)
        buf_ref[CV_HALO:, :] = a_ref[...] * _sigmoid(b_ref[...])
        cv = _conv_taps(buf_ref, w_ref[...], CONV_WIDTH, CV_HALO, tm) + bdw_ref[...]
        cv_ref[...] = cv
        o_ref[...] = _cv_post(cv, lnw_ref[...], lnb_ref[...])

    halo = lambda c: pl.BlockSpec((CV_HALO, D), lambda i: (jnp.maximum(i * hb - 1, 0), c))
    vec = pl.BlockSpec((1, D), lambda i: (0, 0))
    tile = pl.BlockSpec((tm, D), lambda i: (i, 0))
    outs, gathered, _ = _pcall(
        body, (ab, ab, ab, ab, w, bdw, lnw, lnb), name="cv_fwd", grid=(T // tm,),
        in_specs=[tile, pl.BlockSpec((tm, D), lambda i: (i, 1)),
                  halo(0), halo(1), pl.BlockSpec((CONV_WIDTH, D), lambda i: (0, 0)), vec, vec, vec],
        out_specs=[tile, tile],
        out_shape=[jax.ShapeDtypeStruct((T, D), F32), jax.ShapeDtypeStruct((T, D), F32)],
        scratch_shapes=[pltpu.VMEM((CV_HALO + tm, D), F32)], sem=("arbitrary",), ag=ag)
    return outs, gathered


def cv_bwd(ab, cv, w, lnw, lnb, dout, rs=()):
    T = ab.shape[0]
    D = ab.shape[1] // 2
    tm = min(T, 256)
    nt = T // tm
    hb = tm // CV_HALO

    def body(a_ref, b_ref, ah_ref, bh_ref, cv_ref, w_ref, lnw_ref, lnb_ref, do_ref,
             da_ref, db_ref, dw_ref, dbdw_ref, dlnw_ref, dlnb_ref, buf_ref, dbuf_ref):
        r = pl.program_id(0)
        i = nt - 1 - r

        @pl.when(r == 0)
        def _():
            dw_ref[...] = jnp.zeros_like(dw_ref)
            dbdw_ref[...] = jnp.zeros_like(dbdw_ref)
            dlnw_ref[...] = jnp.zeros_like(dlnw_ref)
            dlnb_ref[...] = jnp.zeros_like(dlnb_ref)
            dbuf_ref[tm:, :] = jnp.zeros((CV_HALO, D), F32)

        a = a_ref[...]
        sb = _sigmoid(b_ref[...])
        buf_ref[0:CV_HALO, :] = jnp.where(i > 0, ah_ref[...] * _sigmoid(bh_ref[...]), 0.0)
        buf_ref[CV_HALO:, :] = a * sb
        wv = w_ref[...]
        _, vjp = jax.vjp(_cv_post, cv_ref[...], lnw_ref[...], lnb_ref[...])
        dcv, dlnw, dlnb = vjp(do_ref[...])
        dlnw_ref[...] += dlnw
        dlnb_ref[...] += dlnb
        dbdw_ref[...] += jnp.sum(dcv, axis=0, keepdims=True)
        dbuf_ref[0:tm, :] = dcv
        du = _conv_taps_bwd(dbuf_ref, wv, CONV_WIDTH, tm)
        _conv_dw_acc(dw_ref, dcv, buf_ref, CONV_WIDTH, CV_HALO, tm)
        dbuf_ref[tm:, :] = dcv[0:CV_HALO, :]
        da_ref[...] = du * sb
        db_ref[...] = du * a * sb * (1.0 - sb)

    tile = lambda c: pl.BlockSpec((tm, D), lambda r: (nt - 1 - r, c))
    halo = lambda c: pl.BlockSpec((CV_HALO, D), lambda r: (jnp.maximum((nt - 1 - r) * hb - 1, 0), c))
    vec = pl.BlockSpec((1, D), lambda r: (0, 0))
    wsp = pl.BlockSpec((CONV_WIDTH, D), lambda r: (0, 0))
    (da, db, dw, dbdw, dlnw, dlnb), _, slots = _pcall(
        body, (ab, ab, ab, ab, cv, w, lnw, lnb, dout), name="cv_bwd", grid=(nt,),
        in_specs=[tile(0), tile(1), halo(0), halo(1), tile(0), wsp, vec, vec, tile(0)],
        out_specs=[tile(0), tile(0), wsp, vec, vec, vec],
        out_shape=[jax.ShapeDtypeStruct((T, D), F32), jax.ShapeDtypeStruct((T, D), F32),
                   jax.ShapeDtypeStruct((CONV_WIDTH, D), F32)] + [jax.ShapeDtypeStruct((1, D), F32)] * 3,
        scratch_shapes=[pltpu.VMEM((CV_HALO + tm, D), F32), pltpu.VMEM((tm + CV_HALO, D), F32)],
        sem=("arbitrary",), rs=rs)
    return (jnp.concatenate([da, db], axis=1), dw, dbdw, dlnw, dlnb), slots


def adamw(w, m, v, slots, rs=()):
    L, R, C = w.shape
    ns = slots[0].shape[0]
    fits = lambda r, c: ns * r * c * 2 <= ADAM_SLOT_BLOCK
    tiles = [(R, C)] if fits(R, C) else []
    tiles += [(d, C) for d in range(16, R, 16) if R % d == 0 and fits(d, C)]
    tiles += [(R, d) for d in range(LANES, C, LANES) if C % d == 0 and fits(R, d)]
    tr, tc = max(tiles, key=lambda t: t[0] * t[1])
    c1 = 1.0 / (1.0 - ADAM_B1 ** ADAM_STEP)
    c2 = 1.0 / (1.0 - ADAM_B2 ** ADAM_STEP)

    def body(w_ref, m_ref, v_ref, *rest):
        s_refs = rest[:L]
        g_ref, d_ref, nm_ref, nv_ref = rest[L:]
        l = pl.program_id(0)
        for k in range(L):
            @pl.when(l == k)
            def _(s_ref=s_refs[k]):
                g = s_ref[0].astype(F32)
                for j in range(1, ns):
                    g = g + s_ref[j].astype(F32)
                nm = ADAM_B1 * m_ref[0] + (1.0 - ADAM_B1) * g
                nv = ADAM_B2 * v_ref[0] + (1.0 - ADAM_B2) * (g * g)
                g_ref[0] = g
                nm_ref[0] = nm
                nv_ref[0] = nv
                d_ref[0] = -ADAM_LR * ((nm * c1) / (jnp.sqrt(nv * c2) + ADAM_EPS) + ADAM_WD * w_ref[0])

    nc = C // tc
    blk = pl.BlockSpec((1, tr, tc), lambda l, i: (l, i // nc, i % nc))
    slot = lambda k: pl.BlockSpec((ns, tr, tc), lambda l, i: (0, jnp.where(l == k, i // nc, 0),
                                                                 jnp.where(l == k, i % nc, 0)))
    outs, _, landed = _pcall(
        body, (w, m, v, *slots), name="adamw", grid=(L, (R // tr) * nc),
        in_specs=[blk, blk, blk] + [slot(k) for k in range(L)],
        out_specs=[blk, blk, blk, blk],
        out_shape=[jax.ShapeDtypeStruct((L, R, C), F32)] * 4,
        sem=("arbitrary", "arbitrary"), rs=rs)
    return outs, landed


def _unshard(g, axis):
    g = jnp.moveaxis(g, 0, axis)
    s = g.shape
    return g.reshape(s[:axis] + (s[axis] * s[axis + 1],) + s[axis + 2:])


def _to_blocks(full, axis):
    s = full.shape
    g = full.reshape(s[:axis] + (N_DEV, s[axis] // N_DEV) + s[axis + 1:])
    return jnp.moveaxis(g, axis, 0)


SMALL = (("norm_w", 2), ("dn_conv_w", 2), ("conv_b_pw1", 1), ("conv_w_dw", 2), ("conv_b_dw", 1),
         ("conv_ln_w", 1), ("conv_ln_b", 1), ("conv_b_pw2", 1),
         ("attn_sinks", None), ("dn_a_log", None), ("dn_dt_bias", None), ("dn_norm_w", None), ("final_norm_w", None))
SMALL_AXIS = dict(SMALL)


def _pack(parts):
    flat = jnp.concatenate([p.reshape(-1) for p in parts])
    pad = (-flat.shape[0]) % LANES
    return jnp.pad(flat, (0, pad))


def _unpack(flat, shapes):
    out, off = [], 0
    for s in shapes:
        n = int(np.prod(s))
        out.append(flat[off:off + n].reshape(s))
        off += n
    return out


def kernel(x, norm_w, ffn_w_gate, ffn_w_up, ffn_w_down, mix_w_in, dn_conv_w, attn_sinks, dn_a_log, dn_dt_bias, dn_norm_w, mix_w_out, conv_w_pw1, conv_b_pw1, conv_w_dw, conv_b_dw, conv_ln_w, conv_ln_b, conv_w_pw2, conv_b_pw2, final_norm_w, loss_target, m_norm_w, m_ffn_w_gate, m_ffn_w_up, m_ffn_w_down, m_mix_w_in, m_dn_conv_w, m_attn_sinks, m_dn_a_log, m_dn_dt_bias, m_dn_norm_w, m_mix_w_out, m_conv_w_pw1, m_conv_b_pw1, m_conv_w_dw, m_conv_b_dw, m_conv_ln_w, m_conv_ln_b, m_conv_w_pw2, m_conv_b_pw2, m_final_norm_w, v_norm_w, v_ffn_w_gate, v_ffn_w_up, v_ffn_w_down, v_mix_w_in, v_dn_conv_w, v_attn_sinks, v_dn_a_log, v_dn_dt_bias, v_dn_norm_w, v_mix_w_out, v_conv_w_pw1, v_conv_b_pw1, v_conv_w_dw, v_conv_b_dw, v_conv_ln_w, v_conv_ln_b, v_conv_w_pw2, v_conv_b_pw2, v_final_norm_w):
    W = dict(norm_w=norm_w, ffn_w_gate=ffn_w_gate, ffn_w_up=ffn_w_up, ffn_w_down=ffn_w_down, mix_w_in=mix_w_in,
             dn_conv_w=dn_conv_w, attn_sinks=attn_sinks, dn_a_log=dn_a_log, dn_dt_bias=dn_dt_bias,
             dn_norm_w=dn_norm_w, mix_w_out=mix_w_out, conv_w_pw1=conv_w_pw1, conv_b_pw1=conv_b_pw1,
             conv_w_dw=conv_w_dw, conv_b_dw=conv_b_dw, conv_ln_w=conv_ln_w, conv_ln_b=conv_ln_b,
             conv_w_pw2=conv_w_pw2, conv_b_pw2=conv_b_pw2, final_norm_w=final_norm_w)
    M = dict(norm_w=m_norm_w, ffn_w_gate=m_ffn_w_gate, ffn_w_up=m_ffn_w_up, ffn_w_down=m_ffn_w_down,
             mix_w_in=m_mix_w_in, dn_conv_w=m_dn_conv_w, attn_sinks=m_attn_sinks, dn_a_log=m_dn_a_log,
             dn_dt_bias=m_dn_dt_bias, dn_norm_w=m_dn_norm_w, mix_w_out=m_mix_w_out, conv_w_pw1=m_conv_w_pw1,
             conv_b_pw1=m_conv_b_pw1, conv_w_dw=m_conv_w_dw, conv_b_dw=m_conv_b_dw, conv_ln_w=m_conv_ln_w,
             conv_ln_b=m_conv_ln_b, conv_w_pw2=m_conv_w_pw2, conv_b_pw2=m_conv_b_pw2, final_norm_w=m_final_norm_w)
    V = dict(norm_w=v_norm_w, ffn_w_gate=v_ffn_w_gate, ffn_w_up=v_ffn_w_up, ffn_w_down=v_ffn_w_down,
             mix_w_in=v_mix_w_in, dn_conv_w=v_dn_conv_w, attn_sinks=v_attn_sinks, dn_a_log=v_dn_a_log,
             dn_dt_bias=v_dn_dt_bias, dn_norm_w=v_dn_norm_w, mix_w_out=v_mix_w_out, conv_w_pw1=v_conv_w_pw1,
             conv_b_pw1=v_conv_b_pw1, conv_w_dw=v_conv_w_dw, conv_b_dw=v_conv_b_dw, conv_ln_w=v_conv_ln_w,
             conv_ln_b=v_conv_ln_b, conv_w_pw2=v_conv_w_pw2, conv_b_pw2=v_conv_b_pw2, final_norm_w=v_final_norm_w)

    T, D = x.shape[1], x.shape[2]
    xs = x[0]

    big = ("ffn_w_gate", "ffn_w_up", "ffn_w_down", "mix_w_in", "mix_w_out", "conv_w_pw1", "conv_w_pw2")
    shard3 = {k: W[k].reshape((-1,) + W[k].shape[-2:]) for k in big}
    shard_bf = {k: shard3[k].astype(BF16) for k in big}
    ffn_unit = lambda i: [("ffn_w_gate", i), ("ffn_w_up", i), ("ffn_w_down", i)]
    even_unit = lambda e: [("mix_w_in", e), ("mix_w_out", e)]
    odd_unit = lambda e: [("conv_w_pw1", e), ("conv_w_pw2", e)]
    have = {}

    def ag_jobs(units):
        return [(shard_bf[k], i) for k, i in units]

    def ag_done(units, gathered):
        have.update(zip(units, gathered))

    small_sharded = [(k, ax) for k, ax in SMALL if ax is not None]
    small_pack = _pack([W[k] for k, _ in small_sharded])[None, :]
    first_units = ffn_unit(0)
    gathered = exchange(ag_jobs(first_units) + [(small_pack, None)])
    ag_done(first_units, gathered[:-1])
    small_full = {}
    for (k, ax), parts in zip(small_sharded,
                              zip(*[_unpack(gathered[-1][s, 0], [W[k].shape for k, _ in small_sharded])
                                    for s in range(N_DEV)])):
        small_full[k] = _unshard(jnp.stack(parts), ax)
    nw_full = small_full["norm_w"]

    ffn_w = lambda i: [have[u] for u in ffn_unit(i)]
    w_in_of = lambda e: jnp.pad(_unshard(have[("mix_w_in", e)], 1), ((0, 0), (0, IN_COLS_PAD - IN_COLS)))
    w_out_of = lambda e: have[("mix_w_out", e)].reshape(D, D)
    w_pw1_of = lambda e: _unshard(have[("conv_w_pw1", e)], 1)
    w_pw2_of = lambda e: have[("conv_w_pw2", e)].reshape(D, D)
    fwd_order, needed = [], {}
    for l in range(DEPTH):
        mixer = [("A", l), ("E", l)] if l % 2 == 0 else [("O", l)]
        fwd_order += [("F", 2 * l)] + mixer + [("F", 2 * l + 1)]
        needed[("F", 2 * l)], needed[("F", 2 * l + 1)] = ffn_unit(2 * l), ffn_unit(2 * l + 1)
        needed[mixer[0]] = even_unit(l // 2) if l % 2 == 0 else odd_unit(l // 2)
    queue = [(u, pos) for pos, key in enumerate(fwd_order) for u in needed.get(key, []) if u not in first_units]
    unit_bytes = lambda u: N_DEV * shard_bf[u[0]][u[1]].size * 2
    fwd_carry, at = {}, 0
    for pos, key in enumerate(fwd_order):
        cap = FWD_CARRY_BYTES[key[0]]
        taken, used = [], 0
        while at < len(queue) and (queue[at][1] <= pos + 1 or used + unit_bytes(queue[at][0]) <= cap):
            taken.append(queue[at][0])
            used += unit_bytes(queue[at][0])
            at += 1
        fwd_carry[key] = taken
    zero_in = jnp.zeros((1, IN_COLS_PAD), F32)
    zero_d = jnp.zeros((1, D), F32)
    slope_rows = jnp.asarray(np.repeat(2.0 ** (-8.0 * np.arange(1, ATTN_HEADS + 1) / ATTN_HEADS), ATTN_BLOCK)
                             .astype(np.float32)[:, None])

    saved = []
    h = xs
    w_in, w_out, w_pw1, w_pw2 = {}, {}, {}, {}

    def ffn_forward(h, l, half):
        i = 2 * l + half
        units = fwd_carry.get(("F", i), [])
        h, gathered = ffn_fwd(h, nw_full[l, 2 * half][None], *ffn_w(i), ag=ag_jobs(units))
        ag_done(units, gathered)
        return h

    for l in range(DEPTH):
        e = l // 2
        st = {"x0": h}
        h = ffn_forward(h, l, 0)
        st["x1"] = h
        if l % 2 == 0:
            w_in[e], w_out[e] = w_in_of(e), w_out_of(e)
            proj = rmslin_fwd(h, nw_full[l, 1][None], w_in[e], zero_in)
            st["proj"] = proj
            st["qkvc"] = dnconv_fwd(proj, small_full["dn_conv_w"][e])
            st["sink_rows"] = jnp.repeat(attn_sinks[e], ATTN_BLOCK)[:, None]
            st["alog"] = dn_a_log[e].reshape(DN_HEADS, 1, 1)
            st["dtb"] = dn_dt_bias[e].reshape(DN_HEADS, 1, 1)
            st["dnw"] = dn_norm_w[e].reshape(1, 1, DN_D)
            units = fwd_carry[("A", l)]
            st["att"], gathered = attn_fwd(proj, st["sink_rows"], slope_rows, ag=ag_jobs(units))
            ag_done(units, gathered)
            units = fwd_carry[("E", l)]
            (st["og"], st["sall"]), gathered = dn_fwd(st["qkvc"], proj, st["alog"], st["dtb"], st["dnw"],
                                                      ag=ag_jobs(units))
            ag_done(units, gathered)
            h = lin_fwd(h, [st["att"], st["og"]], w_out[e], zero_d)
        else:
            units = fwd_carry[("O", l)]
            w_pw1[e], w_pw2[e] = w_pw1_of(e), w_pw2_of(e)
            st["ab"] = rmslin_fwd(h, nw_full[l, 1][None], w_pw1[e], small_full["conv_b_pw1"][e][None])
            (st["act"], st["cv"]), gathered = cv_fwd(st["ab"], small_full["conv_w_dw"][e], small_full["conv_b_dw"][e][None],
                                         small_full["conv_ln_w"][e][None], small_full["conv_ln_b"][e][None],
                                         ag=ag_jobs(units))
            ag_done(units, gathered)
            h = lin_fwd(h, [st["act"]], w_pw2[e], small_full["conv_b_pw2"][e][None])
        st["x2"] = h
        h = ffn_forward(h, l, 1)
        saved.append(st)

    loss_part, dh, dfinal = loss_fwd_bwd(h, final_norm_w[None], loss_target[0])
    loss = lax.psum(loss_part[0, 0], ("x", "y", "c"))

    d_norm = [[None] * 3 for _ in range(DEPTH)]
    d_small = {k: [None, None] for k in ("dn_conv_w", "conv_b_pw1", "conv_w_dw", "conv_b_dw", "conv_ln_w",
                                         "conv_ln_b", "conv_b_pw2", "attn_sinks", "dn_a_log", "dn_dt_bias",
                                         "dn_norm_w")}
    pending, slot = [], {}

    def take_jobs(cap=None, only=None):
        taken = [p for p in pending if p[1] == "swap"]
        used = 0
        for p in pending:
            if p[1] == "swap" or (only is not None and p[0][0] not in only):
                continue
            if cap is not None and used + p[2].size * 2 > cap:
                break
            taken.append(p)
            used += p[2].size * 2
        pending[:] = [p for p in pending if all(p is not t for t in taken)]
        return taken, [(kind, arr) for _, kind, arr in taken]

    def land(taken, results):
        swapped = [(unit, arr, res) for (unit, kind, arr), res in zip(taken, results) if kind == "swap"]
        slot.update({unit: res for (unit, kind, _), res in zip(taken, results) if kind != "swap"})
        if swapped:
            sums = pair_add([g for _, g, _ in swapped], [r for _, _, r in swapped])
            pending.extend((unit, "chips", h) for (unit, _, _), h in zip(swapped, sums))

    def ffn_backward(dh, l, half):
        i = 2 * l + half
        taken, jobs = take_jobs(BWD_CARRY_BYTES["F"])
        (dh, dg, du, dd, d_norm[l][2 * half]), results = ffn_bwd(
            st["x2" if half else "x0"], dh, nw_full[l, 2 * half][None], *ffn_w(i), rs=jobs)
        land(taken, results)
        pending.extend((u, "swap", g) for u, g in zip(ffn_unit(i), (dg, du, dd)))
        return dh

    for l in reversed(range(DEPTH)):
        e = l // 2
        st = saved[l]
        dh = ffn_backward(dh, l, 1)
        if l % 2 == 0:
            dmix, d_out, _ = lin_bwd([st["att"], st["og"]], dh, w_out[e])
            pending.append((("mix_w_out", e), "direct", d_out.reshape(N_DEV, D // N_DEV, D)))
            taken, jobs = take_jobs(BWD_CARRY_BYTES["E"])
            (dqkvc, dzba, dalog, ddtb, ddnw), results = dn_bwd(
                st["qkvc"], st["proj"], st["alog"], st["dtb"], st["dnw"], st["sall"], dmix, rs=jobs)
            land(taken, results)
            taken, jobs = take_jobs(BWD_CARRY_BYTES["A"])
            (dqa, dkva, dsink), results = attn_bwd(st["proj"], st["sink_rows"], slope_rows, dmix, rs=jobs)
            land(taken, results)
            dqkv, d_small["dn_conv_w"][e] = dnconv_bwd(st["proj"], small_full["dn_conv_w"][e], dqkvc)
            dh, d_in, _, d_norm[l][1] = rmslin_bwd(st["x1"], dh, [dqa, dkva, dqkv, dzba], nw_full[l, 1][None],
                                                   w_in[e])
            pending.append((("mix_w_in", e), "direct", _to_blocks(d_in[:, :IN_COLS], 1)))
            d_small["attn_sinks"][e] = jnp.sum(dsink.reshape(ATTN_HEADS, ATTN_BLOCK), axis=1)
            d_small["dn_a_log"][e] = dalog.reshape(DN_HEADS)
            d_small["dn_dt_bias"][e] = ddtb.reshape(DN_HEADS)
            d_small["dn_norm_w"][e] = ddnw.reshape(DN_D)
        else:
            dact, d_pw2, d_small["conv_b_pw2"][e] = lin_bwd([st["act"]], dh, w_pw2[e])
            pending.append((("conv_w_pw2", e), "direct", d_pw2.reshape(N_DEV, D // N_DEV, D)))
            taken, jobs = take_jobs(BWD_CARRY_BYTES["O"])
            (dab, d_small["conv_w_dw"][e], d_small["conv_b_dw"][e], d_small["conv_ln_w"][e],
             d_small["conv_ln_b"][e]), results = cv_bwd(
                st["ab"], st["cv"], small_full["conv_w_dw"][e],
                small_full["conv_ln_w"][e][None], small_full["conv_ln_b"][e][None], dact, rs=jobs)
            land(taken, results)
            dh, d_pw1, d_small["conv_b_pw1"][e], d_norm[l][1] = rmslin_bwd(
                st["x1"], dh, dab, nw_full[l, 1][None], w_pw1[e])
            pending.append((("conv_w_pw1", e), "direct", _to_blocks(d_pw1, 1)))
        dh = ffn_backward(dh, l, 0)
    grad_x = dh[None]

    full_small = {"norm_w": jnp.stack([jnp.concatenate(r, axis=0) for r in d_norm]),
                  "final_norm_w": dfinal[0]}
    for k, pair in d_small.items():
        full_small[k] = jnp.stack([p.reshape(W[k].shape[1:-1] + (-1,)) if SMALL_AXIS[k] is not None
                                   else p for p in pair])
    rows = []
    for s in range(N_DEV):
        parts = [_to_blocks(full_small[k], ax)[s] if ax is not None else full_small[k] for k, ax in SMALL]
        rows.append(_pack(parts))
    send_small = jnp.stack(rows)[:, None, :]
    pending.append((("small", 0), "direct", send_small))

    res = {}
    waiting = lambda k: [p for p in pending if p[0][0] == k]
    adam_order = sorted(big, key=lambda k: len(waiting(k))) + ["small"]
    for n, k in enumerate(adam_order[:-1]):
        nxt = next((kk for kk in adam_order[n + 1:] if waiting(kk)), "small")
        taken, jobs = take_jobs(only=(nxt, "small") if n == 1 else (nxt,))
        turned = k in ("ffn_w_gate", "ffn_w_up")
        view = lambda a: jnp.swapaxes(a.reshape(shard3[k].shape), 1, 2) if turned else a.reshape(shard3[k].shape)
        outs, results = adamw(view(W[k]), view(M[k]), view(V[k]),
                              [slot[(k, i)] for i in range(shard3[k].shape[0])], rs=jobs)
        land(taken, results)
        res[k] = [(jnp.swapaxes(o, 1, 2) if turned else o).reshape(W[k].shape) for o in outs]
    pk = lambda d: _pack([d[k] for k, _ in SMALL])[None, None, :]
    outs, _ = adamw(pk(W), pk(M), pk(V), [slot[("small", 0)]])
    shapes = [W[k].shape for k, _ in SMALL]
    unp = [_unpack(o[0, 0], shapes) for o in outs]
    for i, (k, _) in enumerate(SMALL):
        res[k] = [u[i] for u in unp]

    order = ("norm_w", "ffn_w_gate", "ffn_w_up", "ffn_w_down", "mix_w_in", "dn_conv_w", "attn_sinks", "dn_a_log",
             "dn_dt_bias", "dn_norm_w", "mix_w_out", "conv_w_pw1", "conv_b_pw1", "conv_w_dw", "conv_b_dw",
             "conv_ln_w", "conv_ln_b", "conv_w_pw2", "conv_b_pw2", "final_norm_w")
    return (loss, grad_x, *[res[k][0] for k in order], *[res[k][1] for k in order],
            *[res[k][2] for k in order], *[res[k][3] for k in order])
```
